```python
import math
import jax, jax.numpy as jnp
from jax import lax
import numpy as np

D_MODEL = 1024
BATCH = 8
SEQ = 8192
DEPTH = 2

GRID_W = 64
ROPE_BASE = 10000.0
EPS = 1e-6
Q_BLOCK = 128

N_GROUPS = 4
GROUP_W = D_MODEL // N_GROUPS
MIX_W = N_GROUPS * GROUP_W

A_HEADS = 4
A_NOPE = 64
A_ROPE = 32
A_VDIM = GROUP_W // A_HEADS
A_Q_LORA = 192
A_KV_LORA = 128
A_COLS = A_Q_LORA + A_KV_LORA + A_ROPE

B_HEADS = 4
B_KV_HEADS = 2
B_HDIM = GROUP_W // B_HEADS
B_COLS = (B_HEADS + 2 * B_KV_HEADS) * B_HDIM

C_HEADS = 4
C_HDIM = GROUP_W // C_HEADS
C_NGROUPS = 2
C_STATE = 64
C_CHUNK = 128
C_XBC = GROUP_W + 2 * C_NGROUPS * C_STATE
C_COLS = GROUP_W + C_XBC + 2 * C_HEADS

D_HEADS = 4
D_HDIM = GROUP_W // D_HEADS
D_CHUNK = 64
D_QKV = 3 * GROUP_W
D_COLS = D_QKV + GROUP_W + 4 * D_HEADS

IN_COLS = A_COLS + B_COLS + C_COLS + D_COLS
CONV_W = 3

D_FF = 2816
FFN_CONV_W = 3

kernel_name = "bidir_hybrid_parallel_heads_mla_gqa_ssd_deltanet"


def split_last(t, sizes):
    return jnp.split(t, [int(s) for s in np.cumsum(sizes)[:-1]], axis=-1)


def rms_norm(x, w, eps=EPS):
    x32 = x.astype(jnp.float32)
    y = x32 * lax.rsqrt(jnp.mean(x32 * x32, axis=-1, keepdims=True) + eps)
    return (y * w.astype(jnp.float32)).astype(x.dtype)


def l2_normalize(x, eps=1e-6):
    x32 = x.astype(jnp.float32)
    return (x32 * lax.rsqrt(jnp.sum(x32 * x32, axis=-1, keepdims=True) + eps)).astype(x.dtype)


def dwconv_centred(x, w, b=None):
    K = w.shape[0]
    L = x.shape[1]
    pad = K // 2
    xp = jnp.pad(x, ((0, 0), (pad, pad), (0, 0)))
    y = sum(xp[:, k:k + L] * w[k] for k in range(K))
    return y if b is None else y + b


def axial_rope_tables(seq_len, rot_dim):
    rows = seq_len // GRID_W
    row = jnp.repeat(jnp.arange(rows), GRID_W).astype(jnp.float32)
    col = jnp.tile(jnp.arange(GRID_W), rows).astype(jnp.float32)
    sec = rot_dim // 2
    inv_freq = ROPE_BASE ** (-jnp.arange(0, sec, 2, dtype=jnp.float32) / sec)
    ang_r = row[:, None] * inv_freq
    ang_c = col[:, None] * inv_freq
    ang = jnp.concatenate([ang_r, ang_r, ang_c, ang_c], axis=-1)
    return jnp.cos(ang), jnp.sin(ang)


def apply_axial_rope(x, cos, sin):
    r = x.shape[-1]
    xs = x.reshape(x.shape[:-1] + (2, 2, r // 4))
    rot = jnp.stack([-xs[..., 1, :], xs[..., 0, :]], axis=-2).reshape(x.shape)
    return (x * cos + rot * sin).astype(x.dtype)


def mla_attention(q_nope, q_rope, k_nope, k_rope, v):
    Bsz, L, H, dn = q_nope.shape
    dr = q_rope.shape[-1]
    nb = L // Q_BLOCK
    scale = (dn + dr) ** -0.5
    qn_b = q_nope.reshape(Bsz, nb, Q_BLOCK, H, dn).swapaxes(0, 1)
    qr_b = q_rope.reshape(Bsz, nb, Q_BLOCK, H, dr).swapaxes(0, 1)

    def block(qs):
        qn, qr = qs
        s = jnp.einsum('bqhd,bkhd->bhqk', qn, k_nope) + jnp.einsum('bqhr,bkr->bhqk', qr, k_rope)
        p = jax.nn.softmax(s.astype(jnp.float32) * scale, axis=-1).astype(v.dtype)
        return jnp.einsum('bhqk,bkhd->bqhd', p, v)

    o = lax.map(block, (qn_b, qr_b))
    return o.swapaxes(0, 1).reshape(Bsz, L, H * v.shape[-1])


def gqa_attention(q, k, v):
    Bsz, L, Hq, hd = q.shape
    Hkv = k.shape[2]
    rep = Hq // Hkv
    nb = L // Q_BLOCK
    qb = q.reshape(Bsz, nb, Q_BLOCK, Hkv, rep, hd).swapaxes(0, 1)
    scale = hd ** -0.5

    def block(qi):
        s = jnp.einsum('bqgrd,bkgd->bgrqk', qi, k)
        p = jax.nn.softmax(s.astype(jnp.float32) * scale, axis=-1).astype(v.dtype)
        return jnp.einsum('bgrqk,bkgd->bqgrd', p, v)

    o = lax.map(block, qb)
    return o.swapaxes(0, 1).reshape(Bsz, L, Hq * hd)


def ssd_scan(x, dt, A, Bm, Cm):
    Bsz, L, H, P = x.shape
    N = Bm.shape[-1]
    Q = C_CHUNK
    nc = L // Q
    xc = (x * dt[..., None]).reshape(Bsz, nc, Q, H, P)
    Bc = Bm.reshape(Bsz, nc, Q, H, N)
    Cc = Cm.reshape(Bsz, nc, Q, H, N)
    acum = jnp.cumsum((dt * A).reshape(Bsz, nc, Q, H).transpose(0, 1, 3, 2), axis=-1)
    idx = jnp.arange(Q)
    causal = idx[:, None] >= idx[None, :]
    seg = acum[..., :, None] - acum[..., None, :]
    decay = jnp.exp(jnp.where(causal, seg, -jnp.inf))
    scores = jnp.einsum('bcihn,bcjhn->bchij', Cc, Bc) * decay
    y_diag = jnp.einsum('bchij,bcjhp->bcihp', scores, xc)
    decay_to_end = jnp.exp(acum[..., -1:] - acum)
    states = jnp.einsum('bchj,bcjhn,bcjhp->bchpn', decay_to_end, Bc, xc)
    chunk_decay = jnp.exp(acum[..., -1])

    def step(S, inp):
        st, dec = inp
        return S * dec[..., None, None] + st, S

    S0 = jnp.zeros((Bsz, H, P, N), x.dtype)
    _, S_in = lax.scan(step, S0, (states.swapaxes(0, 1), chunk_decay.swapaxes(0, 1)))
    S_in = S_in.swapaxes(0, 1)
    y_off = jnp.einsum('bcihn,bchpn,bchi->bcihp', Cc, S_in, jnp.exp(acum))
    return (y_diag + y_off).reshape(Bsz, L, H, P)


def gated_delta_rule(q, k, v, g, beta):
    Bsz, L, H, dk = q.shape
    dv = v.shape[-1]
    Q = D_CHUNK
    nc = L // Q

    def chunks(t):
        return t.reshape(Bsz, nc, Q, H, -1).transpose(0, 1, 3, 2, 4)

    qc = chunks(q) * dk ** -0.5
    kc = chunks(k)
    vc = chunks(v)
    bc = beta.reshape(Bsz, nc, Q, H).transpose(0, 1, 3, 2)
    G = jnp.cumsum(g.reshape(Bsz, nc, Q, H).transpose(0, 1, 3, 2), axis=-1)
    idx = jnp.arange(Q)
    lower_strict = idx[:, None] > idx[None, :]
    lower_incl = idx[:, None] >= idx[None, :]
    seg = G[..., :, None] - G[..., None, :]
    decay = jnp.exp(jnp.where(lower_incl, seg, -jnp.inf))
    kb = kc * bc[..., None]
    Lmat = jnp.where(lower_strict, jnp.einsum('bchid,bchjd->bchij', kb, kc) * decay, 0.0)
    eye = jnp.eye(Q, dtype=Lmat.dtype)
    T = lax.linalg.triangular_solve(Lmat + eye, jnp.broadcast_to(eye, Lmat.shape),
                                    left_side=True, lower=True, unit_diagonal=True)
    u_base = jnp.einsum('bchij,bchjd->bchid', T, vc * bc[..., None])
    w = jnp.einsum('bchij,bchjd->bchid', T, kb * jnp.exp(G)[..., None])
    qk = jnp.einsum('bchid,bchjd->bchij', qc, kc) * decay
    q_dec = qc * jnp.exp(G)[..., None]
    k_dec = kc * jnp.exp(G[..., -1:] - G)[..., None]
    g_end = jnp.exp(G[..., -1])

    def step(S, inp):
        u_c, w_c, qk_c, qd_c, kd_c, ge_c = inp
        v_new = u_c - jnp.einsum('bhid,bhdv->bhiv', w_c, S)
        o = jnp.einsum('bhid,bhdv->bhiv', qd_c, S) + jnp.einsum('bhij,bhjv->bhiv', qk_c, v_new)
        S = S * ge_c[..., None, None] + jnp.einsum('bhjd,bhjv->bhdv', kd_c, v_new)
        return S, o

    xs = tuple(t.swapaxes(0, 1) for t in (u_base, w, qk, q_dec, k_dec, g_end))
    S0 = jnp.zeros((Bsz, H, dk, dv), q.dtype)
    _, o = lax.scan(step, S0, xs)
    return o.transpose(1, 0, 3, 2, 4).reshape(Bsz, L, H, dv)


def mla_mixer(p, q_norm, w_uq, kv_norm, w_ukv, out_norm, cos, sin):
    Bsz, L, _ = p.shape
    cq, ckv, kr = split_last(p, [A_Q_LORA, A_KV_LORA, A_ROPE])
    q = (rms_norm(cq, q_norm) @ w_uq).reshape(Bsz, L, A_HEADS, A_NOPE + A_ROPE)
    kv = (rms_norm(ckv, kv_norm) @ w_ukv).reshape(Bsz, L, A_HEADS, A_NOPE + A_VDIM)
    q_nope, q_rope = q[..., :A_NOPE], q[..., A_NOPE:]
    k_nope, v = kv[..., :A_NOPE], kv[..., A_NOPE:]
    q_rope = apply_axial_rope(q_rope, cos[:, None, :], sin[:, None, :])
    k_rope = apply_axial_rope(kr, cos, sin)
    o = mla_attention(q_nope, q_rope, k_nope, k_rope, v)
    return rms_norm(o, out_norm)


def gqa_mixer(p, q_norm, k_norm, out_norm, cos, sin):
    Bsz, L, _ = p.shape
    q, k, v = split_last(p, [B_HEADS * B_HDIM, B_KV_HEADS * B_HDIM, B_KV_HEADS * B_HDIM])
    q = rms_norm(q.reshape(Bsz, L, B_HEADS, B_HDIM), q_norm)
    k = rms_norm(k.reshape(Bsz, L, B_KV_HEADS, B_HDIM), k_norm)
    v = v.reshape(Bsz, L, B_KV_HEADS, B_HDIM)
    q = apply_axial_rope(q, cos[:, None, :], sin[:, None, :])
    k = apply_axial_rope(k, cos[:, None, :], sin[:, None, :])
    return rms_norm(gqa_attention(q, k, v), out_norm)


def mamba2_mixer(p, conv_w, conv_b, a_log, dt_bias, d_skip, out_norm):
    Bsz, L, _ = p.shape
    f32 = jnp.float32
    z, xbc, dt_raw = split_last(p, [GROUP_W, C_XBC, 2 * C_HEADS])
    xbc = jax.nn.silu(dwconv_centred(xbc, conv_w, conv_b))
    xs, Bm, Cm = split_last(xbc, [GROUP_W, C_NGROUPS * C_STATE, C_NGROUPS * C_STATE])
    rep = C_HEADS // C_NGROUPS
    xs = xs.reshape(Bsz, L, C_HEADS, C_HDIM).astype(f32)
    Bm = jnp.repeat(Bm.reshape(Bsz, L, C_NGROUPS, C_STATE), rep, axis=2).astype(f32)
    Cm = jnp.repeat(Cm.reshape(Bsz, L, C_NGROUPS, C_STATE), rep, axis=2).astype(f32)
    dt = jax.nn.softplus((dt_raw.reshape(Bsz, L, 2, C_HEADS) + dt_bias).astype(f32))
    A = -jnp.exp(a_log.astype(f32))
    fl = lambda t: jnp.flip(t, axis=1)
    y_f = ssd_scan(xs, dt[:, :, 0], A[0], Bm, Cm)
    y_b = fl(ssd_scan(fl(xs), fl(dt[:, :, 1]), A[1], fl(Bm), fl(Cm)))
    y = y_f + y_b + xs * d_skip.astype(f32)[:, None]
    y = y.reshape(Bsz, L, GROUP_W).astype(p.dtype)
    return rms_norm(y * jax.nn.silu(z), out_norm)


def deltanet_mixer(p, conv_w, a_log, dt_bias, out_norm):
    Bsz, L, _ = p.shape
    f32 = jnp.float32
    qkv, z, ab = split_last(p, [D_QKV, GROUP_W, 4 * D_HEADS])
    qkv = jax.nn.silu(dwconv_centred(qkv, conv_w))
    q, k, v = split_last(qkv, [GROUP_W, GROUP_W, GROUP_W])
    q = l2_normalize(q.reshape(Bsz, L, D_HEADS, D_HDIM)).astype(f32)
    k = l2_normalize(k.reshape(Bsz, L, D_HEADS, D_HDIM)).astype(f32)
    v = v.reshape(Bsz, L, D_HEADS, D_HDIM).astype(f32)
    ab = ab.reshape(Bsz, L, 4, D_HEADS).astype(f32)
    beta = jax.nn.sigmoid(ab[:, :, 0:2])
    g = -jnp.exp(a_log.astype(f32)) * jax.nn.softplus(ab[:, :, 2:4] + dt_bias.astype(f32))
    fl = lambda t: jnp.flip(t, axis=1)
    o_f = gated_delta_rule(q, k, v, g[:, :, 0], beta[:, :, 0])
    o_b = fl(gated_delta_rule(fl(q), fl(k), fl(v), fl(g[:, :, 1]), fl(beta[:, :, 1])))
    o = rms_norm((o_f + o_b).astype(p.dtype), out_norm)
    o = o * jax.nn.silu(z.reshape(Bsz, L, D_HEADS, D_HDIM))
    return o.reshape(Bsz, L, GROUP_W)


def conv_ffn(h, w_in, conv_w, conv_b, w_out):
    gu = dwconv_centred(h @ w_in, conv_w, conv_b)
    gate, up = split_last(gu, [D_FF, D_FF])
    return (jax.nn.silu(gate) * up) @ w_out


def _fwd_setup_inputs(seed: int = 0) -> dict:
    key = jax.random.key(seed)
    ks = iter(jax.random.split(key, 40))
    nrm = lambda shape, scale: jax.random.normal(next(ks), shape, jnp.float32) * scale
    gain = lambda shape: 1.0 + nrm(shape, 0.05)
    log_a = lambda h: jnp.log(jax.random.uniform(next(ks), (DEPTH, 2, h), jnp.float32, 1.0, 16.0))

    def dt_bias(h):
        dt = jnp.exp(jax.random.uniform(next(ks), (DEPTH, 2, h), jnp.float32, math.log(1e-3), math.log(1e-1)))
        return dt + jnp.log(-jnp.expm1(-dt))

    return {
        "x": nrm((BATCH, SEQ, D_MODEL), 1.0),
        "pre_mix_norm": gain((DEPTH, D_MODEL)),
        "w_in": nrm((DEPTH, D_MODEL, IN_COLS), D_MODEL ** -0.5),
        "a_q_norm": gain((DEPTH, A_Q_LORA)),
        "a_w_uq": nrm((DEPTH, A_Q_LORA, A_HEADS * (A_NOPE + A_ROPE)), A_Q_LORA ** -0.5),
        "a_kv_norm": gain((DEPTH, A_KV_LORA)),
        "a_w_ukv": nrm((DEPTH, A_KV_LORA, A_HEADS * (A_NOPE + A_VDIM)), A_KV_LORA ** -0.5),
        "a_out_norm": gain((DEPTH, GROUP_W)),
        "b_q_norm": gain((DEPTH, B_HDIM)),
        "b_k_norm": gain((DEPTH, B_HDIM)),
        "b_out_norm": gain((DEPTH, GROUP_W)),
        "c_conv_w": nrm((DEPTH, CONV_W, C_XBC), CONV_W ** -0.5),
        "c_conv_b": nrm((DEPTH, C_XBC), 0.02),
        "c_a_log": log_a(C_HEADS),
        "c_dt_bias": dt_bias(C_HEADS),
        "c_d_skip": 1.0 + nrm((DEPTH, C_HEADS), 0.1),
        "c_out_norm": gain((DEPTH, GROUP_W)),
        "d_conv_w": nrm((DEPTH, CONV_W, D_QKV), CONV_W ** -0.5),
        "d_a_log": log_a(D_HEADS),
        "d_dt_bias": dt_bias(D_HEADS),
        "d_out_norm": gain((DEPTH, D_HDIM)),
        "w_out": nrm((DEPTH, MIX_W, D_MODEL), MIX_W ** -0.5),
        "post_mix_norm": gain((DEPTH, D_MODEL)),
        "pre_ffn_norm": gain((DEPTH, D_MODEL)),
        "f_w_in": nrm((DEPTH, D_MODEL, 2 * D_FF), D_MODEL ** -0.5),
        "f_conv_w": nrm((DEPTH, FFN_CONV_W, 2 * D_FF), FFN_CONV_W ** -0.5),
        "f_conv_b": nrm((DEPTH, 2 * D_FF), 0.02),
        "f_w_out": nrm((DEPTH, D_FF, D_MODEL), D_FF ** -0.5),
        "post_ffn_norm": gain((DEPTH, D_MODEL)),
    }


def _fwd_reference(x, pre_mix_norm, w_in, a_q_norm, a_w_uq, a_kv_norm, a_w_ukv, a_out_norm,
              b_q_norm, b_k_norm, b_out_norm, c_conv_w, c_conv_b, c_a_log, c_dt_bias, c_d_skip,
              c_out_norm, d_conv_w, d_a_log, d_dt_bias, d_out_norm, w_out, post_mix_norm,
              pre_ffn_norm, f_w_in, f_conv_w, f_conv_b, f_w_out, post_ffn_norm):
    L = x.shape[1]
    cos_a, sin_a = axial_rope_tables(L, A_ROPE)
    cos_b, sin_b = axial_rope_tables(L, B_HDIM)
    for l in range(DEPTH):
        h = rms_norm(x, pre_mix_norm[l])
        p = h @ w_in[l]
        pa, pb, pc, pd = split_last(p, [A_COLS, B_COLS, C_COLS, D_COLS])
        o_a = mla_mixer(pa, a_q_norm[l], a_w_uq[l], a_kv_norm[l], a_w_ukv[l], a_out_norm[l], cos_a, sin_a)
        o_b = gqa_mixer(pb, b_q_norm[l], b_k_norm[l], b_out_norm[l], cos_b, sin_b)
        o_c = mamba2_mixer(pc, c_conv_w[l], c_conv_b[l], c_a_log[l], c_dt_bias[l], c_d_skip[l], c_out_norm[l])
        o_d = deltanet_mixer(pd, d_conv_w[l], d_a_log[l], d_dt_bias[l], d_out_norm[l])
        o = jnp.concatenate([o_a, o_b, o_c, o_d], axis=-1)
        x = x + rms_norm(o @ w_out[l], post_mix_norm[l])
        h = rms_norm(x, pre_ffn_norm[l])
        x = x + rms_norm(conv_ffn(h, f_w_in[l], f_conv_w[l], f_conv_b[l], f_w_out[l]), post_ffn_norm[l])
    return x


import jax as _jax
import jax.numpy as _jnp

TWIN_FORMAT = 'train_step'
FWD_PARAMS = ['x', 'pre_mix_norm', 'w_in', 'a_q_norm', 'a_w_uq', 'a_kv_norm', 'a_w_ukv', 'a_out_norm', 'b_q_norm', 'b_k_norm', 'b_out_norm', 'c_conv_w', 'c_conv_b', 'c_a_log', 'c_dt_bias', 'c_d_skip', 'c_out_norm', 'd_conv_w', 'd_a_log', 'd_dt_bias', 'd_out_norm', 'w_out', 'post_mix_norm', 'pre_ffn_norm', 'f_w_in', 'f_conv_w', 'f_conv_b', 'f_w_out', 'post_ffn_norm']
TWIN_WEIGHTS = ['pre_mix_norm', 'w_in', 'a_q_norm', 'a_w_uq', 'a_kv_norm', 'a_w_ukv', 'a_out_norm', 'b_q_norm', 'b_k_norm', 'b_out_norm', 'c_conv_w', 'c_conv_b', 'c_a_log', 'c_dt_bias', 'c_d_skip', 'c_out_norm', 'd_conv_w', 'd_a_log', 'd_dt_bias', 'd_out_norm', 'w_out', 'post_mix_norm', 'pre_ffn_norm', 'f_w_in', 'f_conv_w', 'f_conv_b', 'f_w_out', 'post_ffn_norm']
TWIN_DIFF_INPUT = 'x'
TWIN_INPUTS = ['x', 'pre_mix_norm', 'w_in', 'a_q_norm', 'a_w_uq', 'a_kv_norm', 'a_w_ukv', 'a_out_norm', 'b_q_norm', 'b_k_norm', 'b_out_norm', 'c_conv_w', 'c_conv_b', 'c_a_log', 'c_dt_bias', 'c_d_skip', 'c_out_norm', 'd_conv_w', 'd_a_log', 'd_dt_bias', 'd_out_norm', 'w_out', 'post_mix_norm', 'pre_ffn_norm', 'f_w_in', 'f_conv_w', 'f_conv_b', 'f_w_out', 'post_ffn_norm', 'loss_target', 'm_pre_mix_norm', 'm_w_in', 'm_a_q_norm', 'm_a_w_uq', 'm_a_kv_norm', 'm_a_w_ukv', 'm_a_out_norm', 'm_b_q_norm', 'm_b_k_norm', 'm_b_out_norm', 'm_c_conv_w', 'm_c_conv_b', 'm_c_a_log', 'm_c_dt_bias', 'm_c_d_skip', 'm_c_out_norm', 'm_d_conv_w', 'm_d_a_log', 'm_d_dt_bias', 'm_d_out_norm', 'm_w_out', 'm_post_mix_norm', 'm_pre_ffn_norm', 'm_f_w_in', 'm_f_conv_w', 'm_f_conv_b', 'm_f_w_out', 'm_post_ffn_norm', 'v_pre_mix_norm', 'v_w_in', 'v_a_q_norm', 'v_a_w_uq', 'v_a_kv_norm', 'v_a_w_ukv', 'v_a_out_norm', 'v_b_q_norm', 'v_b_k_norm', 'v_b_out_norm', 'v_c_conv_w', 'v_c_conv_b', 'v_c_a_log', 'v_c_dt_bias', 'v_c_d_skip', 'v_c_out_norm', 'v_d_conv_w', 'v_d_a_log', 'v_d_dt_bias', 'v_d_out_norm', 'v_w_out', 'v_post_mix_norm', 'v_pre_ffn_norm', 'v_f_w_in', 'v_f_conv_w', 'v_f_conv_b', 'v_f_w_out', 'v_post_ffn_norm']
TWIN_OUTPUTS = ['loss', 'grad_x', 'grad_pre_mix_norm', 'grad_w_in', 'grad_a_q_norm', 'grad_a_w_uq', 'grad_a_kv_norm', 'grad_a_w_ukv', 'grad_a_out_norm', 'grad_b_q_norm', 'grad_b_k_norm', 'grad_b_out_norm', 'grad_c_conv_w', 'grad_c_conv_b', 'grad_c_a_log', 'grad_c_dt_bias', 'grad_c_d_skip', 'grad_c_out_norm', 'grad_d_conv_w', 'grad_d_a_log', 'grad_d_dt_bias', 'grad_d_out_norm', 'grad_w_out', 'grad_post_mix_norm', 'grad_pre_ffn_norm', 'grad_f_w_in', 'grad_f_conv_w', 'grad_f_conv_b', 'grad_f_w_out', 'grad_post_ffn_norm', 'delta_pre_mix_norm', 'delta_w_in', 'delta_a_q_norm', 'delta_a_w_uq', 'delta_a_kv_norm', 'delta_a_w_ukv', 'delta_a_out_norm', 'delta_b_q_norm', 'delta_b_k_norm', 'delta_b_out_norm', 'delta_c_conv_w', 'delta_c_conv_b', 'delta_c_a_log', 'delta_c_dt_bias', 'delta_c_d_skip', 'delta_c_out_norm', 'delta_d_conv_w', 'delta_d_a_log', 'delta_d_dt_bias', 'delta_d_out_norm', 'delta_w_out', 'delta_post_mix_norm', 'delta_pre_ffn_norm', 'delta_f_w_in', 'delta_f_conv_w', 'delta_f_conv_b', 'delta_f_w_out', 'delta_post_ffn_norm', 'new_m_pre_mix_norm', 'new_m_w_in', 'new_m_a_q_norm', 'new_m_a_w_uq', 'new_m_a_kv_norm', 'new_m_a_w_ukv', 'new_m_a_out_norm', 'new_m_b_q_norm', 'new_m_b_k_norm', 'new_m_b_out_norm', 'new_m_c_conv_w', 'new_m_c_conv_b', 'new_m_c_a_log', 'new_m_c_dt_bias', 'new_m_c_d_skip', 'new_m_c_out_norm', 'new_m_d_conv_w', 'new_m_d_a_log', 'new_m_d_dt_bias', 'new_m_d_out_norm', 'new_m_w_out', 'new_m_post_mix_norm', 'new_m_pre_ffn_norm', 'new_m_f_w_in', 'new_m_f_conv_w', 'new_m_f_conv_b', 'new_m_f_w_out', 'new_m_post_ffn_norm', 'new_v_pre_mix_norm', 'new_v_w_in', 'new_v_a_q_norm', 'new_v_a_w_uq', 'new_v_a_kv_norm', 'new_v_a_w_ukv', 'new_v_a_out_norm', 'new_v_b_q_norm', 'new_v_b_k_norm', 'new_v_b_out_norm', 'new_v_c_conv_w', 'new_v_c_conv_b', 'new_v_c_a_log', 'new_v_c_dt_bias', 'new_v_c_d_skip', 'new_v_c_out_norm', 'new_v_d_conv_w', 'new_v_d_a_log', 'new_v_d_dt_bias', 'new_v_d_out_norm', 'new_v_w_out', 'new_v_post_mix_norm', 'new_v_pre_ffn_norm', 'new_v_f_w_in', 'new_v_f_conv_w', 'new_v_f_conv_b', 'new_v_f_w_out', 'new_v_post_ffn_norm']
TWIN_LEAF_KINDS = {'loss': 'loss', 'grad_x': 'grad_x', 'grad_pre_mix_norm': 'grad_w', 'grad_w_in': 'grad_w', 'grad_a_q_norm': 'grad_w', 'grad_a_w_uq': 'grad_w', 'grad_a_kv_norm': 'grad_w', 'grad_a_w_ukv': 'grad_w', 'grad_a_out_norm': 'grad_w', 'grad_b_q_norm': 'grad_w', 'grad_b_k_norm': 'grad_w', 'grad_b_out_norm': 'grad_w', 'grad_c_conv_w': 'grad_w', 'grad_c_conv_b': 'grad_w', 'grad_c_a_log': 'grad_w', 'grad_c_dt_bias': 'grad_w', 'grad_c_d_skip': 'grad_w', 'grad_c_out_norm': 'grad_w', 'grad_d_conv_w': 'grad_w', 'grad_d_a_log': 'grad_w', 'grad_d_dt_bias': 'grad_w', 'grad_d_out_norm': 'grad_w', 'grad_w_out': 'grad_w', 'grad_post_mix_norm': 'grad_w', 'grad_pre_ffn_norm': 'grad_w', 'grad_f_w_in': 'grad_w', 'grad_f_conv_w': 'grad_w', 'grad_f_conv_b': 'grad_w', 'grad_f_w_out': 'grad_w', 'grad_post_ffn_norm': 'grad_w', 'delta_pre_mix_norm': 'delta_w', 'delta_w_in': 'delta_w', 'delta_a_q_norm': 'delta_w', 'delta_a_w_uq': 'delta_w', 'delta_a_kv_norm': 'delta_w', 'delta_a_w_ukv': 'delta_w', 'delta_a_out_norm': 'delta_w', 'delta_b_q_norm': 'delta_w', 'delta_b_k_norm': 'delta_w', 'delta_b_out_norm': 'delta_w', 'delta_c_conv_w': 'delta_w', 'delta_c_conv_b': 'delta_w', 'delta_c_a_log': 'delta_w', 'delta_c_dt_bias': 'delta_w', 'delta_c_d_skip': 'delta_w', 'delta_c_out_norm': 'delta_w', 'delta_d_conv_w': 'delta_w', 'delta_d_a_log': 'delta_w', 'delta_d_dt_bias': 'delta_w', 'delta_d_out_norm': 'delta_w', 'delta_w_out': 'delta_w', 'delta_post_mix_norm': 'delta_w', 'delta_pre_ffn_norm': 'delta_w', 'delta_f_w_in': 'delta_w', 'delta_f_conv_w': 'delta_w', 'delta_f_conv_b': 'delta_w', 'delta_f_w_out': 'delta_w', 'delta_post_ffn_norm': 'delta_w', 'new_m_pre_mix_norm': 'new_m', 'new_m_w_in': 'new_m', 'new_m_a_q_norm': 'new_m', 'new_m_a_w_uq': 'new_m', 'new_m_a_kv_norm': 'new_m', 'new_m_a_w_ukv': 'new_m', 'new_m_a_out_norm': 'new_m', 'new_m_b_q_norm': 'new_m', 'new_m_b_k_norm': 'new_m', 'new_m_b_out_norm': 'new_m', 'new_m_c_conv_w': 'new_m', 'new_m_c_conv_b': 'new_m', 'new_m_c_a_log': 'new_m', 'new_m_c_dt_bias': 'new_m', 'new_m_c_d_skip': 'new_m', 'new_m_c_out_norm': 'new_m', 'new_m_d_conv_w': 'new_m', 'new_m_d_a_log': 'new_m', 'new_m_d_dt_bias': 'new_m', 'new_m_d_out_norm': 'new_m', 'new_m_w_out': 'new_m', 'new_m_post_mix_norm': 'new_m', 'new_m_pre_ffn_norm': 'new_m', 'new_m_f_w_in': 'new_m', 'new_m_f_conv_w': 'new_m', 'new_m_f_conv_b': 'new_m', 'new_m_f_w_out': 'new_m', 'new_m_post_ffn_norm': 'new_m', 'new_v_pre_mix_norm': 'new_v', 'new_v_w_in': 'new_v', 'new_v_a_q_norm': 'new_v', 'new_v_a_w_uq': 'new_v', 'new_v_a_kv_norm': 'new_v', 'new_v_a_w_ukv': 'new_v', 'new_v_a_out_norm': 'new_v', 'new_v_b_q_norm': 'new_v', 'new_v_b_k_norm': 'new_v', 'new_v_b_out_norm': 'new_v', 'new_v_c_conv_w': 'new_v', 'new_v_c_conv_b': 'new_v', 'new_v_c_a_log': 'new_v', 'new_v_c_dt_bias': 'new_v', 'new_v_c_d_skip': 'new_v', 'new_v_c_out_norm': 'new_v', 'new_v_d_conv_w': 'new_v', 'new_v_d_a_log': 'new_v', 'new_v_d_dt_bias': 'new_v', 'new_v_d_out_norm': 'new_v', 'new_v_w_out': 'new_v', 'new_v_post_mix_norm': 'new_v', 'new_v_pre_ffn_norm': 'new_v', 'new_v_f_w_in': 'new_v', 'new_v_f_conv_w': 'new_v', 'new_v_f_conv_b': 'new_v', 'new_v_f_w_out': 'new_v', 'new_v_post_ffn_norm': 'new_v'}


def _forward(args):
    return _fwd_reference(*[args[k] for k in FWD_PARAMS])


def _output_shape():
    def fwd():
        inp = _fwd_setup_inputs(0)
        return _fwd_reference(*[inp[k] for k in FWD_PARAMS])
    out = _jax.eval_shape(fwd)
    return out.shape, out.dtype

N_MICROBATCH = 1
ADAM_LR = 0.001
ADAM_B1 = 0.9
ADAM_B2 = 0.999
ADAM_EPS = 1e-08
ADAM_WD = 0.01
ADAM_STEP = 10
PER_EXAMPLE_BATCH_AXIS = {'x': 0, 'loss_target': 0}
SHARED_INPUTS = []
_WEIGHT_DTYPES = {'pre_mix_norm': _jnp.float32, 'w_in': _jnp.float32, 'a_q_norm': _jnp.float32, 'a_w_uq': _jnp.float32, 'a_kv_norm': _jnp.float32, 'a_w_ukv': _jnp.float32, 'a_out_norm': _jnp.float32, 'b_q_norm': _jnp.float32, 'b_k_norm': _jnp.float32, 'b_out_norm': _jnp.float32, 'c_conv_w': _jnp.float32, 'c_conv_b': _jnp.float32, 'c_a_log': _jnp.float32, 'c_dt_bias': _jnp.float32, 'c_d_skip': _jnp.float32, 'c_out_norm': _jnp.float32, 'd_conv_w': _jnp.float32, 'd_a_log': _jnp.float32, 'd_dt_bias': _jnp.float32, 'd_out_norm': _jnp.float32, 'w_out': _jnp.float32, 'post_mix_norm': _jnp.float32, 'pre_ffn_norm': _jnp.float32, 'f_w_in': _jnp.float32, 'f_conv_w': _jnp.float32, 'f_conv_b': _jnp.float32, 'f_w_out': _jnp.float32, 'post_ffn_norm': _jnp.float32}
MOMENT_SCALE = {'pre_mix_norm': 6.654325e+00, 'w_in': 4.419561e+00, 'a_q_norm': 2.131287e+00, 'a_w_uq': 1.352979e+00, 'a_kv_norm': 1.068332e+01, 'a_w_ukv': 6.024538e+00, 'a_out_norm': 8.804863e+00, 'b_q_norm': 3.512517e+00, 'b_k_norm': 3.309599e+00, 'b_out_norm': 1.304668e+01, 'c_conv_w': 2.530201e+00, 'c_conv_b': 9.692471e+00, 'c_a_log': 6.895355e+00, 'c_dt_bias': 1.429239e+00, 'c_d_skip': 1.619070e+01, 'c_out_norm': 5.802700e+00, 'd_conv_w': 1.489534e+00, 'd_a_log': 1.971405e+00, 'd_dt_bias': 1.924523e+00, 'd_out_norm': 9.354078e+00, 'w_out': 8.124334e+00, 'post_mix_norm': 6.494323e+01, 'pre_ffn_norm': 3.219141e+00, 'f_w_in': 1.408042e+00, 'f_conv_w': 1.725091e+00, 'f_conv_b': 6.898025e+00, 'f_w_out': 3.077725e+00, 'post_ffn_norm': 6.360323e+01}


def _to_microbatches(a, axis):
    t = _jnp.moveaxis(a, axis, 0)
    t = t.reshape((N_MICROBATCH, t.shape[0] // N_MICROBATCH) + t.shape[1:])
    return _jnp.moveaxis(t, 1, axis + 1)


def setup_inputs(seed: int = 0) -> dict:
    inp = _fwd_setup_inputs(seed)
    key = _jax.random.fold_in(_jax.random.key(seed), 7919)
    shape, _ = _output_shape()
    out = dict(inp)
    out["loss_target"] = _jax.random.normal(_jax.random.fold_in(key, 0), shape, _jnp.float32)
    for i, name in enumerate(TWIN_WEIGHTS):
        w = inp[name].astype(_jnp.float32)
        if MOMENT_SCALE is None:
            s = _jnp.sqrt(_jnp.mean(_jnp.square(w)) + 1e-30)
        else:
            s = MOMENT_SCALE[name]
        km, kv = _jax.random.split(_jax.random.fold_in(key, i + 1))
        out[name] = w
        out["m_" + name] = s * _jax.random.normal(km, w.shape, _jnp.float32)
        out["v_" + name] = (s * s) * _jax.random.uniform(kv, w.shape, _jnp.float32, 0.5, 1.5)
    if N_MICROBATCH > 1:
        for name, axis in PER_EXAMPLE_BATCH_AXIS.items():
            out[name] = _to_microbatches(out[name], axis)
    return {'x': out['x'], 'pre_mix_norm': out['pre_mix_norm'], 'w_in': out['w_in'], 'a_q_norm': out['a_q_norm'], 'a_w_uq': out['a_w_uq'], 'a_kv_norm': out['a_kv_norm'], 'a_w_ukv': out['a_w_ukv'], 'a_out_norm': out['a_out_norm'], 'b_q_norm': out['b_q_norm'], 'b_k_norm': out['b_k_norm'], 'b_out_norm': out['b_out_norm'], 'c_conv_w': out['c_conv_w'], 'c_conv_b': out['c_conv_b'], 'c_a_log': out['c_a_log'], 'c_dt_bias': out['c_dt_bias'], 'c_d_skip': out['c_d_skip'], 'c_out_norm': out['c_out_norm'], 'd_conv_w': out['d_conv_w'], 'd_a_log': out['d_a_log'], 'd_dt_bias': out['d_dt_bias'], 'd_out_norm': out['d_out_norm'], 'w_out': out['w_out'], 'post_mix_norm': out['post_mix_norm'], 'pre_ffn_norm': out['pre_ffn_norm'], 'f_w_in': out['f_w_in'], 'f_conv_w': out['f_conv_w'], 'f_conv_b': out['f_conv_b'], 'f_w_out': out['f_w_out'], 'post_ffn_norm': out['post_ffn_norm'], 'loss_target': out['loss_target'], 'm_pre_mix_norm': out['m_pre_mix_norm'], 'm_w_in': out['m_w_in'], 'm_a_q_norm': out['m_a_q_norm'], 'm_a_w_uq': out['m_a_w_uq'], 'm_a_kv_norm': out['m_a_kv_norm'], 'm_a_w_ukv': out['m_a_w_ukv'], 'm_a_out_norm': out['m_a_out_norm'], 'm_b_q_norm': out['m_b_q_norm'], 'm_b_k_norm': out['m_b_k_norm'], 'm_b_out_norm': out['m_b_out_norm'], 'm_c_conv_w': out['m_c_conv_w'], 'm_c_conv_b': out['m_c_conv_b'], 'm_c_a_log': out['m_c_a_log'], 'm_c_dt_bias': out['m_c_dt_bias'], 'm_c_d_skip': out['m_c_d_skip'], 'm_c_out_norm': out['m_c_out_norm'], 'm_d_conv_w': out['m_d_conv_w'], 'm_d_a_log': out['m_d_a_log'], 'm_d_dt_bias': out['m_d_dt_bias'], 'm_d_out_norm': out['m_d_out_norm'], 'm_w_out': out['m_w_out'], 'm_post_mix_norm': out['m_post_mix_norm'], 'm_pre_ffn_norm': out['m_pre_ffn_norm'], 'm_f_w_in': out['m_f_w_in'], 'm_f_conv_w': out['m_f_conv_w'], 'm_f_conv_b': out['m_f_conv_b'], 'm_f_w_out': out['m_f_w_out'], 'm_post_ffn_norm': out['m_post_ffn_norm'], 'v_pre_mix_norm': out['v_pre_mix_norm'], 'v_w_in': out['v_w_in'], 'v_a_q_norm': out['v_a_q_norm'], 'v_a_w_uq': out['v_a_w_uq'], 'v_a_kv_norm': out['v_a_kv_norm'], 'v_a_w_ukv': out['v_a_w_ukv'], 'v_a_out_norm': out['v_a_out_norm'], 'v_b_q_norm': out['v_b_q_norm'], 'v_b_k_norm': out['v_b_k_norm'], 'v_b_out_norm': out['v_b_out_norm'], 'v_c_conv_w': out['v_c_conv_w'], 'v_c_conv_b': out['v_c_conv_b'], 'v_c_a_log': out['v_c_a_log'], 'v_c_dt_bias': out['v_c_dt_bias'], 'v_c_d_skip': out['v_c_d_skip'], 'v_c_out_norm': out['v_c_out_norm'], 'v_d_conv_w': out['v_d_conv_w'], 'v_d_a_log': out['v_d_a_log'], 'v_d_dt_bias': out['v_d_dt_bias'], 'v_d_out_norm': out['v_d_out_norm'], 'v_w_out': out['v_w_out'], 'v_post_mix_norm': out['v_post_mix_norm'], 'v_pre_ffn_norm': out['v_pre_ffn_norm'], 'v_f_w_in': out['v_f_w_in'], 'v_f_conv_w': out['v_f_conv_w'], 'v_f_conv_b': out['v_f_conv_b'], 'v_f_w_out': out['v_f_w_out'], 'v_post_ffn_norm': out['v_post_ffn_norm']}


def _loss(weights, diff, rest, loss_target):
    with _jax.named_scope("forward"):
        args = {**rest, TWIN_DIFF_INPUT: diff, **{k: w.astype(_WEIGHT_DTYPES[k]) for k, w in weights.items()}}
        y = _forward(args)
    with _jax.named_scope("loss_head"):
        err = _jnp.square(y.astype(_jnp.float32) - loss_target)
        return 0.5 * _jnp.sum(_jnp.mean(err, axis=-1)) if err.ndim else 0.5 * err


def _adamw(w, g, m, v):
    m = ADAM_B1 * m + (1.0 - ADAM_B1) * g
    v = ADAM_B2 * v + (1.0 - ADAM_B2) * _jnp.square(g)
    m_hat = m / (1.0 - ADAM_B1 ** ADAM_STEP)
    v_hat = v / (1.0 - ADAM_B2 ** ADAM_STEP)
    delta = -ADAM_LR * (m_hat / (_jnp.sqrt(v_hat) + ADAM_EPS) + ADAM_WD * w)
    return delta, m, v


def reference(x, pre_mix_norm, w_in, a_q_norm, a_w_uq, a_kv_norm, a_w_ukv, a_out_norm, b_q_norm, b_k_norm, b_out_norm, c_conv_w, c_conv_b, c_a_log, c_dt_bias, c_d_skip, c_out_norm, d_conv_w, d_a_log, d_dt_bias, d_out_norm, w_out, post_mix_norm, pre_ffn_norm, f_w_in, f_conv_w, f_conv_b, f_w_out, post_ffn_norm, loss_target, m_pre_mix_norm, m_w_in, m_a_q_norm, m_a_w_uq, m_a_kv_norm, m_a_w_ukv, m_a_out_norm, m_b_q_norm, m_b_k_norm, m_b_out_norm, m_c_conv_w, m_c_conv_b, m_c_a_log, m_c_dt_bias, m_c_d_skip, m_c_out_norm, m_d_conv_w, m_d_a_log, m_d_dt_bias, m_d_out_norm, m_w_out, m_post_mix_norm, m_pre_ffn_norm, m_f_w_in, m_f_conv_w, m_f_conv_b, m_f_w_out, m_post_ffn_norm, v_pre_mix_norm, v_w_in, v_a_q_norm, v_a_w_uq, v_a_kv_norm, v_a_w_ukv, v_a_out_norm, v_b_q_norm, v_b_k_norm, v_b_out_norm, v_c_conv_w, v_c_conv_b, v_c_a_log, v_c_dt_bias, v_c_d_skip, v_c_out_norm, v_d_conv_w, v_d_a_log, v_d_dt_bias, v_d_out_norm, v_w_out, v_post_mix_norm, v_pre_ffn_norm, v_f_w_in, v_f_conv_w, v_f_conv_b, v_f_w_out, v_post_ffn_norm):
    given = dict(x=x, pre_mix_norm=pre_mix_norm, w_in=w_in, a_q_norm=a_q_norm, a_w_uq=a_w_uq, a_kv_norm=a_kv_norm, a_w_ukv=a_w_ukv, a_out_norm=a_out_norm, b_q_norm=b_q_norm, b_k_norm=b_k_norm, b_out_norm=b_out_norm, c_conv_w=c_conv_w, c_conv_b=c_conv_b, c_a_log=c_a_log, c_dt_bias=c_dt_bias, c_d_skip=c_d_skip, c_out_norm=c_out_norm, d_conv_w=d_conv_w, d_a_log=d_a_log, d_dt_bias=d_dt_bias, d_out_norm=d_out_norm, w_out=w_out, post_mix_norm=post_mix_norm, pre_ffn_norm=pre_ffn_norm, f_w_in=f_w_in, f_conv_w=f_conv_w, f_conv_b=f_conv_b, f_w_out=f_w_out, post_ffn_norm=post_ffn_norm, loss_target=loss_target, m_pre_mix_norm=m_pre_mix_norm, m_w_in=m_w_in, m_a_q_norm=m_a_q_norm, m_a_w_uq=m_a_w_uq, m_a_kv_norm=m_a_kv_norm, m_a_w_ukv=m_a_w_ukv, m_a_out_norm=m_a_out_norm, m_b_q_norm=m_b_q_norm, m_b_k_norm=m_b_k_norm, m_b_out_norm=m_b_out_norm, m_c_conv_w=m_c_conv_w, m_c_conv_b=m_c_conv_b, m_c_a_log=m_c_a_log, m_c_dt_bias=m_c_dt_bias, m_c_d_skip=m_c_d_skip, m_c_out_norm=m_c_out_norm, m_d_conv_w=m_d_conv_w, m_d_a_log=m_d_a_log, m_d_dt_bias=m_d_dt_bias, m_d_out_norm=m_d_out_norm, m_w_out=m_w_out, m_post_mix_norm=m_post_mix_norm, m_pre_ffn_norm=m_pre_ffn_norm, m_f_w_in=m_f_w_in, m_f_conv_w=m_f_conv_w, m_f_conv_b=m_f_conv_b, m_f_w_out=m_f_w_out, m_post_ffn_norm=m_post_ffn_norm, v_pre_mix_norm=v_pre_mix_norm, v_w_in=v_w_in, v_a_q_norm=v_a_q_norm, v_a_w_uq=v_a_w_uq, v_a_kv_norm=v_a_kv_norm, v_a_w_ukv=v_a_w_ukv, v_a_out_norm=v_a_out_norm, v_b_q_norm=v_b_q_norm, v_b_k_norm=v_b_k_norm, v_b_out_norm=v_b_out_norm, v_c_conv_w=v_c_conv_w, v_c_conv_b=v_c_conv_b, v_c_a_log=v_c_a_log, v_c_dt_bias=v_c_dt_bias, v_c_d_skip=v_c_d_skip, v_c_out_norm=v_c_out_norm, v_d_conv_w=v_d_conv_w, v_d_a_log=v_d_a_log, v_d_dt_bias=v_d_dt_bias, v_d_out_norm=v_d_out_norm, v_w_out=v_w_out, v_post_mix_norm=v_post_mix_norm, v_pre_ffn_norm=v_pre_ffn_norm, v_f_w_in=v_f_w_in, v_f_conv_w=v_f_conv_w, v_f_conv_b=v_f_conv_b, v_f_w_out=v_f_w_out, v_post_ffn_norm=v_post_ffn_norm)
    weights = {n: given[n] for n in TWIN_WEIGHTS}
    shared = {n: given[n] for n in SHARED_INPUTS}
    per_example = {n: given[n] for n in ['x']}
    grad_fn = _jax.value_and_grad(_loss, argnums=(0, 1))

    def one_microbatch(ex, loss_target):
        ex = dict(ex)
        diff = ex.pop(TWIN_DIFF_INPUT)
        return grad_fn(weights, diff, {**shared, **ex}, loss_target)

    if N_MICROBATCH == 1:
        loss, (grad_w, grad_x) = one_microbatch(per_example, given["loss_target"])
    else:
        def body(carry, xs):
            loss_sum, grad_sum = carry
            l_k, (gw_k, gx_k) = one_microbatch(xs[0], xs[1])
            with _jax.named_scope("update"):
                return (loss_sum + l_k, _jax.tree.map(_jnp.add, grad_sum, gw_k)), gx_k

        init = (_jnp.zeros((), _jnp.float32), _jax.tree.map(_jnp.zeros_like, weights))
        (loss, grad_w), grad_x = _jax.lax.scan(body, init, (per_example, given["loss_target"]))
    with _jax.named_scope("update"):
        delta_w, new_m, new_v = {}, {}, {}
        for n in TWIN_WEIGHTS:
            delta_w[n], new_m[n], new_v[n] = _adamw(weights[n], grad_w[n], given["m_" + n], given["v_" + n])
    return (loss, grad_x, *[grad_w[n] for n in TWIN_WEIGHTS], *[delta_w[n] for n in TWIN_WEIGHTS],
            *[new_m[n] for n in TWIN_WEIGHTS], *[new_v[n] for n in TWIN_WEIGHTS])
```

```python
import functools
import math

import jax
import jax.numpy as jnp
from jax import lax
from jax.experimental import pallas as pl
from jax.experimental.pallas import tpu as pltpu

f32 = jnp.float32
MXU_DTYPE = jnp.bfloat16
HI = lax.Precision.HIGHEST

D_MODEL = 1024
DEPTH = 2
GRID_W = 64
ROPE_BASE = 10000.0
EPS = 1e-6
A_Q_LORA, A_KV_LORA, A_ROPE, A_NOPE = 192, 128, 32, 64
SSD_CHUNK = 128
DN_CHUNK = 64
HEAD = 64
HPAD = 128
D_FF = 2816
FF_BLK = 256
N_DEV = 8
ADAM_LR, ADAM_B1, ADAM_B2, ADAM_EPS, ADAM_WD, ADAM_STEP = 0.001, 0.9, 0.999, 1e-08, 0.01, 10

V7X_VMEM_BYTES = 64 * 2 ** 20
VMEM_LIMIT = (V7X_VMEM_BYTES * 3) // 4

PIN = 3072
SEG_A, SEG_B, SEG_CX, SEG_DX, SEG_CZ, SEG_DZ, SEG_CDT, SEG_DAB = 0, 512, 1024, 1536, 2304, 2560, 2816, 2944


def _tile(n, pref):
    for t in (512, 256, 128, 64, 32, 16, 8):
        if t <= pref and n % t == 0:
            return t
    raise ValueError(f"no tile for {n}")


def _params(sem=None):
    return pltpu.CompilerParams(vmem_limit_bytes=VMEM_LIMIT, dimension_semantics=sem)


def _tup(r):
    return tuple(r) if isinstance(r, (tuple, list)) else (r,)


def matmul(a, b, form, out_dtype, name):
    if form == "nn":
        (M, K), N = a.shape, b.shape[1]
    elif form == "nt":
        (M, K), N = a.shape, b.shape[0]
    else:
        (K, M), N = a.shape, b.shape[1]
    tm, tn, tk = _tile(M, 512), _tile(N, 512), _tile(K, 512)
    nk = K // tk
    dims = {"nn": ((1,), (0,)), "nt": ((1,), (1,)), "tn": ((0,), (0,))}[form]

    def body(a_ref, b_ref, o_ref, acc_ref):
        k = pl.program_id(2)
        part = lax.dot_general(a_ref[...].astype(MXU_DTYPE), b_ref[...].astype(MXU_DTYPE), (dims, ((), ())),
                               preferred_element_type=f32)

        @pl.when(k == 0)
        def _():
            acc_ref[...] = part

        @pl.when(k > 0)
        def _():
            acc_ref[...] += part

        @pl.when(k == nk - 1)
        def _():
            o_ref[...] = acc_ref[...].astype(out_dtype)

    a_spec = pl.BlockSpec((tk, tm), lambda i, j, k: (k, i)) if form == "tn" else pl.BlockSpec((tm, tk), lambda i, j, k: (i, k))
    b_spec = pl.BlockSpec((tn, tk), lambda i, j, k: (j, k)) if form == "nt" else pl.BlockSpec((tk, tn), lambda i, j, k: (k, j))
    return pl.pallas_call(
        body, name=name, grid=(M // tm, N // tn, nk), in_specs=[a_spec, b_spec],
        out_specs=pl.BlockSpec((tm, tn), lambda i, j, k: (i, j)), out_shape=jax.ShapeDtypeStruct((M, N), out_dtype),
        scratch_shapes=[pltpu.VMEM((tm, tn), f32)], compiler_params=_params(("parallel", "parallel", "arbitrary")),
    )(a, b)


def _row_specs(rows, tm):
    return [pl.BlockSpec((tm, w), lambda i, b=b: (i, b)) for (_, w, b) in rows]


def _full_specs(params):
    return [pl.BlockSpec(p.shape, lambda i: (0, 0)) for p in params]


def rowwise(fn, rows, params, outs, name, tm=256):
    L = rows[0][0].shape[0]
    tm = _tile(L, tm)
    n_in = len(rows) + len(params)

    def body(*refs):
        res = _tup(fn(*[r[...].astype(f32) for r in refs[:n_in]]))
        for o_ref, r in zip(refs[n_in:], res, strict=True):
            o_ref[...] = r.astype(o_ref.dtype)

    res = pl.pallas_call(
        body, name=name, grid=(L // tm,), in_specs=_row_specs(rows, tm) + _full_specs(params),
        out_specs=[pl.BlockSpec((tm, w), lambda i: (i, 0)) for (w, _) in outs],
        out_shape=[jax.ShapeDtypeStruct((L, w), dt) for (w, dt) in outs], compiler_params=_params(("parallel",)),
    )(*[r[0] for r in rows], *params)
    return list(res)


def rowwise_vjp(fn, rows, params, cts, name, row_grads, param_grads, tm=256):
    L = rows[0][0].shape[0]
    tm = _tile(L, tm)
    nr, npar, nct = len(rows), len(params), len(cts)

    def body(*refs):
        i = pl.program_id(0)
        rv = [r[...].astype(f32) for r in refs[:nr]]
        pv = [r[...].astype(f32) for r in refs[nr:nr + npar]]
        cv = tuple(r[...].astype(f32) for r in refs[nr + npar:nr + npar + nct])
        out_refs = refs[nr + npar + nct:]

        def g(*diff):
            rr, pp = list(rv), list(pv)
            for k, v in zip(row_grads, diff[:len(row_grads)]):
                rr[k] = v
            for k, v in zip(param_grads, diff[len(row_grads):]):
                pp[k] = v
            return _tup(fn(*rr, *pp))

        _, vjp = jax.vjp(g, *[rv[k] for k in row_grads], *[pv[k] for k in param_grads])
        grads = vjp(cv)
        for o_ref, gval in zip(out_refs[:len(row_grads)], grads[:len(row_grads)]):
            o_ref[...] = gval
        for o_ref, gval in zip(out_refs[len(row_grads):], grads[len(row_grads):]):
            @pl.when(i == 0)
            def _(o_ref=o_ref, gval=gval):
                o_ref[...] = gval

            @pl.when(i > 0)
            def _(o_ref=o_ref, gval=gval):
                o_ref[...] += gval

    out_specs = [pl.BlockSpec((tm, rows[k][1]), lambda i: (i, 0)) for k in row_grads] + \
                [pl.BlockSpec(params[k].shape, lambda i: (0, 0)) for k in param_grads]
    out_shape = [jax.ShapeDtypeStruct((L, rows[k][1]), f32) for k in row_grads] + \
                [jax.ShapeDtypeStruct(params[k].shape, f32) for k in param_grads]
    res = pl.pallas_call(
        body, name=name, grid=(L // tm,), in_specs=_row_specs(rows, tm) + _full_specs(params) + _row_specs(cts, tm),
        out_specs=out_specs, out_shape=out_shape, compiler_params=_params(("arbitrary",)),
    )(*[r[0] for r in rows], *params, *[c[0] for c in cts])
    res = list(res)
    return res[:len(row_grads)], res[len(row_grads):]


def _rms(x, w, n=None):
    n = x.shape[-1] if n is None else n
    return x * lax.rsqrt(jnp.sum(x * x, axis=-1, keepdims=True) * (1.0 / n) + EPS) * w


def _silu(x):
    return x * jax.nn.sigmoid(x)


def _softplus(x):
    return jnp.maximum(x, 0.0) + jnp.log1p(jnp.exp(-jnp.abs(x)))


def _mm(a, b, dims, precision=None):
    if precision is None:
        a, b = a.astype(MXU_DTYPE), b.astype(MXU_DTYPE)
    return lax.dot_general(a, b, (dims, ((), ())), precision=precision, preferred_element_type=f32)


_NN, _NT, _TN = ((1,), (0,)), ((1,), (1,)), ((0,), (0,))


@functools.partial(jax.custom_vjp, nondiff_argnums=(1, 2))
def _roll(x, shift, axis):
    return pltpu.roll(x, shift, axis)


def _roll_fwd(x, shift, axis):
    return pltpu.roll(x, shift, axis), None


def _roll_bwd(shift, axis, _, ct):
    return (pltpu.roll(ct, (ct.shape[axis] - shift) % ct.shape[axis], axis),)


_roll.defvjp(_roll_fwd, _roll_bwd)


def _rope(x, cos, sin, lo, half):
    n = x.shape[-1]
    lane = lax.broadcasted_iota(jnp.int32, x.shape, x.ndim - 1) % HPAD - lo
    first = ((lane >= 0) & (lane < half)) | ((lane >= 2 * half) & (lane < 3 * half))
    rot = jnp.where(first, -_roll(x, n - half, x.ndim - 1), _roll(x, half, x.ndim - 1))
    return x * cos + rot * sin


def _heads(x, n, width=HEAD):
    return [x[:, h * width:(h + 1) * width] for h in range(n)]


def _pad_heads(hs):
    z = jnp.zeros_like(hs[0])
    return jnp.concatenate([t for h in hs for t in (h, z)], axis=-1)


def _unpad_heads(x, n):
    return jnp.concatenate([x[:, h * HPAD:h * HPAD + HEAD] for h in range(n)], axis=-1)


HALO = 8


def _conv_specs(tm, tc, c0, L):
    nh = tm // HALO
    last = L // HALO - 1
    return [pl.BlockSpec((tm, tc), lambda i, j: (i, c0 + j)),
            pl.BlockSpec((HALO, tc), lambda i, j: (jnp.maximum(i * nh - 1, 0), c0 + j)),
            pl.BlockSpec((HALO, tc), lambda i, j: (jnp.minimum((i + 1) * nh, last), c0 + j))]


def _shift3(x):
    n = x.shape[0]
    return _roll(x, 1, 0), _roll(x, n - 1, 0)


def conv_fwd(gfn, x, c0, ncol, tc, w, b, tco, out_dtype, name, tm=256):
    L = x.shape[0]
    tm = _tile(L, tm)
    ni = L // tm

    def body(x_ref, p_ref, n_ref, w_ref, b_ref, o_ref):
        i = pl.program_id(0)
        xv = x_ref[...].astype(f32)
        xp = jnp.where(i == 0, 0.0, p_ref[HALO - 1:HALO, :].astype(f32))
        xn = jnp.where(i == ni - 1, 0.0, n_ref[0:1, :].astype(f32))
        rid = lax.broadcasted_iota(jnp.int32, xv.shape, 0)
        dn, up = _shift3(xv)
        dn = jnp.where(rid == 0, xp, dn)
        up = jnp.where(rid == tm - 1, xn, up)
        c = w_ref[0:1, :] * dn + w_ref[1:2, :] * xv + w_ref[2:3, :] * up + b_ref[...]
        o_ref[...] = gfn(c).astype(o_ref.dtype)

    return pl.pallas_call(
        body, name=name, grid=(ni, ncol),
        in_specs=_conv_specs(tm, tc, c0, L) + [pl.BlockSpec((3, tc), lambda i, j: (0, j)), pl.BlockSpec((1, tc), lambda i, j: (0, j))],
        out_specs=pl.BlockSpec((tm, tco), lambda i, j: (i, j)), out_shape=jax.ShapeDtypeStruct((L, ncol * tco), out_dtype),
        compiler_params=_params(("parallel", "parallel")),
    )(x, x, x, w, b)


def conv_bwd(gfn, x, c0, ncol, tc, w, b, dy, tco, name, tm=256):
    L = x.shape[0]
    tm = _tile(L, tm)
    ni = L // tm
    te = tm + 2 * HALO

    def body(x_ref, xp_ref, xn_ref, d_ref, dp_ref, dn_ref, w_ref, b_ref, dx_ref, dw_ref, db_ref):
        i = pl.program_id(1)
        first, lastb = i == 0, i == ni - 1

        def ext(m, p, n):
            return jnp.concatenate([jnp.where(first, 0.0, p[...].astype(f32)), m[...].astype(f32),
                                    jnp.where(lastb, 0.0, n[...].astype(f32))], axis=0)

        xe = ext(x_ref, xp_ref, xn_ref)
        de = ext(d_ref, dp_ref, dn_ref)
        x_dn, x_up = _shift3(xe)
        w0, w1, w2 = w_ref[0:1, :], w_ref[1:2, :], w_ref[2:3, :]
        ce = w0 * x_dn + w1 * xe + w2 * x_up + b_ref[...]
        _, vjp = jax.vjp(gfn, ce)
        (dce,) = vjp(de)
        dc_dn, dc_up = _shift3(dce)
        dx_ref[...] = (w0 * dc_up + w1 * dce + w2 * dc_dn)[HALO:HALO + tm, :]
        rid = lax.broadcasted_iota(jnp.int32, dce.shape, 0)
        dci = jnp.where((rid >= HALO) & (rid < HALO + tm), dce, 0.0)
        dw = jnp.concatenate([jnp.sum(dci * x_dn, axis=0, keepdims=True), jnp.sum(dci * xe, axis=0, keepdims=True),
                              jnp.sum(dci * x_up, axis=0, keepdims=True)], axis=0)
        db = jnp.sum(dci, axis=0, keepdims=True)

        @pl.when(first)
        def _():
            dw_ref[...] = dw
            db_ref[...] = db

        @pl.when(i > 0)
        def _():
            dw_ref[...] += dw
            db_ref[...] += db

    res = pl.pallas_call(
        body, name=name, grid=(ncol, ni),
        in_specs=[pl.BlockSpec(s.block_shape, (lambda j, i, f=s.index_map: f(i, j))) for s in _conv_specs(tm, tc, c0, L)]
        + [pl.BlockSpec(s.block_shape, (lambda j, i, f=s.index_map: f(i, j))) for s in _conv_specs(tm, tco, 0, L)]
        + [pl.BlockSpec((3, tc), lambda j, i: (0, j)), pl.BlockSpec((1, tc), lambda j, i: (0, j))],
        out_specs=[pl.BlockSpec((tm, tc), lambda j, i: (i, j)), pl.BlockSpec((3, tc), lambda j, i: (0, j)),
                   pl.BlockSpec((1, tc), lambda j, i: (0, j))],
        out_shape=[jax.ShapeDtypeStruct((L, ncol * tc), f32), jax.ShapeDtypeStruct((3, ncol * tc), f32),
                   jax.ShapeDtypeStruct((1, ncol * tc), f32)],
        compiler_params=_params(("parallel", "arbitrary")),
    )(x, x, x, dy, dy, dy, w, b)
    return res


def flash_fwd(q, k, v, n_q, n_kv, scale, name, tq=512, tk=512):
    L = q.shape[0]
    tq, tk = _tile(L, tq), _tile(L, tk)
    rep = n_q // n_kv
    nkv = L // tk

    def body(q_ref, k_ref, v_ref, o_ref, lse_ref):
        qv = q_ref[...]

        def step(c, carry):
            m, l, acc = carry
            off = pl.multiple_of(c * tk, tk)
            s = _mm(qv, k_ref[pl.ds(off, tk), :], _NT) * scale
            m_new = jnp.maximum(m, jnp.max(s, axis=-1, keepdims=True))
            p = jnp.exp(s - m_new)
            alpha = jnp.exp(m - m_new)
            l = alpha * l + jnp.sum(p, axis=-1, keepdims=True)
            acc = alpha * acc + _mm(p, v_ref[pl.ds(off, tk), :], _NN)
            return m_new, l, acc

        m, l, acc = lax.fori_loop(0, nkv, step, (jnp.full((tq, 1), -jnp.inf, f32), jnp.zeros((tq, 1), f32),
                                                  jnp.zeros((tq, HPAD), f32)))
        o_ref[...] = acc / l
        lse_ref[0] = m + jnp.log(l)

    return pl.pallas_call(
        body, name=name, grid=(n_q, L // tq),
        in_specs=[pl.BlockSpec((tq, HPAD), lambda h, i: (i, h)), pl.BlockSpec((L, HPAD), lambda h, i: (0, h // rep)),
                  pl.BlockSpec((L, HPAD), lambda h, i: (0, h // rep))],
        out_specs=[pl.BlockSpec((tq, HPAD), lambda h, i: (i, h)), pl.BlockSpec((1, tq, 1), lambda h, i: (h, i, 0))],
        out_shape=[jax.ShapeDtypeStruct((L, n_q * HPAD), f32), jax.ShapeDtypeStruct((n_q, L, 1), f32)],
        compiler_params=_params(("parallel", "parallel")),
    )(q, k, v)


def flash_bwd(q, k, v, o, lse, do, n_q, n_kv, scale, name, tq=512, tk=512):
    L = q.shape[0]
    tq, tk = _tile(L, tq), _tile(L, tk)
    rep = n_q // n_kv

    def body(k_ref, v_ref, q_ref, do_ref, o_ref, lse_ref, dq_ref, dk_ref, dv_ref):
        j, i = pl.program_id(1), pl.program_id(2)
        kv, vv = k_ref[...], v_ref[...]
        rows = pl.ds(pl.multiple_of(i * tq, tq), tq)
        dk, dv = jnp.zeros((tk, HPAD), f32), jnp.zeros((tk, HPAD), f32)
        for r in range(rep):
            cols = slice(r * HPAD, (r + 1) * HPAD)
            qv = q_ref[:, cols]
            dov = do_ref[:, cols]
            delta = jnp.sum(dov * o_ref[:, cols], axis=-1, keepdims=True)
            p = jnp.exp(_mm(qv, kv, _NT) * scale - lse_ref[r])
            dv = dv + _mm(p, dov, _TN)
            ds = p * (_mm(dov, vv, _NT) - delta) * scale
            dk = dk + _mm(ds, qv, _TN)
            dq = _mm(ds, kv, _NN)

            @pl.when(j == 0)
            def _(dq=dq, cols=cols):
                dq_ref[rows, cols] = dq

            @pl.when(j > 0)
            def _(dq=dq, cols=cols):
                dq_ref[rows, cols] += dq

        @pl.when(i == 0)
        def _():
            dk_ref[...] = dk
            dv_ref[...] = dv

        @pl.when(i > 0)
        def _():
            dk_ref[...] += dk
            dv_ref[...] += dv

    w = rep * HPAD
    return pl.pallas_call(
        body, name=name, grid=(n_kv, L // tk, L // tq),
        in_specs=[pl.BlockSpec((tk, HPAD), lambda g, j, i: (j, g)), pl.BlockSpec((tk, HPAD), lambda g, j, i: (j, g)),
                  pl.BlockSpec((tq, w), lambda g, j, i: (i, g)), pl.BlockSpec((tq, w), lambda g, j, i: (i, g)),
                  pl.BlockSpec((tq, w), lambda g, j, i: (i, g)), pl.BlockSpec((rep, tq, 1), lambda g, j, i: (g, i, 0))],
        out_specs=[pl.BlockSpec((L, w), lambda g, j, i: (0, g)), pl.BlockSpec((tk, HPAD), lambda g, j, i: (j, g)),
                   pl.BlockSpec((tk, HPAD), lambda g, j, i: (j, g))],
        out_shape=[jax.ShapeDtypeStruct((L, n_q * HPAD), f32), jax.ShapeDtypeStruct((L, n_kv * HPAD), f32),
                   jax.ShapeDtypeStruct((L, n_kv * HPAD), f32)],
        compiler_params=_params(("parallel", "arbitrary", "arbitrary")),
    )(k, v, q, do, o, lse)


N_SCAN_HEADS = 4
STATE_ROWS = N_SCAN_HEADS * HEAD


def _tri(n, rev):
    i = lax.broadcasted_iota(jnp.int32, (n, n), 0)
    k = lax.broadcasted_iota(jnp.int32, (n, n), 1)
    return ((k >= i), (k > i)) if rev else ((k <= i), (k < i))


def _decay(acol, incl):
    n = acol.shape[0]
    m1 = jnp.broadcast_to(acol, (n, n))
    return jnp.exp(jnp.where(incl, m1 - m1.T, -jnp.inf))


def _ssd_chunk(S, xbc, win, a_log, dt_bias, rev):
    Q = xbc.shape[0]
    d = 1 if rev else 0
    incl, _ = _tri(Q, rev)
    dt = _softplus(win[:, 4 * d:4 * d + 4] + dt_bias[d:d + 1, :])
    a = dt * (-jnp.exp(a_log[d:d + 1, :]))
    acum = _mm(incl.astype(f32), a, _NN, HI)
    tot = jnp.sum(a, axis=0, keepdims=True)
    xs, Bm, Cm = xbc[:, :256], xbc[:, 256:384], xbc[:, 384:512]
    ys, Ss = [], []
    for h in range(N_SCAN_HEADS):
        g = h // 2
        Bg, Cg = Bm[:, g * HEAD:(g + 1) * HEAD], Cm[:, g * HEAD:(g + 1) * HEAD]
        xdt = xs[:, h * HEAD:(h + 1) * HEAD] * dt[:, h:h + 1]
        ac, th = acum[:, h:h + 1], tot[:, h:h + 1]
        y = _mm(_mm(Cg, Bg, _NT) * _decay(ac, incl), xdt, _NN)
        Sh = S[h * HEAD:(h + 1) * HEAD, :]
        ys.append(y + _mm(Cg, Sh, _NT) * jnp.exp(ac))
        Ss.append(Sh * jnp.exp(th) + _mm(xdt * jnp.exp(th - ac), Bg, _TN))
    return jnp.concatenate(Ss, axis=0), jnp.concatenate(ys, axis=1)


def _inv_unit_tri(Lm):
    n = Lm.shape[0]
    eye = (lax.broadcasted_iota(jnp.int32, (n, n), 0) == lax.broadcasted_iota(jnp.int32, (n, n), 1)).astype(f32)
    P = -Lm
    T = eye + P
    k = 1
    while 2 * k < n:
        P = _mm(P, P, _NN, HI)
        T = T + _mm(T, P, _NN, HI)
        k *= 2
    return T


def _dn_chunk(S, qkv, win, a_log, dt_bias, rev):
    Q = qkv.shape[0]
    d = 1 if rev else 0
    incl, strict = _tri(Q, rev)
    beta = jax.nn.sigmoid(win[:, 4 * d:4 * d + 4])
    gl = -jnp.exp(a_log[d:d + 1, :]) * _softplus(win[:, 8 + 4 * d:12 + 4 * d] + dt_bias[d:d + 1, :])
    G = _mm(incl.astype(f32), gl, _NN, HI)
    tot = jnp.sum(gl, axis=0, keepdims=True)
    os_, Ss = [], []
    for h in range(N_SCAN_HEADS):
        q = qkv[:, h * HEAD:(h + 1) * HEAD] * (HEAD ** -0.5)
        k = qkv[:, 256 + h * HEAD:256 + (h + 1) * HEAD]
        v = qkv[:, 512 + h * HEAD:512 + (h + 1) * HEAD]
        b, Gh, th = beta[:, h:h + 1], G[:, h:h + 1], tot[:, h:h + 1]
        dec = _decay(Gh, incl)
        kb = k * b
        T = _inv_unit_tri(jnp.where(strict, _mm(kb, k, _NT) * dec, 0.0))
        eG = jnp.exp(Gh)
        u = _mm(T, v * b, _NN)
        w = _mm(T, kb * eG, _NN)
        Sh = S[h * HEAD:(h + 1) * HEAD, :]
        vnew = u - _mm(w, Sh, _NN)
        os_.append(_mm(q * eG, Sh, _NN) + _mm(_mm(q, k, _NT) * dec, vnew, _NN))
        Ss.append(Sh * jnp.exp(th) + _mm(k * jnp.exp(th - Gh), vnew, _TN))
    return jnp.concatenate(Ss, axis=0), jnp.concatenate(os_, axis=1)


def scan_fwd(chunk_fn, Q, x, p, win_blk, a_log, dt_bias, rev, prev, name):
    L, W = x.shape
    nc = L // Q
    cidx = (lambda t: nc - 1 - t) if rev else (lambda t: t)
    has_prev = prev is not None

    def body(*refs):
        x_ref, w_ref, al_ref, db_ref = refs[:4]
        prev_ref = refs[4] if has_prev else None
        y_ref, sin_ref, s_scr = refs[4 + has_prev:]

        @pl.when(pl.program_id(0) == 0)
        def _():
            s_scr[...] = jnp.zeros_like(s_scr)

        S = s_scr[...]
        sin_ref[...] = S
        S_next, y = chunk_fn(S, x_ref[...], w_ref[...], al_ref[...], db_ref[...], rev)
        s_scr[...] = S_next
        y_ref[...] = y + prev_ref[...] if has_prev else y

    in_specs = [pl.BlockSpec((Q, W), lambda t: (cidx(t), 0)), pl.BlockSpec((Q, HPAD), lambda t: (cidx(t), win_blk)),
                pl.BlockSpec(a_log.shape, lambda t: (0, 0)), pl.BlockSpec(dt_bias.shape, lambda t: (0, 0))]
    args = [x, p, a_log, dt_bias]
    if has_prev:
        in_specs.append(pl.BlockSpec((Q, STATE_ROWS), lambda t: (cidx(t), 0)))
        args.append(prev)
    return pl.pallas_call(
        body, name=name, grid=(nc,), in_specs=in_specs,
        out_specs=[pl.BlockSpec((Q, STATE_ROWS), lambda t: (cidx(t), 0)), pl.BlockSpec((STATE_ROWS, HEAD), lambda t: (cidx(t), 0))],
        out_shape=[jax.ShapeDtypeStruct((L, STATE_ROWS), f32), jax.ShapeDtypeStruct((nc * STATE_ROWS, HEAD), f32)],
        scratch_shapes=[pltpu.VMEM((STATE_ROWS, HEAD), f32)], compiler_params=_params(("arbitrary",)),
    )(*args)


def scan_bwd(chunk_fn, Q, x, p, win_blk, a_log, dt_bias, rev, s_in, dy, prev, name):
    L, W = x.shape
    nc = L // Q
    cidx = (lambda t: t) if rev else (lambda t: nc - 1 - t)
    prev = (None,) * 4 if prev is None else tuple(prev)
    have = [q is not None for q in prev]

    def body(*refs):
        x_ref, w_ref, al_ref, db_ref, sin_ref, dy_ref = refs[:6]
        it = iter(refs[6:6 + sum(have)])
        prev_refs = [next(it) if hv else None for hv in have]
        dx_ref, dw_ref, dal_ref, ddb_ref, ds_scr = refs[6 + sum(have):]
        t = pl.program_id(0)

        @pl.when(t == 0)
        def _():
            ds_scr[...] = jnp.zeros_like(ds_scr)
            dal_ref[...] = prev_refs[2][...] if have[2] else jnp.zeros_like(dal_ref)
            ddb_ref[...] = prev_refs[3][...] if have[3] else jnp.zeros_like(ddb_ref)

        _, vjp = jax.vjp(lambda S, xv, wv, al, db: chunk_fn(S, xv, wv, al, db, rev),
                         sin_ref[...], x_ref[...], w_ref[...], al_ref[...], db_ref[...])
        dS, dx, dw, dal, ddb = vjp((ds_scr[...], dy_ref[...]))
        ds_scr[...] = dS
        dx_ref[...] = dx + prev_refs[0][...] if have[0] else dx
        dw_ref[...] = dw + prev_refs[1][...] if have[1] else dw
        dal_ref[...] += dal
        ddb_ref[...] += ddb

    row = lambda w, b=0: pl.BlockSpec((Q, w), lambda t: (cidx(t), b))
    small = pl.BlockSpec(a_log.shape, lambda t: (0, 0))
    in_specs = [row(W), row(HPAD, win_blk), small, small, pl.BlockSpec((STATE_ROWS, HEAD), lambda t: (cidx(t), 0)), row(STATE_ROWS)]
    args = [x, p, a_log, dt_bias, s_in, dy]
    for q, spec in zip(prev, [row(W), row(HPAD), small, small]):
        if q is not None:
            in_specs.append(spec)
            args.append(q)
    return pl.pallas_call(
        body, name=name, grid=(nc,), in_specs=in_specs, out_specs=[row(W), row(HPAD), small, small],
        out_shape=[jax.ShapeDtypeStruct((L, W), f32), jax.ShapeDtypeStruct((L, HPAD), f32),
                   jax.ShapeDtypeStruct(a_log.shape, f32), jax.ShapeDtypeStruct(a_log.shape, f32)],
        scratch_shapes=[pltpu.VMEM((STATE_ROWS, HEAD), f32)], compiler_params=_params(("arbitrary",)),
    )(*args)


def _cat(xs):
    return jnp.concatenate(xs, axis=-1)


def _a_prep(pa, cos, sin, qn, wuq, kvn, wk, wv):
    cq, ckv, kr = pa[:, :256], pa[:, 256:384], pa[:, 384:512]
    q = _rope(_mm(_rms(cq, qn, A_Q_LORA), wuq, _NN), _cat([cos] * 4), _cat([sin] * 4), A_NOPE, A_ROPE // 4)
    kvh = _rms(ckv, kvn)
    k = _mm(kvh, wk, _NN) + _cat([_rope(kr, cos, sin, A_NOPE, A_ROPE // 4)] * 4)
    return q, k, _mm(kvh, wv, _NN)


def _b_prep(pb, cos, sin, qn, kn):
    q = _pad_heads([_rms(h, qn) for h in _heads(pb[:, :256], 4)])
    k = _pad_heads([_rms(h, kn) for h in _heads(pb[:, 256:384], 2)])
    v = _pad_heads(_heads(pb[:, 384:512], 2))
    return _rope(q, _cat([cos] * 4), _cat([sin] * 4), 0, HEAD // 4), _rope(k, _cat([cos] * 2), _cat([sin] * 2), 0, HEAD // 4), v


def _c_act(c):
    return _silu(c)


def _d_act(c):
    s = _silu(c)
    return _cat([h * lax.rsqrt(jnp.sum(h * h, axis=-1, keepdims=True) + 1e-6) for h in _heads(s[:, :512], 8)] + [s[:, 512:]])


def _ffn_act(c):
    return _silu(c[:, :FF_BLK]) * c[:, FF_BLK:]


def _mix_post(oa, ob, yc, xbc, cz, od, dz, a_out, b_out, dskip, c_out, d_out):
    o_a = _rms(_unpad_heads(oa, 4), a_out)
    o_b = _rms(_unpad_heads(ob, 4), b_out)
    skip = _cat([jnp.broadcast_to(dskip[:, h:h + 1], (1, HEAD)) for h in range(4)])
    o_c = _rms((yc + xbc[:, :256] * skip) * _silu(cz), c_out)
    o_d = _cat([_rms(h, d_out) for h in _heads(od, 4)]) * _silu(dz)
    return _cat([o_a, o_b, o_c, o_d])


def _post_mix(x, y1, g_pm, g_pf):
    x1 = x + _rms(y1, g_pm)
    return x1, _rms(x1, g_pf)


def _post_ffn(x1, y2, g):
    return x1 + _rms(y2, g)


def _pre_bwd(x, g):
    return x, _rms(x, g)


def _layer_fwd(x, W, tabs, n):
    mx = MXU_DTYPE
    cos_a, sin_a, cos_b, sin_b = tabs
    full = lambda a: (a, a.shape[1], 0)
    s = {"x": x}
    (s["h"],) = rowwise(_rms, [full(x)], [W["g_pre"]], [(D_MODEL, mx)], n + "pre")
    p = s["p"] = matmul(s["h"], W["w_in"], "nn", f32, n + "in")
    s["qa"], s["ka"], s["va"] = rowwise(_a_prep, [(p, 512, SEG_A // 512), full(cos_a), full(sin_a)],
                                        [W["a_qn"], W["a_wuq"], W["a_kvn"], W["a_wk"], W["a_wv"]], [(512, mx)] * 3, n + "a_prep")
    s["qb"], s["kb"], s["vb"] = rowwise(_b_prep, [(p, 512, SEG_B // 512), full(cos_b), full(sin_b)], [W["b_qn"], W["b_kn"]],
                                        [(512, mx), (256, mx), (256, mx)], n + "b_prep")
    s["oa"], s["lsea"] = flash_fwd(s["qa"], s["ka"], s["va"], 4, 4, (A_NOPE + A_ROPE) ** -0.5, n + "a_attn")
    s["ob"], s["lseb"] = flash_fwd(s["qb"], s["kb"], s["vb"], 4, 2, HEAD ** -0.5, n + "b_attn")
    xbc = s["xbc"] = conv_fwd(_c_act, p, SEG_CX // 512, 1, 512, W["c_cw"], W["c_cb"], 512, f32, n + "c_conv")
    yc0, s["sc0"] = scan_fwd(_ssd_chunk, SSD_CHUNK, xbc, p, SEG_CDT // HPAD, W["c_alog"], W["c_dtb"], False, None, n + "c_ssd_f")
    s["yc"], s["sc1"] = scan_fwd(_ssd_chunk, SSD_CHUNK, xbc, p, SEG_CDT // HPAD, W["c_alog"], W["c_dtb"], True, yc0, n + "c_ssd_b")
    qkv = s["qkv"] = conv_fwd(_d_act, p, SEG_DX // 768, 1, 768, W["d_cw"], W["d_cb"], 768, f32, n + "d_conv")
    od0, s["sd0"] = scan_fwd(_dn_chunk, DN_CHUNK, qkv, p, SEG_DAB // HPAD, W["d_alog"], W["d_dtb"], False, None, n + "d_dn_f")
    s["od"], s["sd1"] = scan_fwd(_dn_chunk, DN_CHUNK, qkv, p, SEG_DAB // HPAD, W["d_alog"], W["d_dtb"], True, od0, n + "d_dn_b")
    (s["omix"],) = rowwise(_mix_post, _mix_rows(s), _mix_params(W), [(D_MODEL, mx)], n + "mix_post")
    s["y1"] = matmul(s["omix"], W["w_out"], "nn", f32, n + "out")
    s["x1"], s["h2"] = rowwise(_post_mix, [full(x), full(s["y1"])], [W["g_pm"], W["g_pf"]], [(D_MODEL, f32), (D_MODEL, mx)], n + "post_mix")
    s["u"] = matmul(s["h2"], W["f_win"], "nn", f32, n + "f_in")
    s["a"] = conv_fwd(_ffn_act, s["u"], 0, D_FF // FF_BLK, 2 * FF_BLK, W["f_cw"], W["f_cb"], FF_BLK, mx, n + "f_conv")
    s["y2"] = matmul(s["a"], W["f_wout"], "nn", f32, n + "f_out")
    (x2,) = rowwise(_post_ffn, [full(s["x1"]), full(s["y2"])], [W["g_po"]], [(D_MODEL, f32)], n + "post_ffn")
    return x2, s


def _mix_rows(s):
    p = s["p"]
    return [(s["oa"], 512, 0), (s["ob"], 512, 0), (s["yc"], 256, 0), (s["xbc"], 512, 0), (p, 256, SEG_CZ // 256),
            (s["od"], 256, 0), (p, 256, SEG_DZ // 256)]


def _mix_params(W):
    return [W["a_out"], W["b_out"], W["c_dskip"], W["c_out"], W["d_out"]]


def _layer_bwd(s, dx2, W, tabs, n):
    cos_a, sin_a, cos_b, sin_b = tabs
    full = lambda a: (a, a.shape[1], 0)
    p, g = s["p"], {}
    (dx1, dy2), (g["g_po"],) = rowwise_vjp(_post_ffn, [full(s["x1"]), full(s["y2"])], [W["g_po"]], [full(dx2)], n + "post_ffn_b", [0, 1], [0])
    da = matmul(dy2, W["f_wout"], "nt", f32, n + "f_out_dx")
    g["f_wout"] = matmul(s["a"], dy2, "tn", f32, n + "f_out_dw")
    du, g["f_cw"], g["f_cb"] = conv_bwd(_ffn_act, s["u"], 0, D_FF // FF_BLK, 2 * FF_BLK, W["f_cw"], W["f_cb"], da, FF_BLK, n + "f_conv_b")
    dh2 = matmul(du, W["f_win"], "nt", f32, n + "f_in_dx")
    g["f_win"] = matmul(s["h2"], du, "tn", f32, n + "f_in_dw")
    (dx, dy1), (g["g_pm"], g["g_pf"]) = rowwise_vjp(_post_mix, [full(s["x"]), full(s["y1"])], [W["g_pm"], W["g_pf"]],
                                                   [full(dx1), full(dh2)], n + "post_mix_b", [0, 1], [0, 1])
    domix = matmul(dy1, W["w_out"], "nt", f32, n + "out_dx")
    g["w_out"] = matmul(s["omix"], dy1, "tn", f32, n + "out_dw")
    (doa, dob, dyc, dxbc_skip, dcz, dod, ddz), (g["a_out"], g["b_out"], g["c_dskip"], g["c_out"], g["d_out"]) = rowwise_vjp(
        _mix_post, _mix_rows(s), _mix_params(W), [full(domix)], n + "mix_post_b", list(range(7)), list(range(5)))
    gd = scan_bwd(_dn_chunk, DN_CHUNK, s["qkv"], p, SEG_DAB // HPAD, W["d_alog"], W["d_dtb"], False, s["sd0"], dod, None, n + "d_dn_f_b")
    dqkv, ddab, g["d_alog"], g["d_dtb"] = scan_bwd(_dn_chunk, DN_CHUNK, s["qkv"], p, SEG_DAB // HPAD, W["d_alog"], W["d_dtb"], True,
                                                   s["sd1"], dod, gd, n + "d_dn_b_b")
    ddx, g["d_cw"], _ = conv_bwd(_d_act, p, SEG_DX // 768, 1, 768, W["d_cw"], W["d_cb"], dqkv, 768, n + "d_conv_b")
    gc = scan_bwd(_ssd_chunk, SSD_CHUNK, s["xbc"], p, SEG_CDT // HPAD, W["c_alog"], W["c_dtb"], False, s["sc0"], dyc,
                  (dxbc_skip, None, None, None), n + "c_ssd_f_b")
    dxbc, dcdt, g["c_alog"], g["c_dtb"] = scan_bwd(_ssd_chunk, SSD_CHUNK, s["xbc"], p, SEG_CDT // HPAD, W["c_alog"], W["c_dtb"], True,
                                                   s["sc1"], dyc, gc, n + "c_ssd_b_b")
    dcx, g["c_cw"], g["c_cb"] = conv_bwd(_c_act, p, SEG_CX // 512, 1, 512, W["c_cw"], W["c_cb"], dxbc, 512, n + "c_conv_b")
    dqa, dka, dva = flash_bwd(s["qa"], s["ka"], s["va"], s["oa"], s["lsea"], doa, 4, 4, (A_NOPE + A_ROPE) ** -0.5, n + "a_attn_b")
    dqb, dkb, dvb = flash_bwd(s["qb"], s["kb"], s["vb"], s["ob"], s["lseb"], dob, 4, 2, HEAD ** -0.5, n + "b_attn_b")
    (dpa,), (g["a_qn"], g["a_wuq"], g["a_kvn"], g["a_wk"], g["a_wv"]) = rowwise_vjp(
        _a_prep, [(p, 512, SEG_A // 512), full(cos_a), full(sin_a)], [W["a_qn"], W["a_wuq"], W["a_kvn"], W["a_wk"], W["a_wv"]],
        [full(dqa), full(dka), full(dva)], n + "a_prep_b", [0], list(range(5)))
    (dpb,), (g["b_qn"], g["b_kn"]) = rowwise_vjp(_b_prep, [(p, 512, SEG_B // 512), full(cos_b), full(sin_b)], [W["b_qn"], W["b_kn"]],
                                               [full(dqb), full(dkb), full(dvb)], n + "b_prep_b", [0], [0, 1])
    dp = jnp.concatenate([dpa, dpb, dcx, ddx, dcz, ddz, dcdt, ddab], axis=1)
    dh = matmul(dp, W["w_in"], "nt", f32, n + "in_dx")
    g["w_in"] = matmul(s["h"], dp, "tn", f32, n + "in_dw")
    (dx0,), (g["g_pre"],) = rowwise_vjp(_pre_bwd, [full(s["x"])], [W["g_pre"]], [full(dx), full(dh)], n + "pre_b", [0], [0])
    return dx0, g


def loss_and_grad(y, target, name):
    L, D = y.shape
    tm = _tile(L, 512)

    def body(y_ref, t_ref, loss_ref, dy_ref):
        e = y_ref[...] - t_ref[...]
        dy_ref[...] = e * (1.0 / D)
        part = 0.5 * jnp.sum(jnp.sum(e * e, axis=1, keepdims=True) * (1.0 / D), axis=0, keepdims=True)

        @pl.when(pl.program_id(0) == 0)
        def _():
            loss_ref[...] = part

        @pl.when(pl.program_id(0) > 0)
        def _():
            loss_ref[...] += part

    row = pl.BlockSpec((tm, D), lambda i: (i, 0))
    return pl.pallas_call(
        body, name=name, grid=(L // tm,), in_specs=[row, row], out_specs=[pl.BlockSpec((1, 1), lambda i: (0, 0)), row],
        out_shape=[jax.ShapeDtypeStruct((1, 1), f32), jax.ShapeDtypeStruct((L, D), f32)], compiler_params=_params(("arbitrary",)),
    )(y, target)


_IN_SEGS = [(0, 192), (256, 128), (384 + A_NOPE, 32), (512, 256), (768, 128), (896, 128), (SEG_CZ, 256), (SEG_CX, 512),
            (SEG_CDT, 8), (SEG_DX, 768), (SEG_DZ, 256), (SEG_DAB, 16)]


def _pad_in(w):
    src, pieces = 0, {}
    for off, wd in _IN_SEGS:
        pieces[off] = w[..., src:src + wd]
        src += wd
    out, pos = [], 0
    for off in sorted(pieces):
        if off > pos:
            out.append(jnp.zeros(w.shape[:-1] + (off - pos,), w.dtype))
        out.append(pieces[off])
        pos = off + pieces[off].shape[-1]
    out.append(jnp.zeros(w.shape[:-1] + (PIN - pos,), w.dtype))
    return jnp.concatenate(out, axis=-1)


def _unpad_in(wp):
    return jnp.concatenate([wp[..., off:off + wd] for off, wd in _IN_SEGS], axis=-1)


def _pad_last(a, n):
    return jnp.pad(a, [(0, 0)] * (a.ndim - 1) + [(0, n - a.shape[-1])])


def _ff_interleave(w):
    lead = w.shape[:-1]
    return jnp.stack([w[..., :D_FF].reshape(lead + (D_FF // FF_BLK, FF_BLK)), w[..., D_FF:].reshape(lead + (D_FF // FF_BLK, FF_BLK))],
                     axis=-2).reshape(lead + (2 * D_FF,))


def _ff_deinterleave(w):
    lead = w.shape[:-1]
    t = w.reshape(lead + (D_FF // FF_BLK, 2, FF_BLK))
    return jnp.concatenate([t[..., 0, :].reshape(lead + (D_FF,)), t[..., 1, :].reshape(lead + (D_FF,))], axis=-1)


def _row(v):
    return v.reshape(1, -1)


def _kernel_weights(P, l, wdt):
    uq = P["a_w_uq"][l].reshape(A_Q_LORA, 4, A_NOPE + A_ROPE)
    ukv = P["a_w_ukv"][l].reshape(A_KV_LORA, 4, 2 * HEAD)
    z = jnp.zeros((A_KV_LORA, 4, HEAD), ukv.dtype)
    return {
        "g_pre": _row(P["pre_mix_norm"][l]), "w_in": _pad_in(P["w_in"][l]).astype(wdt),
        "a_qn": _pad_last(_row(P["a_q_norm"][l]), 256),
        "a_wuq": jnp.pad(_pad_last(uq, HPAD).reshape(A_Q_LORA, 4 * HPAD), ((0, 256 - A_Q_LORA), (0, 0))).astype(wdt),
        "a_kvn": _row(P["a_kv_norm"][l]),
        "a_wk": jnp.concatenate([ukv[..., :HEAD], z], axis=-1).reshape(A_KV_LORA, 4 * HPAD).astype(wdt),
        "a_wv": jnp.concatenate([ukv[..., HEAD:], z], axis=-1).reshape(A_KV_LORA, 4 * HPAD).astype(wdt),
        "a_out": _row(P["a_out_norm"][l]), "b_qn": _row(P["b_q_norm"][l]), "b_kn": _row(P["b_k_norm"][l]), "b_out": _row(P["b_out_norm"][l]),
        "c_cw": P["c_conv_w"][l], "c_cb": _row(P["c_conv_b"][l]), "c_alog": P["c_a_log"][l], "c_dtb": P["c_dt_bias"][l],
        "c_dskip": _row(P["c_d_skip"][l]), "c_out": _row(P["c_out_norm"][l]),
        "d_cw": P["d_conv_w"][l], "d_cb": jnp.zeros((1, 768), f32), "d_alog": P["d_a_log"][l], "d_dtb": P["d_dt_bias"][l],
        "d_out": _row(P["d_out_norm"][l]), "w_out": P["w_out"][l].astype(wdt), "g_pm": _row(P["post_mix_norm"][l]),
        "g_pf": _row(P["pre_ffn_norm"][l]), "f_win": _ff_interleave(P["f_w_in"][l]).astype(wdt),
        "f_cw": _ff_interleave(P["f_conv_w"][l]), "f_cb": _row(_ff_interleave(P["f_conv_b"][l])),
        "f_wout": P["f_w_out"][l].astype(wdt), "g_po": _row(P["post_ffn_norm"][l]),
    }


def _reference_grads(g):
    uq = g["a_wuq"][:A_Q_LORA].reshape(A_Q_LORA, 4, HPAD)[..., :A_NOPE + A_ROPE].reshape(A_Q_LORA, 4 * (A_NOPE + A_ROPE))
    wk = g["a_wk"].reshape(A_KV_LORA, 4, HPAD)[..., :HEAD]
    wv = g["a_wv"].reshape(A_KV_LORA, 4, HPAD)[..., :HEAD]
    return {
        "pre_mix_norm": g["g_pre"][0], "w_in": _unpad_in(g["w_in"]), "a_q_norm": g["a_qn"][0, :A_Q_LORA], "a_w_uq": uq,
        "a_kv_norm": g["a_kvn"][0], "a_w_ukv": jnp.concatenate([wk, wv], axis=-1).reshape(A_KV_LORA, 8 * HEAD),
        "a_out_norm": g["a_out"][0], "b_q_norm": g["b_qn"][0], "b_k_norm": g["b_kn"][0], "b_out_norm": g["b_out"][0],
        "c_conv_w": g["c_cw"], "c_conv_b": g["c_cb"][0], "c_a_log": g["c_alog"], "c_dt_bias": g["c_dtb"], "c_d_skip": g["c_dskip"][0],
        "c_out_norm": g["c_out"][0], "d_conv_w": g["d_cw"], "d_a_log": g["d_alog"], "d_dt_bias": g["d_dtb"], "d_out_norm": g["d_out"][0],
        "w_out": g["w_out"], "post_mix_norm": g["g_pm"][0], "pre_ffn_norm": g["g_pf"][0], "f_w_in": _ff_deinterleave(g["f_win"]),
        "f_conv_w": _ff_deinterleave(g["f_cw"]), "f_conv_b": _ff_deinterleave(g["f_cb"][0]), "f_w_out": g["f_wout"],
        "post_ffn_norm": g["g_po"][0],
    }


def _rope_tables(L):
    def tables(rot):
        rows = L // GRID_W
        row = jnp.repeat(jnp.arange(rows), GRID_W).astype(f32)
        col = jnp.tile(jnp.arange(GRID_W), rows).astype(f32)
        sec = rot // 2
        inv = ROPE_BASE ** (-jnp.arange(0, sec, 2, dtype=f32) / sec)
        ang = jnp.concatenate([row[:, None] * inv] * 2 + [col[:, None] * inv] * 2, axis=-1)
        return jnp.cos(ang), jnp.sin(ang)

    ca, sa = tables(A_ROPE)
    cb, sb = tables(HEAD)
    one, zero = jnp.ones, jnp.zeros
    return (jnp.concatenate([one((L, A_NOPE), f32), ca, one((L, HPAD - A_NOPE - A_ROPE), f32)], axis=1),
            jnp.concatenate([zero((L, A_NOPE), f32), sa, zero((L, HPAD - A_NOPE - A_ROPE), f32)], axis=1),
            jnp.concatenate([cb, one((L, HPAD - HEAD), f32)], axis=1), jnp.concatenate([sb, zero((L, HPAD - HEAD), f32)], axis=1))


def local_step(x, target, P, wdt):
    tabs = _rope_tables(x.shape[0])
    Ws = [_kernel_weights(P, l, wdt) for l in range(DEPTH)]
    saved = []
    for l in range(DEPTH):
        x, s = _layer_fwd(x, Ws[l], tabs, f"l{l}_")
        saved.append(s)
    loss, dx = loss_and_grad(x, target, "loss")
    grads = [None] * DEPTH
    for l in reversed(range(DEPTH)):
        dx, g = _layer_bwd(saved[l], dx, Ws[l], tabs, f"l{l}_")
        grads[l] = _reference_grads(g)
    return loss, dx, grads


def exchange(src, gather, name):
    blk = src.shape if gather else src.shape[1:]

    def body(src_ref, out_ref, send_sems, recv_sems, local_sem):
        x, y, c = lax.axis_index("x"), lax.axis_index("y"), lax.axis_index("c")
        me = 4 * x + 2 * y + c

        def copy(k, arriving):
            px, py, pc = (1 - x if k & 4 else x), (1 - y if k & 2 else y), (1 - c if k & 1 else c)
            pid = 4 * px + 2 * py + pc
            return pltpu.make_async_remote_copy(
                src_ref=src_ref if gather else src_ref.at[pid], dst_ref=out_ref.at[pid if arriving else me],
                send_sem=send_sems.at[k - 1], recv_sem=recv_sems.at[k - 1], device_id=(px, py, pc), device_id_type=pl.DeviceIdType.MESH)

        local = pltpu.make_async_copy(src_ref if gather else src_ref.at[me], out_ref.at[me], local_sem)
        local.start()
        for k in range(1, N_DEV):
            copy(k, False).start()
        for k in range(1, N_DEV):
            copy(k, False).wait_send()
        for k in range(1, N_DEV):
            copy(k, True).wait_recv()
        local.wait()

    hbm = pl.BlockSpec(memory_space=pl.ANY)
    return pl.pallas_call(
        body, name=name, in_specs=[hbm], out_specs=hbm, out_shape=jax.ShapeDtypeStruct((N_DEV,) + tuple(blk), src.dtype),
        scratch_shapes=[pltpu.SemaphoreType.DMA((N_DEV - 1,)), pltpu.SemaphoreType.DMA((N_DEV - 1,)), pltpu.SemaphoreType.DMA(())],
    )(src)


LANES = 1024


def adamw(gstack, w, m, v, name):
    S, R, _ = gstack.shape
    tr = _tile(R, 256)

    def body(g_ref, w_ref, m_ref, v_ref, go_ref, d_ref, mo_ref, vo_ref):
        g = g_ref[0]
        for sl in range(1, S):
            g = g + g_ref[sl]
        m_new = ADAM_B1 * m_ref[...] + (1.0 - ADAM_B1) * g
        v_new = ADAM_B2 * v_ref[...] + (1.0 - ADAM_B2) * jnp.square(g)
        m_hat = m_new / (1.0 - ADAM_B1 ** ADAM_STEP)
        v_hat = v_new / (1.0 - ADAM_B2 ** ADAM_STEP)
        go_ref[...] = g
        d_ref[...] = -ADAM_LR * (m_hat / (jnp.sqrt(v_hat) + ADAM_EPS) + ADAM_WD * w_ref[...])
        mo_ref[...] = m_new
        vo_ref[...] = v_new

    row = pl.BlockSpec((tr, LANES), lambda i: (i, 0))
    return pl.pallas_call(
        body, name=name, grid=(R // tr,), in_specs=[pl.BlockSpec((S, tr, LANES), lambda i: (0, i, 0)), row, row, row],
        out_specs=[row] * 4, out_shape=[jax.ShapeDtypeStruct((R, LANES), f32)] * 4, compiler_params=_params(("parallel",)),
    )(gstack, w, m, v)


def sum_slots(gstack, name):
    S, R, _ = gstack.shape

    def body(g_ref, o_ref):
        g = g_ref[0]
        for sl in range(1, S):
            g = g + g_ref[sl]
        o_ref[...] = g

    return pl.pallas_call(body, name=name, out_shape=jax.ShapeDtypeStruct((R, LANES), f32), compiler_params=_params())(gstack)


def _pack(parts, rows, dtype=f32):
    flat = jnp.concatenate([q.reshape(-1).astype(dtype) for q in parts])
    return jnp.pad(flat, (0, rows * LANES - flat.shape[0])).reshape(rows, LANES)


def _unpack(buf, shapes):
    lead = buf.shape[:-2]
    flat = buf.reshape(lead + (-1,))
    out, off = [], 0
    for shp in shapes:
        n = math.prod(shp)
        out.append(flat[..., off:off + n].reshape(lead + tuple(shp)))
        off += n
    return out


_WEIGHTS = ["pre_mix_norm", "w_in", "a_q_norm", "a_w_uq", "a_kv_norm", "a_w_ukv", "a_out_norm", "b_q_norm", "b_k_norm", "b_out_norm",
            "c_conv_w", "c_conv_b", "c_a_log", "c_dt_bias", "c_d_skip", "c_out_norm", "d_conv_w", "d_a_log", "d_dt_bias", "d_out_norm",
            "w_out", "post_mix_norm", "pre_ffn_norm", "f_w_in", "f_conv_w", "f_conv_b", "f_w_out", "post_ffn_norm"]
_BIG = {"w_in": 1, "a_w_uq": 1, "a_w_ukv": 1, "w_out": 0, "f_w_in": 1, "f_w_out": 0}
_CONV = ["c_conv_w", "d_conv_w", "f_conv_w"]
_REP = [n for n in _WEIGHTS if n not in _BIG and n not in _CONV]
BIG_ROWS, CONV_ROWS, SMALLG_ROWS, SMALLW_ROWS = 3072, 8, 64, 32


def _join(blocks, axis):
    if axis == 0:
        return blocks.reshape(-1, blocks.shape[2])
    return blocks.transpose(1, 0, 2).reshape(blocks.shape[1], -1)


def _split(full, axis):
    if axis == 0:
        return full.reshape(N_DEV, -1, full.shape[1])
    return full.reshape(full.shape[0], N_DEV, -1).transpose(1, 0, 2)


def kernel(x, pre_mix_norm, w_in, a_q_norm, a_w_uq, a_kv_norm, a_w_ukv, a_out_norm, b_q_norm, b_k_norm, b_out_norm, c_conv_w, c_conv_b, c_a_log, c_dt_bias, c_d_skip, c_out_norm, d_conv_w, d_a_log, d_dt_bias, d_out_norm, w_out, post_mix_norm, pre_ffn_norm, f_w_in, f_conv_w, f_conv_b, f_w_out, post_ffn_norm, loss_target, m_pre_mix_norm, m_w_in, m_a_q_norm, m_a_w_uq, m_a_kv_norm, m_a_w_ukv, m_a_out_norm, m_b_q_norm, m_b_k_norm, m_b_out_norm, m_c_conv_w, m_c_conv_b, m_c_a_log, m_c_dt_bias, m_c_d_skip, m_c_out_norm, m_d_conv_w, m_d_a_log, m_d_dt_bias, m_d_out_norm, m_w_out, m_post_mix_norm, m_pre_ffn_norm, m_f_w_in, m_f_conv_w, m_f_conv_b, m_f_w_out, m_post_ffn_norm, v_pre_mix_norm, v_w_in, v_a_q_norm, v_a_w_uq, v_a_kv_norm, v_a_w_ukv, v_a_out_norm, v_b_q_norm, v_b_k_norm, v_b_out_norm, v_c_conv_w, v_c_conv_b, v_c_a_log, v_c_dt_bias, v_c_d_skip, v_c_out_norm, v_d_conv_w, v_d_a_log, v_d_dt_bias, v_d_out_norm, v_w_out, v_post_mix_norm, v_pre_ffn_norm, v_f_w_in, v_f_conv_w, v_f_conv_b, v_f_w_out, v_post_ffn_norm):
    given = dict(locals())
    w = {n: given[n] for n in _WEIGHTS}
    mom = {n: given["m_" + n] for n in _WEIGHTS}
    var = {n: given["v_" + n] for n in _WEIGHTS}
    me = 4 * lax.axis_index("x") + 2 * lax.axis_index("y") + lax.axis_index("c")
    layered = lambda names: [(l, n) for l in range(DEPTH) for n in names]

    big_shapes = [w[n].shape[1:] for _, n in layered(_BIG)]
    conv_shapes = [w[n].shape[1:] for _, n in layered(_CONV)]
    gbig = exchange(_pack([w[n][l] for l, n in layered(_BIG)], BIG_ROWS, MXU_DTYPE), True, "gather_weights")
    gconv = exchange(_pack([w[n][l] for l, n in layered(_CONV)], CONV_ROWS), True, "gather_conv_taps")
    P = {n: [w[n][l] for l in range(DEPTH)] for n in _REP}
    for (l, n), blocks in zip(layered(_BIG), _unpack(gbig, big_shapes)):
        P.setdefault(n, []).append(_join(blocks, _BIG[n]))
    for (l, n), blocks in zip(layered(_CONV), _unpack(gconv, conv_shapes)):
        P.setdefault(n, []).append(_join(blocks, 1))

    loss, dx, grads = local_step(x[0], loss_target[0], P, MXU_DTYPE)
    loss = lax.psum(loss[0, 0], ("x", "y", "c"))

    send = jnp.concatenate([_split(grads[l][n], _BIG[n]).reshape(N_DEV, -1) for l, n in layered(_BIG)], axis=1)
    recv = exchange(send.reshape(N_DEV, BIG_ROWS, LANES), False, "scatter_weight_grads")
    pack_big = lambda d: _pack([d[n][l] for l, n in layered(_BIG)], BIG_ROWS)
    res_big = adamw(recv, pack_big(w), pack_big(mom), pack_big(var), "adamw_matmul_weights")
    out = {n: [[None] * DEPTH for _ in range(4)] for n in _WEIGHTS}
    for kind, buf in enumerate(res_big):
        for (l, n), a in zip(layered(_BIG), _unpack(buf, big_shapes)):
            out[n][kind][l] = a

    small = _REP + _CONV
    full_shapes = [grads[l][n].shape for l, n in layered(small)]
    gsmall = exchange(_pack([grads[l][n] for l, n in layered(small)], SMALLG_ROWS), True, "gather_small_grads")
    gsum = dict(zip(layered(small), _unpack(sum_slots(gsmall, "sum_small_grads"), full_shapes)))
    for l, n in layered(_CONV):
        cols = w[n].shape[2]
        gsum[(l, n)] = lax.dynamic_slice_in_dim(gsum[(l, n)], me * cols, cols, axis=1)
    held_shapes = [w[n].shape[1:] for _, n in layered(small)]
    pack_small = lambda d: _pack([d[n][l] for l, n in layered(small)], SMALLW_ROWS)
    gpack = _pack([gsum[k] for k in layered(small)], SMALLW_ROWS)
    res_small = adamw(gpack[None], pack_small(w), pack_small(mom), pack_small(var), "adamw_small")
    for kind, buf in enumerate(res_small):
        for (l, n), a in zip(layered(small), _unpack(buf, held_shapes)):
            out[n][kind][l] = a

    stacked = [[jnp.stack(out[n][kind]) for n in _WEIGHTS] for kind in range(4)]
    return (loss, dx[None], *stacked[0], *stacked[1], *stacked[2], *stacked[3])
```

```python
import functools
import math

import jax
import jax.numpy as jnp
from jax import lax
from jax.experimental import pallas as pl
from jax.experimental.pallas import tpu as pltpu

f32 = jnp.float32
MXU_DTYPE = jnp.bfloat16
HI = lax.Precision.HIGHEST

D_MODEL = 1024
DEPTH = 2
GRID_W = 64
ROPE_BASE = 10000.0
EPS = 1e-6
A_Q_LORA, A_KV_LORA, A_ROPE, A_NOPE = 192, 128, 32, 64
SSD_CHUNK = 128
DN_CHUNK = 64
HEAD = 64
HPAD = 128
D_FF = 2816
FF_BLK = 256
N_DEV = 8
ADAM_LR, ADAM_B1, ADAM_B2, ADAM_EPS, ADAM_WD, ADAM_STEP = 0.001, 0.9, 0.999, 1e-08, 0.01, 10

V7X_VMEM_BYTES = 64 * 2 ** 20
VMEM_LIMIT = (V7X_VMEM_BYTES * 3) // 4

PIN = 3072
SEG_A, SEG_B, SEG_CX, SEG_DX, SEG_CZ, SEG_DZ, SEG_CDT, SEG_DAB = 0, 512, 1024, 1536, 2304, 2560, 2816, 2944


def _tile(n, pref):
    for t in (512, 256, 128, 64, 32, 16, 8):
        if t <= pref and n % t == 0:
            return t
    raise ValueError(f"no tile for {n}")


LANE = 128


def _tile_div(n, pref):
    if n <= pref:
        return n
    return max(d for d in range(LANE, pref + 1, LANE) if n % d == 0)


def _params(sem=None):
    return pltpu.CompilerParams(vmem_limit_bytes=VMEM_LIMIT, dimension_semantics=sem)


def _tup(r):
    return tuple(r) if isinstance(r, (tuple, list)) else (r,)


def matmul(a, b, form, out_dtype, name):
    if form == "nn":
        (M, K), N = a.shape, b.shape[1]
    elif form == "nt":
        (M, K), N = a.shape, b.shape[0]
    else:
        (K, M), N = a.shape, b.shape[1]
    if form == "tn":
        tm, tn, tk = _tile_div(M, 1024), _tile_div(N, 1408), _tile_div(K, 1024)
    else:
        tm, tn, tk = _tile_div(M, 512), _tile_div(N, 1536), _tile_div(K, 3072)
    nk = K // tk
    dims = {"nn": ((1,), (0,)), "nt": ((1,), (1,)), "tn": ((0,), (0,))}[form]

    def body(a_ref, b_ref, o_ref, *acc):
        part = lax.dot_general(a_ref[...].astype(MXU_DTYPE), b_ref[...].astype(MXU_DTYPE), (dims, ((), ())),
                               preferred_element_type=f32)
        if nk == 1:
            o_ref[...] = part.astype(out_dtype)
            return
        (acc_ref,) = acc
        k = pl.program_id(2)

        @pl.when(k == 0)
        def _():
            acc_ref[...] = part

        @pl.when((k > 0) & (k < nk - 1))
        def _():
            acc_ref[...] += part

        @pl.when(k == nk - 1)
        def _():
            o_ref[...] = (acc_ref[...] + part).astype(out_dtype)

    a_spec = pl.BlockSpec((tk, tm), lambda i, j, k: (k, i)) if form == "tn" else pl.BlockSpec((tm, tk), lambda i, j, k: (i, k))
    b_spec = pl.BlockSpec((tn, tk), lambda i, j, k: (j, k)) if form == "nt" else pl.BlockSpec((tk, tn), lambda i, j, k: (k, j))
    return pl.pallas_call(
        body, name=name, grid=(M // tm, N // tn, nk), in_specs=[a_spec, b_spec],
        out_specs=pl.BlockSpec((tm, tn), lambda i, j, k: (i, j)), out_shape=jax.ShapeDtypeStruct((M, N), out_dtype),
        scratch_shapes=[pltpu.VMEM((tm, tn), f32)] if nk > 1 else [], compiler_params=_params(("parallel", "parallel", "arbitrary")),
    )(a, b)


def _row_specs(rows, tm):
    return [pl.BlockSpec((tm, w), lambda i, b=b: (i, b)) for (_, w, b) in rows]


def _full_specs(params):
    return [pl.BlockSpec(p.shape, lambda i: (0, 0)) for p in params]


def rowwise(fn, rows, params, outs, name, tm=256):
    L = rows[0][0].shape[0]
    tm = _tile(L, tm)
    n_in = len(rows) + len(params)

    def body(*refs):
        res = _tup(fn(*[r[...].astype(f32) for r in refs[:n_in]]))
        for o_ref, r in zip(refs[n_in:], res, strict=True):
            o_ref[...] = r.astype(o_ref.dtype)

    res = pl.pallas_call(
        body, name=name, grid=(L // tm,), in_specs=_row_specs(rows, tm) + _full_specs(params),
        out_specs=[pl.BlockSpec((tm, w), lambda i: (i, 0)) for (w, _) in outs],
        out_shape=[jax.ShapeDtypeStruct((L, w), dt) for (w, dt) in outs], compiler_params=_params(("parallel",)),
    )(*[r[0] for r in rows], *params)
    return list(res)


def rowwise_vjp(fn, rows, params, cts, name, row_grads, param_grads, tm=256):
    L = rows[0][0].shape[0]
    tm = _tile(L, tm)
    nr, npar, nct = len(rows), len(params), len(cts)

    def body(*refs):
        i = pl.program_id(0)
        rv = [r[...].astype(f32) for r in refs[:nr]]
        pv = [r[...].astype(f32) for r in refs[nr:nr + npar]]
        cv = tuple(r[...].astype(f32) for r in refs[nr + npar:nr + npar + nct])
        out_refs = refs[nr + npar + nct:]

        def g(*diff):
            rr, pp = list(rv), list(pv)
            for k, v in zip(row_grads, diff[:len(row_grads)]):
                rr[k] = v
            for k, v in zip(param_grads, diff[len(row_grads):]):
                pp[k] = v
            return _tup(fn(*rr, *pp))

        _, vjp = jax.vjp(g, *[rv[k] for k in row_grads], *[pv[k] for k in param_grads])
        grads = vjp(cv)
        for o_ref, gval in zip(out_refs[:len(row_grads)], grads[:len(row_grads)]):
            o_ref[...] = gval
        for o_ref, gval in zip(out_refs[len(row_grads):], grads[len(row_grads):]):
            @pl.when(i == 0)
            def _(o_ref=o_ref, gval=gval):
                o_ref[...] = gval

            @pl.when(i > 0)
            def _(o_ref=o_ref, gval=gval):
                o_ref[...] += gval

    out_specs = [pl.BlockSpec((tm, rows[k][1]), lambda i: (i, 0)) for k in row_grads] + \
                [pl.BlockSpec(params[k].shape, lambda i: (0, 0)) for k in param_grads]
    out_shape = [jax.ShapeDtypeStruct((L, rows[k][1]), f32) for k in row_grads] + \
                [jax.ShapeDtypeStruct(params[k].shape, f32) for k in param_grads]
    res = pl.pallas_call(
        body, name=name, grid=(L // tm,), in_specs=_row_specs(rows, tm) + _full_specs(params) + _row_specs(cts, tm),
        out_specs=out_specs, out_shape=out_shape, compiler_params=_params(("arbitrary",)),
    )(*[r[0] for r in rows], *params, *[c[0] for c in cts])
    res = list(res)
    return res[:len(row_grads)], res[len(row_grads):]


def _rms(x, w, n=None):
    n = x.shape[-1] if n is None else n
    return x * lax.rsqrt(jnp.sum(x * x, axis=-1, keepdims=True) * (1.0 / n) + EPS) * w


def _silu(x):
    return x * jax.nn.sigmoid(x)


def _softplus(x):
    return jnp.maximum(x, 0.0) + jnp.log1p(jnp.exp(-jnp.abs(x)))


def _mm(a, b, dims, precision=None):
    if precision is None:
        a, b = a.astype(MXU_DTYPE), b.astype(MXU_DTYPE)
    return lax.dot_general(a, b, (dims, ((), ())), precision=precision, preferred_element_type=f32)


_NN, _NT, _TN = ((1,), (0,)), ((1,), (1,)), ((0,), (0,))


@functools.partial(jax.custom_vjp, nondiff_argnums=(1, 2))
def _roll(x, shift, axis):
    return pltpu.roll(x, shift, axis)


def _roll_fwd(x, shift, axis):
    return pltpu.roll(x, shift, axis), None


def _roll_bwd(shift, axis, _, ct):
    return (pltpu.roll(ct, (ct.shape[axis] - shift) % ct.shape[axis], axis),)


_roll.defvjp(_roll_fwd, _roll_bwd)


def _rope(x, cos, sin, lo, half):
    n = x.shape[-1]
    lane = lax.broadcasted_iota(jnp.int32, x.shape, x.ndim - 1) % HPAD - lo
    first = ((lane >= 0) & (lane < half)) | ((lane >= 2 * half) & (lane < 3 * half))
    rot = jnp.where(first, -_roll(x, n - half, x.ndim - 1), _roll(x, half, x.ndim - 1))
    return x * cos + rot * sin


def _heads(x, n, width=HEAD):
    return [x[:, h * width:(h + 1) * width] for h in range(n)]


def _pad_heads(hs):
    z = jnp.zeros_like(hs[0])
    return jnp.concatenate([t for h in hs for t in (h, z)], axis=-1)


def _unpad_heads(x, n):
    return jnp.concatenate([x[:, h * HPAD:h * HPAD + HEAD] for h in range(n)], axis=-1)


HALO = 8


def _conv_specs(tm, tc, c0, L):
    nh = tm // HALO
    last = L // HALO - 1
    return [pl.BlockSpec((tm, tc), lambda i, j: (i, c0 + j)),
            pl.BlockSpec((HALO, tc), lambda i, j: (jnp.maximum(i * nh - 1, 0), c0 + j)),
            pl.BlockSpec((HALO, tc), lambda i, j: (jnp.minimum((i + 1) * nh, last), c0 + j))]


def _shift3(x):
    n = x.shape[0]
    return _roll(x, 1, 0), _roll(x, n - 1, 0)


def conv_fwd(gfn, x, c0, ncol, tc, w, b, tco, out_dtype, name, tm=256):
    L = x.shape[0]
    tm = _tile(L, tm)
    ni = L // tm

    def body(x_ref, p_ref, n_ref, w_ref, b_ref, o_ref):
        i = pl.program_id(0)
        xv = x_ref[...].astype(f32)
        xp = jnp.where(i == 0, 0.0, p_ref[HALO - 1:HALO, :].astype(f32))
        xn = jnp.where(i == ni - 1, 0.0, n_ref[0:1, :].astype(f32))
        rid = lax.broadcasted_iota(jnp.int32, xv.shape, 0)
        dn, up = _shift3(xv)
        dn = jnp.where(rid == 0, xp, dn)
        up = jnp.where(rid == tm - 1, xn, up)
        c = w_ref[0:1, :] * dn + w_ref[1:2, :] * xv + w_ref[2:3, :] * up + b_ref[...]
        o_ref[...] = gfn(c).astype(o_ref.dtype)

    return pl.pallas_call(
        body, name=name, grid=(ni, ncol),
        in_specs=_conv_specs(tm, tc, c0, L) + [pl.BlockSpec((3, tc), lambda i, j: (0, j)), pl.BlockSpec((1, tc), lambda i, j: (0, j))],
        out_specs=pl.BlockSpec((tm, tco), lambda i, j: (i, j)), out_shape=jax.ShapeDtypeStruct((L, ncol * tco), out_dtype),
        compiler_params=_params(("parallel", "parallel")),
    )(x, x, x, w, b)


def conv_bwd(gfn, x, c0, ncol, tc, w, b, dys, tco, name, tm=256):
    L = x.shape[0]
    tm = _tile(L, tm)
    ni = L // tm
    nd = len(dys)

    def body(*refs):
        x_ref, xp_ref, xn_ref = refs[:3]
        d_refs = refs[3:3 + 3 * nd]
        w_ref, b_ref, dx_ref, dw_ref, db_ref = refs[3 + 3 * nd:]
        i = pl.program_id(1)
        first, lastb = i == 0, i == ni - 1

        def ext(m, p, n):
            return jnp.concatenate([jnp.where(first, 0.0, p[...].astype(f32)), m[...].astype(f32),
                                    jnp.where(lastb, 0.0, n[...].astype(f32))], axis=0)

        xe = ext(x_ref, xp_ref, xn_ref)
        de = ext(*d_refs[:3])
        for q in range(1, nd):
            de = de + ext(*d_refs[3 * q:3 * q + 3])
        x_dn, x_up = _shift3(xe)
        w0, w1, w2 = w_ref[0:1, :], w_ref[1:2, :], w_ref[2:3, :]
        ce = w0 * x_dn + w1 * xe + w2 * x_up + b_ref[...]
        _, vjp = jax.vjp(gfn, ce)
        (dce,) = vjp(de)
        dc_dn, dc_up = _shift3(dce)
        dx_ref[...] = (w0 * dc_up + w1 * dce + w2 * dc_dn)[HALO:HALO + tm, :]
        rid = lax.broadcasted_iota(jnp.int32, dce.shape, 0)
        dci = jnp.where((rid >= HALO) & (rid < HALO + tm), dce, 0.0)
        dw = jnp.concatenate([jnp.sum(dci * x_dn, axis=0, keepdims=True), jnp.sum(dci * xe, axis=0, keepdims=True),
                              jnp.sum(dci * x_up, axis=0, keepdims=True)], axis=0)
        db = jnp.sum(dci, axis=0, keepdims=True)

        @pl.when(first)
        def _():
            dw_ref[...] = dw
            db_ref[...] = db

        @pl.when(i > 0)
        def _():
            dw_ref[...] += dw
            db_ref[...] += db

    res = pl.pallas_call(
        body, name=name, grid=(ncol, ni),
        in_specs=[pl.BlockSpec(s.block_shape, (lambda j, i, f=s.index_map: f(i, j))) for s in _conv_specs(tm, tc, c0, L)]
        + [pl.BlockSpec(s.block_shape, (lambda j, i, f=s.index_map: f(i, j))) for s in _conv_specs(tm, tco, 0, L)] * nd
        + [pl.BlockSpec((3, tc), lambda j, i: (0, j)), pl.BlockSpec((1, tc), lambda j, i: (0, j))],
        out_specs=[pl.BlockSpec((tm, tc), lambda j, i: (i, j)), pl.BlockSpec((3, tc), lambda j, i: (0, j)),
                   pl.BlockSpec((1, tc), lambda j, i: (0, j))],
        out_shape=[jax.ShapeDtypeStruct((L, ncol * tc), f32), jax.ShapeDtypeStruct((3, ncol * tc), f32),
                   jax.ShapeDtypeStruct((1, ncol * tc), f32)],
        compiler_params=_params(("parallel", "arbitrary")),
    )(x, x, x, *[d for d in dys for _ in range(3)], w, b)
    return res


def flash_fwd(q, k, v, n_q, n_kv, scale, name, tq=512, tk=512):
    L = q.shape[0]
    tq, tk = _tile(L, tq), _tile(L, tk)
    rep = n_q // n_kv
    nkv = L // tk

    def body(q_ref, k_ref, v_ref, o_ref, lse_ref):
        qv = q_ref[...]

        def step(c, carry):
            m, l, acc = carry
            off = pl.multiple_of(c * tk, tk)
            s = _mm(qv, k_ref[pl.ds(off, tk), :], _NT) * scale
            m_new = jnp.maximum(m, jnp.max(s, axis=-1, keepdims=True))
            p = jnp.exp(s - m_new)
            alpha = jnp.exp(m - m_new)
            l = alpha * l + jnp.sum(p, axis=-1, keepdims=True)
            acc = alpha * acc + _mm(p, v_ref[pl.ds(off, tk), :], _NN)
            return m_new, l, acc

        m, l, acc = lax.fori_loop(0, nkv, step, (jnp.full((tq, 1), -jnp.inf, f32), jnp.zeros((tq, 1), f32),
                                                  jnp.zeros((tq, HPAD), f32)))
        o_ref[...] = acc / l
        lse_ref[0] = m + jnp.log(l)

    return pl.pallas_call(
        body, name=name, grid=(n_q, L // tq),
        in_specs=[pl.BlockSpec((tq, HPAD), lambda h, i: (i, h)), pl.BlockSpec((L, HPAD), lambda h, i: (0, h // rep)),
                  pl.BlockSpec((L, HPAD), lambda h, i: (0, h // rep))],
        out_specs=[pl.BlockSpec((tq, HPAD), lambda h, i: (i, h)), pl.BlockSpec((1, tq, 1), lambda h, i: (h, i, 0))],
        out_shape=[jax.ShapeDtypeStruct((L, n_q * HPAD), f32), jax.ShapeDtypeStruct((n_q, L, 1), f32)],
        compiler_params=_params(("parallel", "parallel")),
    )(q, k, v)


def flash_bwd(q, k, v, o, lse, do, n_q, n_kv, scale, name, tq=512, tk=512):
    L = q.shape[0]
    tq, tk = _tile(L, tq), _tile(L, tk)
    rep = n_q // n_kv

    def body(k_ref, v_ref, q_ref, do_ref, o_ref, lse_ref, dq_ref, dk_ref, dv_ref):
        j, i = pl.program_id(1), pl.program_id(2)
        kv, vv = k_ref[...], v_ref[...]
        rows = pl.ds(pl.multiple_of(i * tq, tq), tq)
        dk, dv = jnp.zeros((tk, HPAD), f32), jnp.zeros((tk, HPAD), f32)
        for r in range(rep):
            cols = slice(r * HPAD, (r + 1) * HPAD)
            qv = q_ref[:, cols]
            dov = do_ref[:, cols]
            delta = jnp.sum(dov * o_ref[:, cols], axis=-1, keepdims=True)
            p = jnp.exp(_mm(qv, kv, _NT) * scale - lse_ref[r])
            dv = dv + _mm(p, dov, _TN)
            ds = p * (_mm(dov, vv, _NT) - delta) * scale
            dk = dk + _mm(ds, qv, _TN)
            dq = _mm(ds, kv, _NN)

            @pl.when(j == 0)
            def _(dq=dq, cols=cols):
                dq_ref[rows, cols] = dq

            @pl.when(j > 0)
            def _(dq=dq, cols=cols):
                dq_ref[rows, cols] += dq

        @pl.when(i == 0)
        def _():
            dk_ref[...] = dk
            dv_ref[...] = dv

        @pl.when(i > 0)
        def _():
            dk_ref[...] += dk
            dv_ref[...] += dv

    w = rep * HPAD
    return pl.pallas_call(
        body, name=name, grid=(n_kv, L // tk, L // tq),
        in_specs=[pl.BlockSpec((tk, HPAD), lambda g, j, i: (j, g)), pl.BlockSpec((tk, HPAD), lambda g, j, i: (j, g)),
                  pl.BlockSpec((tq, w), lambda g, j, i: (i, g)), pl.BlockSpec((tq, w), lambda g, j, i: (i, g)),
                  pl.BlockSpec((tq, w), lambda g, j, i: (i, g)), pl.BlockSpec((rep, tq, 1), lambda g, j, i: (g, i, 0))],
        out_specs=[pl.BlockSpec((L, w), lambda g, j, i: (0, g)), pl.BlockSpec((tk, HPAD), lambda g, j, i: (j, g)),
                   pl.BlockSpec((tk, HPAD), lambda g, j, i: (j, g))],
        out_shape=[jax.ShapeDtypeStruct((L, n_q * HPAD), f32), jax.ShapeDtypeStruct((L, n_kv * HPAD), f32),
                   jax.ShapeDtypeStruct((L, n_kv * HPAD), f32)],
        compiler_params=_params(("parallel", "arbitrary", "arbitrary")),
    )(k, v, q, do, o, lse)


N_SCAN_HEADS = 4
STATE_ROWS = N_SCAN_HEADS * HEAD


def _tri(n, rev):
    i = lax.broadcasted_iota(jnp.int32, (n, n), 0)
    k = lax.broadcasted_iota(jnp.int32, (n, n), 1)
    return ((k >= i), (k > i)) if rev else ((k <= i), (k < i))


def _decay(acol, incl):
    n = acol.shape[0]
    m1 = jnp.broadcast_to(acol, (n, n))
    return jnp.exp(jnp.where(incl, m1 - m1.T, -jnp.inf))


def _ssd_chunk(S, xbc, win, a_log, dt_bias, rev):
    Q = xbc.shape[0]
    d = 1 if rev else 0
    incl, _ = _tri(Q, rev)
    dt = _softplus(win[:, 4 * d:4 * d + 4] + dt_bias[d:d + 1, :])
    a = dt * (-jnp.exp(a_log[d:d + 1, :]))
    acum = _mm(incl.astype(f32), a, _NN, HI)
    tot = jnp.sum(a, axis=0, keepdims=True)
    xs, Bm, Cm = xbc[:, :256], xbc[:, 256:384], xbc[:, 384:512]
    ys, Ss = [], []
    for h in range(N_SCAN_HEADS):
        g = h // 2
        Bg, Cg = Bm[:, g * HEAD:(g + 1) * HEAD], Cm[:, g * HEAD:(g + 1) * HEAD]
        xdt = xs[:, h * HEAD:(h + 1) * HEAD] * dt[:, h:h + 1]
        ac, th = acum[:, h:h + 1], tot[:, h:h + 1]
        y = _mm(_mm(Cg, Bg, _NT) * _decay(ac, incl), xdt, _NN)
        Sh = S[h * HEAD:(h + 1) * HEAD, :]
        ys.append(y + _mm(Cg, Sh, _NT) * jnp.exp(ac))
        Ss.append(Sh * jnp.exp(th) + _mm(xdt * jnp.exp(th - ac), Bg, _TN))
    return jnp.concatenate(Ss, axis=0), jnp.concatenate(ys, axis=1)


def _inv_unit_tri(Lm):
    n = Lm.shape[0]
    eye = (lax.broadcasted_iota(jnp.int32, (n, n), 0) == lax.broadcasted_iota(jnp.int32, (n, n), 1)).astype(f32)
    P = -Lm
    T = eye + P
    k = 1
    while 2 * k < n:
        P = _mm(P, P, _NN, HI)
        T = T + _mm(T, P, _NN, HI)
        k *= 2
    return T


def _dn_chunk(S, qkv, win, a_log, dt_bias, rev):
    Q = qkv.shape[0]
    d = 1 if rev else 0
    incl, strict = _tri(Q, rev)
    beta = jax.nn.sigmoid(win[:, 4 * d:4 * d + 4])
    gl = -jnp.exp(a_log[d:d + 1, :]) * _softplus(win[:, 8 + 4 * d:12 + 4 * d] + dt_bias[d:d + 1, :])
    G = _mm(incl.astype(f32), gl, _NN, HI)
    tot = jnp.sum(gl, axis=0, keepdims=True)
    os_, Ss = [], []
    for h in range(N_SCAN_HEADS):
        q = qkv[:, h * HEAD:(h + 1) * HEAD] * (HEAD ** -0.5)
        k = qkv[:, 256 + h * HEAD:256 + (h + 1) * HEAD]
        v = qkv[:, 512 + h * HEAD:512 + (h + 1) * HEAD]
        b, Gh, th = beta[:, h:h + 1], G[:, h:h + 1], tot[:, h:h + 1]
        dec = _decay(Gh, incl)
        kb = k * b
        T = _inv_unit_tri(jnp.where(strict, _mm(kb, k, _NT) * dec, 0.0))
        eG = jnp.exp(Gh)
        u = _mm(T, v * b, _NN)
        w = _mm(T, kb * eG, _NN)
        Sh = S[h * HEAD:(h + 1) * HEAD, :]
        vnew = u - _mm(w, Sh, _NN)
        os_.append(_mm(q * eG, Sh, _NN) + _mm(_mm(q, k, _NT) * dec, vnew, _NN))
        Ss.append(Sh * jnp.exp(th) + _mm(k * jnp.exp(th - Gh), vnew, _TN))
    return jnp.concatenate(Ss, axis=0), jnp.concatenate(os_, axis=1)


def scan_fwd(chunk_fn, Q, x, p, win_blk, a_log, dt_bias, name):
    L, W = x.shape
    nc = L // Q
    fidx, ridx = (lambda t: t), (lambda t: nc - 1 - t)

    def body(xf_ref, xr_ref, wf_ref, wr_ref, al_ref, db_ref, yf_ref, yr_ref, sf_ref, sr_ref, sf_scr, sr_scr):
        @pl.when(pl.program_id(0) == 0)
        def _():
            sf_scr[...] = jnp.zeros_like(sf_scr)
            sr_scr[...] = jnp.zeros_like(sr_scr)

        for rev, x_ref, w_ref, y_ref, sin_ref, s_scr in ((False, xf_ref, wf_ref, yf_ref, sf_ref, sf_scr),
                                                        (True, xr_ref, wr_ref, yr_ref, sr_ref, sr_scr)):
            S = s_scr[...]
            sin_ref[...] = S
            s_scr[...], y_ref[...] = chunk_fn(S, x_ref[...], w_ref[...], al_ref[...], db_ref[...], rev)

    row = lambda w, idx, b=0: pl.BlockSpec((Q, w), lambda t: (idx(t), b))
    small = pl.BlockSpec(a_log.shape, lambda t: (0, 0))
    state = lambda idx: pl.BlockSpec((STATE_ROWS, HEAD), lambda t: (idx(t), 0))
    return pl.pallas_call(
        body, name=name, grid=(nc,),
        in_specs=[row(W, fidx), row(W, ridx), row(HPAD, fidx, win_blk), row(HPAD, ridx, win_blk), small, small],
        out_specs=[row(STATE_ROWS, fidx), row(STATE_ROWS, ridx), state(fidx), state(ridx)],
        out_shape=[jax.ShapeDtypeStruct((L, STATE_ROWS), f32)] * 2 + [jax.ShapeDtypeStruct((nc * STATE_ROWS, HEAD), f32)] * 2,
        scratch_shapes=[pltpu.VMEM((STATE_ROWS, HEAD), f32)] * 2, compiler_params=_params(("arbitrary",)),
    )(x, x, p, p, a_log, dt_bias)


def scan_bwd(chunk_fn, Q, x, p, win_blk, a_log, dt_bias, s_fwd, s_rev, dy, name):
    L, W = x.shape
    nc = L // Q
    fidx, ridx = (lambda t: nc - 1 - t), (lambda t: t)

    def body(xf_ref, xr_ref, wf_ref, wr_ref, al_ref, db_ref, sf_ref, sr_ref, dyf_ref, dyr_ref,
             dxf_ref, dxr_ref, dwf_ref, dwr_ref, dal_ref, ddb_ref, dsf_scr, dsr_scr):
        @pl.when(pl.program_id(0) == 0)
        def _():
            dsf_scr[...] = jnp.zeros_like(dsf_scr)
            dsr_scr[...] = jnp.zeros_like(dsr_scr)
            dal_ref[...] = jnp.zeros_like(dal_ref)
            ddb_ref[...] = jnp.zeros_like(ddb_ref)

        dal_sum, ddb_sum = dal_ref[...], ddb_ref[...]
        for rev, x_ref, w_ref, sin_ref, dy_ref, dx_ref, dw_ref, ds_scr in (
                (False, xf_ref, wf_ref, sf_ref, dyf_ref, dxf_ref, dwf_ref, dsf_scr),
                (True, xr_ref, wr_ref, sr_ref, dyr_ref, dxr_ref, dwr_ref, dsr_scr)):
            _, vjp = jax.vjp(lambda S, xv, wv, al, db, rev=rev: chunk_fn(S, xv, wv, al, db, rev),
                             sin_ref[...], x_ref[...], w_ref[...], al_ref[...], db_ref[...])
            ds_scr[...], dx_ref[...], dw_ref[...], dal, ddb = vjp((ds_scr[...], dy_ref[...]))
            dal_sum, ddb_sum = dal_sum + dal, ddb_sum + ddb
        dal_ref[...] = dal_sum
        ddb_ref[...] = ddb_sum

    row = lambda w, idx, b=0: pl.BlockSpec((Q, w), lambda t: (idx(t), b))
    small = pl.BlockSpec(a_log.shape, lambda t: (0, 0))
    state = lambda idx: pl.BlockSpec((STATE_ROWS, HEAD), lambda t: (idx(t), 0))
    return pl.pallas_call(
        body, name=name, grid=(nc,),
        in_specs=[row(W, fidx), row(W, ridx), row(HPAD, fidx, win_blk), row(HPAD, ridx, win_blk), small, small,
                  state(fidx), state(ridx), row(STATE_ROWS, fidx), row(STATE_ROWS, ridx)],
        out_specs=[row(W, fidx), row(W, ridx), row(HPAD, fidx), row(HPAD, ridx), small, small],
        out_shape=[jax.ShapeDtypeStruct((L, W), f32)] * 2 + [jax.ShapeDtypeStruct((L, HPAD), f32)] * 2
        + [jax.ShapeDtypeStruct(a_log.shape, f32)] * 2,
        scratch_shapes=[pltpu.VMEM((STATE_ROWS, HEAD), f32)] * 2, compiler_params=_params(("arbitrary",)),
    )(x, x, p, p, a_log, dt_bias, s_fwd, s_rev, dy, dy)


def _cat(xs):
    return jnp.concatenate(xs, axis=-1)


def _a_prep(pa, cos, sin, qn, wuq, kvn, wk, wv):
    cq, ckv, kr = pa[:, :256], pa[:, 256:384], pa[:, 384:512]
    q = _rope(_mm(_rms(cq, qn, A_Q_LORA), wuq, _NN), _cat([cos] * 4), _cat([sin] * 4), A_NOPE, A_ROPE // 4)
    kvh = _rms(ckv, kvn)
    k = _mm(kvh, wk, _NN) + _cat([_rope(kr, cos, sin, A_NOPE, A_ROPE // 4)] * 4)
    return q, k, _mm(kvh, wv, _NN)


def _b_prep(pb, cos, sin, qn, kn):
    q = _pad_heads([_rms(h, qn) for h in _heads(pb[:, :256], 4)])
    k = _pad_heads([_rms(h, kn) for h in _heads(pb[:, 256:384], 2)])
    v = _pad_heads(_heads(pb[:, 384:512], 2))
    return _rope(q, _cat([cos] * 4), _cat([sin] * 4), 0, HEAD // 4), _rope(k, _cat([cos] * 2), _cat([sin] * 2), 0, HEAD // 4), v


def _c_act(c):
    return _silu(c)


def _d_act(c):
    s = _silu(c)
    return _cat([h * lax.rsqrt(jnp.sum(h * h, axis=-1, keepdims=True) + 1e-6) for h in _heads(s[:, :512], 8)] + [s[:, 512:]])


def _ffn_act(c):
    return _silu(c[:, :FF_BLK]) * c[:, FF_BLK:]


def _mix_post(oa, ob, yc_f, yc_r, xbc, cz, od_f, od_r, dz, a_out, b_out, dskip, c_out, d_out):
    o_a = _rms(_unpad_heads(oa, 4), a_out)
    o_b = _rms(_unpad_heads(ob, 4), b_out)
    skip = _cat([jnp.broadcast_to(dskip[:, h:h + 1], (1, HEAD)) for h in range(4)])
    o_c = _rms((yc_f + yc_r + xbc[:, :256] * skip) * _silu(cz), c_out)
    o_d = _cat([_rms(h, d_out) for h in _heads(od_f + od_r, 4)]) * _silu(dz)
    return _cat([o_a, o_b, o_c, o_d])


def _post_mix(x, y1, g_pm, g_pf):
    x1 = x + _rms(y1, g_pm)
    return x1, _rms(x1, g_pf)


def _post_ffn(x1, y2, g):
    return x1 + _rms(y2, g)


def _pre_bwd(x, g):
    return x, _rms(x, g)


def _layer_fwd(x, W, tabs, n):
    mx = MXU_DTYPE
    cos_a, sin_a, cos_b, sin_b = tabs
    full = lambda a: (a, a.shape[1], 0)
    s = {"x": x}
    (s["h"],) = rowwise(_rms, [full(x)], [W["g_pre"]], [(D_MODEL, mx)], n + "pre")
    p = s["p"] = matmul(s["h"], W["w_in"], "nn", f32, n + "in")
    s["qa"], s["ka"], s["va"] = rowwise(_a_prep, [(p, 512, SEG_A // 512), full(cos_a), full(sin_a)],
                                        [W["a_qn"], W["a_wuq"], W["a_kvn"], W["a_wk"], W["a_wv"]], [(512, mx)] * 3, n + "a_prep")
    s["qb"], s["kb"], s["vb"] = rowwise(_b_prep, [(p, 512, SEG_B // 512), full(cos_b), full(sin_b)], [W["b_qn"], W["b_kn"]],
                                        [(512, mx), (256, mx), (256, mx)], n + "b_prep")
    s["oa"], s["lsea"] = flash_fwd(s["qa"], s["ka"], s["va"], 4, 4, (A_NOPE + A_ROPE) ** -0.5, n + "a_attn")
    s["ob"], s["lseb"] = flash_fwd(s["qb"], s["kb"], s["vb"], 4, 2, HEAD ** -0.5, n + "b_attn")
    xbc = s["xbc"] = conv_fwd(_c_act, p, SEG_CX // 512, 1, 512, W["c_cw"], W["c_cb"], 512, f32, n + "c_conv")
    s["yc_f"], s["yc_r"], s["sc_f"], s["sc_r"] = scan_fwd(_ssd_chunk, SSD_CHUNK, xbc, p, SEG_CDT // HPAD, W["c_alog"], W["c_dtb"], n + "c_ssd")
    qkv = s["qkv"] = conv_fwd(_d_act, p, SEG_DX // 768, 1, 768, W["d_cw"], W["d_cb"], 768, f32, n + "d_conv")
    s["od_f"], s["od_r"], s["sd_f"], s["sd_r"] = scan_fwd(_dn_chunk, DN_CHUNK, qkv, p, SEG_DAB // HPAD, W["d_alog"], W["d_dtb"], n + "d_dn")
    (s["omix"],) = rowwise(_mix_post, _mix_rows(s), _mix_params(W), [(D_MODEL, mx)], n + "mix_post")
    s["y1"] = matmul(s["omix"], W["w_out"], "nn", f32, n + "out")
    s["x1"], s["h2"] = rowwise(_post_mix, [full(x), full(s["y1"])], [W["g_pm"], W["g_pf"]], [(D_MODEL, f32), (D_MODEL, mx)], n + "post_mix")
    s["u"] = matmul(s["h2"], W["f_win"], "nn", f32, n + "f_in")
    s["a"] = conv_fwd(_ffn_act, s["u"], 0, D_FF // FF_BLK, 2 * FF_BLK, W["f_cw"], W["f_cb"], FF_BLK, mx, n + "f_conv")
    s["y2"] = matmul(s["a"], W["f_wout"], "nn", f32, n + "f_out")
    (x2,) = rowwise(_post_ffn, [full(s["x1"]), full(s["y2"])], [W["g_po"]], [(D_MODEL, f32)], n + "post_ffn")
    return x2, s


def _mix_rows(s):
    p = s["p"]
    return [(s["oa"], 512, 0), (s["ob"], 512, 0), (s["yc_f"], 256, 0), (s["yc_r"], 256, 0), (s["xbc"], 512, 0),
            (p, 256, SEG_CZ // 256), (s["od_f"], 256, 0), (s["od_r"], 256, 0), (p, 256, SEG_DZ // 256)]


def _mix_params(W):
    return [W["a_out"], W["b_out"], W["c_dskip"], W["c_out"], W["d_out"]]


def _layer_bwd(s, dx2, W, tabs, n):
    cos_a, sin_a, cos_b, sin_b = tabs
    full = lambda a: (a, a.shape[1], 0)
    p, g = s["p"], {}
    (dx1, dy2), (g["g_po"],) = rowwise_vjp(_post_ffn, [full(s["x1"]), full(s["y2"])], [W["g_po"]], [full(dx2)], n + "post_ffn_b", [0, 1], [0])
    da = matmul(dy2, W["f_wout"], "nt", f32, n + "f_out_dx")
    g["f_wout"] = matmul(s["a"], dy2, "tn", f32, n + "f_out_dw")
    du, g["f_cw"], g["f_cb"] = conv_bwd(_ffn_act, s["u"], 0, D_FF // FF_BLK, 2 * FF_BLK, W["f_cw"], W["f_cb"], [da], FF_BLK, n + "f_conv_b")
    dh2 = matmul(du, W["f_win"], "nt", f32, n + "f_in_dx")
    g["f_win"] = matmul(s["h2"], du, "tn", f32, n + "f_in_dw")
    (dx, dy1), (g["g_pm"], g["g_pf"]) = rowwise_vjp(_post_mix, [full(s["x"]), full(s["y1"])], [W["g_pm"], W["g_pf"]],
                                                   [full(dx1), full(dh2)], n + "post_mix_b", [0, 1], [0, 1])
    domix = matmul(dy1, W["w_out"], "nt", f32, n + "out_dx")
    g["w_out"] = matmul(s["omix"], dy1, "tn", f32, n + "out_dw")
    (doa, dob, dyc, dxbc_skip, dcz, dod, ddz), (g["a_out"], g["b_out"], g["c_dskip"], g["c_out"], g["d_out"]) = rowwise_vjp(
        _mix_post, _mix_rows(s), _mix_params(W), [full(domix)], n + "mix_post_b", [0, 1, 2, 4, 5, 6, 8], list(range(5)))
    dqkv_f, dqkv_r, ddab_f, ddab_r, g["d_alog"], g["d_dtb"] = scan_bwd(
        _dn_chunk, DN_CHUNK, s["qkv"], p, SEG_DAB // HPAD, W["d_alog"], W["d_dtb"], s["sd_f"], s["sd_r"], dod, n + "d_dn_b")
    ddx, g["d_cw"], _ = conv_bwd(_d_act, p, SEG_DX // 768, 1, 768, W["d_cw"], W["d_cb"], [dqkv_f, dqkv_r], 768, n + "d_conv_b")
    dxbc_f, dxbc_r, dcdt_f, dcdt_r, g["c_alog"], g["c_dtb"] = scan_bwd(
        _ssd_chunk, SSD_CHUNK, s["xbc"], p, SEG_CDT // HPAD, W["c_alog"], W["c_dtb"], s["sc_f"], s["sc_r"], dyc, n + "c_ssd_b")
    dcx, g["c_cw"], g["c_cb"] = conv_bwd(_c_act, p, SEG_CX // 512, 1, 512, W["c_cw"], W["c_cb"], [dxbc_f, dxbc_r, dxbc_skip], 512,
                                         n + "c_conv_b")
    (dsmall,) = rowwise(lambda a, b, c, d: _cat([a + b, c + d]), [full(dcdt_f), full(dcdt_r), full(ddab_f), full(ddab_r)], [],
                        [(2 * HPAD, f32)], n + "dwin_sum")
    dqa, dka, dva = flash_bwd(s["qa"], s["ka"], s["va"], s["oa"], s["lsea"], doa, 4, 4, (A_NOPE + A_ROPE) ** -0.5, n + "a_attn_b")
    dqb, dkb, dvb = flash_bwd(s["qb"], s["kb"], s["vb"], s["ob"], s["lseb"], dob, 4, 2, HEAD ** -0.5, n + "b_attn_b")
    (dpa,), (g["a_qn"], g["a_wuq"], g["a_kvn"], g["a_wk"], g["a_wv"]) = rowwise_vjp(
        _a_prep, [(p, 512, SEG_A // 512), full(cos_a), full(sin_a)], [W["a_qn"], W["a_wuq"], W["a_kvn"], W["a_wk"], W["a_wv"]],
        [full(dqa), full(dka), full(dva)], n + "a_prep_b", [0], list(range(5)))
    (dpb,), (g["b_qn"], g["b_kn"]) = rowwise_vjp(_b_prep, [(p, 512, SEG_B // 512), full(cos_b), full(sin_b)], [W["b_qn"], W["b_kn"]],
                                               [full(dqb), full(dkb), full(dvb)], n + "b_prep_b", [0], [0, 1])
    dp = jnp.concatenate([dpa, dpb, dcx, ddx, dcz, ddz, dsmall], axis=1)
    dh = matmul(dp, W["w_in"], "nt", f32, n + "in_dx")
    g["w_in"] = matmul(s["h"], dp, "tn", f32, n + "in_dw")
    (dx0,), (g["g_pre"],) = rowwise_vjp(_pre_bwd, [full(s["x"])], [W["g_pre"]], [full(dx), full(dh)], n + "pre_b", [0], [0])
    return dx0, g


def loss_and_grad(y, target, name):
    L, D = y.shape
    tm = _tile(L, 512)

    def body(y_ref, t_ref, loss_ref, dy_ref):
        e = y_ref[...] - t_ref[...]
        dy_ref[...] = e * (1.0 / D)
        part = 0.5 * jnp.sum(jnp.sum(e * e, axis=1, keepdims=True) * (1.0 / D), axis=0, keepdims=True)

        @pl.when(pl.program_id(0) == 0)
        def _():
            loss_ref[...] = part

        @pl.when(pl.program_id(0) > 0)
        def _():
            loss_ref[...] += part

    row = pl.BlockSpec((tm, D), lambda i: (i, 0))
    return pl.pallas_call(
        body, name=name, grid=(L // tm,), in_specs=[row, row], out_specs=[pl.BlockSpec((1, 1), lambda i: (0, 0)), row],
        out_shape=[jax.ShapeDtypeStruct((1, 1), f32), jax.ShapeDtypeStruct((L, D), f32)], compiler_params=_params(("arbitrary",)),
    )(y, target)


_IN_SEGS = [(0, 192), (256, 128), (384 + A_NOPE, 32), (512, 256), (768, 128), (896, 128), (SEG_CZ, 256), (SEG_CX, 512),
            (SEG_CDT, 8), (SEG_DX, 768), (SEG_DZ, 256), (SEG_DAB, 16)]


def _pad_in(w):
    src, pieces = 0, {}
    for off, wd in _IN_SEGS:
        pieces[off] = w[..., src:src + wd]
        src += wd
    out, pos = [], 0
    for off in sorted(pieces):
        if off > pos:
            out.append(jnp.zeros(w.shape[:-1] + (off - pos,), w.dtype))
        out.append(pieces[off])
        pos = off + pieces[off].shape[-1]
    out.append(jnp.zeros(w.shape[:-1] + (PIN - pos,), w.dtype))
    return jnp.concatenate(out, axis=-1)


def _unpad_in(wp):
    return jnp.concatenate([wp[..., off:off + wd] for off, wd in _IN_SEGS], axis=-1)


def _pad_last(a, n):
    return jnp.pad(a, [(0, 0)] * (a.ndim - 1) + [(0, n - a.shape[-1])])


def _ff_interleave(w):
    lead = w.shape[:-1]
    return jnp.stack([w[..., :D_FF].reshape(lead + (D_FF // FF_BLK, FF_BLK)), w[..., D_FF:].reshape(lead + (D_FF // FF_BLK, FF_BLK))],
                     axis=-2).reshape(lead + (2 * D_FF,))


def _ff_deinterleave(w):
    lead = w.shape[:-1]
    t = w.reshape(lead + (D_FF // FF_BLK, 2, FF_BLK))
    return jnp.concatenate([t[..., 0, :].reshape(lead + (D_FF,)), t[..., 1, :].reshape(lead + (D_FF,))], axis=-1)


def _row(v):
    return v.reshape(1, -1)


def _kernel_weights(P, l, wdt):
    uq = P["a_w_uq"][l].reshape(A_Q_LORA, 4, A_NOPE + A_ROPE)
    ukv = P["a_w_ukv"][l].reshape(A_KV_LORA, 4, 2 * HEAD)
    z = jnp.zeros((A_KV_LORA, 4, HEAD), ukv.dtype)
    return {
        "g_pre": _row(P["pre_mix_norm"][l]), "w_in": _pad_in(P["w_in"][l]).astype(wdt),
        "a_qn": _pad_last(_row(P["a_q_norm"][l]), 256),
        "a_wuq": jnp.pad(_pad_last(uq, HPAD).reshape(A_Q_LORA, 4 * HPAD), ((0, 256 - A_Q_LORA), (0, 0))).astype(wdt),
        "a_kvn": _row(P["a_kv_norm"][l]),
        "a_wk": jnp.concatenate([ukv[..., :HEAD], z], axis=-1).reshape(A_KV_LORA, 4 * HPAD).astype(wdt),
        "a_wv": jnp.concatenate([ukv[..., HEAD:], z], axis=-1).reshape(A_KV_LORA, 4 * HPAD).astype(wdt),
        "a_out": _row(P["a_out_norm"][l]), "b_qn": _row(P["b_q_norm"][l]), "b_kn": _row(P["b_k_norm"][l]), "b_out": _row(P["b_out_norm"][l]),
        "c_cw": P["c_conv_w"][l], "c_cb": _row(P["c_conv_b"][l]), "c_alog": P["c_a_log"][l], "c_dtb": P["c_dt_bias"][l],
        "c_dskip": _row(P["c_d_skip"][l]), "c_out": _row(P["c_out_norm"][l]),
        "d_cw": P["d_conv_w"][l], "d_cb": jnp.zeros((1, 768), f32), "d_alog": P["d_a_log"][l], "d_dtb": P["d_dt_bias"][l],
        "d_out": _row(P["d_out_norm"][l]), "w_out": P["w_out"][l].astype(wdt), "g_pm": _row(P["post_mix_norm"][l]),
        "g_pf": _row(P["pre_ffn_norm"][l]), "f_win": _ff_interleave(P["f_w_in"][l]).astype(wdt),
        "f_cw": _ff_interleave(P["f_conv_w"][l]), "f_cb": _row(_ff_interleave(P["f_conv_b"][l])),
        "f_wout": P["f_w_out"][l].astype(wdt), "g_po": _row(P["post_ffn_norm"][l]),
    }


def _reference_grads(g):
    uq = g["a_wuq"][:A_Q_LORA].reshape(A_Q_LORA, 4, HPAD)[..., :A_NOPE + A_ROPE].reshape(A_Q_LORA, 4 * (A_NOPE + A_ROPE))
    wk = g["a_wk"].reshape(A_KV_LORA, 4, HPAD)[..., :HEAD]
    wv = g["a_wv"].reshape(A_KV_LORA, 4, HPAD)[..., :HEAD]
    return {
        "pre_mix_norm": g["g_pre"][0], "w_in": _unpad_in(g["w_in"]), "a_q_norm": g["a_qn"][0, :A_Q_LORA], "a_w_uq": uq,
        "a_kv_norm": g["a_kvn"][0], "a_w_ukv": jnp.concatenate([wk, wv], axis=-1).reshape(A_KV_LORA, 8 * HEAD),
        "a_out_norm": g["a_out"][0], "b_q_norm": g["b_qn"][0], "b_k_norm": g["b_kn"][0], "b_out_norm": g["b_out"][0],
        "c_conv_w": g["c_cw"], "c_conv_b": g["c_cb"][0], "c_a_log": g["c_alog"], "c_dt_bias": g["c_dtb"], "c_d_skip": g["c_dskip"][0],
        "c_out_norm": g["c_out"][0], "d_conv_w": g["d_cw"], "d_a_log": g["d_alog"], "d_dt_bias": g["d_dtb"], "d_out_norm": g["d_out"][0],
        "w_out": g["w_out"], "post_mix_norm": g["g_pm"][0], "pre_ffn_norm": g["g_pf"][0], "f_w_in": _ff_deinterleave(g["f_win"]),
        "f_conv_w": _ff_deinterleave(g["f_cw"]), "f_conv_b": _ff_deinterleave(g["f_cb"][0]), "f_w_out": g["f_wout"],
        "post_ffn_norm": g["g_po"][0],
    }


def _rope_tables(L):
    def tables(rot):
        rows = L // GRID_W
        row = jnp.repeat(jnp.arange(rows), GRID_W).astype(f32)
        col = jnp.tile(jnp.arange(GRID_W), rows).astype(f32)
        sec = rot // 2
        inv = ROPE_BASE ** (-jnp.arange(0, sec, 2, dtype=f32) / sec)
        ang = jnp.concatenate([row[:, None] * inv] * 2 + [col[:, None] * inv] * 2, axis=-1)
        return jnp.cos(ang), jnp.sin(ang)

    ca, sa = tables(A_ROPE)
    cb, sb = tables(HEAD)
    one, zero = jnp.ones, jnp.zeros
    return (jnp.concatenate([one((L, A_NOPE), f32), ca, one((L, HPAD - A_NOPE - A_ROPE), f32)], axis=1),
            jnp.concatenate([zero((L, A_NOPE), f32), sa, zero((L, HPAD - A_NOPE - A_ROPE), f32)], axis=1),
            jnp.concatenate([cb, one((L, HPAD - HEAD), f32)], axis=1), jnp.concatenate([sb, zero((L, HPAD - HEAD), f32)], axis=1))


def local_step(x, target, P, wdt):
    tabs = _rope_tables(x.shape[0])
    Ws = [_kernel_weights(P, l, wdt) for l in range(DEPTH)]
    saved = []
    for l in range(DEPTH):
        x, s = _layer_fwd(x, Ws[l], tabs, f"l{l}_")
        saved.append(s)
    loss, dx = loss_and_grad(x, target, "loss")
    grads = [None] * DEPTH
    for l in reversed(range(DEPTH)):
        dx, g = _layer_bwd(saved[l], dx, Ws[l], tabs, f"l{l}_")
        grads[l] = _reference_grads(g)
    return loss, dx, grads


def exchange(items, name):
    n = len(items)
    modes = [g for _, g in items]

    def body(*refs):
        src_refs, out_refs = refs[:n], refs[n:2 * n]
        send_sems, recv_sems, local_sems = refs[2 * n:]
        x, y, c = lax.axis_index("x"), lax.axis_index("y"), lax.axis_index("c")
        me = 4 * x + 2 * y + c

        def copy(a, k, arriving):
            px, py, pc = (1 - x if k & 4 else x), (1 - y if k & 2 else y), (1 - c if k & 1 else c)
            pid = 4 * px + 2 * py + pc
            sem = a * (N_DEV - 1) + k - 1
            return pltpu.make_async_remote_copy(
                src_ref=src_refs[a] if modes[a] else src_refs[a].at[pid], dst_ref=out_refs[a].at[pid if arriving else me],
                send_sem=send_sems.at[sem], recv_sem=recv_sems.at[sem], device_id=(px, py, pc), device_id_type=pl.DeviceIdType.MESH)

        local = [pltpu.make_async_copy(src_refs[a] if modes[a] else src_refs[a].at[me], out_refs[a].at[me], local_sems.at[a])
                 for a in range(n)]
        pairs = [(a, k) for k in range(1, N_DEV) for a in range(n)]
        for a, k in pairs:
            copy(a, k, False).start()
        for cp in local:
            cp.start()
        for a, k in pairs:
            copy(a, k, False).wait_send()
        for a, k in pairs:
            copy(a, k, True).wait_recv()
        for cp in local:
            cp.wait()

    hbm = pl.BlockSpec(memory_space=pl.ANY)
    res = pl.pallas_call(
        body, name=name, in_specs=[hbm] * n, out_specs=[hbm] * n,
        out_shape=[jax.ShapeDtypeStruct((N_DEV,) + tuple(s.shape if g else s.shape[1:]), s.dtype) for s, g in items],
        scratch_shapes=[pltpu.SemaphoreType.DMA((n * (N_DEV - 1),)), pltpu.SemaphoreType.DMA((n * (N_DEV - 1),)),
                        pltpu.SemaphoreType.DMA((n,))],
    )(*[s for s, _ in items])
    return list(res)


LANES = 1024


def adamw(gstack, w, m, v, name):
    S, D, R, C = gstack.shape
    tr = R if R <= 512 else _tile(R, 256)

    def body(g_ref, w_ref, m_ref, v_ref, go_ref, d_ref, mo_ref, vo_ref):
        g = g_ref[0]
        for sl in range(1, S):
            g = g + g_ref[sl]
        m_new = ADAM_B1 * m_ref[...] + (1.0 - ADAM_B1) * g
        v_new = ADAM_B2 * v_ref[...] + (1.0 - ADAM_B2) * jnp.square(g)
        m_hat = m_new / (1.0 - ADAM_B1 ** ADAM_STEP)
        v_hat = v_new / (1.0 - ADAM_B2 ** ADAM_STEP)
        go_ref[...] = g
        d_ref[...] = -ADAM_LR * (m_hat / (jnp.sqrt(v_hat) + ADAM_EPS) + ADAM_WD * w_ref[...])
        mo_ref[...] = m_new
        vo_ref[...] = v_new

    row = pl.BlockSpec((1, tr, C), lambda l, i: (l, i, 0))
    return pl.pallas_call(
        body, name=name, grid=(D, R // tr), in_specs=[pl.BlockSpec((S, 1, tr, C), lambda l, i: (0, l, i, 0)), row, row, row],
        out_specs=[row] * 4, out_shape=[jax.ShapeDtypeStruct((D, R, C), f32)] * 4, compiler_params=_params(("parallel", "parallel")),
    )(gstack, w, m, v)


def sum_slots(gstack, name):
    S, R, _ = gstack.shape

    def body(g_ref, o_ref):
        g = g_ref[0]
        for sl in range(1, S):
            g = g + g_ref[sl]
        o_ref[...] = g

    return pl.pallas_call(body, name=name, out_shape=jax.ShapeDtypeStruct((R, LANES), f32), compiler_params=_params())(gstack)


def _pack(parts, rows, dtype=f32):
    flat = jnp.concatenate([q.reshape(-1).astype(dtype) for q in parts])
    return jnp.pad(flat, (0, rows * LANES - flat.shape[0])).reshape(rows, LANES)


def _unpack(buf, shapes):
    lead = buf.shape[:-2]
    flat = buf.reshape(lead + (-1,))
    out, off = [], 0
    for shp in shapes:
        n = math.prod(shp)
        out.append(flat[..., off:off + n].reshape(lead + tuple(shp)))
        off += n
    return out


_WEIGHTS = ["pre_mix_norm", "w_in", "a_q_norm", "a_w_uq", "a_kv_norm", "a_w_ukv", "a_out_norm", "b_q_norm", "b_k_norm", "b_out_norm",
            "c_conv_w", "c_conv_b", "c_a_log", "c_dt_bias", "c_d_skip", "c_out_norm", "d_conv_w", "d_a_log", "d_dt_bias", "d_out_norm",
            "w_out", "post_mix_norm", "pre_ffn_norm", "f_w_in", "f_conv_w", "f_conv_b", "f_w_out", "post_ffn_norm"]
_BIG = {"w_in": 1, "a_w_uq": 1, "a_w_ukv": 1, "w_out": 0, "f_w_in": 1, "f_w_out": 0}
_CONV = ["c_conv_w", "d_conv_w", "f_conv_w"]
_REP = [n for n in _WEIGHTS if n not in _BIG and n not in _CONV]
SMALLG_ROWS, SMALLW_ROWS = 64, 32


def _join(blocks, axis):
    if axis == 0:
        return blocks.reshape(-1, blocks.shape[2])
    return blocks.transpose(1, 0, 2).reshape(blocks.shape[1], -1)


def _split(full, axis):
    if axis == 0:
        return full.reshape(N_DEV, -1, full.shape[1])
    return full.reshape(full.shape[0], N_DEV, -1).transpose(1, 0, 2)


def kernel(x, pre_mix_norm, w_in, a_q_norm, a_w_uq, a_kv_norm, a_w_ukv, a_out_norm, b_q_norm, b_k_norm, b_out_norm, c_conv_w, c_conv_b, c_a_log, c_dt_bias, c_d_skip, c_out_norm, d_conv_w, d_a_log, d_dt_bias, d_out_norm, w_out, post_mix_norm, pre_ffn_norm, f_w_in, f_conv_w, f_conv_b, f_w_out, post_ffn_norm, loss_target, m_pre_mix_norm, m_w_in, m_a_q_norm, m_a_w_uq, m_a_kv_norm, m_a_w_ukv, m_a_out_norm, m_b_q_norm, m_b_k_norm, m_b_out_norm, m_c_conv_w, m_c_conv_b, m_c_a_log, m_c_dt_bias, m_c_d_skip, m_c_out_norm, m_d_conv_w, m_d_a_log, m_d_dt_bias, m_d_out_norm, m_w_out, m_post_mix_norm, m_pre_ffn_norm, m_f_w_in, m_f_conv_w, m_f_conv_b, m_f_w_out, m_post_ffn_norm, v_pre_mix_norm, v_w_in, v_a_q_norm, v_a_w_uq, v_a_kv_norm, v_a_w_ukv, v_a_out_norm, v_b_q_norm, v_b_k_norm, v_b_out_norm, v_c_conv_w, v_c_conv_b, v_c_a_log, v_c_dt_bias, v_c_d_skip, v_c_out_norm, v_d_conv_w, v_d_a_log, v_d_dt_bias, v_d_out_norm, v_w_out, v_post_mix_norm, v_pre_ffn_norm, v_f_w_in, v_f_conv_w, v_f_conv_b, v_f_w_out, v_post_ffn_norm):
    given = dict(locals())
    w = {n: given[n] for n in _WEIGHTS}
    mom = {n: given["m_" + n] for n in _WEIGHTS}
    var = {n: given["v_" + n] for n in _WEIGHTS}
    me = 4 * lax.axis_index("x") + 2 * lax.axis_index("y") + lax.axis_index("c")
    layered = lambda names: [(l, n) for l in range(DEPTH) for n in names]

    gathered = exchange([(w[n].astype(MXU_DTYPE), True) for n in _BIG] + [(w[n], True) for n in _CONV], "gather_weights")
    P = {n: [w[n][l] for l in range(DEPTH)] for n in _REP}
    for n, got in zip(list(_BIG) + _CONV, gathered):
        P[n] = [_join(got[:, l], _BIG.get(n, 1)) for l in range(DEPTH)]

    loss, dx, grads = local_step(x[0], loss_target[0], P, MXU_DTYPE)
    loss = lax.psum(loss[0, 0], ("x", "y", "c"))

    small = _REP + _CONV
    full_shapes = [grads[l][n].shape for l, n in layered(small)]
    sends = [(jnp.stack([_split(grads[l][n], _BIG[n]) for l in range(DEPTH)], axis=1), False) for n in _BIG]
    *recv, gsmall = exchange(sends + [(_pack([grads[l][n] for l, n in layered(small)], SMALLG_ROWS), True)], "exchange_grads")

    out = {}
    for n, got in zip(_BIG, recv):
        out[n] = adamw(got, w[n], mom[n], var[n], "adamw_" + n)
    gsum = dict(zip(layered(small), _unpack(sum_slots(gsmall, "sum_small_grads"), full_shapes)))
    for l, n in layered(_CONV):
        cols = w[n].shape[2]
        gsum[(l, n)] = lax.dynamic_slice_in_dim(gsum[(l, n)], me * cols, cols, axis=1)
    held_shapes = [w[n].shape[1:] for _, n in layered(small)]
    pack_small = lambda d: _pack([d[n][l] for l, n in layered(small)], SMALLW_ROWS)[None]
    res_small = adamw(_pack([gsum[k] for k in layered(small)], SMALLW_ROWS)[None, None], pack_small(w), pack_small(mom),
                      pack_small(var), "adamw_small")
    per_kind = [dict(zip(layered(small), _unpack(buf[0], held_shapes))) for buf in res_small]
    for n in small:
        out[n] = [jnp.stack([per_kind[kind][(l, n)] for l in range(DEPTH)]) for kind in range(4)]
    return (loss, dx[None], *[out[n][kind] for kind in range(4) for n in _WEIGHTS])
```

```python
import functools
import math

import jax
import jax.numpy as jnp
from jax import lax
from jax.experimental import pallas as pl
from jax.experimental.pallas import tpu as pltpu

f32 = jnp.float32
MXU_DTYPE = jnp.bfloat16
HI = lax.Precision.HIGHEST

D_MODEL = 1024
DEPTH = 2
GRID_W = 64
ROPE_BASE = 10000.0
EPS = 1e-6
A_Q_LORA, A_KV_LORA, A_ROPE, A_NOPE = 192, 128, 32, 64
SSD_CHUNK = 128
DN_CHUNK = 64
HEAD = 64
HPAD = 128
D_FF = 2816
FF_BLK = 256
N_DEV = 8
ADAM_LR, ADAM_B1, ADAM_B2, ADAM_EPS, ADAM_WD, ADAM_STEP = 0.001, 0.9, 0.999, 1e-08, 0.01, 10

V7X_VMEM_BYTES = 64 * 2 ** 20
VMEM_LIMIT = (V7X_VMEM_BYTES * 3) // 4

PIN = 3072
SEG_A, SEG_B, SEG_CX, SEG_DX, SEG_CZ, SEG_DZ, SEG_CDT, SEG_DAB = 0, 512, 1024, 1536, 2304, 2560, 2816, 2944


def _tile(n, pref):
    for t in (512, 256, 128, 64, 32, 16, 8):
        if t <= pref and n % t == 0:
            return t
    raise ValueError(f"no tile for {n}")


LANE = 128


def _tile_div(n, pref):
    if n <= pref:
        return n
    return max(d for d in range(LANE, pref + 1, LANE) if n % d == 0)


def _params(sem=None):
    return pltpu.CompilerParams(vmem_limit_bytes=VMEM_LIMIT, dimension_semantics=sem)


def _tup(r):
    return tuple(r) if isinstance(r, (tuple, list)) else (r,)


def matmul(a, b, form, out_dtype, name):
    if form == "nn":
        (M, K), N = a.shape, b.shape[1]
    elif form == "nt":
        (M, K), N = a.shape, b.shape[0]
    else:
        (K, M), N = a.shape, b.shape[1]
    if form == "tn":
        tm, tn, tk = _tile_div(M, 1024), _tile_div(N, 1408), _tile_div(K, 1024)
    else:
        tm, tn, tk = _tile_div(M, 512), _tile_div(N, 1536), _tile_div(K, 3072)
    nk = K // tk
    dims = {"nn": ((1,), (0,)), "nt": ((1,), (1,)), "tn": ((0,), (0,))}[form]

    def body(a_ref, b_ref, o_ref, *acc):
        part = lax.dot_general(a_ref[...].astype(MXU_DTYPE), b_ref[...].astype(MXU_DTYPE), (dims, ((), ())),
                               preferred_element_type=f32)
        if nk == 1:
            o_ref[...] = part.astype(out_dtype)
            return
        (acc_ref,) = acc
        k = pl.program_id(2)

        @pl.when(k == 0)
        def _():
            acc_ref[...] = part

        @pl.when((k > 0) & (k < nk - 1))
        def _():
            acc_ref[...] += part

        @pl.when(k == nk - 1)
        def _():
            o_ref[...] = (acc_ref[...] + part).astype(out_dtype)

    a_spec = pl.BlockSpec((tk, tm), lambda i, j, k: (k, i)) if form == "tn" else pl.BlockSpec((tm, tk), lambda i, j, k: (i, k))
    b_spec = pl.BlockSpec((tn, tk), lambda i, j, k: (j, k)) if form == "nt" else pl.BlockSpec((tk, tn), lambda i, j, k: (k, j))
    return pl.pallas_call(
        body, name=name, grid=(M // tm, N // tn, nk), in_specs=[a_spec, b_spec],
        out_specs=pl.BlockSpec((tm, tn), lambda i, j, k: (i, j)), out_shape=jax.ShapeDtypeStruct((M, N), out_dtype),
        scratch_shapes=[pltpu.VMEM((tm, tn), f32)] if nk > 1 else [], compiler_params=_params(("parallel", "parallel", "arbitrary")),
    )(a, b)


def _row_specs(rows, tm):
    return [pl.BlockSpec((tm, w), lambda i, b=b: (i, b)) for (_, w, b) in rows]


def _full_specs(params):
    return [pl.BlockSpec(p.shape, lambda i: (0, 0)) for p in params]


def rowwise(fn, rows, params, outs, name, tm=256):
    L = rows[0][0].shape[0]
    tm = _tile(L, tm)
    n_in = len(rows) + len(params)

    def body(*refs):
        res = _tup(fn(*[r[...].astype(f32) for r in refs[:n_in]]))
        for o_ref, r in zip(refs[n_in:], res, strict=True):
            o_ref[...] = r.astype(o_ref.dtype)

    res = pl.pallas_call(
        body, name=name, grid=(L // tm,), in_specs=_row_specs(rows, tm) + _full_specs(params),
        out_specs=[pl.BlockSpec((tm, w), lambda i: (i, 0)) for (w, _) in outs],
        out_shape=[jax.ShapeDtypeStruct((L, w), dt) for (w, dt) in outs], compiler_params=_params(("parallel",)),
    )(*[r[0] for r in rows], *params)
    return list(res)


def rowwise_vjp(fn, rows, params, cts, name, row_grads, param_grads, tm=256):
    L = rows[0][0].shape[0]
    tm = _tile(L, tm)
    nr, npar, nct = len(rows), len(params), len(cts)

    def body(*refs):
        i = pl.program_id(0)
        rv = [r[...].astype(f32) for r in refs[:nr]]
        pv = [r[...].astype(f32) for r in refs[nr:nr + npar]]
        cv = tuple(r[...].astype(f32) for r in refs[nr + npar:nr + npar + nct])
        out_refs = refs[nr + npar + nct:]

        def g(*diff):
            rr, pp = list(rv), list(pv)
            for k, v in zip(row_grads, diff[:len(row_grads)]):
                rr[k] = v
            for k, v in zip(param_grads, diff[len(row_grads):]):
                pp[k] = v
            return _tup(fn(*rr, *pp))

        _, vjp = jax.vjp(g, *[rv[k] for k in row_grads], *[pv[k] for k in param_grads])
        grads = vjp(cv)
        for o_ref, gval in zip(out_refs[:len(row_grads)], grads[:len(row_grads)]):
            o_ref[...] = gval
        for o_ref, gval in zip(out_refs[len(row_grads):], grads[len(row_grads):]):
            @pl.when(i == 0)
            def _(o_ref=o_ref, gval=gval):
                o_ref[...] = gval

            @pl.when(i > 0)
            def _(o_ref=o_ref, gval=gval):
                o_ref[...] += gval

    out_specs = [pl.BlockSpec((tm, rows[k][1]), lambda i: (i, 0)) for k in row_grads] + \
                [pl.BlockSpec(params[k].shape, lambda i: (0, 0)) for k in param_grads]
    out_shape = [jax.ShapeDtypeStruct((L, rows[k][1]), f32) for k in row_grads] + \
                [jax.ShapeDtypeStruct(params[k].shape, f32) for k in param_grads]
    res = pl.pallas_call(
        body, name=name, grid=(L // tm,), in_specs=_row_specs(rows, tm) + _full_specs(params) + _row_specs(cts, tm),
        out_specs=out_specs, out_shape=out_shape, compiler_params=_params(("arbitrary",)),
    )(*[r[0] for r in rows], *params, *[c[0] for c in cts])
    res = list(res)
    return res[:len(row_grads)], res[len(row_grads):]


def _rms(x, w, n=None):
    n = x.shape[-1] if n is None else n
    return x * lax.rsqrt(jnp.sum(x * x, axis=-1, keepdims=True) * (1.0 / n) + EPS) * w


def _silu(x):
    return x * jax.nn.sigmoid(x)


def _softplus(x):
    return jnp.maximum(x, 0.0) + jnp.log1p(jnp.exp(-jnp.abs(x)))


def _mm(a, b, dims, precision=None):
    if precision is None:
        a, b = a.astype(MXU_DTYPE), b.astype(MXU_DTYPE)
    return lax.dot_general(a, b, (dims, ((), ())), precision=precision, preferred_element_type=f32)


_NN, _NT, _TN = ((1,), (0,)), ((1,), (1,)), ((0,), (0,))


@functools.partial(jax.custom_vjp, nondiff_argnums=(1, 2))
def _roll(x, shift, axis):
    return pltpu.roll(x, shift, axis)


def _roll_fwd(x, shift, axis):
    return pltpu.roll(x, shift, axis), None


def _roll_bwd(shift, axis, _, ct):
    return (pltpu.roll(ct, (ct.shape[axis] - shift) % ct.shape[axis], axis),)


_roll.defvjp(_roll_fwd, _roll_bwd)


def _rope(x, cos, sin, lo, half):
    n = x.shape[-1]
    lane = lax.broadcasted_iota(jnp.int32, x.shape, x.ndim - 1) % HPAD - lo
    first = ((lane >= 0) & (lane < half)) | ((lane >= 2 * half) & (lane < 3 * half))
    rot = jnp.where(first, -_roll(x, n - half, x.ndim - 1), _roll(x, half, x.ndim - 1))
    return x * cos + rot * sin


def _heads(x, n, width=HEAD):
    return [x[:, h * width:(h + 1) * width] for h in range(n)]


def _pad_heads(hs):
    z = jnp.zeros_like(hs[0])
    return jnp.concatenate([t for h in hs for t in (h, z)], axis=-1)


def _unpad_heads(x, n):
    return jnp.concatenate([x[:, h * HPAD:h * HPAD + HEAD] for h in range(n)], axis=-1)


HALO = 8


def _conv_specs(tm, tc, c0, L):
    nh = tm // HALO
    last = L // HALO - 1
    return [pl.BlockSpec((tm, tc), lambda i, j: (i, c0 + j)),
            pl.BlockSpec((HALO, tc), lambda i, j: (jnp.maximum(i * nh - 1, 0), c0 + j)),
            pl.BlockSpec((HALO, tc), lambda i, j: (jnp.minimum((i + 1) * nh, last), c0 + j))]


def _shift3(x):
    n = x.shape[0]
    return _roll(x, 1, 0), _roll(x, n - 1, 0)


def conv_fwd(gfn, x, c0, ncol, tc, w, b, tco, out_dtype, name, tm=256):
    L = x.shape[0]
    tm = _tile(L, tm)
    ni = L // tm

    def body(x_ref, p_ref, n_ref, w_ref, b_ref, o_ref):
        i = pl.program_id(0)
        xv = x_ref[...].astype(f32)
        xp = jnp.where(i == 0, 0.0, p_ref[HALO - 1:HALO, :].astype(f32))
        xn = jnp.where(i == ni - 1, 0.0, n_ref[0:1, :].astype(f32))
        rid = lax.broadcasted_iota(jnp.int32, xv.shape, 0)
        dn, up = _shift3(xv)
        dn = jnp.where(rid == 0, xp, dn)
        up = jnp.where(rid == tm - 1, xn, up)
        c = w_ref[0:1, :] * dn + w_ref[1:2, :] * xv + w_ref[2:3, :] * up + b_ref[...]
        o_ref[...] = gfn(c).astype(o_ref.dtype)

    return pl.pallas_call(
        body, name=name, grid=(ni, ncol),
        in_specs=_conv_specs(tm, tc, c0, L) + [pl.BlockSpec((3, tc), lambda i, j: (0, j)), pl.BlockSpec((1, tc), lambda i, j: (0, j))],
        out_specs=pl.BlockSpec((tm, tco), lambda i, j: (i, j)), out_shape=jax.ShapeDtypeStruct((L, ncol * tco), out_dtype),
        compiler_params=_params(("parallel", "parallel")),
    )(x, x, x, w, b)


def conv_bwd(gfn, x, c0, ncol, tc, w, b, dys, tco, name, tm=256):
    L = x.shape[0]
    tm = _tile(L, tm)
    ni = L // tm
    nd = len(dys)

    def body(*refs):
        x_ref, xp_ref, xn_ref = refs[:3]
        d_refs = refs[3:3 + 3 * nd]
        w_ref, b_ref, dx_ref, dw_ref, db_ref = refs[3 + 3 * nd:]
        i = pl.program_id(1)
        first, lastb = i == 0, i == ni - 1

        def ext(m, p, n):
            return jnp.concatenate([jnp.where(first, 0.0, p[...].astype(f32)), m[...].astype(f32),
                                    jnp.where(lastb, 0.0, n[...].astype(f32))], axis=0)

        xe = ext(x_ref, xp_ref, xn_ref)
        de = ext(*d_refs[:3])
        for q in range(1, nd):
            de = de + ext(*d_refs[3 * q:3 * q + 3])
        x_dn, x_up = _shift3(xe)
        w0, w1, w2 = w_ref[0:1, :], w_ref[1:2, :], w_ref[2:3, :]
        ce = w0 * x_dn + w1 * xe + w2 * x_up + b_ref[...]
        _, vjp = jax.vjp(gfn, ce)
        (dce,) = vjp(de)
        dc_dn, dc_up = _shift3(dce)
        dx_ref[...] = (w0 * dc_up + w1 * dce + w2 * dc_dn)[HALO:HALO + tm, :]
        rid = lax.broadcasted_iota(jnp.int32, dce.shape, 0)
        dci = jnp.where((rid >= HALO) & (rid < HALO + tm), dce, 0.0)
        dw = jnp.concatenate([jnp.sum(dci * x_dn, axis=0, keepdims=True), jnp.sum(dci * xe, axis=0, keepdims=True),
                              jnp.sum(dci * x_up, axis=0, keepdims=True)], axis=0)
        db = jnp.sum(dci, axis=0, keepdims=True)

        @pl.when(first)
        def _():
            dw_ref[...] = dw
            db_ref[...] = db

        @pl.when(i > 0)
        def _():
            dw_ref[...] += dw
            db_ref[...] += db

    res = pl.pallas_call(
        body, name=name, grid=(ncol, ni),
        in_specs=[pl.BlockSpec(s.block_shape, (lambda j, i, f=s.index_map: f(i, j))) for s in _conv_specs(tm, tc, c0, L)]
        + [pl.BlockSpec(s.block_shape, (lambda j, i, f=s.index_map: f(i, j))) for s in _conv_specs(tm, tco, 0, L)] * nd
        + [pl.BlockSpec((3, tc), lambda j, i: (0, j)), pl.BlockSpec((1, tc), lambda j, i: (0, j))],
        out_specs=[pl.BlockSpec((tm, tc), lambda j, i: (i, j)), pl.BlockSpec((3, tc), lambda j, i: (0, j)),
                   pl.BlockSpec((1, tc), lambda j, i: (0, j))],
        out_shape=[jax.ShapeDtypeStruct((L, ncol * tc), f32), jax.ShapeDtypeStruct((3, ncol * tc), f32),
                   jax.ShapeDtypeStruct((1, ncol * tc), f32)],
        compiler_params=_params(("parallel", "arbitrary")),
    )(x, x, x, *[d for d in dys for _ in range(3)], w, b)
    return res


LOG2E = math.log2(math.e)


def flash_fwd(q, k, v, n_q, n_kv, scale, name, tq=512, tk=2048):
    L = q.shape[0]
    tq, tk = _tile_div(L, tq), _tile_div(L, tk)
    rep = n_q // n_kv
    nkv = L // tk
    c2 = scale * LOG2E

    def body(q_ref, k_ref, v_ref, o_ref, lse_ref):
        qv = q_ref[...]

        def step(c, carry):
            m, l, acc = carry
            off = pl.multiple_of(c * tk, tk)
            s = _mm(qv, k_ref[pl.ds(off, tk), :], _NT)
            m_new = jnp.maximum(m, jnp.max(s, axis=-1, keepdims=True))
            p = jnp.exp2((s - m_new) * c2)
            alpha = jnp.exp2((m - m_new) * c2)
            l = alpha * l + jnp.sum(p, axis=-1, keepdims=True)
            acc = alpha * acc + _mm(p, v_ref[pl.ds(off, tk), :], _NN)
            return m_new, l, acc

        m, l, acc = lax.fori_loop(0, nkv, step, (jnp.full((tq, 1), -jnp.inf, f32), jnp.zeros((tq, 1), f32),
                                                  jnp.zeros((tq, HPAD), f32)))
        o_ref[...] = acc / l
        lse_ref[0] = m * scale + jnp.log(l)

    return pl.pallas_call(
        body, name=name, grid=(n_q, L // tq),
        in_specs=[pl.BlockSpec((tq, HPAD), lambda h, i: (i, h)), pl.BlockSpec((L, HPAD), lambda h, i: (0, h // rep)),
                  pl.BlockSpec((L, HPAD), lambda h, i: (0, h // rep))],
        out_specs=[pl.BlockSpec((tq, HPAD), lambda h, i: (i, h)), pl.BlockSpec((1, tq, 1), lambda h, i: (h, i, 0))],
        out_shape=[jax.ShapeDtypeStruct((L, n_q * HPAD), f32), jax.ShapeDtypeStruct((n_q, L, 1), f32)],
        compiler_params=_params(("parallel", "parallel")),
    )(q, k, v)


def flash_bwd(q, k, v, o, lse, do, n_q, n_kv, scale, name, tq=1024, tk=1024):
    L = q.shape[0]
    tq, tk = _tile_div(L, tq), _tile_div(L, tk)
    rep = n_q // n_kv
    c2 = scale * LOG2E

    def body(k_ref, v_ref, q_ref, do_ref, o_ref, lse_ref, dq_ref, dk_ref, dv_ref):
        j, i = pl.program_id(1), pl.program_id(2)
        kv, vv = k_ref[...], v_ref[...]
        rows = pl.ds(pl.multiple_of(i * tq, tq), tq)
        dk, dv = jnp.zeros((tk, HPAD), f32), jnp.zeros((tk, HPAD), f32)
        for r in range(rep):
            cols = slice(r * HPAD, (r + 1) * HPAD)
            qv = q_ref[:, cols]
            dov = do_ref[:, cols]
            delta = jnp.sum(dov * o_ref[:, cols], axis=-1, keepdims=True)
            p = jnp.exp2(_mm(qv, kv, _NT) * c2 - lse_ref[r] * LOG2E)
            dv = dv + _mm(p, dov, _TN)
            ds = p * (_mm(dov, vv, _NT) - delta) * scale
            dk = dk + _mm(ds, qv, _TN)
            dq = _mm(ds, kv, _NN)

            @pl.when(j == 0)
            def _(dq=dq, cols=cols):
                dq_ref[rows, cols] = dq

            @pl.when(j > 0)
            def _(dq=dq, cols=cols):
                dq_ref[rows, cols] += dq

        @pl.when(i == 0)
        def _():
            dk_ref[...] = dk
            dv_ref[...] = dv

        @pl.when(i > 0)
        def _():
            dk_ref[...] += dk
            dv_ref[...] += dv

    w = rep * HPAD
    return pl.pallas_call(
        body, name=name, grid=(n_kv, L // tk, L // tq),
        in_specs=[pl.BlockSpec((tk, HPAD), lambda g, j, i: (j, g)), pl.BlockSpec((tk, HPAD), lambda g, j, i: (j, g)),
                  pl.BlockSpec((tq, w), lambda g, j, i: (i, g)), pl.BlockSpec((tq, w), lambda g, j, i: (i, g)),
                  pl.BlockSpec((tq, w), lambda g, j, i: (i, g)), pl.BlockSpec((rep, tq, 1), lambda g, j, i: (g, i, 0))],
        out_specs=[pl.BlockSpec((L, w), lambda g, j, i: (0, g)), pl.BlockSpec((tk, HPAD), lambda g, j, i: (j, g)),
                   pl.BlockSpec((tk, HPAD), lambda g, j, i: (j, g))],
        out_shape=[jax.ShapeDtypeStruct((L, n_q * HPAD), f32), jax.ShapeDtypeStruct((L, n_kv * HPAD), f32),
                   jax.ShapeDtypeStruct((L, n_kv * HPAD), f32)],
        compiler_params=_params(("parallel", "arbitrary", "arbitrary")),
    )(k, v, q, do, o, lse)


N_SCAN_HEADS = 4
STATE_ROWS = N_SCAN_HEADS * HEAD


def _tri(n, rev):
    i = lax.broadcasted_iota(jnp.int32, (n, n), 0)
    k = lax.broadcasted_iota(jnp.int32, (n, n), 1)
    return ((k >= i), (k > i)) if rev else ((k <= i), (k < i))


def _decay(acol, incl):
    n = acol.shape[0]
    m1 = jnp.broadcast_to(acol, (n, n))
    return jnp.exp(jnp.where(incl, m1 - m1.T, -jnp.inf))


def _ssd_chunk(S, xbc, win, a_log, dt_bias, rev):
    Q = xbc.shape[0]
    d = 1 if rev else 0
    incl, _ = _tri(Q, rev)
    dt = _softplus(win[:, 4 * d:4 * d + 4] + dt_bias[d:d + 1, :])
    a = dt * (-jnp.exp(a_log[d:d + 1, :]))
    acum = _mm(incl.astype(f32), a, _NN, HI)
    tot = jnp.sum(a, axis=0, keepdims=True)
    xs, Bm, Cm = xbc[:, :256], xbc[:, 256:384], xbc[:, 384:512]
    ys, Ss = [], []
    for h in range(N_SCAN_HEADS):
        g = h // 2
        Bg, Cg = Bm[:, g * HEAD:(g + 1) * HEAD], Cm[:, g * HEAD:(g + 1) * HEAD]
        xdt = xs[:, h * HEAD:(h + 1) * HEAD] * dt[:, h:h + 1]
        ac, th = acum[:, h:h + 1], tot[:, h:h + 1]
        y = _mm(_mm(Cg, Bg, _NT) * _decay(ac, incl), xdt, _NN)
        Sh = S[h * HEAD:(h + 1) * HEAD, :]
        ys.append(y + _mm(Cg, Sh, _NT) * jnp.exp(ac))
        Ss.append(Sh * jnp.exp(th) + _mm(xdt * jnp.exp(th - ac), Bg, _TN))
    return jnp.concatenate(Ss, axis=0), jnp.concatenate(ys, axis=1)


def _inv_unit_tri(Lm, order):
    n = Lm.shape[0]
    eye = (lax.broadcasted_iota(jnp.int32, (n, n), 0) == lax.broadcasted_iota(jnp.int32, (n, n), 1)).astype(f32)
    P = -Lm
    T = eye + P
    k = 1
    while 2 * k < order:
        P = _mm(P, P, _NN, lax.Precision.HIGH)
        T = T + _mm(T, P, _NN, lax.Precision.HIGH)
        k *= 2
    return T


def _dn_chunk(S, qkv, win, a_log, dt_bias, rev):
    Q = qkv.shape[0]
    H, R = N_SCAN_HEADS, N_SCAN_HEADS * qkv.shape[0]
    d = 1 if rev else 0
    beta = jax.nn.sigmoid(win[:, 4 * d:4 * d + 4])
    gl = -jnp.exp(a_log[d:d + 1, :]) * _softplus(win[:, 8 + 4 * d:12 + 4 * d] + dt_bias[d:d + 1, :])
    G = _mm(_tri(Q, rev)[0].astype(f32), gl, _NN, HI)
    tot = jnp.sum(gl, axis=0, keepdims=True)
    rows = lambda x: jnp.concatenate(_heads(x, H), axis=0)
    col = lambda x: jnp.concatenate([x[:, h:h + 1] for h in range(H)], axis=0)
    per_head = lambda x, n: jnp.concatenate([jnp.broadcast_to(x[:, h:h + 1], (n, 1)) for h in range(H)], axis=0)
    r = lax.broadcasted_iota(jnp.int32, (R, R), 0)
    c = lax.broadcasted_iota(jnp.int32, (R, R), 1)
    same = (r // Q) == (c // Q)
    incl = same & ((c >= r) if rev else (c <= r))
    strict = same & ((c > r) if rev else (c < r))
    own = (lax.broadcasted_iota(jnp.int32, (R, H * HEAD), 0) // Q) == (lax.broadcasted_iota(jnp.int32, (R, H * HEAD), 1) // HEAD)
    blk = lambda x: jnp.where(own, _cat([x] * H), 0.0)
    q, k, v = rows(qkv[:, :256]) * (HEAD ** -0.5), rows(qkv[:, 256:512]), rows(qkv[:, 512:])
    b, Gs, tots = col(beta), col(G), per_head(tot, Q)
    dec = _decay(Gs, incl)
    kb = k * b
    T = _inv_unit_tri(jnp.where(strict, _mm(kb, k, _NT) * dec, 0.0), Q)
    eG = jnp.exp(Gs)
    uw = _mm(T, _cat([v * b, kb * eG]), _NN)
    vnew = uw[:, :HEAD] - _mm(blk(uw[:, HEAD:]), S, _NN)
    o = _mm(blk(q * eG), S, _NN) + _mm(_mm(q, k, _NT) * dec, vnew, _NN)
    S_next = S * jnp.exp(per_head(tot, HEAD)) + _mm(blk(k * jnp.exp(tots - Gs)), vnew, _TN)
    return S_next, _cat([o[h * Q:(h + 1) * Q, :] for h in range(H)])


def scan_fwd(chunk_fn, Q, x, p, win_blk, a_log, dt_bias, name):
    L, W = x.shape
    nc = L // Q
    fidx, ridx = (lambda t: t), (lambda t: nc - 1 - t)

    def body(xf_ref, xr_ref, wf_ref, wr_ref, al_ref, db_ref, yf_ref, yr_ref, sf_ref, sr_ref, sf_scr, sr_scr):
        @pl.when(pl.program_id(0) == 0)
        def _():
            sf_scr[...] = jnp.zeros_like(sf_scr)
            sr_scr[...] = jnp.zeros_like(sr_scr)

        for rev, x_ref, w_ref, y_ref, sin_ref, s_scr in ((False, xf_ref, wf_ref, yf_ref, sf_ref, sf_scr),
                                                        (True, xr_ref, wr_ref, yr_ref, sr_ref, sr_scr)):
            S = s_scr[...]
            sin_ref[...] = S
            s_scr[...], y_ref[...] = chunk_fn(S, x_ref[...], w_ref[...], al_ref[...], db_ref[...], rev)

    row = lambda w, idx, b=0: pl.BlockSpec((Q, w), lambda t: (idx(t), b))
    small = pl.BlockSpec(a_log.shape, lambda t: (0, 0))
    state = lambda idx: pl.BlockSpec((STATE_ROWS, HEAD), lambda t: (idx(t), 0))
    return pl.pallas_call(
        body, name=name, grid=(nc,),
        in_specs=[row(W, fidx), row(W, ridx), row(HPAD, fidx, win_blk), row(HPAD, ridx, win_blk), small, small],
        out_specs=[row(STATE_ROWS, fidx), row(STATE_ROWS, ridx), state(fidx), state(ridx)],
        out_shape=[jax.ShapeDtypeStruct((L, STATE_ROWS), f32)] * 2 + [jax.ShapeDtypeStruct((nc * STATE_ROWS, HEAD), f32)] * 2,
        scratch_shapes=[pltpu.VMEM((STATE_ROWS, HEAD), f32)] * 2, compiler_params=_params(("arbitrary",)),
    )(x, x, p, p, a_log, dt_bias)


def scan_bwd(chunk_fn, Q, x, p, win_blk, a_log, dt_bias, s_fwd, s_rev, dy, name):
    L, W = x.shape
    nc = L // Q
    fidx, ridx = (lambda t: nc - 1 - t), (lambda t: t)

    def body(xf_ref, xr_ref, wf_ref, wr_ref, al_ref, db_ref, sf_ref, sr_ref, dyf_ref, dyr_ref,
             dxf_ref, dxr_ref, dwf_ref, dwr_ref, dal_ref, ddb_ref, dsf_scr, dsr_scr):
        @pl.when(pl.program_id(0) == 0)
        def _():
            dsf_scr[...] = jnp.zeros_like(dsf_scr)
            dsr_scr[...] = jnp.zeros_like(dsr_scr)
            dal_ref[...] = jnp.zeros_like(dal_ref)
            ddb_ref[...] = jnp.zeros_like(ddb_ref)

        dal_sum, ddb_sum = dal_ref[...], ddb_ref[...]
        for rev, x_ref, w_ref, sin_ref, dy_ref, dx_ref, dw_ref, ds_scr in (
                (False, xf_ref, wf_ref, sf_ref, dyf_ref, dxf_ref, dwf_ref, dsf_scr),
                (True, xr_ref, wr_ref, sr_ref, dyr_ref, dxr_ref, dwr_ref, dsr_scr)):
            _, vjp = jax.vjp(lambda S, xv, wv, al, db, rev=rev: chunk_fn(S, xv, wv, al, db, rev),
                             sin_ref[...], x_ref[...], w_ref[...], al_ref[...], db_ref[...])
            ds_scr[...], dx_ref[...], dw_ref[...], dal, ddb = vjp((ds_scr[...], dy_ref[...]))
            dal_sum, ddb_sum = dal_sum + dal, ddb_sum + ddb
        dal_ref[...] = dal_sum
        ddb_ref[...] = ddb_sum

    row = lambda w, idx, b=0: pl.BlockSpec((Q, w), lambda t: (idx(t), b))
    small = pl.BlockSpec(a_log.shape, lambda t: (0, 0))
    state = lambda idx: pl.BlockSpec((STATE_ROWS, HEAD), lambda t: (idx(t), 0))
    return pl.pallas_call(
        body, name=name, grid=(nc,),
        in_specs=[row(W, fidx), row(W, ridx), row(HPAD, fidx, win_blk), row(HPAD, ridx, win_blk), small, small,
                  state(fidx), state(ridx), row(STATE_ROWS, fidx), row(STATE_ROWS, ridx)],
        out_specs=[row(W, fidx), row(W, ridx), row(HPAD, fidx), row(HPAD, ridx), small, small],
        out_shape=[jax.ShapeDtypeStruct((L, W), f32)] * 2 + [jax.ShapeDtypeStruct((L, HPAD), f32)] * 2
        + [jax.ShapeDtypeStruct(a_log.shape, f32)] * 2,
        scratch_shapes=[pltpu.VMEM((STATE_ROWS, HEAD), f32)] * 2, compiler_params=_params(("arbitrary",)),
    )(x, x, p, p, a_log, dt_bias, s_fwd, s_rev, dy, dy)


def _cat(xs):
    return jnp.concatenate(xs, axis=-1)


def _a_prep(pa, cos, sin, qn, wuq, kvn, wk, wv):
    cq, ckv, kr = pa[:, :256], pa[:, 256:384], pa[:, 384:512]
    q = _rope(_mm(_rms(cq, qn, A_Q_LORA), wuq, _NN), _cat([cos] * 4), _cat([sin] * 4), A_NOPE, A_ROPE // 4)
    kvh = _rms(ckv, kvn)
    k = _mm(kvh, wk, _NN) + _cat([_rope(kr, cos, sin, A_NOPE, A_ROPE // 4)] * 4)
    return q, k, _mm(kvh, wv, _NN)


def _b_prep(pb, cos, sin, qn, kn):
    q = _pad_heads([_rms(h, qn) for h in _heads(pb[:, :256], 4)])
    k = _pad_heads([_rms(h, kn) for h in _heads(pb[:, 256:384], 2)])
    v = _pad_heads(_heads(pb[:, 384:512], 2))
    return _rope(q, _cat([cos] * 4), _cat([sin] * 4), 0, HEAD // 4), _rope(k, _cat([cos] * 2), _cat([sin] * 2), 0, HEAD // 4), v


def _c_act(c):
    return _silu(c)


def _d_act(c):
    s = _silu(c)
    return _cat([h * lax.rsqrt(jnp.sum(h * h, axis=-1, keepdims=True) + 1e-6) for h in _heads(s[:, :512], 8)] + [s[:, 512:]])


def _ffn_act(c):
    return _silu(c[:, :FF_BLK]) * c[:, FF_BLK:]


def _mix_post(oa, ob, yc_f, yc_r, xbc, cz, od_f, od_r, dz, a_out, b_out, dskip, c_out, d_out):
    o_a = _rms(_unpad_heads(oa, 4), a_out)
    o_b = _rms(_unpad_heads(ob, 4), b_out)
    skip = _cat([jnp.broadcast_to(dskip[:, h:h + 1], (1, HEAD)) for h in range(4)])
    o_c = _rms((yc_f + yc_r + xbc[:, :256] * skip) * _silu(cz), c_out)
    o_d = _cat([_rms(h, d_out) for h in _heads(od_f + od_r, 4)]) * _silu(dz)
    return _cat([o_a, o_b, o_c, o_d])


def _post_mix(x, y1, g_pm, g_pf):
    x1 = x + _rms(y1, g_pm)
    return x1, _rms(x1, g_pf)


def _post_ffn(x1, y2, g):
    return x1 + _rms(y2, g)


def _pre_bwd(x, g):
    return x, _rms(x, g)


def _layer_fwd(x, W, tabs, n):
    mx = MXU_DTYPE
    cos_a, sin_a, cos_b, sin_b = tabs
    full = lambda a: (a, a.shape[1], 0)
    s = {"x": x}
    (s["h"],) = rowwise(_rms, [full(x)], [W["g_pre"]], [(D_MODEL, mx)], n + "pre")
    p = s["p"] = matmul(s["h"], W["w_in"], "nn", f32, n + "in")
    s["qa"], s["ka"], s["va"] = rowwise(_a_prep, [(p, 512, SEG_A // 512), full(cos_a), full(sin_a)],
                                        [W["a_qn"], W["a_wuq"], W["a_kvn"], W["a_wk"], W["a_wv"]], [(512, mx)] * 3, n + "a_prep")
    s["qb"], s["kb"], s["vb"] = rowwise(_b_prep, [(p, 512, SEG_B // 512), full(cos_b), full(sin_b)], [W["b_qn"], W["b_kn"]],
                                        [(512, mx), (256, mx), (256, mx)], n + "b_prep")
    s["oa"], s["lsea"] = flash_fwd(s["qa"], s["ka"], s["va"], 4, 4, (A_NOPE + A_ROPE) ** -0.5, n + "a_attn")
    s["ob"], s["lseb"] = flash_fwd(s["qb"], s["kb"], s["vb"], 4, 2, HEAD ** -0.5, n + "b_attn")
    xbc = s["xbc"] = conv_fwd(_c_act, p, SEG_CX // 512, 1, 512, W["c_cw"], W["c_cb"], 512, f32, n + "c_conv")
    s["yc_f"], s["yc_r"], s["sc_f"], s["sc_r"] = scan_fwd(_ssd_chunk, SSD_CHUNK, xbc, p, SEG_CDT // HPAD, W["c_alog"], W["c_dtb"], n + "c_ssd")
    qkv = s["qkv"] = conv_fwd(_d_act, p, SEG_DX // 768, 1, 768, W["d_cw"], W["d_cb"], 768, f32, n + "d_conv")
    s["od_f"], s["od_r"], s["sd_f"], s["sd_r"] = scan_fwd(_dn_chunk, DN_CHUNK, qkv, p, SEG_DAB // HPAD, W["d_alog"], W["d_dtb"], n + "d_dn")
    (s["omix"],) = rowwise(_mix_post, _mix_rows(s), _mix_params(W), [(D_MODEL, mx)], n + "mix_post")
    s["y1"] = matmul(s["omix"], W["w_out"], "nn", f32, n + "out")
    s["x1"], s["h2"] = rowwise(_post_mix, [full(x), full(s["y1"])], [W["g_pm"], W["g_pf"]], [(D_MODEL, f32), (D_MODEL, mx)], n + "post_mix")
    s["u"] = matmul(s["h2"], W["f_win"], "nn", f32, n + "f_in")
    s["a"] = conv_fwd(_ffn_act, s["u"], 0, D_FF // FF_BLK, 2 * FF_BLK, W["f_cw"], W["f_cb"], FF_BLK, mx, n + "f_conv")
    s["y2"] = matmul(s["a"], W["f_wout"], "nn", f32, n + "f_out")
    (x2,) = rowwise(_post_ffn, [full(s["x1"]), full(s["y2"])], [W["g_po"]], [(D_MODEL, f32)], n + "post_ffn")
    return x2, s


def _mix_rows(s):
    p = s["p"]
    return [(s["oa"], 512, 0), (s["ob"], 512, 0), (s["yc_f"], 256, 0), (s["yc_r"], 256, 0), (s["xbc"], 512, 0),
            (p, 256, SEG_CZ // 256), (s["od_f"], 256, 0), (s["od_r"], 256, 0), (p, 256, SEG_DZ // 256)]


def _mix_params(W):
    return [W["a_out"], W["b_out"], W["c_dskip"], W["c_out"], W["d_out"]]


def _layer_bwd(s, dx2, W, tabs, n):
    cos_a, sin_a, cos_b, sin_b = tabs
    full = lambda a: (a, a.shape[1], 0)
    p, g = s["p"], {}
    (dx1, dy2), (g["g_po"],) = rowwise_vjp(_post_ffn, [full(s["x1"]), full(s["y2"])], [W["g_po"]], [full(dx2)], n + "post_ffn_b", [0, 1], [0])
    da = matmul(dy2, W["f_wout"], "nt", f32, n + "f_out_dx")
    g["f_wout"] = matmul(s["a"], dy2, "tn", f32, n + "f_out_dw")
    du, g["f_cw"], g["f_cb"] = conv_bwd(_ffn_act, s["u"], 0, D_FF // FF_BLK, 2 * FF_BLK, W["f_cw"], W["f_cb"], [da], FF_BLK, n + "f_conv_b")
    dh2 = matmul(du, W["f_win"], "nt", f32, n + "f_in_dx")
    g["f_win"] = matmul(s["h2"], du, "tn", f32, n + "f_in_dw")
    (dx, dy1), (g["g_pm"], g["g_pf"]) = rowwise_vjp(_post_mix, [full(s["x"]), full(s["y1"])], [W["g_pm"], W["g_pf"]],
                                                   [full(dx1), full(dh2)], n + "post_mix_b", [0, 1], [0, 1])
    domix = matmul(dy1, W["w_out"], "nt", f32, n + "out_dx")
    g["w_out"] = matmul(s["omix"], dy1, "tn", f32, n + "out_dw")
    (doa, dob, dyc, dxbc_skip, dcz, dod, ddz), (g["a_out"], g["b_out"], g["c_dskip"], g["c_out"], g["d_out"]) = rowwise_vjp(
        _mix_post, _mix_rows(s), _mix_params(W), [full(domix)], n + "mix_post_b", [0, 1, 2, 4, 5, 6, 8], list(range(5)))
    dqkv_f, dqkv_r, ddab_f, ddab_r, g["d_alog"], g["d_dtb"] = scan_bwd(
        _dn_chunk, DN_CHUNK, s["qkv"], p, SEG_DAB // HPAD, W["d_alog"], W["d_dtb"], s["sd_f"], s["sd_r"], dod, n + "d_dn_b")
    ddx, g["d_cw"], _ = conv_bwd(_d_act, p, SEG_DX // 768, 1, 768, W["d_cw"], W["d_cb"], [dqkv_f, dqkv_r], 768, n + "d_conv_b")
    dxbc_f, dxbc_r, dcdt_f, dcdt_r, g["c_alog"], g["c_dtb"] = scan_bwd(
        _ssd_chunk, SSD_CHUNK, s["xbc"], p, SEG_CDT // HPAD, W["c_alog"], W["c_dtb"], s["sc_f"], s["sc_r"], dyc, n + "c_ssd_b")
    dcx, g["c_cw"], g["c_cb"] = conv_bwd(_c_act, p, SEG_CX // 512, 1, 512, W["c_cw"], W["c_cb"], [dxbc_f, dxbc_r, dxbc_skip], 512,
                                         n + "c_conv_b")
    (dsmall,) = rowwise(lambda a, b, c, d: _cat([a + b, c + d]), [full(dcdt_f), full(dcdt_r), full(ddab_f), full(ddab_r)], [],
                        [(2 * HPAD, f32)], n + "dwin_sum")
    dqa, dka, dva = flash_bwd(s["qa"], s["ka"], s["va"], s["oa"], s["lsea"], doa, 4, 4, (A_NOPE + A_ROPE) ** -0.5, n + "a_attn_b")
    dqb, dkb, dvb = flash_bwd(s["qb"], s["kb"], s["vb"], s["ob"], s["lseb"], dob, 4, 2, HEAD ** -0.5, n + "b_attn_b")
    (dpa,), (g["a_qn"], g["a_wuq"], g["a_kvn"], g["a_wk"], g["a_wv"]) = rowwise_vjp(
        _a_prep, [(p, 512, SEG_A // 512), full(cos_a), full(sin_a)], [W["a_qn"], W["a_wuq"], W["a_kvn"], W["a_wk"], W["a_wv"]],
        [full(dqa), full(dka), full(dva)], n + "a_prep_b", [0], list(range(5)))
    (dpb,), (g["b_qn"], g["b_kn"]) = rowwise_vjp(_b_prep, [(p, 512, SEG_B // 512), full(cos_b), full(sin_b)], [W["b_qn"], W["b_kn"]],
                                               [full(dqb), full(dkb), full(dvb)], n + "b_prep_b", [0], [0, 1])
    dp = jnp.concatenate([dpa, dpb, dcx, ddx, dcz, ddz, dsmall], axis=1)
    dh = matmul(dp, W["w_in"], "nt", f32, n + "in_dx")
    g["w_in"] = matmul(s["h"], dp, "tn", f32, n + "in_dw")
    (dx0,), (g["g_pre"],) = rowwise_vjp(_pre_bwd, [full(s["x"])], [W["g_pre"]], [full(dx), full(dh)], n + "pre_b", [0], [0])
    return dx0, g


def loss_and_grad(y, target, name):
    L, D = y.shape
    tm = _tile(L, 512)

    def body(y_ref, t_ref, loss_ref, dy_ref):
        e = y_ref[...] - t_ref[...]
        dy_ref[...] = e * (1.0 / D)
        part = 0.5 * jnp.sum(jnp.sum(e * e, axis=1, keepdims=True) * (1.0 / D), axis=0, keepdims=True)

        @pl.when(pl.program_id(0) == 0)
        def _():
            loss_ref[...] = part

        @pl.when(pl.program_id(0) > 0)
        def _():
            loss_ref[...] += part

    row = pl.BlockSpec((tm, D), lambda i: (i, 0))
    return pl.pallas_call(
        body, name=name, grid=(L // tm,), in_specs=[row, row], out_specs=[pl.BlockSpec((1, 1), lambda i: (0, 0)), row],
        out_shape=[jax.ShapeDtypeStruct((1, 1), f32), jax.ShapeDtypeStruct((L, D), f32)], compiler_params=_params(("arbitrary",)),
    )(y, target)


_IN_SEGS = [(0, 192), (256, 128), (384 + A_NOPE, 32), (512, 256), (768, 128), (896, 128), (SEG_CZ, 256), (SEG_CX, 512),
            (SEG_CDT, 8), (SEG_DX, 768), (SEG_DZ, 256), (SEG_DAB, 16)]


def _pad_in(w):
    src, pieces = 0, {}
    for off, wd in _IN_SEGS:
        pieces[off] = w[..., src:src + wd]
        src += wd
    out, pos = [], 0
    for off in sorted(pieces):
        if off > pos:
            out.append(jnp.zeros(w.shape[:-1] + (off - pos,), w.dtype))
        out.append(pieces[off])
        pos = off + pieces[off].shape[-1]
    out.append(jnp.zeros(w.shape[:-1] + (PIN - pos,), w.dtype))
    return jnp.concatenate(out, axis=-1)


def _unpad_in(wp):
    return jnp.concatenate([wp[..., off:off + wd] for off, wd in _IN_SEGS], axis=-1)


def _pad_last(a, n):
    return jnp.pad(a, [(0, 0)] * (a.ndim - 1) + [(0, n - a.shape[-1])])


def _ff_interleave(w):
    lead = w.shape[:-1]
    return jnp.stack([w[..., :D_FF].reshape(lead + (D_FF // FF_BLK, FF_BLK)), w[..., D_FF:].reshape(lead + (D_FF // FF_BLK, FF_BLK))],
                     axis=-2).reshape(lead + (2 * D_FF,))


def _ff_deinterleave(w):
    lead = w.shape[:-1]
    t = w.reshape(lead + (D_FF // FF_BLK, 2, FF_BLK))
    return jnp.concatenate([t[..., 0, :].reshape(lead + (D_FF,)), t[..., 1, :].reshape(lead + (D_FF,))], axis=-1)


def _row(v):
    return v.reshape(1, -1)


def _kernel_weights(P, l, wdt):
    uq = P["a_w_uq"][l].reshape(A_Q_LORA, 4, A_NOPE + A_ROPE)
    ukv = P["a_w_ukv"][l].reshape(A_KV_LORA, 4, 2 * HEAD)
    z = jnp.zeros((A_KV_LORA, 4, HEAD), ukv.dtype)
    return {
        "g_pre": _row(P["pre_mix_norm"][l]), "w_in": _pad_in(P["w_in"][l]).astype(wdt),
        "a_qn": _pad_last(_row(P["a_q_norm"][l]), 256),
        "a_wuq": jnp.pad(_pad_last(uq, HPAD).reshape(A_Q_LORA, 4 * HPAD), ((0, 256 - A_Q_LORA), (0, 0))).astype(wdt),
        "a_kvn": _row(P["a_kv_norm"][l]),
        "a_wk": jnp.concatenate([ukv[..., :HEAD], z], axis=-1).reshape(A_KV_LORA, 4 * HPAD).astype(wdt),
        "a_wv": jnp.concatenate([ukv[..., HEAD:], z], axis=-1).reshape(A_KV_LORA, 4 * HPAD).astype(wdt),
        "a_out": _row(P["a_out_norm"][l]), "b_qn": _row(P["b_q_norm"][l]), "b_kn": _row(P["b_k_norm"][l]), "b_out": _row(P["b_out_norm"][l]),
        "c_cw": P["c_conv_w"][l], "c_cb": _row(P["c_conv_b"][l]), "c_alog": P["c_a_log"][l], "c_dtb": P["c_dt_bias"][l],
        "c_dskip": _row(P["c_d_skip"][l]), "c_out": _row(P["c_out_norm"][l]),
        "d_cw": P["d_conv_w"][l], "d_cb": jnp.zeros((1, 768), f32), "d_alog": P["d_a_log"][l], "d_dtb": P["d_dt_bias"][l],
        "d_out": _row(P["d_out_norm"][l]), "w_out": P["w_out"][l].astype(wdt), "g_pm": _row(P["post_mix_norm"][l]),
        "g_pf": _row(P["pre_ffn_norm"][l]), "f_win": _ff_interleave(P["f_w_in"][l]).astype(wdt),
        "f_cw": _ff_interleave(P["f_conv_w"][l]), "f_cb": _row(_ff_interleave(P["f_conv_b"][l])),
        "f_wout": P["f_w_out"][l].astype(wdt), "g_po": _row(P["post_ffn_norm"][l]),
    }


def _reference_grads(g):
    uq = g["a_wuq"][:A_Q_LORA].reshape(A_Q_LORA, 4, HPAD)[..., :A_NOPE + A_ROPE].reshape(A_Q_LORA, 4 * (A_NOPE + A_ROPE))
    wk = g["a_wk"].reshape(A_KV_LORA, 4, HPAD)[..., :HEAD]
    wv = g["a_wv"].reshape(A_KV_LORA, 4, HPAD)[..., :HEAD]
    return {
        "pre_mix_norm": g["g_pre"][0], "w_in": _unpad_in(g["w_in"]), "a_q_norm": g["a_qn"][0, :A_Q_LORA], "a_w_uq": uq,
        "a_kv_norm": g["a_kvn"][0], "a_w_ukv": jnp.concatenate([wk, wv], axis=-1).reshape(A_KV_LORA, 8 * HEAD),
        "a_out_norm": g["a_out"][0], "b_q_norm": g["b_qn"][0], "b_k_norm": g["b_kn"][0], "b_out_norm": g["b_out"][0],
        "c_conv_w": g["c_cw"], "c_conv_b": g["c_cb"][0], "c_a_log": g["c_alog"], "c_dt_bias": g["c_dtb"], "c_d_skip": g["c_dskip"][0],
        "c_out_norm": g["c_out"][0], "d_conv_w": g["d_cw"], "d_a_log": g["d_alog"], "d_dt_bias": g["d_dtb"], "d_out_norm": g["d_out"][0],
        "w_out": g["w_out"], "post_mix_norm": g["g_pm"][0], "pre_ffn_norm": g["g_pf"][0], "f_w_in": _ff_deinterleave(g["f_win"]),
        "f_conv_w": _ff_deinterleave(g["f_cw"]), "f_conv_b": _ff_deinterleave(g["f_cb"][0]), "f_w_out": g["f_wout"],
        "post_ffn_norm": g["g_po"][0],
    }


def _rope_tables(L):
    def tables(rot):
        rows = L // GRID_W
        row = jnp.repeat(jnp.arange(rows), GRID_W).astype(f32)
        col = jnp.tile(jnp.arange(GRID_W), rows).astype(f32)
        sec = rot // 2
        inv = ROPE_BASE ** (-jnp.arange(0, sec, 2, dtype=f32) / sec)
        ang = jnp.concatenate([row[:, None] * inv] * 2 + [col[:, None] * inv] * 2, axis=-1)
        return jnp.cos(ang), jnp.sin(ang)

    ca, sa = tables(A_ROPE)
    cb, sb = tables(HEAD)
    one, zero = jnp.ones, jnp.zeros
    return (jnp.concatenate([one((L, A_NOPE), f32), ca, one((L, HPAD - A_NOPE - A_ROPE), f32)], axis=1),
            jnp.concatenate([zero((L, A_NOPE), f32), sa, zero((L, HPAD - A_NOPE - A_ROPE), f32)], axis=1),
            jnp.concatenate([cb, one((L, HPAD - HEAD), f32)], axis=1), jnp.concatenate([sb, zero((L, HPAD - HEAD), f32)], axis=1))


def local_step(x, target, P, wdt):
    tabs = _rope_tables(x.shape[0])
    Ws = [_kernel_weights(P, l, wdt) for l in range(DEPTH)]
    saved = []
    for l in range(DEPTH):
        x, s = _layer_fwd(x, Ws[l], tabs, f"l{l}_")
        saved.append(s)
    loss, dx = loss_and_grad(x, target, "loss")
    grads = [None] * DEPTH
    for l in reversed(range(DEPTH)):
        dx, g = _layer_bwd(saved[l], dx, Ws[l], tabs, f"l{l}_")
        grads[l] = _reference_grads(g)
    return loss, dx, grads


def exchange(items, name):
    n = len(items)
    modes = [g for _, g in items]

    def body(*refs):
        src_refs, out_refs = refs[:n], refs[n:2 * n]
        send_sems, recv_sems, local_sems = refs[2 * n:]
        x, y, c = lax.axis_index("x"), lax.axis_index("y"), lax.axis_index("c")
        me = 4 * x + 2 * y + c

        def copy(a, k, arriving):
            px, py, pc = (1 - x if k & 4 else x), (1 - y if k & 2 else y), (1 - c if k & 1 else c)
            pid = 4 * px + 2 * py + pc
            sem = a * (N_DEV - 1) + k - 1
            return pltpu.make_async_remote_copy(
                src_ref=src_refs[a] if modes[a] else src_refs[a].at[pid], dst_ref=out_refs[a].at[pid if arriving else me],
                send_sem=send_sems.at[sem], recv_sem=recv_sems.at[sem], device_id=(px, py, pc), device_id_type=pl.DeviceIdType.MESH)

        local = [pltpu.make_async_copy(src_refs[a] if modes[a] else src_refs[a].at[me], out_refs[a].at[me], local_sems.at[a])
                 for a in range(n)]
        pairs = [(a, k) for k in range(1, N_DEV) for a in range(n)]
        for a, k in pairs:
            copy(a, k, False).start()
        for cp in local:
            cp.start()
        for a, k in pairs:
            copy(a, k, False).wait_send()
        for a, k in pairs:
            copy(a, k, True).wait_recv()
        for cp in local:
            cp.wait()

    hbm = pl.BlockSpec(memory_space=pl.ANY)
    res = pl.pallas_call(
        body, name=name, in_specs=[hbm] * n, out_specs=[hbm] * n,
        out_shape=[jax.ShapeDtypeStruct((N_DEV,) + tuple(s.shape if g else s.shape[1:]), s.dtype) for s, g in items],
        scratch_shapes=[pltpu.SemaphoreType.DMA((n * (N_DEV - 1),)), pltpu.SemaphoreType.DMA((n * (N_DEV - 1),)),
                        pltpu.SemaphoreType.DMA((n,))],
    )(*[s for s, _ in items])
    return list(res)


LANES = 1024


def adamw(gstack, w, m, v, name):
    S, D, R, C = gstack.shape
    tr = R if R <= 512 else _tile(R, 256)

    def body(g_ref, w_ref, m_ref, v_ref, go_ref, d_ref, mo_ref, vo_ref):
        g = g_ref[0]
        for sl in range(1, S):
            g = g + g_ref[sl]
        m_new = ADAM_B1 * m_ref[...] + (1.0 - ADAM_B1) * g
        v_new = ADAM_B2 * v_ref[...] + (1.0 - ADAM_B2) * jnp.square(g)
        m_hat = m_new / (1.0 - ADAM_B1 ** ADAM_STEP)
        v_hat = v_new / (1.0 - ADAM_B2 ** ADAM_STEP)
        go_ref[...] = g
        d_ref[...] = -ADAM_LR * (m_hat / (jnp.sqrt(v_hat) + ADAM_EPS) + ADAM_WD * w_ref[...])
        mo_ref[...] = m_new
        vo_ref[...] = v_new

    row = pl.BlockSpec((1, tr, C), lambda l, i: (l, i, 0))
    return pl.pallas_call(
        body, name=name, grid=(D, R // tr), in_specs=[pl.BlockSpec((S, 1, tr, C), lambda l, i: (0, l, i, 0)), row, row, row],
        out_specs=[row] * 4, out_shape=[jax.ShapeDtypeStruct((D, R, C), f32)] * 4, compiler_params=_params(("parallel", "parallel")),
    )(gstack, w, m, v)


def sum_slots(gstack, name):
    S, R, _ = gstack.shape

    def body(g_ref, o_ref):
        g = g_ref[0]
        for sl in range(1, S):
            g = g + g_ref[sl]
        o_ref[...] = g

    return pl.pallas_call(body, name=name, out_shape=jax.ShapeDtypeStruct((R, LANES), f32), compiler_params=_params())(gstack)


def _pack(parts, rows, dtype=f32):
    flat = jnp.concatenate([q.reshape(-1).astype(dtype) for q in parts])
    return jnp.pad(flat, (0, rows * LANES - flat.shape[0])).reshape(rows, LANES)


def _unpack(buf, shapes):
    lead = buf.shape[:-2]
    flat = buf.reshape(lead + (-1,))
    out, off = [], 0
    for shp in shapes:
        n = math.prod(shp)
        out.append(flat[..., off:off + n].reshape(lead + tuple(shp)))
        off += n
    return out


_WEIGHTS = ["pre_mix_norm", "w_in", "a_q_norm", "a_w_uq", "a_kv_norm", "a_w_ukv", "a_out_norm", "b_q_norm", "b_k_norm", "b_out_norm",
            "c_conv_w", "c_conv_b", "c_a_log", "c_dt_bias", "c_d_skip", "c_out_norm", "d_conv_w", "d_a_log", "d_dt_bias", "d_out_norm",
            "w_out", "post_mix_norm", "pre_ffn_norm", "f_w_in", "f_conv_w", "f_conv_b", "f_w_out", "post_ffn_norm"]
_BIG = {"w_in": 1, "a_w_uq": 1, "a_w_ukv": 1, "w_out": 0, "f_w_in": 1, "f_w_out": 0}
_CONV = ["c_conv_w", "d_conv_w", "f_conv_w"]
_REP = [n for n in _WEIGHTS if n not in _BIG and n not in _CONV]
SMALLG_ROWS, SMALLW_ROWS = 64, 32


def _join(blocks, axis):
    if axis == 0:
        return blocks.reshape(-1, blocks.shape[2])
    return blocks.transpose(1, 0, 2).reshape(blocks.shape[1], -1)


def _split(full, axis):
    if axis == 0:
        return full.reshape(N_DEV, -1, full.shape[1])
    return full.reshape(full.shape[0], N_DEV, -1).transpose(1, 0, 2)


def kernel(x, pre_mix_norm, w_in, a_q_norm, a_w_uq, a_kv_norm, a_w_ukv, a_out_norm, b_q_norm, b_k_norm, b_out_norm, c_conv_w, c_conv_b, c_a_log, c_dt_bias, c_d_skip, c_out_norm, d_conv_w, d_a_log, d_dt_bias, d_out_norm, w_out, post_mix_norm, pre_ffn_norm, f_w_in, f_conv_w, f_conv_b, f_w_out, post_ffn_norm, loss_target, m_pre_mix_norm, m_w_in, m_a_q_norm, m_a_w_uq, m_a_kv_norm, m_a_w_ukv, m_a_out_norm, m_b_q_norm, m_b_k_norm, m_b_out_norm, m_c_conv_w, m_c_conv_b, m_c_a_log, m_c_dt_bias, m_c_d_skip, m_c_out_norm, m_d_conv_w, m_d_a_log, m_d_dt_bias, m_d_out_norm, m_w_out, m_post_mix_norm, m_pre_ffn_norm, m_f_w_in, m_f_conv_w, m_f_conv_b, m_f_w_out, m_post_ffn_norm, v_pre_mix_norm, v_w_in, v_a_q_norm, v_a_w_uq, v_a_kv_norm, v_a_w_ukv, v_a_out_norm, v_b_q_norm, v_b_k_norm, v_b_out_norm, v_c_conv_w, v_c_conv_b, v_c_a_log, v_c_dt_bias, v_c_d_skip, v_c_out_norm, v_d_conv_w, v_d_a_log, v_d_dt_bias, v_d_out_norm, v_w_out, v_post_mix_norm, v_pre_ffn_norm, v_f_w_in, v_f_conv_w, v_f_conv_b, v_f_w_out, v_post_ffn_norm):
    given = dict(locals())
    w = {n: given[n] for n in _WEIGHTS}
    mom = {n: given["m_" + n] for n in _WEIGHTS}
    var = {n: given["v_" + n] for n in _WEIGHTS}
    me = 4 * lax.axis_index("x") + 2 * lax.axis_index("y") + lax.axis_index("c")
    layered = lambda names: [(l, n) for l in range(DEPTH) for n in names]

    gathered = exchange([(w[n].astype(MXU_DTYPE), True) for n in _BIG] + [(w[n], True) for n in _CONV], "gather_weights")
    P = {n: [w[n][l] for l in range(DEPTH)] for n in _REP}
    for n, got in zip(list(_BIG) + _CONV, gathered):
        P[n] = [_join(got[:, l], _BIG.get(n, 1)) for l in range(DEPTH)]

    loss, dx, grads = local_step(x[0], loss_target[0], P, MXU_DTYPE)
    loss = lax.psum(loss[0, 0], ("x", "y", "c"))

    small = _REP + _CONV
    full_shapes = [grads[l][n].shape for l, n in layered(small)]
    sends = [(jnp.stack([_split(grads[l][n], _BIG[n]) for l in range(DEPTH)], axis=1), False) for n in _BIG]
    *recv, gsmall = exchange(sends + [(_pack([grads[l][n] for l, n in layered(small)], SMALLG_ROWS), True)], "exchange_grads")

    out = {}
    for n, got in zip(_BIG, recv):
        out[n] = adamw(got, w[n], mom[n], var[n], "adamw_" + n)
    gsum = dict(zip(layered(small), _unpack(sum_slots(gsmall, "sum_small_grads"), full_shapes)))
    for l, n in layered(_CONV):
        cols = w[n].shape[2]
        gsum[(l, n)] = lax.dynamic_slice_in_dim(gsum[(l, n)], me * cols, cols, axis=1)
    held_shapes = [w[n].shape[1:] for _, n in layered(small)]
    pack_small = lambda d: _pack([d[n][l] for l, n in layered(small)], SMALLW_ROWS)[None]
    res_small = adamw(_pack([gsum[k] for k in layered(small)], SMALLW_ROWS)[None, None], pack_small(w), pack_small(mom),
                      pack_small(var), "adamw_small")
    per_kind = [dict(zip(layered(small), _unpack(buf[0], held_shapes))) for buf in res_small]
    for n in small:
        out[n] = [jnp.stack([per_kind[kind][(l, n)] for l in range(DEPTH)]) for kind in range(4)]
    return (loss, dx[None], *[out[n][kind] for kind in range(4) for n in _WEIGHTS])
```

```python
import functools
import math

import jax
import jax.numpy as jnp
from jax import lax
from jax.experimental import pallas as pl
from jax.experimental.pallas import tpu as pltpu

f32 = jnp.float32
MXU_DTYPE = jnp.bfloat16
WIRE_DTYPE = jnp.bfloat16
HI = lax.Precision.HIGHEST

D_MODEL = 1024
DEPTH = 2
GRID_W = 64
ROPE_BASE = 10000.0
EPS = 1e-6
A_Q_LORA, A_KV_LORA, A_ROPE, A_NOPE = 192, 128, 32, 64
SSD_CHUNK = 128
DN_CHUNK = 64
HEAD = 64
HPAD = 128
D_FF = 2816
FF_BLK = 256
N_DEV = 8
ADAM_LR, ADAM_B1, ADAM_B2, ADAM_EPS, ADAM_WD, ADAM_STEP = 0.001, 0.9, 0.999, 1e-08, 0.01, 10

V7X_VMEM_BYTES = 64 * 2 ** 20
VMEM_LIMIT = (V7X_VMEM_BYTES * 3) // 4

PIN = 3072
SEG_A, SEG_B, SEG_CX, SEG_DX, SEG_CZ, SEG_DZ, SEG_CDT, SEG_DAB = 0, 512, 1024, 1536, 2304, 2560, 2816, 2944


def _tile(n, pref):
    for t in (512, 256, 128, 64, 32, 16, 8):
        if t <= pref and n % t == 0:
            return t
    raise ValueError(f"no tile for {n}")


LANE = 128


def _tile_div(n, pref):
    if n <= pref:
        return n
    return max(d for d in range(LANE, pref + 1, LANE) if n % d == 0)


def _params(sem=None):
    return pltpu.CompilerParams(vmem_limit_bytes=VMEM_LIMIT, dimension_semantics=sem)


def _tup(r):
    return tuple(r) if isinstance(r, (tuple, list)) else (r,)


def matmul(a, b, form, out_dtype, name):
    if form == "nn":
        (M, K), N = a.shape, b.shape[1]
    elif form == "nt":
        (M, K), N = a.shape, b.shape[0]
    else:
        (K, M), N = a.shape, b.shape[1]
    if form == "tn":
        tm, tn, tk = _tile_div(M, 1024), _tile_div(N, 1408), _tile_div(K, 1024)
    else:
        tm, tn, tk = _tile_div(M, 512), _tile_div(N, 1536), _tile_div(K, 3072)
    nk = K // tk
    dims = {"nn": ((1,), (0,)), "nt": ((1,), (1,)), "tn": ((0,), (0,))}[form]

    def body(a_ref, b_ref, o_ref, *acc):
        part = lax.dot_general(a_ref[...].astype(MXU_DTYPE), b_ref[...].astype(MXU_DTYPE), (dims, ((), ())),
                               preferred_element_type=f32)
        if nk == 1:
            o_ref[...] = part.astype(out_dtype)
            return
        (acc_ref,) = acc
        k = pl.program_id(2)

        @pl.when(k == 0)
        def _():
            acc_ref[...] = part

        @pl.when((k > 0) & (k < nk - 1))
        def _():
            acc_ref[...] += part

        @pl.when(k == nk - 1)
        def _():
            o_ref[...] = (acc_ref[...] + part).astype(out_dtype)

    a_spec = pl.BlockSpec((tk, tm), lambda i, j, k: (k, i)) if form == "tn" else pl.BlockSpec((tm, tk), lambda i, j, k: (i, k))
    b_spec = pl.BlockSpec((tn, tk), lambda i, j, k: (j, k)) if form == "nt" else pl.BlockSpec((tk, tn), lambda i, j, k: (k, j))
    return pl.pallas_call(
        body, name=name, grid=(M // tm, N // tn, nk), in_specs=[a_spec, b_spec],
        out_specs=pl.BlockSpec((tm, tn), lambda i, j, k: (i, j)), out_shape=jax.ShapeDtypeStruct((M, N), out_dtype),
        scratch_shapes=[pltpu.VMEM((tm, tn), f32)] if nk > 1 else [], compiler_params=_params(("parallel", "parallel", "arbitrary")),
    )(a, b)


def _row_specs(rows, tm):
    return [pl.BlockSpec((tm, w), lambda i, b=b: (i, b)) for (_, w, b) in rows]


def _full_specs(params):
    return [pl.BlockSpec(p.shape, lambda i: (0, 0)) for p in params]


def rowwise(fn, rows, params, outs, name, tm=256):
    L = rows[0][0].shape[0]
    tm = _tile(L, tm)
    n_in = len(rows) + len(params)

    def body(*refs):
        res = _tup(fn(*[r[...].astype(f32) for r in refs[:n_in]]))
        for o_ref, r in zip(refs[n_in:], res, strict=True):
            o_ref[...] = r.astype(o_ref.dtype)

    res = pl.pallas_call(
        body, name=name, grid=(L // tm,), in_specs=_row_specs(rows, tm) + _full_specs(params),
        out_specs=[pl.BlockSpec((tm, w), lambda i: (i, 0)) for (w, _) in outs],
        out_shape=[jax.ShapeDtypeStruct((L, w), dt) for (w, dt) in outs], compiler_params=_params(("parallel",)),
    )(*[r[0] for r in rows], *params)
    return list(res)


def rowwise_vjp(fn, rows, params, cts, name, row_grads, param_grads, tm=256):
    L = rows[0][0].shape[0]
    tm = _tile(L, tm)
    nr, npar, nct = len(rows), len(params), len(cts)

    def body(*refs):
        i = pl.program_id(0)
        rv = [r[...].astype(f32) for r in refs[:nr]]
        pv = [r[...].astype(f32) for r in refs[nr:nr + npar]]
        cv = tuple(r[...].astype(f32) for r in refs[nr + npar:nr + npar + nct])
        out_refs = refs[nr + npar + nct:]

        def g(*diff):
            rr, pp = list(rv), list(pv)
            for k, v in zip(row_grads, diff[:len(row_grads)]):
                rr[k] = v
            for k, v in zip(param_grads, diff[len(row_grads):]):
                pp[k] = v
            return _tup(fn(*rr, *pp))

        _, vjp = jax.vjp(g, *[rv[k] for k in row_grads], *[pv[k] for k in param_grads])
        grads = vjp(cv)
        for o_ref, gval in zip(out_refs[:len(row_grads)], grads[:len(row_grads)]):
            o_ref[...] = gval
        for o_ref, gval in zip(out_refs[len(row_grads):], grads[len(row_grads):]):
            @pl.when(i == 0)
            def _(o_ref=o_ref, gval=gval):
                o_ref[...] = gval

            @pl.when(i > 0)
            def _(o_ref=o_ref, gval=gval):
                o_ref[...] += gval

    out_specs = [pl.BlockSpec((tm, rows[k][1]), lambda i: (i, 0)) for k in row_grads] + \
                [pl.BlockSpec(params[k].shape, lambda i: (0, 0)) for k in param_grads]
    out_shape = [jax.ShapeDtypeStruct((L, rows[k][1]), f32) for k in row_grads] + \
                [jax.ShapeDtypeStruct(params[k].shape, f32) for k in param_grads]
    res = pl.pallas_call(
        body, name=name, grid=(L // tm,), in_specs=_row_specs(rows, tm) + _full_specs(params) + _row_specs(cts, tm),
        out_specs=out_specs, out_shape=out_shape, compiler_params=_params(("arbitrary",)),
    )(*[r[0] for r in rows], *params, *[c[0] for c in cts])
    res = list(res)
    return res[:len(row_grads)], res[len(row_grads):]


def _rms(x, w, n=None):
    n = x.shape[-1] if n is None else n
    return x * lax.rsqrt(jnp.sum(x * x, axis=-1, keepdims=True) * (1.0 / n) + EPS) * w


def _silu(x):
    return x * jax.nn.sigmoid(x)


def _softplus(x):
    return jnp.maximum(x, 0.0) + jnp.log1p(jnp.exp(-jnp.abs(x)))


def _mm(a, b, dims, precision=None):
    if precision is None:
        a, b = a.astype(MXU_DTYPE), b.astype(MXU_DTYPE)
    return lax.dot_general(a, b, (dims, ((), ())), precision=precision, preferred_element_type=f32)


_NN, _NT, _TN = ((1,), (0,)), ((1,), (1,)), ((0,), (0,))


@functools.partial(jax.custom_vjp, nondiff_argnums=(1, 2))
def _roll(x, shift, axis):
    return pltpu.roll(x, shift, axis)


def _roll_fwd(x, shift, axis):
    return pltpu.roll(x, shift, axis), None


def _roll_bwd(shift, axis, _, ct):
    return (pltpu.roll(ct, (ct.shape[axis] - shift) % ct.shape[axis], axis),)


_roll.defvjp(_roll_fwd, _roll_bwd)


def _rope(x, cos, sin, lo, half):
    n = x.shape[-1]
    lane = lax.broadcasted_iota(jnp.int32, x.shape, x.ndim - 1) % HPAD - lo
    first = ((lane >= 0) & (lane < half)) | ((lane >= 2 * half) & (lane < 3 * half))
    rot = jnp.where(first, -_roll(x, n - half, x.ndim - 1), _roll(x, half, x.ndim - 1))
    return x * cos + rot * sin


def _heads(x, n, width=HEAD):
    return [x[:, h * width:(h + 1) * width] for h in range(n)]


def _pad_heads(hs):
    z = jnp.zeros_like(hs[0])
    return jnp.concatenate([t for h in hs for t in (h, z)], axis=-1)


def _unpad_heads(x, n):
    return jnp.concatenate([x[:, h * HPAD:h * HPAD + HEAD] for h in range(n)], axis=-1)


HALO = 8


def _conv_specs(tm, tc, c0, L):
    nh = tm // HALO
    last = L // HALO - 1
    return [pl.BlockSpec((tm, tc), lambda i, j: (i, c0 + j)),
            pl.BlockSpec((HALO, tc), lambda i, j: (jnp.maximum(i * nh - 1, 0), c0 + j)),
            pl.BlockSpec((HALO, tc), lambda i, j: (jnp.minimum((i + 1) * nh, last), c0 + j))]


def _shift3(x):
    n = x.shape[0]
    return _roll(x, 1, 0), _roll(x, n - 1, 0)


def conv_fwd(gfn, x, c0, ncol, tc, w, b, tco, out_dtype, name, tm=256):
    L = x.shape[0]
    tm = _tile(L, tm)
    ni = L // tm

    def body(x_ref, p_ref, n_ref, w_ref, b_ref, o_ref):
        i = pl.program_id(0)
        xv = x_ref[...].astype(f32)
        xp = jnp.where(i == 0, 0.0, p_ref[HALO - 1:HALO, :].astype(f32))
        xn = jnp.where(i == ni - 1, 0.0, n_ref[0:1, :].astype(f32))
        rid = lax.broadcasted_iota(jnp.int32, xv.shape, 0)
        dn, up = _shift3(xv)
        dn = jnp.where(rid == 0, xp, dn)
        up = jnp.where(rid == tm - 1, xn, up)
        c = w_ref[0:1, :] * dn + w_ref[1:2, :] * xv + w_ref[2:3, :] * up + b_ref[...]
        o_ref[...] = gfn(c).astype(o_ref.dtype)

    return pl.pallas_call(
        body, name=name, grid=(ni, ncol),
        in_specs=_conv_specs(tm, tc, c0, L) + [pl.BlockSpec((3, tc), lambda i, j: (0, j)), pl.BlockSpec((1, tc), lambda i, j: (0, j))],
        out_specs=pl.BlockSpec((tm, tco), lambda i, j: (i, j)), out_shape=jax.ShapeDtypeStruct((L, ncol * tco), out_dtype),
        compiler_params=_params(("parallel", "parallel")),
    )(x, x, x, w, b)


def conv_bwd(gfn, x, c0, ncol, tc, w, b, dys, tco, name, tm=256):
    L = x.shape[0]
    tm = _tile(L, tm)
    ni = L // tm
    nd = len(dys)

    def body(*refs):
        x_ref, xp_ref, xn_ref = refs[:3]
        d_refs = refs[3:3 + 3 * nd]
        w_ref, b_ref, dx_ref, dw_ref, db_ref = refs[3 + 3 * nd:]
        i = pl.program_id(1)
        first, lastb = i == 0, i == ni - 1

        def ext(m, p, n):
            return jnp.concatenate([jnp.where(first, 0.0, p[...].astype(f32)), m[...].astype(f32),
                                    jnp.where(lastb, 0.0, n[...].astype(f32))], axis=0)

        xe = ext(x_ref, xp_ref, xn_ref)
        de = ext(*d_refs[:3])
        for q in range(1, nd):
            de = de + ext(*d_refs[3 * q:3 * q + 3])
        x_dn, x_up = _shift3(xe)
        w0, w1, w2 = w_ref[0:1, :], w_ref[1:2, :], w_ref[2:3, :]
        ce = w0 * x_dn + w1 * xe + w2 * x_up + b_ref[...]
        _, vjp = jax.vjp(gfn, ce)
        (dce,) = vjp(de)
        dc_dn, dc_up = _shift3(dce)
        dx_ref[...] = (w0 * dc_up + w1 * dce + w2 * dc_dn)[HALO:HALO + tm, :]
        rid = lax.broadcasted_iota(jnp.int32, dce.shape, 0)
        dci = jnp.where((rid >= HALO) & (rid < HALO + tm), dce, 0.0)
        dw = jnp.concatenate([jnp.sum(dci * x_dn, axis=0, keepdims=True), jnp.sum(dci * xe, axis=0, keepdims=True),
                              jnp.sum(dci * x_up, axis=0, keepdims=True)], axis=0)
        db = jnp.sum(dci, axis=0, keepdims=True)

        @pl.when(first)
        def _():
            dw_ref[...] = dw
            db_ref[...] = db

        @pl.when(i > 0)
        def _():
            dw_ref[...] += dw
            db_ref[...] += db

    res = pl.pallas_call(
        body, name=name, grid=(ncol, ni),
        in_specs=[pl.BlockSpec(s.block_shape, (lambda j, i, f=s.index_map: f(i, j))) for s in _conv_specs(tm, tc, c0, L)]
        + [pl.BlockSpec(s.block_shape, (lambda j, i, f=s.index_map: f(i, j))) for s in _conv_specs(tm, tco, 0, L)] * nd
        + [pl.BlockSpec((3, tc), lambda j, i: (0, j)), pl.BlockSpec((1, tc), lambda j, i: (0, j))],
        out_specs=[pl.BlockSpec((tm, tc), lambda j, i: (i, j)), pl.BlockSpec((3, tc), lambda j, i: (0, j)),
                   pl.BlockSpec((1, tc), lambda j, i: (0, j))],
        out_shape=[jax.ShapeDtypeStruct((L, ncol * tc), f32), jax.ShapeDtypeStruct((3, ncol * tc), f32),
                   jax.ShapeDtypeStruct((1, ncol * tc), f32)],
        compiler_params=_params(("parallel", "arbitrary")),
    )(x, x, x, *[d for d in dys for _ in range(3)], w, b)
    return res


LOG2E = math.log2(math.e)


def flash_fwd(q, k, v, n_q, n_kv, scale, name, tq=512, tk=2048):
    L = q.shape[0]
    tq, tk = _tile_div(L, tq), _tile_div(L, tk)
    rep = n_q // n_kv
    nkv = L // tk
    c2 = scale * LOG2E

    def body(q_ref, k_ref, v_ref, o_ref, lse_ref):
        qv = q_ref[...]

        def step(c, carry):
            m, l, acc = carry
            off = pl.multiple_of(c * tk, tk)
            s = _mm(qv, k_ref[pl.ds(off, tk), :], _NT)
            m_new = jnp.maximum(m, jnp.max(s, axis=-1, keepdims=True))
            p = jnp.exp2((s - m_new) * c2)
            alpha = jnp.exp2((m - m_new) * c2)
            l = alpha * l + jnp.sum(p, axis=-1, keepdims=True)
            acc = alpha * acc + _mm(p, v_ref[pl.ds(off, tk), :], _NN)
            return m_new, l, acc

        m, l, acc = lax.fori_loop(0, nkv, step, (jnp.full((tq, 1), -jnp.inf, f32), jnp.zeros((tq, 1), f32),
                                                  jnp.zeros((tq, HPAD), f32)))
        o_ref[...] = acc / l
        lse_ref[0] = m * scale + jnp.log(l)

    return pl.pallas_call(
        body, name=name, grid=(n_q, L // tq),
        in_specs=[pl.BlockSpec((tq, HPAD), lambda h, i: (i, h)), pl.BlockSpec((L, HPAD), lambda h, i: (0, h // rep)),
                  pl.BlockSpec((L, HPAD), lambda h, i: (0, h // rep))],
        out_specs=[pl.BlockSpec((tq, HPAD), lambda h, i: (i, h)), pl.BlockSpec((1, tq, 1), lambda h, i: (h, i, 0))],
        out_shape=[jax.ShapeDtypeStruct((L, n_q * HPAD), f32), jax.ShapeDtypeStruct((n_q, L, 1), f32)],
        compiler_params=_params(("parallel", "parallel")),
    )(q, k, v)


def flash_bwd(q, k, v, o, lse, do, n_q, n_kv, scale, name, tq=1024, tk=1024):
    L = q.shape[0]
    tq, tk = _tile_div(L, tq), _tile_div(L, tk)
    rep = n_q // n_kv
    c2 = scale * LOG2E

    def body(k_ref, v_ref, q_ref, do_ref, o_ref, lse_ref, dq_ref, dk_ref, dv_ref):
        j, i = pl.program_id(1), pl.program_id(2)
        kv, vv = k_ref[...], v_ref[...]
        rows = pl.ds(pl.multiple_of(i * tq, tq), tq)
        dk, dv = jnp.zeros((tk, HPAD), f32), jnp.zeros((tk, HPAD), f32)
        for r in range(rep):
            cols = slice(r * HPAD, (r + 1) * HPAD)
            qv = q_ref[:, cols]
            dov = do_ref[:, cols]
            delta = jnp.sum(dov * o_ref[:, cols], axis=-1, keepdims=True)
            p = jnp.exp2(_mm(qv, kv, _NT) * c2 - lse_ref[r] * LOG2E)
            dv = dv + _mm(p, dov, _TN)
            ds = p * (_mm(dov, vv, _NT) - delta) * scale
            dk = dk + _mm(ds, qv, _TN)
            dq = _mm(ds, kv, _NN)

            @pl.when(j == 0)
            def _(dq=dq, cols=cols):
                dq_ref[rows, cols] = dq

            @pl.when(j > 0)
            def _(dq=dq, cols=cols):
                dq_ref[rows, cols] += dq

        @pl.when(i == 0)
        def _():
            dk_ref[...] = dk
            dv_ref[...] = dv

        @pl.when(i > 0)
        def _():
            dk_ref[...] += dk
            dv_ref[...] += dv

    w = rep * HPAD
    return pl.pallas_call(
        body, name=name, grid=(n_kv, L // tk, L // tq),
        in_specs=[pl.BlockSpec((tk, HPAD), lambda g, j, i: (j, g)), pl.BlockSpec((tk, HPAD), lambda g, j, i: (j, g)),
                  pl.BlockSpec((tq, w), lambda g, j, i: (i, g)), pl.BlockSpec((tq, w), lambda g, j, i: (i, g)),
                  pl.BlockSpec((tq, w), lambda g, j, i: (i, g)), pl.BlockSpec((rep, tq, 1), lambda g, j, i: (g, i, 0))],
        out_specs=[pl.BlockSpec((L, w), lambda g, j, i: (0, g)), pl.BlockSpec((tk, HPAD), lambda g, j, i: (j, g)),
                   pl.BlockSpec((tk, HPAD), lambda g, j, i: (j, g))],
        out_shape=[jax.ShapeDtypeStruct((L, n_q * HPAD), f32), jax.ShapeDtypeStruct((L, n_kv * HPAD), f32),
                   jax.ShapeDtypeStruct((L, n_kv * HPAD), f32)],
        compiler_params=_params(("parallel", "arbitrary", "arbitrary")),
    )(k, v, q, do, o, lse)


N_SCAN_HEADS = 4
STATE_ROWS = N_SCAN_HEADS * HEAD


def _tri(n, rev):
    i = lax.broadcasted_iota(jnp.int32, (n, n), 0)
    k = lax.broadcasted_iota(jnp.int32, (n, n), 1)
    return ((k >= i), (k > i)) if rev else ((k <= i), (k < i))


def _decay(acol, incl):
    n = acol.shape[0]
    m1 = jnp.broadcast_to(acol, (n, n))
    return jnp.exp(jnp.where(incl, m1 - m1.T, -jnp.inf))


def _ssd_chunk(S, xbc, win, a_log, dt_bias, rev):
    Q = xbc.shape[0]
    d = 1 if rev else 0
    incl, _ = _tri(Q, rev)
    dt = _softplus(win[:, 4 * d:4 * d + 4] + dt_bias[d:d + 1, :])
    a = dt * (-jnp.exp(a_log[d:d + 1, :]))
    acum = _mm(incl.astype(f32), a, _NN, HI)
    tot = jnp.sum(a, axis=0, keepdims=True)
    xs, Bm, Cm = xbc[:, :256], xbc[:, 256:384], xbc[:, 384:512]
    H, W = N_SCAN_HEADS, N_SCAN_HEADS * HEAD
    lanes = lambda x: _cat([jnp.broadcast_to(x[:, h:h + 1], (x.shape[0], HEAD)) for h in range(H)])
    xdt = xs * lanes(dt)
    cb = [_mm(Cm[:, g * HEAD:(g + 1) * HEAD], Bm[:, g * HEAD:(g + 1) * HEAD], _NT) for g in range(2)]
    scores = _cat([cb[h // 2] * _decay(acum[:, h:h + 1], incl) for h in range(H)])
    own = (lax.broadcasted_iota(jnp.int32, (H * Q, W), 0) // Q) == (lax.broadcasted_iota(jnp.int32, (H * Q, W), 1) // HEAD)
    y = _mm(scores, jnp.where(own, jnp.concatenate([xdt] * H, axis=0), 0.0), _NN)
    grp = (lax.broadcasted_iota(jnp.int32, (W, 2 * HEAD), 0) // (2 * HEAD)) == (lax.broadcasted_iota(jnp.int32, (W, 2 * HEAD), 1) // HEAD)
    y = y + _mm(Cm, jnp.where(grp, _cat([S, S]), 0.0), _NT) * jnp.exp(lanes(acum))
    st = _mm(xdt * jnp.exp(lanes(tot - acum)), Bm, _TN)
    first = lax.broadcasted_iota(jnp.int32, (W, HEAD), 0) < 2 * HEAD
    exp_tot = jnp.concatenate([jnp.broadcast_to(jnp.exp(tot[:, h:h + 1]), (HEAD, 1)) for h in range(H)], axis=0)
    return S * exp_tot + jnp.where(first, st[:, :HEAD], st[:, HEAD:]), y


@functools.partial(jax.custom_vjp, nondiff_argnums=(1,))
def _inv_unit_tri(Lm, order):
    n = Lm.shape[0]
    eye = (lax.broadcasted_iota(jnp.int32, (n, n), 0) == lax.broadcasted_iota(jnp.int32, (n, n), 1)).astype(f32)
    P = -Lm
    T = eye + P
    k = 1
    while 2 * k < order:
        P = _mm(P, P, _NN, lax.Precision.HIGH)
        T = T + _mm(T, P, _NN, lax.Precision.HIGH)
        k *= 2
    return T


def _inv_unit_tri_fwd(Lm, order):
    T = _inv_unit_tri(Lm, order)
    return T, T


def _inv_unit_tri_bwd(order, T, dT):
    return (-_mm(_mm(T, dT, _TN, lax.Precision.HIGH), T, _NT, lax.Precision.HIGH),)


_inv_unit_tri.defvjp(_inv_unit_tri_fwd, _inv_unit_tri_bwd)


def _dn_chunk(S, qkv, win, a_log, dt_bias, rev):
    Q = qkv.shape[0]
    H, R = N_SCAN_HEADS, N_SCAN_HEADS * qkv.shape[0]
    d = 1 if rev else 0
    beta = jax.nn.sigmoid(win[:, 4 * d:4 * d + 4])
    gl = -jnp.exp(a_log[d:d + 1, :]) * _softplus(win[:, 8 + 4 * d:12 + 4 * d] + dt_bias[d:d + 1, :])
    G = _mm(_tri(Q, rev)[0].astype(f32), gl, _NN, HI)
    tot = jnp.sum(gl, axis=0, keepdims=True)
    rows = lambda x: jnp.concatenate(_heads(x, H), axis=0)
    col = lambda x: jnp.concatenate([x[:, h:h + 1] for h in range(H)], axis=0)
    per_head = lambda x, n: jnp.concatenate([jnp.broadcast_to(x[:, h:h + 1], (n, 1)) for h in range(H)], axis=0)
    r = lax.broadcasted_iota(jnp.int32, (R, R), 0)
    c = lax.broadcasted_iota(jnp.int32, (R, R), 1)
    same = (r // Q) == (c // Q)
    incl = same & ((c >= r) if rev else (c <= r))
    strict = same & ((c > r) if rev else (c < r))
    own = (lax.broadcasted_iota(jnp.int32, (R, H * HEAD), 0) // Q) == (lax.broadcasted_iota(jnp.int32, (R, H * HEAD), 1) // HEAD)
    blk = lambda x: jnp.where(own, _cat([x] * H), 0.0)
    q, k, v = rows(qkv[:, :256]) * (HEAD ** -0.5), rows(qkv[:, 256:512]), rows(qkv[:, 512:])
    b, Gs, tots = col(beta), col(G), per_head(tot, Q)
    dec = _decay(Gs, incl)
    kb = k * b
    T = _inv_unit_tri(jnp.where(strict, _mm(kb, k, _NT) * dec, 0.0), Q)
    eG = jnp.exp(Gs)
    uw = _mm(T, _cat([v * b, kb * eG]), _NN)
    vnew = uw[:, :HEAD] - _mm(blk(uw[:, HEAD:]), S, _NN)
    o = _mm(blk(q * eG), S, _NN) + _mm(_mm(q, k, _NT) * dec, vnew, _NN)
    S_next = S * jnp.exp(per_head(tot, HEAD)) + _mm(blk(k * jnp.exp(tots - Gs)), vnew, _TN)
    return S_next, _cat([o[h * Q:(h + 1) * Q, :] for h in range(H)])


def scan_fwd(chunk_fn, Q, x, p, win_blk, a_log, dt_bias, name):
    L, W = x.shape
    nc = L // Q
    fidx, ridx = (lambda t: t), (lambda t: nc - 1 - t)

    def body(xf_ref, xr_ref, wf_ref, wr_ref, al_ref, db_ref, yf_ref, yr_ref, sf_ref, sr_ref, sf_scr, sr_scr):
        @pl.when(pl.program_id(0) == 0)
        def _():
            sf_scr[...] = jnp.zeros_like(sf_scr)
            sr_scr[...] = jnp.zeros_like(sr_scr)

        for rev, x_ref, w_ref, y_ref, sin_ref, s_scr in ((False, xf_ref, wf_ref, yf_ref, sf_ref, sf_scr),
                                                        (True, xr_ref, wr_ref, yr_ref, sr_ref, sr_scr)):
            S = s_scr[...]
            sin_ref[...] = S
            s_scr[...], y_ref[...] = chunk_fn(S, x_ref[...], w_ref[...], al_ref[...], db_ref[...], rev)

    row = lambda w, idx, b=0: pl.BlockSpec((Q, w), lambda t: (idx(t), b))
    small = pl.BlockSpec(a_log.shape, lambda t: (0, 0))
    state = lambda idx: pl.BlockSpec((STATE_ROWS, HEAD), lambda t: (idx(t), 0))
    return pl.pallas_call(
        body, name=name, grid=(nc,),
        in_specs=[row(W, fidx), row(W, ridx), row(HPAD, fidx, win_blk), row(HPAD, ridx, win_blk), small, small],
        out_specs=[row(STATE_ROWS, fidx), row(STATE_ROWS, ridx), state(fidx), state(ridx)],
        out_shape=[jax.ShapeDtypeStruct((L, STATE_ROWS), f32)] * 2 + [jax.ShapeDtypeStruct((nc * STATE_ROWS, HEAD), f32)] * 2,
        scratch_shapes=[pltpu.VMEM((STATE_ROWS, HEAD), f32)] * 2, compiler_params=_params(("arbitrary",)),
    )(x, x, p, p, a_log, dt_bias)


def scan_bwd(chunk_fn, Q, x, p, win_blk, a_log, dt_bias, s_fwd, s_rev, dy, name):
    L, W = x.shape
    nc = L // Q
    fidx, ridx = (lambda t: nc - 1 - t), (lambda t: t)

    def body(xf_ref, xr_ref, wf_ref, wr_ref, al_ref, db_ref, sf_ref, sr_ref, dyf_ref, dyr_ref,
             dxf_ref, dxr_ref, dwf_ref, dwr_ref, dal_ref, ddb_ref, dsf_scr, dsr_scr):
        @pl.when(pl.program_id(0) == 0)
        def _():
            dsf_scr[...] = jnp.zeros_like(dsf_scr)
            dsr_scr[...] = jnp.zeros_like(dsr_scr)
            dal_ref[...] = jnp.zeros_like(dal_ref)
            ddb_ref[...] = jnp.zeros_like(ddb_ref)

        dal_sum, ddb_sum = dal_ref[...], ddb_ref[...]
        for rev, x_ref, w_ref, sin_ref, dy_ref, dx_ref, dw_ref, ds_scr in (
                (False, xf_ref, wf_ref, sf_ref, dyf_ref, dxf_ref, dwf_ref, dsf_scr),
                (True, xr_ref, wr_ref, sr_ref, dyr_ref, dxr_ref, dwr_ref, dsr_scr)):
            _, vjp = jax.vjp(lambda S, xv, wv, al, db, rev=rev: chunk_fn(S, xv, wv, al, db, rev),
                             sin_ref[...], x_ref[...], w_ref[...], al_ref[...], db_ref[...])
            ds_scr[...], dx_ref[...], dw_ref[...], dal, ddb = vjp((ds_scr[...], dy_ref[...]))
            dal_sum, ddb_sum = dal_sum + dal, ddb_sum + ddb
        dal_ref[...] = dal_sum
        ddb_ref[...] = ddb_sum

    row = lambda w, idx, b=0: pl.BlockSpec((Q, w), lambda t: (idx(t), b))
    small = pl.BlockSpec(a_log.shape, lambda t: (0, 0))
    state = lambda idx: pl.BlockSpec((STATE_ROWS, HEAD), lambda t: (idx(t), 0))
    return pl.pallas_call(
        body, name=name, grid=(nc,),
        in_specs=[row(W, fidx), row(W, ridx), row(HPAD, fidx, win_blk), row(HPAD, ridx, win_blk), small, small,
                  state(fidx), state(ridx), row(STATE_ROWS, fidx), row(STATE_ROWS, ridx)],
        out_specs=[row(W, fidx), row(W, ridx), row(HPAD, fidx), row(HPAD, ridx), small, small],
        out_shape=[jax.ShapeDtypeStruct((L, W), f32)] * 2 + [jax.ShapeDtypeStruct((L, HPAD), f32)] * 2
        + [jax.ShapeDtypeStruct(a_log.shape, f32)] * 2,
        scratch_shapes=[pltpu.VMEM((STATE_ROWS, HEAD), f32)] * 2, compiler_params=_params(("arbitrary",)),
    )(x, x, p, p, a_log, dt_bias, s_fwd, s_rev, dy, dy)


def _cat(xs):
    return jnp.concatenate(xs, axis=-1)


def _a_prep(pa, cos, sin, qn, wuq, kvn, wk, wv):
    cq, ckv, kr = pa[:, :256], pa[:, 256:384], pa[:, 384:512]
    q = _rope(_mm(_rms(cq, qn, A_Q_LORA), wuq, _NN), _cat([cos] * 4), _cat([sin] * 4), A_NOPE, A_ROPE // 4)
    kvh = _rms(ckv, kvn)
    k = _mm(kvh, wk, _NN) + _cat([_rope(kr, cos, sin, A_NOPE, A_ROPE // 4)] * 4)
    return q, k, _mm(kvh, wv, _NN)


def _b_prep(pb, cos, sin, qn, kn):
    q = _pad_heads([_rms(h, qn) for h in _heads(pb[:, :256], 4)])
    k = _pad_heads([_rms(h, kn) for h in _heads(pb[:, 256:384], 2)])
    v = _pad_heads(_heads(pb[:, 384:512], 2))
    return _rope(q, _cat([cos] * 4), _cat([sin] * 4), 0, HEAD // 4), _rope(k, _cat([cos] * 2), _cat([sin] * 2), 0, HEAD // 4), v


def _c_act(c):
    return _silu(c)


def _d_act(c):
    s = _silu(c)
    return _cat([h * lax.rsqrt(jnp.sum(h * h, axis=-1, keepdims=True) + 1e-6) for h in _heads(s[:, :512], 8)] + [s[:, 512:]])


def _ffn_act(c):
    return _silu(c[:, :FF_BLK]) * c[:, FF_BLK:]


def _mix_post(oa, ob, yc_f, yc_r, xbc, cz, od_f, od_r, dz, a_out, b_out, dskip, c_out, d_out):
    o_a = _rms(_unpad_heads(oa, 4), a_out)
    o_b = _rms(_unpad_heads(ob, 4), b_out)
    skip = _cat([jnp.broadcast_to(dskip[:, h:h + 1], (1, HEAD)) for h in range(4)])
    o_c = _rms((yc_f + yc_r + xbc[:, :256] * skip) * _silu(cz), c_out)
    o_d = _cat([_rms(h, d_out) for h in _heads(od_f + od_r, 4)]) * _silu(dz)
    return _cat([o_a, o_b, o_c, o_d])


def _post_mix(x, y1, g_pm, g_pf):
    x1 = x + _rms(y1, g_pm)
    return x1, _rms(x1, g_pf)


def _post_ffn(x1, y2, g):
    return x1 + _rms(y2, g)


def _pre_bwd(x, g):
    return x, _rms(x, g)


def _layer_fwd(x, W, tabs, n):
    mx = MXU_DTYPE
    cos_a, sin_a, cos_b, sin_b = tabs
    full = lambda a: (a, a.shape[1], 0)
    s = {"x": x}
    (s["h"],) = rowwise(_rms, [full(x)], [W["g_pre"]], [(D_MODEL, mx)], n + "pre")
    p = s["p"] = matmul(s["h"], W["w_in"], "nn", f32, n + "in")
    s["qa"], s["ka"], s["va"] = rowwise(_a_prep, [(p, 512, SEG_A // 512), full(cos_a), full(sin_a)],
                                        [W["a_qn"], W["a_wuq"], W["a_kvn"], W["a_wk"], W["a_wv"]], [(512, mx)] * 3, n + "a_prep")
    s["qb"], s["kb"], s["vb"] = rowwise(_b_prep, [(p, 512, SEG_B // 512), full(cos_b), full(sin_b)], [W["b_qn"], W["b_kn"]],
                                        [(512, mx), (256, mx), (256, mx)], n + "b_prep")
    s["oa"], s["lsea"] = flash_fwd(s["qa"], s["ka"], s["va"], 4, 4, (A_NOPE + A_ROPE) ** -0.5, n + "a_attn")
    s["ob"], s["lseb"] = flash_fwd(s["qb"], s["kb"], s["vb"], 4, 2, HEAD ** -0.5, n + "b_attn")
    xbc = s["xbc"] = conv_fwd(_c_act, p, SEG_CX // 512, 1, 512, W["c_cw"], W["c_cb"], 512, f32, n + "c_conv")
    s["yc_f"], s["yc_r"], s["sc_f"], s["sc_r"] = scan_fwd(_ssd_chunk, SSD_CHUNK, xbc, p, SEG_CDT // HPAD, W["c_alog"], W["c_dtb"], n + "c_ssd")
    qkv = s["qkv"] = conv_fwd(_d_act, p, SEG_DX // 768, 1, 768, W["d_cw"], W["d_cb"], 768, f32, n + "d_conv")
    s["od_f"], s["od_r"], s["sd_f"], s["sd_r"] = scan_fwd(_dn_chunk, DN_CHUNK, qkv, p, SEG_DAB // HPAD, W["d_alog"], W["d_dtb"], n + "d_dn")
    (s["omix"],) = rowwise(_mix_post, _mix_rows(s), _mix_params(W), [(D_MODEL, mx)], n + "mix_post")
    s["y1"] = matmul(s["omix"], W["w_out"], "nn", f32, n + "out")
    s["x1"], s["h2"] = rowwise(_post_mix, [full(x), full(s["y1"])], [W["g_pm"], W["g_pf"]], [(D_MODEL, f32), (D_MODEL, mx)], n + "post_mix")
    s["u"] = matmul(s["h2"], W["f_win"], "nn", f32, n + "f_in")
    s["a"] = conv_fwd(_ffn_act, s["u"], 0, D_FF // FF_BLK, 2 * FF_BLK, W["f_cw"], W["f_cb"], FF_BLK, mx, n + "f_conv", tm=512)
    s["y2"] = matmul(s["a"], W["f_wout"], "nn", f32, n + "f_out")
    (x2,) = rowwise(_post_ffn, [full(s["x1"]), full(s["y2"])], [W["g_po"]], [(D_MODEL, f32)], n + "post_ffn")
    return x2, s


def _mix_rows(s):
    p = s["p"]
    return [(s["oa"], 512, 0), (s["ob"], 512, 0), (s["yc_f"], 256, 0), (s["yc_r"], 256, 0), (s["xbc"], 512, 0),
            (p, 256, SEG_CZ // 256), (s["od_f"], 256, 0), (s["od_r"], 256, 0), (p, 256, SEG_DZ // 256)]


def _mix_params(W):
    return [W["a_out"], W["b_out"], W["c_dskip"], W["c_out"], W["d_out"]]


def _layer_bwd(s, dx2, W, tabs, n):
    cos_a, sin_a, cos_b, sin_b = tabs
    full = lambda a: (a, a.shape[1], 0)
    p, g = s["p"], {}
    (dx1, dy2), (g["g_po"],) = rowwise_vjp(_post_ffn, [full(s["x1"]), full(s["y2"])], [W["g_po"]], [full(dx2)], n + "post_ffn_b", [0, 1], [0])
    da = matmul(dy2, W["f_wout"], "nt", f32, n + "f_out_dx")
    g["f_wout"] = matmul(s["a"], dy2, "tn", f32, n + "f_out_dw")
    du, g["f_cw"], g["f_cb"] = conv_bwd(_ffn_act, s["u"], 0, D_FF // FF_BLK, 2 * FF_BLK, W["f_cw"], W["f_cb"], [da], FF_BLK, n + "f_conv_b", tm=512)
    dh2 = matmul(du, W["f_win"], "nt", f32, n + "f_in_dx")
    g["f_win"] = matmul(s["h2"], du, "tn", f32, n + "f_in_dw")
    (dx, dy1), (g["g_pm"], g["g_pf"]) = rowwise_vjp(_post_mix, [full(s["x"]), full(s["y1"])], [W["g_pm"], W["g_pf"]],
                                                   [full(dx1), full(dh2)], n + "post_mix_b", [0, 1], [0, 1])
    domix = matmul(dy1, W["w_out"], "nt", f32, n + "out_dx")
    g["w_out"] = matmul(s["omix"], dy1, "tn", f32, n + "out_dw")
    (doa, dob, dyc, dxbc_skip, dcz, dod, ddz), (g["a_out"], g["b_out"], g["c_dskip"], g["c_out"], g["d_out"]) = rowwise_vjp(
        _mix_post, _mix_rows(s), _mix_params(W), [full(domix)], n + "mix_post_b", [0, 1, 2, 4, 5, 6, 8], list(range(5)))
    dqkv_f, dqkv_r, ddab_f, ddab_r, g["d_alog"], g["d_dtb"] = scan_bwd(
        _dn_chunk, DN_CHUNK, s["qkv"], p, SEG_DAB // HPAD, W["d_alog"], W["d_dtb"], s["sd_f"], s["sd_r"], dod, n + "d_dn_b")
    ddx, g["d_cw"], _ = conv_bwd(_d_act, p, SEG_DX // 768, 1, 768, W["d_cw"], W["d_cb"], [dqkv_f, dqkv_r], 768, n + "d_conv_b")
    dxbc_f, dxbc_r, dcdt_f, dcdt_r, g["c_alog"], g["c_dtb"] = scan_bwd(
        _ssd_chunk, SSD_CHUNK, s["xbc"], p, SEG_CDT // HPAD, W["c_alog"], W["c_dtb"], s["sc_f"], s["sc_r"], dyc, n + "c_ssd_b")
    dcx, g["c_cw"], g["c_cb"] = conv_bwd(_c_act, p, SEG_CX // 512, 1, 512, W["c_cw"], W["c_cb"], [dxbc_f, dxbc_r, dxbc_skip], 512,
                                         n + "c_conv_b")
    (dsmall,) = rowwise(lambda a, b, c, d: _cat([a + b, c + d]), [full(dcdt_f), full(dcdt_r), full(ddab_f), full(ddab_r)], [],
                        [(2 * HPAD, f32)], n + "dwin_sum")
    dqa, dka, dva = flash_bwd(s["qa"], s["ka"], s["va"], s["oa"], s["lsea"], doa, 4, 4, (A_NOPE + A_ROPE) ** -0.5, n + "a_attn_b")
    dqb, dkb, dvb = flash_bwd(s["qb"], s["kb"], s["vb"], s["ob"], s["lseb"], dob, 4, 2, HEAD ** -0.5, n + "b_attn_b")
    (dpa,), (g["a_qn"], g["a_wuq"], g["a_kvn"], g["a_wk"], g["a_wv"]) = rowwise_vjp(
        _a_prep, [(p, 512, SEG_A // 512), full(cos_a), full(sin_a)], [W["a_qn"], W["a_wuq"], W["a_kvn"], W["a_wk"], W["a_wv"]],
        [full(dqa), full(dka), full(dva)], n + "a_prep_b", [0], list(range(5)))
    (dpb,), (g["b_qn"], g["b_kn"]) = rowwise_vjp(_b_prep, [(p, 512, SEG_B // 512), full(cos_b), full(sin_b)], [W["b_qn"], W["b_kn"]],
                                               [full(dqb), full(dkb), full(dvb)], n + "b_prep_b", [0], [0, 1])
    dp = jnp.concatenate([dpa, dpb, dcx, ddx, dcz, ddz, dsmall], axis=1)
    dh = matmul(dp, W["w_in"], "nt", f32, n + "in_dx")
    g["w_in"] = matmul(s["h"], dp, "tn", f32, n + "in_dw")
    (dx0,), (g["g_pre"],) = rowwise_vjp(_pre_bwd, [full(s["x"])], [W["g_pre"]], [full(dx), full(dh)], n + "pre_b", [0], [0])
    return dx0, g


def loss_and_grad(y, target, name):
    L, D = y.shape
    tm = _tile(L, 512)

    def body(y_ref, t_ref, loss_ref, dy_ref):
        e = y_ref[...] - t_ref[...]
        dy_ref[...] = e * (1.0 / D)
        part = 0.5 * jnp.sum(jnp.sum(e * e, axis=1, keepdims=True) * (1.0 / D), axis=0, keepdims=True)

        @pl.when(pl.program_id(0) == 0)
        def _():
            loss_ref[...] = part

        @pl.when(pl.program_id(0) > 0)
        def _():
            loss_ref[...] += part

    row = pl.BlockSpec((tm, D), lambda i: (i, 0))
    return pl.pallas_call(
        body, name=name, grid=(L // tm,), in_specs=[row, row], out_specs=[pl.BlockSpec((1, 1), lambda i: (0, 0)), row],
        out_shape=[jax.ShapeDtypeStruct((1, 1), f32), jax.ShapeDtypeStruct((L, D), f32)], compiler_params=_params(("arbitrary",)),
    )(y, target)


_IN_SEGS = [(0, 192), (256, 128), (384 + A_NOPE, 32), (512, 256), (768, 128), (896, 128), (SEG_CZ, 256), (SEG_CX, 512),
            (SEG_CDT, 8), (SEG_DX, 768), (SEG_DZ, 256), (SEG_DAB, 16)]


def _pad_in(w):
    src, pieces = 0, {}
    for off, wd in _IN_SEGS:
        pieces[off] = w[..., src:src + wd]
        src += wd
    out, pos = [], 0
    for off in sorted(pieces):
        if off > pos:
            out.append(jnp.zeros(w.shape[:-1] + (off - pos,), w.dtype))
        out.append(pieces[off])
        pos = off + pieces[off].shape[-1]
    out.append(jnp.zeros(w.shape[:-1] + (PIN - pos,), w.dtype))
    return jnp.concatenate(out, axis=-1)


def _unpad_in(wp):
    return jnp.concatenate([wp[..., off:off + wd] for off, wd in _IN_SEGS], axis=-1)


def _pad_last(a, n):
    return jnp.pad(a, [(0, 0)] * (a.ndim - 1) + [(0, n - a.shape[-1])])


def _ff_interleave(w):
    lead = w.shape[:-1]
    return jnp.stack([w[..., :D_FF].reshape(lead + (D_FF // FF_BLK, FF_BLK)), w[..., D_FF:].reshape(lead + (D_FF // FF_BLK, FF_BLK))],
                     axis=-2).reshape(lead + (2 * D_FF,))


def _ff_deinterleave(w):
    lead = w.shape[:-1]
    t = w.reshape(lead + (D_FF // FF_BLK, 2, FF_BLK))
    return jnp.concatenate([t[..., 0, :].reshape(lead + (D_FF,)), t[..., 1, :].reshape(lead + (D_FF,))], axis=-1)


def _row(v):
    return v.reshape(1, -1)


def _kernel_weights(P, l, wdt):
    uq = P["a_w_uq"][l].reshape(A_Q_LORA, 4, A_NOPE + A_ROPE)
    ukv = P["a_w_ukv"][l].reshape(A_KV_LORA, 4, 2 * HEAD)
    z = jnp.zeros((A_KV_LORA, 4, HEAD), ukv.dtype)
    return {
        "g_pre": _row(P["pre_mix_norm"][l]), "w_in": _pad_in(P["w_in"][l]).astype(wdt),
        "a_qn": _pad_last(_row(P["a_q_norm"][l]), 256),
        "a_wuq": jnp.pad(_pad_last(uq, HPAD).reshape(A_Q_LORA, 4 * HPAD), ((0, 256 - A_Q_LORA), (0, 0))).astype(wdt),
        "a_kvn": _row(P["a_kv_norm"][l]),
        "a_wk": jnp.concatenate([ukv[..., :HEAD], z], axis=-1).reshape(A_KV_LORA, 4 * HPAD).astype(wdt),
        "a_wv": jnp.concatenate([ukv[..., HEAD:], z], axis=-1).reshape(A_KV_LORA, 4 * HPAD).astype(wdt),
        "a_out": _row(P["a_out_norm"][l]), "b_qn": _row(P["b_q_norm"][l]), "b_kn": _row(P["b_k_norm"][l]), "b_out": _row(P["b_out_norm"][l]),
        "c_cw": P["c_conv_w"][l], "c_cb": _row(P["c_conv_b"][l]), "c_alog": P["c_a_log"][l], "c_dtb": P["c_dt_bias"][l],
        "c_dskip": _row(P["c_d_skip"][l]), "c_out": _row(P["c_out_norm"][l]),
        "d_cw": P["d_conv_w"][l], "d_cb": jnp.zeros((1, 768), f32), "d_alog": P["d_a_log"][l], "d_dtb": P["d_dt_bias"][l],
        "d_out": _row(P["d_out_norm"][l]), "w_out": P["w_out"][l].astype(wdt), "g_pm": _row(P["post_mix_norm"][l]),
        "g_pf": _row(P["pre_ffn_norm"][l]), "f_win": _ff_interleave(P["f_w_in"][l]).astype(wdt),
        "f_cw": _ff_interleave(P["f_conv_w"][l]), "f_cb": _row(_ff_interleave(P["f_conv_b"][l])),
        "f_wout": P["f_w_out"][l].astype(wdt), "g_po": _row(P["post_ffn_norm"][l]),
    }


def _reference_grads(g):
    uq = g["a_wuq"][:A_Q_LORA].reshape(A_Q_LORA, 4, HPAD)[..., :A_NOPE + A_ROPE].reshape(A_Q_LORA, 4 * (A_NOPE + A_ROPE))
    wk = g["a_wk"].reshape(A_KV_LORA, 4, HPAD)[..., :HEAD]
    wv = g["a_wv"].reshape(A_KV_LORA, 4, HPAD)[..., :HEAD]
    return {
        "pre_mix_norm": g["g_pre"][0], "w_in": _unpad_in(g["w_in"]), "a_q_norm": g["a_qn"][0, :A_Q_LORA], "a_w_uq": uq,
        "a_kv_norm": g["a_kvn"][0], "a_w_ukv": jnp.concatenate([wk, wv], axis=-1).reshape(A_KV_LORA, 8 * HEAD),
        "a_out_norm": g["a_out"][0], "b_q_norm": g["b_qn"][0], "b_k_norm": g["b_kn"][0], "b_out_norm": g["b_out"][0],
        "c_conv_w": g["c_cw"], "c_conv_b": g["c_cb"][0], "c_a_log": g["c_alog"], "c_dt_bias": g["c_dtb"], "c_d_skip": g["c_dskip"][0],
        "c_out_norm": g["c_out"][0], "d_conv_w": g["d_cw"], "d_a_log": g["d_alog"], "d_dt_bias": g["d_dtb"], "d_out_norm": g["d_out"][0],
        "w_out": g["w_out"], "post_mix_norm": g["g_pm"][0], "pre_ffn_norm": g["g_pf"][0], "f_w_in": _ff_deinterleave(g["f_win"]),
        "f_conv_w": _ff_deinterleave(g["f_cw"]), "f_conv_b": _ff_deinterleave(g["f_cb"][0]), "f_w_out": g["f_wout"],
        "post_ffn_norm": g["g_po"][0],
    }


def _rope_tables(L):
    def tables(rot):
        rows = L // GRID_W
        row = jnp.repeat(jnp.arange(rows), GRID_W).astype(f32)
        col = jnp.tile(jnp.arange(GRID_W), rows).astype(f32)
        sec = rot // 2
        inv = ROPE_BASE ** (-jnp.arange(0, sec, 2, dtype=f32) / sec)
        ang = jnp.concatenate([row[:, None] * inv] * 2 + [col[:, None] * inv] * 2, axis=-1)
        return jnp.cos(ang), jnp.sin(ang)

    ca, sa = tables(A_ROPE)
    cb, sb = tables(HEAD)
    one, zero = jnp.ones, jnp.zeros
    return (jnp.concatenate([one((L, A_NOPE), f32), ca, one((L, HPAD - A_NOPE - A_ROPE), f32)], axis=1),
            jnp.concatenate([zero((L, A_NOPE), f32), sa, zero((L, HPAD - A_NOPE - A_ROPE), f32)], axis=1),
            jnp.concatenate([cb, one((L, HPAD - HEAD), f32)], axis=1), jnp.concatenate([sb, zero((L, HPAD - HEAD), f32)], axis=1))


def local_step(x, target, P, wdt):
    tabs = _rope_tables(x.shape[0])
    Ws = [_kernel_weights(P, l, wdt) for l in range(DEPTH)]
    saved = []
    for l in range(DEPTH):
        x, s = _layer_fwd(x, Ws[l], tabs, f"l{l}_")
        saved.append(s)
    loss, dx = loss_and_grad(x, target, "loss")
    grads = [None] * DEPTH
    for l in reversed(range(DEPTH)):
        dx, g = _layer_bwd(saved[l], dx, Ws[l], tabs, f"l{l}_")
        grads[l] = _reference_grads(g)
    return loss, dx, grads


def exchange(items, name):
    n = len(items)
    modes = [g for _, g in items]

    def body(*refs):
        src_refs, out_refs = refs[:n], refs[n:2 * n]
        send_sems, recv_sems, local_sems = refs[2 * n:]
        x, y, c = lax.axis_index("x"), lax.axis_index("y"), lax.axis_index("c")
        me = 4 * x + 2 * y + c

        def copy(a, k, arriving):
            px, py, pc = (1 - x if k & 4 else x), (1 - y if k & 2 else y), (1 - c if k & 1 else c)
            pid = 4 * px + 2 * py + pc
            sem = a * (N_DEV - 1) + k - 1
            return pltpu.make_async_remote_copy(
                src_ref=src_refs[a] if modes[a] else src_refs[a].at[pid], dst_ref=out_refs[a].at[pid if arriving else me],
                send_sem=send_sems.at[sem], recv_sem=recv_sems.at[sem], device_id=(px, py, pc), device_id_type=pl.DeviceIdType.MESH)

        local = [pltpu.make_async_copy(src_refs[a] if modes[a] else src_refs[a].at[me], out_refs[a].at[me], local_sems.at[a])
                 for a in range(n)]
        pairs = [(a, k) for k in range(1, N_DEV) for a in range(n)]
        for a, k in pairs:
            copy(a, k, False).start()
        for cp in local:
            cp.start()
        for a, k in pairs:
            copy(a, k, False).wait_send()
        for a, k in pairs:
            copy(a, k, True).wait_recv()
        for cp in local:
            cp.wait()

    hbm = pl.BlockSpec(memory_space=pl.ANY)
    res = pl.pallas_call(
        body, name=name, in_specs=[hbm] * n, out_specs=[hbm] * n,
        out_shape=[jax.ShapeDtypeStruct((N_DEV,) + tuple(s.shape if g else s.shape[1:]), s.dtype) for s, g in items],
        scratch_shapes=[pltpu.SemaphoreType.DMA((n * (N_DEV - 1),)), pltpu.SemaphoreType.DMA((n * (N_DEV - 1),)),
                        pltpu.SemaphoreType.DMA((n,))],
    )(*[s for s, _ in items])
    return list(res)


LANES = 1024


def adamw(gstack, w, m, v, name):
    S, D, R, C = gstack.shape
    tr = R if R <= 512 else _tile(R, 256)

    def body(g_ref, w_ref, m_ref, v_ref, go_ref, d_ref, mo_ref, vo_ref):
        g = g_ref[0].astype(f32)
        for sl in range(1, S):
            g = g + g_ref[sl].astype(f32)
        m_new = ADAM_B1 * m_ref[...] + (1.0 - ADAM_B1) * g
        v_new = ADAM_B2 * v_ref[...] + (1.0 - ADAM_B2) * jnp.square(g)
        m_hat = m_new / (1.0 - ADAM_B1 ** ADAM_STEP)
        v_hat = v_new / (1.0 - ADAM_B2 ** ADAM_STEP)
        go_ref[...] = g
        d_ref[...] = -ADAM_LR * (m_hat / (jnp.sqrt(v_hat) + ADAM_EPS) + ADAM_WD * w_ref[...])
        mo_ref[...] = m_new
        vo_ref[...] = v_new

    row = pl.BlockSpec((1, tr, C), lambda l, i: (l, i, 0))
    return pl.pallas_call(
        body, name=name, grid=(D, R // tr), in_specs=[pl.BlockSpec((S, 1, tr, C), lambda l, i: (0, l, i, 0)), row, row, row],
        out_specs=[row] * 4, out_shape=[jax.ShapeDtypeStruct((D, R, C), f32)] * 4, compiler_params=_params(("parallel", "parallel")),
    )(gstack, w, m, v)


def sum_slots(gstack, name):
    S, R, _ = gstack.shape

    def body(g_ref, o_ref):
        g = g_ref[0]
        for sl in range(1, S):
            g = g + g_ref[sl]
        o_ref[...] = g

    return pl.pallas_call(body, name=name, out_shape=jax.ShapeDtypeStruct((R, LANES), f32), compiler_params=_params())(gstack)


def _pack(parts, rows, dtype=f32):
    flat = jnp.concatenate([q.reshape(-1).astype(dtype) for q in parts])
    return jnp.pad(flat, (0, rows * LANES - flat.shape[0])).reshape(rows, LANES)


def _unpack(buf, shapes):
    lead = buf.shape[:-2]
    flat = buf.reshape(lead + (-1,))
    out, off = [], 0
    for shp in shapes:
        n = math.prod(shp)
        out.append(flat[..., off:off + n].reshape(lead + tuple(shp)))
        off += n
    return out


_WEIGHTS = ["pre_mix_norm", "w_in", "a_q_norm", "a_w_uq", "a_kv_norm", "a_w_ukv", "a_out_norm", "b_q_norm", "b_k_norm", "b_out_norm",
            "c_conv_w", "c_conv_b", "c_a_log", "c_dt_bias", "c_d_skip", "c_out_norm", "d_conv_w", "d_a_log", "d_dt_bias", "d_out_norm",
            "w_out", "post_mix_norm", "pre_ffn_norm", "f_w_in", "f_conv_w", "f_conv_b", "f_w_out", "post_ffn_norm"]
_BIG = {"w_in": 1, "a_w_uq": 1, "a_w_ukv": 1, "w_out": 0, "f_w_in": 1, "f_w_out": 0}
_CONV = ["c_conv_w", "d_conv_w", "f_conv_w"]
_REP = [n for n in _WEIGHTS if n not in _BIG and n not in _CONV]
SMALLG_ROWS, SMALLW_ROWS = 64, 32


def _join(blocks, axis):
    if axis == 0:
        return blocks.reshape(-1, blocks.shape[2])
    return blocks.transpose(1, 0, 2).reshape(blocks.shape[1], -1)


def _split(full, axis):
    if axis == 0:
        return full.reshape(N_DEV, -1, full.shape[1])
    return full.reshape(full.shape[0], N_DEV, -1).transpose(1, 0, 2)


def kernel(x, pre_mix_norm, w_in, a_q_norm, a_w_uq, a_kv_norm, a_w_ukv, a_out_norm, b_q_norm, b_k_norm, b_out_norm, c_conv_w, c_conv_b, c_a_log, c_dt_bias, c_d_skip, c_out_norm, d_conv_w, d_a_log, d_dt_bias, d_out_norm, w_out, post_mix_norm, pre_ffn_norm, f_w_in, f_conv_w, f_conv_b, f_w_out, post_ffn_norm, loss_target, m_pre_mix_norm, m_w_in, m_a_q_norm, m_a_w_uq, m_a_kv_norm, m_a_w_ukv, m_a_out_norm, m_b_q_norm, m_b_k_norm, m_b_out_norm, m_c_conv_w, m_c_conv_b, m_c_a_log, m_c_dt_bias, m_c_d_skip, m_c_out_norm, m_d_conv_w, m_d_a_log, m_d_dt_bias, m_d_out_norm, m_w_out, m_post_mix_norm, m_pre_ffn_norm, m_f_w_in, m_f_conv_w, m_f_conv_b, m_f_w_out, m_post_ffn_norm, v_pre_mix_norm, v_w_in, v_a_q_norm, v_a_w_uq, v_a_kv_norm, v_a_w_ukv, v_a_out_norm, v_b_q_norm, v_b_k_norm, v_b_out_norm, v_c_conv_w, v_c_conv_b, v_c_a_log, v_c_dt_bias, v_c_d_skip, v_c_out_norm, v_d_conv_w, v_d_a_log, v_d_dt_bias, v_d_out_norm, v_w_out, v_post_mix_norm, v_pre_ffn_norm, v_f_w_in, v_f_conv_w, v_f_conv_b, v_f_w_out, v_post_ffn_norm):
    given = dict(locals())
    w = {n: given[n] for n in _WEIGHTS}
    mom = {n: given["m_" + n] for n in _WEIGHTS}
    var = {n: given["v_" + n] for n in _WEIGHTS}
    me = 4 * lax.axis_index("x") + 2 * lax.axis_index("y") + lax.axis_index("c")
    layered = lambda names: [(l, n) for l in range(DEPTH) for n in names]

    gathered = exchange([(w[n].astype(MXU_DTYPE), True) for n in _BIG] + [(w[n], True) for n in _CONV], "gather_weights")
    P = {n: [w[n][l] for l in range(DEPTH)] for n in _REP}
    for n, got in zip(list(_BIG) + _CONV, gathered):
        P[n] = [_join(got[:, l], _BIG.get(n, 1)) for l in range(DEPTH)]

    loss, dx, grads = local_step(x[0], loss_target[0], P, MXU_DTYPE)
    loss = lax.psum(loss[0, 0], ("x", "y", "c"))

    small = _REP + _CONV
    full_shapes = [grads[l][n].shape for l, n in layered(small)]
    sends = [(jnp.stack([_split(grads[l][n], _BIG[n]) for l in range(DEPTH)], axis=1).astype(WIRE_DTYPE), False) for n in _BIG]
    *recv, gsmall = exchange(sends + [(_pack([grads[l][n] for l, n in layered(small)], SMALLG_ROWS), True)], "exchange_grads")

    out = {}
    for n, got in zip(_BIG, recv):
        out[n] = adamw(got, w[n], mom[n], var[n], "adamw_" + n)
    gsum = dict(zip(layered(small), _unpack(sum_slots(gsmall, "sum_small_grads"), full_shapes)))
    for l, n in layered(_CONV):
        cols = w[n].shape[2]
        gsum[(l, n)] = lax.dynamic_slice_in_dim(gsum[(l, n)], me * cols, cols, axis=1)
    held_shapes = [w[n].shape[1:] for _, n in layered(small)]
    pack_small = lambda d: _pack([d[n][l] for l, n in layered(small)], SMALLW_ROWS)[None]
    res_small = adamw(_pack([gsum[k] for k in layered(small)], SMALLW_ROWS)[None, None], pack_small(w), pack_small(mom),
                      pack_small(var), "adamw_small")
    per_kind = [dict(zip(layered(small), _unpack(buf[0], held_shapes))) for buf in res_small]
    for n in small:
        out[n] = [jnp.stack([per_kind[kind][(l, n)] for l in range(DEPTH)]) for kind in range(4)]
    return (loss, dx[None], *[out[n][kind] for kind in range(4) for n in _WEIGHTS])
```

```python
import functools
import math

import jax
import jax.numpy as jnp
from jax import lax
from jax.experimental import pallas as pl
from jax.experimental.pallas import tpu as pltpu

f32 = jnp.float32
MXU_DTYPE = jnp.bfloat16
WIRE_DTYPE = jnp.bfloat16
HI = lax.Precision.HIGHEST

D_MODEL = 1024
DEPTH = 2
GRID_W = 64
ROPE_BASE = 10000.0
EPS = 1e-6
A_Q_LORA, A_KV_LORA, A_ROPE, A_NOPE = 192, 128, 32, 64
SSD_CHUNK = 128
DN_CHUNK = 64
HEAD = 64
HPAD = 128
D_FF = 2816
FF_BLK = 256
N_DEV = 8
ADAM_LR, ADAM_B1, ADAM_B2, ADAM_EPS, ADAM_WD, ADAM_STEP = 0.001, 0.9, 0.999, 1e-08, 0.01, 10

V7X_VMEM_BYTES = 64 * 2 ** 20
VMEM_LIMIT = (V7X_VMEM_BYTES * 3) // 4

PIN = 3072
SEG_A, SEG_B, SEG_CX, SEG_DX, SEG_CZ, SEG_DZ, SEG_CDT, SEG_DAB = 0, 512, 1024, 1536, 2304, 2560, 2816, 2944


def _tile(n, pref):
    for t in (512, 256, 128, 64, 32, 16, 8):
        if t <= pref and n % t == 0:
            return t
    raise ValueError(f"no tile for {n}")


LANE = 128


def _tile_div(n, pref):
    if n <= pref:
        return n
    return max(d for d in range(LANE, pref + 1, LANE) if n % d == 0)


def _params(sem=None):
    return pltpu.CompilerParams(vmem_limit_bytes=VMEM_LIMIT, dimension_semantics=sem)


def _tup(r):
    return tuple(r) if isinstance(r, (tuple, list)) else (r,)


def matmul(a, b, form, out_dtype, name):
    if form == "nn":
        (M, K), N = a.shape, b.shape[1]
    elif form == "nt":
        (M, K), N = a.shape, b.shape[0]
    else:
        (K, M), N = a.shape, b.shape[1]
    if form == "tn":
        tm, tn, tk = _tile_div(M, 1408), _tile_div(N, 1408), _tile_div(K, 1024)
    else:
        tm, tn, tk = _tile_div(M, 512), _tile_div(N, 1536), _tile_div(K, 3072)
    nk = K // tk
    dims = {"nn": ((1,), (0,)), "nt": ((1,), (1,)), "tn": ((0,), (0,))}[form]

    def body(a_ref, b_ref, o_ref, *acc):
        part = lax.dot_general(a_ref[...].astype(MXU_DTYPE), b_ref[...].astype(MXU_DTYPE), (dims, ((), ())),
                               preferred_element_type=f32)
        if nk == 1:
            o_ref[...] = part.astype(out_dtype)
            return
        (acc_ref,) = acc
        k = pl.program_id(2)

        @pl.when(k == 0)
        def _():
            acc_ref[...] = part

        @pl.when((k > 0) & (k < nk - 1))
        def _():
            acc_ref[...] += part

        @pl.when(k == nk - 1)
        def _():
            o_ref[...] = (acc_ref[...] + part).astype(out_dtype)

    a_spec = pl.BlockSpec((tk, tm), lambda i, j, k: (k, i)) if form == "tn" else pl.BlockSpec((tm, tk), lambda i, j, k: (i, k))
    b_spec = pl.BlockSpec((tn, tk), lambda i, j, k: (j, k)) if form == "nt" else pl.BlockSpec((tk, tn), lambda i, j, k: (k, j))
    return pl.pallas_call(
        body, name=name, grid=(M // tm, N // tn, nk), in_specs=[a_spec, b_spec],
        out_specs=pl.BlockSpec((tm, tn), lambda i, j, k: (i, j)), out_shape=jax.ShapeDtypeStruct((M, N), out_dtype),
        scratch_shapes=[pltpu.VMEM((tm, tn), f32)] if nk > 1 else [], compiler_params=_params(("parallel", "parallel", "arbitrary")),
    )(a, b)


def _row_specs(rows, tm):
    return [pl.BlockSpec((tm, w), lambda i, b=b: (i, b)) for (_, w, b) in rows]


def _full_specs(params):
    return [pl.BlockSpec(p.shape, lambda i: (0, 0)) for p in params]


def rowwise(fn, rows, params, outs, name, tm=256):
    L = rows[0][0].shape[0]
    tm = _tile(L, tm)
    n_in = len(rows) + len(params)

    def body(*refs):
        res = _tup(fn(*[r[...].astype(f32) for r in refs[:n_in]]))
        for o_ref, r in zip(refs[n_in:], res, strict=True):
            o_ref[...] = r.astype(o_ref.dtype)

    res = pl.pallas_call(
        body, name=name, grid=(L // tm,), in_specs=_row_specs(rows, tm) + _full_specs(params),
        out_specs=[pl.BlockSpec((tm, w), lambda i: (i, 0)) for (w, _) in outs],
        out_shape=[jax.ShapeDtypeStruct((L, w), dt) for (w, dt) in outs], compiler_params=_params(("parallel",)),
    )(*[r[0] for r in rows], *params)
    return list(res)


def rowwise_vjp(fn, rows, params, cts, name, row_grads, param_grads, tm=256):
    L = rows[0][0].shape[0]
    tm = _tile(L, tm)
    nr, npar, nct = len(rows), len(params), len(cts)

    def body(*refs):
        i = pl.program_id(0)
        rv = [r[...].astype(f32) for r in refs[:nr]]
        pv = [r[...].astype(f32) for r in refs[nr:nr + npar]]
        cv = tuple(r[...].astype(f32) for r in refs[nr + npar:nr + npar + nct])
        out_refs = refs[nr + npar + nct:]

        def g(*diff):
            rr, pp = list(rv), list(pv)
            for k, v in zip(row_grads, diff[:len(row_grads)]):
                rr[k] = v
            for k, v in zip(param_grads, diff[len(row_grads):]):
                pp[k] = v
            return _tup(fn(*rr, *pp))

        _, vjp = jax.vjp(g, *[rv[k] for k in row_grads], *[pv[k] for k in param_grads])
        grads = vjp(cv)
        for o_ref, gval in zip(out_refs[:len(row_grads)], grads[:len(row_grads)]):
            o_ref[...] = gval
        for o_ref, gval in zip(out_refs[len(row_grads):], grads[len(row_grads):]):
            @pl.when(i == 0)
            def _(o_ref=o_ref, gval=gval):
                o_ref[...] = gval

            @pl.when(i > 0)
            def _(o_ref=o_ref, gval=gval):
                o_ref[...] += gval

    out_specs = [pl.BlockSpec((tm, rows[k][1]), lambda i: (i, 0)) for k in row_grads] + \
                [pl.BlockSpec(params[k].shape, lambda i: (0, 0)) for k in param_grads]
    out_shape = [jax.ShapeDtypeStruct((L, rows[k][1]), f32) for k in row_grads] + \
                [jax.ShapeDtypeStruct(params[k].shape, f32) for k in param_grads]
    res = pl.pallas_call(
        body, name=name, grid=(L // tm,), in_specs=_row_specs(rows, tm) + _full_specs(params) + _row_specs(cts, tm),
        out_specs=out_specs, out_shape=out_shape, compiler_params=_params(("arbitrary",)),
    )(*[r[0] for r in rows], *params, *[c[0] for c in cts])
    res = list(res)
    return res[:len(row_grads)], res[len(row_grads):]


def _rms(x, w, n=None):
    n = x.shape[-1] if n is None else n
    return x * lax.rsqrt(jnp.sum(x * x, axis=-1, keepdims=True) * (1.0 / n) + EPS) * w


def _silu(x):
    return x * jax.nn.sigmoid(x)


def _softplus(x):
    return jnp.maximum(x, 0.0) + jnp.log1p(jnp.exp(-jnp.abs(x)))


def _mm(a, b, dims, precision=None):
    if precision is None:
        a, b = a.astype(MXU_DTYPE), b.astype(MXU_DTYPE)
    return lax.dot_general(a, b, (dims, ((), ())), precision=precision, preferred_element_type=f32)


_NN, _NT, _TN = ((1,), (0,)), ((1,), (1,)), ((0,), (0,))


@functools.partial(jax.custom_vjp, nondiff_argnums=(1, 2))
def _roll(x, shift, axis):
    return pltpu.roll(x, shift, axis)


def _roll_fwd(x, shift, axis):
    return pltpu.roll(x, shift, axis), None


def _roll_bwd(shift, axis, _, ct):
    return (pltpu.roll(ct, (ct.shape[axis] - shift) % ct.shape[axis], axis),)


_roll.defvjp(_roll_fwd, _roll_bwd)


def _rope(x, cos, sin, lo, half):
    n = x.shape[-1]
    lane = lax.broadcasted_iota(jnp.int32, x.shape, x.ndim - 1) % HPAD - lo
    first = ((lane >= 0) & (lane < half)) | ((lane >= 2 * half) & (lane < 3 * half))
    rot = jnp.where(first, -_roll(x, n - half, x.ndim - 1), _roll(x, half, x.ndim - 1))
    return x * cos + rot * sin


def _heads(x, n, width=HEAD):
    return [x[:, h * width:(h + 1) * width] for h in range(n)]


def _pad_heads(hs):
    z = jnp.zeros_like(hs[0])
    return jnp.concatenate([t for h in hs for t in (h, z)], axis=-1)


def _unpad_heads(x, n):
    return jnp.concatenate([x[:, h * HPAD:h * HPAD + HEAD] for h in range(n)], axis=-1)


HALO = 8


def _conv_specs(tm, tc, c0, L):
    nh = tm // HALO
    last = L // HALO - 1
    return [pl.BlockSpec((tm, tc), lambda i, j: (i, c0 + j)),
            pl.BlockSpec((HALO, tc), lambda i, j: (jnp.maximum(i * nh - 1, 0), c0 + j)),
            pl.BlockSpec((HALO, tc), lambda i, j: (jnp.minimum((i + 1) * nh, last), c0 + j))]


def _shift3(x):
    n = x.shape[0]
    return _roll(x, 1, 0), _roll(x, n - 1, 0)


def conv_fwd(gfn, x, c0, ncol, tc, w, b, tco, out_dtype, name, tm=256):
    L = x.shape[0]
    tm = _tile(L, tm)
    ni = L // tm

    def body(x_ref, p_ref, n_ref, w_ref, b_ref, o_ref):
        i = pl.program_id(0)
        xv = x_ref[...].astype(f32)
        xp = jnp.where(i == 0, 0.0, p_ref[HALO - 1:HALO, :].astype(f32))
        xn = jnp.where(i == ni - 1, 0.0, n_ref[0:1, :].astype(f32))
        rid = lax.broadcasted_iota(jnp.int32, xv.shape, 0)
        dn, up = _shift3(xv)
        dn = jnp.where(rid == 0, xp, dn)
        up = jnp.where(rid == tm - 1, xn, up)
        c = w_ref[0:1, :] * dn + w_ref[1:2, :] * xv + w_ref[2:3, :] * up + b_ref[...]
        o_ref[...] = gfn(c).astype(o_ref.dtype)

    return pl.pallas_call(
        body, name=name, grid=(ni, ncol),
        in_specs=_conv_specs(tm, tc, c0, L) + [pl.BlockSpec((3, tc), lambda i, j: (0, j)), pl.BlockSpec((1, tc), lambda i, j: (0, j))],
        out_specs=pl.BlockSpec((tm, tco), lambda i, j: (i, j)), out_shape=jax.ShapeDtypeStruct((L, ncol * tco), out_dtype),
        compiler_params=_params(("parallel", "parallel")),
    )(x, x, x, w, b)


def conv_bwd(gfn, x, c0, ncol, tc, w, b, dys, tco, name, tm=256):
    L = x.shape[0]
    tm = _tile(L, tm)
    ni = L // tm
    nd = len(dys)

    def body(*refs):
        x_ref, xp_ref, xn_ref = refs[:3]
        d_refs = refs[3:3 + 3 * nd]
        w_ref, b_ref, dx_ref, dw_ref, db_ref = refs[3 + 3 * nd:]
        i = pl.program_id(1)
        first, lastb = i == 0, i == ni - 1

        def ext(m, p, n):
            return jnp.concatenate([jnp.where(first, 0.0, p[...].astype(f32)), m[...].astype(f32),
                                    jnp.where(lastb, 0.0, n[...].astype(f32))], axis=0)

        xe = ext(x_ref, xp_ref, xn_ref)
        de = ext(*d_refs[:3])
        for q in range(1, nd):
            de = de + ext(*d_refs[3 * q:3 * q + 3])
        x_dn, x_up = _shift3(xe)
        w0, w1, w2 = w_ref[0:1, :], w_ref[1:2, :], w_ref[2:3, :]
        ce = w0 * x_dn + w1 * xe + w2 * x_up + b_ref[...]
        _, vjp = jax.vjp(gfn, ce)
        (dce,) = vjp(de)
        dc_dn, dc_up = _shift3(dce)
        dx_ref[...] = (w0 * dc_up + w1 * dce + w2 * dc_dn)[HALO:HALO + tm, :]
        rid = lax.broadcasted_iota(jnp.int32, dce.shape, 0)
        dci = jnp.where((rid >= HALO) & (rid < HALO + tm), dce, 0.0)
        dw = jnp.concatenate([jnp.sum(dci * x_dn, axis=0, keepdims=True), jnp.sum(dci * xe, axis=0, keepdims=True),
                              jnp.sum(dci * x_up, axis=0, keepdims=True)], axis=0)
        db = jnp.sum(dci, axis=0, keepdims=True)

        @pl.when(first)
        def _():
            dw_ref[...] = dw
            db_ref[...] = db

        @pl.when(i > 0)
        def _():
            dw_ref[...] += dw
            db_ref[...] += db

    res = pl.pallas_call(
        body, name=name, grid=(ncol, ni),
        in_specs=[pl.BlockSpec(s.block_shape, (lambda j, i, f=s.index_map: f(i, j))) for s in _conv_specs(tm, tc, c0, L)]
        + [pl.BlockSpec(s.block_shape, (lambda j, i, f=s.index_map: f(i, j))) for s in _conv_specs(tm, tco, 0, L)] * nd
        + [pl.BlockSpec((3, tc), lambda j, i: (0, j)), pl.BlockSpec((1, tc), lambda j, i: (0, j))],
        out_specs=[pl.BlockSpec((tm, tc), lambda j, i: (i, j)), pl.BlockSpec((3, tc), lambda j, i: (0, j)),
                   pl.BlockSpec((1, tc), lambda j, i: (0, j))],
        out_shape=[jax.ShapeDtypeStruct((L, ncol * tc), f32), jax.ShapeDtypeStruct((3, ncol * tc), f32),
                   jax.ShapeDtypeStruct((1, ncol * tc), f32)],
        compiler_params=_params(("parallel", "arbitrary")),
    )(x, x, x, *[d for d in dys for _ in range(3)], w, b)
    return res


LOG2E = math.log2(math.e)


def flash_fwd(q, k, v, n_q, n_kv, scale, name, side=(), tq=512, tk=8192):
    L = q.shape[0]
    tq, tk = _tile_div(L, tq), _tile_div(L, tk)
    rep = n_q // n_kv
    nkv = L // tk
    c2 = scale * LOG2E

    def body(q_ref, k_ref, v_ref, o_ref, lse_ref):
        qv = q_ref[...]
        ones_lane = lax.broadcasted_iota(jnp.int32, (tk, HPAD), 1) == HPAD - 1

        def step(c, carry):
            m, acc = carry
            off = pl.multiple_of(c * tk, tk)
            s = _mm(qv, k_ref[pl.ds(off, tk), :], _NT)
            m_new = jnp.maximum(m, jnp.max(s, axis=-1, keepdims=True))
            p = jnp.exp2((s - m_new) * c2)
            vv = v_ref[pl.ds(off, tk), :]
            acc = jnp.exp2((m - m_new) * c2) * acc + _mm(p, jnp.where(ones_lane, jnp.ones_like(vv), vv), _NN)
            return m_new, acc

        m, acc = lax.fori_loop(0, nkv, step, (jnp.full((tq, 1), -jnp.inf, f32), jnp.zeros((tq, HPAD), f32)))
        l = acc[:, HPAD - 1:]
        o_ref[...] = jnp.where(lax.broadcasted_iota(jnp.int32, (tq, HPAD), 1) == HPAD - 1, 0.0, acc / l)
        lse_ref[0] = m * scale + jnp.log(l)

    grid = (n_q, L // tq)
    body, ex = _with_exchange(body, 3, 2, side, grid)
    res = pl.pallas_call(
        body, name=name, grid=grid,
        in_specs=[pl.BlockSpec((tq, HPAD), lambda h, i: (i, h)), pl.BlockSpec((L, HPAD), lambda h, i: (0, h // rep)),
                  pl.BlockSpec((L, HPAD), lambda h, i: (0, h // rep))] + ex.specs,
        out_specs=[pl.BlockSpec((tq, HPAD), lambda h, i: (i, h)), pl.BlockSpec((1, tq, 1), lambda h, i: (h, i, 0))] + ex.specs,
        out_shape=[jax.ShapeDtypeStruct((L, n_q * HPAD), f32), jax.ShapeDtypeStruct((n_q, L, 1), f32)] + ex.out_shape,
        scratch_shapes=ex.scratch, compiler_params=_params(("arbitrary", "arbitrary") if side else ("parallel", "parallel")),
    )(q, k, v, *ex.args)
    return res[0], res[1], list(res[2:])


def flash_bwd(q, k, v, o, lse, do, n_q, n_kv, scale, name, side=(), tq=1024, tk=1024):
    L = q.shape[0]
    tq, tk = _tile_div(L, tq), _tile_div(L, tk)
    rep = n_q // n_kv
    c2 = scale * LOG2E

    def body(k_ref, v_ref, q_ref, do_ref, o_ref, lse_ref, dq_ref, dk_ref, dv_ref):
        j, i = pl.program_id(1), pl.program_id(2)
        kv, vv = k_ref[...], v_ref[...]
        rows = pl.ds(pl.multiple_of(i * tq, tq), tq)
        dk, dv = jnp.zeros((tk, HPAD), f32), jnp.zeros((tk, HPAD), f32)
        for r in range(rep):
            cols = slice(r * HPAD, (r + 1) * HPAD)
            qv = q_ref[:, cols]
            dov = do_ref[:, cols]
            delta = jnp.sum(dov * o_ref[:, cols], axis=-1, keepdims=True)
            p = jnp.exp2(_mm(qv, kv, _NT) * c2 - lse_ref[r] * LOG2E)
            dv = dv + _mm(p, dov, _TN)
            ds = p * (_mm(dov, vv, _NT) - delta) * scale
            dk = dk + _mm(ds, qv, _TN)
            dq = _mm(ds, kv, _NN)

            @pl.when(j == 0)
            def _(dq=dq, cols=cols):
                dq_ref[rows, cols] = dq

            @pl.when(j > 0)
            def _(dq=dq, cols=cols):
                dq_ref[rows, cols] += dq

        @pl.when(i == 0)
        def _():
            dk_ref[...] = dk
            dv_ref[...] = dv

        @pl.when(i > 0)
        def _():
            dk_ref[...] += dk
            dv_ref[...] += dv

    w = rep * HPAD
    grid = (n_kv, L // tk, L // tq)
    body, ex = _with_exchange(body, 6, 3, side, grid)
    res = pl.pallas_call(
        body, name=name, grid=grid,
        in_specs=[pl.BlockSpec((tk, HPAD), lambda g, j, i: (j, g)), pl.BlockSpec((tk, HPAD), lambda g, j, i: (j, g)),
                  pl.BlockSpec((tq, w), lambda g, j, i: (i, g)), pl.BlockSpec((tq, w), lambda g, j, i: (i, g)),
                  pl.BlockSpec((tq, w), lambda g, j, i: (i, g)), pl.BlockSpec((rep, tq, 1), lambda g, j, i: (g, i, 0))] + ex.specs,
        out_specs=[pl.BlockSpec((L, w), lambda g, j, i: (0, g)), pl.BlockSpec((tk, HPAD), lambda g, j, i: (j, g)),
                   pl.BlockSpec((tk, HPAD), lambda g, j, i: (j, g))] + ex.specs,
        out_shape=[jax.ShapeDtypeStruct((L, n_q * HPAD), f32), jax.ShapeDtypeStruct((L, n_kv * HPAD), f32),
                   jax.ShapeDtypeStruct((L, n_kv * HPAD), f32)] + ex.out_shape,
        scratch_shapes=ex.scratch, compiler_params=_params(("arbitrary",) * 3 if side else ("parallel", "arbitrary", "arbitrary")),
    )(k, v, q, do, o, lse, *ex.args)
    return res[0], res[1], res[2], list(res[3:])


N_SCAN_HEADS = 4
STATE_ROWS = N_SCAN_HEADS * HEAD


def _tri(n, rev):
    i = lax.broadcasted_iota(jnp.int32, (n, n), 0)
    k = lax.broadcasted_iota(jnp.int32, (n, n), 1)
    return ((k >= i), (k > i)) if rev else ((k <= i), (k < i))


def _decay(acol, incl):
    n = acol.shape[0]
    m1 = jnp.broadcast_to(acol, (n, n))
    return jnp.exp(jnp.where(incl, m1 - m1.T, -jnp.inf))


def _ssd_chunk(S, xbc, win, a_log, dt_bias, rev):
    Q = xbc.shape[0]
    d = 1 if rev else 0
    incl, _ = _tri(Q, rev)
    dt = _softplus(win[:, 4 * d:4 * d + 4] + dt_bias[d:d + 1, :])
    a = dt * (-jnp.exp(a_log[d:d + 1, :]))
    acum = _mm(incl.astype(f32), a, _NN, HI)
    tot = jnp.sum(a, axis=0, keepdims=True)
    xs, Bm, Cm = xbc[:, :256], xbc[:, 256:384], xbc[:, 384:512]
    H, W = N_SCAN_HEADS, N_SCAN_HEADS * HEAD
    lanes = lambda x: _cat([jnp.broadcast_to(x[:, h:h + 1], (x.shape[0], HEAD)) for h in range(H)])
    xdt = xs * lanes(dt)
    cb = [_mm(Cm[:, g * HEAD:(g + 1) * HEAD], Bm[:, g * HEAD:(g + 1) * HEAD], _NT) for g in range(2)]
    scores = _cat([cb[h // 2] * _decay(acum[:, h:h + 1], incl) for h in range(H)])
    own = (lax.broadcasted_iota(jnp.int32, (H * Q, W), 0) // Q) == (lax.broadcasted_iota(jnp.int32, (H * Q, W), 1) // HEAD)
    y = _mm(scores, jnp.where(own, jnp.concatenate([xdt] * H, axis=0), 0.0), _NN)
    grp = (lax.broadcasted_iota(jnp.int32, (W, 2 * HEAD), 0) // (2 * HEAD)) == (lax.broadcasted_iota(jnp.int32, (W, 2 * HEAD), 1) // HEAD)
    y = y + _mm(Cm, jnp.where(grp, _cat([S, S]), 0.0), _NT) * jnp.exp(lanes(acum))
    st = _mm(xdt * jnp.exp(lanes(tot - acum)), Bm, _TN)
    first = lax.broadcasted_iota(jnp.int32, (W, HEAD), 0) < 2 * HEAD
    exp_tot = jnp.concatenate([jnp.broadcast_to(jnp.exp(tot[:, h:h + 1]), (HEAD, 1)) for h in range(H)], axis=0)
    return S * exp_tot + jnp.where(first, st[:, :HEAD], st[:, HEAD:]), y


@functools.partial(jax.custom_vjp, nondiff_argnums=(1,))
def _inv_unit_tri(Lm, order):
    n = Lm.shape[0]
    eye = (lax.broadcasted_iota(jnp.int32, (n, n), 0) == lax.broadcasted_iota(jnp.int32, (n, n), 1)).astype(f32)
    P = -Lm
    T = eye + P
    k = 1
    while 2 * k < order:
        P = _mm(P, P, _NN, lax.Precision.HIGH)
        T = T + _mm(T, P, _NN, lax.Precision.HIGH)
        k *= 2
    return T


def _inv_unit_tri_fwd(Lm, order):
    T = _inv_unit_tri(Lm, order)
    return T, T


def _inv_unit_tri_bwd(order, T, dT):
    return (-_mm(_mm(T, dT, _TN, lax.Precision.HIGH), T, _NT, lax.Precision.HIGH),)


_inv_unit_tri.defvjp(_inv_unit_tri_fwd, _inv_unit_tri_bwd)


def _dn_chunk(S, qkv, win, a_log, dt_bias, rev):
    Q = qkv.shape[0]
    H, R = N_SCAN_HEADS, N_SCAN_HEADS * qkv.shape[0]
    d = 1 if rev else 0
    beta = jax.nn.sigmoid(win[:, 4 * d:4 * d + 4])
    gl = -jnp.exp(a_log[d:d + 1, :]) * _softplus(win[:, 8 + 4 * d:12 + 4 * d] + dt_bias[d:d + 1, :])
    G = _mm(_tri(Q, rev)[0].astype(f32), gl, _NN, HI)
    tot = jnp.sum(gl, axis=0, keepdims=True)
    rows = lambda x: jnp.concatenate(_heads(x, H), axis=0)
    col = lambda x: jnp.concatenate([x[:, h:h + 1] for h in range(H)], axis=0)
    per_head = lambda x, n: jnp.concatenate([jnp.broadcast_to(x[:, h:h + 1], (n, 1)) for h in range(H)], axis=0)
    r = lax.broadcasted_iota(jnp.int32, (R, R), 0)
    c = lax.broadcasted_iota(jnp.int32, (R, R), 1)
    same = (r // Q) == (c // Q)
    incl = same & ((c >= r) if rev else (c <= r))
    strict = same & ((c > r) if rev else (c < r))
    own = (lax.broadcasted_iota(jnp.int32, (R, H * HEAD), 0) // Q) == (lax.broadcasted_iota(jnp.int32, (R, H * HEAD), 1) // HEAD)
    blk = lambda x: jnp.where(own, _cat([x] * H), 0.0)
    q, k, v = rows(qkv[:, :256]) * (HEAD ** -0.5), rows(qkv[:, 256:512]), rows(qkv[:, 512:])
    b, Gs, tots = col(beta), col(G), per_head(tot, Q)
    dec = _decay(Gs, incl)
    kb = k * b
    T = _inv_unit_tri(jnp.where(strict, _mm(kb, k, _NT) * dec, 0.0), Q)
    eG = jnp.exp(Gs)
    uw = _mm(T, _cat([v * b, kb * eG]), _NN)
    vnew = uw[:, :HEAD] - _mm(blk(uw[:, HEAD:]), S, _NN)
    o = _mm(blk(q * eG), S, _NN) + _mm(_mm(q, k, _NT) * dec, vnew, _NN)
    S_next = S * jnp.exp(per_head(tot, HEAD)) + _mm(blk(k * jnp.exp(tots - Gs)), vnew, _TN)
    return S_next, _cat([o[h * Q:(h + 1) * Q, :] for h in range(H)])


def scan_fwd(chunk_fn, Q, x, p, win_blk, a_log, dt_bias, name):
    L, W = x.shape
    nc = L // Q
    fidx, ridx = (lambda t: t), (lambda t: nc - 1 - t)

    def body(xf_ref, xr_ref, wf_ref, wr_ref, al_ref, db_ref, yf_ref, yr_ref, sf_ref, sr_ref, sf_scr, sr_scr):
        @pl.when(pl.program_id(0) == 0)
        def _():
            sf_scr[...] = jnp.zeros_like(sf_scr)
            sr_scr[...] = jnp.zeros_like(sr_scr)

        for rev, x_ref, w_ref, y_ref, sin_ref, s_scr in ((False, xf_ref, wf_ref, yf_ref, sf_ref, sf_scr),
                                                        (True, xr_ref, wr_ref, yr_ref, sr_ref, sr_scr)):
            S = s_scr[...]
            sin_ref[...] = S
            s_scr[...], y_ref[...] = chunk_fn(S, x_ref[...], w_ref[...], al_ref[...], db_ref[...], rev)

    row = lambda w, idx, b=0: pl.BlockSpec((Q, w), lambda t: (idx(t), b))
    small = pl.BlockSpec(a_log.shape, lambda t: (0, 0))
    state = lambda idx: pl.BlockSpec((STATE_ROWS, HEAD), lambda t: (idx(t), 0))
    return pl.pallas_call(
        body, name=name, grid=(nc,),
        in_specs=[row(W, fidx), row(W, ridx), row(HPAD, fidx, win_blk), row(HPAD, ridx, win_blk), small, small],
        out_specs=[row(STATE_ROWS, fidx), row(STATE_ROWS, ridx), state(fidx), state(ridx)],
        out_shape=[jax.ShapeDtypeStruct((L, STATE_ROWS), f32)] * 2 + [jax.ShapeDtypeStruct((nc * STATE_ROWS, HEAD), f32)] * 2,
        scratch_shapes=[pltpu.VMEM((STATE_ROWS, HEAD), f32)] * 2, compiler_params=_params(("arbitrary",)),
    )(x, x, p, p, a_log, dt_bias)


def scan_bwd(chunk_fn, Q, x, p, win_blk, a_log, dt_bias, s_fwd, s_rev, dy, name):
    L, W = x.shape
    nc = L // Q
    fidx, ridx = (lambda t: nc - 1 - t), (lambda t: t)

    def body(xf_ref, xr_ref, wf_ref, wr_ref, al_ref, db_ref, sf_ref, sr_ref, dyf_ref, dyr_ref,
             dxf_ref, dxr_ref, dwf_ref, dwr_ref, dal_ref, ddb_ref, dsf_scr, dsr_scr):
        @pl.when(pl.program_id(0) == 0)
        def _():
            dsf_scr[...] = jnp.zeros_like(dsf_scr)
            dsr_scr[...] = jnp.zeros_like(dsr_scr)
            dal_ref[...] = jnp.zeros_like(dal_ref)
            ddb_ref[...] = jnp.zeros_like(ddb_ref)

        dal_sum, ddb_sum = dal_ref[...], ddb_ref[...]
        for rev, x_ref, w_ref, sin_ref, dy_ref, dx_ref, dw_ref, ds_scr in (
                (False, xf_ref, wf_ref, sf_ref, dyf_ref, dxf_ref, dwf_ref, dsf_scr),
                (True, xr_ref, wr_ref, sr_ref, dyr_ref, dxr_ref, dwr_ref, dsr_scr)):
            _, vjp = jax.vjp(lambda S, xv, wv, al, db, rev=rev: chunk_fn(S, xv, wv, al, db, rev),
                             sin_ref[...], x_ref[...], w_ref[...], al_ref[...], db_ref[...])
            ds_scr[...], dx_ref[...], dw_ref[...], dal, ddb = vjp((ds_scr[...], dy_ref[...]))
            dal_sum, ddb_sum = dal_sum + dal, ddb_sum + ddb
        dal_ref[...] = dal_sum
        ddb_ref[...] = ddb_sum

    row = lambda w, idx, b=0: pl.BlockSpec((Q, w), lambda t: (idx(t), b))
    small = pl.BlockSpec(a_log.shape, lambda t: (0, 0))
    state = lambda idx: pl.BlockSpec((STATE_ROWS, HEAD), lambda t: (idx(t), 0))
    return pl.pallas_call(
        body, name=name, grid=(nc,),
        in_specs=[row(W, fidx), row(W, ridx), row(HPAD, fidx, win_blk), row(HPAD, ridx, win_blk), small, small,
                  state(fidx), state(ridx), row(STATE_ROWS, fidx), row(STATE_ROWS, ridx)],
        out_specs=[row(W, fidx), row(W, ridx), row(HPAD, fidx), row(HPAD, ridx), small, small],
        out_shape=[jax.ShapeDtypeStruct((L, W), f32)] * 2 + [jax.ShapeDtypeStruct((L, HPAD), f32)] * 2
        + [jax.ShapeDtypeStruct(a_log.shape, f32)] * 2,
        scratch_shapes=[pltpu.VMEM((STATE_ROWS, HEAD), f32)] * 2, compiler_params=_params(("arbitrary",)),
    )(x, x, p, p, a_log, dt_bias, s_fwd, s_rev, dy, dy)


def _cat(xs):
    return jnp.concatenate(xs, axis=-1)


def _a_prep(pa, cos, sin, qn, wuq, kvn, wk, wv):
    cq, ckv, kr = pa[:, :256], pa[:, 256:384], pa[:, 384:512]
    q = _rope(_mm(_rms(cq, qn, A_Q_LORA), wuq, _NN), _cat([cos] * 4), _cat([sin] * 4), A_NOPE, A_ROPE // 4)
    kvh = _rms(ckv, kvn)
    k = _mm(kvh, wk, _NN) + _cat([_rope(kr, cos, sin, A_NOPE, A_ROPE // 4)] * 4)
    return q, k, _mm(kvh, wv, _NN)


def _b_prep(pb, cos, sin, qn, kn):
    q = _pad_heads([_rms(h, qn) for h in _heads(pb[:, :256], 4)])
    k = _pad_heads([_rms(h, kn) for h in _heads(pb[:, 256:384], 2)])
    v = _pad_heads(_heads(pb[:, 384:512], 2))
    return _rope(q, _cat([cos] * 4), _cat([sin] * 4), 0, HEAD // 4), _rope(k, _cat([cos] * 2), _cat([sin] * 2), 0, HEAD // 4), v


def _c_act(c):
    return _silu(c)


def _d_act(c):
    s = _silu(c)
    return _cat([h * lax.rsqrt(jnp.sum(h * h, axis=-1, keepdims=True) + 1e-6) for h in _heads(s[:, :512], 8)] + [s[:, 512:]])


def _ffn_act(c):
    return _silu(c[:, :FF_BLK]) * c[:, FF_BLK:]


def _mix_post(oa, ob, yc_f, yc_r, xbc, cz, od_f, od_r, dz, a_out, b_out, dskip, c_out, d_out):
    o_a = _rms(_unpad_heads(oa, 4), a_out)
    o_b = _rms(_unpad_heads(ob, 4), b_out)
    skip = _cat([jnp.broadcast_to(dskip[:, h:h + 1], (1, HEAD)) for h in range(4)])
    o_c = _rms((yc_f + yc_r + xbc[:, :256] * skip) * _silu(cz), c_out)
    o_d = _cat([_rms(h, d_out) for h in _heads(od_f + od_r, 4)]) * _silu(dz)
    return _cat([o_a, o_b, o_c, o_d])


def _post_mix(x, y1, g_pm, g_pf):
    x1 = x + _rms(y1, g_pm)
    return x1, _rms(x1, g_pf)


def _post_ffn(x1, y2, g):
    return x1 + _rms(y2, g)


def _pre_bwd(x, g):
    return x, _rms(x, g)


def _layer_fwd(x, W, tabs, n, side=()):
    mx = MXU_DTYPE
    cos_a, sin_a, cos_b, sin_b = tabs
    full = lambda a: (a, a.shape[1], 0)
    s = {"x": x}
    (s["h"],) = rowwise(_rms, [full(x)], [W["g_pre"]], [(D_MODEL, mx)], n + "pre")
    p = s["p"] = matmul(s["h"], W["w_in"], "nn", f32, n + "in")
    s["qa"], s["ka"], s["va"] = rowwise(_a_prep, [(p, 512, SEG_A // 512), full(cos_a), full(sin_a)],
                                        [W["a_qn"], W["a_wuq"], W["a_kvn"], W["a_wk"], W["a_wv"]], [(512, mx)] * 3, n + "a_prep")
    s["qb"], s["kb"], s["vb"] = rowwise(_b_prep, [(p, 512, SEG_B // 512), full(cos_b), full(sin_b)], [W["b_qn"], W["b_kn"]],
                                        [(512, mx), (256, mx), (256, mx)], n + "b_prep")
    s["oa"], s["lsea"], side_out = flash_fwd(s["qa"], s["ka"], s["va"], 4, 4, (A_NOPE + A_ROPE) ** -0.5, n + "a_attn", side)
    s["ob"], s["lseb"], _ = flash_fwd(s["qb"], s["kb"], s["vb"], 4, 2, HEAD ** -0.5, n + "b_attn")
    xbc = s["xbc"] = conv_fwd(_c_act, p, SEG_CX // 512, 1, 512, W["c_cw"], W["c_cb"], 512, f32, n + "c_conv")
    s["yc_f"], s["yc_r"], s["sc_f"], s["sc_r"] = scan_fwd(_ssd_chunk, SSD_CHUNK, xbc, p, SEG_CDT // HPAD, W["c_alog"], W["c_dtb"], n + "c_ssd")
    qkv = s["qkv"] = conv_fwd(_d_act, p, SEG_DX // 768, 1, 768, W["d_cw"], W["d_cb"], 768, f32, n + "d_conv")
    s["od_f"], s["od_r"], s["sd_f"], s["sd_r"] = scan_fwd(_dn_chunk, DN_CHUNK, qkv, p, SEG_DAB // HPAD, W["d_alog"], W["d_dtb"], n + "d_dn")
    (s["omix"],) = rowwise(_mix_post, _mix_rows(s), _mix_params(W), [(D_MODEL, mx)], n + "mix_post")
    s["y1"] = matmul(s["omix"], W["w_out"], "nn", f32, n + "out")
    s["x1"], s["h2"] = rowwise(_post_mix, [full(x), full(s["y1"])], [W["g_pm"], W["g_pf"]], [(D_MODEL, f32), (D_MODEL, mx)], n + "post_mix")
    s["u"] = matmul(s["h2"], W["f_win"], "nn", f32, n + "f_in")
    s["a"] = conv_fwd(_ffn_act, s["u"], 0, D_FF // FF_BLK, 2 * FF_BLK, W["f_cw"], W["f_cb"], FF_BLK, mx, n + "f_conv", tm=512)
    s["y2"] = matmul(s["a"], W["f_wout"], "nn", f32, n + "f_out")
    (x2,) = rowwise(_post_ffn, [full(s["x1"]), full(s["y2"])], [W["g_po"]], [(D_MODEL, f32)], n + "post_ffn")
    return x2, s, side_out


def _mix_rows(s):
    p = s["p"]
    return [(s["oa"], 512, 0), (s["ob"], 512, 0), (s["yc_f"], 256, 0), (s["yc_r"], 256, 0), (s["xbc"], 512, 0),
            (p, 256, SEG_CZ // 256), (s["od_f"], 256, 0), (s["od_r"], 256, 0), (p, 256, SEG_DZ // 256)]


def _mix_params(W):
    return [W["a_out"], W["b_out"], W["c_dskip"], W["c_out"], W["d_out"]]


def _layer_bwd(s, dx2, W, tabs, n, side=()):
    cos_a, sin_a, cos_b, sin_b = tabs
    full = lambda a: (a, a.shape[1], 0)
    p, g = s["p"], {}
    (dx1, dy2), (g["g_po"],) = rowwise_vjp(_post_ffn, [full(s["x1"]), full(s["y2"])], [W["g_po"]], [full(dx2)], n + "post_ffn_b", [0, 1], [0])
    da = matmul(dy2, W["f_wout"], "nt", f32, n + "f_out_dx")
    g["f_wout"] = matmul(s["a"], dy2, "tn", f32, n + "f_out_dw")
    du, g["f_cw"], g["f_cb"] = conv_bwd(_ffn_act, s["u"], 0, D_FF // FF_BLK, 2 * FF_BLK, W["f_cw"], W["f_cb"], [da], FF_BLK, n + "f_conv_b", tm=512)
    dh2 = matmul(du, W["f_win"], "nt", f32, n + "f_in_dx")
    g["f_win"] = matmul(s["h2"], du, "tn", f32, n + "f_in_dw")
    (dx, dy1), (g["g_pm"], g["g_pf"]) = rowwise_vjp(_post_mix, [full(s["x"]), full(s["y1"])], [W["g_pm"], W["g_pf"]],
                                                   [full(dx1), full(dh2)], n + "post_mix_b", [0, 1], [0, 1])
    domix = matmul(dy1, W["w_out"], "nt", f32, n + "out_dx")
    g["w_out"] = matmul(s["omix"], dy1, "tn", f32, n + "out_dw")
    (doa, dob, dyc, dxbc_skip, dcz, dod, ddz), (g["a_out"], g["b_out"], g["c_dskip"], g["c_out"], g["d_out"]) = rowwise_vjp(
        _mix_post, _mix_rows(s), _mix_params(W), [full(domix)], n + "mix_post_b", [0, 1, 2, 4, 5, 6, 8], list(range(5)))
    dqkv_f, dqkv_r, ddab_f, ddab_r, g["d_alog"], g["d_dtb"] = scan_bwd(
        _dn_chunk, DN_CHUNK, s["qkv"], p, SEG_DAB // HPAD, W["d_alog"], W["d_dtb"], s["sd_f"], s["sd_r"], dod, n + "d_dn_b")
    ddx, g["d_cw"], _ = conv_bwd(_d_act, p, SEG_DX // 768, 1, 768, W["d_cw"], W["d_cb"], [dqkv_f, dqkv_r], 768, n + "d_conv_b")
    dxbc_f, dxbc_r, dcdt_f, dcdt_r, g["c_alog"], g["c_dtb"] = scan_bwd(
        _ssd_chunk, SSD_CHUNK, s["xbc"], p, SEG_CDT // HPAD, W["c_alog"], W["c_dtb"], s["sc_f"], s["sc_r"], dyc, n + "c_ssd_b")
    dcx, g["c_cw"], g["c_cb"] = conv_bwd(_c_act, p, SEG_CX // 512, 1, 512, W["c_cw"], W["c_cb"], [dxbc_f, dxbc_r, dxbc_skip], 512,
                                         n + "c_conv_b")
    (dsmall,) = rowwise(lambda a, b, c, d: _cat([a + b, c + d]), [full(dcdt_f), full(dcdt_r), full(ddab_f), full(ddab_r)], [],
                        [(2 * HPAD, f32)], n + "dwin_sum")
    dqa, dka, dva, side_out = flash_bwd(s["qa"], s["ka"], s["va"], s["oa"], s["lsea"], doa, 4, 4, (A_NOPE + A_ROPE) ** -0.5,
                                        n + "a_attn_b", side)
    dqb, dkb, dvb, _ = flash_bwd(s["qb"], s["kb"], s["vb"], s["ob"], s["lseb"], dob, 4, 2, HEAD ** -0.5, n + "b_attn_b")
    (dpa,), (g["a_qn"], g["a_wuq"], g["a_kvn"], g["a_wk"], g["a_wv"]) = rowwise_vjp(
        _a_prep, [(p, 512, SEG_A // 512), full(cos_a), full(sin_a)], [W["a_qn"], W["a_wuq"], W["a_kvn"], W["a_wk"], W["a_wv"]],
        [full(dqa), full(dka), full(dva)], n + "a_prep_b", [0], list(range(5)))
    (dpb,), (g["b_qn"], g["b_kn"]) = rowwise_vjp(_b_prep, [(p, 512, SEG_B // 512), full(cos_b), full(sin_b)], [W["b_qn"], W["b_kn"]],
                                               [full(dqb), full(dkb), full(dvb)], n + "b_prep_b", [0], [0, 1])
    dp = jnp.concatenate([dpa, dpb, dcx, ddx, dcz, ddz, dsmall], axis=1)
    dh = matmul(dp, W["w_in"], "nt", f32, n + "in_dx")
    g["w_in"] = matmul(s["h"], dp, "tn", f32, n + "in_dw")
    (dx0,), (g["g_pre"],) = rowwise_vjp(_pre_bwd, [full(s["x"])], [W["g_pre"]], [full(dx), full(dh)], n + "pre_b", [0], [0])
    return dx0, g, side_out


def loss_and_grad(y, target, name):
    L, D = y.shape
    tm = _tile(L, 512)

    def body(y_ref, t_ref, loss_ref, dy_ref):
        e = y_ref[...] - t_ref[...]
        dy_ref[...] = e * (1.0 / D)
        part = 0.5 * jnp.sum(jnp.sum(e * e, axis=1, keepdims=True) * (1.0 / D), axis=0, keepdims=True)

        @pl.when(pl.program_id(0) == 0)
        def _():
            loss_ref[...] = part

        @pl.when(pl.program_id(0) > 0)
        def _():
            loss_ref[...] += part

    row = pl.BlockSpec((tm, D), lambda i: (i, 0))
    return pl.pallas_call(
        body, name=name, grid=(L // tm,), in_specs=[row, row], out_specs=[pl.BlockSpec((1, 1), lambda i: (0, 0)), row],
        out_shape=[jax.ShapeDtypeStruct((1, 1), f32), jax.ShapeDtypeStruct((L, D), f32)], compiler_params=_params(("arbitrary",)),
    )(y, target)


_IN_SEGS = [(0, 192), (256, 128), (384 + A_NOPE, 32), (512, 256), (768, 128), (896, 128), (SEG_CZ, 256), (SEG_CX, 512),
            (SEG_CDT, 8), (SEG_DX, 768), (SEG_DZ, 256), (SEG_DAB, 16)]


def _pad_in(w):
    src, pieces = 0, {}
    for off, wd in _IN_SEGS:
        pieces[off] = w[..., src:src + wd]
        src += wd
    out, pos = [], 0
    for off in sorted(pieces):
        if off > pos:
            out.append(jnp.zeros(w.shape[:-1] + (off - pos,), w.dtype))
        out.append(pieces[off])
        pos = off + pieces[off].shape[-1]
    out.append(jnp.zeros(w.shape[:-1] + (PIN - pos,), w.dtype))
    return jnp.concatenate(out, axis=-1)


def _unpad_in(wp):
    return jnp.concatenate([wp[..., off:off + wd] for off, wd in _IN_SEGS], axis=-1)


def _pad_last(a, n):
    return jnp.pad(a, [(0, 0)] * (a.ndim - 1) + [(0, n - a.shape[-1])])


def _ff_interleave(w):
    lead = w.shape[:-1]
    return jnp.stack([w[..., :D_FF].reshape(lead + (D_FF // FF_BLK, FF_BLK)), w[..., D_FF:].reshape(lead + (D_FF // FF_BLK, FF_BLK))],
                     axis=-2).reshape(lead + (2 * D_FF,))


def _ff_deinterleave(w):
    lead = w.shape[:-1]
    t = w.reshape(lead + (D_FF // FF_BLK, 2, FF_BLK))
    return jnp.concatenate([t[..., 0, :].reshape(lead + (D_FF,)), t[..., 1, :].reshape(lead + (D_FF,))], axis=-1)


def _row(v):
    return v.reshape(1, -1)


def _kernel_weights(P, l, wdt):
    uq = P["a_w_uq"][l].reshape(A_Q_LORA, 4, A_NOPE + A_ROPE)
    ukv = P["a_w_ukv"][l].reshape(A_KV_LORA, 4, 2 * HEAD)
    z = jnp.zeros((A_KV_LORA, 4, HEAD), ukv.dtype)
    return {
        "g_pre": _row(P["pre_mix_norm"][l]), "w_in": _pad_in(P["w_in"][l]).astype(wdt),
        "a_qn": _pad_last(_row(P["a_q_norm"][l]), 256),
        "a_wuq": jnp.pad(_pad_last(uq, HPAD).reshape(A_Q_LORA, 4 * HPAD), ((0, 256 - A_Q_LORA), (0, 0))).astype(wdt),
        "a_kvn": _row(P["a_kv_norm"][l]),
        "a_wk": jnp.concatenate([ukv[..., :HEAD], z], axis=-1).reshape(A_KV_LORA, 4 * HPAD).astype(wdt),
        "a_wv": jnp.concatenate([ukv[..., HEAD:], z], axis=-1).reshape(A_KV_LORA, 4 * HPAD).astype(wdt),
        "a_out": _row(P["a_out_norm"][l]), "b_qn": _row(P["b_q_norm"][l]), "b_kn": _row(P["b_k_norm"][l]), "b_out": _row(P["b_out_norm"][l]),
        "c_cw": P["c_conv_w"][l], "c_cb": _row(P["c_conv_b"][l]), "c_alog": P["c_a_log"][l], "c_dtb": P["c_dt_bias"][l],
        "c_dskip": _row(P["c_d_skip"][l]), "c_out": _row(P["c_out_norm"][l]),
        "d_cw": P["d_conv_w"][l], "d_cb": jnp.zeros((1, 768), f32), "d_alog": P["d_a_log"][l], "d_dtb": P["d_dt_bias"][l],
        "d_out": _row(P["d_out_norm"][l]), "w_out": P["w_out"][l].astype(wdt), "g_pm": _row(P["post_mix_norm"][l]),
        "g_pf": _row(P["pre_ffn_norm"][l]), "f_win": _ff_interleave(P["f_w_in"][l]).astype(wdt),
        "f_cw": _ff_interleave(P["f_conv_w"][l]), "f_cb": _row(_ff_interleave(P["f_conv_b"][l])),
        "f_wout": P["f_w_out"][l].astype(wdt), "g_po": _row(P["post_ffn_norm"][l]),
    }


def _reference_grads(g):
    uq = g["a_wuq"][:A_Q_LORA].reshape(A_Q_LORA, 4, HPAD)[..., :A_NOPE + A_ROPE].reshape(A_Q_LORA, 4 * (A_NOPE + A_ROPE))
    wk = g["a_wk"].reshape(A_KV_LORA, 4, HPAD)[..., :HEAD]
    wv = g["a_wv"].reshape(A_KV_LORA, 4, HPAD)[..., :HEAD]
    return {
        "pre_mix_norm": g["g_pre"][0], "w_in": _unpad_in(g["w_in"]), "a_q_norm": g["a_qn"][0, :A_Q_LORA], "a_w_uq": uq,
        "a_kv_norm": g["a_kvn"][0], "a_w_ukv": jnp.concatenate([wk, wv], axis=-1).reshape(A_KV_LORA, 8 * HEAD),
        "a_out_norm": g["a_out"][0], "b_q_norm": g["b_qn"][0], "b_k_norm": g["b_kn"][0], "b_out_norm": g["b_out"][0],
        "c_conv_w": g["c_cw"], "c_conv_b": g["c_cb"][0], "c_a_log": g["c_alog"], "c_dt_bias": g["c_dtb"], "c_d_skip": g["c_dskip"][0],
        "c_out_norm": g["c_out"][0], "d_conv_w": g["d_cw"], "d_a_log": g["d_alog"], "d_dt_bias": g["d_dtb"], "d_out_norm": g["d_out"][0],
        "w_out": g["w_out"], "post_mix_norm": g["g_pm"][0], "pre_ffn_norm": g["g_pf"][0], "f_w_in": _ff_deinterleave(g["f_win"]),
        "f_conv_w": _ff_deinterleave(g["f_cw"]), "f_conv_b": _ff_deinterleave(g["f_cb"][0]), "f_w_out": g["f_wout"],
        "post_ffn_norm": g["g_po"][0],
    }


def _rope_tables(L):
    def tables(rot):
        rows = L // GRID_W
        row = jnp.repeat(jnp.arange(rows), GRID_W).astype(f32)
        col = jnp.tile(jnp.arange(GRID_W), rows).astype(f32)
        sec = rot // 2
        inv = ROPE_BASE ** (-jnp.arange(0, sec, 2, dtype=f32) / sec)
        ang = jnp.concatenate([row[:, None] * inv] * 2 + [col[:, None] * inv] * 2, axis=-1)
        return jnp.cos(ang), jnp.sin(ang)

    ca, sa = tables(A_ROPE)
    cb, sb = tables(HEAD)
    one, zero = jnp.ones, jnp.zeros
    return (jnp.concatenate([one((L, A_NOPE), f32), ca, one((L, HPAD - A_NOPE - A_ROPE), f32)], axis=1),
            jnp.concatenate([zero((L, A_NOPE), f32), sa, zero((L, HPAD - A_NOPE - A_ROPE), f32)], axis=1),
            jnp.concatenate([cb, one((L, HPAD - HEAD), f32)], axis=1), jnp.concatenate([sb, zero((L, HPAD - HEAD), f32)], axis=1))


def local_step(x, target, P, wdt):
    tabs = _rope_tables(x.shape[0])
    Ws = [_kernel_weights(P, l, wdt) for l in range(DEPTH)]
    saved = []
    for l in range(DEPTH):
        x, s, _ = _layer_fwd(x, Ws[l], tabs, f"l{l}_")
        saved.append(s)
    loss, dx = loss_and_grad(x, target, "loss")
    grads = [None] * DEPTH
    for l in reversed(range(DEPTH)):
        dx, g, _ = _layer_bwd(saved[l], dx, Ws[l], tabs, f"l{l}_")
        grads[l] = _reference_grads(g)
    return loss, dx, grads


def exchange(items, name):
    n = len(items)

    def body(*refs):
        start, wait = _exchange_ops([g for _, g in items], refs[:n], refs[n:2 * n], *refs[2 * n:])
        start()
        wait()

    ex = _exchange_io(items)
    res = pl.pallas_call(body, name=name, in_specs=ex.specs, out_specs=ex.specs, out_shape=ex.out_shape,
                         scratch_shapes=ex.scratch)(*ex.args)
    return list(res)


def _exchange_ops(modes, src_refs, out_refs, send_sems, recv_sems, local_sems):
    n = len(modes)
    x, y, c = lax.axis_index("x"), lax.axis_index("y"), lax.axis_index("c")
    me = 4 * x + 2 * y + c

    def copy(a, k, arriving):
        px, py, pc = (1 - x if k & 4 else x), (1 - y if k & 2 else y), (1 - c if k & 1 else c)
        pid = 4 * px + 2 * py + pc
        sem = a * (N_DEV - 1) + k - 1
        return pltpu.make_async_remote_copy(
            src_ref=src_refs[a] if modes[a] else src_refs[a].at[pid], dst_ref=out_refs[a].at[pid if arriving else me],
            send_sem=send_sems.at[sem], recv_sem=recv_sems.at[sem], device_id=(px, py, pc), device_id_type=pl.DeviceIdType.MESH)

    def local(a):
        return pltpu.make_async_copy(src_refs[a] if modes[a] else src_refs[a].at[me], out_refs[a].at[me], local_sems.at[a])

    pairs = [(a, k) for k in range(1, N_DEV) for a in range(n)]

    def start():
        for a, k in pairs:
            copy(a, k, False).start()
        for a in range(n):
            local(a).start()

    def wait():
        for a, k in pairs:
            copy(a, k, False).wait_send()
        for a, k in pairs:
            copy(a, k, True).wait_recv()
        for a in range(n):
            local(a).wait()

    return start, wait


class _ExchangeIO:
    def __init__(self, items):
        n = len(items)
        self.args = [s for s, _ in items]
        self.specs = [pl.BlockSpec(memory_space=pl.ANY)] * n
        self.out_shape = [jax.ShapeDtypeStruct((N_DEV,) + tuple(s.shape if g else s.shape[1:]), s.dtype) for s, g in items]
        self.scratch = [pltpu.SemaphoreType.DMA((n * (N_DEV - 1),)), pltpu.SemaphoreType.DMA((n * (N_DEV - 1),)),
                        pltpu.SemaphoreType.DMA((n,))] if n else []


def _exchange_io(items):
    return _ExchangeIO(list(items))


def _with_exchange(body, n_in, n_out, side, grid):
    ex = _exchange_io(side)
    n = len(ex.args)
    if not n:
        return body, ex
    modes = [g for _, g in side]

    def wrapped(*refs):
        ins, src = refs[:n_in], refs[n_in:n_in + n]
        outs, dst = refs[n_in + n:n_in + n + n_out], refs[n_in + n + n_out:n_in + 2 * n + n_out]
        sems = refs[n_in + 2 * n + n_out:]
        start, wait = _exchange_ops(modes, src, dst, *sems)
        ids = [pl.program_id(a) for a in range(len(grid))]
        first, last = ids[0] == 0, ids[0] == grid[0] - 1
        for a in range(1, len(grid)):
            first, last = first & (ids[a] == 0), last & (ids[a] == grid[a] - 1)
        pl.when(first)(start)
        body(*ins, *outs)
        pl.when(last)(wait)

    return wrapped, ex


LANES = 1024


def adamw(gstacks, w, m, v, name):
    D, R, C = w.shape
    S = gstacks[0].shape[0]
    tr = R if R <= 512 else _tile(R, 256)

    def body(*refs):
        g_refs = refs[:D]
        w_ref, m_ref, v_ref, go_ref, d_ref, mo_ref, vo_ref = refs[D:]
        layer = pl.program_id(0)
        g = None
        for d, g_ref in enumerate(g_refs):
            gd = g_ref[0].astype(f32)
            for sl in range(1, S):
                gd = gd + g_ref[sl].astype(f32)
            g = gd if g is None else jnp.where(layer == d, gd, g)
        g = g[None]
        m_new = ADAM_B1 * m_ref[...] + (1.0 - ADAM_B1) * g
        v_new = ADAM_B2 * v_ref[...] + (1.0 - ADAM_B2) * jnp.square(g)
        m_hat = m_new / (1.0 - ADAM_B1 ** ADAM_STEP)
        v_hat = v_new / (1.0 - ADAM_B2 ** ADAM_STEP)
        go_ref[...] = g
        d_ref[...] = -ADAM_LR * (m_hat / (jnp.sqrt(v_hat) + ADAM_EPS) + ADAM_WD * w_ref[...])
        mo_ref[...] = m_new
        vo_ref[...] = v_new

    row = pl.BlockSpec((1, tr, C), lambda l, i: (l, i, 0))
    g_specs = [pl.BlockSpec((S, tr, C), lambda l, i, d=d: (0, jnp.where(l == d, i, 0), 0)) for d in range(D)]
    return pl.pallas_call(
        body, name=name, grid=(D, R // tr), in_specs=g_specs + [row, row, row],
        out_specs=[row] * 4, out_shape=[jax.ShapeDtypeStruct((D, R, C), f32)] * 4, compiler_params=_params(("parallel", "parallel")),
    )(*gstacks, w, m, v)


def sum_slots(gstack, name):
    S, R, _ = gstack.shape

    def body(g_ref, o_ref):
        g = g_ref[0]
        for sl in range(1, S):
            g = g + g_ref[sl]
        o_ref[...] = g

    return pl.pallas_call(body, name=name, out_shape=jax.ShapeDtypeStruct((R, LANES), f32), compiler_params=_params())(gstack)


def _pack(parts, rows, dtype=f32):
    flat = jnp.concatenate([q.reshape(-1).astype(dtype) for q in parts])
    return jnp.pad(flat, (0, rows * LANES - flat.shape[0])).reshape(rows, LANES)


def _unpack(buf, shapes):
    lead = buf.shape[:-2]
    flat = buf.reshape(lead + (-1,))
    out, off = [], 0
    for shp in shapes:
        n = math.prod(shp)
        out.append(flat[..., off:off + n].reshape(lead + tuple(shp)))
        off += n
    return out


_WEIGHTS = ["pre_mix_norm", "w_in", "a_q_norm", "a_w_uq", "a_kv_norm", "a_w_ukv", "a_out_norm", "b_q_norm", "b_k_norm", "b_out_norm",
            "c_conv_w", "c_conv_b", "c_a_log", "c_dt_bias", "c_d_skip", "c_out_norm", "d_conv_w", "d_a_log", "d_dt_bias", "d_out_norm",
            "w_out", "post_mix_norm", "pre_ffn_norm", "f_w_in", "f_conv_w", "f_conv_b", "f_w_out", "post_ffn_norm"]
_BIG = {"w_in": 1, "a_w_uq": 1, "a_w_ukv": 1, "w_out": 0, "f_w_in": 1, "f_w_out": 0}
_CONV = ["c_conv_w", "d_conv_w", "f_conv_w"]
_REP = [n for n in _WEIGHTS if n not in _BIG and n not in _CONV]
SMALLG_ROWS, SMALLW_ROWS = 64, 32


def _join(blocks, axis):
    if axis == 0:
        return blocks.reshape(-1, blocks.shape[2])
    return blocks.transpose(1, 0, 2).reshape(blocks.shape[1], -1)


def _split(full, axis):
    if axis == 0:
        return full.reshape(N_DEV, -1, full.shape[1])
    return full.reshape(full.shape[0], N_DEV, -1).transpose(1, 0, 2)


def kernel(x, pre_mix_norm, w_in, a_q_norm, a_w_uq, a_kv_norm, a_w_ukv, a_out_norm, b_q_norm, b_k_norm, b_out_norm, c_conv_w, c_conv_b, c_a_log, c_dt_bias, c_d_skip, c_out_norm, d_conv_w, d_a_log, d_dt_bias, d_out_norm, w_out, post_mix_norm, pre_ffn_norm, f_w_in, f_conv_w, f_conv_b, f_w_out, post_ffn_norm, loss_target, m_pre_mix_norm, m_w_in, m_a_q_norm, m_a_w_uq, m_a_kv_norm, m_a_w_ukv, m_a_out_norm, m_b_q_norm, m_b_k_norm, m_b_out_norm, m_c_conv_w, m_c_conv_b, m_c_a_log, m_c_dt_bias, m_c_d_skip, m_c_out_norm, m_d_conv_w, m_d_a_log, m_d_dt_bias, m_d_out_norm, m_w_out, m_post_mix_norm, m_pre_ffn_norm, m_f_w_in, m_f_conv_w, m_f_conv_b, m_f_w_out, m_post_ffn_norm, v_pre_mix_norm, v_w_in, v_a_q_norm, v_a_w_uq, v_a_kv_norm, v_a_w_ukv, v_a_out_norm, v_b_q_norm, v_b_k_norm, v_b_out_norm, v_c_conv_w, v_c_conv_b, v_c_a_log, v_c_dt_bias, v_c_d_skip, v_c_out_norm, v_d_conv_w, v_d_a_log, v_d_dt_bias, v_d_out_norm, v_w_out, v_post_mix_norm, v_pre_ffn_norm, v_f_w_in, v_f_conv_w, v_f_conv_b, v_f_w_out, v_post_ffn_norm):
    given = dict(locals())
    w = {n: given[n] for n in _WEIGHTS}
    mom = {n: given["m_" + n] for n in _WEIGHTS}
    var = {n: given["v_" + n] for n in _WEIGHTS}
    me = 4 * lax.axis_index("x") + 2 * lax.axis_index("y") + lax.axis_index("c")
    layered = lambda names: [(l, n) for l in range(DEPTH) for n in names]

    sharded = list(_BIG) + _CONV
    weight_items = lambda l: [(w[n][l].astype(MXU_DTYPE), True) for n in _BIG] + [(w[n][l], True) for n in _CONV]
    grad_items = lambda g: [(_split(g[n], _BIG[n]).astype(WIRE_DTYPE), False) for n in _BIG]
    P = {n: [w[n][l] for l in range(DEPTH)] for n in _REP}
    P.update({n: [None] * DEPTH for n in sharded})

    def place(l, gathered):
        for n, got in zip(sharded, gathered):
            P[n][l] = _join(got, _BIG.get(n, 1))

    tabs = _rope_tables(x.shape[1])
    place(0, exchange(weight_items(0), "gather_weights_l0"))
    W0 = _kernel_weights(P, 0, MXU_DTYPE)
    x1, s0, got = _layer_fwd(x[0], W0, tabs, "l0_", weight_items(1))
    place(1, got)
    W1 = _kernel_weights(P, 1, MXU_DTYPE)
    x2, s1, _ = _layer_fwd(x1, W1, tabs, "l1_")
    loss, dx = loss_and_grad(x2, loss_target[0], "loss")
    loss = lax.psum(loss[0, 0], ("x", "y", "c"))

    dx, g1, _ = _layer_bwd(s1, dx, W1, tabs, "l1_")
    g1 = _reference_grads(g1)
    dx, g0, recv1 = _layer_bwd(s0, dx, W0, tabs, "l0_", grad_items(g1))
    grads = [_reference_grads(g0), g1]
    small = _REP + _CONV
    full_shapes = [grads[l][n].shape for l, n in layered(small)]
    *recv0, gsmall = exchange(grad_items(grads[0]) + [(_pack([grads[l][n] for l, n in layered(small)], SMALLG_ROWS), True)],
                              "exchange_grads")

    out = {}
    for n, got0, got1 in zip(_BIG, recv0, recv1):
        out[n] = adamw([got0, got1], w[n], mom[n], var[n], "adamw_" + n)
    gsum = dict(zip(layered(small), _unpack(sum_slots(gsmall, "sum_small_grads"), full_shapes)))
    for l, n in layered(_CONV):
        cols = w[n].shape[2]
        gsum[(l, n)] = lax.dynamic_slice_in_dim(gsum[(l, n)], me * cols, cols, axis=1)
    held_shapes = [w[n].shape[1:] for _, n in layered(small)]
    pack_small = lambda d: _pack([d[n][l] for l, n in layered(small)], SMALLW_ROWS)[None]
    res_small = adamw([_pack([gsum[k] for k in layered(small)], SMALLW_ROWS)[None]], pack_small(w), pack_small(mom),
                      pack_small(var), "adamw_small")
    per_kind = [dict(zip(layered(small), _unpack(buf[0], held_shapes))) for buf in res_small]
    for n in small:
        out[n] = [jnp.stack([per_kind[kind][(l, n)] for l in range(DEPTH)]) for kind in range(4)]
    return (loss, dx[None], *[out[n][kind] for kind in range(4) for n in _WEIGHTS])
```

```python
import functools
import math

import jax
import jax.numpy as jnp
from jax import lax
from jax.experimental import pallas as pl
from jax.experimental.pallas import tpu as pltpu

f32 = jnp.float32
MXU_DTYPE = jnp.bfloat16
WIRE_DTYPE = jnp.bfloat16
HI = lax.Precision.HIGHEST

D_MODEL = 1024
DEPTH = 2
GRID_W = 64
ROPE_BASE = 10000.0
EPS = 1e-6
A_Q_LORA, A_KV_LORA, A_ROPE, A_NOPE = 192, 128, 32, 64
SSD_CHUNK = 128
DN_CHUNK = 64
HEAD = 64
HPAD = 128
D_FF = 2816
FF_BLK = 256
N_DEV = 8
ADAM_LR, ADAM_B1, ADAM_B2, ADAM_EPS, ADAM_WD, ADAM_STEP = 0.001, 0.9, 0.999, 1e-08, 0.01, 10

V7X_VMEM_BYTES = 64 * 2 ** 20
VMEM_LIMIT = (V7X_VMEM_BYTES * 3) // 4

PIN = 3072
SEG_A, SEG_B, SEG_CX, SEG_DX, SEG_CZ, SEG_DZ, SEG_CDT, SEG_DAB = 0, 512, 1024, 1536, 2304, 2560, 2816, 2944


def _tile(n, pref):
    for t in (512, 256, 128, 64, 32, 16, 8):
        if t <= pref and n % t == 0:
            return t
    raise ValueError(f"no tile for {n}")


LANE = 128


def _tile_div(n, pref):
    if n <= pref:
        return n
    return max(d for d in range(LANE, pref + 1, LANE) if n % d == 0)


def _params(sem=None):
    return pltpu.CompilerParams(vmem_limit_bytes=VMEM_LIMIT, dimension_semantics=sem)


def _tup(r):
    return tuple(r) if isinstance(r, (tuple, list)) else (r,)


def matmul(a, b, form, out_dtype, name):
    if form == "nn":
        (M, K), N = a.shape, b.shape[1]
    elif form == "nt":
        (M, K), N = a.shape, b.shape[0]
    else:
        (K, M), N = a.shape, b.shape[1]
    if form == "tn":
        tm, tn, tk = _tile_div(M, 1408), _tile_div(N, 1408), _tile_div(K, 1024)
    else:
        tm, tn, tk = _tile_div(M, 512), _tile_div(N, 1536), _tile_div(K, 3072)
    nk = K // tk
    dims = {"nn": ((1,), (0,)), "nt": ((1,), (1,)), "tn": ((0,), (0,))}[form]

    def body(a_ref, b_ref, o_ref, *acc):
        part = lax.dot_general(a_ref[...].astype(MXU_DTYPE), b_ref[...].astype(MXU_DTYPE), (dims, ((), ())),
                               preferred_element_type=f32)
        if nk == 1:
            o_ref[...] = part.astype(out_dtype)
            return
        (acc_ref,) = acc
        k = pl.program_id(2)

        @pl.when(k == 0)
        def _():
            acc_ref[...] = part

        @pl.when((k > 0) & (k < nk - 1))
        def _():
            acc_ref[...] += part

        @pl.when(k == nk - 1)
        def _():
            o_ref[...] = (acc_ref[...] + part).astype(out_dtype)

    a_spec = pl.BlockSpec((tk, tm), lambda i, j, k: (k, i)) if form == "tn" else pl.BlockSpec((tm, tk), lambda i, j, k: (i, k))
    b_spec = pl.BlockSpec((tn, tk), lambda i, j, k: (j, k)) if form == "nt" else pl.BlockSpec((tk, tn), lambda i, j, k: (k, j))
    return pl.pallas_call(
        body, name=name, grid=(M // tm, N // tn, nk), in_specs=[a_spec, b_spec],
        out_specs=pl.BlockSpec((tm, tn), lambda i, j, k: (i, j)), out_shape=jax.ShapeDtypeStruct((M, N), out_dtype),
        scratch_shapes=[pltpu.VMEM((tm, tn), f32)] if nk > 1 else [], compiler_params=_params(("parallel", "parallel", "arbitrary")),
    )(a, b)


def _row_specs(rows, tm):
    return [pl.BlockSpec((tm, w), lambda i, b=b: (i, b)) for (_, w, b) in rows]


def _full_specs(params):
    return [pl.BlockSpec(p.shape, lambda i: (0, 0)) for p in params]


def rowwise(fn, rows, params, outs, name, tm=256):
    L = rows[0][0].shape[0]
    tm = _tile(L, tm)
    n_in = len(rows) + len(params)

    def body(*refs):
        res = _tup(fn(*[r[...].astype(f32) for r in refs[:n_in]]))
        for o_ref, r in zip(refs[n_in:], res, strict=True):
            o_ref[...] = r.astype(o_ref.dtype)

    res = pl.pallas_call(
        body, name=name, grid=(L // tm,), in_specs=_row_specs(rows, tm) + _full_specs(params),
        out_specs=[pl.BlockSpec((tm, w), lambda i: (i, 0)) for (w, _) in outs],
        out_shape=[jax.ShapeDtypeStruct((L, w), dt) for (w, dt) in outs], compiler_params=_params(("parallel",)),
    )(*[r[0] for r in rows], *params)
    return list(res)


def rowwise_vjp(fn, rows, params, cts, name, row_grads, param_grads, tm=256):
    L = rows[0][0].shape[0]
    tm = _tile(L, tm)
    nr, npar, nct = len(rows), len(params), len(cts)

    def body(*refs):
        i = pl.program_id(0)
        rv = [r[...].astype(f32) for r in refs[:nr]]
        pv = [r[...].astype(f32) for r in refs[nr:nr + npar]]
        cv = tuple(r[...].astype(f32) for r in refs[nr + npar:nr + npar + nct])
        out_refs = refs[nr + npar + nct:]

        def g(*diff):
            rr, pp = list(rv), list(pv)
            for k, v in zip(row_grads, diff[:len(row_grads)]):
                rr[k] = v
            for k, v in zip(param_grads, diff[len(row_grads):]):
                pp[k] = v
            return _tup(fn(*rr, *pp))

        _, vjp = jax.vjp(g, *[rv[k] for k in row_grads], *[pv[k] for k in param_grads])
        grads = vjp(cv)
        for o_ref, gval in zip(out_refs[:len(row_grads)], grads[:len(row_grads)]):
            o_ref[...] = gval
        for o_ref, gval in zip(out_refs[len(row_grads):], grads[len(row_grads):]):
            @pl.when(i == 0)
            def _(o_ref=o_ref, gval=gval):
                o_ref[...] = gval

            @pl.when(i > 0)
            def _(o_ref=o_ref, gval=gval):
                o_ref[...] += gval

    out_specs = [pl.BlockSpec((tm, rows[k][1]), lambda i: (i, 0)) for k in row_grads] + \
                [pl.BlockSpec(params[k].shape, lambda i: (0, 0)) for k in param_grads]
    out_shape = [jax.ShapeDtypeStruct((L, rows[k][1]), f32) for k in row_grads] + \
                [jax.ShapeDtypeStruct(params[k].shape, f32) for k in param_grads]
    res = pl.pallas_call(
        body, name=name, grid=(L // tm,), in_specs=_row_specs(rows, tm) + _full_specs(params) + _row_specs(cts, tm),
        out_specs=out_specs, out_shape=out_shape, compiler_params=_params(("arbitrary",)),
    )(*[r[0] for r in rows], *params, *[c[0] for c in cts])
    res = list(res)
    return res[:len(row_grads)], res[len(row_grads):]


def _rms(x, w, n=None):
    n = x.shape[-1] if n is None else n
    return x * lax.rsqrt(jnp.sum(x * x, axis=-1, keepdims=True) * (1.0 / n) + EPS) * w


def _silu(x):
    return x * jax.nn.sigmoid(x)


def _softplus(x):
    return jnp.maximum(x, 0.0) + jnp.log1p(jnp.exp(-jnp.abs(x)))


def _mm(a, b, dims, precision=None):
    if precision is None:
        a, b = a.astype(MXU_DTYPE), b.astype(MXU_DTYPE)
    return lax.dot_general(a, b, (dims, ((), ())), precision=precision, preferred_element_type=f32)


_NN, _NT, _TN = ((1,), (0,)), ((1,), (1,)), ((0,), (0,))


@functools.partial(jax.custom_vjp, nondiff_argnums=(1, 2))
def _roll(x, shift, axis):
    return pltpu.roll(x, shift, axis)


def _roll_fwd(x, shift, axis):
    return pltpu.roll(x, shift, axis), None


def _roll_bwd(shift, axis, _, ct):
    return (pltpu.roll(ct, (ct.shape[axis] - shift) % ct.shape[axis], axis),)


_roll.defvjp(_roll_fwd, _roll_bwd)


def _rope(x, cos, sin, lo, half):
    n = x.shape[-1]
    lane = lax.broadcasted_iota(jnp.int32, x.shape, x.ndim - 1) % HPAD - lo
    first = ((lane >= 0) & (lane < half)) | ((lane >= 2 * half) & (lane < 3 * half))
    rot = jnp.where(first, -_roll(x, n - half, x.ndim - 1), _roll(x, half, x.ndim - 1))
    return x * cos + rot * sin


def _heads(x, n, width=HEAD):
    return [x[:, h * width:(h + 1) * width] for h in range(n)]


def _pad_heads(hs):
    z = jnp.zeros_like(hs[0])
    return jnp.concatenate([t for h in hs for t in (h, z)], axis=-1)


def _unpad_heads(x, n):
    return jnp.concatenate([x[:, h * HPAD:h * HPAD + HEAD] for h in range(n)], axis=-1)


HALO = 8


def _conv_specs(tm, tc, c0, L):
    nh = tm // HALO
    last = L // HALO - 1
    return [pl.BlockSpec((tm, tc), lambda i, j: (i, c0 + j)),
            pl.BlockSpec((HALO, tc), lambda i, j: (jnp.maximum(i * nh - 1, 0), c0 + j)),
            pl.BlockSpec((HALO, tc), lambda i, j: (jnp.minimum((i + 1) * nh, last), c0 + j))]


def _shift3(x):
    n = x.shape[0]
    return _roll(x, 1, 0), _roll(x, n - 1, 0)


def conv_fwd(gfn, x, c0, ncol, tc, w, b, tco, out_dtype, name, tm=256):
    L = x.shape[0]
    tm = _tile(L, tm)
    ni = L // tm

    def body(x_ref, p_ref, n_ref, w_ref, b_ref, o_ref):
        i = pl.program_id(0)
        xv = x_ref[...].astype(f32)
        xp = jnp.where(i == 0, 0.0, p_ref[HALO - 1:HALO, :].astype(f32))
        xn = jnp.where(i == ni - 1, 0.0, n_ref[0:1, :].astype(f32))
        rid = lax.broadcasted_iota(jnp.int32, xv.shape, 0)
        dn, up = _shift3(xv)
        dn = jnp.where(rid == 0, xp, dn)
        up = jnp.where(rid == tm - 1, xn, up)
        c = w_ref[0:1, :] * dn + w_ref[1:2, :] * xv + w_ref[2:3, :] * up + b_ref[...]
        o_ref[...] = gfn(c).astype(o_ref.dtype)

    return pl.pallas_call(
        body, name=name, grid=(ni, ncol),
        in_specs=_conv_specs(tm, tc, c0, L) + [pl.BlockSpec((3, tc), lambda i, j: (0, j)), pl.BlockSpec((1, tc), lambda i, j: (0, j))],
        out_specs=pl.BlockSpec((tm, tco), lambda i, j: (i, j)), out_shape=jax.ShapeDtypeStruct((L, ncol * tco), out_dtype),
        compiler_params=_params(("parallel", "parallel")),
    )(x, x, x, w, b)


def conv_bwd(gfn, x, c0, ncol, tc, w, b, dys, tco, name, tm=256):
    L = x.shape[0]
    tm = _tile(L, tm)
    ni = L // tm
    nd = len(dys)

    def body(*refs):
        x_ref, xp_ref, xn_ref = refs[:3]
        d_refs = refs[3:3 + 3 * nd]
        w_ref, b_ref, dx_ref, dw_ref, db_ref = refs[3 + 3 * nd:]
        i = pl.program_id(1)
        first, lastb = i == 0, i == ni - 1

        def ext(m, p, n):
            return jnp.concatenate([jnp.where(first, 0.0, p[...].astype(f32)), m[...].astype(f32),
                                    jnp.where(lastb, 0.0, n[...].astype(f32))], axis=0)

        xe = ext(x_ref, xp_ref, xn_ref)
        de = ext(*d_refs[:3])
        for q in range(1, nd):
            de = de + ext(*d_refs[3 * q:3 * q + 3])
        x_dn, x_up = _shift3(xe)
        w0, w1, w2 = w_ref[0:1, :], w_ref[1:2, :], w_ref[2:3, :]
        ce = w0 * x_dn + w1 * xe + w2 * x_up + b_ref[...]
        _, vjp = jax.vjp(gfn, ce)
        (dce,) = vjp(de)
        dc_dn, dc_up = _shift3(dce)
        dx_ref[...] = (w0 * dc_up + w1 * dce + w2 * dc_dn)[HALO:HALO + tm, :]
        rid = lax.broadcasted_iota(jnp.int32, dce.shape, 0)
        dci = jnp.where((rid >= HALO) & (rid < HALO + tm), dce, 0.0)
        dw = jnp.concatenate([jnp.sum(dci * x_dn, axis=0, keepdims=True), jnp.sum(dci * xe, axis=0, keepdims=True),
                              jnp.sum(dci * x_up, axis=0, keepdims=True)], axis=0)
        db = jnp.sum(dci, axis=0, keepdims=True)

        @pl.when(first)
        def _():
            dw_ref[...] = dw
            db_ref[...] = db

        @pl.when(i > 0)
        def _():
            dw_ref[...] += dw
            db_ref[...] += db

    res = pl.pallas_call(
        body, name=name, grid=(ncol, ni),
        in_specs=[pl.BlockSpec(s.block_shape, (lambda j, i, f=s.index_map: f(i, j))) for s in _conv_specs(tm, tc, c0, L)]
        + [pl.BlockSpec(s.block_shape, (lambda j, i, f=s.index_map: f(i, j))) for s in _conv_specs(tm, tco, 0, L)] * nd
        + [pl.BlockSpec((3, tc), lambda j, i: (0, j)), pl.BlockSpec((1, tc), lambda j, i: (0, j))],
        out_specs=[pl.BlockSpec((tm, tc), lambda j, i: (i, j)), pl.BlockSpec((3, tc), lambda j, i: (0, j)),
                   pl.BlockSpec((1, tc), lambda j, i: (0, j))],
        out_shape=[jax.ShapeDtypeStruct((L, ncol * tc), f32), jax.ShapeDtypeStruct((3, ncol * tc), f32),
                   jax.ShapeDtypeStruct((1, ncol * tc), f32)],
        compiler_params=_params(("parallel", "arbitrary")),
    )(x, x, x, *[d for d in dys for _ in range(3)], w, b)
    return res


LOG2E = math.log2(math.e)


def flash_fwd(q, k, v, n_q, n_kv, scale, name, side=(), tq=512, tk=8192):
    L = q.shape[0]
    tq, tk = _tile_div(L, tq), _tile_div(L, tk)
    rep = n_q // n_kv
    nkv = L // tk
    c2 = scale * LOG2E

    def body(q_ref, k_ref, v_ref, o_ref, lse_ref):
        qv = q_ref[...]
        ones_lane = lax.broadcasted_iota(jnp.int32, (tk, HPAD), 1) == HPAD - 1

        def step(c, carry):
            m, acc = carry
            off = pl.multiple_of(c * tk, tk)
            s = _mm(qv, k_ref[pl.ds(off, tk), :], _NT)
            m_new = jnp.maximum(m, jnp.max(s, axis=-1, keepdims=True))
            p = jnp.exp2((s - m_new) * c2)
            vv = v_ref[pl.ds(off, tk), :]
            acc = jnp.exp2((m - m_new) * c2) * acc + _mm(p, jnp.where(ones_lane, jnp.ones_like(vv), vv), _NN)
            return m_new, acc

        m, acc = lax.fori_loop(0, nkv, step, (jnp.full((tq, 1), -jnp.inf, f32), jnp.zeros((tq, HPAD), f32)))
        l = acc[:, HPAD - 1:]
        o_ref[...] = jnp.where(lax.broadcasted_iota(jnp.int32, (tq, HPAD), 1) == HPAD - 1, 0.0, acc / l)
        lse_ref[0] = m * scale + jnp.log(l)

    grid = (n_q, L // tq)
    body, ex = _with_exchange(body, 3, 2, side, grid)
    res = pl.pallas_call(
        body, name=name, grid=grid,
        in_specs=[pl.BlockSpec((tq, HPAD), lambda h, i: (i, h)), pl.BlockSpec((L, HPAD), lambda h, i: (0, h // rep)),
                  pl.BlockSpec((L, HPAD), lambda h, i: (0, h // rep))] + ex.specs,
        out_specs=[pl.BlockSpec((tq, HPAD), lambda h, i: (i, h)), pl.BlockSpec((1, tq, 1), lambda h, i: (h, i, 0))] + ex.specs,
        out_shape=[jax.ShapeDtypeStruct((L, n_q * HPAD), f32), jax.ShapeDtypeStruct((n_q, L, 1), f32)] + ex.out_shape,
        scratch_shapes=ex.scratch, compiler_params=_params(("arbitrary", "arbitrary") if side else ("parallel", "parallel")),
    )(q, k, v, *ex.args)
    return res[0], res[1], list(res[2:])


def flash_bwd(q, k, v, o, lse, do, n_q, n_kv, scale, name, side=(), tq=1024, tk=1024):
    L = q.shape[0]
    tq, tk = _tile_div(L, tq), _tile_div(L, tk)
    rep = n_q // n_kv
    c2 = scale * LOG2E

    def body(k_ref, v_ref, q_ref, do_ref, o_ref, lse_ref, dq_ref, dk_ref, dv_ref):
        j, i = pl.program_id(1), pl.program_id(2)
        kv, vv = k_ref[...], v_ref[...]
        rows = pl.ds(pl.multiple_of(i * tq, tq), tq)
        dk, dv = jnp.zeros((tk, HPAD), f32), jnp.zeros((tk, HPAD), f32)
        for r in range(rep):
            cols = slice(r * HPAD, (r + 1) * HPAD)
            qv = q_ref[:, cols]
            dov = do_ref[:, cols]
            delta = jnp.sum(dov * o_ref[:, cols], axis=-1, keepdims=True)
            p = jnp.exp2(_mm(qv, kv, _NT) * c2 - lse_ref[r] * LOG2E)
            dv = dv + _mm(p, dov, _TN)
            ds = p * (_mm(dov, vv, _NT) - delta) * scale
            dk = dk + _mm(ds, qv, _TN)
            dq = _mm(ds, kv, _NN)

            @pl.when(j == 0)
            def _(dq=dq, cols=cols):
                dq_ref[rows, cols] = dq

            @pl.when(j > 0)
            def _(dq=dq, cols=cols):
                dq_ref[rows, cols] += dq

        @pl.when(i == 0)
        def _():
            dk_ref[...] = dk
            dv_ref[...] = dv

        @pl.when(i > 0)
        def _():
            dk_ref[...] += dk
            dv_ref[...] += dv

    w = rep * HPAD
    grid = (n_kv, L // tk, L // tq)
    body, ex = _with_exchange(body, 6, 3, side, grid)
    res = pl.pallas_call(
        body, name=name, grid=grid,
        in_specs=[pl.BlockSpec((tk, HPAD), lambda g, j, i: (j, g)), pl.BlockSpec((tk, HPAD), lambda g, j, i: (j, g)),
                  pl.BlockSpec((tq, w), lambda g, j, i: (i, g)), pl.BlockSpec((tq, w), lambda g, j, i: (i, g)),
                  pl.BlockSpec((tq, w), lambda g, j, i: (i, g)), pl.BlockSpec((rep, tq, 1), lambda g, j, i: (g, i, 0))] + ex.specs,
        out_specs=[pl.BlockSpec((L, w), lambda g, j, i: (0, g)), pl.BlockSpec((tk, HPAD), lambda g, j, i: (j, g)),
                   pl.BlockSpec((tk, HPAD), lambda g, j, i: (j, g))] + ex.specs,
        out_shape=[jax.ShapeDtypeStruct((L, n_q * HPAD), f32), jax.ShapeDtypeStruct((L, n_kv * HPAD), f32),
                   jax.ShapeDtypeStruct((L, n_kv * HPAD), f32)] + ex.out_shape,
        scratch_shapes=ex.scratch, compiler_params=_params(("arbitrary",) * 3 if side else ("parallel", "arbitrary", "arbitrary")),
    )(k, v, q, do, o, lse, *ex.args)
    return res[0], res[1], res[2], list(res[3:])


N_SCAN_HEADS = 4
STATE_ROWS = N_SCAN_HEADS * HEAD


def _tri(n, rev):
    i = lax.broadcasted_iota(jnp.int32, (n, n), 0)
    k = lax.broadcasted_iota(jnp.int32, (n, n), 1)
    return ((k >= i), (k > i)) if rev else ((k <= i), (k < i))


def _decay(acol, incl):
    n = acol.shape[0]
    m1 = jnp.broadcast_to(acol, (n, n))
    return jnp.exp(jnp.where(incl, m1 - m1.T, -jnp.inf))


def _ssd_chunk(S, xbc, win, a_log, dt_bias, rev):
    Q = xbc.shape[0]
    d = 1 if rev else 0
    incl, _ = _tri(Q, rev)
    dt = _softplus(win[:, 4 * d:4 * d + 4] + dt_bias[d:d + 1, :])
    a = dt * (-jnp.exp(a_log[d:d + 1, :]))
    acum = _mm(incl.astype(f32), a, _NN, HI)
    tot = jnp.sum(a, axis=0, keepdims=True)
    xs, Bm, Cm = xbc[:, :256], xbc[:, 256:384], xbc[:, 384:512]
    H, W = N_SCAN_HEADS, N_SCAN_HEADS * HEAD
    lanes = lambda x: _cat([jnp.broadcast_to(x[:, h:h + 1], (x.shape[0], HEAD)) for h in range(H)])
    xdt = xs * lanes(dt)
    cb = [_mm(Cm[:, g * HEAD:(g + 1) * HEAD], Bm[:, g * HEAD:(g + 1) * HEAD], _NT) for g in range(2)]
    scores = _cat([cb[h // 2] * _decay(acum[:, h:h + 1], incl) for h in range(H)])
    own = (lax.broadcasted_iota(jnp.int32, (H * Q, W), 0) // Q) == (lax.broadcasted_iota(jnp.int32, (H * Q, W), 1) // HEAD)
    y = _mm(scores, jnp.where(own, jnp.concatenate([xdt] * H, axis=0), 0.0), _NN)
    grp = (lax.broadcasted_iota(jnp.int32, (W, 2 * HEAD), 0) // (2 * HEAD)) == (lax.broadcasted_iota(jnp.int32, (W, 2 * HEAD), 1) // HEAD)
    y = y + _mm(Cm, jnp.where(grp, _cat([S, S]), 0.0), _NT) * jnp.exp(lanes(acum))
    st = _mm(xdt * jnp.exp(lanes(tot - acum)), Bm, _TN)
    first = lax.broadcasted_iota(jnp.int32, (W, HEAD), 0) < 2 * HEAD
    exp_tot = jnp.concatenate([jnp.broadcast_to(jnp.exp(tot[:, h:h + 1]), (HEAD, 1)) for h in range(H)], axis=0)
    return S * exp_tot + jnp.where(first, st[:, :HEAD], st[:, HEAD:]), y


@functools.partial(jax.custom_vjp, nondiff_argnums=(1,))
def _inv_unit_tri(Lm, order):
    n = Lm.shape[0]
    eye = (lax.broadcasted_iota(jnp.int32, (n, n), 0) == lax.broadcasted_iota(jnp.int32, (n, n), 1)).astype(f32)
    P = -Lm
    T = eye + P
    k = 1
    while 2 * k < order:
        P = _mm(P, P, _NN, lax.Precision.HIGH)
        T = T + _mm(T, P, _NN, lax.Precision.HIGH)
        k *= 2
    return T


def _inv_unit_tri_fwd(Lm, order):
    T = _inv_unit_tri(Lm, order)
    return T, T


def _inv_unit_tri_bwd(order, T, dT):
    return (-_mm(_mm(T, dT, _TN, lax.Precision.HIGH), T, _NT, lax.Precision.HIGH),)


_inv_unit_tri.defvjp(_inv_unit_tri_fwd, _inv_unit_tri_bwd)


@jax.custom_vjp
def _inv_known(Lm, T):
    return T


def _inv_known_fwd(Lm, T):
    return T, T


def _inv_known_bwd(T, dT):
    return _inv_unit_tri_bwd(None, T, dT) + (jnp.zeros_like(T),)


_inv_known.defvjp(_inv_known_fwd, _inv_known_bwd)


def _dn_chunk(S, qkv, win, a_log, dt_bias, rev, kept=None):
    Q = qkv.shape[0]
    H, R = N_SCAN_HEADS, N_SCAN_HEADS * qkv.shape[0]
    d = 1 if rev else 0
    beta = jax.nn.sigmoid(win[:, 4 * d:4 * d + 4])
    gl = -jnp.exp(a_log[d:d + 1, :]) * _softplus(win[:, 8 + 4 * d:12 + 4 * d] + dt_bias[d:d + 1, :])
    G = _mm(_tri(Q, rev)[0].astype(f32), gl, _NN, HI)
    tot = jnp.sum(gl, axis=0, keepdims=True)
    rows = lambda x: jnp.concatenate(_heads(x, H), axis=0)
    col = lambda x: jnp.concatenate([x[:, h:h + 1] for h in range(H)], axis=0)
    per_head = lambda x, n: jnp.concatenate([jnp.broadcast_to(x[:, h:h + 1], (n, 1)) for h in range(H)], axis=0)
    r = lax.broadcasted_iota(jnp.int32, (R, R), 0)
    c = lax.broadcasted_iota(jnp.int32, (R, R), 1)
    same = (r // Q) == (c // Q)
    incl = same & ((c >= r) if rev else (c <= r))
    strict = same & ((c > r) if rev else (c < r))
    own = (lax.broadcasted_iota(jnp.int32, (R, H * HEAD), 0) // Q) == (lax.broadcasted_iota(jnp.int32, (R, H * HEAD), 1) // HEAD)
    blk = lambda x: jnp.where(own, _cat([x] * H), 0.0)
    q, k, v = rows(qkv[:, :256]) * (HEAD ** -0.5), rows(qkv[:, 256:512]), rows(qkv[:, 512:])
    b, Gs, tots = col(beta), col(G), per_head(tot, Q)
    dec = _decay(Gs, incl)
    kb = k * b
    Lm = jnp.where(strict, _mm(kb, k, _NT) * dec, 0.0)
    T = _inv_unit_tri(Lm, Q) if kept is None else _inv_known(Lm, blk(kept))
    eG = jnp.exp(Gs)
    uw = _mm(T, _cat([v * b, kb * eG]), _NN)
    vnew = uw[:, :HEAD] - _mm(blk(uw[:, HEAD:]), S, _NN)
    o = _mm(blk(q * eG), S, _NN) + _mm(_mm(q, k, _NT) * dec, vnew, _NN)
    S_next = S * jnp.exp(per_head(tot, HEAD)) + _mm(blk(k * jnp.exp(tots - Gs)), vnew, _TN)
    out = S_next, _cat([o[h * Q:(h + 1) * Q, :] for h in range(H)])
    if kept is None:
        out += (jnp.concatenate([T[h * Q:(h + 1) * Q, h * Q:(h + 1) * Q] for h in range(H)], axis=0),)
    return out


def scan_fwd(chunk_fn, Q, x, p, win_blk, a_log, dt_bias, name, keeps=False):
    L, W = x.shape
    nc = L // Q
    fidx, ridx = (lambda t: t), (lambda t: nc - 1 - t)

    def body(xf_ref, xr_ref, wf_ref, wr_ref, al_ref, db_ref, yf_ref, yr_ref, sf_ref, sr_ref, *rest):
        (sf_scr, sr_scr), keep_refs = rest[-2:], rest[:-2]

        @pl.when(pl.program_id(0) == 0)
        def _():
            sf_scr[...] = jnp.zeros_like(sf_scr)
            sr_scr[...] = jnp.zeros_like(sr_scr)

        for d, (x_ref, w_ref, y_ref, sin_ref, s_scr) in enumerate(((xf_ref, wf_ref, yf_ref, sf_ref, sf_scr),
                                                                   (xr_ref, wr_ref, yr_ref, sr_ref, sr_scr))):
            S = s_scr[...]
            sin_ref[...] = S
            res = chunk_fn(S, x_ref[...], w_ref[...], al_ref[...], db_ref[...], d == 1)
            s_scr[...], y_ref[...] = res[:2]
            if keeps:
                keep_refs[d][...] = res[2]

    row = lambda w, idx, b=0: pl.BlockSpec((Q, w), lambda t: (idx(t), b))
    small = pl.BlockSpec(a_log.shape, lambda t: (0, 0))
    state = lambda idx: pl.BlockSpec((STATE_ROWS, HEAD), lambda t: (idx(t), 0))
    n_state = 4 if keeps else 2
    return pl.pallas_call(
        body, name=name, grid=(nc,),
        in_specs=[row(W, fidx), row(W, ridx), row(HPAD, fidx, win_blk), row(HPAD, ridx, win_blk), small, small],
        out_specs=[row(STATE_ROWS, fidx), row(STATE_ROWS, ridx)] + [state(fidx), state(ridx)] * (n_state // 2),
        out_shape=[jax.ShapeDtypeStruct((L, STATE_ROWS), f32)] * 2 + [jax.ShapeDtypeStruct((nc * STATE_ROWS, HEAD), f32)] * n_state,
        scratch_shapes=[pltpu.VMEM((STATE_ROWS, HEAD), f32)] * 2, compiler_params=_params(("arbitrary",)),
    )(x, x, p, p, a_log, dt_bias)


def scan_bwd(chunk_fn, Q, x, p, win_blk, a_log, dt_bias, s_fwd, s_rev, dy, name, kept=()):
    L, W = x.shape
    nc = L // Q
    fidx, ridx = (lambda t: nc - 1 - t), (lambda t: t)
    nk = len(kept)

    def body(xf_ref, xr_ref, wf_ref, wr_ref, al_ref, db_ref, sf_ref, sr_ref, dyf_ref, dyr_ref, *rest):
        keep_refs = rest[:nk]
        dxf_ref, dxr_ref, dwf_ref, dwr_ref, dal_ref, ddb_ref, dsf_scr, dsr_scr = rest[nk:]

        @pl.when(pl.program_id(0) == 0)
        def _():
            dsf_scr[...] = jnp.zeros_like(dsf_scr)
            dsr_scr[...] = jnp.zeros_like(dsr_scr)
            dal_ref[...] = jnp.zeros_like(dal_ref)
            ddb_ref[...] = jnp.zeros_like(ddb_ref)

        dal_sum, ddb_sum = dal_ref[...], ddb_ref[...]
        for rev, x_ref, w_ref, sin_ref, dy_ref, dx_ref, dw_ref, ds_scr in (
                (False, xf_ref, wf_ref, sf_ref, dyf_ref, dxf_ref, dwf_ref, dsf_scr),
                (True, xr_ref, wr_ref, sr_ref, dyr_ref, dxr_ref, dwr_ref, dsr_scr)):
            extra = (keep_refs[int(rev)][...],) if nk else ()
            _, vjp = jax.vjp(lambda S, xv, wv, al, db, rev=rev, extra=extra: chunk_fn(S, xv, wv, al, db, rev, *extra),
                             sin_ref[...], x_ref[...], w_ref[...], al_ref[...], db_ref[...])
            ds_scr[...], dx_ref[...], dw_ref[...], dal, ddb = vjp((ds_scr[...], dy_ref[...]))
            dal_sum, ddb_sum = dal_sum + dal, ddb_sum + ddb
        dal_ref[...] = dal_sum
        ddb_ref[...] = ddb_sum

    row = lambda w, idx, b=0: pl.BlockSpec((Q, w), lambda t: (idx(t), b))
    small = pl.BlockSpec(a_log.shape, lambda t: (0, 0))
    state = lambda idx: pl.BlockSpec((STATE_ROWS, HEAD), lambda t: (idx(t), 0))
    return pl.pallas_call(
        body, name=name, grid=(nc,),
        in_specs=[row(W, fidx), row(W, ridx), row(HPAD, fidx, win_blk), row(HPAD, ridx, win_blk), small, small,
                  state(fidx), state(ridx), row(STATE_ROWS, fidx), row(STATE_ROWS, ridx)] + [state(fidx), state(ridx)][:nk],
        out_specs=[row(W, fidx), row(W, ridx), row(HPAD, fidx), row(HPAD, ridx), small, small],
        out_shape=[jax.ShapeDtypeStruct((L, W), f32)] * 2 + [jax.ShapeDtypeStruct((L, HPAD), f32)] * 2
        + [jax.ShapeDtypeStruct(a_log.shape, f32)] * 2,
        scratch_shapes=[pltpu.VMEM((STATE_ROWS, HEAD), f32)] * 2, compiler_params=_params(("arbitrary",)),
    )(x, x, p, p, a_log, dt_bias, s_fwd, s_rev, dy, dy, *kept)


def _cat(xs):
    return jnp.concatenate(xs, axis=-1)


def _a_prep(pa, cos, sin, qn, wuq, kvn, wk, wv):
    cq, ckv, kr = pa[:, :256], pa[:, 256:384], pa[:, 384:512]
    q = _rope(_mm(_rms(cq, qn, A_Q_LORA), wuq, _NN), _cat([cos] * 4), _cat([sin] * 4), A_NOPE, A_ROPE // 4)
    kvh = _rms(ckv, kvn)
    k = _mm(kvh, wk, _NN) + _cat([_rope(kr, cos, sin, A_NOPE, A_ROPE // 4)] * 4)
    return q, k, _mm(kvh, wv, _NN)


def _b_prep(pb, cos, sin, qn, kn):
    q = _pad_heads([_rms(h, qn) for h in _heads(pb[:, :256], 4)])
    k = _pad_heads([_rms(h, kn) for h in _heads(pb[:, 256:384], 2)])
    v = _pad_heads(_heads(pb[:, 384:512], 2))
    return _rope(q, _cat([cos] * 4), _cat([sin] * 4), 0, HEAD // 4), _rope(k, _cat([cos] * 2), _cat([sin] * 2), 0, HEAD // 4), v


def _c_act(c):
    return _silu(c)


def _d_act(c):
    s = _silu(c)
    return _cat([h * lax.rsqrt(jnp.sum(h * h, axis=-1, keepdims=True) + 1e-6) for h in _heads(s[:, :512], 8)] + [s[:, 512:]])


def _ffn_act(c):
    return _silu(c[:, :FF_BLK]) * c[:, FF_BLK:]


def _mix_post(oa, ob, yc_f, yc_r, xbc, cz, od_f, od_r, dz, a_out, b_out, dskip, c_out, d_out):
    o_a = _rms(_unpad_heads(oa, 4), a_out)
    o_b = _rms(_unpad_heads(ob, 4), b_out)
    skip = _cat([jnp.broadcast_to(dskip[:, h:h + 1], (1, HEAD)) for h in range(4)])
    o_c = _rms((yc_f + yc_r + xbc[:, :256] * skip) * _silu(cz), c_out)
    o_d = _cat([_rms(h, d_out) for h in _heads(od_f + od_r, 4)]) * _silu(dz)
    return _cat([o_a, o_b, o_c, o_d])


def _post_mix(x, y1, g_pm, g_pf):
    x1 = x + _rms(y1, g_pm)
    return x1, _rms(x1, g_pf)


def _post_ffn(x1, y2, g):
    return x1 + _rms(y2, g)


def _pre_bwd(x, g):
    return x, _rms(x, g)


def _layer_fwd(x, W, tabs, n, side=(), late_side=(), late_weights=None):
    mx = MXU_DTYPE
    cos_a, sin_a, cos_b, sin_b = tabs
    full = lambda a: (a, a.shape[1], 0)
    s = {"x": x}
    (s["h"],) = rowwise(_rms, [full(x)], [W["g_pre"]], [(D_MODEL, mx)], n + "pre")
    p = s["p"] = matmul(s["h"], W["w_in"], "nn", f32, n + "in")
    s["qa"], s["ka"], s["va"] = rowwise(_a_prep, [(p, 512, SEG_A // 512), full(cos_a), full(sin_a)],
                                        [W["a_qn"], W["a_wuq"], W["a_kvn"], W["a_wk"], W["a_wv"]], [(512, mx)] * 3, n + "a_prep")
    s["qb"], s["kb"], s["vb"] = rowwise(_b_prep, [(p, 512, SEG_B // 512), full(cos_b), full(sin_b)], [W["b_qn"], W["b_kn"]],
                                        [(512, mx), (256, mx), (256, mx)], n + "b_prep")
    s["oa"], s["lsea"], side_out = flash_fwd(s["qa"], s["ka"], s["va"], 4, 4, (A_NOPE + A_ROPE) ** -0.5, n + "a_attn", side)
    s["ob"], s["lseb"], late_out = flash_fwd(s["qb"], s["kb"], s["vb"], 4, 2, HEAD ** -0.5, n + "b_attn", late_side)
    if late_weights is not None:
        W = {**W, **late_weights(late_out)}
    xbc = s["xbc"] = conv_fwd(_c_act, p, SEG_CX // 512, 1, 512, W["c_cw"], W["c_cb"], 512, f32, n + "c_conv")
    s["yc_f"], s["yc_r"], s["sc_f"], s["sc_r"] = scan_fwd(_ssd_chunk, SSD_CHUNK, xbc, p, SEG_CDT // HPAD, W["c_alog"], W["c_dtb"], n + "c_ssd")
    qkv = s["qkv"] = conv_fwd(_d_act, p, SEG_DX // 768, 1, 768, W["d_cw"], W["d_cb"], 768, f32, n + "d_conv")
    s["od_f"], s["od_r"], s["sd_f"], s["sd_r"], s["td_f"], s["td_r"] = scan_fwd(
        _dn_chunk, DN_CHUNK, qkv, p, SEG_DAB // HPAD, W["d_alog"], W["d_dtb"], n + "d_dn", keeps=True)
    (s["omix"],) = rowwise(_mix_post, _mix_rows(s), _mix_params(W), [(D_MODEL, mx)], n + "mix_post")
    s["y1"] = matmul(s["omix"], W["w_out"], "nn", f32, n + "out")
    s["x1"], s["h2"] = rowwise(_post_mix, [full(x), full(s["y1"])], [W["g_pm"], W["g_pf"]], [(D_MODEL, f32), (D_MODEL, mx)], n + "post_mix")
    s["u"] = matmul(s["h2"], W["f_win"], "nn", f32, n + "f_in")
    s["a"] = conv_fwd(_ffn_act, s["u"], 0, D_FF // FF_BLK, 2 * FF_BLK, W["f_cw"], W["f_cb"], FF_BLK, mx, n + "f_conv", tm=512)
    s["y2"] = matmul(s["a"], W["f_wout"], "nn", f32, n + "f_out")
    (x2,) = rowwise(_post_ffn, [full(s["x1"]), full(s["y2"])], [W["g_po"]], [(D_MODEL, f32)], n + "post_ffn")
    return x2, s, side_out, W


def _mix_rows(s):
    p = s["p"]
    return [(s["oa"], 512, 0), (s["ob"], 512, 0), (s["yc_f"], 256, 0), (s["yc_r"], 256, 0), (s["xbc"], 512, 0),
            (p, 256, SEG_CZ // 256), (s["od_f"], 256, 0), (s["od_r"], 256, 0), (p, 256, SEG_DZ // 256)]


def _mix_params(W):
    return [W["a_out"], W["b_out"], W["c_dskip"], W["c_out"], W["d_out"]]


def _layer_bwd(s, dx2, W, tabs, n, side=(), late_side=None):
    cos_a, sin_a, cos_b, sin_b = tabs
    full = lambda a: (a, a.shape[1], 0)
    p, g = s["p"], {}
    (dx1, dy2), (g["g_po"],) = rowwise_vjp(_post_ffn, [full(s["x1"]), full(s["y2"])], [W["g_po"]], [full(dx2)], n + "post_ffn_b", [0, 1], [0])
    da = matmul(dy2, W["f_wout"], "nt", f32, n + "f_out_dx")
    g["f_wout"] = matmul(s["a"], dy2, "tn", f32, n + "f_out_dw")
    du, g["f_cw"], g["f_cb"] = conv_bwd(_ffn_act, s["u"], 0, D_FF // FF_BLK, 2 * FF_BLK, W["f_cw"], W["f_cb"], [da], FF_BLK, n + "f_conv_b", tm=512)
    dh2 = matmul(du, W["f_win"], "nt", f32, n + "f_in_dx")
    g["f_win"] = matmul(s["h2"], du, "tn", f32, n + "f_in_dw")
    (dx, dy1), (g["g_pm"], g["g_pf"]) = rowwise_vjp(_post_mix, [full(s["x"]), full(s["y1"])], [W["g_pm"], W["g_pf"]],
                                                   [full(dx1), full(dh2)], n + "post_mix_b", [0, 1], [0, 1])
    domix = matmul(dy1, W["w_out"], "nt", f32, n + "out_dx")
    g["w_out"] = matmul(s["omix"], dy1, "tn", f32, n + "out_dw")
    (doa, dob, dyc, dxbc_skip, dcz, dod, ddz), (g["a_out"], g["b_out"], g["c_dskip"], g["c_out"], g["d_out"]) = rowwise_vjp(
        _mix_post, _mix_rows(s), _mix_params(W), [full(domix)], n + "mix_post_b", [0, 1, 2, 4, 5, 6, 8], list(range(5)))
    dqkv_f, dqkv_r, ddab_f, ddab_r, g["d_alog"], g["d_dtb"] = scan_bwd(
        _dn_chunk, DN_CHUNK, s["qkv"], p, SEG_DAB // HPAD, W["d_alog"], W["d_dtb"], s["sd_f"], s["sd_r"], dod, n + "d_dn_b",
        kept=(s["td_f"], s["td_r"]))
    ddx, g["d_cw"], _ = conv_bwd(_d_act, p, SEG_DX // 768, 1, 768, W["d_cw"], W["d_cb"], [dqkv_f, dqkv_r], 768, n + "d_conv_b")
    dxbc_f, dxbc_r, dcdt_f, dcdt_r, g["c_alog"], g["c_dtb"] = scan_bwd(
        _ssd_chunk, SSD_CHUNK, s["xbc"], p, SEG_CDT // HPAD, W["c_alog"], W["c_dtb"], s["sc_f"], s["sc_r"], dyc, n + "c_ssd_b")
    dcx, g["c_cw"], g["c_cb"] = conv_bwd(_c_act, p, SEG_CX // 512, 1, 512, W["c_cw"], W["c_cb"], [dxbc_f, dxbc_r, dxbc_skip], 512,
                                         n + "c_conv_b")
    (dsmall,) = rowwise(lambda a, b, c, d: _cat([a + b, c + d]), [full(dcdt_f), full(dcdt_r), full(ddab_f), full(ddab_r)], [],
                        [(2 * HPAD, f32)], n + "dwin_sum")
    dqa, dka, dva, side_out = flash_bwd(s["qa"], s["ka"], s["va"], s["oa"], s["lsea"], doa, 4, 4, (A_NOPE + A_ROPE) ** -0.5,
                                        n + "a_attn_b", side)
    dqb, dkb, dvb, late_out = flash_bwd(s["qb"], s["kb"], s["vb"], s["ob"], s["lseb"], dob, 4, 2, HEAD ** -0.5, n + "b_attn_b",
                                        late_side(g) if late_side is not None else ())
    (dpa,), (g["a_qn"], g["a_wuq"], g["a_kvn"], g["a_wk"], g["a_wv"]) = rowwise_vjp(
        _a_prep, [(p, 512, SEG_A // 512), full(cos_a), full(sin_a)], [W["a_qn"], W["a_wuq"], W["a_kvn"], W["a_wk"], W["a_wv"]],
        [full(dqa), full(dka), full(dva)], n + "a_prep_b", [0], list(range(5)))
    (dpb,), (g["b_qn"], g["b_kn"]) = rowwise_vjp(_b_prep, [(p, 512, SEG_B // 512), full(cos_b), full(sin_b)], [W["b_qn"], W["b_kn"]],
                                               [full(dqb), full(dkb), full(dvb)], n + "b_prep_b", [0], [0, 1])
    dp = jnp.concatenate([dpa, dpb, dcx, ddx, dcz, ddz, dsmall], axis=1)
    dh = matmul(dp, W["w_in"], "nt", f32, n + "in_dx")
    g["w_in"] = matmul(s["h"], dp, "tn", f32, n + "in_dw")
    (dx0,), (g["g_pre"],) = rowwise_vjp(_pre_bwd, [full(s["x"])], [W["g_pre"]], [full(dx), full(dh)], n + "pre_b", [0], [0])
    return dx0, g, side_out, late_out


def loss_and_grad(y, target, name):
    L, D = y.shape
    tm = _tile(L, 512)

    def body(y_ref, t_ref, loss_ref, dy_ref):
        e = y_ref[...] - t_ref[...]
        dy_ref[...] = e * (1.0 / D)
        part = 0.5 * jnp.sum(jnp.sum(e * e, axis=1, keepdims=True) * (1.0 / D), axis=0, keepdims=True)

        @pl.when(pl.program_id(0) == 0)
        def _():
            loss_ref[...] = part

        @pl.when(pl.program_id(0) > 0)
        def _():
            loss_ref[...] += part

    row = pl.BlockSpec((tm, D), lambda i: (i, 0))
    return pl.pallas_call(
        body, name=name, grid=(L // tm,), in_specs=[row, row], out_specs=[pl.BlockSpec((1, 1), lambda i: (0, 0)), row],
        out_shape=[jax.ShapeDtypeStruct((1, 1), f32), jax.ShapeDtypeStruct((L, D), f32)], compiler_params=_params(("arbitrary",)),
    )(y, target)


_IN_SEGS = [(0, 192), (256, 128), (384 + A_NOPE, 32), (512, 256), (768, 128), (896, 128), (SEG_CZ, 256), (SEG_CX, 512),
            (SEG_CDT, 8), (SEG_DX, 768), (SEG_DZ, 256), (SEG_DAB, 16)]


def _pad_in(w):
    src, pieces = 0, {}
    for off, wd in _IN_SEGS:
        pieces[off] = w[..., src:src + wd]
        src += wd
    out, pos = [], 0
    for off in sorted(pieces):
        if off > pos:
            out.append(jnp.zeros(w.shape[:-1] + (off - pos,), w.dtype))
        out.append(pieces[off])
        pos = off + pieces[off].shape[-1]
    out.append(jnp.zeros(w.shape[:-1] + (PIN - pos,), w.dtype))
    return jnp.concatenate(out, axis=-1)


def _unpad_in(wp):
    return jnp.concatenate([wp[..., off:off + wd] for off, wd in _IN_SEGS], axis=-1)


def _pad_last(a, n):
    return jnp.pad(a, [(0, 0)] * (a.ndim - 1) + [(0, n - a.shape[-1])])


def _ff_interleave(w):
    lead = w.shape[:-1]
    return jnp.stack([w[..., :D_FF].reshape(lead + (D_FF // FF_BLK, FF_BLK)), w[..., D_FF:].reshape(lead + (D_FF // FF_BLK, FF_BLK))],
                     axis=-2).reshape(lead + (2 * D_FF,))


def _ff_deinterleave(w):
    lead = w.shape[:-1]
    t = w.reshape(lead + (D_FF // FF_BLK, 2, FF_BLK))
    return jnp.concatenate([t[..., 0, :].reshape(lead + (D_FF,)), t[..., 1, :].reshape(lead + (D_FF,))], axis=-1)


def _row(v):
    return v.reshape(1, -1)


_LATE = ["w_out", "f_w_in", "f_w_out"]


def _late_kernel_weights(P, l, wdt):
    return {"w_out": P["w_out"][l].astype(wdt), "f_win": _ff_interleave(P["f_w_in"][l]).astype(wdt), "f_wout": P["f_w_out"][l].astype(wdt)}


def _kernel_weights(P, l, wdt):
    late = {} if P["w_out"][l] is None else _late_kernel_weights(P, l, wdt)
    uq = P["a_w_uq"][l].reshape(A_Q_LORA, 4, A_NOPE + A_ROPE)
    ukv = P["a_w_ukv"][l].reshape(A_KV_LORA, 4, 2 * HEAD)
    z = jnp.zeros((A_KV_LORA, 4, HEAD), ukv.dtype)
    return {
        "g_pre": _row(P["pre_mix_norm"][l]), "w_in": _pad_in(P["w_in"][l]).astype(wdt),
        "a_qn": _pad_last(_row(P["a_q_norm"][l]), 256),
        "a_wuq": jnp.pad(_pad_last(uq, HPAD).reshape(A_Q_LORA, 4 * HPAD), ((0, 256 - A_Q_LORA), (0, 0))).astype(wdt),
        "a_kvn": _row(P["a_kv_norm"][l]),
        "a_wk": jnp.concatenate([ukv[..., :HEAD], z], axis=-1).reshape(A_KV_LORA, 4 * HPAD).astype(wdt),
        "a_wv": jnp.concatenate([ukv[..., HEAD:], z], axis=-1).reshape(A_KV_LORA, 4 * HPAD).astype(wdt),
        "a_out": _row(P["a_out_norm"][l]), "b_qn": _row(P["b_q_norm"][l]), "b_kn": _row(P["b_k_norm"][l]), "b_out": _row(P["b_out_norm"][l]),
        "c_cw": P["c_conv_w"][l], "c_cb": _row(P["c_conv_b"][l]), "c_alog": P["c_a_log"][l], "c_dtb": P["c_dt_bias"][l],
        "c_dskip": _row(P["c_d_skip"][l]), "c_out": _row(P["c_out_norm"][l]),
        "d_cw": P["d_conv_w"][l], "d_cb": jnp.zeros((1, 768), f32), "d_alog": P["d_a_log"][l], "d_dtb": P["d_dt_bias"][l],
        "d_out": _row(P["d_out_norm"][l]), "g_pm": _row(P["post_mix_norm"][l]), "g_pf": _row(P["pre_ffn_norm"][l]),
        "f_cw": _ff_interleave(P["f_conv_w"][l]), "f_cb": _row(_ff_interleave(P["f_conv_b"][l])),
        "g_po": _row(P["post_ffn_norm"][l]), **late,
    }


def _reference_grads(g):
    uq = g["a_wuq"][:A_Q_LORA].reshape(A_Q_LORA, 4, HPAD)[..., :A_NOPE + A_ROPE].reshape(A_Q_LORA, 4 * (A_NOPE + A_ROPE))
    wk = g["a_wk"].reshape(A_KV_LORA, 4, HPAD)[..., :HEAD]
    wv = g["a_wv"].reshape(A_KV_LORA, 4, HPAD)[..., :HEAD]
    return {
        "pre_mix_norm": g["g_pre"][0], "w_in": _unpad_in(g["w_in"]), "a_q_norm": g["a_qn"][0, :A_Q_LORA], "a_w_uq": uq,
        "a_kv_norm": g["a_kvn"][0], "a_w_ukv": jnp.concatenate([wk, wv], axis=-1).reshape(A_KV_LORA, 8 * HEAD),
        "a_out_norm": g["a_out"][0], "b_q_norm": g["b_qn"][0], "b_k_norm": g["b_kn"][0], "b_out_norm": g["b_out"][0],
        "c_conv_w": g["c_cw"], "c_conv_b": g["c_cb"][0], "c_a_log": g["c_alog"], "c_dt_bias": g["c_dtb"], "c_d_skip": g["c_dskip"][0],
        "c_out_norm": g["c_out"][0], "d_conv_w": g["d_cw"], "d_a_log": g["d_alog"], "d_dt_bias": g["d_dtb"], "d_out_norm": g["d_out"][0],
        "w_out": g["w_out"], "post_mix_norm": g["g_pm"][0], "pre_ffn_norm": g["g_pf"][0], "f_w_in": _ff_deinterleave(g["f_win"]),
        "f_conv_w": _ff_deinterleave(g["f_cw"]), "f_conv_b": _ff_deinterleave(g["f_cb"][0]), "f_w_out": g["f_wout"],
        "post_ffn_norm": g["g_po"][0],
    }


def _rope_tables(L):
    def tables(rot):
        rows = L // GRID_W
        row = jnp.repeat(jnp.arange(rows), GRID_W).astype(f32)
        col = jnp.tile(jnp.arange(GRID_W), rows).astype(f32)
        sec = rot // 2
        inv = ROPE_BASE ** (-jnp.arange(0, sec, 2, dtype=f32) / sec)
        ang = jnp.concatenate([row[:, None] * inv] * 2 + [col[:, None] * inv] * 2, axis=-1)
        return jnp.cos(ang), jnp.sin(ang)

    ca, sa = tables(A_ROPE)
    cb, sb = tables(HEAD)
    one, zero = jnp.ones, jnp.zeros
    return (jnp.concatenate([one((L, A_NOPE), f32), ca, one((L, HPAD - A_NOPE - A_ROPE), f32)], axis=1),
            jnp.concatenate([zero((L, A_NOPE), f32), sa, zero((L, HPAD - A_NOPE - A_ROPE), f32)], axis=1),
            jnp.concatenate([cb, one((L, HPAD - HEAD), f32)], axis=1), jnp.concatenate([sb, zero((L, HPAD - HEAD), f32)], axis=1))


def exchange(items, name):
    n = len(items)

    def body(*refs):
        start, wait = _exchange_ops([g for _, g in items], refs[:n], refs[n:2 * n], *refs[2 * n:])
        start()
        wait()

    ex = _exchange_io(items)
    res = pl.pallas_call(body, name=name, in_specs=ex.specs, out_specs=ex.specs, out_shape=ex.out_shape,
                         scratch_shapes=ex.scratch)(*ex.args)
    return list(res)


def _exchange_ops(modes, src_refs, out_refs, send_sems, recv_sems, local_sems):
    n = len(modes)
    x, y, c = lax.axis_index("x"), lax.axis_index("y"), lax.axis_index("c")
    me = 4 * x + 2 * y + c

    def copy(a, k, arriving):
        px, py, pc = (1 - x if k & 4 else x), (1 - y if k & 2 else y), (1 - c if k & 1 else c)
        pid = 4 * px + 2 * py + pc
        sem = a * (N_DEV - 1) + k - 1
        return pltpu.make_async_remote_copy(
            src_ref=src_refs[a] if modes[a] else src_refs[a].at[pid], dst_ref=out_refs[a].at[pid if arriving else me],
            send_sem=send_sems.at[sem], recv_sem=recv_sems.at[sem], device_id=(px, py, pc), device_id_type=pl.DeviceIdType.MESH)

    def local(a):
        return pltpu.make_async_copy(src_refs[a] if modes[a] else src_refs[a].at[me], out_refs[a].at[me], local_sems.at[a])

    pairs = [(a, k) for k in range(1, N_DEV) for a in range(n)]

    def start():
        for a, k in pairs:
            copy(a, k, False).start()
        for a in range(n):
            local(a).start()

    def wait():
        for a, k in pairs:
            copy(a, k, False).wait_send()
        for a, k in pairs:
            copy(a, k, True).wait_recv()
        for a in range(n):
            local(a).wait()

    return start, wait


class _ExchangeIO:
    def __init__(self, items):
        n = len(items)
        self.args = [s for s, _ in items]
        self.specs = [pl.BlockSpec(memory_space=pl.ANY)] * n
        self.out_shape = [jax.ShapeDtypeStruct((N_DEV,) + tuple(s.shape if g else s.shape[1:]), s.dtype) for s, g in items]
        self.scratch = [pltpu.SemaphoreType.DMA((n * (N_DEV - 1),)), pltpu.SemaphoreType.DMA((n * (N_DEV - 1),)),
                        pltpu.SemaphoreType.DMA((n,))] if n else []


def _exchange_io(items):
    return _ExchangeIO(list(items))


def _with_exchange(body, n_in, n_out, side, grid):
    ex = _exchange_io(side)
    n = len(ex.args)
    if not n:
        return body, ex
    modes = [g for _, g in side]

    def wrapped(*refs):
        ins, src = refs[:n_in], refs[n_in:n_in + n]
        outs, dst = refs[n_in + n:n_in + n + n_out], refs[n_in + n + n_out:n_in + 2 * n + n_out]
        sems = refs[n_in + 2 * n + n_out:]
        start, wait = _exchange_ops(modes, src, dst, *sems)
        ids = [pl.program_id(a) for a in range(len(grid))]
        first, last = ids[0] == 0, ids[0] == grid[0] - 1
        for a in range(1, len(grid)):
            first, last = first & (ids[a] == 0), last & (ids[a] == grid[a] - 1)
        pl.when(first)(start)
        body(*ins, *outs)
        pl.when(last)(wait)

    return wrapped, ex


LANES = 1024


def adamw(gstacks, w, m, v, name):
    D, R, C = w.shape
    S = gstacks[0].shape[0]
    tr = R if R <= 512 else _tile(R, 256)

    def body(*refs):
        g_refs = refs[:D]
        w_ref, m_ref, v_ref, go_ref, d_ref, mo_ref, vo_ref = refs[D:]
        layer = pl.program_id(0)
        g = None
        for d, g_ref in enumerate(g_refs):
            gd = g_ref[0].astype(f32)
            for sl in range(1, S):
                gd = gd + g_ref[sl].astype(f32)
            g = gd if g is None else jnp.where(layer == d, gd, g)
        g = g[None]
        m_new = ADAM_B1 * m_ref[...] + (1.0 - ADAM_B1) * g
        v_new = ADAM_B2 * v_ref[...] + (1.0 - ADAM_B2) * jnp.square(g)
        m_hat = m_new / (1.0 - ADAM_B1 ** ADAM_STEP)
        v_hat = v_new / (1.0 - ADAM_B2 ** ADAM_STEP)
        go_ref[...] = g
        d_ref[...] = -ADAM_LR * (m_hat / (jnp.sqrt(v_hat) + ADAM_EPS) + ADAM_WD * w_ref[...])
        mo_ref[...] = m_new
        vo_ref[...] = v_new

    row = pl.BlockSpec((1, tr, C), lambda l, i: (l, i, 0))
    g_specs = [pl.BlockSpec((S, tr, C), lambda l, i, d=d: (0, jnp.where(l == d, i, 0), 0)) for d in range(D)]
    return pl.pallas_call(
        body, name=name, grid=(D, R // tr), in_specs=g_specs + [row, row, row],
        out_specs=[row] * 4, out_shape=[jax.ShapeDtypeStruct((D, R, C), f32)] * 4, compiler_params=_params(("parallel", "parallel")),
    )(*gstacks, w, m, v)


def sum_slots(gstack, name):
    S, R, _ = gstack.shape

    def body(g_ref, o_ref):
        g = g_ref[0]
        for sl in range(1, S):
            g = g + g_ref[sl]
        o_ref[...] = g

    return pl.pallas_call(body, name=name, out_shape=jax.ShapeDtypeStruct((R, LANES), f32), compiler_params=_params())(gstack)


def _pack(parts, rows, dtype=f32):
    flat = jnp.concatenate([q.reshape(-1).astype(dtype) for q in parts])
    return jnp.pad(flat, (0, rows * LANES - flat.shape[0])).reshape(rows, LANES)


def _unpack(buf, shapes):
    lead = buf.shape[:-2]
    flat = buf.reshape(lead + (-1,))
    out, off = [], 0
    for shp in shapes:
        n = math.prod(shp)
        out.append(flat[..., off:off + n].reshape(lead + tuple(shp)))
        off += n
    return out


_WEIGHTS = ["pre_mix_norm", "w_in", "a_q_norm", "a_w_uq", "a_kv_norm", "a_w_ukv", "a_out_norm", "b_q_norm", "b_k_norm", "b_out_norm",
            "c_conv_w", "c_conv_b", "c_a_log", "c_dt_bias", "c_d_skip", "c_out_norm", "d_conv_w", "d_a_log", "d_dt_bias", "d_out_norm",
            "w_out", "post_mix_norm", "pre_ffn_norm", "f_w_in", "f_conv_w", "f_conv_b", "f_w_out", "post_ffn_norm"]
_BIG = {"w_in": 1, "a_w_uq": 1, "a_w_ukv": 1, "w_out": 0, "f_w_in": 1, "f_w_out": 0}
_CONV = ["c_conv_w", "d_conv_w", "f_conv_w"]
_REP = [n for n in _WEIGHTS if n not in _BIG and n not in _CONV]
SMALLG_ROWS, SMALLW_ROWS = 64, 32


def _join(blocks, axis):
    if axis == 0:
        return blocks.reshape(-1, blocks.shape[2])
    return blocks.transpose(1, 0, 2).reshape(blocks.shape[1], -1)


def _split(full, axis):
    if axis == 0:
        return full.reshape(N_DEV, -1, full.shape[1])
    return full.reshape(full.shape[0], N_DEV, -1).transpose(1, 0, 2)


def kernel(x, pre_mix_norm, w_in, a_q_norm, a_w_uq, a_kv_norm, a_w_ukv, a_out_norm, b_q_norm, b_k_norm, b_out_norm, c_conv_w, c_conv_b, c_a_log, c_dt_bias, c_d_skip, c_out_norm, d_conv_w, d_a_log, d_dt_bias, d_out_norm, w_out, post_mix_norm, pre_ffn_norm, f_w_in, f_conv_w, f_conv_b, f_w_out, post_ffn_norm, loss_target, m_pre_mix_norm, m_w_in, m_a_q_norm, m_a_w_uq, m_a_kv_norm, m_a_w_ukv, m_a_out_norm, m_b_q_norm, m_b_k_norm, m_b_out_norm, m_c_conv_w, m_c_conv_b, m_c_a_log, m_c_dt_bias, m_c_d_skip, m_c_out_norm, m_d_conv_w, m_d_a_log, m_d_dt_bias, m_d_out_norm, m_w_out, m_post_mix_norm, m_pre_ffn_norm, m_f_w_in, m_f_conv_w, m_f_conv_b, m_f_w_out, m_post_ffn_norm, v_pre_mix_norm, v_w_in, v_a_q_norm, v_a_w_uq, v_a_kv_norm, v_a_w_ukv, v_a_out_norm, v_b_q_norm, v_b_k_norm, v_b_out_norm, v_c_conv_w, v_c_conv_b, v_c_a_log, v_c_dt_bias, v_c_d_skip, v_c_out_norm, v_d_conv_w, v_d_a_log, v_d_dt_bias, v_d_out_norm, v_w_out, v_post_mix_norm, v_pre_ffn_norm, v_f_w_in, v_f_conv_w, v_f_conv_b, v_f_w_out, v_post_ffn_norm):
    given = dict(locals())
    w = {n: given[n] for n in _WEIGHTS}
    mom = {n: given["m_" + n] for n in _WEIGHTS}
    var = {n: given["v_" + n] for n in _WEIGHTS}
    me = 4 * lax.axis_index("x") + 2 * lax.axis_index("y") + lax.axis_index("c")
    layered = lambda names: [(l, n) for l in range(DEPTH) for n in names]

    sharded = list(_BIG) + _CONV
    early = [n for n in sharded if n not in _LATE]
    weight_items = lambda l, names: [(w[n][l].astype(MXU_DTYPE if n in _BIG else f32), True) for n in names]
    grad_items = lambda g, names: [(_split(g[n], _BIG[n]).astype(WIRE_DTYPE), False) for n in names]
    P = {n: [w[n][l] for l in range(DEPTH)] for n in _REP}
    P.update({n: [None] * DEPTH for n in sharded})

    def place(l, names, gathered):
        for n, got in zip(names, gathered):
            P[n][l] = _join(got, _BIG.get(n, 1))

    def late_weights_l0(gathered):
        place(0, _LATE, gathered)
        return _late_kernel_weights(P, 0, MXU_DTYPE)

    tabs = _rope_tables(x.shape[1])
    place(0, early, exchange(weight_items(0, early), "gather_weights"))
    x1, s0, got, W0 = _layer_fwd(x[0], _kernel_weights(P, 0, MXU_DTYPE), tabs, "l0_", weight_items(1, sharded),
                                 weight_items(0, _LATE), late_weights_l0)
    place(1, sharded, got)
    x2, s1, _, W1 = _layer_fwd(x1, _kernel_weights(P, 1, MXU_DTYPE), tabs, "l1_")
    loss, dx = loss_and_grad(x2, loss_target[0], "loss")
    loss = lax.psum(loss[0, 0], ("x", "y", "c"))

    def late_grads_l0(g):
        return grad_items({"w_out": g["w_out"], "f_w_in": _ff_deinterleave(g["f_win"]), "f_w_out": g["f_wout"]}, _LATE)

    dx, g1, _, _ = _layer_bwd(s1, dx, W1, tabs, "l1_")
    g1 = _reference_grads(g1)
    dx, g0, recv1, recv0_late = _layer_bwd(s0, dx, W0, tabs, "l0_", grad_items(g1, _BIG), late_grads_l0)
    grads = [_reference_grads(g0), g1]
    small = _REP + _CONV
    full_shapes = [grads[l][n].shape for l, n in layered(small)]
    first = [n for n in _BIG if n not in _LATE]
    *recv0_first, gsmall = exchange(grad_items(grads[0], first) + [(_pack([grads[l][n] for l, n in layered(small)], SMALLG_ROWS), True)],
                                    "exchange_grads")
    recv0 = dict(zip(first + _LATE, recv0_first + recv0_late))

    out = {}
    for n, got1 in zip(_BIG, recv1):
        out[n] = adamw([recv0[n], got1], w[n], mom[n], var[n], "adamw_" + n)
    gsum = dict(zip(layered(small), _unpack(sum_slots(gsmall, "sum_small_grads"), full_shapes)))
    for l, n in layered(_CONV):
        cols = w[n].shape[2]
        gsum[(l, n)] = lax.dynamic_slice_in_dim(gsum[(l, n)], me * cols, cols, axis=1)
    held_shapes = [w[n].shape[1:] for _, n in layered(small)]
    pack_small = lambda d: _pack([d[n][l] for l, n in layered(small)], SMALLW_ROWS)[None]
    res_small = adamw([_pack([gsum[k] for k in layered(small)], SMALLW_ROWS)[None]], pack_small(w), pack_small(mom),
                      pack_small(var), "adamw_small")
    per_kind = [dict(zip(layered(small), _unpack(buf[0], held_shapes))) for buf in res_small]
    for n in small:
        out[n] = [jnp.stack([per_kind[kind][(l, n)] for l in range(DEPTH)]) for kind in range(4)]
    return (loss, dx[None], *[out[n][kind] for kind in range(4) for n in _WEIGHTS])
```

```python
import functools
import math

import jax
import jax.numpy as jnp
from jax import lax
from jax.experimental import pallas as pl
from jax.experimental.pallas import tpu as pltpu

f32 = jnp.float32
MXU_DTYPE = jnp.bfloat16
WIRE_DTYPE = jnp.bfloat16
HI = lax.Precision.HIGHEST

D_MODEL = 1024
DEPTH = 2
GRID_W = 64
ROPE_BASE = 10000.0
EPS = 1e-6
A_Q_LORA, A_KV_LORA, A_ROPE, A_NOPE = 192, 128, 32, 64
SSD_CHUNK = 128
DN_CHUNK = 64
HEAD = 64
HPAD = 128
D_FF = 2816
FF_BLK = 256
N_DEV = 8
ADAM_LR, ADAM_B1, ADAM_B2, ADAM_EPS, ADAM_WD, ADAM_STEP = 0.001, 0.9, 0.999, 1e-08, 0.01, 10

V7X_VMEM_BYTES = 64 * 2 ** 20
VMEM_LIMIT = (V7X_VMEM_BYTES * 3) // 4

PIN = 3072
SEG_A, SEG_B, SEG_CX, SEG_DX, SEG_CZ, SEG_DZ, SEG_CDT, SEG_DAB = 0, 512, 1024, 1536, 2304, 2560, 2816, 2944


def _tile(n, pref):
    for t in (512, 256, 128, 64, 32, 16, 8):
        if t <= pref and n % t == 0:
            return t
    raise ValueError(f"no tile for {n}")


LANE = 128


def _tile_div(n, pref):
    if n <= pref:
        return n
    return max(d for d in range(LANE, pref + 1, LANE) if n % d == 0)


def _params(sem=None):
    return pltpu.CompilerParams(vmem_limit_bytes=VMEM_LIMIT, dimension_semantics=sem)


def _tup(r):
    return tuple(r) if isinstance(r, (tuple, list)) else (r,)


def matmul(a, b, form, out_dtype, name):
    if form == "nn":
        (M, K), N = a.shape, b.shape[1]
    elif form == "nt":
        (M, K), N = a.shape, b.shape[0]
    else:
        (K, M), N = a.shape, b.shape[1]
    if form == "tn":
        tm, tn, tk = _tile_div(M, 1408), _tile_div(N, 1408), _tile_div(K, 1024)
    else:
        tm, tn, tk = _tile_div(M, 512), _tile_div(N, 1536), _tile_div(K, 3072)
    nk = K // tk
    dims = {"nn": ((1,), (0,)), "nt": ((1,), (1,)), "tn": ((0,), (0,))}[form]

    def body(a_ref, b_ref, o_ref, *acc):
        part = lax.dot_general(a_ref[...].astype(MXU_DTYPE), b_ref[...].astype(MXU_DTYPE), (dims, ((), ())),
                               preferred_element_type=f32)
        if nk == 1:
            o_ref[...] = part.astype(out_dtype)
            return
        (acc_ref,) = acc
        k = pl.program_id(2)

        @pl.when(k == 0)
        def _():
            acc_ref[...] = part

        @pl.when((k > 0) & (k < nk - 1))
        def _():
            acc_ref[...] += part

        @pl.when(k == nk - 1)
        def _():
            o_ref[...] = (acc_ref[...] + part).astype(out_dtype)

    a_spec = pl.BlockSpec((tk, tm), lambda i, j, k: (k, i)) if form == "tn" else pl.BlockSpec((tm, tk), lambda i, j, k: (i, k))
    b_spec = pl.BlockSpec((tn, tk), lambda i, j, k: (j, k)) if form == "nt" else pl.BlockSpec((tk, tn), lambda i, j, k: (k, j))
    return pl.pallas_call(
        body, name=name, grid=(M // tm, N // tn, nk), in_specs=[a_spec, b_spec],
        out_specs=pl.BlockSpec((tm, tn), lambda i, j, k: (i, j)), out_shape=jax.ShapeDtypeStruct((M, N), out_dtype),
        scratch_shapes=[pltpu.VMEM((tm, tn), f32)] if nk > 1 else [], compiler_params=_params(("parallel", "parallel", "arbitrary")),
    )(a, b)


def _row_specs(rows, tm):
    return [pl.BlockSpec((tm, w), lambda i, b=b: (i, b)) for (_, w, b) in rows]


def _full_specs(params):
    return [pl.BlockSpec(p.shape, lambda i: (0, 0)) for p in params]


def rowwise(fn, rows, params, outs, name, tm=512):
    L = rows[0][0].shape[0]
    tm = _tile(L, tm)
    n_in = len(rows) + len(params)

    def body(*refs):
        res = _tup(fn(*[r[...].astype(f32) for r in refs[:n_in]]))
        for o_ref, r in zip(refs[n_in:], res, strict=True):
            o_ref[...] = r.astype(o_ref.dtype)

    res = pl.pallas_call(
        body, name=name, grid=(L // tm,), in_specs=_row_specs(rows, tm) + _full_specs(params),
        out_specs=[pl.BlockSpec((tm, w), lambda i: (i, 0)) for (w, _) in outs],
        out_shape=[jax.ShapeDtypeStruct((L, w), dt) for (w, dt) in outs], compiler_params=_params(("parallel",)),
    )(*[r[0] for r in rows], *params)
    return list(res)


def rowwise_vjp(fn, rows, params, cts, name, row_grads, param_grads, tm=512):
    L = rows[0][0].shape[0]
    tm = _tile(L, tm)
    nr, npar, nct = len(rows), len(params), len(cts)

    def body(*refs):
        i = pl.program_id(0)
        rv = [r[...].astype(f32) for r in refs[:nr]]
        pv = [r[...].astype(f32) for r in refs[nr:nr + npar]]
        cv = tuple(r[...].astype(f32) for r in refs[nr + npar:nr + npar + nct])
        out_refs = refs[nr + npar + nct:]

        def g(*diff):
            rr, pp = list(rv), list(pv)
            for k, v in zip(row_grads, diff[:len(row_grads)]):
                rr[k] = v
            for k, v in zip(param_grads, diff[len(row_grads):]):
                pp[k] = v
            return _tup(fn(*rr, *pp))

        _, vjp = jax.vjp(g, *[rv[k] for k in row_grads], *[pv[k] for k in param_grads])
        grads = vjp(cv)
        for o_ref, gval in zip(out_refs[:len(row_grads)], grads[:len(row_grads)]):
            o_ref[...] = gval
        for o_ref, gval in zip(out_refs[len(row_grads):], grads[len(row_grads):]):
            @pl.when(i == 0)
            def _(o_ref=o_ref, gval=gval):
                o_ref[...] = gval

            @pl.when(i > 0)
            def _(o_ref=o_ref, gval=gval):
                o_ref[...] += gval

    out_specs = [pl.BlockSpec((tm, rows[k][1]), lambda i: (i, 0)) for k in row_grads] + \
                [pl.BlockSpec(params[k].shape, lambda i: (0, 0)) for k in param_grads]
    out_shape = [jax.ShapeDtypeStruct((L, rows[k][1]), f32) for k in row_grads] + \
                [jax.ShapeDtypeStruct(params[k].shape, f32) for k in param_grads]
    res = pl.pallas_call(
        body, name=name, grid=(L // tm,), in_specs=_row_specs(rows, tm) + _full_specs(params) + _row_specs(cts, tm),
        out_specs=out_specs, out_shape=out_shape, compiler_params=_params(("arbitrary",)),
    )(*[r[0] for r in rows], *params, *[c[0] for c in cts])
    res = list(res)
    return res[:len(row_grads)], res[len(row_grads):]


def _rms(x, w, n=None):
    n = x.shape[-1] if n is None else n
    return x * lax.rsqrt(jnp.sum(x * x, axis=-1, keepdims=True) * (1.0 / n) + EPS) * w


def _silu(x):
    return x * jax.nn.sigmoid(x)


def _softplus(x):
    return jnp.maximum(x, 0.0) + jnp.log1p(jnp.exp(-jnp.abs(x)))


def _mm(a, b, dims, precision=None):
    if precision is None:
        a, b = a.astype(MXU_DTYPE), b.astype(MXU_DTYPE)
    return lax.dot_general(a, b, (dims, ((), ())), precision=precision, preferred_element_type=f32)


_NN, _NT, _TN = ((1,), (0,)), ((1,), (1,)), ((0,), (0,))


@functools.partial(jax.custom_vjp, nondiff_argnums=(1, 2))
def _roll(x, shift, axis):
    return pltpu.roll(x, shift, axis)


def _roll_fwd(x, shift, axis):
    return pltpu.roll(x, shift, axis), None


def _roll_bwd(shift, axis, _, ct):
    return (pltpu.roll(ct, (ct.shape[axis] - shift) % ct.shape[axis], axis),)


_roll.defvjp(_roll_fwd, _roll_bwd)


def _rope(x, cos, sin, lo, half):
    n = x.shape[-1]
    lane = lax.broadcasted_iota(jnp.int32, x.shape, x.ndim - 1) % HPAD - lo
    first = ((lane >= 0) & (lane < half)) | ((lane >= 2 * half) & (lane < 3 * half))
    rot = jnp.where(first, -_roll(x, n - half, x.ndim - 1), _roll(x, half, x.ndim - 1))
    return x * cos + rot * sin


def _heads(x, n, width=HEAD):
    return [x[:, h * width:(h + 1) * width] for h in range(n)]


def _pad_heads(hs):
    z = jnp.zeros_like(hs[0])
    return jnp.concatenate([t for h in hs for t in (h, z)], axis=-1)


def _unpad_heads(x, n):
    return jnp.concatenate([x[:, h * HPAD:h * HPAD + HEAD] for h in range(n)], axis=-1)


HALO = 8


def _conv_specs(tm, tc, c0, L):
    nh = tm // HALO
    last = L // HALO - 1
    return [pl.BlockSpec((tm, tc), lambda i, j: (i, c0 + j)),
            pl.BlockSpec((HALO, tc), lambda i, j: (jnp.maximum(i * nh - 1, 0), c0 + j)),
            pl.BlockSpec((HALO, tc), lambda i, j: (jnp.minimum((i + 1) * nh, last), c0 + j))]


def _shift3(x):
    n = x.shape[0]
    return _roll(x, 1, 0), _roll(x, n - 1, 0)


def conv_fwd(gfn, x, c0, ncol, tc, w, b, tco, out_dtype, name, tm=512):
    L = x.shape[0]
    tm = _tile(L, tm)
    ni = L // tm

    def body(x_ref, p_ref, n_ref, w_ref, b_ref, o_ref):
        i = pl.program_id(0)
        xv = x_ref[...].astype(f32)
        xp = jnp.where(i == 0, 0.0, p_ref[HALO - 1:HALO, :].astype(f32))
        xn = jnp.where(i == ni - 1, 0.0, n_ref[0:1, :].astype(f32))
        rid = lax.broadcasted_iota(jnp.int32, xv.shape, 0)
        dn, up = _shift3(xv)
        dn = jnp.where(rid == 0, xp, dn)
        up = jnp.where(rid == tm - 1, xn, up)
        c = w_ref[0:1, :] * dn + w_ref[1:2, :] * xv + w_ref[2:3, :] * up + b_ref[...]
        o_ref[...] = gfn(c).astype(o_ref.dtype)

    return pl.pallas_call(
        body, name=name, grid=(ni, ncol),
        in_specs=_conv_specs(tm, tc, c0, L) + [pl.BlockSpec((3, tc), lambda i, j: (0, j)), pl.BlockSpec((1, tc), lambda i, j: (0, j))],
        out_specs=pl.BlockSpec((tm, tco), lambda i, j: (i, j)), out_shape=jax.ShapeDtypeStruct((L, ncol * tco), out_dtype),
        compiler_params=_params(("parallel", "parallel")),
    )(x, x, x, w, b)


def conv_bwd(gfn, x, c0, ncol, tc, w, b, dys, tco, name, tm=512):
    L = x.shape[0]
    tm = _tile(L, tm)
    ni = L // tm
    nd = len(dys)

    def body(*refs):
        x_ref, xp_ref, xn_ref = refs[:3]
        d_refs = refs[3:3 + 3 * nd]
        w_ref, b_ref, dx_ref, dw_ref, db_ref = refs[3 + 3 * nd:]
        i = pl.program_id(1)
        first, lastb = i == 0, i == ni - 1

        def ext(m, p, n):
            return jnp.concatenate([jnp.where(first, 0.0, p[...].astype(f32)), m[...].astype(f32),
                                    jnp.where(lastb, 0.0, n[...].astype(f32))], axis=0)

        xe = ext(x_ref, xp_ref, xn_ref)
        de = ext(*d_refs[:3])
        for q in range(1, nd):
            de = de + ext(*d_refs[3 * q:3 * q + 3])
        x_dn, x_up = _shift3(xe)
        w0, w1, w2 = w_ref[0:1, :], w_ref[1:2, :], w_ref[2:3, :]
        ce = w0 * x_dn + w1 * xe + w2 * x_up + b_ref[...]
        _, vjp = jax.vjp(gfn, ce)
        (dce,) = vjp(de)
        dc_dn, dc_up = _shift3(dce)
        dx_ref[...] = (w0 * dc_up + w1 * dce + w2 * dc_dn)[HALO:HALO + tm, :]
        rid = lax.broadcasted_iota(jnp.int32, dce.shape, 0)
        dci = jnp.where((rid >= HALO) & (rid < HALO + tm), dce, 0.0)
        dw = jnp.concatenate([jnp.sum(dci * x_dn, axis=0, keepdims=True), jnp.sum(dci * xe, axis=0, keepdims=True),
                              jnp.sum(dci * x_up, axis=0, keepdims=True)], axis=0)
        db = jnp.sum(dci, axis=0, keepdims=True)

        @pl.when(first)
        def _():
            dw_ref[...] = dw
            db_ref[...] = db

        @pl.when(i > 0)
        def _():
            dw_ref[...] += dw
            db_ref[...] += db

    res = pl.pallas_call(
        body, name=name, grid=(ncol, ni),
        in_specs=[pl.BlockSpec(s.block_shape, (lambda j, i, f=s.index_map: f(i, j))) for s in _conv_specs(tm, tc, c0, L)]
        + [pl.BlockSpec(s.block_shape, (lambda j, i, f=s.index_map: f(i, j))) for s in _conv_specs(tm, tco, 0, L)] * nd
        + [pl.BlockSpec((3, tc), lambda j, i: (0, j)), pl.BlockSpec((1, tc), lambda j, i: (0, j))],
        out_specs=[pl.BlockSpec((tm, tc), lambda j, i: (i, j)), pl.BlockSpec((3, tc), lambda j, i: (0, j)),
                   pl.BlockSpec((1, tc), lambda j, i: (0, j))],
        out_shape=[jax.ShapeDtypeStruct((L, ncol * tc), f32), jax.ShapeDtypeStruct((3, ncol * tc), f32),
                   jax.ShapeDtypeStruct((1, ncol * tc), f32)],
        compiler_params=_params(("parallel", "arbitrary")),
    )(x, x, x, *[d for d in dys for _ in range(3)], w, b)
    return res


LOG2E = math.log2(math.e)


def flash_fwd(q, k, v, n_q, n_kv, scale, name, side=(), tq=512, tk=8192):
    L = q.shape[0]
    tq, tk = _tile_div(L, tq), _tile_div(L, tk)
    rep = n_q // n_kv
    nkv = L // tk
    c2 = scale * LOG2E

    def body(q_ref, k_ref, v_ref, o_ref, lse_ref):
        qv = q_ref[...]
        ones_lane = lax.broadcasted_iota(jnp.int32, (tk, HPAD), 1) == HPAD - 1

        def step(c, carry):
            m, acc = carry
            off = pl.multiple_of(c * tk, tk)
            s = _mm(qv, k_ref[pl.ds(off, tk), :], _NT)
            m_new = jnp.maximum(m, jnp.max(s, axis=-1, keepdims=True))
            p = jnp.exp2((s - m_new) * c2)
            vv = v_ref[pl.ds(off, tk), :]
            acc = jnp.exp2((m - m_new) * c2) * acc + _mm(p, jnp.where(ones_lane, jnp.ones_like(vv), vv), _NN)
            return m_new, acc

        m, acc = lax.fori_loop(0, nkv, step, (jnp.full((tq, 1), -jnp.inf, f32), jnp.zeros((tq, HPAD), f32)))
        l = acc[:, HPAD - 1:]
        o_ref[...] = jnp.where(lax.broadcasted_iota(jnp.int32, (tq, HPAD), 1) == HPAD - 1, 0.0, acc / l)
        lse_ref[0] = m * scale + jnp.log(l)

    grid = (n_q, L // tq)
    body, ex = _with_exchange(body, 3, 2, side, grid)
    res = pl.pallas_call(
        body, name=name, grid=grid,
        in_specs=[pl.BlockSpec((tq, HPAD), lambda h, i: (i, h)), pl.BlockSpec((L, HPAD), lambda h, i: (0, h // rep)),
                  pl.BlockSpec((L, HPAD), lambda h, i: (0, h // rep))] + ex.specs,
        out_specs=[pl.BlockSpec((tq, HPAD), lambda h, i: (i, h)), pl.BlockSpec((1, tq, 1), lambda h, i: (h, i, 0))] + ex.specs,
        out_shape=[jax.ShapeDtypeStruct((L, n_q * HPAD), f32), jax.ShapeDtypeStruct((n_q, L, 1), f32)] + ex.out_shape,
        scratch_shapes=ex.scratch, compiler_params=_params(("arbitrary", "arbitrary") if side else ("parallel", "parallel")),
    )(q, k, v, *ex.args)
    return res[0], res[1], list(res[2:])


def flash_bwd(q, k, v, o, lse, do, n_q, n_kv, scale, name, side=(), tq=1024, tk=1024):
    L = q.shape[0]
    tq, tk = _tile_div(L, tq), _tile_div(L, tk)
    rep = n_q // n_kv
    c2 = scale * LOG2E

    def body(k_ref, v_ref, q_ref, do_ref, o_ref, lse_ref, dq_ref, dk_ref, dv_ref):
        j, i = pl.program_id(1), pl.program_id(2)
        kv, vv = k_ref[...], v_ref[...]
        rows = pl.ds(pl.multiple_of(i * tq, tq), tq)
        dk, dv = jnp.zeros((tk, HPAD), f32), jnp.zeros((tk, HPAD), f32)
        for r in range(rep):
            cols = slice(r * HPAD, (r + 1) * HPAD)
            qv = q_ref[:, cols]
            dov = do_ref[:, cols]
            delta = jnp.sum(dov * o_ref[:, cols], axis=-1, keepdims=True)
            p = jnp.exp2(_mm(qv, kv, _NT) * c2 - lse_ref[r] * LOG2E)
            dv = dv + _mm(p, dov, _TN)
            ds = p * (_mm(dov, vv, _NT) - delta) * scale
            dk = dk + _mm(ds, qv, _TN)
            dq = _mm(ds, kv, _NN)

            @pl.when(j == 0)
            def _(dq=dq, cols=cols):
                dq_ref[rows, cols] = dq

            @pl.when(j > 0)
            def _(dq=dq, cols=cols):
                dq_ref[rows, cols] += dq

        @pl.when(i == 0)
        def _():
            dk_ref[...] = dk
            dv_ref[...] = dv

        @pl.when(i > 0)
        def _():
            dk_ref[...] += dk
            dv_ref[...] += dv

    w = rep * HPAD
    grid = (n_kv, L // tk, L // tq)
    body, ex = _with_exchange(body, 6, 3, side, grid)
    res = pl.pallas_call(
        body, name=name, grid=grid,
        in_specs=[pl.BlockSpec((tk, HPAD), lambda g, j, i: (j, g)), pl.BlockSpec((tk, HPAD), lambda g, j, i: (j, g)),
                  pl.BlockSpec((tq, w), lambda g, j, i: (i, g)), pl.BlockSpec((tq, w), lambda g, j, i: (i, g)),
                  pl.BlockSpec((tq, w), lambda g, j, i: (i, g)), pl.BlockSpec((rep, tq, 1), lambda g, j, i: (g, i, 0))] + ex.specs,
        out_specs=[pl.BlockSpec((L, w), lambda g, j, i: (0, g)), pl.BlockSpec((tk, HPAD), lambda g, j, i: (j, g)),
                   pl.BlockSpec((tk, HPAD), lambda g, j, i: (j, g))] + ex.specs,
        out_shape=[jax.ShapeDtypeStruct((L, n_q * HPAD), f32), jax.ShapeDtypeStruct((L, n_kv * HPAD), f32),
                   jax.ShapeDtypeStruct((L, n_kv * HPAD), f32)] + ex.out_shape,
        scratch_shapes=ex.scratch, compiler_params=_params(("arbitrary",) * 3 if side else ("parallel", "arbitrary", "arbitrary")),
    )(k, v, q, do, o, lse, *ex.args)
    return res[0], res[1], res[2], list(res[3:])


N_SCAN_HEADS = 4
STATE_ROWS = N_SCAN_HEADS * HEAD


def _tri(n, rev):
    i = lax.broadcasted_iota(jnp.int32, (n, n), 0)
    k = lax.broadcasted_iota(jnp.int32, (n, n), 1)
    return ((k >= i), (k > i)) if rev else ((k <= i), (k < i))


def _decay(acol, incl):
    n = acol.shape[0]
    m1 = jnp.broadcast_to(acol, (n, n))
    return jnp.exp(jnp.where(incl, m1 - m1.T, -jnp.inf))


def _ssd_chunk(S, xbc, win, a_log, dt_bias, rev):
    Q = xbc.shape[0]
    d = 1 if rev else 0
    incl, _ = _tri(Q, rev)
    dt = _softplus(win[:, 4 * d:4 * d + 4] + dt_bias[d:d + 1, :])
    a = dt * (-jnp.exp(a_log[d:d + 1, :]))
    acum = _mm(incl.astype(f32), a, _NN, HI)
    tot = jnp.sum(a, axis=0, keepdims=True)
    xs, Bm, Cm = xbc[:, :256], xbc[:, 256:384], xbc[:, 384:512]
    H, W = N_SCAN_HEADS, N_SCAN_HEADS * HEAD
    lanes = lambda x: _cat([jnp.broadcast_to(x[:, h:h + 1], (x.shape[0], HEAD)) for h in range(H)])
    xdt = xs * lanes(dt)
    cb = [_mm(Cm[:, g * HEAD:(g + 1) * HEAD], Bm[:, g * HEAD:(g + 1) * HEAD], _NT) for g in range(2)]
    scores = _cat([cb[h // 2] * _decay(acum[:, h:h + 1], incl) for h in range(H)])
    own = (lax.broadcasted_iota(jnp.int32, (H * Q, W), 0) // Q) == (lax.broadcasted_iota(jnp.int32, (H * Q, W), 1) // HEAD)
    y = _mm(scores, jnp.where(own, jnp.concatenate([xdt] * H, axis=0), 0.0), _NN)
    grp = (lax.broadcasted_iota(jnp.int32, (W, 2 * HEAD), 0) // (2 * HEAD)) == (lax.broadcasted_iota(jnp.int32, (W, 2 * HEAD), 1) // HEAD)
    y = y + _mm(Cm, jnp.where(grp, _cat([S, S]), 0.0), _NT) * jnp.exp(lanes(acum))
    st = _mm(xdt * jnp.exp(lanes(tot - acum)), Bm, _TN)
    first = lax.broadcasted_iota(jnp.int32, (W, HEAD), 0) < 2 * HEAD
    exp_tot = jnp.concatenate([jnp.broadcast_to(jnp.exp(tot[:, h:h + 1]), (HEAD, 1)) for h in range(H)], axis=0)
    return S * exp_tot + jnp.where(first, st[:, :HEAD], st[:, HEAD:]), y


@functools.partial(jax.custom_vjp, nondiff_argnums=(1,))
def _inv_unit_tri(Lm, order):
    n = Lm.shape[0]
    eye = (lax.broadcasted_iota(jnp.int32, (n, n), 0) == lax.broadcasted_iota(jnp.int32, (n, n), 1)).astype(f32)
    P = -Lm
    T = eye + P
    k = 1
    while 2 * k < order:
        P = _mm(P, P, _NN, lax.Precision.HIGH)
        T = T + _mm(T, P, _NN, lax.Precision.HIGH)
        k *= 2
    return T


def _inv_unit_tri_fwd(Lm, order):
    T = _inv_unit_tri(Lm, order)
    return T, T


def _inv_unit_tri_bwd(order, T, dT):
    return (-_mm(_mm(T, dT, _TN, lax.Precision.HIGH), T, _NT, lax.Precision.HIGH),)


_inv_unit_tri.defvjp(_inv_unit_tri_fwd, _inv_unit_tri_bwd)


@jax.custom_vjp
def _inv_known(Lm, T):
    return T


def _inv_known_fwd(Lm, T):
    return T, T


def _inv_known_bwd(T, dT):
    return _inv_unit_tri_bwd(None, T, dT) + (jnp.zeros_like(T),)


_inv_known.defvjp(_inv_known_fwd, _inv_known_bwd)


def _dn_chunk(S, qkv, win, a_log, dt_bias, rev, kept=None):
    Q = qkv.shape[0]
    H, R = N_SCAN_HEADS, N_SCAN_HEADS * qkv.shape[0]
    d = 1 if rev else 0
    beta = jax.nn.sigmoid(win[:, 4 * d:4 * d + 4])
    gl = -jnp.exp(a_log[d:d + 1, :]) * _softplus(win[:, 8 + 4 * d:12 + 4 * d] + dt_bias[d:d + 1, :])
    G = _mm(_tri(Q, rev)[0].astype(f32), gl, _NN, HI)
    tot = jnp.sum(gl, axis=0, keepdims=True)
    rows = lambda x: jnp.concatenate(_heads(x, H), axis=0)
    col = lambda x: jnp.concatenate([x[:, h:h + 1] for h in range(H)], axis=0)
    per_head = lambda x, n: jnp.concatenate([jnp.broadcast_to(x[:, h:h + 1], (n, 1)) for h in range(H)], axis=0)
    r = lax.broadcasted_iota(jnp.int32, (R, R), 0)
    c = lax.broadcasted_iota(jnp.int32, (R, R), 1)
    same = (r // Q) == (c // Q)
    incl = same & ((c >= r) if rev else (c <= r))
    strict = same & ((c > r) if rev else (c < r))
    own = (lax.broadcasted_iota(jnp.int32, (R, H * HEAD), 0) // Q) == (lax.broadcasted_iota(jnp.int32, (R, H * HEAD), 1) // HEAD)
    blk = lambda x: jnp.where(own, _cat([x] * H), 0.0)
    q, k, v = rows(qkv[:, :256]) * (HEAD ** -0.5), rows(qkv[:, 256:512]), rows(qkv[:, 512:])
    b, Gs, tots = col(beta), col(G), per_head(tot, Q)
    dec = _decay(Gs, incl)
    kb = k * b
    Lm = jnp.where(strict, _mm(kb, k, _NT) * dec, 0.0)
    T = _inv_unit_tri(Lm, Q) if kept is None else _inv_known(Lm, blk(kept))
    eG = jnp.exp(Gs)
    uw = _mm(T, _cat([v * b, kb * eG]), _NN)
    vnew = uw[:, :HEAD] - _mm(blk(uw[:, HEAD:]), S, _NN)
    o = _mm(blk(q * eG), S, _NN) + _mm(_mm(q, k, _NT) * dec, vnew, _NN)
    S_next = S * jnp.exp(per_head(tot, HEAD)) + _mm(blk(k * jnp.exp(tots - Gs)), vnew, _TN)
    out = S_next, _cat([o[h * Q:(h + 1) * Q, :] for h in range(H)])
    if kept is None:
        out += (jnp.concatenate([T[h * Q:(h + 1) * Q, h * Q:(h + 1) * Q] for h in range(H)], axis=0),)
    return out


def scan_fwd(chunk_fn, Q, x, p, win_blk, a_log, dt_bias, name, keeps=False):
    L, W = x.shape
    nc = L // Q
    fidx, ridx = (lambda t: t), (lambda t: nc - 1 - t)

    def body(xf_ref, xr_ref, wf_ref, wr_ref, al_ref, db_ref, yf_ref, yr_ref, sf_ref, sr_ref, *rest):
        (sf_scr, sr_scr), keep_refs = rest[-2:], rest[:-2]

        @pl.when(pl.program_id(0) == 0)
        def _():
            sf_scr[...] = jnp.zeros_like(sf_scr)
            sr_scr[...] = jnp.zeros_like(sr_scr)

        for d, (x_ref, w_ref, y_ref, sin_ref, s_scr) in enumerate(((xf_ref, wf_ref, yf_ref, sf_ref, sf_scr),
                                                                   (xr_ref, wr_ref, yr_ref, sr_ref, sr_scr))):
            S = s_scr[...]
            sin_ref[...] = S
            res = chunk_fn(S, x_ref[...], w_ref[...], al_ref[...], db_ref[...], d == 1)
            s_scr[...], y_ref[...] = res[:2]
            if keeps:
                keep_refs[d][...] = res[2]

    row = lambda w, idx, b=0: pl.BlockSpec((Q, w), lambda t: (idx(t), b))
    small = pl.BlockSpec(a_log.shape, lambda t: (0, 0))
    state = lambda idx: pl.BlockSpec((STATE_ROWS, HEAD), lambda t: (idx(t), 0))
    n_state = 4 if keeps else 2
    return pl.pallas_call(
        body, name=name, grid=(nc,),
        in_specs=[row(W, fidx), row(W, ridx), row(HPAD, fidx, win_blk), row(HPAD, ridx, win_blk), small, small],
        out_specs=[row(STATE_ROWS, fidx), row(STATE_ROWS, ridx)] + [state(fidx), state(ridx)] * (n_state // 2),
        out_shape=[jax.ShapeDtypeStruct((L, STATE_ROWS), f32)] * 2 + [jax.ShapeDtypeStruct((nc * STATE_ROWS, HEAD), f32)] * n_state,
        scratch_shapes=[pltpu.VMEM((STATE_ROWS, HEAD), f32)] * 2, compiler_params=_params(("arbitrary",)),
    )(x, x, p, p, a_log, dt_bias)


def scan_bwd(chunk_fn, Q, x, p, win_blk, a_log, dt_bias, s_fwd, s_rev, dy, name, kept=()):
    L, W = x.shape
    nc = L // Q
    fidx, ridx = (lambda t: nc - 1 - t), (lambda t: t)
    nk = len(kept)

    def body(xf_ref, xr_ref, wf_ref, wr_ref, al_ref, db_ref, sf_ref, sr_ref, dyf_ref, dyr_ref, *rest):
        keep_refs = rest[:nk]
        dxf_ref, dxr_ref, dwf_ref, dwr_ref, dal_ref, ddb_ref, dsf_scr, dsr_scr = rest[nk:]

        @pl.when(pl.program_id(0) == 0)
        def _():
            dsf_scr[...] = jnp.zeros_like(dsf_scr)
            dsr_scr[...] = jnp.zeros_like(dsr_scr)
            dal_ref[...] = jnp.zeros_like(dal_ref)
            ddb_ref[...] = jnp.zeros_like(ddb_ref)

        dal_sum, ddb_sum = dal_ref[...], ddb_ref[...]
        for rev, x_ref, w_ref, sin_ref, dy_ref, dx_ref, dw_ref, ds_scr in (
                (False, xf_ref, wf_ref, sf_ref, dyf_ref, dxf_ref, dwf_ref, dsf_scr),
                (True, xr_ref, wr_ref, sr_ref, dyr_ref, dxr_ref, dwr_ref, dsr_scr)):
            extra = (keep_refs[int(rev)][...],) if nk else ()
            _, vjp = jax.vjp(lambda S, xv, wv, al, db, rev=rev, extra=extra: chunk_fn(S, xv, wv, al, db, rev, *extra),
                             sin_ref[...], x_ref[...], w_ref[...], al_ref[...], db_ref[...])
            ds_scr[...], dx_ref[...], dw_ref[...], dal, ddb = vjp((ds_scr[...], dy_ref[...]))
            dal_sum, ddb_sum = dal_sum + dal, ddb_sum + ddb
        dal_ref[...] = dal_sum
        ddb_ref[...] = ddb_sum

    row = lambda w, idx, b=0: pl.BlockSpec((Q, w), lambda t: (idx(t), b))
    small = pl.BlockSpec(a_log.shape, lambda t: (0, 0))
    state = lambda idx: pl.BlockSpec((STATE_ROWS, HEAD), lambda t: (idx(t), 0))
    return pl.pallas_call(
        body, name=name, grid=(nc,),
        in_specs=[row(W, fidx), row(W, ridx), row(HPAD, fidx, win_blk), row(HPAD, ridx, win_blk), small, small,
                  state(fidx), state(ridx), row(STATE_ROWS, fidx), row(STATE_ROWS, ridx)] + [state(fidx), state(ridx)][:nk],
        out_specs=[row(W, fidx), row(W, ridx), row(HPAD, fidx), row(HPAD, ridx), small, small],
        out_shape=[jax.ShapeDtypeStruct((L, W), f32)] * 2 + [jax.ShapeDtypeStruct((L, HPAD), f32)] * 2
        + [jax.ShapeDtypeStruct(a_log.shape, f32)] * 2,
        scratch_shapes=[pltpu.VMEM((STATE_ROWS, HEAD), f32)] * 2, compiler_params=_params(("arbitrary",)),
    )(x, x, p, p, a_log, dt_bias, s_fwd, s_rev, dy, dy, *kept)


def _cat(xs):
    return jnp.concatenate(xs, axis=-1)


def _a_prep(pa, cos, sin, qn, wuq, kvn, wk, wv):
    cq, ckv, kr = pa[:, :256], pa[:, 256:384], pa[:, 384:512]
    q = _rope(_mm(_rms(cq, qn, A_Q_LORA), wuq, _NN), _cat([cos] * 4), _cat([sin] * 4), A_NOPE, A_ROPE // 4)
    kvh = _rms(ckv, kvn)
    k = _mm(kvh, wk, _NN) + _cat([_rope(kr, cos, sin, A_NOPE, A_ROPE // 4)] * 4)
    return q, k, _mm(kvh, wv, _NN)


def _b_prep(pb, cos, sin, qn, kn):
    q = _pad_heads([_rms(h, qn) for h in _heads(pb[:, :256], 4)])
    k = _pad_heads([_rms(h, kn) for h in _heads(pb[:, 256:384], 2)])
    v = _pad_heads(_heads(pb[:, 384:512], 2))
    return _rope(q, _cat([cos] * 4), _cat([sin] * 4), 0, HEAD // 4), _rope(k, _cat([cos] * 2), _cat([sin] * 2), 0, HEAD // 4), v


def _c_act(c):
    return _silu(c)


def _d_act(c):
    s = _silu(c)
    return _cat([h * lax.rsqrt(jnp.sum(h * h, axis=-1, keepdims=True) + 1e-6) for h in _heads(s[:, :512], 8)] + [s[:, 512:]])


def _ffn_act(c):
    return _silu(c[:, :FF_BLK]) * c[:, FF_BLK:]


def _mix_post(oa, ob, yc_f, yc_r, xbc, cz, od_f, od_r, dz, a_out, b_out, dskip, c_out, d_out):
    o_a = _rms(_unpad_heads(oa, 4), a_out)
    o_b = _rms(_unpad_heads(ob, 4), b_out)
    skip = _cat([jnp.broadcast_to(dskip[:, h:h + 1], (1, HEAD)) for h in range(4)])
    o_c = _rms((yc_f + yc_r + xbc[:, :256] * skip) * _silu(cz), c_out)
    o_d = _cat([_rms(h, d_out) for h in _heads(od_f + od_r, 4)]) * _silu(dz)
    return _cat([o_a, o_b, o_c, o_d])


def _post_mix(x, y1, g_pm, g_pf):
    x1 = x + _rms(y1, g_pm)
    return x1, _rms(x1, g_pf)


def _post_ffn(x1, y2, g):
    return x1 + _rms(y2, g)


def _pre_bwd(x, g):
    return x, _rms(x, g)


def _layer_fwd(x, W, tabs, n, side=(), late_side=(), late_weights=None):
    mx = MXU_DTYPE
    cos_a, sin_a, cos_b, sin_b = tabs
    full = lambda a: (a, a.shape[1], 0)
    s = {"x": x}
    (s["h"],) = rowwise(_rms, [full(x)], [W["g_pre"]], [(D_MODEL, mx)], n + "pre")
    p = s["p"] = matmul(s["h"], W["w_in"], "nn", f32, n + "in")
    s["qa"], s["ka"], s["va"] = rowwise(_a_prep, [(p, 512, SEG_A // 512), full(cos_a), full(sin_a)],
                                        [W["a_qn"], W["a_wuq"], W["a_kvn"], W["a_wk"], W["a_wv"]], [(512, mx)] * 3, n + "a_prep")
    s["qb"], s["kb"], s["vb"] = rowwise(_b_prep, [(p, 512, SEG_B // 512), full(cos_b), full(sin_b)], [W["b_qn"], W["b_kn"]],
                                        [(512, mx), (256, mx), (256, mx)], n + "b_prep")
    s["oa"], s["lsea"], side_out = flash_fwd(s["qa"], s["ka"], s["va"], 4, 4, (A_NOPE + A_ROPE) ** -0.5, n + "a_attn", side)
    s["ob"], s["lseb"], late_out = flash_fwd(s["qb"], s["kb"], s["vb"], 4, 2, HEAD ** -0.5, n + "b_attn", late_side)
    if late_weights is not None:
        W = {**W, **late_weights(late_out)}
    xbc = s["xbc"] = conv_fwd(_c_act, p, SEG_CX // 512, 1, 512, W["c_cw"], W["c_cb"], 512, f32, n + "c_conv")
    s["yc_f"], s["yc_r"], s["sc_f"], s["sc_r"] = scan_fwd(_ssd_chunk, SSD_CHUNK, xbc, p, SEG_CDT // HPAD, W["c_alog"], W["c_dtb"], n + "c_ssd")
    qkv = s["qkv"] = conv_fwd(_d_act, p, SEG_DX // 768, 1, 768, W["d_cw"], W["d_cb"], 768, f32, n + "d_conv")
    s["od_f"], s["od_r"], s["sd_f"], s["sd_r"], s["td_f"], s["td_r"] = scan_fwd(
        _dn_chunk, DN_CHUNK, qkv, p, SEG_DAB // HPAD, W["d_alog"], W["d_dtb"], n + "d_dn", keeps=True)
    (s["omix"],) = rowwise(_mix_post, _mix_rows(s), _mix_params(W), [(D_MODEL, mx)], n + "mix_post")
    s["y1"] = matmul(s["omix"], W["w_out"], "nn", f32, n + "out")
    s["x1"], s["h2"] = rowwise(_post_mix, [full(x), full(s["y1"])], [W["g_pm"], W["g_pf"]], [(D_MODEL, f32), (D_MODEL, mx)], n + "post_mix")
    s["u"] = matmul(s["h2"], W["f_win"], "nn", f32, n + "f_in")
    s["a"] = conv_fwd(_ffn_act, s["u"], 0, D_FF // FF_BLK, 2 * FF_BLK, W["f_cw"], W["f_cb"], FF_BLK, mx, n + "f_conv", tm=512)
    s["y2"] = matmul(s["a"], W["f_wout"], "nn", f32, n + "f_out")
    (x2,) = rowwise(_post_ffn, [full(s["x1"]), full(s["y2"])], [W["g_po"]], [(D_MODEL, f32)], n + "post_ffn")
    return x2, s, side_out, W


def _mix_rows(s):
    p = s["p"]
    return [(s["oa"], 512, 0), (s["ob"], 512, 0), (s["yc_f"], 256, 0), (s["yc_r"], 256, 0), (s["xbc"], 512, 0),
            (p, 256, SEG_CZ // 256), (s["od_f"], 256, 0), (s["od_r"], 256, 0), (p, 256, SEG_DZ // 256)]


def _mix_params(W):
    return [W["a_out"], W["b_out"], W["c_dskip"], W["c_out"], W["d_out"]]


def _layer_bwd(s, dx2, W, tabs, n, side=(), late_side=None):
    cos_a, sin_a, cos_b, sin_b = tabs
    full = lambda a: (a, a.shape[1], 0)
    p, g = s["p"], {}
    (dx1, dy2), (g["g_po"],) = rowwise_vjp(_post_ffn, [full(s["x1"]), full(s["y2"])], [W["g_po"]], [full(dx2)], n + "post_ffn_b", [0, 1], [0])
    da = matmul(dy2, W["f_wout"], "nt", f32, n + "f_out_dx")
    g["f_wout"] = matmul(s["a"], dy2, "tn", f32, n + "f_out_dw")
    du, g["f_cw"], g["f_cb"] = conv_bwd(_ffn_act, s["u"], 0, D_FF // FF_BLK, 2 * FF_BLK, W["f_cw"], W["f_cb"], [da], FF_BLK, n + "f_conv_b", tm=512)
    dh2 = matmul(du, W["f_win"], "nt", f32, n + "f_in_dx")
    g["f_win"] = matmul(s["h2"], du, "tn", f32, n + "f_in_dw")
    (dx, dy1), (g["g_pm"], g["g_pf"]) = rowwise_vjp(_post_mix, [full(s["x"]), full(s["y1"])], [W["g_pm"], W["g_pf"]],
                                                   [full(dx1), full(dh2)], n + "post_mix_b", [0, 1], [0, 1])
    domix = matmul(dy1, W["w_out"], "nt", f32, n + "out_dx")
    g["w_out"] = matmul(s["omix"], dy1, "tn", f32, n + "out_dw")
    (doa, dob, dyc, dxbc_skip, dcz, dod, ddz), (g["a_out"], g["b_out"], g["c_dskip"], g["c_out"], g["d_out"]) = rowwise_vjp(
        _mix_post, _mix_rows(s), _mix_params(W), [full(domix)], n + "mix_post_b", [0, 1, 2, 4, 5, 6, 8], list(range(5)))
    dqkv_f, dqkv_r, ddab_f, ddab_r, g["d_alog"], g["d_dtb"] = scan_bwd(
        _dn_chunk, DN_CHUNK, s["qkv"], p, SEG_DAB // HPAD, W["d_alog"], W["d_dtb"], s["sd_f"], s["sd_r"], dod, n + "d_dn_b",
        kept=(s["td_f"], s["td_r"]))
    ddx, g["d_cw"], _ = conv_bwd(_d_act, p, SEG_DX // 768, 1, 768, W["d_cw"], W["d_cb"], [dqkv_f, dqkv_r], 768, n + "d_conv_b")
    dxbc_f, dxbc_r, dcdt_f, dcdt_r, g["c_alog"], g["c_dtb"] = scan_bwd(
        _ssd_chunk, SSD_CHUNK, s["xbc"], p, SEG_CDT // HPAD, W["c_alog"], W["c_dtb"], s["sc_f"], s["sc_r"], dyc, n + "c_ssd_b")
    dcx, g["c_cw"], g["c_cb"] = conv_bwd(_c_act, p, SEG_CX // 512, 1, 512, W["c_cw"], W["c_cb"], [dxbc_f, dxbc_r, dxbc_skip], 512,
                                         n + "c_conv_b")
    (dsmall,) = rowwise(lambda a, b, c, d: _cat([a + b, c + d]), [full(dcdt_f), full(dcdt_r), full(ddab_f), full(ddab_r)], [],
                        [(2 * HPAD, f32)], n + "dwin_sum")
    dqa, dka, dva, side_out = flash_bwd(s["qa"], s["ka"], s["va"], s["oa"], s["lsea"], doa, 4, 4, (A_NOPE + A_ROPE) ** -0.5,
                                        n + "a_attn_b", side)
    dqb, dkb, dvb, late_out = flash_bwd(s["qb"], s["kb"], s["vb"], s["ob"], s["lseb"], dob, 4, 2, HEAD ** -0.5, n + "b_attn_b",
                                        late_side(g) if late_side is not None else ())
    (dpa,), (g["a_qn"], g["a_wuq"], g["a_kvn"], g["a_wk"], g["a_wv"]) = rowwise_vjp(
        _a_prep, [(p, 512, SEG_A // 512), full(cos_a), full(sin_a)], [W["a_qn"], W["a_wuq"], W["a_kvn"], W["a_wk"], W["a_wv"]],
        [full(dqa), full(dka), full(dva)], n + "a_prep_b", [0], list(range(5)))
    (dpb,), (g["b_qn"], g["b_kn"]) = rowwise_vjp(_b_prep, [(p, 512, SEG_B // 512), full(cos_b), full(sin_b)], [W["b_qn"], W["b_kn"]],
                                               [full(dqb), full(dkb), full(dvb)], n + "b_prep_b", [0], [0, 1])
    dp = jnp.concatenate([dpa, dpb, dcx, ddx, dcz, ddz, dsmall], axis=1)
    dh = matmul(dp, W["w_in"], "nt", f32, n + "in_dx")
    g["w_in"] = matmul(s["h"], dp, "tn", f32, n + "in_dw")
    (dx0,), (g["g_pre"],) = rowwise_vjp(_pre_bwd, [full(s["x"])], [W["g_pre"]], [full(dx), full(dh)], n + "pre_b", [0], [0])
    return dx0, g, side_out, late_out


def loss_and_grad(y, target, name):
    L, D = y.shape
    tm = _tile(L, 512)

    def body(y_ref, t_ref, loss_ref, dy_ref):
        e = y_ref[...] - t_ref[...]
        dy_ref[...] = e * (1.0 / D)
        part = 0.5 * jnp.sum(jnp.sum(e * e, axis=1, keepdims=True) * (1.0 / D), axis=0, keepdims=True)

        @pl.when(pl.program_id(0) == 0)
        def _():
            loss_ref[...] = part

        @pl.when(pl.program_id(0) > 0)
        def _():
            loss_ref[...] += part

    row = pl.BlockSpec((tm, D), lambda i: (i, 0))
    return pl.pallas_call(
        body, name=name, grid=(L // tm,), in_specs=[row, row], out_specs=[pl.BlockSpec((1, 1), lambda i: (0, 0)), row],
        out_shape=[jax.ShapeDtypeStruct((1, 1), f32), jax.ShapeDtypeStruct((L, D), f32)], compiler_params=_params(("arbitrary",)),
    )(y, target)


_IN_SEGS = [(0, 192), (256, 128), (384 + A_NOPE, 32), (512, 256), (768, 128), (896, 128), (SEG_CZ, 256), (SEG_CX, 512),
            (SEG_CDT, 8), (SEG_DX, 768), (SEG_DZ, 256), (SEG_DAB, 16)]


def _pad_in(w):
    src, pieces = 0, {}
    for off, wd in _IN_SEGS:
        pieces[off] = w[..., src:src + wd]
        src += wd
    out, pos = [], 0
    for off in sorted(pieces):
        if off > pos:
            out.append(jnp.zeros(w.shape[:-1] + (off - pos,), w.dtype))
        out.append(pieces[off])
        pos = off + pieces[off].shape[-1]
    out.append(jnp.zeros(w.shape[:-1] + (PIN - pos,), w.dtype))
    return jnp.concatenate(out, axis=-1)


def _unpad_in(wp):
    return jnp.concatenate([wp[..., off:off + wd] for off, wd in _IN_SEGS], axis=-1)


def _pad_last(a, n):
    return jnp.pad(a, [(0, 0)] * (a.ndim - 1) + [(0, n - a.shape[-1])])


def _ff_interleave(w):
    blocks = range(D_FF // FF_BLK)
    return jnp.concatenate([w[..., off + j * FF_BLK:off + (j + 1) * FF_BLK] for j in blocks for off in (0, D_FF)], axis=-1)


def _ff_deinterleave(w):
    blocks = range(D_FF // FF_BLK)
    return jnp.concatenate([w[..., (2 * j + half) * FF_BLK:(2 * j + half + 1) * FF_BLK] for half in (0, 1) for j in blocks], axis=-1)


def _row(v):
    return v.reshape(1, -1)


_LATE = ["w_out", "f_w_in", "f_w_out"]


def _late_kernel_weights(P, l, wdt):
    return {"w_out": P["w_out"][l].astype(wdt), "f_win": _ff_interleave(P["f_w_in"][l]).astype(wdt), "f_wout": P["f_w_out"][l].astype(wdt)}


def _kernel_weights(P, l, wdt):
    late = {} if P["w_out"][l] is None else _late_kernel_weights(P, l, wdt)
    uq = P["a_w_uq"][l].reshape(A_Q_LORA, 4, A_NOPE + A_ROPE)
    ukv = P["a_w_ukv"][l].reshape(A_KV_LORA, 4, 2 * HEAD)
    z = jnp.zeros((A_KV_LORA, 4, HEAD), ukv.dtype)
    return {
        "g_pre": _row(P["pre_mix_norm"][l]), "w_in": _pad_in(P["w_in"][l]).astype(wdt),
        "a_qn": _pad_last(_row(P["a_q_norm"][l]), 256),
        "a_wuq": jnp.pad(_pad_last(uq, HPAD).reshape(A_Q_LORA, 4 * HPAD), ((0, 256 - A_Q_LORA), (0, 0))).astype(wdt),
        "a_kvn": _row(P["a_kv_norm"][l]),
        "a_wk": jnp.concatenate([ukv[..., :HEAD], z], axis=-1).reshape(A_KV_LORA, 4 * HPAD).astype(wdt),
        "a_wv": jnp.concatenate([ukv[..., HEAD:], z], axis=-1).reshape(A_KV_LORA, 4 * HPAD).astype(wdt),
        "a_out": _row(P["a_out_norm"][l]), "b_qn": _row(P["b_q_norm"][l]), "b_kn": _row(P["b_k_norm"][l]), "b_out": _row(P["b_out_norm"][l]),
        "c_cw": P["c_conv_w"][l], "c_cb": _row(P["c_conv_b"][l]), "c_alog": P["c_a_log"][l], "c_dtb": P["c_dt_bias"][l],
        "c_dskip": _row(P["c_d_skip"][l]), "c_out": _row(P["c_out_norm"][l]),
        "d_cw": P["d_conv_w"][l], "d_cb": jnp.zeros((1, 768), f32), "d_alog": P["d_a_log"][l], "d_dtb": P["d_dt_bias"][l],
        "d_out": _row(P["d_out_norm"][l]), "g_pm": _row(P["post_mix_norm"][l]), "g_pf": _row(P["pre_ffn_norm"][l]),
        "f_cw": _ff_interleave(P["f_conv_w"][l]), "f_cb": _row(_ff_interleave(P["f_conv_b"][l])),
        "g_po": _row(P["post_ffn_norm"][l]), **late,
    }


def _reference_grads(g):
    uq = g["a_wuq"][:A_Q_LORA].reshape(A_Q_LORA, 4, HPAD)[..., :A_NOPE + A_ROPE].reshape(A_Q_LORA, 4 * (A_NOPE + A_ROPE))
    wk = g["a_wk"].reshape(A_KV_LORA, 4, HPAD)[..., :HEAD]
    wv = g["a_wv"].reshape(A_KV_LORA, 4, HPAD)[..., :HEAD]
    return {
        "pre_mix_norm": g["g_pre"][0], "w_in": _unpad_in(g["w_in"]), "a_q_norm": g["a_qn"][0, :A_Q_LORA], "a_w_uq": uq,
        "a_kv_norm": g["a_kvn"][0], "a_w_ukv": jnp.concatenate([wk, wv], axis=-1).reshape(A_KV_LORA, 8 * HEAD),
        "a_out_norm": g["a_out"][0], "b_q_norm": g["b_qn"][0], "b_k_norm": g["b_kn"][0], "b_out_norm": g["b_out"][0],
        "c_conv_w": g["c_cw"], "c_conv_b": g["c_cb"][0], "c_a_log": g["c_alog"], "c_dt_bias": g["c_dtb"], "c_d_skip": g["c_dskip"][0],
        "c_out_norm": g["c_out"][0], "d_conv_w": g["d_cw"], "d_a_log": g["d_alog"], "d_dt_bias": g["d_dtb"], "d_out_norm": g["d_out"][0],
        "w_out": g["w_out"], "post_mix_norm": g["g_pm"][0], "pre_ffn_norm": g["g_pf"][0], "f_w_in": _ff_deinterleave(g["f_win"]),
        "f_conv_w": _ff_deinterleave(g["f_cw"]), "f_conv_b": _ff_deinterleave(g["f_cb"][0]), "f_w_out": g["f_wout"],
        "post_ffn_norm": g["g_po"][0],
    }


def _rope_tables(L):
    def tables(rot):
        rows = L // GRID_W
        row = jnp.repeat(jnp.arange(rows), GRID_W).astype(f32)
        col = jnp.tile(jnp.arange(GRID_W), rows).astype(f32)
        sec = rot // 2
        inv = ROPE_BASE ** (-jnp.arange(0, sec, 2, dtype=f32) / sec)
        ang = jnp.concatenate([row[:, None] * inv] * 2 + [col[:, None] * inv] * 2, axis=-1)
        return jnp.cos(ang), jnp.sin(ang)

    ca, sa = tables(A_ROPE)
    cb, sb = tables(HEAD)
    one, zero = jnp.ones, jnp.zeros
    return (jnp.concatenate([one((L, A_NOPE), f32), ca, one((L, HPAD - A_NOPE - A_ROPE), f32)], axis=1),
            jnp.concatenate([zero((L, A_NOPE), f32), sa, zero((L, HPAD - A_NOPE - A_ROPE), f32)], axis=1),
            jnp.concatenate([cb, one((L, HPAD - HEAD), f32)], axis=1), jnp.concatenate([sb, zero((L, HPAD - HEAD), f32)], axis=1))


def exchange(items, name):
    n = len(items)

    def body(*refs):
        start, wait = _exchange_ops([g for _, g in items], refs[:n], refs[n:2 * n], *refs[2 * n:])
        start()
        wait()

    ex = _exchange_io(items)
    res = pl.pallas_call(body, name=name, in_specs=ex.specs, out_specs=ex.specs, out_shape=ex.out_shape,
                         scratch_shapes=ex.scratch)(*ex.args)
    return list(res)


def _exchange_ops(modes, src_refs, out_refs, send_sems, recv_sems, local_sems):
    n = len(modes)
    x, y, c = lax.axis_index("x"), lax.axis_index("y"), lax.axis_index("c")
    me = 4 * x + 2 * y + c

    def copy(a, k, arriving):
        px, py, pc = (1 - x if k & 4 else x), (1 - y if k & 2 else y), (1 - c if k & 1 else c)
        pid = 4 * px + 2 * py + pc
        sem = a * (N_DEV - 1) + k - 1
        return pltpu.make_async_remote_copy(
            src_ref=src_refs[a] if modes[a] else src_refs[a].at[pid], dst_ref=out_refs[a].at[pid if arriving else me],
            send_sem=send_sems.at[sem], recv_sem=recv_sems.at[sem], device_id=(px, py, pc), device_id_type=pl.DeviceIdType.MESH)

    def local(a):
        return pltpu.make_async_copy(src_refs[a] if modes[a] else src_refs[a].at[me], out_refs[a].at[me], local_sems.at[a])

    pairs = [(a, k) for k in range(1, N_DEV) for a in range(n)]

    def start():
        for a, k in pairs:
            copy(a, k, False).start()
        for a in range(n):
            local(a).start()

    def wait():
        for a, k in pairs:
            copy(a, k, False).wait_send()
        for a, k in pairs:
            copy(a, k, True).wait_recv()
        for a in range(n):
            local(a).wait()

    return start, wait


class _ExchangeIO:
    def __init__(self, items):
        n = len(items)
        self.args = [s for s, _ in items]
        self.specs = [pl.BlockSpec(memory_space=pl.ANY)] * n
        self.out_shape = [jax.ShapeDtypeStruct((N_DEV,) + tuple(s.shape if g else s.shape[1:]), s.dtype) for s, g in items]
        self.scratch = [pltpu.SemaphoreType.DMA((n * (N_DEV - 1),)), pltpu.SemaphoreType.DMA((n * (N_DEV - 1),)),
                        pltpu.SemaphoreType.DMA((n,))] if n else []


def _exchange_io(items):
    return _ExchangeIO(list(items))


def _with_exchange(body, n_in, n_out, side, grid):
    ex = _exchange_io(side)
    n = len(ex.args)
    if not n:
        return body, ex
    modes = [g for _, g in side]

    def wrapped(*refs):
        ins, src = refs[:n_in], refs[n_in:n_in + n]
        outs, dst = refs[n_in + n:n_in + n + n_out], refs[n_in + n + n_out:n_in + 2 * n + n_out]
        sems = refs[n_in + 2 * n + n_out:]
        start, wait = _exchange_ops(modes, src, dst, *sems)
        ids = [pl.program_id(a) for a in range(len(grid))]
        first, last = ids[0] == 0, ids[0] == grid[0] - 1
        for a in range(1, len(grid)):
            first, last = first & (ids[a] == 0), last & (ids[a] == grid[a] - 1)
        pl.when(first)(start)
        body(*ins, *outs)
        pl.when(last)(wait)

    return wrapped, ex


LANES = 1024


def adamw(gstacks, w, m, v, name):
    D, R, C = w.shape
    S = gstacks[0].shape[0]
    tr = R if R <= 512 else _tile(R, 256)

    def body(*refs):
        g_refs = refs[:D]
        w_ref, m_ref, v_ref, go_ref, d_ref, mo_ref, vo_ref = refs[D:]
        layer = pl.program_id(0)
        g = None
        for d, g_ref in enumerate(g_refs):
            gd = g_ref[0].astype(f32)
            for sl in range(1, S):
                gd = gd + g_ref[sl].astype(f32)
            g = gd if g is None else jnp.where(layer == d, gd, g)
        g = g[None]
        m_new = ADAM_B1 * m_ref[...] + (1.0 - ADAM_B1) * g
        v_new = ADAM_B2 * v_ref[...] + (1.0 - ADAM_B2) * jnp.square(g)
        m_hat = m_new / (1.0 - ADAM_B1 ** ADAM_STEP)
        v_hat = v_new / (1.0 - ADAM_B2 ** ADAM_STEP)
        go_ref[...] = g
        d_ref[...] = -ADAM_LR * (m_hat / (jnp.sqrt(v_hat) + ADAM_EPS) + ADAM_WD * w_ref[...])
        mo_ref[...] = m_new
        vo_ref[...] = v_new

    row = pl.BlockSpec((1, tr, C), lambda l, i: (l, i, 0))
    g_specs = [pl.BlockSpec((S, tr, C), lambda l, i, d=d: (0, jnp.where(l == d, i, 0), 0)) for d in range(D)]
    return pl.pallas_call(
        body, name=name, grid=(D, R // tr), in_specs=g_specs + [row, row, row],
        out_specs=[row] * 4, out_shape=[jax.ShapeDtypeStruct((D, R, C), f32)] * 4, compiler_params=_params(("parallel", "parallel")),
    )(*gstacks, w, m, v)


def sum_slots(gstack, name):
    S, R, _ = gstack.shape

    def body(g_ref, o_ref):
        g = g_ref[0]
        for sl in range(1, S):
            g = g + g_ref[sl]
        o_ref[...] = g

    return pl.pallas_call(body, name=name, out_shape=jax.ShapeDtypeStruct((R, LANES), f32), compiler_params=_params())(gstack)


def _pack(parts, rows, dtype=f32):
    flat = jnp.concatenate([q.reshape(-1).astype(dtype) for q in parts])
    return jnp.pad(flat, (0, rows * LANES - flat.shape[0])).reshape(rows, LANES)


def _unpack(buf, shapes):
    lead = buf.shape[:-2]
    flat = buf.reshape(lead + (-1,))
    out, off = [], 0
    for shp in shapes:
        n = math.prod(shp)
        out.append(flat[..., off:off + n].reshape(lead + tuple(shp)))
        off += n
    return out


_WEIGHTS = ["pre_mix_norm", "w_in", "a_q_norm", "a_w_uq", "a_kv_norm", "a_w_ukv", "a_out_norm", "b_q_norm", "b_k_norm", "b_out_norm",
            "c_conv_w", "c_conv_b", "c_a_log", "c_dt_bias", "c_d_skip", "c_out_norm", "d_conv_w", "d_a_log", "d_dt_bias", "d_out_norm",
            "w_out", "post_mix_norm", "pre_ffn_norm", "f_w_in", "f_conv_w", "f_conv_b", "f_w_out", "post_ffn_norm"]
_BIG = {"w_in": 1, "a_w_uq": 1, "a_w_ukv": 1, "w_out": 0, "f_w_in": 1, "f_w_out": 0}
_CONV = ["c_conv_w", "d_conv_w", "f_conv_w"]
_REP = [n for n in _WEIGHTS if n not in _BIG and n not in _CONV]
SMALLG_ROWS, SMALLW_ROWS = 64, 32


def _join(blocks, axis):
    return jnp.concatenate([blocks[d] for d in range(N_DEV)], axis=axis)


def _split(full, axis):
    n = full.shape[axis] // N_DEV
    return jnp.stack([lax.slice_in_dim(full, d * n, (d + 1) * n, axis=axis) for d in range(N_DEV)])


def kernel(x, pre_mix_norm, w_in, a_q_norm, a_w_uq, a_kv_norm, a_w_ukv, a_out_norm, b_q_norm, b_k_norm, b_out_norm, c_conv_w, c_conv_b, c_a_log, c_dt_bias, c_d_skip, c_out_norm, d_conv_w, d_a_log, d_dt_bias, d_out_norm, w_out, post_mix_norm, pre_ffn_norm, f_w_in, f_conv_w, f_conv_b, f_w_out, post_ffn_norm, loss_target, m_pre_mix_norm, m_w_in, m_a_q_norm, m_a_w_uq, m_a_kv_norm, m_a_w_ukv, m_a_out_norm, m_b_q_norm, m_b_k_norm, m_b_out_norm, m_c_conv_w, m_c_conv_b, m_c_a_log, m_c_dt_bias, m_c_d_skip, m_c_out_norm, m_d_conv_w, m_d_a_log, m_d_dt_bias, m_d_out_norm, m_w_out, m_post_mix_norm, m_pre_ffn_norm, m_f_w_in, m_f_conv_w, m_f_conv_b, m_f_w_out, m_post_ffn_norm, v_pre_mix_norm, v_w_in, v_a_q_norm, v_a_w_uq, v_a_kv_norm, v_a_w_ukv, v_a_out_norm, v_b_q_norm, v_b_k_norm, v_b_out_norm, v_c_conv_w, v_c_conv_b, v_c_a_log, v_c_dt_bias, v_c_d_skip, v_c_out_norm, v_d_conv_w, v_d_a_log, v_d_dt_bias, v_d_out_norm, v_w_out, v_post_mix_norm, v_pre_ffn_norm, v_f_w_in, v_f_conv_w, v_f_conv_b, v_f_w_out, v_post_ffn_norm):
    given = dict(locals())
    w = {n: given[n] for n in _WEIGHTS}
    mom = {n: given["m_" + n] for n in _WEIGHTS}
    var = {n: given["v_" + n] for n in _WEIGHTS}
    me = 4 * lax.axis_index("x") + 2 * lax.axis_index("y") + lax.axis_index("c")
    layered = lambda names: [(l, n) for l in range(DEPTH) for n in names]

    sharded = list(_BIG) + _CONV
    early = [n for n in sharded if n not in _LATE]
    weight_items = lambda l, names: [(w[n][l].astype(MXU_DTYPE if n in _BIG else f32), True) for n in names]
    grad_items = lambda g, names: [(_split(g[n], _BIG[n]).astype(WIRE_DTYPE), False) for n in names]
    P = {n: [w[n][l] for l in range(DEPTH)] for n in _REP}
    P.update({n: [None] * DEPTH for n in sharded})

    def place(l, names, gathered):
        for n, got in zip(names, gathered):
            P[n][l] = _join(got, _BIG.get(n, 1))

    def late_weights_l0(gathered):
        place(0, _LATE, gathered)
        return _late_kernel_weights(P, 0, MXU_DTYPE)

    tabs = _rope_tables(x.shape[1])
    place(0, early, exchange(weight_items(0, early), "gather_weights"))
    x1, s0, got, W0 = _layer_fwd(x[0], _kernel_weights(P, 0, MXU_DTYPE), tabs, "l0_", weight_items(1, sharded),
                                 weight_items(0, _LATE), late_weights_l0)
    place(1, sharded, got)
    x2, s1, _, W1 = _layer_fwd(x1, _kernel_weights(P, 1, MXU_DTYPE), tabs, "l1_")
    loss, dx = loss_and_grad(x2, loss_target[0], "loss")
    loss = lax.psum(loss[0, 0], ("x", "y", "c"))

    def late_grads_l0(g):
        return grad_items({"w_out": g["w_out"], "f_w_in": _ff_deinterleave(g["f_win"]), "f_w_out": g["f_wout"]}, _LATE)

    dx, g1, _, _ = _layer_bwd(s1, dx, W1, tabs, "l1_")
    g1 = _reference_grads(g1)
    dx, g0, recv1, recv0_late = _layer_bwd(s0, dx, W0, tabs, "l0_", grad_items(g1, _BIG), late_grads_l0)
    grads = [_reference_grads(g0), g1]
    small = _REP + _CONV
    full_shapes = [grads[l][n].shape for l, n in layered(small)]
    first = [n for n in _BIG if n not in _LATE]
    *recv0_first, gsmall = exchange(grad_items(grads[0], first) + [(_pack([grads[l][n] for l, n in layered(small)], SMALLG_ROWS), True)],
                                    "exchange_grads")
    recv0 = dict(zip(first + _LATE, recv0_first + recv0_late))

    out = {}
    for n, got1 in zip(_BIG, recv1):
        out[n] = adamw([recv0[n], got1], w[n], mom[n], var[n], "adamw_" + n)
    gsum = dict(zip(layered(small), _unpack(sum_slots(gsmall, "sum_small_grads"), full_shapes)))
    for l, n in layered(_CONV):
        cols = w[n].shape[2]
        gsum[(l, n)] = lax.dynamic_slice_in_dim(gsum[(l, n)], me * cols, cols, axis=1)
    held_shapes = [w[n].shape[1:] for _, n in layered(small)]
    pack_small = lambda d: _pack([d[n][l] for l, n in layered(small)], SMALLW_ROWS)[None]
    res_small = adamw([_pack([gsum[k] for k in layered(small)], SMALLW_ROWS)[None]], pack_small(w), pack_small(mom),
                      pack_small(var), "adamw_small")
    per_kind = [dict(zip(layered(small), _unpack(buf[0], held_shapes))) for buf in res_small]
    for n in small:
        out[n] = [jnp.stack([per_kind[kind][(l, n)] for l in range(DEPTH)]) for kind in range(4)]
    return (loss, dx[None], *[out[n][kind] for kind in range(4) for n in _WEIGHTS])
```

```python
import functools
import math

import jax
import jax.numpy as jnp
from jax import lax
from jax.experimental import pallas as pl
from jax.experimental.pallas import tpu as pltpu

f32 = jnp.float32
MXU_DTYPE = jnp.bfloat16
WIRE_DTYPE = jnp.bfloat16
HI = lax.Precision.HIGHEST

D_MODEL = 1024
DEPTH = 2
GRID_W = 64
ROPE_BASE = 10000.0
EPS = 1e-6
A_Q_LORA, A_KV_LORA, A_ROPE, A_NOPE = 192, 128, 32, 64
SSD_CHUNK = 128
DN_CHUNK = 64
HEAD = 64
HPAD = 128
D_FF = 2816
FF_BLK = 256
N_DEV = 8
ADAM_LR, ADAM_B1, ADAM_B2, ADAM_EPS, ADAM_WD, ADAM_STEP = 0.001, 0.9, 0.999, 1e-08, 0.01, 10

V7X_VMEM_BYTES = 64 * 2 ** 20
VMEM_LIMIT = (V7X_VMEM_BYTES * 3) // 4

PIN = 3072
SEG_A, SEG_B, SEG_CX, SEG_DX, SEG_CZ, SEG_DZ, SEG_CDT, SEG_DAB = 0, 512, 1024, 1536, 2304, 2560, 2816, 2944


def _tile(n, pref):
    for t in (512, 256, 128, 64, 32, 16, 8):
        if t <= pref and n % t == 0:
            return t
    raise ValueError(f"no tile for {n}")


LANE = 128


def _tile_div(n, pref):
    if n <= pref:
        return n
    return max(d for d in range(LANE, pref + 1, LANE) if n % d == 0)


def _params(sem=None):
    return pltpu.CompilerParams(vmem_limit_bytes=VMEM_LIMIT, dimension_semantics=sem)


def _tup(r):
    return tuple(r) if isinstance(r, (tuple, list)) else (r,)


def matmul(a, b, form, out_dtype, name):
    if form == "nn":
        (M, K), N = a.shape, b.shape[1]
    elif form == "nt":
        (M, K), N = a.shape, b.shape[0]
    else:
        (K, M), N = a.shape, b.shape[1]
    if form == "tn":
        tm, tn, tk = _tile_div(M, 1408), _tile_div(N, 1408), _tile_div(K, 1024)
    else:
        tm, tn, tk = _tile_div(M, 512), _tile_div(N, 1536), _tile_div(K, 3072)
    nk = K // tk
    dims = {"nn": ((1,), (0,)), "nt": ((1,), (1,)), "tn": ((0,), (0,))}[form]

    def body(a_ref, b_ref, o_ref, *acc):
        part = lax.dot_general(a_ref[...].astype(MXU_DTYPE), b_ref[...].astype(MXU_DTYPE), (dims, ((), ())),
                               preferred_element_type=f32)
        if nk == 1:
            o_ref[...] = part.astype(out_dtype)
            return
        (acc_ref,) = acc
        k = pl.program_id(2)

        @pl.when(k == 0)
        def _():
            acc_ref[...] = part

        @pl.when((k > 0) & (k < nk - 1))
        def _():
            acc_ref[...] += part

        @pl.when(k == nk - 1)
        def _():
            o_ref[...] = (acc_ref[...] + part).astype(out_dtype)

    a_spec = pl.BlockSpec((tk, tm), lambda i, j, k: (k, i)) if form == "tn" else pl.BlockSpec((tm, tk), lambda i, j, k: (i, k))
    b_spec = pl.BlockSpec((tn, tk), lambda i, j, k: (j, k)) if form == "nt" else pl.BlockSpec((tk, tn), lambda i, j, k: (k, j))
    return pl.pallas_call(
        body, name=name, grid=(M // tm, N // tn, nk), in_specs=[a_spec, b_spec],
        out_specs=pl.BlockSpec((tm, tn), lambda i, j, k: (i, j)), out_shape=jax.ShapeDtypeStruct((M, N), out_dtype),
        scratch_shapes=[pltpu.VMEM((tm, tn), f32)] if nk > 1 else [], compiler_params=_params(("parallel", "parallel", "arbitrary")),
    )(a, b)


def _row_specs(rows, tm):
    return [pl.BlockSpec((tm, w), lambda i, b=b: (i, b)) for (_, w, b) in rows]


def _full_specs(params):
    return [pl.BlockSpec(p.shape, lambda i: (0, 0)) for p in params]


def rowwise(fn, rows, params, outs, name, tm=512):
    L = rows[0][0].shape[0]
    tm = _tile(L, tm)
    n_in = len(rows) + len(params)

    def body(*refs):
        res = _tup(fn(*[r[...].astype(f32) for r in refs[:n_in]]))
        for o_ref, r in zip(refs[n_in:], res, strict=True):
            o_ref[...] = r.astype(o_ref.dtype)

    res = pl.pallas_call(
        body, name=name, grid=(L // tm,), in_specs=_row_specs(rows, tm) + _full_specs(params),
        out_specs=[pl.BlockSpec((tm, w), lambda i: (i, 0)) for (w, _) in outs],
        out_shape=[jax.ShapeDtypeStruct((L, w), dt) for (w, dt) in outs], compiler_params=_params(("parallel",)),
    )(*[r[0] for r in rows], *params)
    return list(res)


def rowwise_vjp(fn, rows, params, cts, name, row_grads, param_grads, tm=512):
    L = rows[0][0].shape[0]
    tm = _tile(L, tm)
    nr, npar, nct = len(rows), len(params), len(cts)

    def body(*refs):
        i = pl.program_id(0)
        rv = [r[...].astype(f32) for r in refs[:nr]]
        pv = [r[...].astype(f32) for r in refs[nr:nr + npar]]
        cv = tuple(r[...].astype(f32) for r in refs[nr + npar:nr + npar + nct])
        out_refs = refs[nr + npar + nct:]

        def g(*diff):
            rr, pp = list(rv), list(pv)
            for k, v in zip(row_grads, diff[:len(row_grads)]):
                rr[k] = v
            for k, v in zip(param_grads, diff[len(row_grads):]):
                pp[k] = v
            return _tup(fn(*rr, *pp))

        _, vjp = jax.vjp(g, *[rv[k] for k in row_grads], *[pv[k] for k in param_grads])
        grads = vjp(cv)
        for o_ref, gval in zip(out_refs[:len(row_grads)], grads[:len(row_grads)]):
            o_ref[...] = gval
        for o_ref, gval in zip(out_refs[len(row_grads):], grads[len(row_grads):]):
            @pl.when(i == 0)
            def _(o_ref=o_ref, gval=gval):
                o_ref[...] = gval

            @pl.when(i > 0)
            def _(o_ref=o_ref, gval=gval):
                o_ref[...] += gval

    out_specs = [pl.BlockSpec((tm, rows[k][1]), lambda i: (i, 0)) for k in row_grads] + \
                [pl.BlockSpec(params[k].shape, lambda i: (0, 0)) for k in param_grads]
    out_shape = [jax.ShapeDtypeStruct((L, rows[k][1]), f32) for k in row_grads] + \
                [jax.ShapeDtypeStruct(params[k].shape, f32) for k in param_grads]
    res = pl.pallas_call(
        body, name=name, grid=(L // tm,), in_specs=_row_specs(rows, tm) + _full_specs(params) + _row_specs(cts, tm),
        out_specs=out_specs, out_shape=out_shape, compiler_params=_params(("arbitrary",)),
    )(*[r[0] for r in rows], *params, *[c[0] for c in cts])
    res = list(res)
    return res[:len(row_grads)], res[len(row_grads):]


def _rms(x, w, n=None):
    n = x.shape[-1] if n is None else n
    return x * lax.rsqrt(jnp.sum(x * x, axis=-1, keepdims=True) * (1.0 / n) + EPS) * w


def _silu(x):
    return x * jax.nn.sigmoid(x)


def _softplus(x):
    return jnp.maximum(x, 0.0) + jnp.log1p(jnp.exp(-jnp.abs(x)))


def _mm(a, b, dims, precision=None):
    if precision is None:
        a, b = a.astype(MXU_DTYPE), b.astype(MXU_DTYPE)
    return lax.dot_general(a, b, (dims, ((), ())), precision=precision, preferred_element_type=f32)


_NN, _NT, _TN = ((1,), (0,)), ((1,), (1,)), ((0,), (0,))


@functools.partial(jax.custom_vjp, nondiff_argnums=(1, 2))
def _roll(x, shift, axis):
    return pltpu.roll(x, shift, axis)


def _roll_fwd(x, shift, axis):
    return pltpu.roll(x, shift, axis), None


def _roll_bwd(shift, axis, _, ct):
    return (pltpu.roll(ct, (ct.shape[axis] - shift) % ct.shape[axis], axis),)


_roll.defvjp(_roll_fwd, _roll_bwd)


def _rope(x, cos, sin, lo, half):
    n = x.shape[-1]
    lane = lax.broadcasted_iota(jnp.int32, x.shape, x.ndim - 1) % HPAD - lo
    first = ((lane >= 0) & (lane < half)) | ((lane >= 2 * half) & (lane < 3 * half))
    rot = jnp.where(first, -_roll(x, n - half, x.ndim - 1), _roll(x, half, x.ndim - 1))
    return x * cos + rot * sin


def _heads(x, n, width=HEAD):
    return [x[:, h * width:(h + 1) * width] for h in range(n)]


def _pad_heads(hs):
    z = jnp.zeros_like(hs[0])
    return jnp.concatenate([t for h in hs for t in (h, z)], axis=-1)


def _unpad_heads(x, n):
    return jnp.concatenate([x[:, h * HPAD:h * HPAD + HEAD] for h in range(n)], axis=-1)


HALO = 8


def _conv_specs(tm, tc, c0, L):
    nh = tm // HALO
    last = L // HALO - 1
    return [pl.BlockSpec((tm, tc), lambda i, j: (i, c0 + j)),
            pl.BlockSpec((HALO, tc), lambda i, j: (jnp.maximum(i * nh - 1, 0), c0 + j)),
            pl.BlockSpec((HALO, tc), lambda i, j: (jnp.minimum((i + 1) * nh, last), c0 + j))]


def _shift3(x):
    n = x.shape[0]
    return _roll(x, 1, 0), _roll(x, n - 1, 0)


def conv_fwd(gfn, x, c0, ncol, tc, w, b, tco, out_dtype, name, tm=512):
    L = x.shape[0]
    tm = _tile(L, tm)
    ni = L // tm

    def body(x_ref, p_ref, n_ref, w_ref, b_ref, o_ref):
        i = pl.program_id(0)
        xv = x_ref[...].astype(f32)
        xp = jnp.where(i == 0, 0.0, p_ref[HALO - 1:HALO, :].astype(f32))
        xn = jnp.where(i == ni - 1, 0.0, n_ref[0:1, :].astype(f32))
        rid = lax.broadcasted_iota(jnp.int32, xv.shape, 0)
        dn, up = _shift3(xv)
        dn = jnp.where(rid == 0, xp, dn)
        up = jnp.where(rid == tm - 1, xn, up)
        c = w_ref[0:1, :] * dn + w_ref[1:2, :] * xv + w_ref[2:3, :] * up + b_ref[...]
        o_ref[...] = gfn(c).astype(o_ref.dtype)

    return pl.pallas_call(
        body, name=name, grid=(ni, ncol),
        in_specs=_conv_specs(tm, tc, c0, L) + [pl.BlockSpec((3, tc), lambda i, j: (0, j)), pl.BlockSpec((1, tc), lambda i, j: (0, j))],
        out_specs=pl.BlockSpec((tm, tco), lambda i, j: (i, j)), out_shape=jax.ShapeDtypeStruct((L, ncol * tco), out_dtype),
        compiler_params=_params(("parallel", "parallel")),
    )(x, x, x, w, b)


def conv_bwd(gfn, x, c0, ncol, tc, w, b, dys, tco, name, tm=512):
    L = x.shape[0]
    tm = _tile(L, tm)
    ni = L // tm
    nd = len(dys)

    def body(*refs):
        x_ref, xp_ref, xn_ref = refs[:3]
        d_refs = refs[3:3 + 3 * nd]
        w_ref, b_ref, dx_ref, dw_ref, db_ref = refs[3 + 3 * nd:]
        i = pl.program_id(1)
        first, lastb = i == 0, i == ni - 1

        def ext(m, p, n):
            return jnp.concatenate([jnp.where(first, 0.0, p[...].astype(f32)), m[...].astype(f32),
                                    jnp.where(lastb, 0.0, n[...].astype(f32))], axis=0)

        xe = ext(x_ref, xp_ref, xn_ref)
        de = ext(*d_refs[:3])
        for q in range(1, nd):
            de = de + ext(*d_refs[3 * q:3 * q + 3])
        x_dn, x_up = _shift3(xe)
        w0, w1, w2 = w_ref[0:1, :], w_ref[1:2, :], w_ref[2:3, :]
        ce = w0 * x_dn + w1 * xe + w2 * x_up + b_ref[...]
        _, vjp = jax.vjp(gfn, ce)
        (dce,) = vjp(de)
        dc_dn, dc_up = _shift3(dce)
        dx_ref[...] = (w0 * dc_up + w1 * dce + w2 * dc_dn)[HALO:HALO + tm, :]
        inner = slice(HALO, HALO + tm)
        dci = dce[inner]
        dw = jnp.concatenate([jnp.sum(dci * x_dn[inner], axis=0, keepdims=True), jnp.sum(dci * xe[inner], axis=0, keepdims=True),
                              jnp.sum(dci * x_up[inner], axis=0, keepdims=True)], axis=0)
        db = jnp.sum(dci, axis=0, keepdims=True)

        @pl.when(first)
        def _():
            dw_ref[...] = dw
            db_ref[...] = db

        @pl.when(i > 0)
        def _():
            dw_ref[...] += dw
            db_ref[...] += db

    res = pl.pallas_call(
        body, name=name, grid=(ncol, ni),
        in_specs=[pl.BlockSpec(s.block_shape, (lambda j, i, f=s.index_map: f(i, j))) for s in _conv_specs(tm, tc, c0, L)]
        + [pl.BlockSpec(s.block_shape, (lambda j, i, f=s.index_map: f(i, j))) for s in _conv_specs(tm, tco, 0, L)] * nd
        + [pl.BlockSpec((3, tc), lambda j, i: (0, j)), pl.BlockSpec((1, tc), lambda j, i: (0, j))],
        out_specs=[pl.BlockSpec((tm, tc), lambda j, i: (i, j)), pl.BlockSpec((3, tc), lambda j, i: (0, j)),
                   pl.BlockSpec((1, tc), lambda j, i: (0, j))],
        out_shape=[jax.ShapeDtypeStruct((L, ncol * tc), f32), jax.ShapeDtypeStruct((3, ncol * tc), f32),
                   jax.ShapeDtypeStruct((1, ncol * tc), f32)],
        compiler_params=_params(("parallel", "arbitrary")),
    )(x, x, x, *[d for d in dys for _ in range(3)], w, b)
    return res


LOG2E = math.log2(math.e)


def flash_fwd(q, k, v, n_q, n_kv, scale, name, side=(), tq=512, tk=8192):
    L = q.shape[0]
    tq, tk = _tile_div(L, tq), _tile_div(L, tk)
    rep = n_q // n_kv
    nkv = L // tk
    c2 = scale * LOG2E

    def body(q_ref, k_ref, v_ref, o_ref, lse_ref):
        qv = q_ref[...]
        ones_lane = lax.broadcasted_iota(jnp.int32, (tk, HPAD), 1) == HPAD - 1

        def step(c, carry):
            m, acc = carry
            off = pl.multiple_of(c * tk, tk)
            s = _mm(qv, k_ref[pl.ds(off, tk), :], _NT)
            m_new = jnp.maximum(m, jnp.max(s, axis=-1, keepdims=True))
            p = jnp.exp2((s - m_new) * c2)
            vv = v_ref[pl.ds(off, tk), :]
            acc = jnp.exp2((m - m_new) * c2) * acc + _mm(p, jnp.where(ones_lane, jnp.ones_like(vv), vv), _NN)
            return m_new, acc

        m, acc = lax.fori_loop(0, nkv, step, (jnp.full((tq, 1), -jnp.inf, f32), jnp.zeros((tq, HPAD), f32)))
        l = acc[:, HPAD - 1:]
        o_ref[...] = jnp.where(lax.broadcasted_iota(jnp.int32, (tq, HPAD), 1) == HPAD - 1, 0.0, acc / l)
        lse_ref[0] = m * scale + jnp.log(l)

    grid = (n_q, L // tq)
    body, ex = _with_exchange(body, 3, 2, side, grid)
    res = pl.pallas_call(
        body, name=name, grid=grid,
        in_specs=[pl.BlockSpec((tq, HPAD), lambda h, i: (i, h)), pl.BlockSpec((L, HPAD), lambda h, i: (0, h // rep)),
                  pl.BlockSpec((L, HPAD), lambda h, i: (0, h // rep))] + ex.specs,
        out_specs=[pl.BlockSpec((tq, HPAD), lambda h, i: (i, h)), pl.BlockSpec((1, tq, 1), lambda h, i: (h, i, 0))] + ex.specs,
        out_shape=[jax.ShapeDtypeStruct((L, n_q * HPAD), f32), jax.ShapeDtypeStruct((n_q, L, 1), f32)] + ex.out_shape,
        scratch_shapes=ex.scratch, compiler_params=_params(("arbitrary", "arbitrary") if side else ("parallel", "parallel")),
    )(q, k, v, *ex.args)
    return res[0], res[1], list(res[2:])


def flash_bwd(q, k, v, o, lse, do, n_q, n_kv, scale, name, side=(), tq=2048, tk=1024):
    L = q.shape[0]
    tq, tk = _tile_div(L, tq), _tile_div(L, tk)
    rep = n_q // n_kv
    c2 = scale * LOG2E

    def body(k_ref, v_ref, q_ref, do_ref, o_ref, lse_ref, dq_ref, dk_ref, dv_ref):
        j, i = pl.program_id(1), pl.program_id(2)
        kv, vv = k_ref[...], v_ref[...]
        rows = pl.ds(pl.multiple_of(i * tq, tq), tq)
        dk, dv = jnp.zeros((tk, HPAD), f32), jnp.zeros((tk, HPAD), f32)
        for r in range(rep):
            cols = slice(r * HPAD, (r + 1) * HPAD)
            qv = q_ref[:, cols]
            dov = do_ref[:, cols]
            delta = jnp.sum(dov * o_ref[:, cols], axis=-1, keepdims=True)
            p = jnp.exp2(_mm(qv, kv, _NT) * c2 - lse_ref[r] * LOG2E)
            dv = dv + _mm(p, dov, _TN)
            ds = p * (_mm(dov, vv, _NT) - delta) * scale
            dk = dk + _mm(ds, qv, _TN)
            dq = _mm(ds, kv, _NN)

            @pl.when(j == 0)
            def _(dq=dq, cols=cols):
                dq_ref[rows, cols] = dq

            @pl.when(j > 0)
            def _(dq=dq, cols=cols):
                dq_ref[rows, cols] += dq

        @pl.when(i == 0)
        def _():
            dk_ref[...] = dk
            dv_ref[...] = dv

        @pl.when(i > 0)
        def _():
            dk_ref[...] += dk
            dv_ref[...] += dv

    w = rep * HPAD
    grid = (n_kv, L // tk, L // tq)
    body, ex = _with_exchange(body, 6, 3, side, grid)
    res = pl.pallas_call(
        body, name=name, grid=grid,
        in_specs=[pl.BlockSpec((tk, HPAD), lambda g, j, i: (j, g)), pl.BlockSpec((tk, HPAD), lambda g, j, i: (j, g)),
                  pl.BlockSpec((tq, w), lambda g, j, i: (i, g)), pl.BlockSpec((tq, w), lambda g, j, i: (i, g)),
                  pl.BlockSpec((tq, w), lambda g, j, i: (i, g)), pl.BlockSpec((rep, tq, 1), lambda g, j, i: (g, i, 0))] + ex.specs,
        out_specs=[pl.BlockSpec((L, w), lambda g, j, i: (0, g)), pl.BlockSpec((tk, HPAD), lambda g, j, i: (j, g)),
                   pl.BlockSpec((tk, HPAD), lambda g, j, i: (j, g))] + ex.specs,
        out_shape=[jax.ShapeDtypeStruct((L, n_q * HPAD), f32), jax.ShapeDtypeStruct((L, n_kv * HPAD), f32),
                   jax.ShapeDtypeStruct((L, n_kv * HPAD), f32)] + ex.out_shape,
        scratch_shapes=ex.scratch, compiler_params=_params(("arbitrary",) * 3 if side else ("parallel", "arbitrary", "arbitrary")),
    )(k, v, q, do, o, lse, *ex.args)
    return res[0], res[1], res[2], list(res[3:])


N_SCAN_HEADS = 4
STATE_ROWS = N_SCAN_HEADS * HEAD


def _tri(n, rev):
    i = lax.broadcasted_iota(jnp.int32, (n, n), 0)
    k = lax.broadcasted_iota(jnp.int32, (n, n), 1)
    return ((k >= i), (k > i)) if rev else ((k <= i), (k < i))


def _decay(acol, incl):
    n = acol.shape[0]
    m1 = jnp.broadcast_to(acol, (n, n))
    return jnp.exp(jnp.where(incl, m1 - m1.T, -jnp.inf))


def _ssd_chunk(S, xbc, win, a_log, dt_bias, rev):
    Q = xbc.shape[0]
    d = 1 if rev else 0
    incl, _ = _tri(Q, rev)
    dt = _softplus(win[:, 4 * d:4 * d + 4] + dt_bias[d:d + 1, :])
    a = dt * (-jnp.exp(a_log[d:d + 1, :]))
    acum = _mm(incl.astype(f32), a, _NN, HI)
    tot = jnp.sum(a, axis=0, keepdims=True)
    xs, Bm, Cm = xbc[:, :256], xbc[:, 256:384], xbc[:, 384:512]
    H, W = N_SCAN_HEADS, N_SCAN_HEADS * HEAD
    lanes = lambda x: _cat([jnp.broadcast_to(x[:, h:h + 1], (x.shape[0], HEAD)) for h in range(H)])
    xdt = xs * lanes(dt)
    cb = [_mm(Cm[:, g * HEAD:(g + 1) * HEAD], Bm[:, g * HEAD:(g + 1) * HEAD], _NT) for g in range(2)]
    scores = _cat([cb[h // 2] * _decay(acum[:, h:h + 1], incl) for h in range(H)])
    own = (lax.broadcasted_iota(jnp.int32, (H * Q, W), 0) // Q) == (lax.broadcasted_iota(jnp.int32, (H * Q, W), 1) // HEAD)
    y = _mm(scores, jnp.where(own, jnp.concatenate([xdt] * H, axis=0), 0.0), _NN)
    grp = (lax.broadcasted_iota(jnp.int32, (W, 2 * HEAD), 0) // (2 * HEAD)) == (lax.broadcasted_iota(jnp.int32, (W, 2 * HEAD), 1) // HEAD)
    y = y + _mm(Cm, jnp.where(grp, _cat([S, S]), 0.0), _NT) * jnp.exp(lanes(acum))
    st = _mm(xdt * jnp.exp(lanes(tot - acum)), Bm, _TN)
    first = lax.broadcasted_iota(jnp.int32, (W, HEAD), 0) < 2 * HEAD
    exp_tot = jnp.concatenate([jnp.broadcast_to(jnp.exp(tot[:, h:h + 1]), (HEAD, 1)) for h in range(H)], axis=0)
    return S * exp_tot + jnp.where(first, st[:, :HEAD], st[:, HEAD:]), y


@functools.partial(jax.custom_vjp, nondiff_argnums=(1,))
def _inv_unit_tri(Lm, order):
    n = Lm.shape[0]
    eye = (lax.broadcasted_iota(jnp.int32, (n, n), 0) == lax.broadcasted_iota(jnp.int32, (n, n), 1)).astype(f32)
    P = -Lm
    T = eye + P
    k = 1
    while 2 * k < order:
        P = _mm(P, P, _NN, lax.Precision.HIGH)
        T = T + _mm(T, P, _NN, lax.Precision.HIGH)
        k *= 2
    return T


def _inv_unit_tri_fwd(Lm, order):
    T = _inv_unit_tri(Lm, order)
    return T, T


def _inv_unit_tri_bwd(order, T, dT):
    return (-_mm(_mm(T, dT, _TN, lax.Precision.HIGH), T, _NT, lax.Precision.HIGH),)


_inv_unit_tri.defvjp(_inv_unit_tri_fwd, _inv_unit_tri_bwd)


@jax.custom_vjp
def _inv_known(Lm, T):
    return T


def _inv_known_fwd(Lm, T):
    return T, T


def _inv_known_bwd(T, dT):
    return _inv_unit_tri_bwd(None, T, dT) + (jnp.zeros_like(T),)


_inv_known.defvjp(_inv_known_fwd, _inv_known_bwd)


def _dn_chunk(S, qkv, win, a_log, dt_bias, rev, kept=None):
    Q = qkv.shape[0]
    H, R = N_SCAN_HEADS, N_SCAN_HEADS * qkv.shape[0]
    d = 1 if rev else 0
    beta = jax.nn.sigmoid(win[:, 4 * d:4 * d + 4])
    gl = -jnp.exp(a_log[d:d + 1, :]) * _softplus(win[:, 8 + 4 * d:12 + 4 * d] + dt_bias[d:d + 1, :])
    G = _mm(_tri(Q, rev)[0].astype(f32), gl, _NN, HI)
    tot = jnp.sum(gl, axis=0, keepdims=True)
    rows = lambda x: jnp.concatenate(_heads(x, H), axis=0)
    col = lambda x: jnp.concatenate([x[:, h:h + 1] for h in range(H)], axis=0)
    per_head = lambda x, n: jnp.concatenate([jnp.broadcast_to(x[:, h:h + 1], (n, 1)) for h in range(H)], axis=0)
    r = lax.broadcasted_iota(jnp.int32, (R, R), 0)
    c = lax.broadcasted_iota(jnp.int32, (R, R), 1)
    same = (r // Q) == (c // Q)
    incl = same & ((c >= r) if rev else (c <= r))
    strict = same & ((c > r) if rev else (c < r))
    own = (lax.broadcasted_iota(jnp.int32, (R, H * HEAD), 0) // Q) == (lax.broadcasted_iota(jnp.int32, (R, H * HEAD), 1) // HEAD)
    blk = lambda x: jnp.where(own, _cat([x] * H), 0.0)
    q, k, v = rows(qkv[:, :256]) * (HEAD ** -0.5), rows(qkv[:, 256:512]), rows(qkv[:, 512:])
    b, Gs, tots = col(beta), col(G), per_head(tot, Q)
    dec = _decay(Gs, incl)
    kb = k * b
    Lm = jnp.where(strict, _mm(kb, k, _NT) * dec, 0.0)
    T = _inv_unit_tri(Lm, Q) if kept is None else _inv_known(Lm, blk(kept))
    eG = jnp.exp(Gs)
    uw = _mm(T, _cat([v * b, kb * eG]), _NN)
    vnew = uw[:, :HEAD] - _mm(blk(uw[:, HEAD:]), S, _NN)
    o = _mm(blk(q * eG), S, _NN) + _mm(_mm(q, k, _NT) * dec, vnew, _NN)
    S_next = S * jnp.exp(per_head(tot, HEAD)) + _mm(blk(k * jnp.exp(tots - Gs)), vnew, _TN)
    out = S_next, _cat([o[h * Q:(h + 1) * Q, :] for h in range(H)])
    if kept is None:
        out += (jnp.concatenate([T[h * Q:(h + 1) * Q, h * Q:(h + 1) * Q] for h in range(H)], axis=0),)
    return out


def scan_fwd(chunk_fn, Q, x, p, win_blk, a_log, dt_bias, name, keeps=False):
    L, W = x.shape
    nc = L // Q
    fidx, ridx = (lambda t: t), (lambda t: nc - 1 - t)

    def body(xf_ref, xr_ref, wf_ref, wr_ref, al_ref, db_ref, yf_ref, yr_ref, sf_ref, sr_ref, *rest):
        (sf_scr, sr_scr), keep_refs = rest[-2:], rest[:-2]

        @pl.when(pl.program_id(0) == 0)
        def _():
            sf_scr[...] = jnp.zeros_like(sf_scr)
            sr_scr[...] = jnp.zeros_like(sr_scr)

        for d, (x_ref, w_ref, y_ref, sin_ref, s_scr) in enumerate(((xf_ref, wf_ref, yf_ref, sf_ref, sf_scr),
                                                                   (xr_ref, wr_ref, yr_ref, sr_ref, sr_scr))):
            S = s_scr[...]
            sin_ref[...] = S
            res = chunk_fn(S, x_ref[...], w_ref[...], al_ref[...], db_ref[...], d == 1)
            s_scr[...], y_ref[...] = res[:2]
            if keeps:
                keep_refs[d][...] = res[2]

    row = lambda w, idx, b=0: pl.BlockSpec((Q, w), lambda t: (idx(t), b))
    small = pl.BlockSpec(a_log.shape, lambda t: (0, 0))
    state = lambda idx: pl.BlockSpec((STATE_ROWS, HEAD), lambda t: (idx(t), 0))
    n_state = 4 if keeps else 2
    return pl.pallas_call(
        body, name=name, grid=(nc,),
        in_specs=[row(W, fidx), row(W, ridx), row(HPAD, fidx, win_blk), row(HPAD, ridx, win_blk), small, small],
        out_specs=[row(STATE_ROWS, fidx), row(STATE_ROWS, ridx)] + [state(fidx), state(ridx)] * (n_state // 2),
        out_shape=[jax.ShapeDtypeStruct((L, STATE_ROWS), f32)] * 2 + [jax.ShapeDtypeStruct((nc * STATE_ROWS, HEAD), f32)] * n_state,
        scratch_shapes=[pltpu.VMEM((STATE_ROWS, HEAD), f32)] * 2, compiler_params=_params(("arbitrary",)),
    )(x, x, p, p, a_log, dt_bias)


def scan_bwd(chunk_fn, Q, x, p, win_blk, a_log, dt_bias, s_fwd, s_rev, dy, name, kept=()):
    L, W = x.shape
    nc = L // Q
    fidx, ridx = (lambda t: nc - 1 - t), (lambda t: t)
    nk = len(kept)

    def body(xf_ref, xr_ref, wf_ref, wr_ref, al_ref, db_ref, sf_ref, sr_ref, dyf_ref, dyr_ref, *rest):
        keep_refs = rest[:nk]
        dxf_ref, dxr_ref, dwf_ref, dwr_ref, dal_ref, ddb_ref, dsf_scr, dsr_scr = rest[nk:]

        @pl.when(pl.program_id(0) == 0)
        def _():
            dsf_scr[...] = jnp.zeros_like(dsf_scr)
            dsr_scr[...] = jnp.zeros_like(dsr_scr)
            dal_ref[...] = jnp.zeros_like(dal_ref)
            ddb_ref[...] = jnp.zeros_like(ddb_ref)

        dal_sum, ddb_sum = dal_ref[...], ddb_ref[...]
        for rev, x_ref, w_ref, sin_ref, dy_ref, dx_ref, dw_ref, ds_scr in (
                (False, xf_ref, wf_ref, sf_ref, dyf_ref, dxf_ref, dwf_ref, dsf_scr),
                (True, xr_ref, wr_ref, sr_ref, dyr_ref, dxr_ref, dwr_ref, dsr_scr)):
            extra = (keep_refs[int(rev)][...],) if nk else ()
            _, vjp = jax.vjp(lambda S, xv, wv, al, db, rev=rev, extra=extra: chunk_fn(S, xv, wv, al, db, rev, *extra),
                             sin_ref[...], x_ref[...], w_ref[...], al_ref[...], db_ref[...])
            ds_scr[...], dx_ref[...], dw_ref[...], dal, ddb = vjp((ds_scr[...], dy_ref[...]))
            dal_sum, ddb_sum = dal_sum + dal, ddb_sum + ddb
        dal_ref[...] = dal_sum
        ddb_ref[...] = ddb_sum

    row = lambda w, idx, b=0: pl.BlockSpec((Q, w), lambda t: (idx(t), b))
    small = pl.BlockSpec(a_log.shape, lambda t: (0, 0))
    state = lambda idx: pl.BlockSpec((STATE_ROWS, HEAD), lambda t: (idx(t), 0))
    return pl.pallas_call(
        body, name=name, grid=(nc,),
        in_specs=[row(W, fidx), row(W, ridx), row(HPAD, fidx, win_blk), row(HPAD, ridx, win_blk), small, small,
                  state(fidx), state(ridx), row(STATE_ROWS, fidx), row(STATE_ROWS, ridx)] + [state(fidx), state(ridx)][:nk],
        out_specs=[row(W, fidx), row(W, ridx), row(HPAD, fidx), row(HPAD, ridx), small, small],
        out_shape=[jax.ShapeDtypeStruct((L, W), f32)] * 2 + [jax.ShapeDtypeStruct((L, HPAD), f32)] * 2
        + [jax.ShapeDtypeStruct(a_log.shape, f32)] * 2,
        scratch_shapes=[pltpu.VMEM((STATE_ROWS, HEAD), f32)] * 2, compiler_params=_params(("arbitrary",)),
    )(x, x, p, p, a_log, dt_bias, s_fwd, s_rev, dy, dy, *kept)


def _cat(xs):
    return jnp.concatenate(xs, axis=-1)


def _a_prep(pa, cos, sin, qn, wuq, kvn, wk, wv):
    cq, ckv, kr = pa[:, :256], pa[:, 256:384], pa[:, 384:512]
    q = _rope(_mm(_rms(cq, qn, A_Q_LORA), wuq, _NN), _cat([cos] * 4), _cat([sin] * 4), A_NOPE, A_ROPE // 4)
    kvh = _rms(ckv, kvn)
    k = _mm(kvh, wk, _NN) + _cat([_rope(kr, cos, sin, A_NOPE, A_ROPE // 4)] * 4)
    return q, k, _mm(kvh, wv, _NN)


def _b_prep(pb, cos, sin, qn, kn):
    cos4, sin4 = _cat([cos] * 4), _cat([sin] * 4)
    q = _pad_heads([_rms(h, qn) for h in _heads(pb[:, :256], 4)])
    k = _pad_heads([_rms(h, kn) for h in _heads(pb[:, 256:384], 2) for _ in range(2)])
    v = _pad_heads([h for h in _heads(pb[:, 384:512], 2) for _ in range(2)])
    return _rope(q, cos4, sin4, 0, HEAD // 4), _rope(k, cos4, sin4, 0, HEAD // 4), v


def _c_act(c):
    return _silu(c)


def _d_act(c):
    s = _silu(c)
    return _cat([h * lax.rsqrt(jnp.sum(h * h, axis=-1, keepdims=True) + 1e-6) for h in _heads(s[:, :512], 8)] + [s[:, 512:]])


def _ffn_act(c):
    return _silu(c[:, :FF_BLK]) * c[:, FF_BLK:]


def _mix_post(oa, ob, yc_f, yc_r, xbc, cz, od_f, od_r, dz, a_out, b_out, dskip, c_out, d_out):
    o_a = _rms(_unpad_heads(oa, 4), a_out)
    o_b = _rms(_unpad_heads(ob, 4), b_out)
    skip = _cat([jnp.broadcast_to(dskip[:, h:h + 1], (1, HEAD)) for h in range(4)])
    o_c = _rms((yc_f + yc_r + xbc[:, :256] * skip) * _silu(cz), c_out)
    o_d = _cat([_rms(h, d_out) for h in _heads(od_f + od_r, 4)]) * _silu(dz)
    return _cat([o_a, o_b, o_c, o_d])


def _post_mix(x, y1, g_pm, g_pf):
    x1 = x + _rms(y1, g_pm)
    return x1, _rms(x1, g_pf)


def _post_ffn(x1, y2, g):
    return x1 + _rms(y2, g)


def _pre_bwd(x, g):
    return x, _rms(x, g)


def _layer_fwd(x, W, tabs, n, side=(), late_side=(), late_weights=None):
    mx = MXU_DTYPE
    cos_a, sin_a, cos_b, sin_b = tabs
    full = lambda a: (a, a.shape[1], 0)
    s = {"x": x}
    (s["h"],) = rowwise(_rms, [full(x)], [W["g_pre"]], [(D_MODEL, mx)], n + "pre")
    p = s["p"] = matmul(s["h"], W["w_in"], "nn", f32, n + "in")
    s["qa"], s["ka"], s["va"] = rowwise(_a_prep, [(p, 512, SEG_A // 512), full(cos_a), full(sin_a)],
                                        [W["a_qn"], W["a_wuq"], W["a_kvn"], W["a_wk"], W["a_wv"]], [(512, mx)] * 3, n + "a_prep")
    s["qb"], s["kb"], s["vb"] = rowwise(_b_prep, [(p, 512, SEG_B // 512), full(cos_b), full(sin_b)], [W["b_qn"], W["b_kn"]],
                                        [(512, mx)] * 3, n + "b_prep")
    s["oa"], s["lsea"], side_out = flash_fwd(s["qa"], s["ka"], s["va"], 4, 4, (A_NOPE + A_ROPE) ** -0.5, n + "a_attn", side)
    s["ob"], s["lseb"], late_out = flash_fwd(s["qb"], s["kb"], s["vb"], 4, 4, HEAD ** -0.5, n + "b_attn", late_side)
    if late_weights is not None:
        W = {**W, **late_weights(late_out)}
    xbc = s["xbc"] = conv_fwd(_c_act, p, SEG_CX // 512, 1, 512, W["c_cw"], W["c_cb"], 512, f32, n + "c_conv")
    s["yc_f"], s["yc_r"], s["sc_f"], s["sc_r"] = scan_fwd(_ssd_chunk, SSD_CHUNK, xbc, p, SEG_CDT // HPAD, W["c_alog"], W["c_dtb"], n + "c_ssd")
    qkv = s["qkv"] = conv_fwd(_d_act, p, SEG_DX // 768, 1, 768, W["d_cw"], W["d_cb"], 768, f32, n + "d_conv")
    s["od_f"], s["od_r"], s["sd_f"], s["sd_r"], s["td_f"], s["td_r"] = scan_fwd(
        _dn_chunk, DN_CHUNK, qkv, p, SEG_DAB // HPAD, W["d_alog"], W["d_dtb"], n + "d_dn", keeps=True)
    (s["omix"],) = rowwise(_mix_post, _mix_rows(s), _mix_params(W), [(D_MODEL, mx)], n + "mix_post")
    s["y1"] = matmul(s["omix"], W["w_out"], "nn", f32, n + "out")
    s["x1"], s["h2"] = rowwise(_post_mix, [full(x), full(s["y1"])], [W["g_pm"], W["g_pf"]], [(D_MODEL, f32), (D_MODEL, mx)], n + "post_mix")
    s["u"] = matmul(s["h2"], W["f_win"], "nn", f32, n + "f_in")
    s["a"] = conv_fwd(_ffn_act, s["u"], 0, D_FF // FF_BLK, 2 * FF_BLK, W["f_cw"], W["f_cb"], FF_BLK, mx, n + "f_conv", tm=512)
    s["y2"] = matmul(s["a"], W["f_wout"], "nn", f32, n + "f_out")
    (x2,) = rowwise(_post_ffn, [full(s["x1"]), full(s["y2"])], [W["g_po"]], [(D_MODEL, f32)], n + "post_ffn")
    return x2, s, side_out, W


def _mix_rows(s):
    p = s["p"]
    return [(s["oa"], 512, 0), (s["ob"], 512, 0), (s["yc_f"], 256, 0), (s["yc_r"], 256, 0), (s["xbc"], 512, 0),
            (p, 256, SEG_CZ // 256), (s["od_f"], 256, 0), (s["od_r"], 256, 0), (p, 256, SEG_DZ // 256)]


def _mix_params(W):
    return [W["a_out"], W["b_out"], W["c_dskip"], W["c_out"], W["d_out"]]


def _layer_bwd(s, dx2, W, tabs, n, side=(), late_side=None):
    cos_a, sin_a, cos_b, sin_b = tabs
    full = lambda a: (a, a.shape[1], 0)
    p, g = s["p"], {}
    (dx1, dy2), (g["g_po"],) = rowwise_vjp(_post_ffn, [full(s["x1"]), full(s["y2"])], [W["g_po"]], [full(dx2)], n + "post_ffn_b", [0, 1], [0])
    da = matmul(dy2, W["f_wout"], "nt", f32, n + "f_out_dx")
    g["f_wout"] = matmul(s["a"], dy2, "tn", f32, n + "f_out_dw")
    du, g["f_cw"], g["f_cb"] = conv_bwd(_ffn_act, s["u"], 0, D_FF // FF_BLK, 2 * FF_BLK, W["f_cw"], W["f_cb"], [da], FF_BLK, n + "f_conv_b", tm=512)
    dh2 = matmul(du, W["f_win"], "nt", f32, n + "f_in_dx")
    g["f_win"] = matmul(s["h2"], du, "tn", f32, n + "f_in_dw")
    (dx, dy1), (g["g_pm"], g["g_pf"]) = rowwise_vjp(_post_mix, [full(s["x"]), full(s["y1"])], [W["g_pm"], W["g_pf"]],
                                                   [full(dx1), full(dh2)], n + "post_mix_b", [0, 1], [0, 1])
    domix = matmul(dy1, W["w_out"], "nt", f32, n + "out_dx")
    g["w_out"] = matmul(s["omix"], dy1, "tn", f32, n + "out_dw")
    (doa, dob, dyc, dxbc_skip, dcz, dod, ddz), (g["a_out"], g["b_out"], g["c_dskip"], g["c_out"], g["d_out"]) = rowwise_vjp(
        _mix_post, _mix_rows(s), _mix_params(W), [full(domix)], n + "mix_post_b", [0, 1, 2, 4, 5, 6, 8], list(range(5)))
    dqkv_f, dqkv_r, ddab_f, ddab_r, g["d_alog"], g["d_dtb"] = scan_bwd(
        _dn_chunk, DN_CHUNK, s["qkv"], p, SEG_DAB // HPAD, W["d_alog"], W["d_dtb"], s["sd_f"], s["sd_r"], dod, n + "d_dn_b",
        kept=(s["td_f"], s["td_r"]))
    ddx, g["d_cw"], _ = conv_bwd(_d_act, p, SEG_DX // 768, 1, 768, W["d_cw"], W["d_cb"], [dqkv_f, dqkv_r], 768, n + "d_conv_b")
    dxbc_f, dxbc_r, dcdt_f, dcdt_r, g["c_alog"], g["c_dtb"] = scan_bwd(
        _ssd_chunk, SSD_CHUNK, s["xbc"], p, SEG_CDT // HPAD, W["c_alog"], W["c_dtb"], s["sc_f"], s["sc_r"], dyc, n + "c_ssd_b")
    dcx, g["c_cw"], g["c_cb"] = conv_bwd(_c_act, p, SEG_CX // 512, 1, 512, W["c_cw"], W["c_cb"], [dxbc_f, dxbc_r, dxbc_skip], 512,
                                         n + "c_conv_b")
    (dsmall,) = rowwise(lambda a, b, c, d: _cat([a + b, c + d]), [full(dcdt_f), full(dcdt_r), full(ddab_f), full(ddab_r)], [],
                        [(2 * HPAD, f32)], n + "dwin_sum")
    dqa, dka, dva, side_out = flash_bwd(s["qa"], s["ka"], s["va"], s["oa"], s["lsea"], doa, 4, 4, (A_NOPE + A_ROPE) ** -0.5,
                                        n + "a_attn_b", side)
    dqb, dkb, dvb, late_out = flash_bwd(s["qb"], s["kb"], s["vb"], s["ob"], s["lseb"], dob, 4, 4, HEAD ** -0.5, n + "b_attn_b",
                                        late_side(g) if late_side is not None else ())
    (dpa,), (g["a_qn"], g["a_wuq"], g["a_kvn"], g["a_wk"], g["a_wv"]) = rowwise_vjp(
        _a_prep, [(p, 512, SEG_A // 512), full(cos_a), full(sin_a)], [W["a_qn"], W["a_wuq"], W["a_kvn"], W["a_wk"], W["a_wv"]],
        [full(dqa), full(dka), full(dva)], n + "a_prep_b", [0], list(range(5)))
    (dpb,), (g["b_qn"], g["b_kn"]) = rowwise_vjp(_b_prep, [(p, 512, SEG_B // 512), full(cos_b), full(sin_b)], [W["b_qn"], W["b_kn"]],
                                               [full(dqb), full(dkb), full(dvb)], n + "b_prep_b", [0], [0, 1])
    dp = jnp.concatenate([dpa, dpb, dcx, ddx, dcz, ddz, dsmall], axis=1)
    dh = matmul(dp, W["w_in"], "nt", f32, n + "in_dx")
    g["w_in"] = matmul(s["h"], dp, "tn", f32, n + "in_dw")
    (dx0,), (g["g_pre"],) = rowwise_vjp(_pre_bwd, [full(s["x"])], [W["g_pre"]], [full(dx), full(dh)], n + "pre_b", [0], [0])
    return dx0, g, side_out, late_out


def loss_and_grad(y, target, name):
    L, D = y.shape
    tm = _tile(L, 512)

    def body(y_ref, t_ref, loss_ref, dy_ref):
        e = y_ref[...] - t_ref[...]
        dy_ref[...] = e * (1.0 / D)
        part = 0.5 * jnp.sum(jnp.sum(e * e, axis=1, keepdims=True) * (1.0 / D), axis=0, keepdims=True)

        @pl.when(pl.program_id(0) == 0)
        def _():
            loss_ref[...] = part

        @pl.when(pl.program_id(0) > 0)
        def _():
            loss_ref[...] += part

    row = pl.BlockSpec((tm, D), lambda i: (i, 0))
    return pl.pallas_call(
        body, name=name, grid=(L // tm,), in_specs=[row, row], out_specs=[pl.BlockSpec((1, 1), lambda i: (0, 0)), row],
        out_shape=[jax.ShapeDtypeStruct((1, 1), f32), jax.ShapeDtypeStruct((L, D), f32)], compiler_params=_params(("arbitrary",)),
    )(y, target)


_IN_SEGS = [(0, 192), (256, 128), (384 + A_NOPE, 32), (512, 256), (768, 128), (896, 128), (SEG_CZ, 256), (SEG_CX, 512),
            (SEG_CDT, 8), (SEG_DX, 768), (SEG_DZ, 256), (SEG_DAB, 16)]


def _pad_in(w):
    src, pieces = 0, {}
    for off, wd in _IN_SEGS:
        pieces[off] = w[..., src:src + wd]
        src += wd
    out, pos = [], 0
    for off in sorted(pieces):
        if off > pos:
            out.append(jnp.zeros(w.shape[:-1] + (off - pos,), w.dtype))
        out.append(pieces[off])
        pos = off + pieces[off].shape[-1]
    out.append(jnp.zeros(w.shape[:-1] + (PIN - pos,), w.dtype))
    return jnp.concatenate(out, axis=-1)


def _unpad_in(wp):
    return jnp.concatenate([wp[..., off:off + wd] for off, wd in _IN_SEGS], axis=-1)


def _pad_last(a, n):
    return jnp.pad(a, [(0, 0)] * (a.ndim - 1) + [(0, n - a.shape[-1])])


def _ff_interleave(w):
    blocks = range(D_FF // FF_BLK)
    return jnp.concatenate([w[..., off + j * FF_BLK:off + (j + 1) * FF_BLK] for j in blocks for off in (0, D_FF)], axis=-1)


def _ff_deinterleave(w):
    blocks = range(D_FF // FF_BLK)
    return jnp.concatenate([w[..., (2 * j + half) * FF_BLK:(2 * j + half + 1) * FF_BLK] for half in (0, 1) for j in blocks], axis=-1)


def _row(v):
    return v.reshape(1, -1)


_LATE = ["w_out", "f_w_in", "f_w_out"]


def _late_kernel_weights(P, l, wdt):
    return {"w_out": P["w_out"][l].astype(wdt), "f_win": _ff_interleave(P["f_w_in"][l]).astype(wdt), "f_wout": P["f_w_out"][l].astype(wdt)}


def _kernel_weights(P, l, wdt):
    late = {} if P["w_out"][l] is None else _late_kernel_weights(P, l, wdt)
    uq = P["a_w_uq"][l].reshape(A_Q_LORA, 4, A_NOPE + A_ROPE)
    ukv = P["a_w_ukv"][l].reshape(A_KV_LORA, 4, 2 * HEAD)
    z = jnp.zeros((A_KV_LORA, 4, HEAD), ukv.dtype)
    return {
        "g_pre": _row(P["pre_mix_norm"][l]), "w_in": _pad_in(P["w_in"][l]).astype(wdt),
        "a_qn": _pad_last(_row(P["a_q_norm"][l]), 256),
        "a_wuq": jnp.pad(_pad_last(uq, HPAD).reshape(A_Q_LORA, 4 * HPAD), ((0, 256 - A_Q_LORA), (0, 0))).astype(wdt),
        "a_kvn": _row(P["a_kv_norm"][l]),
        "a_wk": jnp.concatenate([ukv[..., :HEAD], z], axis=-1).reshape(A_KV_LORA, 4 * HPAD).astype(wdt),
        "a_wv": jnp.concatenate([ukv[..., HEAD:], z], axis=-1).reshape(A_KV_LORA, 4 * HPAD).astype(wdt),
        "a_out": _row(P["a_out_norm"][l]), "b_qn": _row(P["b_q_norm"][l]), "b_kn": _row(P["b_k_norm"][l]), "b_out": _row(P["b_out_norm"][l]),
        "c_cw": P["c_conv_w"][l], "c_cb": _row(P["c_conv_b"][l]), "c_alog": P["c_a_log"][l], "c_dtb": P["c_dt_bias"][l],
        "c_dskip": _row(P["c_d_skip"][l]), "c_out": _row(P["c_out_norm"][l]),
        "d_cw": P["d_conv_w"][l], "d_cb": jnp.zeros((1, 768), f32), "d_alog": P["d_a_log"][l], "d_dtb": P["d_dt_bias"][l],
        "d_out": _row(P["d_out_norm"][l]), "g_pm": _row(P["post_mix_norm"][l]), "g_pf": _row(P["pre_ffn_norm"][l]),
        "f_cw": _ff_interleave(P["f_conv_w"][l]), "f_cb": _row(_ff_interleave(P["f_conv_b"][l])),
        "g_po": _row(P["post_ffn_norm"][l]), **late,
    }


def _reference_grads(g):
    uq = g["a_wuq"][:A_Q_LORA].reshape(A_Q_LORA, 4, HPAD)[..., :A_NOPE + A_ROPE].reshape(A_Q_LORA, 4 * (A_NOPE + A_ROPE))
    wk = g["a_wk"].reshape(A_KV_LORA, 4, HPAD)[..., :HEAD]
    wv = g["a_wv"].reshape(A_KV_LORA, 4, HPAD)[..., :HEAD]
    return {
        "pre_mix_norm": g["g_pre"][0], "w_in": _unpad_in(g["w_in"]), "a_q_norm": g["a_qn"][0, :A_Q_LORA], "a_w_uq": uq,
        "a_kv_norm": g["a_kvn"][0], "a_w_ukv": jnp.concatenate([wk, wv], axis=-1).reshape(A_KV_LORA, 8 * HEAD),
        "a_out_norm": g["a_out"][0], "b_q_norm": g["b_qn"][0], "b_k_norm": g["b_kn"][0], "b_out_norm": g["b_out"][0],
        "c_conv_w": g["c_cw"], "c_conv_b": g["c_cb"][0], "c_a_log": g["c_alog"], "c_dt_bias": g["c_dtb"], "c_d_skip": g["c_dskip"][0],
        "c_out_norm": g["c_out"][0], "d_conv_w": g["d_cw"], "d_a_log": g["d_alog"], "d_dt_bias": g["d_dtb"], "d_out_norm": g["d_out"][0],
        "w_out": g["w_out"], "post_mix_norm": g["g_pm"][0], "pre_ffn_norm": g["g_pf"][0], "f_w_in": _ff_deinterleave(g["f_win"]),
        "f_conv_w": _ff_deinterleave(g["f_cw"]), "f_conv_b": _ff_deinterleave(g["f_cb"][0]), "f_w_out": g["f_wout"],
        "post_ffn_norm": g["g_po"][0],
    }


def _rope_tables(L):
    def tables(rot):
        rows = L // GRID_W
        row = jnp.repeat(jnp.arange(rows), GRID_W).astype(f32)
        col = jnp.tile(jnp.arange(GRID_W), rows).astype(f32)
        sec = rot // 2
        inv = ROPE_BASE ** (-jnp.arange(0, sec, 2, dtype=f32) / sec)
        ang = jnp.concatenate([row[:, None] * inv] * 2 + [col[:, None] * inv] * 2, axis=-1)
        return jnp.cos(ang), jnp.sin(ang)

    ca, sa = tables(A_ROPE)
    cb, sb = tables(HEAD)
    one, zero = jnp.ones, jnp.zeros
    return (jnp.concatenate([one((L, A_NOPE), f32), ca, one((L, HPAD - A_NOPE - A_ROPE), f32)], axis=1),
            jnp.concatenate([zero((L, A_NOPE), f32), sa, zero((L, HPAD - A_NOPE - A_ROPE), f32)], axis=1),
            jnp.concatenate([cb, one((L, HPAD - HEAD), f32)], axis=1), jnp.concatenate([sb, zero((L, HPAD - HEAD), f32)], axis=1))


def exchange(items, name):
    n = len(items)

    def body(*refs):
        start, wait = _exchange_ops([g for _, g in items], refs[:n], refs[n:2 * n], *refs[2 * n:])
        start()
        wait()

    ex = _exchange_io(items)
    res = pl.pallas_call(body, name=name, in_specs=ex.specs, out_specs=ex.specs, out_shape=ex.out_shape,
                         scratch_shapes=ex.scratch)(*ex.args)
    return list(res)


def _exchange_ops(modes, src_refs, out_refs, send_sems, recv_sems, local_sems):
    n = len(modes)
    x, y, c = lax.axis_index("x"), lax.axis_index("y"), lax.axis_index("c")
    me = 4 * x + 2 * y + c

    def copy(a, k, arriving):
        px, py, pc = (1 - x if k & 4 else x), (1 - y if k & 2 else y), (1 - c if k & 1 else c)
        pid = 4 * px + 2 * py + pc
        sem = a * (N_DEV - 1) + k - 1
        return pltpu.make_async_remote_copy(
            src_ref=src_refs[a] if modes[a] else src_refs[a].at[pid], dst_ref=out_refs[a].at[pid if arriving else me],
            send_sem=send_sems.at[sem], recv_sem=recv_sems.at[sem], device_id=(px, py, pc), device_id_type=pl.DeviceIdType.MESH)

    def local(a):
        return pltpu.make_async_copy(src_refs[a] if modes[a] else src_refs[a].at[me], out_refs[a].at[me], local_sems.at[a])

    pairs = [(a, k) for k in range(1, N_DEV) for a in range(n)]

    def start():
        for a, k in pairs:
            copy(a, k, False).start()
        for a in range(n):
            local(a).start()

    def wait():
        for a, k in pairs:
            copy(a, k, False).wait_send()
        for a, k in pairs:
            copy(a, k, True).wait_recv()
        for a in range(n):
            local(a).wait()

    return start, wait


class _ExchangeIO:
    def __init__(self, items):
        n = len(items)
        self.args = [s for s, _ in items]
        self.specs = [pl.BlockSpec(memory_space=pl.ANY)] * n
        self.out_shape = [jax.ShapeDtypeStruct((N_DEV,) + tuple(s.shape if g else s.shape[1:]), s.dtype) for s, g in items]
        self.scratch = [pltpu.SemaphoreType.DMA((n * (N_DEV - 1),)), pltpu.SemaphoreType.DMA((n * (N_DEV - 1),)),
                        pltpu.SemaphoreType.DMA((n,))] if n else []


def _exchange_io(items):
    return _ExchangeIO(list(items))


def _with_exchange(body, n_in, n_out, side, grid):
    ex = _exchange_io(side)
    n = len(ex.args)
    if not n:
        return body, ex
    modes = [g for _, g in side]

    def wrapped(*refs):
        ins, src = refs[:n_in], refs[n_in:n_in + n]
        outs, dst = refs[n_in + n:n_in + n + n_out], refs[n_in + n + n_out:n_in + 2 * n + n_out]
        sems = refs[n_in + 2 * n + n_out:]
        start, wait = _exchange_ops(modes, src, dst, *sems)
        ids = [pl.program_id(a) for a in range(len(grid))]
        first, last = ids[0] == 0, ids[0] == grid[0] - 1
        for a in range(1, len(grid)):
            first, last = first & (ids[a] == 0), last & (ids[a] == grid[a] - 1)
        pl.when(first)(start)
        body(*ins, *outs)
        pl.when(last)(wait)

    return wrapped, ex


LANES = 1024


def adamw(gstacks, w, m, v, name):
    D, R, C = w.shape
    S = gstacks[0].shape[0]
    tr = R if R <= 512 else _tile(R, 256)

    def body(*refs):
        g_refs = refs[:D]
        w_ref, m_ref, v_ref, go_ref, d_ref, mo_ref, vo_ref = refs[D:]
        layer = pl.program_id(0)
        g = None
        for d, g_ref in enumerate(g_refs):
            gd = g_ref[0].astype(f32)
            for sl in range(1, S):
                gd = gd + g_ref[sl].astype(f32)
            g = gd if g is None else jnp.where(layer == d, gd, g)
        g = g[None]
        m_new = ADAM_B1 * m_ref[...] + (1.0 - ADAM_B1) * g
        v_new = ADAM_B2 * v_ref[...] + (1.0 - ADAM_B2) * jnp.square(g)
        m_hat = m_new / (1.0 - ADAM_B1 ** ADAM_STEP)
        v_hat = v_new / (1.0 - ADAM_B2 ** ADAM_STEP)
        go_ref[...] = g
        d_ref[...] = -ADAM_LR * (m_hat / (jnp.sqrt(v_hat) + ADAM_EPS) + ADAM_WD * w_ref[...])
        mo_ref[...] = m_new
        vo_ref[...] = v_new

    row = pl.BlockSpec((1, tr, C), lambda l, i: (l, i, 0))
    g_specs = [pl.BlockSpec((S, tr, C), lambda l, i, d=d: (0, jnp.where(l == d, i, 0), 0)) for d in range(D)]
    return pl.pallas_call(
        body, name=name, grid=(D, R // tr), in_specs=g_specs + [row, row, row],
        out_specs=[row] * 4, out_shape=[jax.ShapeDtypeStruct((D, R, C), f32)] * 4, compiler_params=_params(("parallel", "parallel")),
    )(*gstacks, w, m, v)


def sum_slots(gstack, name):
    S, R, _ = gstack.shape

    def body(g_ref, o_ref):
        g = g_ref[0]
        for sl in range(1, S):
            g = g + g_ref[sl]
        o_ref[...] = g

    return pl.pallas_call(body, name=name, out_shape=jax.ShapeDtypeStruct((R, LANES), f32), compiler_params=_params())(gstack)


def _pack(parts, rows, dtype=f32):
    flat = jnp.concatenate([q.reshape(-1).astype(dtype) for q in parts])
    return jnp.pad(flat, (0, rows * LANES - flat.shape[0])).reshape(rows, LANES)


def _unpack(buf, shapes):
    lead = buf.shape[:-2]
    flat = buf.reshape(lead + (-1,))
    out, off = [], 0
    for shp in shapes:
        n = math.prod(shp)
        out.append(flat[..., off:off + n].reshape(lead + tuple(shp)))
        off += n
    return out


_WEIGHTS = ["pre_mix_norm", "w_in", "a_q_norm", "a_w_uq", "a_kv_norm", "a_w_ukv", "a_out_norm", "b_q_norm", "b_k_norm", "b_out_norm",
            "c_conv_w", "c_conv_b", "c_a_log", "c_dt_bias", "c_d_skip", "c_out_norm", "d_conv_w", "d_a_log", "d_dt_bias", "d_out_norm",
            "w_out", "post_mix_norm", "pre_ffn_norm", "f_w_in", "f_conv_w", "f_conv_b", "f_w_out", "post_ffn_norm"]
_BIG = {"w_in": 1, "a_w_uq": 1, "a_w_ukv": 1, "w_out": 0, "f_w_in": 1, "f_w_out": 0}
_CONV = ["c_conv_w", "d_conv_w", "f_conv_w"]
_REP = [n for n in _WEIGHTS if n not in _BIG and n not in _CONV]
SMALLG_ROWS, SMALLW_ROWS = 64, 32


def _join(blocks, axis):
    return jnp.concatenate([blocks[d] for d in range(N_DEV)], axis=axis)


def _split(full, axis):
    n = full.shape[axis] // N_DEV
    return jnp.stack([lax.slice_in_dim(full, d * n, (d + 1) * n, axis=axis) for d in range(N_DEV)])


def kernel(x, pre_mix_norm, w_in, a_q_norm, a_w_uq, a_kv_norm, a_w_ukv, a_out_norm, b_q_norm, b_k_norm, b_out_norm, c_conv_w, c_conv_b, c_a_log, c_dt_bias, c_d_skip, c_out_norm, d_conv_w, d_a_log, d_dt_bias, d_out_norm, w_out, post_mix_norm, pre_ffn_norm, f_w_in, f_conv_w, f_conv_b, f_w_out, post_ffn_norm, loss_target, m_pre_mix_norm, m_w_in, m_a_q_norm, m_a_w_uq, m_a_kv_norm, m_a_w_ukv, m_a_out_norm, m_b_q_norm, m_b_k_norm, m_b_out_norm, m_c_conv_w, m_c_conv_b, m_c_a_log, m_c_dt_bias, m_c_d_skip, m_c_out_norm, m_d_conv_w, m_d_a_log, m_d_dt_bias, m_d_out_norm, m_w_out, m_post_mix_norm, m_pre_ffn_norm, m_f_w_in, m_f_conv_w, m_f_conv_b, m_f_w_out, m_post_ffn_norm, v_pre_mix_norm, v_w_in, v_a_q_norm, v_a_w_uq, v_a_kv_norm, v_a_w_ukv, v_a_out_norm, v_b_q_norm, v_b_k_norm, v_b_out_norm, v_c_conv_w, v_c_conv_b, v_c_a_log, v_c_dt_bias, v_c_d_skip, v_c_out_norm, v_d_conv_w, v_d_a_log, v_d_dt_bias, v_d_out_norm, v_w_out, v_post_mix_norm, v_pre_ffn_norm, v_f_w_in, v_f_conv_w, v_f_conv_b, v_f_w_out, v_post_ffn_norm):
    given = dict(locals())
    w = {n: given[n] for n in _WEIGHTS}
    mom = {n: given["m_" + n] for n in _WEIGHTS}
    var = {n: given["v_" + n] for n in _WEIGHTS}
    me = 4 * lax.axis_index("x") + 2 * lax.axis_index("y") + lax.axis_index("c")
    layered = lambda names: [(l, n) for l in range(DEPTH) for n in names]

    sharded = list(_BIG) + _CONV
    early = [n for n in sharded if n not in _LATE]
    weight_items = lambda l, names: [(w[n][l].astype(MXU_DTYPE if n in _BIG else f32), True) for n in names]
    grad_items = lambda g, names: [(_split(g[n], _BIG[n]).astype(WIRE_DTYPE), False) for n in names]
    P = {n: [w[n][l] for l in range(DEPTH)] for n in _REP}
    P.update({n: [None] * DEPTH for n in sharded})

    def place(l, names, gathered):
        for n, got in zip(names, gathered):
            P[n][l] = _join(got, _BIG.get(n, 1))

    def late_weights_l0(gathered):
        place(0, _LATE, gathered)
        return _late_kernel_weights(P, 0, MXU_DTYPE)

    tabs = _rope_tables(x.shape[1])
    place(0, early, exchange(weight_items(0, early), "gather_weights"))
    x1, s0, got, W0 = _layer_fwd(x[0], _kernel_weights(P, 0, MXU_DTYPE), tabs, "l0_", weight_items(1, sharded),
                                 weight_items(0, _LATE), late_weights_l0)
    place(1, sharded, got)
    x2, s1, _, W1 = _layer_fwd(x1, _kernel_weights(P, 1, MXU_DTYPE), tabs, "l1_")
    loss, dx = loss_and_grad(x2, loss_target[0], "loss")
    loss = lax.psum(loss[0, 0], ("x", "y", "c"))

    def late_grads_l0(g):
        return grad_items({"w_out": g["w_out"], "f_w_in": _ff_deinterleave(g["f_win"]), "f_w_out": g["f_wout"]}, _LATE)

    dx, g1, _, _ = _layer_bwd(s1, dx, W1, tabs, "l1_")
    g1 = _reference_grads(g1)
    dx, g0, recv1, recv0_late = _layer_bwd(s0, dx, W0, tabs, "l0_", grad_items(g1, _BIG), late_grads_l0)
    grads = [_reference_grads(g0), g1]
    small = _REP + _CONV
    full_shapes = [grads[l][n].shape for l, n in layered(small)]
    first = [n for n in _BIG if n not in _LATE]
    *recv0_first, gsmall = exchange(grad_items(grads[0], first) + [(_pack([grads[l][n] for l, n in layered(small)], SMALLG_ROWS), True)],
                                    "exchange_grads")
    recv0 = dict(zip(first + _LATE, recv0_first + recv0_late))

    out = {}
    for n, got1 in zip(_BIG, recv1):
        out[n] = adamw([recv0[n], got1], w[n], mom[n], var[n], "adamw_" + n)
    gsum = dict(zip(layered(small), _unpack(sum_slots(gsmall, "sum_small_grads"), full_shapes)))
    for l, n in layered(_CONV):
        cols = w[n].shape[2]
        gsum[(l, n)] = lax.dynamic_slice_in_dim(gsum[(l, n)], me * cols, cols, axis=1)
    held_shapes = [w[n].shape[1:] for _, n in layered(small)]
    pack_small = lambda d: _pack([d[n][l] for l, n in layered(small)], SMALLW_ROWS)[None]
    res_small = adamw([_pack([gsum[k] for k in layered(small)], SMALLW_ROWS)[None]], pack_small(w), pack_small(mom),
                      pack_small(var), "adamw_small")
    per_kind = [dict(zip(layered(small), _unpack(buf[0], held_shapes))) for buf in res_small]
    for n in small:
        out[n] = [jnp.stack([per_kind[kind][(l, n)] for l in range(DEPTH)]) for kind in range(4)]
    return (loss, dx[None], *[out[n][kind] for kind in range(4) for n in _WEIGHTS])
```

```python
import functools
import math

import jax
import jax.numpy as jnp
from jax import lax
from jax.experimental import pallas as pl
from jax.experimental.pallas import tpu as pltpu

f32 = jnp.float32
MXU_DTYPE = jnp.bfloat16
WIRE_DTYPE = jnp.bfloat16
HI = lax.Precision.HIGHEST

D_MODEL = 1024
DEPTH = 2
GRID_W = 64
ROPE_BASE = 10000.0
EPS = 1e-6
A_Q_LORA, A_KV_LORA, A_ROPE, A_NOPE = 192, 128, 32, 64
SSD_CHUNK = 256
DN_CHUNK = 64
HEAD = 64
HPAD = 128
D_FF = 2816
FF_BLK = 256
N_DEV = 8
ADAM_LR, ADAM_B1, ADAM_B2, ADAM_EPS, ADAM_WD, ADAM_STEP = 0.001, 0.9, 0.999, 1e-08, 0.01, 10

V7X_VMEM_BYTES = 64 * 2 ** 20
VMEM_LIMIT = (V7X_VMEM_BYTES * 3) // 4

PIN = 3072
SEG_A, SEG_B, SEG_CX, SEG_DX, SEG_CZ, SEG_DZ, SEG_CDT, SEG_DAB = 0, 512, 1024, 1536, 2304, 2560, 2816, 2944


def _tile(n, pref):
    for t in (512, 256, 128, 64, 32, 16, 8):
        if t <= pref and n % t == 0:
            return t
    raise ValueError(f"no tile for {n}")


LANE = 128


def _tile_div(n, pref):
    if n <= pref:
        return n
    return max(d for d in range(LANE, pref + 1, LANE) if n % d == 0)


def _params(sem=None):
    return pltpu.CompilerParams(vmem_limit_bytes=VMEM_LIMIT, dimension_semantics=sem)


def _tup(r):
    return tuple(r) if isinstance(r, (tuple, list)) else (r,)


def matmul(a, b, form, out_dtype, name):
    if form == "nn":
        (M, K), N = a.shape, b.shape[1]
    elif form == "nt":
        (M, K), N = a.shape, b.shape[0]
    else:
        (K, M), N = a.shape, b.shape[1]
    if form == "tn":
        tm, tn, tk = _tile_div(M, 1408), _tile_div(N, 1408), _tile_div(K, 1024)
    else:
        tm, tn, tk = _tile_div(M, 512), _tile_div(N, 1536), _tile_div(K, 3072)
    nk = K // tk
    dims = {"nn": ((1,), (0,)), "nt": ((1,), (1,)), "tn": ((0,), (0,))}[form]

    def body(a_ref, b_ref, o_ref, *acc):
        part = lax.dot_general(a_ref[...].astype(MXU_DTYPE), b_ref[...].astype(MXU_DTYPE), (dims, ((), ())),
                               preferred_element_type=f32)
        if nk == 1:
            o_ref[...] = part.astype(out_dtype)
            return
        (acc_ref,) = acc
        k = pl.program_id(2)

        @pl.when(k == 0)
        def _():
            acc_ref[...] = part

        @pl.when((k > 0) & (k < nk - 1))
        def _():
            acc_ref[...] += part

        @pl.when(k == nk - 1)
        def _():
            o_ref[...] = (acc_ref[...] + part).astype(out_dtype)

    a_spec = pl.BlockSpec((tk, tm), lambda i, j, k: (k, i)) if form == "tn" else pl.BlockSpec((tm, tk), lambda i, j, k: (i, k))
    b_spec = pl.BlockSpec((tn, tk), lambda i, j, k: (j, k)) if form == "nt" else pl.BlockSpec((tk, tn), lambda i, j, k: (k, j))
    return pl.pallas_call(
        body, name=name, grid=(M // tm, N // tn, nk), in_specs=[a_spec, b_spec],
        out_specs=pl.BlockSpec((tm, tn), lambda i, j, k: (i, j)), out_shape=jax.ShapeDtypeStruct((M, N), out_dtype),
        scratch_shapes=[pltpu.VMEM((tm, tn), f32)] if nk > 1 else [], compiler_params=_params(("parallel", "parallel", "arbitrary")),
    )(a, b)


def _row_specs(rows, tm):
    return [pl.BlockSpec((tm, w), lambda i, b=b: (i, b)) for (_, w, b) in rows]


def _full_specs(params):
    return [pl.BlockSpec(p.shape, lambda i: (0, 0)) for p in params]


def rowwise(fn, rows, params, outs, name, tm=512):
    L = rows[0][0].shape[0]
    tm = _tile(L, tm)
    n_in = len(rows) + len(params)

    def body(*refs):
        res = _tup(fn(*[r[...].astype(f32) for r in refs[:n_in]]))
        for o_ref, r in zip(refs[n_in:], res, strict=True):
            o_ref[...] = r.astype(o_ref.dtype)

    res = pl.pallas_call(
        body, name=name, grid=(L // tm,), in_specs=_row_specs(rows, tm) + _full_specs(params),
        out_specs=[pl.BlockSpec((tm, w), lambda i: (i, 0)) for (w, _) in outs],
        out_shape=[jax.ShapeDtypeStruct((L, w), dt) for (w, dt) in outs], compiler_params=_params(("parallel",)),
    )(*[r[0] for r in rows], *params)
    return list(res)


def rowwise_vjp(fn, rows, params, cts, name, row_grads, param_grads, tm=512):
    L = rows[0][0].shape[0]
    tm = _tile(L, tm)
    nr, npar, nct = len(rows), len(params), len(cts)

    def body(*refs):
        i = pl.program_id(0)
        rv = [r[...].astype(f32) for r in refs[:nr]]
        pv = [r[...].astype(f32) for r in refs[nr:nr + npar]]
        cv = tuple(r[...].astype(f32) for r in refs[nr + npar:nr + npar + nct])
        out_refs = refs[nr + npar + nct:]

        def g(*diff):
            rr, pp = list(rv), list(pv)
            for k, v in zip(row_grads, diff[:len(row_grads)]):
                rr[k] = v
            for k, v in zip(param_grads, diff[len(row_grads):]):
                pp[k] = v
            return _tup(fn(*rr, *pp))

        _, vjp = jax.vjp(g, *[rv[k] for k in row_grads], *[pv[k] for k in param_grads])
        grads = vjp(cv)
        for o_ref, gval in zip(out_refs[:len(row_grads)], grads[:len(row_grads)]):
            o_ref[...] = gval
        for o_ref, gval in zip(out_refs[len(row_grads):], grads[len(row_grads):]):
            @pl.when(i == 0)
            def _(o_ref=o_ref, gval=gval):
                o_ref[...] = gval

            @pl.when(i > 0)
            def _(o_ref=o_ref, gval=gval):
                o_ref[...] += gval

    out_specs = [pl.BlockSpec((tm, rows[k][1]), lambda i: (i, 0)) for k in row_grads] + \
                [pl.BlockSpec(params[k].shape, lambda i: (0, 0)) for k in param_grads]
    out_shape = [jax.ShapeDtypeStruct((L, rows[k][1]), f32) for k in row_grads] + \
                [jax.ShapeDtypeStruct(params[k].shape, f32) for k in param_grads]
    res = pl.pallas_call(
        body, name=name, grid=(L // tm,), in_specs=_row_specs(rows, tm) + _full_specs(params) + _row_specs(cts, tm),
        out_specs=out_specs, out_shape=out_shape, compiler_params=_params(("arbitrary",)),
    )(*[r[0] for r in rows], *params, *[c[0] for c in cts])
    res = list(res)
    return res[:len(row_grads)], res[len(row_grads):]


def _rms(x, w, n=None):
    n = x.shape[-1] if n is None else n
    return x * lax.rsqrt(jnp.sum(x * x, axis=-1, keepdims=True) * (1.0 / n) + EPS) * w


def _silu(x):
    return x * jax.nn.sigmoid(x)


def _softplus(x):
    return jnp.maximum(x, 0.0) + jnp.log1p(jnp.exp(-jnp.abs(x)))


def _mm(a, b, dims, precision=None):
    if precision is None:
        a, b = a.astype(MXU_DTYPE), b.astype(MXU_DTYPE)
    return lax.dot_general(a, b, (dims, ((), ())), precision=precision, preferred_element_type=f32)


_NN, _NT, _TN = ((1,), (0,)), ((1,), (1,)), ((0,), (0,))


@functools.partial(jax.custom_vjp, nondiff_argnums=(1, 2))
def _roll(x, shift, axis):
    return pltpu.roll(x, shift, axis)


def _roll_fwd(x, shift, axis):
    return pltpu.roll(x, shift, axis), None


def _roll_bwd(shift, axis, _, ct):
    return (pltpu.roll(ct, (ct.shape[axis] - shift) % ct.shape[axis], axis),)


_roll.defvjp(_roll_fwd, _roll_bwd)


def _rope(x, cos, sin, lo, half):
    n = x.shape[-1]
    lane = lax.broadcasted_iota(jnp.int32, x.shape, x.ndim - 1) % HPAD - lo
    first = ((lane >= 0) & (lane < half)) | ((lane >= 2 * half) & (lane < 3 * half))
    rot = jnp.where(first, -_roll(x, n - half, x.ndim - 1), _roll(x, half, x.ndim - 1))
    return x * cos + rot * sin


def _heads(x, n, width=HEAD):
    return [x[:, h * width:(h + 1) * width] for h in range(n)]


def _pad_heads(hs):
    z = jnp.zeros_like(hs[0])
    return jnp.concatenate([t for h in hs for t in (h, z)], axis=-1)


def _unpad_heads(x, n):
    return jnp.concatenate([x[:, h * HPAD:h * HPAD + HEAD] for h in range(n)], axis=-1)


HALO = 8


def _conv_specs(tm, tc, c0, L):
    nh = tm // HALO
    last = L // HALO - 1
    return [pl.BlockSpec((tm, tc), lambda i, j: (i, c0 + j)),
            pl.BlockSpec((HALO, tc), lambda i, j: (jnp.maximum(i * nh - 1, 0), c0 + j)),
            pl.BlockSpec((HALO, tc), lambda i, j: (jnp.minimum((i + 1) * nh, last), c0 + j))]


def _shift3(x):
    n = x.shape[0]
    return _roll(x, 1, 0), _roll(x, n - 1, 0)


def conv_fwd(gfn, x, c0, ncol, tc, w, b, tco, out_dtype, name, tm=512):
    L = x.shape[0]
    tm = _tile(L, tm)
    ni = L // tm

    def body(x_ref, p_ref, n_ref, w_ref, b_ref, o_ref):
        i = pl.program_id(0)
        xv = x_ref[...].astype(f32)
        xp = jnp.where(i == 0, 0.0, p_ref[HALO - 1:HALO, :].astype(f32))
        xn = jnp.where(i == ni - 1, 0.0, n_ref[0:1, :].astype(f32))
        rid = lax.broadcasted_iota(jnp.int32, xv.shape, 0)
        dn, up = _shift3(xv)
        dn = jnp.where(rid == 0, xp, dn)
        up = jnp.where(rid == tm - 1, xn, up)
        c = w_ref[0:1, :] * dn + w_ref[1:2, :] * xv + w_ref[2:3, :] * up + b_ref[...]
        o_ref[...] = gfn(c).astype(o_ref.dtype)

    return pl.pallas_call(
        body, name=name, grid=(ni, ncol),
        in_specs=_conv_specs(tm, tc, c0, L) + [pl.BlockSpec((3, tc), lambda i, j: (0, j)), pl.BlockSpec((1, tc), lambda i, j: (0, j))],
        out_specs=pl.BlockSpec((tm, tco), lambda i, j: (i, j)), out_shape=jax.ShapeDtypeStruct((L, ncol * tco), out_dtype),
        compiler_params=_params(("parallel", "parallel")),
    )(x, x, x, w, b)


def conv_bwd(gfn, x, c0, ncol, tc, w, b, dys, tco, name, tm=512):
    L = x.shape[0]
    tm = _tile(L, tm)
    ni = L // tm
    nd = len(dys)

    def body(*refs):
        x_ref, xp_ref, xn_ref = refs[:3]
        d_refs = refs[3:3 + 3 * nd]
        w_ref, b_ref, dx_ref, dw_ref, db_ref = refs[3 + 3 * nd:]
        i = pl.program_id(1)
        first, lastb = i == 0, i == ni - 1

        def ext(m, p, n):
            return jnp.concatenate([jnp.where(first, 0.0, p[...].astype(f32)), m[...].astype(f32),
                                    jnp.where(lastb, 0.0, n[...].astype(f32))], axis=0)

        xe = ext(x_ref, xp_ref, xn_ref)
        de = ext(*d_refs[:3])
        for q in range(1, nd):
            de = de + ext(*d_refs[3 * q:3 * q + 3])
        x_dn, x_up = _shift3(xe)
        w0, w1, w2 = w_ref[0:1, :], w_ref[1:2, :], w_ref[2:3, :]
        ce = w0 * x_dn + w1 * xe + w2 * x_up + b_ref[...]
        _, vjp = jax.vjp(gfn, ce)
        (dce,) = vjp(de)
        dc_dn, dc_up = _shift3(dce)
        dx_ref[...] = (w0 * dc_up + w1 * dce + w2 * dc_dn)[HALO:HALO + tm, :]
        inner = slice(HALO, HALO + tm)
        dci = dce[inner]
        dw = jnp.concatenate([jnp.sum(dci * x_dn[inner], axis=0, keepdims=True), jnp.sum(dci * xe[inner], axis=0, keepdims=True),
                              jnp.sum(dci * x_up[inner], axis=0, keepdims=True)], axis=0)
        db = jnp.sum(dci, axis=0, keepdims=True)

        @pl.when(first)
        def _():
            dw_ref[...] = dw
            db_ref[...] = db

        @pl.when(i > 0)
        def _():
            dw_ref[...] += dw
            db_ref[...] += db

    res = pl.pallas_call(
        body, name=name, grid=(ncol, ni),
        in_specs=[pl.BlockSpec(s.block_shape, (lambda j, i, f=s.index_map: f(i, j))) for s in _conv_specs(tm, tc, c0, L)]
        + [pl.BlockSpec(s.block_shape, (lambda j, i, f=s.index_map: f(i, j))) for s in _conv_specs(tm, tco, 0, L)] * nd
        + [pl.BlockSpec((3, tc), lambda j, i: (0, j)), pl.BlockSpec((1, tc), lambda j, i: (0, j))],
        out_specs=[pl.BlockSpec((tm, tc), lambda j, i: (i, j)), pl.BlockSpec((3, tc), lambda j, i: (0, j)),
                   pl.BlockSpec((1, tc), lambda j, i: (0, j))],
        out_shape=[jax.ShapeDtypeStruct((L, ncol * tc), f32), jax.ShapeDtypeStruct((3, ncol * tc), f32),
                   jax.ShapeDtypeStruct((1, ncol * tc), f32)],
        compiler_params=_params(("parallel", "arbitrary")),
    )(x, x, x, *[d for d in dys for _ in range(3)], w, b)
    return res


LOG2E = math.log2(math.e)


def flash_fwd(q, k, v, n_q, n_kv, scale, name, side=(), tq=512, tk=8192):
    L = q.shape[0]
    tq, tk = _tile_div(L, tq), _tile_div(L, tk)
    rep = n_q // n_kv
    nkv = L // tk
    c2 = scale * LOG2E

    def body(q_ref, k_ref, v_ref, o_ref, lse_ref):
        qv = q_ref[...]
        ones_lane = lax.broadcasted_iota(jnp.int32, (tk, HPAD), 1) == HPAD - 1

        def step(c, carry):
            m, acc = carry
            off = pl.multiple_of(c * tk, tk)
            s = _mm(qv, k_ref[pl.ds(off, tk), :], _NT)
            m_new = jnp.maximum(m, jnp.max(s, axis=-1, keepdims=True))
            p = jnp.exp2((s - m_new) * c2)
            vv = v_ref[pl.ds(off, tk), :]
            acc = jnp.exp2((m - m_new) * c2) * acc + _mm(p, jnp.where(ones_lane, jnp.ones_like(vv), vv), _NN)
            return m_new, acc

        m, acc = lax.fori_loop(0, nkv, step, (jnp.full((tq, 1), -jnp.inf, f32), jnp.zeros((tq, HPAD), f32)))
        l = acc[:, HPAD - 1:]
        o_ref[...] = jnp.where(lax.broadcasted_iota(jnp.int32, (tq, HPAD), 1) == HPAD - 1, 0.0, acc / l)
        lse_ref[0] = m * scale + jnp.log(l)

    grid = (n_q, L // tq)
    body, ex = _with_exchange(body, 3, 2, side, grid)
    res = pl.pallas_call(
        body, name=name, grid=grid,
        in_specs=[pl.BlockSpec((tq, HPAD), lambda h, i: (i, h)), pl.BlockSpec((L, HPAD), lambda h, i: (0, h // rep)),
                  pl.BlockSpec((L, HPAD), lambda h, i: (0, h // rep))] + ex.specs,
        out_specs=[pl.BlockSpec((tq, HPAD), lambda h, i: (i, h)), pl.BlockSpec((1, tq, 1), lambda h, i: (h, i, 0))] + ex.specs,
        out_shape=[jax.ShapeDtypeStruct((L, n_q * HPAD), f32), jax.ShapeDtypeStruct((n_q, L, 1), f32)] + ex.out_shape,
        scratch_shapes=ex.scratch, compiler_params=_params(("arbitrary", "arbitrary") if side else ("parallel", "parallel")),
    )(q, k, v, *ex.args)
    return res[0], res[1], list(res[2:])


def flash_bwd(q, k, v, o, lse, do, n_q, n_kv, scale, name, side=(), tq=2048, tk=1024):
    L = q.shape[0]
    tq, tk = _tile_div(L, tq), _tile_div(L, tk)
    rep = n_q // n_kv
    c2 = scale * LOG2E

    def body(k_ref, v_ref, q_ref, do_ref, o_ref, lse_ref, dq_ref, dk_ref, dv_ref):
        j, i = pl.program_id(1), pl.program_id(2)
        kv, vv = k_ref[...], v_ref[...]
        rows = pl.ds(pl.multiple_of(i * tq, tq), tq)
        dk, dv = jnp.zeros((tk, HPAD), f32), jnp.zeros((tk, HPAD), f32)
        for r in range(rep):
            cols = slice(r * HPAD, (r + 1) * HPAD)
            qv = q_ref[:, cols]
            dov = do_ref[:, cols]
            delta = jnp.sum(dov * o_ref[:, cols], axis=-1, keepdims=True)
            p = jnp.exp2(_mm(qv, kv, _NT) * c2 - lse_ref[r] * LOG2E)
            dv = dv + _mm(p, dov, _TN)
            ds = p * (_mm(dov, vv, _NT) - delta) * scale
            dk = dk + _mm(ds, qv, _TN)
            dq = _mm(ds, kv, _NN)

            @pl.when(j == 0)
            def _(dq=dq, cols=cols):
                dq_ref[rows, cols] = dq

            @pl.when(j > 0)
            def _(dq=dq, cols=cols):
                dq_ref[rows, cols] += dq

        @pl.when(i == 0)
        def _():
            dk_ref[...] = dk
            dv_ref[...] = dv

        @pl.when(i > 0)
        def _():
            dk_ref[...] += dk
            dv_ref[...] += dv

    w = rep * HPAD
    grid = (n_kv, L // tk, L // tq)
    body, ex = _with_exchange(body, 6, 3, side, grid)
    res = pl.pallas_call(
        body, name=name, grid=grid,
        in_specs=[pl.BlockSpec((tk, HPAD), lambda g, j, i: (j, g)), pl.BlockSpec((tk, HPAD), lambda g, j, i: (j, g)),
                  pl.BlockSpec((tq, w), lambda g, j, i: (i, g)), pl.BlockSpec((tq, w), lambda g, j, i: (i, g)),
                  pl.BlockSpec((tq, w), lambda g, j, i: (i, g)), pl.BlockSpec((rep, tq, 1), lambda g, j, i: (g, i, 0))] + ex.specs,
        out_specs=[pl.BlockSpec((L, w), lambda g, j, i: (0, g)), pl.BlockSpec((tk, HPAD), lambda g, j, i: (j, g)),
                   pl.BlockSpec((tk, HPAD), lambda g, j, i: (j, g))] + ex.specs,
        out_shape=[jax.ShapeDtypeStruct((L, n_q * HPAD), f32), jax.ShapeDtypeStruct((L, n_kv * HPAD), f32),
                   jax.ShapeDtypeStruct((L, n_kv * HPAD), f32)] + ex.out_shape,
        scratch_shapes=ex.scratch, compiler_params=_params(("arbitrary",) * 3 if side else ("parallel", "arbitrary", "arbitrary")),
    )(k, v, q, do, o, lse, *ex.args)
    return res[0], res[1], res[2], list(res[3:])


N_SCAN_HEADS = 4
STATE_ROWS = N_SCAN_HEADS * HEAD


def _tri(n, rev):
    i = lax.broadcasted_iota(jnp.int32, (n, n), 0)
    k = lax.broadcasted_iota(jnp.int32, (n, n), 1)
    return ((k >= i), (k > i)) if rev else ((k <= i), (k < i))


def _decay(acol, incl):
    n = acol.shape[0]
    m1 = jnp.broadcast_to(acol, (n, n))
    return jnp.exp(jnp.where(incl, m1 - m1.T, -jnp.inf))


def _ssd_chunk(S, xbc, win, a_log, dt_bias, rev):
    Q = xbc.shape[0]
    d = 1 if rev else 0
    incl, _ = _tri(Q, rev)
    dt = _softplus(win[:, 4 * d:4 * d + 4] + dt_bias[d:d + 1, :])
    a = dt * (-jnp.exp(a_log[d:d + 1, :]))
    acum = _mm(incl.astype(f32), a, _NN, HI)
    tot = jnp.sum(a, axis=0, keepdims=True)
    xs, Bm, Cm = xbc[:, :256], xbc[:, 256:384], xbc[:, 384:512]
    H, W = N_SCAN_HEADS, N_SCAN_HEADS * HEAD
    lanes = lambda x: _cat([jnp.broadcast_to(x[:, h:h + 1], (x.shape[0], HEAD)) for h in range(H)])
    xdt = xs * lanes(dt)
    cb = [_mm(Cm[:, g * HEAD:(g + 1) * HEAD], Bm[:, g * HEAD:(g + 1) * HEAD], _NT) for g in range(2)]
    scores = _cat([cb[h // 2] * _decay(acum[:, h:h + 1], incl) for h in range(H)])
    own = (lax.broadcasted_iota(jnp.int32, (H * Q, W), 0) // Q) == (lax.broadcasted_iota(jnp.int32, (H * Q, W), 1) // HEAD)
    y = _mm(scores, jnp.where(own, jnp.concatenate([xdt] * H, axis=0), 0.0), _NN)
    grp = (lax.broadcasted_iota(jnp.int32, (W, 2 * HEAD), 0) // (2 * HEAD)) == (lax.broadcasted_iota(jnp.int32, (W, 2 * HEAD), 1) // HEAD)
    y = y + _mm(Cm, jnp.where(grp, _cat([S, S]), 0.0), _NT) * jnp.exp(lanes(acum))
    st = _mm(xdt * jnp.exp(lanes(tot - acum)), Bm, _TN)
    first = lax.broadcasted_iota(jnp.int32, (W, HEAD), 0) < 2 * HEAD
    exp_tot = jnp.concatenate([jnp.broadcast_to(jnp.exp(tot[:, h:h + 1]), (HEAD, 1)) for h in range(H)], axis=0)
    return S * exp_tot + jnp.where(first, st[:, :HEAD], st[:, HEAD:]), y


@functools.partial(jax.custom_vjp, nondiff_argnums=(1,))
def _inv_unit_tri(Lm, order):
    n = Lm.shape[0]
    eye = (lax.broadcasted_iota(jnp.int32, (n, n), 0) == lax.broadcasted_iota(jnp.int32, (n, n), 1)).astype(f32)
    P = -Lm
    T = eye + P
    k = 1
    while 2 * k < order:
        P = _mm(P, P, _NN, lax.Precision.HIGH)
        T = T + _mm(T, P, _NN, lax.Precision.HIGH)
        k *= 2
    return T


def _inv_unit_tri_fwd(Lm, order):
    T = _inv_unit_tri(Lm, order)
    return T, T


def _inv_unit_tri_bwd(order, T, dT):
    return (-_mm(_mm(T, dT, _TN, lax.Precision.HIGH), T, _NT, lax.Precision.HIGH),)


_inv_unit_tri.defvjp(_inv_unit_tri_fwd, _inv_unit_tri_bwd)


@jax.custom_vjp
def _inv_known(Lm, T):
    return T


def _inv_known_fwd(Lm, T):
    return T, T


def _inv_known_bwd(T, dT):
    return _inv_unit_tri_bwd(None, T, dT) + (jnp.zeros_like(T),)


_inv_known.defvjp(_inv_known_fwd, _inv_known_bwd)


def _dn_chunk(S, qkv, win, a_log, dt_bias, rev, kept=None):
    Q = qkv.shape[0]
    H, R = N_SCAN_HEADS, N_SCAN_HEADS * qkv.shape[0]
    d = 1 if rev else 0
    beta = jax.nn.sigmoid(win[:, 4 * d:4 * d + 4])
    gl = -jnp.exp(a_log[d:d + 1, :]) * _softplus(win[:, 8 + 4 * d:12 + 4 * d] + dt_bias[d:d + 1, :])
    G = _mm(_tri(Q, rev)[0].astype(f32), gl, _NN, HI)
    tot = jnp.sum(gl, axis=0, keepdims=True)
    rows = lambda x: jnp.concatenate(_heads(x, H), axis=0)
    col = lambda x: jnp.concatenate([x[:, h:h + 1] for h in range(H)], axis=0)
    per_head = lambda x, n: jnp.concatenate([jnp.broadcast_to(x[:, h:h + 1], (n, 1)) for h in range(H)], axis=0)
    r = lax.broadcasted_iota(jnp.int32, (R, R), 0)
    c = lax.broadcasted_iota(jnp.int32, (R, R), 1)
    same = (r // Q) == (c // Q)
    incl = same & ((c >= r) if rev else (c <= r))
    strict = same & ((c > r) if rev else (c < r))
    own = (lax.broadcasted_iota(jnp.int32, (R, H * HEAD), 0) // Q) == (lax.broadcasted_iota(jnp.int32, (R, H * HEAD), 1) // HEAD)
    blk = lambda x: jnp.where(own, _cat([x] * H), 0.0)
    q, k, v = rows(qkv[:, :256]) * (HEAD ** -0.5), rows(qkv[:, 256:512]), rows(qkv[:, 512:])
    b, Gs, tots = col(beta), col(G), per_head(tot, Q)
    dec = _decay(Gs, incl)
    kb = k * b
    Lm = jnp.where(strict, _mm(kb, k, _NT) * dec, 0.0)
    T = _inv_unit_tri(Lm, Q) if kept is None else _inv_known(Lm, blk(kept))
    eG = jnp.exp(Gs)
    uw = _mm(T, _cat([v * b, kb * eG]), _NN)
    vnew = uw[:, :HEAD] - _mm(blk(uw[:, HEAD:]), S, _NN)
    o = _mm(blk(q * eG), S, _NN) + _mm(_mm(q, k, _NT) * dec, vnew, _NN)
    S_next = S * jnp.exp(per_head(tot, HEAD)) + _mm(blk(k * jnp.exp(tots - Gs)), vnew, _TN)
    out = S_next, _cat([o[h * Q:(h + 1) * Q, :] for h in range(H)])
    if kept is None:
        out += (jnp.concatenate([T[h * Q:(h + 1) * Q, h * Q:(h + 1) * Q] for h in range(H)], axis=0),)
    return out


def scan_fwd(chunk_fn, Q, x, p, win_blk, a_log, dt_bias, name, keeps=False):
    L, W = x.shape
    nc = L // Q
    fidx, ridx = (lambda t: t), (lambda t: nc - 1 - t)

    def body(xf_ref, xr_ref, wf_ref, wr_ref, al_ref, db_ref, yf_ref, yr_ref, sf_ref, sr_ref, *rest):
        (sf_scr, sr_scr), keep_refs = rest[-2:], rest[:-2]

        @pl.when(pl.program_id(0) == 0)
        def _():
            sf_scr[...] = jnp.zeros_like(sf_scr)
            sr_scr[...] = jnp.zeros_like(sr_scr)

        for d, (x_ref, w_ref, y_ref, sin_ref, s_scr) in enumerate(((xf_ref, wf_ref, yf_ref, sf_ref, sf_scr),
                                                                   (xr_ref, wr_ref, yr_ref, sr_ref, sr_scr))):
            S = s_scr[...]
            sin_ref[...] = S
            res = chunk_fn(S, x_ref[...], w_ref[...], al_ref[...], db_ref[...], d == 1)
            s_scr[...], y_ref[...] = res[:2]
            if keeps:
                keep_refs[d][...] = res[2]

    row = lambda w, idx, b=0: pl.BlockSpec((Q, w), lambda t: (idx(t), b))
    small = pl.BlockSpec(a_log.shape, lambda t: (0, 0))
    state = lambda idx: pl.BlockSpec((STATE_ROWS, HEAD), lambda t: (idx(t), 0))
    n_state = 4 if keeps else 2
    return pl.pallas_call(
        body, name=name, grid=(nc,),
        in_specs=[row(W, fidx), row(W, ridx), row(HPAD, fidx, win_blk), row(HPAD, ridx, win_blk), small, small],
        out_specs=[row(STATE_ROWS, fidx), row(STATE_ROWS, ridx)] + [state(fidx), state(ridx)] * (n_state // 2),
        out_shape=[jax.ShapeDtypeStruct((L, STATE_ROWS), f32)] * 2 + [jax.ShapeDtypeStruct((nc * STATE_ROWS, HEAD), f32)] * n_state,
        scratch_shapes=[pltpu.VMEM((STATE_ROWS, HEAD), f32)] * 2, compiler_params=_params(("arbitrary",)),
    )(x, x, p, p, a_log, dt_bias)


def scan_bwd(chunk_fn, Q, x, p, win_blk, a_log, dt_bias, s_fwd, s_rev, dy, name, kept=()):
    L, W = x.shape
    nc = L // Q
    fidx, ridx = (lambda t: nc - 1 - t), (lambda t: t)
    nk = len(kept)

    def body(xf_ref, xr_ref, wf_ref, wr_ref, al_ref, db_ref, sf_ref, sr_ref, dyf_ref, dyr_ref, *rest):
        keep_refs = rest[:nk]
        dxf_ref, dxr_ref, dwf_ref, dwr_ref, dal_ref, ddb_ref, dsf_scr, dsr_scr = rest[nk:]

        @pl.when(pl.program_id(0) == 0)
        def _():
            dsf_scr[...] = jnp.zeros_like(dsf_scr)
            dsr_scr[...] = jnp.zeros_like(dsr_scr)
            dal_ref[...] = jnp.zeros_like(dal_ref)
            ddb_ref[...] = jnp.zeros_like(ddb_ref)

        dal_sum, ddb_sum = dal_ref[...], ddb_ref[...]
        for rev, x_ref, w_ref, sin_ref, dy_ref, dx_ref, dw_ref, ds_scr in (
                (False, xf_ref, wf_ref, sf_ref, dyf_ref, dxf_ref, dwf_ref, dsf_scr),
                (True, xr_ref, wr_ref, sr_ref, dyr_ref, dxr_ref, dwr_ref, dsr_scr)):
            extra = (keep_refs[int(rev)][...],) if nk else ()
            _, vjp = jax.vjp(lambda S, xv, wv, al, db, rev=rev, extra=extra: chunk_fn(S, xv, wv, al, db, rev, *extra),
                             sin_ref[...], x_ref[...], w_ref[...], al_ref[...], db_ref[...])
            ds_scr[...], dx_ref[...], dw_ref[...], dal, ddb = vjp((ds_scr[...], dy_ref[...]))
            dal_sum, ddb_sum = dal_sum + dal, ddb_sum + ddb
        dal_ref[...] = dal_sum
        ddb_ref[...] = ddb_sum

    row = lambda w, idx, b=0: pl.BlockSpec((Q, w), lambda t: (idx(t), b))
    small = pl.BlockSpec(a_log.shape, lambda t: (0, 0))
    state = lambda idx: pl.BlockSpec((STATE_ROWS, HEAD), lambda t: (idx(t), 0))
    return pl.pallas_call(
        body, name=name, grid=(nc,),
        in_specs=[row(W, fidx), row(W, ridx), row(HPAD, fidx, win_blk), row(HPAD, ridx, win_blk), small, small,
                  state(fidx), state(ridx), row(STATE_ROWS, fidx), row(STATE_ROWS, ridx)] + [state(fidx), state(ridx)][:nk],
        out_specs=[row(W, fidx), row(W, ridx), row(HPAD, fidx), row(HPAD, ridx), small, small],
        out_shape=[jax.ShapeDtypeStruct((L, W), f32)] * 2 + [jax.ShapeDtypeStruct((L, HPAD), f32)] * 2
        + [jax.ShapeDtypeStruct(a_log.shape, f32)] * 2,
        scratch_shapes=[pltpu.VMEM((STATE_ROWS, HEAD), f32)] * 2, compiler_params=_params(("arbitrary",)),
    )(x, x, p, p, a_log, dt_bias, s_fwd, s_rev, dy, dy, *kept)


def _cat(xs):
    return jnp.concatenate(xs, axis=-1)


def _a_prep(pa, cos, sin, qn, wuq, kvn, wk, wv):
    cq, ckv, kr = pa[:, :256], pa[:, 256:384], pa[:, 384:512]
    q = _rope(_mm(_rms(cq, qn, A_Q_LORA), wuq, _NN), _cat([cos] * 4), _cat([sin] * 4), A_NOPE, A_ROPE // 4)
    kvh = _rms(ckv, kvn)
    k = _mm(kvh, wk, _NN) + _cat([_rope(kr, cos, sin, A_NOPE, A_ROPE // 4)] * 4)
    return q, k, _mm(kvh, wv, _NN)


def _b_prep(pb, cos, sin, qn, kn):
    cos4, sin4 = _cat([cos] * 4), _cat([sin] * 4)
    q = _pad_heads([_rms(h, qn) for h in _heads(pb[:, :256], 4)])
    k = _pad_heads([_rms(h, kn) for h in _heads(pb[:, 256:384], 2) for _ in range(2)])
    v = _pad_heads([h for h in _heads(pb[:, 384:512], 2) for _ in range(2)])
    return _rope(q, cos4, sin4, 0, HEAD // 4), _rope(k, cos4, sin4, 0, HEAD // 4), v


def _c_act(c):
    return _silu(c)


def _d_act(c):
    s = _silu(c)
    return _cat([h * lax.rsqrt(jnp.sum(h * h, axis=-1, keepdims=True) + 1e-6) for h in _heads(s[:, :512], 8)] + [s[:, 512:]])


def _ffn_act(c):
    return _silu(c[:, :FF_BLK]) * c[:, FF_BLK:]


def _mix_post(oa, ob, yc_f, yc_r, xbc, cz, od_f, od_r, dz, a_out, b_out, dskip, c_out, d_out):
    o_a = _rms(_unpad_heads(oa, 4), a_out)
    o_b = _rms(_unpad_heads(ob, 4), b_out)
    skip = _cat([jnp.broadcast_to(dskip[:, h:h + 1], (1, HEAD)) for h in range(4)])
    o_c = _rms((yc_f + yc_r + xbc[:, :256] * skip) * _silu(cz), c_out)
    o_d = _cat([_rms(h, d_out) for h in _heads(od_f + od_r, 4)]) * _silu(dz)
    return _cat([o_a, o_b, o_c, o_d])


def _post_mix(x, y1, g_pm, g_pf):
    x1 = x + _rms(y1, g_pm)
    return x1, _rms(x1, g_pf)


def _post_ffn(x1, y2, g):
    return x1 + _rms(y2, g)


def _pre_bwd(x, g):
    return x, _rms(x, g)


def _layer_fwd(x, W, tabs, n, side=(), late_side=(), late_weights=None):
    mx = MXU_DTYPE
    cos_a, sin_a, cos_b, sin_b = tabs
    full = lambda a: (a, a.shape[1], 0)
    s = {"x": x}
    (s["h"],) = rowwise(_rms, [full(x)], [W["g_pre"]], [(D_MODEL, mx)], n + "pre")
    p = s["p"] = matmul(s["h"], W["w_in"], "nn", f32, n + "in")
    s["qa"], s["ka"], s["va"] = rowwise(_a_prep, [(p, 512, SEG_A // 512), full(cos_a), full(sin_a)],
                                        [W["a_qn"], W["a_wuq"], W["a_kvn"], W["a_wk"], W["a_wv"]], [(512, mx)] * 3, n + "a_prep")
    s["qb"], s["kb"], s["vb"] = rowwise(_b_prep, [(p, 512, SEG_B // 512), full(cos_b), full(sin_b)], [W["b_qn"], W["b_kn"]],
                                        [(512, mx)] * 3, n + "b_prep")
    s["oa"], s["lsea"], side_out = flash_fwd(s["qa"], s["ka"], s["va"], 4, 4, (A_NOPE + A_ROPE) ** -0.5, n + "a_attn", side)
    s["ob"], s["lseb"], late_out = flash_fwd(s["qb"], s["kb"], s["vb"], 4, 4, HEAD ** -0.5, n + "b_attn", late_side)
    if late_weights is not None:
        W = {**W, **late_weights(late_out)}
    xbc = s["xbc"] = conv_fwd(_c_act, p, SEG_CX // 512, 1, 512, W["c_cw"], W["c_cb"], 512, f32, n + "c_conv")
    s["yc_f"], s["yc_r"], s["sc_f"], s["sc_r"] = scan_fwd(_ssd_chunk, min(SSD_CHUNK, xbc.shape[0]), xbc, p, SEG_CDT // HPAD, W["c_alog"], W["c_dtb"], n + "c_ssd")
    qkv = s["qkv"] = conv_fwd(_d_act, p, SEG_DX // 768, 1, 768, W["d_cw"], W["d_cb"], 768, f32, n + "d_conv")
    s["od_f"], s["od_r"], s["sd_f"], s["sd_r"], s["td_f"], s["td_r"] = scan_fwd(
        _dn_chunk, DN_CHUNK, qkv, p, SEG_DAB // HPAD, W["d_alog"], W["d_dtb"], n + "d_dn", keeps=True)
    (s["omix"],) = rowwise(_mix_post, _mix_rows(s), _mix_params(W), [(D_MODEL, mx)], n + "mix_post")
    s["y1"] = matmul(s["omix"], W["w_out"], "nn", f32, n + "out")
    s["x1"], s["h2"] = rowwise(_post_mix, [full(x), full(s["y1"])], [W["g_pm"], W["g_pf"]], [(D_MODEL, f32), (D_MODEL, mx)], n + "post_mix")
    s["u"] = matmul(s["h2"], W["f_win"], "nn", f32, n + "f_in")
    s["a"] = conv_fwd(_ffn_act, s["u"], 0, D_FF // FF_BLK, 2 * FF_BLK, W["f_cw"], W["f_cb"], FF_BLK, mx, n + "f_conv", tm=512)
    s["y2"] = matmul(s["a"], W["f_wout"], "nn", f32, n + "f_out")
    (x2,) = rowwise(_post_ffn, [full(s["x1"]), full(s["y2"])], [W["g_po"]], [(D_MODEL, f32)], n + "post_ffn")
    return x2, s, side_out, W


def _mix_rows(s):
    p = s["p"]
    return [(s["oa"], 512, 0), (s["ob"], 512, 0), (s["yc_f"], 256, 0), (s["yc_r"], 256, 0), (s["xbc"], 512, 0),
            (p, 256, SEG_CZ // 256), (s["od_f"], 256, 0), (s["od_r"], 256, 0), (p, 256, SEG_DZ // 256)]


def _mix_params(W):
    return [W["a_out"], W["b_out"], W["c_dskip"], W["c_out"], W["d_out"]]


def _layer_bwd(s, dx2, W, tabs, n, side=(), late_side=None):
    cos_a, sin_a, cos_b, sin_b = tabs
    full = lambda a: (a, a.shape[1], 0)
    p, g = s["p"], {}
    (dx1, dy2), (g["g_po"],) = rowwise_vjp(_post_ffn, [full(s["x1"]), full(s["y2"])], [W["g_po"]], [full(dx2)], n + "post_ffn_b", [0, 1], [0])
    da = matmul(dy2, W["f_wout"], "nt", f32, n + "f_out_dx")
    g["f_wout"] = matmul(s["a"], dy2, "tn", f32, n + "f_out_dw")
    du, g["f_cw"], g["f_cb"] = conv_bwd(_ffn_act, s["u"], 0, D_FF // FF_BLK, 2 * FF_BLK, W["f_cw"], W["f_cb"], [da], FF_BLK, n + "f_conv_b", tm=512)
    dh2 = matmul(du, W["f_win"], "nt", f32, n + "f_in_dx")
    g["f_win"] = matmul(s["h2"], du, "tn", f32, n + "f_in_dw")
    (dx, dy1), (g["g_pm"], g["g_pf"]) = rowwise_vjp(_post_mix, [full(s["x"]), full(s["y1"])], [W["g_pm"], W["g_pf"]],
                                                   [full(dx1), full(dh2)], n + "post_mix_b", [0, 1], [0, 1])
    domix = matmul(dy1, W["w_out"], "nt", f32, n + "out_dx")
    g["w_out"] = matmul(s["omix"], dy1, "tn", f32, n + "out_dw")
    (doa, dob, dyc, dxbc_skip, dcz, dod, ddz), (g["a_out"], g["b_out"], g["c_dskip"], g["c_out"], g["d_out"]) = rowwise_vjp(
        _mix_post, _mix_rows(s), _mix_params(W), [full(domix)], n + "mix_post_b", [0, 1, 2, 4, 5, 6, 8], list(range(5)))
    dqkv_f, dqkv_r, ddab_f, ddab_r, g["d_alog"], g["d_dtb"] = scan_bwd(
        _dn_chunk, DN_CHUNK, s["qkv"], p, SEG_DAB // HPAD, W["d_alog"], W["d_dtb"], s["sd_f"], s["sd_r"], dod, n + "d_dn_b",
        kept=(s["td_f"], s["td_r"]))
    ddx, g["d_cw"], _ = conv_bwd(_d_act, p, SEG_DX // 768, 1, 768, W["d_cw"], W["d_cb"], [dqkv_f, dqkv_r], 768, n + "d_conv_b")
    dxbc_f, dxbc_r, dcdt_f, dcdt_r, g["c_alog"], g["c_dtb"] = scan_bwd(
        _ssd_chunk, min(SSD_CHUNK, p.shape[0]), s["xbc"], p, SEG_CDT // HPAD, W["c_alog"], W["c_dtb"], s["sc_f"], s["sc_r"], dyc, n + "c_ssd_b")
    dcx, g["c_cw"], g["c_cb"] = conv_bwd(_c_act, p, SEG_CX // 512, 1, 512, W["c_cw"], W["c_cb"], [dxbc_f, dxbc_r, dxbc_skip], 512,
                                         n + "c_conv_b")
    (dsmall,) = rowwise(lambda a, b, c, d: _cat([a + b, c + d]), [full(dcdt_f), full(dcdt_r), full(ddab_f), full(ddab_r)], [],
                        [(2 * HPAD, f32)], n + "dwin_sum")
    dqa, dka, dva, side_out = flash_bwd(s["qa"], s["ka"], s["va"], s["oa"], s["lsea"], doa, 4, 4, (A_NOPE + A_ROPE) ** -0.5,
                                        n + "a_attn_b", side)
    dqb, dkb, dvb, late_out = flash_bwd(s["qb"], s["kb"], s["vb"], s["ob"], s["lseb"], dob, 4, 4, HEAD ** -0.5, n + "b_attn_b",
                                        late_side(g) if late_side is not None else ())
    (dpa,), (g["a_qn"], g["a_wuq"], g["a_kvn"], g["a_wk"], g["a_wv"]) = rowwise_vjp(
        _a_prep, [(p, 512, SEG_A // 512), full(cos_a), full(sin_a)], [W["a_qn"], W["a_wuq"], W["a_kvn"], W["a_wk"], W["a_wv"]],
        [full(dqa), full(dka), full(dva)], n + "a_prep_b", [0], list(range(5)))
    (dpb,), (g["b_qn"], g["b_kn"]) = rowwise_vjp(_b_prep, [(p, 512, SEG_B // 512), full(cos_b), full(sin_b)], [W["b_qn"], W["b_kn"]],
                                               [full(dqb), full(dkb), full(dvb)], n + "b_prep_b", [0], [0, 1])
    dp = jnp.concatenate([dpa, dpb, dcx, ddx, dcz, ddz, dsmall], axis=1)
    dh = matmul(dp, W["w_in"], "nt", f32, n + "in_dx")
    g["w_in"] = matmul(s["h"], dp, "tn", f32, n + "in_dw")
    (dx0,), (g["g_pre"],) = rowwise_vjp(_pre_bwd, [full(s["x"])], [W["g_pre"]], [full(dx), full(dh)], n + "pre_b", [0], [0])
    return dx0, g, side_out, late_out


def loss_and_grad(y, target, name):
    L, D = y.shape
    tm = _tile(L, 512)

    def body(y_ref, t_ref, loss_ref, dy_ref):
        e = y_ref[...] - t_ref[...]
        dy_ref[...] = e * (1.0 / D)
        part = 0.5 * jnp.sum(jnp.sum(e * e, axis=1, keepdims=True) * (1.0 / D), axis=0, keepdims=True)

        @pl.when(pl.program_id(0) == 0)
        def _():
            loss_ref[...] = part

        @pl.when(pl.program_id(0) > 0)
        def _():
            loss_ref[...] += part

    row = pl.BlockSpec((tm, D), lambda i: (i, 0))
    return pl.pallas_call(
        body, name=name, grid=(L // tm,), in_specs=[row, row], out_specs=[pl.BlockSpec((1, 1), lambda i: (0, 0)), row],
        out_shape=[jax.ShapeDtypeStruct((1, 1), f32), jax.ShapeDtypeStruct((L, D), f32)], compiler_params=_params(("arbitrary",)),
    )(y, target)


_IN_SEGS = [(0, 192), (256, 128), (384 + A_NOPE, 32), (512, 256), (768, 128), (896, 128), (SEG_CZ, 256), (SEG_CX, 512),
            (SEG_CDT, 8), (SEG_DX, 768), (SEG_DZ, 256), (SEG_DAB, 16)]


def _pad_in(w):
    src, pieces = 0, {}
    for off, wd in _IN_SEGS:
        pieces[off] = w[..., src:src + wd]
        src += wd
    out, pos = [], 0
    for off in sorted(pieces):
        if off > pos:
            out.append(jnp.zeros(w.shape[:-1] + (off - pos,), w.dtype))
        out.append(pieces[off])
        pos = off + pieces[off].shape[-1]
    out.append(jnp.zeros(w.shape[:-1] + (PIN - pos,), w.dtype))
    return jnp.concatenate(out, axis=-1)


def _unpad_in(wp):
    return jnp.concatenate([wp[..., off:off + wd] for off, wd in _IN_SEGS], axis=-1)


def _pad_last(a, n):
    return jnp.pad(a, [(0, 0)] * (a.ndim - 1) + [(0, n - a.shape[-1])])


def _ff_interleave(w):
    nb = D_FF // FF_BLK
    if w.size <= 4 * D_FF * 2:
        lead = w.shape[:-1]
        return jnp.stack([w[..., :D_FF].reshape(lead + (nb, FF_BLK)), w[..., D_FF:].reshape(lead + (nb, FF_BLK))],
                         axis=-2).reshape(lead + (2 * D_FF,))
    return jnp.concatenate([w[..., off + j * FF_BLK:off + (j + 1) * FF_BLK] for j in range(nb) for off in (0, D_FF)], axis=-1)


def _ff_deinterleave(w):
    nb = D_FF // FF_BLK
    if w.size <= 4 * D_FF * 2:
        lead = w.shape[:-1]
        t = w.reshape(lead + (nb, 2, FF_BLK))
        return jnp.concatenate([t[..., 0, :].reshape(lead + (D_FF,)), t[..., 1, :].reshape(lead + (D_FF,))], axis=-1)
    return jnp.concatenate([w[..., (2 * j + half) * FF_BLK:(2 * j + half + 1) * FF_BLK] for half in (0, 1) for j in range(nb)], axis=-1)


def _row(v):
    return v.reshape(1, -1)


_LATE = ["w_out", "f_w_in", "f_w_out"]


def _late_kernel_weights(P, l, wdt):
    return {"w_out": P["w_out"][l].astype(wdt), "f_win": _ff_interleave(P["f_w_in"][l]).astype(wdt), "f_wout": P["f_w_out"][l].astype(wdt)}


def _kernel_weights(P, l, wdt):
    late = {} if P["w_out"][l] is None else _late_kernel_weights(P, l, wdt)
    uq = P["a_w_uq"][l].reshape(A_Q_LORA, 4, A_NOPE + A_ROPE)
    ukv = P["a_w_ukv"][l].reshape(A_KV_LORA, 4, 2 * HEAD)
    z = jnp.zeros((A_KV_LORA, 4, HEAD), ukv.dtype)
    return {
        "g_pre": _row(P["pre_mix_norm"][l]), "w_in": _pad_in(P["w_in"][l]).astype(wdt),
        "a_qn": _pad_last(_row(P["a_q_norm"][l]), 256),
        "a_wuq": jnp.pad(_pad_last(uq, HPAD).reshape(A_Q_LORA, 4 * HPAD), ((0, 256 - A_Q_LORA), (0, 0))).astype(wdt),
        "a_kvn": _row(P["a_kv_norm"][l]),
        "a_wk": jnp.concatenate([ukv[..., :HEAD], z], axis=-1).reshape(A_KV_LORA, 4 * HPAD).astype(wdt),
        "a_wv": jnp.concatenate([ukv[..., HEAD:], z], axis=-1).reshape(A_KV_LORA, 4 * HPAD).astype(wdt),
        "a_out": _row(P["a_out_norm"][l]), "b_qn": _row(P["b_q_norm"][l]), "b_kn": _row(P["b_k_norm"][l]), "b_out": _row(P["b_out_norm"][l]),
        "c_cw": P["c_conv_w"][l], "c_cb": _row(P["c_conv_b"][l]), "c_alog": P["c_a_log"][l], "c_dtb": P["c_dt_bias"][l],
        "c_dskip": _row(P["c_d_skip"][l]), "c_out": _row(P["c_out_norm"][l]),
        "d_cw": P["d_conv_w"][l], "d_cb": jnp.zeros((1, 768), f32), "d_alog": P["d_a_log"][l], "d_dtb": P["d_dt_bias"][l],
        "d_out": _row(P["d_out_norm"][l]), "g_pm": _row(P["post_mix_norm"][l]), "g_pf": _row(P["pre_ffn_norm"][l]),
        "f_cw": _ff_interleave(P["f_conv_w"][l]), "f_cb": _row(_ff_interleave(P["f_conv_b"][l])),
        "g_po": _row(P["post_ffn_norm"][l]), **late,
    }


def _reference_grads(g):
    uq = g["a_wuq"][:A_Q_LORA].reshape(A_Q_LORA, 4, HPAD)[..., :A_NOPE + A_ROPE].reshape(A_Q_LORA, 4 * (A_NOPE + A_ROPE))
    wk = g["a_wk"].reshape(A_KV_LORA, 4, HPAD)[..., :HEAD]
    wv = g["a_wv"].reshape(A_KV_LORA, 4, HPAD)[..., :HEAD]
    return {
        "pre_mix_norm": g["g_pre"][0], "w_in": _unpad_in(g["w_in"]), "a_q_norm": g["a_qn"][0, :A_Q_LORA], "a_w_uq": uq,
        "a_kv_norm": g["a_kvn"][0], "a_w_ukv": jnp.concatenate([wk, wv], axis=-1).reshape(A_KV_LORA, 8 * HEAD),
        "a_out_norm": g["a_out"][0], "b_q_norm": g["b_qn"][0], "b_k_norm": g["b_kn"][0], "b_out_norm": g["b_out"][0],
        "c_conv_w": g["c_cw"], "c_conv_b": g["c_cb"][0], "c_a_log": g["c_alog"], "c_dt_bias": g["c_dtb"], "c_d_skip": g["c_dskip"][0],
        "c_out_norm": g["c_out"][0], "d_conv_w": g["d_cw"], "d_a_log": g["d_alog"], "d_dt_bias": g["d_dtb"], "d_out_norm": g["d_out"][0],
        "w_out": g["w_out"], "post_mix_norm": g["g_pm"][0], "pre_ffn_norm": g["g_pf"][0], "f_w_in": _ff_deinterleave(g["f_win"]),
        "f_conv_w": _ff_deinterleave(g["f_cw"]), "f_conv_b": _ff_deinterleave(g["f_cb"][0]), "f_w_out": g["f_wout"],
        "post_ffn_norm": g["g_po"][0],
    }


def _rope_tables(L):
    def tables(rot):
        rows = L // GRID_W
        row = jnp.repeat(jnp.arange(rows), GRID_W).astype(f32)
        col = jnp.tile(jnp.arange(GRID_W), rows).astype(f32)
        sec = rot // 2
        inv = ROPE_BASE ** (-jnp.arange(0, sec, 2, dtype=f32) / sec)
        ang = jnp.concatenate([row[:, None] * inv] * 2 + [col[:, None] * inv] * 2, axis=-1)
        return jnp.cos(ang), jnp.sin(ang)

    ca, sa = tables(A_ROPE)
    cb, sb = tables(HEAD)
    one, zero = jnp.ones, jnp.zeros
    return (jnp.concatenate([one((L, A_NOPE), f32), ca, one((L, HPAD - A_NOPE - A_ROPE), f32)], axis=1),
            jnp.concatenate([zero((L, A_NOPE), f32), sa, zero((L, HPAD - A_NOPE - A_ROPE), f32)], axis=1),
            jnp.concatenate([cb, one((L, HPAD - HEAD), f32)], axis=1), jnp.concatenate([sb, zero((L, HPAD - HEAD), f32)], axis=1))


def exchange(items, name):
    n = len(items)

    def body(*refs):
        start, wait = _exchange_ops([g for _, g in items], refs[:n], refs[n:2 * n], *refs[2 * n:])
        start()
        wait()

    ex = _exchange_io(items)
    res = pl.pallas_call(body, name=name, in_specs=ex.specs, out_specs=ex.specs, out_shape=ex.out_shape,
                         scratch_shapes=ex.scratch)(*ex.args)
    return list(res)


def _exchange_ops(modes, src_refs, out_refs, send_sems, recv_sems, local_sems):
    n = len(modes)
    x, y, c = lax.axis_index("x"), lax.axis_index("y"), lax.axis_index("c")
    me = 4 * x + 2 * y + c

    def copy(a, k, arriving):
        px, py, pc = (1 - x if k & 4 else x), (1 - y if k & 2 else y), (1 - c if k & 1 else c)
        pid = 4 * px + 2 * py + pc
        sem = a * (N_DEV - 1) + k - 1
        return pltpu.make_async_remote_copy(
            src_ref=src_refs[a] if modes[a] else src_refs[a].at[pid], dst_ref=out_refs[a].at[pid if arriving else me],
            send_sem=send_sems.at[sem], recv_sem=recv_sems.at[sem], device_id=(px, py, pc), device_id_type=pl.DeviceIdType.MESH)

    def local(a):
        return pltpu.make_async_copy(src_refs[a] if modes[a] else src_refs[a].at[me], out_refs[a].at[me], local_sems.at[a])

    pairs = [(a, k) for k in range(1, N_DEV) for a in range(n)]

    def start():
        for a, k in pairs:
            copy(a, k, False).start()
        for a in range(n):
            local(a).start()

    def wait():
        for a, k in pairs:
            copy(a, k, False).wait_send()
        for a, k in pairs:
            copy(a, k, True).wait_recv()
        for a in range(n):
            local(a).wait()

    return start, wait


class _ExchangeIO:
    def __init__(self, items):
        n = len(items)
        self.args = [s for s, _ in items]
        self.specs = [pl.BlockSpec(memory_space=pl.ANY)] * n
        self.out_shape = [jax.ShapeDtypeStruct((N_DEV,) + tuple(s.shape if g else s.shape[1:]), s.dtype) for s, g in items]
        self.scratch = [pltpu.SemaphoreType.DMA((n * (N_DEV - 1),)), pltpu.SemaphoreType.DMA((n * (N_DEV - 1),)),
                        pltpu.SemaphoreType.DMA((n,))] if n else []


def _exchange_io(items):
    return _ExchangeIO(list(items))


def _with_exchange(body, n_in, n_out, side, grid):
    ex = _exchange_io(side)
    n = len(ex.args)
    if not n:
        return body, ex
    modes = [g for _, g in side]

    def wrapped(*refs):
        ins, src = refs[:n_in], refs[n_in:n_in + n]
        outs, dst = refs[n_in + n:n_in + n + n_out], refs[n_in + n + n_out:n_in + 2 * n + n_out]
        sems = refs[n_in + 2 * n + n_out:]
        start, wait = _exchange_ops(modes, src, dst, *sems)
        ids = [pl.program_id(a) for a in range(len(grid))]
        first, last = ids[0] == 0, ids[0] == grid[0] - 1
        for a in range(1, len(grid)):
            first, last = first & (ids[a] == 0), last & (ids[a] == grid[a] - 1)
        pl.when(first)(start)
        body(*ins, *outs)
        pl.when(last)(wait)

    return wrapped, ex


LANES = 1024


def adamw(gstacks, w, m, v, name):
    D, R, C = w.shape
    S = gstacks[0].shape[0]
    tr = R if R <= 512 else _tile(R, 256)

    def body(*refs):
        g_refs = refs[:D]
        w_ref, m_ref, v_ref, go_ref, d_ref, mo_ref, vo_ref = refs[D:]
        layer = pl.program_id(0)
        g = None
        for d, g_ref in enumerate(g_refs):
            gd = g_ref[0].astype(f32)
            for sl in range(1, S):
                gd = gd + g_ref[sl].astype(f32)
            g = gd if g is None else jnp.where(layer == d, gd, g)
        g = g[None]
        m_new = ADAM_B1 * m_ref[...] + (1.0 - ADAM_B1) * g
        v_new = ADAM_B2 * v_ref[...] + (1.0 - ADAM_B2) * jnp.square(g)
        m_hat = m_new / (1.0 - ADAM_B1 ** ADAM_STEP)
        v_hat = v_new / (1.0 - ADAM_B2 ** ADAM_STEP)
        go_ref[...] = g
        d_ref[...] = -ADAM_LR * (m_hat / (jnp.sqrt(v_hat) + ADAM_EPS) + ADAM_WD * w_ref[...])
        mo_ref[...] = m_new
        vo_ref[...] = v_new

    row = pl.BlockSpec((1, tr, C), lambda l, i: (l, i, 0))
    g_specs = [pl.BlockSpec((S, tr, C), lambda l, i, d=d: (0, jnp.where(l == d, i, 0), 0)) for d in range(D)]
    return pl.pallas_call(
        body, name=name, grid=(D, R // tr), in_specs=g_specs + [row, row, row],
        out_specs=[row] * 4, out_shape=[jax.ShapeDtypeStruct((D, R, C), f32)] * 4, compiler_params=_params(("parallel", "parallel")),
    )(*gstacks, w, m, v)


def sum_slots(gstack, name):
    S, R, _ = gstack.shape

    def body(g_ref, o_ref):
        g = g_ref[0]
        for sl in range(1, S):
            g = g + g_ref[sl]
        o_ref[...] = g

    return pl.pallas_call(body, name=name, out_shape=jax.ShapeDtypeStruct((R, LANES), f32), compiler_params=_params())(gstack)


def _pack(parts, rows, dtype=f32):
    flat = jnp.concatenate([q.reshape(-1).astype(dtype) for q in parts])
    return jnp.pad(flat, (0, rows * LANES - flat.shape[0])).reshape(rows, LANES)


def _unpack(buf, shapes):
    lead = buf.shape[:-2]
    flat = buf.reshape(lead + (-1,))
    out, off = [], 0
    for shp in shapes:
        n = math.prod(shp)
        out.append(flat[..., off:off + n].reshape(lead + tuple(shp)))
        off += n
    return out


_WEIGHTS = ["pre_mix_norm", "w_in", "a_q_norm", "a_w_uq", "a_kv_norm", "a_w_ukv", "a_out_norm", "b_q_norm", "b_k_norm", "b_out_norm",
            "c_conv_w", "c_conv_b", "c_a_log", "c_dt_bias", "c_d_skip", "c_out_norm", "d_conv_w", "d_a_log", "d_dt_bias", "d_out_norm",
            "w_out", "post_mix_norm", "pre_ffn_norm", "f_w_in", "f_conv_w", "f_conv_b", "f_w_out", "post_ffn_norm"]
_BIG = {"w_in": 1, "a_w_uq": 1, "a_w_ukv": 1, "w_out": 0, "f_w_in": 1, "f_w_out": 0}
_CONV = ["c_conv_w", "d_conv_w", "f_conv_w"]
_REP = [n for n in _WEIGHTS if n not in _BIG and n not in _CONV]
SMALLG_ROWS, SMALLW_ROWS = 64, 32


def _join(blocks, axis):
    return jnp.concatenate([blocks[d] for d in range(N_DEV)], axis=axis)


def _split(full, axis):
    n = full.shape[axis] // N_DEV
    return jnp.stack([lax.slice_in_dim(full, d * n, (d + 1) * n, axis=axis) for d in range(N_DEV)])


def kernel(x, pre_mix_norm, w_in, a_q_norm, a_w_uq, a_kv_norm, a_w_ukv, a_out_norm, b_q_norm, b_k_norm, b_out_norm, c_conv_w, c_conv_b, c_a_log, c_dt_bias, c_d_skip, c_out_norm, d_conv_w, d_a_log, d_dt_bias, d_out_norm, w_out, post_mix_norm, pre_ffn_norm, f_w_in, f_conv_w, f_conv_b, f_w_out, post_ffn_norm, loss_target, m_pre_mix_norm, m_w_in, m_a_q_norm, m_a_w_uq, m_a_kv_norm, m_a_w_ukv, m_a_out_norm, m_b_q_norm, m_b_k_norm, m_b_out_norm, m_c_conv_w, m_c_conv_b, m_c_a_log, m_c_dt_bias, m_c_d_skip, m_c_out_norm, m_d_conv_w, m_d_a_log, m_d_dt_bias, m_d_out_norm, m_w_out, m_post_mix_norm, m_pre_ffn_norm, m_f_w_in, m_f_conv_w, m_f_conv_b, m_f_w_out, m_post_ffn_norm, v_pre_mix_norm, v_w_in, v_a_q_norm, v_a_w_uq, v_a_kv_norm, v_a_w_ukv, v_a_out_norm, v_b_q_norm, v_b_k_norm, v_b_out_norm, v_c_conv_w, v_c_conv_b, v_c_a_log, v_c_dt_bias, v_c_d_skip, v_c_out_norm, v_d_conv_w, v_d_a_log, v_d_dt_bias, v_d_out_norm, v_w_out, v_post_mix_norm, v_pre_ffn_norm, v_f_w_in, v_f_conv_w, v_f_conv_b, v_f_w_out, v_post_ffn_norm):
    given = dict(locals())
    w = {n: given[n] for n in _WEIGHTS}
    mom = {n: given["m_" + n] for n in _WEIGHTS}
    var = {n: given["v_" + n] for n in _WEIGHTS}
    me = 4 * lax.axis_index("x") + 2 * lax.axis_index("y") + lax.axis_index("c")
    layered = lambda names: [(l, n) for l in range(DEPTH) for n in names]

    sharded = list(_BIG) + _CONV
    early = [n for n in sharded if n not in _LATE]
    weight_items = lambda l, names: [(w[n][l].astype(MXU_DTYPE if n in _BIG else f32), True) for n in names]
    grad_items = lambda g, names: [(_split(g[n], _BIG[n]).astype(WIRE_DTYPE), False) for n in names]
    P = {n: [w[n][l] for l in range(DEPTH)] for n in _REP}
    P.update({n: [None] * DEPTH for n in sharded})

    def place(l, names, gathered):
        for n, got in zip(names, gathered):
            P[n][l] = _join(got, _BIG.get(n, 1))

    def late_weights_l0(gathered):
        place(0, _LATE, gathered)
        return _late_kernel_weights(P, 0, MXU_DTYPE)

    tabs = _rope_tables(x.shape[1])
    place(0, early, exchange(weight_items(0, early), "gather_weights"))
    x1, s0, got, W0 = _layer_fwd(x[0], _kernel_weights(P, 0, MXU_DTYPE), tabs, "l0_", weight_items(1, sharded),
                                 weight_items(0, _LATE), late_weights_l0)
    place(1, sharded, got)
    x2, s1, _, W1 = _layer_fwd(x1, _kernel_weights(P, 1, MXU_DTYPE), tabs, "l1_")
    loss, dx = loss_and_grad(x2, loss_target[0], "loss")
    loss = lax.psum(loss[0, 0], ("x", "y", "c"))

    def late_grads_l0(g):
        return grad_items({"w_out": g["w_out"], "f_w_in": _ff_deinterleave(g["f_win"]), "f_w_out": g["f_wout"]}, _LATE)

    dx, g1, _, _ = _layer_bwd(s1, dx, W1, tabs, "l1_")
    g1 = _reference_grads(g1)
    dx, g0, recv1, recv0_late = _layer_bwd(s0, dx, W0, tabs, "l0_", grad_items(g1, _BIG), late_grads_l0)
    grads = [_reference_grads(g0), g1]
    small = _REP + _CONV
    full_shapes = [grads[l][n].shape for l, n in layered(small)]
    first = [n for n in _BIG if n not in _LATE]
    *recv0_first, gsmall = exchange(grad_items(grads[0], first) + [(_pack([grads[l][n] for l, n in layered(small)], SMALLG_ROWS), True)],
                                    "exchange_grads")
    recv0 = dict(zip(first + _LATE, recv0_first + recv0_late))

    out = {}
    for n, got1 in zip(_BIG, recv1):
        out[n] = adamw([recv0[n], got1], w[n], mom[n], var[n], "adamw_" + n)
    gsum = dict(zip(layered(small), _unpack(sum_slots(gsmall, "sum_small_grads"), full_shapes)))
    for l, n in layered(_CONV):
        cols = w[n].shape[2]
        gsum[(l, n)] = lax.dynamic_slice_in_dim(gsum[(l, n)], me * cols, cols, axis=1)
    held_shapes = [w[n].shape[1:] for _, n in layered(small)]
    pack_small = lambda d: _pack([d[n][l] for l, n in layered(small)], SMALLW_ROWS)[None]
    res_small = adamw([_pack([gsum[k] for k in layered(small)], SMALLW_ROWS)[None]], pack_small(w), pack_small(mom),
                      pack_small(var), "adamw_small")
    per_kind = [dict(zip(layered(small), _unpack(buf[0], held_shapes))) for buf in res_small]
    for n in small:
        out[n] = [jnp.stack([per_kind[kind][(l, n)] for l in range(DEPTH)]) for kind in range(4)]
    return (loss, dx[None], *[out[n][kind] for kind in range(4) for n in _WEIGHTS])
```

```python
import functools
import math

import jax
import jax.numpy as jnp
from jax import lax
from jax.experimental import pallas as pl
from jax.experimental.pallas import tpu as pltpu

f32 = jnp.float32
MXU_DTYPE = jnp.bfloat16
WIRE_DTYPE = jnp.bfloat16
HI = lax.Precision.HIGHEST

D_MODEL = 1024
DEPTH = 2
GRID_W = 64
ROPE_BASE = 10000.0
EPS = 1e-6
A_Q_LORA, A_KV_LORA, A_ROPE, A_NOPE = 192, 128, 32, 64
SSD_CHUNK = 256
DN_CHUNK = 64
HEAD = 64
HPAD = 128
D_FF = 2816
FF_BLK = 256
N_DEV = 8
ADAM_LR, ADAM_B1, ADAM_B2, ADAM_EPS, ADAM_WD, ADAM_STEP = 0.001, 0.9, 0.999, 1e-08, 0.01, 10

V7X_VMEM_BYTES = 64 * 2 ** 20
VMEM_LIMIT = (V7X_VMEM_BYTES * 3) // 4

PIN = 3072
SEG_A, SEG_B, SEG_CX, SEG_DX, SEG_CZ, SEG_DZ, SEG_CDT, SEG_DAB = 0, 512, 1024, 1536, 2304, 2560, 2816, 2944


def _tile(n, pref):
    for t in (512, 256, 128, 64, 32, 16, 8):
        if t <= pref and n % t == 0:
            return t
    raise ValueError(f"no tile for {n}")


LANE = 128


def _tile_div(n, pref):
    if n <= pref:
        return n
    return max(d for d in range(LANE, pref + 1, LANE) if n % d == 0)


def _params(sem=None):
    return pltpu.CompilerParams(vmem_limit_bytes=VMEM_LIMIT, dimension_semantics=sem)


def _tup(r):
    return tuple(r) if isinstance(r, (tuple, list)) else (r,)


def matmul(a, b, form, out_dtype, name):
    if form == "nn":
        (M, K), N = a.shape, b.shape[1]
    elif form == "nt":
        (M, K), N = a.shape, b.shape[0]
    else:
        (K, M), N = a.shape, b.shape[1]
    if form == "tn":
        tm, tn, tk = (512 if M % 512 == 0 else _tile_div(M, 1408)), _tile_div(N, 1408), _tile_div(K, 2048)
    else:
        tm, tn, tk = _tile_div(M, 512), _tile_div(N, 2816), _tile_div(K, 5632 if jnp.dtype(a.dtype).itemsize == 2 else 2816)
    nk = K // tk
    dims = {"nn": ((1,), (0,)), "nt": ((1,), (1,)), "tn": ((0,), (0,))}[form]

    def body(a_ref, b_ref, o_ref, *acc):
        part = lax.dot_general(a_ref[...].astype(MXU_DTYPE), b_ref[...].astype(MXU_DTYPE), (dims, ((), ())),
                               preferred_element_type=f32)
        if nk == 1:
            o_ref[...] = part.astype(out_dtype)
            return
        (acc_ref,) = acc
        k = pl.program_id(2)

        @pl.when(k == 0)
        def _():
            acc_ref[...] = part

        @pl.when((k > 0) & (k < nk - 1))
        def _():
            acc_ref[...] += part

        @pl.when(k == nk - 1)
        def _():
            o_ref[...] = (acc_ref[...] + part).astype(out_dtype)

    a_spec = pl.BlockSpec((tk, tm), lambda i, j, k: (k, i)) if form == "tn" else pl.BlockSpec((tm, tk), lambda i, j, k: (i, k))
    b_spec = pl.BlockSpec((tn, tk), lambda i, j, k: (j, k)) if form == "nt" else pl.BlockSpec((tk, tn), lambda i, j, k: (k, j))
    return pl.pallas_call(
        body, name=name, grid=(M // tm, N // tn, nk), in_specs=[a_spec, b_spec],
        out_specs=pl.BlockSpec((tm, tn), lambda i, j, k: (i, j)), out_shape=jax.ShapeDtypeStruct((M, N), out_dtype),
        scratch_shapes=[pltpu.VMEM((tm, tn), f32)] if nk > 1 else [], compiler_params=_params(("parallel", "parallel", "arbitrary")),
    )(a, b)


def _row_specs(rows, tm):
    return [pl.BlockSpec((tm, w), lambda i, b=b: (i, b)) for (_, w, b) in rows]


def _full_specs(params):
    return [pl.BlockSpec(p.shape, lambda i: (0, 0)) for p in params]


def rowwise(fn, rows, params, outs, name, tm=512):
    L = rows[0][0].shape[0]
    tm = _tile(L, tm)
    n_in = len(rows) + len(params)

    def body(*refs):
        res = _tup(fn(*[r[...].astype(f32) for r in refs[:n_in]]))
        for o_ref, r in zip(refs[n_in:], res, strict=True):
            o_ref[...] = r.astype(o_ref.dtype)

    res = pl.pallas_call(
        body, name=name, grid=(L // tm,), in_specs=_row_specs(rows, tm) + _full_specs(params),
        out_specs=[pl.BlockSpec((tm, w), lambda i: (i, 0)) for (w, _) in outs],
        out_shape=[jax.ShapeDtypeStruct((L, w), dt) for (w, dt) in outs], compiler_params=_params(("parallel",)),
    )(*[r[0] for r in rows], *params)
    return list(res)


def rowwise_vjp(fn, rows, params, cts, name, row_grads, param_grads, tm=512, row_dtypes=None):
    L = rows[0][0].shape[0]
    tm = _tile(L, tm)
    nr, npar, nct = len(rows), len(params), len(cts)
    row_dtypes = [f32] * len(row_grads) if row_dtypes is None else row_dtypes

    def body(*refs):
        i = pl.program_id(0)
        rv = [r[...].astype(f32) for r in refs[:nr]]
        pv = [r[...].astype(f32) for r in refs[nr:nr + npar]]
        cv = tuple(r[...].astype(f32) for r in refs[nr + npar:nr + npar + nct])
        out_refs = refs[nr + npar + nct:]

        def g(*diff):
            rr, pp = list(rv), list(pv)
            for k, v in zip(row_grads, diff[:len(row_grads)]):
                rr[k] = v
            for k, v in zip(param_grads, diff[len(row_grads):]):
                pp[k] = v
            return _tup(fn(*rr, *pp))

        _, vjp = jax.vjp(g, *[rv[k] for k in row_grads], *[pv[k] for k in param_grads])
        grads = vjp(cv)
        for o_ref, gval in zip(out_refs[:len(row_grads)], grads[:len(row_grads)]):
            o_ref[...] = gval.astype(o_ref.dtype)
        for o_ref, gval in zip(out_refs[len(row_grads):], grads[len(row_grads):]):
            @pl.when(i == 0)
            def _(o_ref=o_ref, gval=gval):
                o_ref[...] = gval

            @pl.when(i > 0)
            def _(o_ref=o_ref, gval=gval):
                o_ref[...] += gval

    out_specs = [pl.BlockSpec((tm, rows[k][1]), lambda i: (i, 0)) for k in row_grads] + \
                [pl.BlockSpec(params[k].shape, lambda i: (0, 0)) for k in param_grads]
    out_shape = [jax.ShapeDtypeStruct((L, rows[k][1]), dt) for k, dt in zip(row_grads, row_dtypes)] + \
                [jax.ShapeDtypeStruct(params[k].shape, f32) for k in param_grads]
    res = pl.pallas_call(
        body, name=name, grid=(L // tm,), in_specs=_row_specs(rows, tm) + _full_specs(params) + _row_specs(cts, tm),
        out_specs=out_specs, out_shape=out_shape, compiler_params=_params(("arbitrary",)),
    )(*[r[0] for r in rows], *params, *[c[0] for c in cts])
    res = list(res)
    return res[:len(row_grads)], res[len(row_grads):]


def _rms(x, w, n=None):
    n = x.shape[-1] if n is None else n
    return x * lax.rsqrt(jnp.sum(x * x, axis=-1, keepdims=True) * (1.0 / n) + EPS) * w


def _silu(x):
    return x * jax.nn.sigmoid(x)


def _softplus(x):
    return jnp.maximum(x, 0.0) + jnp.log1p(jnp.exp(-jnp.abs(x)))


def _mm(a, b, dims, precision=None):
    if precision is None:
        a, b = a.astype(MXU_DTYPE), b.astype(MXU_DTYPE)
    return lax.dot_general(a, b, (dims, ((), ())), precision=precision, preferred_element_type=f32)


_NN, _NT, _TN = ((1,), (0,)), ((1,), (1,)), ((0,), (0,))


@functools.partial(jax.custom_vjp, nondiff_argnums=(1, 2))
def _roll(x, shift, axis):
    return pltpu.roll(x, shift, axis)


def _roll_fwd(x, shift, axis):
    return pltpu.roll(x, shift, axis), None


def _roll_bwd(shift, axis, _, ct):
    return (pltpu.roll(ct, (ct.shape[axis] - shift) % ct.shape[axis], axis),)


_roll.defvjp(_roll_fwd, _roll_bwd)


def _rope(x, cos, sin, lo, half):
    n = x.shape[-1]
    lane = lax.broadcasted_iota(jnp.int32, x.shape, x.ndim - 1) % HPAD - lo
    first = ((lane >= 0) & (lane < half)) | ((lane >= 2 * half) & (lane < 3 * half))
    rot = jnp.where(first, -_roll(x, n - half, x.ndim - 1), _roll(x, half, x.ndim - 1))
    return x * cos + rot * sin


def _heads(x, n, width=HEAD):
    return [x[:, h * width:(h + 1) * width] for h in range(n)]


def _pad_heads(hs):
    z = jnp.zeros_like(hs[0])
    return jnp.concatenate([t for h in hs for t in (h, z)], axis=-1)


def _unpad_heads(x, n):
    return jnp.concatenate([x[:, h * HPAD:h * HPAD + HEAD] for h in range(n)], axis=-1)


HALO = 8


def _conv_specs(tm, tc, c0, L):
    nh = tm // HALO
    last = L // HALO - 1
    return [pl.BlockSpec((tm, tc), lambda i, j: (i, c0 + j)),
            pl.BlockSpec((HALO, tc), lambda i, j: (jnp.maximum(i * nh - 1, 0), c0 + j)),
            pl.BlockSpec((HALO, tc), lambda i, j: (jnp.minimum((i + 1) * nh, last), c0 + j))]


def _shift3(x):
    n = x.shape[0]
    return _roll(x, 1, 0), _roll(x, n - 1, 0)


def conv_fwd(gfn, x, c0, ncol, tc, w, b, tco, out_dtype, name, tm=512):
    L = x.shape[0]
    tm = _tile(L, tm)
    ni = L // tm

    def body(x_ref, p_ref, n_ref, w_ref, b_ref, o_ref):
        i = pl.program_id(0)
        xv = x_ref[...].astype(f32)
        xp = jnp.where(i == 0, 0.0, p_ref[HALO - 1:HALO, :].astype(f32))
        xn = jnp.where(i == ni - 1, 0.0, n_ref[0:1, :].astype(f32))
        rid = lax.broadcasted_iota(jnp.int32, xv.shape, 0)
        dn, up = _shift3(xv)
        dn = jnp.where(rid == 0, xp, dn)
        up = jnp.where(rid == tm - 1, xn, up)
        c = w_ref[0:1, :] * dn + w_ref[1:2, :] * xv + w_ref[2:3, :] * up + b_ref[...]
        o_ref[...] = gfn(c).astype(o_ref.dtype)

    return pl.pallas_call(
        body, name=name, grid=(ni, ncol),
        in_specs=_conv_specs(tm, tc, c0, L) + [pl.BlockSpec((3, tc), lambda i, j: (0, j)), pl.BlockSpec((1, tc), lambda i, j: (0, j))],
        out_specs=pl.BlockSpec((tm, tco), lambda i, j: (i, j)), out_shape=jax.ShapeDtypeStruct((L, ncol * tco), out_dtype),
        compiler_params=_params(("parallel", "parallel")),
    )(x, x, x, w, b)


def conv_bwd(gfn, x, c0, ncol, tc, w, b, dys, tco, name, tm=512, dx_dtype=f32):
    L = x.shape[0]
    tm = _tile(L, tm)
    ni = L // tm
    nd = len(dys)

    def body(*refs):
        x_ref, xp_ref, xn_ref = refs[:3]
        d_refs = refs[3:3 + 3 * nd]
        w_ref, b_ref, dx_ref, dw_ref, db_ref = refs[3 + 3 * nd:]
        i = pl.program_id(1)
        first, lastb = i == 0, i == ni - 1

        def ext(m, p, n):
            return jnp.concatenate([jnp.where(first, 0.0, p[...].astype(f32)), m[...].astype(f32),
                                    jnp.where(lastb, 0.0, n[...].astype(f32))], axis=0)

        xe = ext(x_ref, xp_ref, xn_ref)
        de = ext(*d_refs[:3])
        for q in range(1, nd):
            de = de + ext(*d_refs[3 * q:3 * q + 3])
        x_dn, x_up = _shift3(xe)
        w0, w1, w2 = w_ref[0:1, :], w_ref[1:2, :], w_ref[2:3, :]
        ce = w0 * x_dn + w1 * xe + w2 * x_up + b_ref[...]
        _, vjp = jax.vjp(gfn, ce)
        (dce,) = vjp(de)
        dc_dn, dc_up = _shift3(dce)
        dx_ref[...] = (w0 * dc_up + w1 * dce + w2 * dc_dn)[HALO:HALO + tm, :].astype(dx_ref.dtype)
        inner = slice(HALO, HALO + tm)
        dci = dce[inner]
        dw = jnp.concatenate([jnp.sum(dci * x_dn[inner], axis=0, keepdims=True), jnp.sum(dci * xe[inner], axis=0, keepdims=True),
                              jnp.sum(dci * x_up[inner], axis=0, keepdims=True)], axis=0)
        db = jnp.sum(dci, axis=0, keepdims=True)

        @pl.when(first)
        def _():
            dw_ref[...] = dw
            db_ref[...] = db

        @pl.when(i > 0)
        def _():
            dw_ref[...] += dw
            db_ref[...] += db

    res = pl.pallas_call(
        body, name=name, grid=(ncol, ni),
        in_specs=[pl.BlockSpec(s.block_shape, (lambda j, i, f=s.index_map: f(i, j))) for s in _conv_specs(tm, tc, c0, L)]
        + [pl.BlockSpec(s.block_shape, (lambda j, i, f=s.index_map: f(i, j))) for s in _conv_specs(tm, tco, 0, L)] * nd
        + [pl.BlockSpec((3, tc), lambda j, i: (0, j)), pl.BlockSpec((1, tc), lambda j, i: (0, j))],
        out_specs=[pl.BlockSpec((tm, tc), lambda j, i: (i, j)), pl.BlockSpec((3, tc), lambda j, i: (0, j)),
                   pl.BlockSpec((1, tc), lambda j, i: (0, j))],
        out_shape=[jax.ShapeDtypeStruct((L, ncol * tc), dx_dtype), jax.ShapeDtypeStruct((3, ncol * tc), f32),
                   jax.ShapeDtypeStruct((1, ncol * tc), f32)],
        compiler_params=_params(("parallel", "arbitrary")),
    )(x, x, x, *[d for d in dys for _ in range(3)], w, b)
    return res


LOG2E = math.log2(math.e)


def flash_fwd(q, k, v, n_q, n_kv, scale, name, side=(), tq=512, tk=8192):
    L = q.shape[0]
    tq, tk = _tile_div(L, tq), _tile_div(L, tk)
    rep = n_q // n_kv
    nkv = L // tk
    c2 = scale * LOG2E

    def body(q_ref, k_ref, v_ref, o_ref, lse_ref):
        qv = q_ref[...]
        ones_lane = lax.broadcasted_iota(jnp.int32, (tk, HPAD), 1) == HPAD - 1

        def step(c, carry):
            m, acc = carry
            off = pl.multiple_of(c * tk, tk)
            s = _mm(qv, k_ref[pl.ds(off, tk), :], _NT)
            m_new = jnp.maximum(m, jnp.max(s, axis=-1, keepdims=True))
            p = jnp.exp2((s - m_new) * c2)
            vv = v_ref[pl.ds(off, tk), :]
            acc = jnp.exp2((m - m_new) * c2) * acc + _mm(p, jnp.where(ones_lane, jnp.ones_like(vv), vv), _NN)
            return m_new, acc

        m, acc = lax.fori_loop(0, nkv, step, (jnp.full((tq, 1), -jnp.inf, f32), jnp.zeros((tq, HPAD), f32)))
        l = acc[:, HPAD - 1:]
        o_ref[...] = jnp.where(lax.broadcasted_iota(jnp.int32, (tq, HPAD), 1) == HPAD - 1, 0.0, acc / l)
        lse_ref[0] = m * scale + jnp.log(l)

    grid = (n_q, L // tq)
    body, ex = _with_exchange(body, 3, 2, side, grid)
    res = pl.pallas_call(
        body, name=name, grid=grid,
        in_specs=[pl.BlockSpec((tq, HPAD), lambda h, i: (i, h)), pl.BlockSpec((L, HPAD), lambda h, i: (0, h // rep)),
                  pl.BlockSpec((L, HPAD), lambda h, i: (0, h // rep))] + ex.specs,
        out_specs=[pl.BlockSpec((tq, HPAD), lambda h, i: (i, h)), pl.BlockSpec((1, tq, 1), lambda h, i: (h, i, 0))] + ex.specs,
        out_shape=[jax.ShapeDtypeStruct((L, n_q * HPAD), f32), jax.ShapeDtypeStruct((n_q, L, 1), f32)] + ex.out_shape,
        scratch_shapes=ex.scratch, compiler_params=_params(("arbitrary", "arbitrary") if side else ("parallel", "parallel")),
    )(q, k, v, *ex.args)
    return res[0], res[1], list(res[2:])


def flash_bwd(q, k, v, o, lse, do, n_q, n_kv, scale, name, side=(), tq=2048, tk=1024):
    L = q.shape[0]
    tq, tk = _tile_div(L, tq), _tile_div(L, tk)
    rep = n_q // n_kv
    c2 = scale * LOG2E

    def body(k_ref, v_ref, q_ref, do_ref, o_ref, lse_ref, dq_ref, dk_ref, dv_ref):
        j, i = pl.program_id(1), pl.program_id(2)
        kv, vv = k_ref[...], v_ref[...]
        rows = pl.ds(pl.multiple_of(i * tq, tq), tq)
        dk, dv = jnp.zeros((tk, HPAD), f32), jnp.zeros((tk, HPAD), f32)
        for r in range(rep):
            cols = slice(r * HPAD, (r + 1) * HPAD)
            qv = q_ref[:, cols]
            dov = do_ref[:, cols]
            delta = jnp.sum(dov * o_ref[:, cols], axis=-1, keepdims=True)
            p = jnp.exp2(_mm(qv, kv, _NT) * c2 - lse_ref[r] * LOG2E)
            dv = dv + _mm(p, dov, _TN)
            ds = p * (_mm(dov, vv, _NT) - delta) * scale
            dk = dk + _mm(ds, qv, _TN)
            dq = _mm(ds, kv, _NN)

            @pl.when(j == 0)
            def _(dq=dq, cols=cols):
                dq_ref[rows, cols] = dq

            @pl.when(j > 0)
            def _(dq=dq, cols=cols):
                dq_ref[rows, cols] += dq

        @pl.when(i == 0)
        def _():
            dk_ref[...] = dk
            dv_ref[...] = dv

        @pl.when(i > 0)
        def _():
            dk_ref[...] += dk
            dv_ref[...] += dv

    w = rep * HPAD
    grid = (n_kv, L // tk, L // tq)
    body, ex = _with_exchange(body, 6, 3, side, grid)
    res = pl.pallas_call(
        body, name=name, grid=grid,
        in_specs=[pl.BlockSpec((tk, HPAD), lambda g, j, i: (j, g)), pl.BlockSpec((tk, HPAD), lambda g, j, i: (j, g)),
                  pl.BlockSpec((tq, w), lambda g, j, i: (i, g)), pl.BlockSpec((tq, w), lambda g, j, i: (i, g)),
                  pl.BlockSpec((tq, w), lambda g, j, i: (i, g)), pl.BlockSpec((rep, tq, 1), lambda g, j, i: (g, i, 0))] + ex.specs,
        out_specs=[pl.BlockSpec((L, w), lambda g, j, i: (0, g)), pl.BlockSpec((tk, HPAD), lambda g, j, i: (j, g)),
                   pl.BlockSpec((tk, HPAD), lambda g, j, i: (j, g))] + ex.specs,
        out_shape=[jax.ShapeDtypeStruct((L, n_q * HPAD), f32), jax.ShapeDtypeStruct((L, n_kv * HPAD), f32),
                   jax.ShapeDtypeStruct((L, n_kv * HPAD), f32)] + ex.out_shape,
        scratch_shapes=ex.scratch, compiler_params=_params(("arbitrary",) * 3 if side else ("parallel", "arbitrary", "arbitrary")),
    )(k, v, q, do, o, lse, *ex.args)
    return res[0], res[1], res[2], list(res[3:])


N_SCAN_HEADS = 4
STATE_ROWS = N_SCAN_HEADS * HEAD


def _tri(n, rev):
    i = lax.broadcasted_iota(jnp.int32, (n, n), 0)
    k = lax.broadcasted_iota(jnp.int32, (n, n), 1)
    return ((k >= i), (k > i)) if rev else ((k <= i), (k < i))


def _decay(acol, incl):
    n = acol.shape[0]
    m1 = jnp.broadcast_to(acol, (n, n))
    return jnp.exp(jnp.where(incl, m1 - m1.T, -jnp.inf))


def _ssd_chunk(S, xbc, win, a_log, dt_bias, rev):
    Q = xbc.shape[0]
    d = 1 if rev else 0
    incl, _ = _tri(Q, rev)
    dt = _softplus(win[:, 4 * d:4 * d + 4] + dt_bias[d:d + 1, :])
    a = dt * (-jnp.exp(a_log[d:d + 1, :]))
    acum = _mm(incl.astype(f32), a, _NN, HI)
    tot = jnp.sum(a, axis=0, keepdims=True)
    xs, Bm, Cm = xbc[:, :256], xbc[:, 256:384], xbc[:, 384:512]
    H, W = N_SCAN_HEADS, N_SCAN_HEADS * HEAD
    lanes = lambda x: _cat([jnp.broadcast_to(x[:, h:h + 1], (x.shape[0], HEAD)) for h in range(H)])
    xdt = xs * lanes(dt)
    cb = [_mm(Cm[:, g * HEAD:(g + 1) * HEAD], Bm[:, g * HEAD:(g + 1) * HEAD], _NT) for g in range(2)]
    scores = _cat([cb[h // 2] * _decay(acum[:, h:h + 1], incl) for h in range(H)])
    own = (lax.broadcasted_iota(jnp.int32, (H * Q, W), 0) // Q) == (lax.broadcasted_iota(jnp.int32, (H * Q, W), 1) // HEAD)
    y = _mm(scores, jnp.where(own, jnp.concatenate([xdt] * H, axis=0), 0.0), _NN)
    grp = (lax.broadcasted_iota(jnp.int32, (W, 2 * HEAD), 0) // (2 * HEAD)) == (lax.broadcasted_iota(jnp.int32, (W, 2 * HEAD), 1) // HEAD)
    y = y + _mm(Cm, jnp.where(grp, _cat([S, S]), 0.0), _NT) * jnp.exp(lanes(acum))
    st = _mm(xdt * jnp.exp(lanes(tot - acum)), Bm, _TN)
    first = lax.broadcasted_iota(jnp.int32, (W, HEAD), 0) < 2 * HEAD
    exp_tot = jnp.concatenate([jnp.broadcast_to(jnp.exp(tot[:, h:h + 1]), (HEAD, 1)) for h in range(H)], axis=0)
    return S * exp_tot + jnp.where(first, st[:, :HEAD], st[:, HEAD:]), y


@functools.partial(jax.custom_vjp, nondiff_argnums=(1,))
def _inv_unit_tri(Lm, order):
    n = Lm.shape[0]
    eye = (lax.broadcasted_iota(jnp.int32, (n, n), 0) == lax.broadcasted_iota(jnp.int32, (n, n), 1)).astype(f32)
    P = -Lm
    T = eye + P
    k = 1
    while 2 * k < order:
        P = _mm(P, P, _NN, lax.Precision.HIGH)
        T = T + _mm(T, P, _NN, lax.Precision.HIGH)
        k *= 2
    return T


def _inv_unit_tri_fwd(Lm, order):
    T = _inv_unit_tri(Lm, order)
    return T, T


def _inv_unit_tri_bwd(order, T, dT):
    return (-_mm(_mm(T, dT, _TN, lax.Precision.HIGH), T, _NT, lax.Precision.HIGH),)


_inv_unit_tri.defvjp(_inv_unit_tri_fwd, _inv_unit_tri_bwd)


@jax.custom_vjp
def _inv_known(Lm, T):
    return T


def _inv_known_fwd(Lm, T):
    return T, T


def _inv_known_bwd(T, dT):
    return _inv_unit_tri_bwd(None, T, dT) + (jnp.zeros_like(T),)


_inv_known.defvjp(_inv_known_fwd, _inv_known_bwd)


def _dn_chunk(S, qkv, win, a_log, dt_bias, rev, kept=None):
    Q = qkv.shape[0]
    H, R = N_SCAN_HEADS, N_SCAN_HEADS * qkv.shape[0]
    d = 1 if rev else 0
    beta = jax.nn.sigmoid(win[:, 4 * d:4 * d + 4])
    gl = -jnp.exp(a_log[d:d + 1, :]) * _softplus(win[:, 8 + 4 * d:12 + 4 * d] + dt_bias[d:d + 1, :])
    G = _mm(_tri(Q, rev)[0].astype(f32), gl, _NN, HI)
    tot = jnp.sum(gl, axis=0, keepdims=True)
    rows = lambda x: jnp.concatenate(_heads(x, H), axis=0)
    col = lambda x: jnp.concatenate([x[:, h:h + 1] for h in range(H)], axis=0)
    per_head = lambda x, n: jnp.concatenate([jnp.broadcast_to(x[:, h:h + 1], (n, 1)) for h in range(H)], axis=0)
    r = lax.broadcasted_iota(jnp.int32, (R, R), 0)
    c = lax.broadcasted_iota(jnp.int32, (R, R), 1)
    same = (r // Q) == (c // Q)
    incl = same & ((c >= r) if rev else (c <= r))
    strict = same & ((c > r) if rev else (c < r))
    own = (lax.broadcasted_iota(jnp.int32, (R, H * HEAD), 0) // Q) == (lax.broadcasted_iota(jnp.int32, (R, H * HEAD), 1) // HEAD)
    blk = lambda x: jnp.where(own, _cat([x] * H), 0.0)
    q, k, v = rows(qkv[:, :256]) * (HEAD ** -0.5), rows(qkv[:, 256:512]), rows(qkv[:, 512:])
    b, Gs, tots = col(beta), col(G), per_head(tot, Q)
    dec = _decay(Gs, incl)
    kb = k * b
    Lm = jnp.where(strict, _mm(kb, k, _NT) * dec, 0.0)
    T = _inv_unit_tri(Lm, Q) if kept is None else _inv_known(Lm, blk(kept))
    eG = jnp.exp(Gs)
    uw = _mm(T, _cat([v * b, kb * eG]), _NN)
    vnew = uw[:, :HEAD] - _mm(blk(uw[:, HEAD:]), S, _NN)
    o = _mm(blk(q * eG), S, _NN) + _mm(_mm(q, k, _NT) * dec, vnew, _NN)
    S_next = S * jnp.exp(per_head(tot, HEAD)) + _mm(blk(k * jnp.exp(tots - Gs)), vnew, _TN)
    out = S_next, _cat([o[h * Q:(h + 1) * Q, :] for h in range(H)])
    if kept is None:
        out += (jnp.concatenate([T[h * Q:(h + 1) * Q, h * Q:(h + 1) * Q] for h in range(H)], axis=0),)
    return out


def scan_fwd(chunk_fn, Q, x, p, win_blk, a_log, dt_bias, name, keeps=False):
    L, W = x.shape
    nc = L // Q
    fidx, ridx = (lambda t: t), (lambda t: nc - 1 - t)

    def body(xf_ref, xr_ref, wf_ref, wr_ref, al_ref, db_ref, yf_ref, yr_ref, sf_ref, sr_ref, *rest):
        (sf_scr, sr_scr), keep_refs = rest[-2:], rest[:-2]

        @pl.when(pl.program_id(0) == 0)
        def _():
            sf_scr[...] = jnp.zeros_like(sf_scr)
            sr_scr[...] = jnp.zeros_like(sr_scr)

        for d, (x_ref, w_ref, y_ref, sin_ref, s_scr) in enumerate(((xf_ref, wf_ref, yf_ref, sf_ref, sf_scr),
                                                                   (xr_ref, wr_ref, yr_ref, sr_ref, sr_scr))):
            S = s_scr[...]
            sin_ref[...] = S
            res = chunk_fn(S, x_ref[...], w_ref[...], al_ref[...], db_ref[...], d == 1)
            s_scr[...], y_ref[...] = res[:2]
            if keeps:
                keep_refs[d][...] = res[2]

    row = lambda w, idx, b=0: pl.BlockSpec((Q, w), lambda t: (idx(t), b))
    small = pl.BlockSpec(a_log.shape, lambda t: (0, 0))
    state = lambda idx: pl.BlockSpec((STATE_ROWS, HEAD), lambda t: (idx(t), 0))
    n_state = 4 if keeps else 2
    return pl.pallas_call(
        body, name=name, grid=(nc,),
        in_specs=[row(W, fidx), row(W, ridx), row(HPAD, fidx, win_blk), row(HPAD, ridx, win_blk), small, small],
        out_specs=[row(STATE_ROWS, fidx), row(STATE_ROWS, ridx)] + [state(fidx), state(ridx)] * (n_state // 2),
        out_shape=[jax.ShapeDtypeStruct((L, STATE_ROWS), f32)] * 2 + [jax.ShapeDtypeStruct((nc * STATE_ROWS, HEAD), f32)] * n_state,
        scratch_shapes=[pltpu.VMEM((STATE_ROWS, HEAD), f32)] * 2, compiler_params=_params(("arbitrary",)),
    )(x, x, p, p, a_log, dt_bias)


def scan_bwd(chunk_fn, Q, x, p, win_blk, a_log, dt_bias, s_fwd, s_rev, dy, name, kept=()):
    L, W = x.shape
    nc = L // Q
    fidx, ridx = (lambda t: nc - 1 - t), (lambda t: t)
    nk = len(kept)

    def body(xf_ref, xr_ref, wf_ref, wr_ref, al_ref, db_ref, sf_ref, sr_ref, dyf_ref, dyr_ref, *rest):
        keep_refs = rest[:nk]
        dxf_ref, dxr_ref, dwf_ref, dwr_ref, dal_ref, ddb_ref, dsf_scr, dsr_scr = rest[nk:]

        @pl.when(pl.program_id(0) == 0)
        def _():
            dsf_scr[...] = jnp.zeros_like(dsf_scr)
            dsr_scr[...] = jnp.zeros_like(dsr_scr)
            dal_ref[...] = jnp.zeros_like(dal_ref)
            ddb_ref[...] = jnp.zeros_like(ddb_ref)

        dal_sum, ddb_sum = dal_ref[...], ddb_ref[...]
        for rev, x_ref, w_ref, sin_ref, dy_ref, dx_ref, dw_ref, ds_scr in (
                (False, xf_ref, wf_ref, sf_ref, dyf_ref, dxf_ref, dwf_ref, dsf_scr),
                (True, xr_ref, wr_ref, sr_ref, dyr_ref, dxr_ref, dwr_ref, dsr_scr)):
            extra = (keep_refs[int(rev)][...],) if nk else ()
            _, vjp = jax.vjp(lambda S, xv, wv, al, db, rev=rev, extra=extra: chunk_fn(S, xv, wv, al, db, rev, *extra),
                             sin_ref[...], x_ref[...], w_ref[...], al_ref[...], db_ref[...])
            ds_scr[...], dx_ref[...], dw_ref[...], dal, ddb = vjp((ds_scr[...], dy_ref[...]))
            dal_sum, ddb_sum = dal_sum + dal, ddb_sum + ddb
        dal_ref[...] = dal_sum
        ddb_ref[...] = ddb_sum

    row = lambda w, idx, b=0: pl.BlockSpec((Q, w), lambda t: (idx(t), b))
    small = pl.BlockSpec(a_log.shape, lambda t: (0, 0))
    state = lambda idx: pl.BlockSpec((STATE_ROWS, HEAD), lambda t: (idx(t), 0))
    return pl.pallas_call(
        body, name=name, grid=(nc,),
        in_specs=[row(W, fidx), row(W, ridx), row(HPAD, fidx, win_blk), row(HPAD, ridx, win_blk), small, small,
                  state(fidx), state(ridx), row(STATE_ROWS, fidx), row(STATE_ROWS, ridx)] + [state(fidx), state(ridx)][:nk],
        out_specs=[row(W, fidx), row(W, ridx), row(HPAD, fidx), row(HPAD, ridx), small, small],
        out_shape=[jax.ShapeDtypeStruct((L, W), f32)] * 2 + [jax.ShapeDtypeStruct((L, HPAD), f32)] * 2
        + [jax.ShapeDtypeStruct(a_log.shape, f32)] * 2,
        scratch_shapes=[pltpu.VMEM((STATE_ROWS, HEAD), f32)] * 2, compiler_params=_params(("arbitrary",)),
    )(x, x, p, p, a_log, dt_bias, s_fwd, s_rev, dy, dy, *kept)


def _cat(xs):
    return jnp.concatenate(xs, axis=-1)


def _a_prep(pa, cos, sin, qn, wuq, kvn, wk, wv):
    cq, ckv, kr = pa[:, :256], pa[:, 256:384], pa[:, 384:512]
    q = _rope(_mm(_rms(cq, qn, A_Q_LORA), wuq, _NN), _cat([cos] * 4), _cat([sin] * 4), A_NOPE, A_ROPE // 4)
    kvh = _rms(ckv, kvn)
    k = _mm(kvh, wk, _NN) + _cat([_rope(kr, cos, sin, A_NOPE, A_ROPE // 4)] * 4)
    return q, k, _mm(kvh, wv, _NN)


def _b_prep(pb, cos, sin, qn, kn):
    cos4, sin4 = _cat([cos] * 4), _cat([sin] * 4)
    q = _pad_heads([_rms(h, qn) for h in _heads(pb[:, :256], 4)])
    k = _pad_heads([_rms(h, kn) for h in _heads(pb[:, 256:384], 2) for _ in range(2)])
    v = _pad_heads([h for h in _heads(pb[:, 384:512], 2) for _ in range(2)])
    return _rope(q, cos4, sin4, 0, HEAD // 4), _rope(k, cos4, sin4, 0, HEAD // 4), v


def _c_act(c):
    return _silu(c)


def _d_act(c):
    s = _silu(c)
    return _cat([h * lax.rsqrt(jnp.sum(h * h, axis=-1, keepdims=True) + 1e-6) for h in _heads(s[:, :512], 8)] + [s[:, 512:]])


def _ffn_act(c):
    return _silu(c[:, :FF_BLK]) * c[:, FF_BLK:]


def _mix_post(oa, ob, yc_f, yc_r, xbc, cz, od_f, od_r, dz, a_out, b_out, dskip, c_out, d_out):
    o_a = _rms(_unpad_heads(oa, 4), a_out)
    o_b = _rms(_unpad_heads(ob, 4), b_out)
    skip = _cat([jnp.broadcast_to(dskip[:, h:h + 1], (1, HEAD)) for h in range(4)])
    o_c = _rms((yc_f + yc_r + xbc[:, :256] * skip) * _silu(cz), c_out)
    o_d = _cat([_rms(h, d_out) for h in _heads(od_f + od_r, 4)]) * _silu(dz)
    return _cat([o_a, o_b, o_c, o_d])


def _post_mix(x, y1, g_pm, g_pf):
    x1 = x + _rms(y1, g_pm)
    return x1, _rms(x1, g_pf)


def _post_ffn(x1, y2, g):
    return x1 + _rms(y2, g)


def _pre_bwd(x, g):
    return x, _rms(x, g)


def _layer_fwd(x, W, tabs, n, side=(), late_side=(), late_weights=None):
    mx = MXU_DTYPE
    cos_a, sin_a, cos_b, sin_b = tabs
    full = lambda a: (a, a.shape[1], 0)
    s = {"x": x}
    (s["h"],) = rowwise(_rms, [full(x)], [W["g_pre"]], [(D_MODEL, mx)], n + "pre")
    p = s["p"] = matmul(s["h"], W["w_in"], "nn", f32, n + "in")
    s["qa"], s["ka"], s["va"] = rowwise(_a_prep, [(p, 512, SEG_A // 512), full(cos_a), full(sin_a)],
                                        [W["a_qn"], W["a_wuq"], W["a_kvn"], W["a_wk"], W["a_wv"]], [(512, mx)] * 3, n + "a_prep")
    s["qb"], s["kb"], s["vb"] = rowwise(_b_prep, [(p, 512, SEG_B // 512), full(cos_b), full(sin_b)], [W["b_qn"], W["b_kn"]],
                                        [(512, mx)] * 3, n + "b_prep")
    s["oa"], s["lsea"], side_out = flash_fwd(s["qa"], s["ka"], s["va"], 4, 4, (A_NOPE + A_ROPE) ** -0.5, n + "a_attn", side)
    s["ob"], s["lseb"], late_out = flash_fwd(s["qb"], s["kb"], s["vb"], 4, 4, HEAD ** -0.5, n + "b_attn", late_side)
    if late_weights is not None:
        W = {**W, **late_weights(late_out)}
    xbc = s["xbc"] = conv_fwd(_c_act, p, SEG_CX // 512, 1, 512, W["c_cw"], W["c_cb"], 512, f32, n + "c_conv")
    s["yc_f"], s["yc_r"], s["sc_f"], s["sc_r"] = scan_fwd(_ssd_chunk, min(SSD_CHUNK, xbc.shape[0]), xbc, p, SEG_CDT // HPAD, W["c_alog"], W["c_dtb"], n + "c_ssd")
    qkv = s["qkv"] = conv_fwd(_d_act, p, SEG_DX // 768, 1, 768, W["d_cw"], W["d_cb"], 768, f32, n + "d_conv")
    s["od_f"], s["od_r"], s["sd_f"], s["sd_r"], s["td_f"], s["td_r"] = scan_fwd(
        _dn_chunk, DN_CHUNK, qkv, p, SEG_DAB // HPAD, W["d_alog"], W["d_dtb"], n + "d_dn", keeps=True)
    (s["omix"],) = rowwise(_mix_post, _mix_rows(s), _mix_params(W), [(D_MODEL, mx)], n + "mix_post")
    s["y1"] = matmul(s["omix"], W["w_out"], "nn", f32, n + "out")
    s["x1"], s["h2"] = rowwise(_post_mix, [full(x), full(s["y1"])], [W["g_pm"], W["g_pf"]], [(D_MODEL, f32), (D_MODEL, mx)], n + "post_mix")
    s["u"] = matmul(s["h2"], W["f_win"], "nn", f32, n + "f_in")
    s["a"] = conv_fwd(_ffn_act, s["u"], 0, D_FF // FF_BLK, 2 * FF_BLK, W["f_cw"], W["f_cb"], FF_BLK, mx, n + "f_conv", tm=512)
    s["y2"] = matmul(s["a"], W["f_wout"], "nn", f32, n + "f_out")
    (x2,) = rowwise(_post_ffn, [full(s["x1"]), full(s["y2"])], [W["g_po"]], [(D_MODEL, f32)], n + "post_ffn")
    return x2, s, side_out, W


def _mix_rows(s):
    p = s["p"]
    return [(s["oa"], 512, 0), (s["ob"], 512, 0), (s["yc_f"], 256, 0), (s["yc_r"], 256, 0), (s["xbc"], 512, 0),
            (p, 256, SEG_CZ // 256), (s["od_f"], 256, 0), (s["od_r"], 256, 0), (p, 256, SEG_DZ // 256)]


def _mix_params(W):
    return [W["a_out"], W["b_out"], W["c_dskip"], W["c_out"], W["d_out"]]


def _layer_bwd(s, dx2, W, tabs, n, side=(), late_side=None):
    cos_a, sin_a, cos_b, sin_b = tabs
    full = lambda a: (a, a.shape[1], 0)
    p, g = s["p"], {}
    (dx1, dy2), (g["g_po"],) = rowwise_vjp(_post_ffn, [full(s["x1"]), full(s["y2"])], [W["g_po"]], [full(dx2)], n + "post_ffn_b", [0, 1], [0],
                                           row_dtypes=[f32, MXU_DTYPE])
    da = matmul(dy2, W["f_wout"], "nt", f32, n + "f_out_dx")
    g["f_wout"] = matmul(s["a"], dy2, "tn", f32, n + "f_out_dw")
    du, g["f_cw"], g["f_cb"] = conv_bwd(_ffn_act, s["u"], 0, D_FF // FF_BLK, 2 * FF_BLK, W["f_cw"], W["f_cb"], [da], FF_BLK, n + "f_conv_b", tm=512,
                                        dx_dtype=MXU_DTYPE)
    dh2 = matmul(du, W["f_win"], "nt", f32, n + "f_in_dx")
    g["f_win"] = matmul(s["h2"], du, "tn", f32, n + "f_in_dw")
    (dx, dy1), (g["g_pm"], g["g_pf"]) = rowwise_vjp(_post_mix, [full(s["x"]), full(s["y1"])], [W["g_pm"], W["g_pf"]],
                                                   [full(dx1), full(dh2)], n + "post_mix_b", [0, 1], [0, 1],
                                                   row_dtypes=[f32, MXU_DTYPE])
    domix = matmul(dy1, W["w_out"], "nt", f32, n + "out_dx")
    g["w_out"] = matmul(s["omix"], dy1, "tn", f32, n + "out_dw")
    (doa, dob, dyc, dxbc_skip, dcz, dod, ddz), (g["a_out"], g["b_out"], g["c_dskip"], g["c_out"], g["d_out"]) = rowwise_vjp(
        _mix_post, _mix_rows(s), _mix_params(W), [full(domix)], n + "mix_post_b", [0, 1, 2, 4, 5, 6, 8], list(range(5)))
    dqkv_f, dqkv_r, ddab_f, ddab_r, g["d_alog"], g["d_dtb"] = scan_bwd(
        _dn_chunk, DN_CHUNK, s["qkv"], p, SEG_DAB // HPAD, W["d_alog"], W["d_dtb"], s["sd_f"], s["sd_r"], dod, n + "d_dn_b",
        kept=(s["td_f"], s["td_r"]))
    ddx, g["d_cw"], _ = conv_bwd(_d_act, p, SEG_DX // 768, 1, 768, W["d_cw"], W["d_cb"], [dqkv_f, dqkv_r], 768, n + "d_conv_b")
    dxbc_f, dxbc_r, dcdt_f, dcdt_r, g["c_alog"], g["c_dtb"] = scan_bwd(
        _ssd_chunk, min(SSD_CHUNK, p.shape[0]), s["xbc"], p, SEG_CDT // HPAD, W["c_alog"], W["c_dtb"], s["sc_f"], s["sc_r"], dyc, n + "c_ssd_b")
    dcx, g["c_cw"], g["c_cb"] = conv_bwd(_c_act, p, SEG_CX // 512, 1, 512, W["c_cw"], W["c_cb"], [dxbc_f, dxbc_r, dxbc_skip], 512,
                                         n + "c_conv_b")
    (dsmall,) = rowwise(lambda a, b, c, d: _cat([a + b, c + d]), [full(dcdt_f), full(dcdt_r), full(ddab_f), full(ddab_r)], [],
                        [(2 * HPAD, f32)], n + "dwin_sum")
    dqa, dka, dva, side_out = flash_bwd(s["qa"], s["ka"], s["va"], s["oa"], s["lsea"], doa, 4, 4, (A_NOPE + A_ROPE) ** -0.5,
                                        n + "a_attn_b", side)
    dqb, dkb, dvb, late_out = flash_bwd(s["qb"], s["kb"], s["vb"], s["ob"], s["lseb"], dob, 4, 4, HEAD ** -0.5, n + "b_attn_b",
                                        late_side(g) if late_side is not None else ())
    (dpa,), (g["a_qn"], g["a_wuq"], g["a_kvn"], g["a_wk"], g["a_wv"]) = rowwise_vjp(
        _a_prep, [(p, 512, SEG_A // 512), full(cos_a), full(sin_a)], [W["a_qn"], W["a_wuq"], W["a_kvn"], W["a_wk"], W["a_wv"]],
        [full(dqa), full(dka), full(dva)], n + "a_prep_b", [0], list(range(5)))
    (dpb,), (g["b_qn"], g["b_kn"]) = rowwise_vjp(_b_prep, [(p, 512, SEG_B // 512), full(cos_b), full(sin_b)], [W["b_qn"], W["b_kn"]],
                                               [full(dqb), full(dkb), full(dvb)], n + "b_prep_b", [0], [0, 1])
    dp = jnp.concatenate([dpa, dpb, dcx, ddx, dcz, ddz, dsmall], axis=1).astype(MXU_DTYPE)
    dh = matmul(dp, W["w_in"], "nt", f32, n + "in_dx")
    g["w_in"] = matmul(s["h"], dp, "tn", f32, n + "in_dw")
    (dx0,), (g["g_pre"],) = rowwise_vjp(_pre_bwd, [full(s["x"])], [W["g_pre"]], [full(dx), full(dh)], n + "pre_b", [0], [0])
    return dx0, g, side_out, late_out


def loss_and_grad(y, target, name):
    L, D = y.shape
    tm = _tile(L, 512)

    def body(y_ref, t_ref, loss_ref, dy_ref):
        e = y_ref[...] - t_ref[...]
        dy_ref[...] = e * (1.0 / D)
        part = 0.5 * jnp.sum(jnp.sum(e * e, axis=1, keepdims=True) * (1.0 / D), axis=0, keepdims=True)

        @pl.when(pl.program_id(0) == 0)
        def _():
            loss_ref[...] = part

        @pl.when(pl.program_id(0) > 0)
        def _():
            loss_ref[...] += part

    row = pl.BlockSpec((tm, D), lambda i: (i, 0))
    return pl.pallas_call(
        body, name=name, grid=(L // tm,), in_specs=[row, row], out_specs=[pl.BlockSpec((1, 1), lambda i: (0, 0)), row],
        out_shape=[jax.ShapeDtypeStruct((1, 1), f32), jax.ShapeDtypeStruct((L, D), f32)], compiler_params=_params(("arbitrary",)),
    )(y, target)


_IN_SEGS = [(0, 192), (256, 128), (384 + A_NOPE, 32), (512, 256), (768, 128), (896, 128), (SEG_CZ, 256), (SEG_CX, 512),
            (SEG_CDT, 8), (SEG_DX, 768), (SEG_DZ, 256), (SEG_DAB, 16)]


def _pad_in(w):
    src, pieces = 0, {}
    for off, wd in _IN_SEGS:
        pieces[off] = w[..., src:src + wd]
        src += wd
    out, pos = [], 0
    for off in sorted(pieces):
        if off > pos:
            out.append(jnp.zeros(w.shape[:-1] + (off - pos,), w.dtype))
        out.append(pieces[off])
        pos = off + pieces[off].shape[-1]
    out.append(jnp.zeros(w.shape[:-1] + (PIN - pos,), w.dtype))
    return jnp.concatenate(out, axis=-1)


def _unpad_in(wp):
    return jnp.concatenate([wp[..., off:off + wd] for off, wd in _IN_SEGS], axis=-1)


def _pad_last(a, n):
    return jnp.pad(a, [(0, 0)] * (a.ndim - 1) + [(0, n - a.shape[-1])])


def _ff_interleave(w):
    nb = D_FF // FF_BLK
    if w.size <= 4 * D_FF * 2:
        lead = w.shape[:-1]
        return jnp.stack([w[..., :D_FF].reshape(lead + (nb, FF_BLK)), w[..., D_FF:].reshape(lead + (nb, FF_BLK))],
                         axis=-2).reshape(lead + (2 * D_FF,))
    return jnp.concatenate([w[..., off + j * FF_BLK:off + (j + 1) * FF_BLK] for j in range(nb) for off in (0, D_FF)], axis=-1)


def _ff_deinterleave(w):
    nb = D_FF // FF_BLK
    if w.size <= 4 * D_FF * 2:
        lead = w.shape[:-1]
        t = w.reshape(lead + (nb, 2, FF_BLK))
        return jnp.concatenate([t[..., 0, :].reshape(lead + (D_FF,)), t[..., 1, :].reshape(lead + (D_FF,))], axis=-1)
    return jnp.concatenate([w[..., (2 * j + half) * FF_BLK:(2 * j + half + 1) * FF_BLK] for half in (0, 1) for j in range(nb)], axis=-1)


def _row(v):
    return v.reshape(1, -1)


_LATE = ["w_out", "f_w_in", "f_w_out"]


def _late_kernel_weights(P, l, wdt):
    return {"w_out": P["w_out"][l].astype(wdt), "f_win": _ff_interleave(P["f_w_in"][l]).astype(wdt), "f_wout": P["f_w_out"][l].astype(wdt)}


def _kernel_weights(P, l, wdt):
    late = {} if P["w_out"][l] is None else _late_kernel_weights(P, l, wdt)
    uq = P["a_w_uq"][l].reshape(A_Q_LORA, 4, A_NOPE + A_ROPE)
    ukv = P["a_w_ukv"][l].reshape(A_KV_LORA, 4, 2 * HEAD)
    z = jnp.zeros((A_KV_LORA, 4, HEAD), ukv.dtype)
    return {
        "g_pre": _row(P["pre_mix_norm"][l]), "w_in": _pad_in(P["w_in"][l]).astype(wdt),
        "a_qn": _pad_last(_row(P["a_q_norm"][l]), 256),
        "a_wuq": jnp.pad(_pad_last(uq, HPAD).reshape(A_Q_LORA, 4 * HPAD), ((0, 256 - A_Q_LORA), (0, 0))).astype(wdt),
        "a_kvn": _row(P["a_kv_norm"][l]),
        "a_wk": jnp.concatenate([ukv[..., :HEAD], z], axis=-1).reshape(A_KV_LORA, 4 * HPAD).astype(wdt),
        "a_wv": jnp.concatenate([ukv[..., HEAD:], z], axis=-1).reshape(A_KV_LORA, 4 * HPAD).astype(wdt),
        "a_out": _row(P["a_out_norm"][l]), "b_qn": _row(P["b_q_norm"][l]), "b_kn": _row(P["b_k_norm"][l]), "b_out": _row(P["b_out_norm"][l]),
        "c_cw": P["c_conv_w"][l], "c_cb": _row(P["c_conv_b"][l]), "c_alog": P["c_a_log"][l], "c_dtb": P["c_dt_bias"][l],
        "c_dskip": _row(P["c_d_skip"][l]), "c_out": _row(P["c_out_norm"][l]),
        "d_cw": P["d_conv_w"][l], "d_cb": jnp.zeros((1, 768), f32), "d_alog": P["d_a_log"][l], "d_dtb": P["d_dt_bias"][l],
        "d_out": _row(P["d_out_norm"][l]), "g_pm": _row(P["post_mix_norm"][l]), "g_pf": _row(P["pre_ffn_norm"][l]),
        "f_cw": _ff_interleave(P["f_conv_w"][l]), "f_cb": _row(_ff_interleave(P["f_conv_b"][l])),
        "g_po": _row(P["post_ffn_norm"][l]), **late,
    }


def _reference_grads(g):
    uq = g["a_wuq"][:A_Q_LORA].reshape(A_Q_LORA, 4, HPAD)[..., :A_NOPE + A_ROPE].reshape(A_Q_LORA, 4 * (A_NOPE + A_ROPE))
    wk = g["a_wk"].reshape(A_KV_LORA, 4, HPAD)[..., :HEAD]
    wv = g["a_wv"].reshape(A_KV_LORA, 4, HPAD)[..., :HEAD]
    return {
        "pre_mix_norm": g["g_pre"][0], "w_in": _unpad_in(g["w_in"]), "a_q_norm": g["a_qn"][0, :A_Q_LORA], "a_w_uq": uq,
        "a_kv_norm": g["a_kvn"][0], "a_w_ukv": jnp.concatenate([wk, wv], axis=-1).reshape(A_KV_LORA, 8 * HEAD),
        "a_out_norm": g["a_out"][0], "b_q_norm": g["b_qn"][0], "b_k_norm": g["b_kn"][0], "b_out_norm": g["b_out"][0],
        "c_conv_w": g["c_cw"], "c_conv_b": g["c_cb"][0], "c_a_log": g["c_alog"], "c_dt_bias": g["c_dtb"], "c_d_skip": g["c_dskip"][0],
        "c_out_norm": g["c_out"][0], "d_conv_w": g["d_cw"], "d_a_log": g["d_alog"], "d_dt_bias": g["d_dtb"], "d_out_norm": g["d_out"][0],
        "w_out": g["w_out"], "post_mix_norm": g["g_pm"][0], "pre_ffn_norm": g["g_pf"][0], "f_w_in": _ff_deinterleave(g["f_win"]),
        "f_conv_w": _ff_deinterleave(g["f_cw"]), "f_conv_b": _ff_deinterleave(g["f_cb"][0]), "f_w_out": g["f_wout"],
        "post_ffn_norm": g["g_po"][0],
    }


def _rope_tables(L):
    def tables(rot):
        rows = L // GRID_W
        row = jnp.repeat(jnp.arange(rows), GRID_W).astype(f32)
        col = jnp.tile(jnp.arange(GRID_W), rows).astype(f32)
        sec = rot // 2
        inv = ROPE_BASE ** (-jnp.arange(0, sec, 2, dtype=f32) / sec)
        ang = jnp.concatenate([row[:, None] * inv] * 2 + [col[:, None] * inv] * 2, axis=-1)
        return jnp.cos(ang), jnp.sin(ang)

    ca, sa = tables(A_ROPE)
    cb, sb = tables(HEAD)
    one, zero = jnp.ones, jnp.zeros
    return (jnp.concatenate([one((L, A_NOPE), f32), ca, one((L, HPAD - A_NOPE - A_ROPE), f32)], axis=1),
            jnp.concatenate([zero((L, A_NOPE), f32), sa, zero((L, HPAD - A_NOPE - A_ROPE), f32)], axis=1),
            jnp.concatenate([cb, one((L, HPAD - HEAD), f32)], axis=1), jnp.concatenate([sb, zero((L, HPAD - HEAD), f32)], axis=1))


def exchange(items, name):
    n = len(items)

    def body(*refs):
        start, wait = _exchange_ops([g for _, g in items], refs[:n], refs[n:2 * n], *refs[2 * n:])
        start()
        wait()

    ex = _exchange_io(items)
    res = pl.pallas_call(body, name=name, in_specs=ex.specs, out_specs=ex.specs, out_shape=ex.out_shape,
                         scratch_shapes=ex.scratch)(*ex.args)
    return list(res)


def _exchange_ops(modes, src_refs, out_refs, send_sems, recv_sems, local_sems):
    n = len(modes)
    x, y, c = lax.axis_index("x"), lax.axis_index("y"), lax.axis_index("c")
    me = 4 * x + 2 * y + c

    def copy(a, k, arriving):
        px, py, pc = (1 - x if k & 4 else x), (1 - y if k & 2 else y), (1 - c if k & 1 else c)
        pid = 4 * px + 2 * py + pc
        sem = a * (N_DEV - 1) + k - 1
        return pltpu.make_async_remote_copy(
            src_ref=src_refs[a] if modes[a] else src_refs[a].at[pid], dst_ref=out_refs[a].at[pid if arriving else me],
            send_sem=send_sems.at[sem], recv_sem=recv_sems.at[sem], device_id=(px, py, pc), device_id_type=pl.DeviceIdType.MESH)

    def local(a):
        return pltpu.make_async_copy(src_refs[a] if modes[a] else src_refs[a].at[me], out_refs[a].at[me], local_sems.at[a])

    pairs = [(a, k) for k in range(1, N_DEV) for a in range(n)]

    def start():
        for a, k in pairs:
            copy(a, k, False).start()
        for a in range(n):
            local(a).start()

    def wait():
        for a, k in pairs:
            copy(a, k, False).wait_send()
        for a, k in pairs:
            copy(a, k, True).wait_recv()
        for a in range(n):
            local(a).wait()

    return start, wait


class _ExchangeIO:
    def __init__(self, items):
        n = len(items)
        self.args = [s for s, _ in items]
        self.specs = [pl.BlockSpec(memory_space=pl.ANY)] * n
        self.out_shape = [jax.ShapeDtypeStruct((N_DEV,) + tuple(s.shape if g else s.shape[1:]), s.dtype) for s, g in items]
        self.scratch = [pltpu.SemaphoreType.DMA((n * (N_DEV - 1),)), pltpu.SemaphoreType.DMA((n * (N_DEV - 1),)),
                        pltpu.SemaphoreType.DMA((n,))] if n else []


def _exchange_io(items):
    return _ExchangeIO(list(items))


def _with_exchange(body, n_in, n_out, side, grid):
    ex = _exchange_io(side)
    n = len(ex.args)
    if not n:
        return body, ex
    modes = [g for _, g in side]

    def wrapped(*refs):
        ins, src = refs[:n_in], refs[n_in:n_in + n]
        outs, dst = refs[n_in + n:n_in + n + n_out], refs[n_in + n + n_out:n_in + 2 * n + n_out]
        sems = refs[n_in + 2 * n + n_out:]
        start, wait = _exchange_ops(modes, src, dst, *sems)
        ids = [pl.program_id(a) for a in range(len(grid))]
        first, last = ids[0] == 0, ids[0] == grid[0] - 1
        for a in range(1, len(grid)):
            first, last = first & (ids[a] == 0), last & (ids[a] == grid[a] - 1)
        pl.when(first)(start)
        body(*ins, *outs)
        pl.when(last)(wait)

    return wrapped, ex


LANES = 1024


def adamw(gstacks, w, m, v, name):
    D, R, C = w.shape
    S = gstacks[0].shape[0]
    tr = R if R <= 512 else _tile(R, 256)

    def body(*refs):
        g_refs = refs[:D]
        w_ref, m_ref, v_ref, go_ref, d_ref, mo_ref, vo_ref = refs[D:]
        layer = pl.program_id(0)
        g = None
        for d, g_ref in enumerate(g_refs):
            gd = g_ref[0].astype(f32)
            for sl in range(1, S):
                gd = gd + g_ref[sl].astype(f32)
            g = gd if g is None else jnp.where(layer == d, gd, g)
        g = g[None]
        m_new = ADAM_B1 * m_ref[...] + (1.0 - ADAM_B1) * g
        v_new = ADAM_B2 * v_ref[...] + (1.0 - ADAM_B2) * jnp.square(g)
        m_hat = m_new / (1.0 - ADAM_B1 ** ADAM_STEP)
        v_hat = v_new / (1.0 - ADAM_B2 ** ADAM_STEP)
        go_ref[...] = g
        d_ref[...] = -ADAM_LR * (m_hat / (jnp.sqrt(v_hat) + ADAM_EPS) + ADAM_WD * w_ref[...])
        mo_ref[...] = m_new
        vo_ref[...] = v_new

    row = pl.BlockSpec((1, tr, C), lambda l, i: (l, i, 0))
    g_specs = [pl.BlockSpec((S, tr, C), lambda l, i, d=d: (0, jnp.where(l == d, i, 0), 0)) for d in range(D)]
    return pl.pallas_call(
        body, name=name, grid=(D, R // tr), in_specs=g_specs + [row, row, row],
        out_specs=[row] * 4, out_shape=[jax.ShapeDtypeStruct((D, R, C), f32)] * 4, compiler_params=_params(("parallel", "parallel")),
    )(*gstacks, w, m, v)


def sum_slots(gstack, name):
    S, R, _ = gstack.shape

    def body(g_ref, o_ref):
        g = g_ref[0]
        for sl in range(1, S):
            g = g + g_ref[sl]
        o_ref[...] = g

    return pl.pallas_call(body, name=name, out_shape=jax.ShapeDtypeStruct((R, LANES), f32), compiler_params=_params())(gstack)


def _pack(parts, rows, dtype=f32):
    flat = jnp.concatenate([q.reshape(-1).astype(dtype) for q in parts])
    return jnp.pad(flat, (0, rows * LANES - flat.shape[0])).reshape(rows, LANES)


def _unpack(buf, shapes):
    lead = buf.shape[:-2]
    flat = buf.reshape(lead + (-1,))
    out, off = [], 0
    for shp in shapes:
        n = math.prod(shp)
        out.append(flat[..., off:off + n].reshape(lead + tuple(shp)))
        off += n
    return out


_WEIGHTS = ["pre_mix_norm", "w_in", "a_q_norm", "a_w_uq", "a_kv_norm", "a_w_ukv", "a_out_norm", "b_q_norm", "b_k_norm", "b_out_norm",
            "c_conv_w", "c_conv_b", "c_a_log", "c_dt_bias", "c_d_skip", "c_out_norm", "d_conv_w", "d_a_log", "d_dt_bias", "d_out_norm",
            "w_out", "post_mix_norm", "pre_ffn_norm", "f_w_in", "f_conv_w", "f_conv_b", "f_w_out", "post_ffn_norm"]
_BIG = {"w_in": 1, "a_w_uq": 1, "a_w_ukv": 1, "w_out": 0, "f_w_in": 1, "f_w_out": 0}
_CONV = ["c_conv_w", "d_conv_w", "f_conv_w"]
_REP = [n for n in _WEIGHTS if n not in _BIG and n not in _CONV]
SMALLG_ROWS, SMALLW_ROWS = 64, 32


def _join(blocks, axis):
    return jnp.concatenate([blocks[d] for d in range(N_DEV)], axis=axis)


def _split(full, axis):
    n = full.shape[axis] // N_DEV
    return jnp.stack([lax.slice_in_dim(full, d * n, (d + 1) * n, axis=axis) for d in range(N_DEV)])


def kernel(x, pre_mix_norm, w_in, a_q_norm, a_w_uq, a_kv_norm, a_w_ukv, a_out_norm, b_q_norm, b_k_norm, b_out_norm, c_conv_w, c_conv_b, c_a_log, c_dt_bias, c_d_skip, c_out_norm, d_conv_w, d_a_log, d_dt_bias, d_out_norm, w_out, post_mix_norm, pre_ffn_norm, f_w_in, f_conv_w, f_conv_b, f_w_out, post_ffn_norm, loss_target, m_pre_mix_norm, m_w_in, m_a_q_norm, m_a_w_uq, m_a_kv_norm, m_a_w_ukv, m_a_out_norm, m_b_q_norm, m_b_k_norm, m_b_out_norm, m_c_conv_w, m_c_conv_b, m_c_a_log, m_c_dt_bias, m_c_d_skip, m_c_out_norm, m_d_conv_w, m_d_a_log, m_d_dt_bias, m_d_out_norm, m_w_out, m_post_mix_norm, m_pre_ffn_norm, m_f_w_in, m_f_conv_w, m_f_conv_b, m_f_w_out, m_post_ffn_norm, v_pre_mix_norm, v_w_in, v_a_q_norm, v_a_w_uq, v_a_kv_norm, v_a_w_ukv, v_a_out_norm, v_b_q_norm, v_b_k_norm, v_b_out_norm, v_c_conv_w, v_c_conv_b, v_c_a_log, v_c_dt_bias, v_c_d_skip, v_c_out_norm, v_d_conv_w, v_d_a_log, v_d_dt_bias, v_d_out_norm, v_w_out, v_post_mix_norm, v_pre_ffn_norm, v_f_w_in, v_f_conv_w, v_f_conv_b, v_f_w_out, v_post_ffn_norm):
    given = dict(locals())
    w = {n: given[n] for n in _WEIGHTS}
    mom = {n: given["m_" + n] for n in _WEIGHTS}
    var = {n: given["v_" + n] for n in _WEIGHTS}
    me = 4 * lax.axis_index("x") + 2 * lax.axis_index("y") + lax.axis_index("c")
    layered = lambda names: [(l, n) for l in range(DEPTH) for n in names]

    sharded = list(_BIG) + _CONV
    early = [n for n in sharded if n not in _LATE]
    weight_items = lambda l, names: [(w[n][l].astype(MXU_DTYPE if n in _BIG else f32), True) for n in names]
    grad_items = lambda g, names: [(_split(g[n], _BIG[n]).astype(WIRE_DTYPE), False) for n in names]
    P = {n: [w[n][l] for l in range(DEPTH)] for n in _REP}
    P.update({n: [None] * DEPTH for n in sharded})

    def place(l, names, gathered):
        for n, got in zip(names, gathered):
            P[n][l] = _join(got, _BIG.get(n, 1))

    def late_weights_l0(gathered):
        place(0, _LATE, gathered)
        return _late_kernel_weights(P, 0, MXU_DTYPE)

    tabs = _rope_tables(x.shape[1])
    place(0, early, exchange(weight_items(0, early), "gather_weights"))
    x1, s0, got, W0 = _layer_fwd(x[0], _kernel_weights(P, 0, MXU_DTYPE), tabs, "l0_", weight_items(1, sharded),
                                 weight_items(0, _LATE), late_weights_l0)
    place(1, sharded, got)
    x2, s1, _, W1 = _layer_fwd(x1, _kernel_weights(P, 1, MXU_DTYPE), tabs, "l1_")
    loss, dx = loss_and_grad(x2, loss_target[0], "loss")
    loss = lax.psum(loss[0, 0], ("x", "y", "c"))

    def late_grads_l0(g):
        return grad_items({"w_out": g["w_out"], "f_w_in": _ff_deinterleave(g["f_win"]), "f_w_out": g["f_wout"]}, _LATE)

    dx, g1, _, _ = _layer_bwd(s1, dx, W1, tabs, "l1_")
    g1 = _reference_grads(g1)
    dx, g0, recv1, recv0_late = _layer_bwd(s0, dx, W0, tabs, "l0_", grad_items(g1, _BIG), late_grads_l0)
    grads = [_reference_grads(g0), g1]
    small = _REP + _CONV
    full_shapes = [grads[l][n].shape for l, n in layered(small)]
    first = [n for n in _BIG if n not in _LATE]
    *recv0_first, gsmall = exchange(grad_items(grads[0], first) + [(_pack([grads[l][n] for l, n in layered(small)], SMALLG_ROWS), True)],
                                    "exchange_grads")
    recv0 = dict(zip(first + _LATE, recv0_first + recv0_late))

    out = {}
    for n, got1 in zip(_BIG, recv1):
        out[n] = adamw([recv0[n], got1], w[n], mom[n], var[n], "adamw_" + n)
    gsum = dict(zip(layered(small), _unpack(sum_slots(gsmall, "sum_small_grads"), full_shapes)))
    for l, n in layered(_CONV):
        cols = w[n].shape[2]
        gsum[(l, n)] = lax.dynamic_slice_in_dim(gsum[(l, n)], me * cols, cols, axis=1)
    held_shapes = [w[n].shape[1:] for _, n in layered(small)]
    pack_small = lambda d: _pack([d[n][l] for l, n in layered(small)], SMALLW_ROWS)[None]
    res_small = adamw([_pack([gsum[k] for k in layered(small)], SMALLW_ROWS)[None]], pack_small(w), pack_small(mom),
                      pack_small(var), "adamw_small")
    per_kind = [dict(zip(layered(small), _unpack(buf[0], held_shapes))) for buf in res_small]
    for n in small:
        out[n] = [jnp.stack([per_kind[kind][(l, n)] for l in range(DEPTH)]) for kind in range(4)]
    return (loss, dx[None], *[out[n][kind] for kind in range(4) for n in _WEIGHTS])
```

```python
import functools
import math

import jax
import jax.numpy as jnp
from jax import lax
from jax.experimental import pallas as pl
from jax.experimental.pallas import tpu as pltpu

f32 = jnp.float32
MXU_DTYPE = jnp.bfloat16
WIRE_DTYPE = jnp.bfloat16
HI = lax.Precision.HIGHEST

D_MODEL = 1024
DEPTH = 2
GRID_W = 64
ROPE_BASE = 10000.0
EPS = 1e-6
A_Q_LORA, A_KV_LORA, A_ROPE, A_NOPE = 192, 128, 32, 64
SSD_CHUNK = 256
DN_CHUNK = 64
HEAD = 64
HPAD = 128
D_FF = 2816
FF_BLK = 256
N_DEV = 8
ADAM_LR, ADAM_B1, ADAM_B2, ADAM_EPS, ADAM_WD, ADAM_STEP = 0.001, 0.9, 0.999, 1e-08, 0.01, 10

V7X_VMEM_BYTES = 64 * 2 ** 20
VMEM_LIMIT = (V7X_VMEM_BYTES * 3) // 4

PIN = 3072
SEG_A, SEG_B, SEG_CX, SEG_DX, SEG_CZ, SEG_DZ, SEG_CDT, SEG_DAB = 0, 512, 1024, 1536, 2304, 2560, 2816, 2944


def _tile(n, pref):
    for t in (512, 256, 128, 64, 32, 16, 8):
        if t <= pref and n % t == 0:
            return t
    raise ValueError(f"no tile for {n}")


LANE = 128


def _tile_div(n, pref):
    if n <= pref:
        return n
    return max(d for d in range(LANE, pref + 1, LANE) if n % d == 0)


def _params(sem=None):
    return pltpu.CompilerParams(vmem_limit_bytes=VMEM_LIMIT, dimension_semantics=sem)


def _tup(r):
    return tuple(r) if isinstance(r, (tuple, list)) else (r,)


def matmul(a, b, form, out_dtype, name):
    if form == "nn":
        (M, K), N = a.shape, b.shape[1]
    elif form == "nt":
        (M, K), N = a.shape, b.shape[0]
    else:
        (K, M), N = a.shape, b.shape[1]
    if form == "tn":
        tm, tn, tk = (512 if M % 512 == 0 else _tile_div(M, 1408)), _tile_div(N, 1408), _tile_div(K, 2048)
    else:
        tm, tn, tk = _tile_div(M, 512), _tile_div(N, 2816), _tile_div(K, 5632 if jnp.dtype(a.dtype).itemsize == 2 else 2816)
    nk = K // tk
    dims = {"nn": ((1,), (0,)), "nt": ((1,), (1,)), "tn": ((0,), (0,))}[form]

    def body(a_ref, b_ref, o_ref, *acc):
        part = lax.dot_general(a_ref[...].astype(MXU_DTYPE), b_ref[...].astype(MXU_DTYPE), (dims, ((), ())),
                               preferred_element_type=f32)
        if nk == 1:
            o_ref[...] = part.astype(out_dtype)
            return
        (acc_ref,) = acc
        k = pl.program_id(2)

        @pl.when(k == 0)
        def _():
            acc_ref[...] = part

        @pl.when((k > 0) & (k < nk - 1))
        def _():
            acc_ref[...] += part

        @pl.when(k == nk - 1)
        def _():
            o_ref[...] = (acc_ref[...] + part).astype(out_dtype)

    a_spec = pl.BlockSpec((tk, tm), lambda i, j, k: (k, i)) if form == "tn" else pl.BlockSpec((tm, tk), lambda i, j, k: (i, k))
    b_spec = pl.BlockSpec((tn, tk), lambda i, j, k: (j, k)) if form == "nt" else pl.BlockSpec((tk, tn), lambda i, j, k: (k, j))
    return pl.pallas_call(
        body, name=name, grid=(M // tm, N // tn, nk), in_specs=[a_spec, b_spec],
        out_specs=pl.BlockSpec((tm, tn), lambda i, j, k: (i, j)), out_shape=jax.ShapeDtypeStruct((M, N), out_dtype),
        scratch_shapes=[pltpu.VMEM((tm, tn), f32)] if nk > 1 else [], compiler_params=_params(("parallel", "parallel", "arbitrary")),
    )(a, b)


def _row_specs(rows, tm):
    return [pl.BlockSpec((tm, w), lambda i, b=b: (i, b)) for (_, w, b) in rows]


def _full_specs(params):
    return [pl.BlockSpec(p.shape, lambda i: (0, 0)) for p in params]


def rowwise(fn, rows, params, outs, name, tm=512):
    L = rows[0][0].shape[0]
    tm = _tile(L, tm)
    n_in = len(rows) + len(params)

    def body(*refs):
        res = _tup(fn(*[r[...].astype(f32) for r in refs[:n_in]]))
        for o_ref, r in zip(refs[n_in:], res, strict=True):
            o_ref[...] = r.astype(o_ref.dtype)

    res = pl.pallas_call(
        body, name=name, grid=(L // tm,), in_specs=_row_specs(rows, tm) + _full_specs(params),
        out_specs=[pl.BlockSpec((tm, w), lambda i: (i, 0)) for (w, _) in outs],
        out_shape=[jax.ShapeDtypeStruct((L, w), dt) for (w, dt) in outs], compiler_params=_params(("parallel",)),
    )(*[r[0] for r in rows], *params)
    return list(res)


def rowwise_vjp(fn, rows, params, cts, name, row_grads, param_grads, tm=512, row_dtypes=None):
    L = rows[0][0].shape[0]
    tm = _tile(L, tm)
    nr, npar, nct = len(rows), len(params), len(cts)
    row_dtypes = [f32] * len(row_grads) if row_dtypes is None else row_dtypes

    def body(*refs):
        i = pl.program_id(0)
        rv = [r[...].astype(f32) for r in refs[:nr]]
        pv = [r[...].astype(f32) for r in refs[nr:nr + npar]]
        cv = tuple(r[...].astype(f32) for r in refs[nr + npar:nr + npar + nct])
        out_refs = refs[nr + npar + nct:]

        def g(*diff):
            rr, pp = list(rv), list(pv)
            for k, v in zip(row_grads, diff[:len(row_grads)]):
                rr[k] = v
            for k, v in zip(param_grads, diff[len(row_grads):]):
                pp[k] = v
            return _tup(fn(*rr, *pp))

        _, vjp = jax.vjp(g, *[rv[k] for k in row_grads], *[pv[k] for k in param_grads])
        grads = vjp(cv)
        for o_ref, gval in zip(out_refs[:len(row_grads)], grads[:len(row_grads)]):
            o_ref[...] = gval.astype(o_ref.dtype)
        for o_ref, gval in zip(out_refs[len(row_grads):], grads[len(row_grads):]):
            @pl.when(i == 0)
            def _(o_ref=o_ref, gval=gval):
                o_ref[...] = gval

            @pl.when(i > 0)
            def _(o_ref=o_ref, gval=gval):
                o_ref[...] += gval

    out_specs = [pl.BlockSpec((tm, rows[k][1]), lambda i: (i, 0)) for k in row_grads] + \
                [pl.BlockSpec(params[k].shape, lambda i: (0, 0)) for k in param_grads]
    out_shape = [jax.ShapeDtypeStruct((L, rows[k][1]), dt) for k, dt in zip(row_grads, row_dtypes)] + \
                [jax.ShapeDtypeStruct(params[k].shape, f32) for k in param_grads]
    res = pl.pallas_call(
        body, name=name, grid=(L // tm,), in_specs=_row_specs(rows, tm) + _full_specs(params) + _row_specs(cts, tm),
        out_specs=out_specs, out_shape=out_shape, compiler_params=_params(("arbitrary",)),
    )(*[r[0] for r in rows], *params, *[c[0] for c in cts])
    res = list(res)
    return res[:len(row_grads)], res[len(row_grads):]


def _rms(x, w, n=None):
    n = x.shape[-1] if n is None else n
    return x * lax.rsqrt(jnp.sum(x * x, axis=-1, keepdims=True) * (1.0 / n) + EPS) * w


def _silu(x):
    return x * jax.nn.sigmoid(x)


def _softplus(x):
    return jnp.maximum(x, 0.0) + jnp.log1p(jnp.exp(-jnp.abs(x)))


def _mm(a, b, dims, precision=None):
    if precision is None:
        a, b = a.astype(MXU_DTYPE), b.astype(MXU_DTYPE)
    return lax.dot_general(a, b, (dims, ((), ())), precision=precision, preferred_element_type=f32)


_NN, _NT, _TN = ((1,), (0,)), ((1,), (1,)), ((0,), (0,))


@functools.partial(jax.custom_vjp, nondiff_argnums=(1, 2))
def _roll(x, shift, axis):
    return pltpu.roll(x, shift, axis)


def _roll_fwd(x, shift, axis):
    return pltpu.roll(x, shift, axis), None


def _roll_bwd(shift, axis, _, ct):
    return (pltpu.roll(ct, (ct.shape[axis] - shift) % ct.shape[axis], axis),)


_roll.defvjp(_roll_fwd, _roll_bwd)


def _rope(x, cos, sin, lo, half):
    n = x.shape[-1]
    lane = lax.broadcasted_iota(jnp.int32, x.shape, x.ndim - 1) % HPAD - lo
    first = ((lane >= 0) & (lane < half)) | ((lane >= 2 * half) & (lane < 3 * half))
    rot = jnp.where(first, -_roll(x, n - half, x.ndim - 1), _roll(x, half, x.ndim - 1))
    return x * cos + rot * sin


def _heads(x, n, width=HEAD):
    return [x[:, h * width:(h + 1) * width] for h in range(n)]


def _pad_heads(hs):
    z = jnp.zeros_like(hs[0])
    return jnp.concatenate([t for h in hs for t in (h, z)], axis=-1)


def _unpad_heads(x, n):
    return jnp.concatenate([x[:, h * HPAD:h * HPAD + HEAD] for h in range(n)], axis=-1)


HALO = 8


def _conv_specs(tm, tc, c0, L):
    nh = tm // HALO
    last = L // HALO - 1
    return [pl.BlockSpec((tm, tc), lambda i, j: (i, c0 + j)),
            pl.BlockSpec((HALO, tc), lambda i, j: (jnp.maximum(i * nh - 1, 0), c0 + j)),
            pl.BlockSpec((HALO, tc), lambda i, j: (jnp.minimum((i + 1) * nh, last), c0 + j))]


def _shift3(x):
    n = x.shape[0]
    return _roll(x, 1, 0), _roll(x, n - 1, 0)


def conv_fwd(gfn, x, c0, ncol, tc, w, b, tco, out_dtype, name, tm=512):
    L = x.shape[0]
    tm = _tile(L, tm)
    ni = L // tm

    def body(x_ref, p_ref, n_ref, w_ref, b_ref, o_ref, c_ref):
        i = pl.program_id(0)
        xv = x_ref[...].astype(f32)
        xp = jnp.where(i == 0, 0.0, p_ref[HALO - 1:HALO, :].astype(f32))
        xn = jnp.where(i == ni - 1, 0.0, n_ref[0:1, :].astype(f32))
        rid = lax.broadcasted_iota(jnp.int32, xv.shape, 0)
        dn, up = _shift3(xv)
        dn = jnp.where(rid == 0, xp, dn)
        up = jnp.where(rid == tm - 1, xn, up)
        c = w_ref[0:1, :] * dn + w_ref[1:2, :] * xv + w_ref[2:3, :] * up + b_ref[...]
        c_ref[...] = c
        o_ref[...] = gfn(c).astype(o_ref.dtype)

    return pl.pallas_call(
        body, name=name, grid=(ni, ncol),
        in_specs=_conv_specs(tm, tc, c0, L) + [pl.BlockSpec((3, tc), lambda i, j: (0, j)), pl.BlockSpec((1, tc), lambda i, j: (0, j))],
        out_specs=[pl.BlockSpec((tm, tco), lambda i, j: (i, j)), pl.BlockSpec((tm, tc), lambda i, j: (i, j))],
        out_shape=[jax.ShapeDtypeStruct((L, ncol * tco), out_dtype), jax.ShapeDtypeStruct((L, ncol * tc), f32)],
        compiler_params=_params(("parallel", "parallel")),
    )(x, x, x, w, b)


def conv_bwd(gfn, x, c, c0, ncol, tc, w, dys, tco, name, tm=512, dx_dtype=f32):
    L = x.shape[0]
    tm = _tile(L, tm)
    ni = L // tm
    nd = len(dys)

    def body(*refs):
        x_ref, xp_ref, xn_ref, c_ref, cp_ref, cn_ref = refs[:6]
        d_refs = refs[6:6 + 3 * nd]
        w_ref, dx_ref, dw_ref, db_ref = refs[6 + 3 * nd:]
        i = pl.program_id(1)
        first, lastb = i == 0, i == ni - 1

        def ext(m, p, n):
            return jnp.concatenate([jnp.where(first, 0.0, p[...].astype(f32)), m[...].astype(f32),
                                    jnp.where(lastb, 0.0, n[...].astype(f32))], axis=0)

        xe = ext(x_ref, xp_ref, xn_ref)
        de = ext(*d_refs[:3])
        for q in range(1, nd):
            de = de + ext(*d_refs[3 * q:3 * q + 3])
        x_dn, x_up = _shift3(xe)
        w0, w1, w2 = w_ref[0:1, :], w_ref[1:2, :], w_ref[2:3, :]
        _, vjp = jax.vjp(gfn, ext(c_ref, cp_ref, cn_ref))
        (dce,) = vjp(de)
        dc_dn, dc_up = _shift3(dce)
        dx_ref[...] = (w0 * dc_up + w1 * dce + w2 * dc_dn)[HALO:HALO + tm, :].astype(dx_ref.dtype)
        inner = slice(HALO, HALO + tm)
        dci = dce[inner]
        dw = jnp.concatenate([jnp.sum(dci * x_dn[inner], axis=0, keepdims=True), jnp.sum(dci * xe[inner], axis=0, keepdims=True),
                              jnp.sum(dci * x_up[inner], axis=0, keepdims=True)], axis=0)
        db = jnp.sum(dci, axis=0, keepdims=True)

        @pl.when(first)
        def _():
            dw_ref[...] = dw
            db_ref[...] = db

        @pl.when(i > 0)
        def _():
            dw_ref[...] += dw
            db_ref[...] += db

    res = pl.pallas_call(
        body, name=name, grid=(ncol, ni),
        in_specs=[pl.BlockSpec(s.block_shape, (lambda j, i, f=s.index_map: f(i, j))) for s in _conv_specs(tm, tc, c0, L)]
        + [pl.BlockSpec(s.block_shape, (lambda j, i, f=s.index_map: f(i, j))) for s in _conv_specs(tm, tc, 0, L)]
        + [pl.BlockSpec(s.block_shape, (lambda j, i, f=s.index_map: f(i, j))) for s in _conv_specs(tm, tco, 0, L)] * nd
        + [pl.BlockSpec((3, tc), lambda j, i: (0, j))],
        out_specs=[pl.BlockSpec((tm, tc), lambda j, i: (i, j)), pl.BlockSpec((3, tc), lambda j, i: (0, j)),
                   pl.BlockSpec((1, tc), lambda j, i: (0, j))],
        out_shape=[jax.ShapeDtypeStruct((L, ncol * tc), dx_dtype), jax.ShapeDtypeStruct((3, ncol * tc), f32),
                   jax.ShapeDtypeStruct((1, ncol * tc), f32)],
        compiler_params=_params(("parallel", "arbitrary")),
    )(x, x, x, c, c, c, *[d for d in dys for _ in range(3)], w)
    return res


LOG2E = math.log2(math.e)


def flash_fwd(q, k, v, n_q, n_kv, scale, name, side=(), tq=512, tk=8192):
    L = q.shape[0]
    tq, tk = _tile_div(L, tq), _tile_div(L, tk)
    rep = n_q // n_kv
    nkv = L // tk
    c2 = scale * LOG2E

    def body(q_ref, k_ref, v_ref, o_ref, lse_ref):
        qv = q_ref[...]
        ones_lane = lax.broadcasted_iota(jnp.int32, (tk, HPAD), 1) == HPAD - 1

        def step(c, carry):
            m, acc = carry
            off = pl.multiple_of(c * tk, tk)
            s = _mm(qv, k_ref[pl.ds(off, tk), :], _NT)
            m_new = jnp.maximum(m, jnp.max(s, axis=-1, keepdims=True))
            p = jnp.exp2((s - m_new) * c2)
            vv = v_ref[pl.ds(off, tk), :]
            acc = jnp.exp2((m - m_new) * c2) * acc + _mm(p, jnp.where(ones_lane, jnp.ones_like(vv), vv), _NN)
            return m_new, acc

        m, acc = lax.fori_loop(0, nkv, step, (jnp.full((tq, 1), -jnp.inf, f32), jnp.zeros((tq, HPAD), f32)))
        l = acc[:, HPAD - 1:]
        o_ref[...] = jnp.where(lax.broadcasted_iota(jnp.int32, (tq, HPAD), 1) == HPAD - 1, 0.0, acc / l)
        lse_ref[0] = m * scale + jnp.log(l)

    grid = (n_q, L // tq)
    body, ex = _with_exchange(body, 3, 2, side, grid)
    res = pl.pallas_call(
        body, name=name, grid=grid,
        in_specs=[pl.BlockSpec((tq, HPAD), lambda h, i: (i, h)), pl.BlockSpec((L, HPAD), lambda h, i: (0, h // rep)),
                  pl.BlockSpec((L, HPAD), lambda h, i: (0, h // rep))] + ex.specs,
        out_specs=[pl.BlockSpec((tq, HPAD), lambda h, i: (i, h)), pl.BlockSpec((1, tq, 1), lambda h, i: (h, i, 0))] + ex.specs,
        out_shape=[jax.ShapeDtypeStruct((L, n_q * HPAD), f32), jax.ShapeDtypeStruct((n_q, L, 1), f32)] + ex.out_shape,
        scratch_shapes=ex.scratch, compiler_params=_params(("arbitrary", "arbitrary") if side else ("parallel", "parallel")),
    )(q, k, v, *ex.args)
    return res[0], res[1], list(res[2:])


def flash_bwd(q, k, v, o, lse, do, n_q, n_kv, scale, name, side=(), tq=2048, tk=1024):
    L = q.shape[0]
    tq, tk = _tile_div(L, tq), _tile_div(L, tk)
    rep = n_q // n_kv
    c2 = scale * LOG2E

    def body(k_ref, v_ref, q_ref, do_ref, o_ref, lse_ref, dq_ref, dk_ref, dv_ref):
        j, i = pl.program_id(1), pl.program_id(2)
        kv, vv = k_ref[...], v_ref[...]
        rows = pl.ds(pl.multiple_of(i * tq, tq), tq)
        dk, dv = jnp.zeros((tk, HPAD), f32), jnp.zeros((tk, HPAD), f32)
        for r in range(rep):
            cols = slice(r * HPAD, (r + 1) * HPAD)
            qv = q_ref[:, cols]
            dov = do_ref[:, cols]
            delta = jnp.sum(dov * o_ref[:, cols], axis=-1, keepdims=True)
            p = jnp.exp2(_mm(qv, kv, _NT) * c2 - lse_ref[r] * LOG2E)
            dv = dv + _mm(p, dov, _TN)
            ds = p * (_mm(dov, vv, _NT) - delta) * scale
            dk = dk + _mm(ds, qv, _TN)
            dq = _mm(ds, kv, _NN)

            @pl.when(j == 0)
            def _(dq=dq, cols=cols):
                dq_ref[rows, cols] = dq

            @pl.when(j > 0)
            def _(dq=dq, cols=cols):
                dq_ref[rows, cols] += dq

        @pl.when(i == 0)
        def _():
            dk_ref[...] = dk
            dv_ref[...] = dv

        @pl.when(i > 0)
        def _():
            dk_ref[...] += dk
            dv_ref[...] += dv

    w = rep * HPAD
    grid = (n_kv, L // tk, L // tq)
    body, ex = _with_exchange(body, 6, 3, side, grid)
    res = pl.pallas_call(
        body, name=name, grid=grid,
        in_specs=[pl.BlockSpec((tk, HPAD), lambda g, j, i: (j, g)), pl.BlockSpec((tk, HPAD), lambda g, j, i: (j, g)),
                  pl.BlockSpec((tq, w), lambda g, j, i: (i, g)), pl.BlockSpec((tq, w), lambda g, j, i: (i, g)),
                  pl.BlockSpec((tq, w), lambda g, j, i: (i, g)), pl.BlockSpec((rep, tq, 1), lambda g, j, i: (g, i, 0))] + ex.specs,
        out_specs=[pl.BlockSpec((L, w), lambda g, j, i: (0, g)), pl.BlockSpec((tk, HPAD), lambda g, j, i: (j, g)),
                   pl.BlockSpec((tk, HPAD), lambda g, j, i: (j, g))] + ex.specs,
        out_shape=[jax.ShapeDtypeStruct((L, n_q * HPAD), f32), jax.ShapeDtypeStruct((L, n_kv * HPAD), f32),
                   jax.ShapeDtypeStruct((L, n_kv * HPAD), f32)] + ex.out_shape,
        scratch_shapes=ex.scratch, compiler_params=_params(("arbitrary",) * 3 if side else ("parallel", "arbitrary", "arbitrary")),
    )(k, v, q, do, o, lse, *ex.args)
    return res[0], res[1], res[2], list(res[3:])


N_SCAN_HEADS = 4
STATE_ROWS = N_SCAN_HEADS * HEAD


def _tri(n, rev):
    i = lax.broadcasted_iota(jnp.int32, (n, n), 0)
    k = lax.broadcasted_iota(jnp.int32, (n, n), 1)
    return ((k >= i), (k > i)) if rev else ((k <= i), (k < i))


def _decay(acol, incl):
    n = acol.shape[0]
    m1 = jnp.broadcast_to(acol, (n, n))
    return jnp.exp(jnp.where(incl, m1 - m1.T, -jnp.inf))


def _ssd_chunk(S, xbc, win, a_log, dt_bias, rev):
    Q = xbc.shape[0]
    d = 1 if rev else 0
    incl, _ = _tri(Q, rev)
    dt = _softplus(win[:, 4 * d:4 * d + 4] + dt_bias[d:d + 1, :])
    a = dt * (-jnp.exp(a_log[d:d + 1, :]))
    acum = _mm(incl.astype(f32), a, _NN, HI)
    tot = jnp.sum(a, axis=0, keepdims=True)
    xs, Bm, Cm = xbc[:, :256], xbc[:, 256:384], xbc[:, 384:512]
    H, W = N_SCAN_HEADS, N_SCAN_HEADS * HEAD
    lanes = lambda x: _cat([jnp.broadcast_to(x[:, h:h + 1], (x.shape[0], HEAD)) for h in range(H)])
    xdt = xs * lanes(dt)
    cb = [_mm(Cm[:, g * HEAD:(g + 1) * HEAD], Bm[:, g * HEAD:(g + 1) * HEAD], _NT) for g in range(2)]
    scores = _cat([cb[h // 2] * _decay(acum[:, h:h + 1], incl) for h in range(H)])
    own = (lax.broadcasted_iota(jnp.int32, (H * Q, W), 0) // Q) == (lax.broadcasted_iota(jnp.int32, (H * Q, W), 1) // HEAD)
    y = _mm(scores, jnp.where(own, jnp.concatenate([xdt] * H, axis=0), 0.0), _NN)
    grp = (lax.broadcasted_iota(jnp.int32, (W, 2 * HEAD), 0) // (2 * HEAD)) == (lax.broadcasted_iota(jnp.int32, (W, 2 * HEAD), 1) // HEAD)
    y = y + _mm(Cm, jnp.where(grp, _cat([S, S]), 0.0), _NT) * jnp.exp(lanes(acum))
    st = _mm(xdt * jnp.exp(lanes(tot - acum)), Bm, _TN)
    first = lax.broadcasted_iota(jnp.int32, (W, HEAD), 0) < 2 * HEAD
    exp_tot = jnp.concatenate([jnp.broadcast_to(jnp.exp(tot[:, h:h + 1]), (HEAD, 1)) for h in range(H)], axis=0)
    return S * exp_tot + jnp.where(first, st[:, :HEAD], st[:, HEAD:]), y


@functools.partial(jax.custom_vjp, nondiff_argnums=(1,))
def _inv_unit_tri(Lm, order):
    n = Lm.shape[0]
    eye = (lax.broadcasted_iota(jnp.int32, (n, n), 0) == lax.broadcasted_iota(jnp.int32, (n, n), 1)).astype(f32)
    P = -Lm
    T = eye + P
    k = 1
    while 2 * k < order:
        P = _mm(P, P, _NN, lax.Precision.HIGH)
        T = T + _mm(T, P, _NN, lax.Precision.HIGH)
        k *= 2
    return T


def _inv_unit_tri_fwd(Lm, order):
    T = _inv_unit_tri(Lm, order)
    return T, T


def _inv_unit_tri_bwd(order, T, dT):
    return (-_mm(_mm(T, dT, _TN, lax.Precision.HIGH), T, _NT, lax.Precision.HIGH),)


_inv_unit_tri.defvjp(_inv_unit_tri_fwd, _inv_unit_tri_bwd)


@jax.custom_vjp
def _inv_known(Lm, T):
    return T


def _inv_known_fwd(Lm, T):
    return T, T


def _inv_known_bwd(T, dT):
    return _inv_unit_tri_bwd(None, T, dT) + (jnp.zeros_like(T),)


_inv_known.defvjp(_inv_known_fwd, _inv_known_bwd)


def _dn_chunk(S, qkv, win, a_log, dt_bias, rev, kept=None):
    Q = qkv.shape[0]
    H, R = N_SCAN_HEADS, N_SCAN_HEADS * qkv.shape[0]
    d = 1 if rev else 0
    beta = jax.nn.sigmoid(win[:, 4 * d:4 * d + 4])
    gl = -jnp.exp(a_log[d:d + 1, :]) * _softplus(win[:, 8 + 4 * d:12 + 4 * d] + dt_bias[d:d + 1, :])
    G = _mm(_tri(Q, rev)[0].astype(f32), gl, _NN, HI)
    tot = jnp.sum(gl, axis=0, keepdims=True)
    rows = lambda x: jnp.concatenate(_heads(x, H), axis=0)
    col = lambda x: jnp.concatenate([x[:, h:h + 1] for h in range(H)], axis=0)
    per_head = lambda x, n: jnp.concatenate([jnp.broadcast_to(x[:, h:h + 1], (n, 1)) for h in range(H)], axis=0)
    r = lax.broadcasted_iota(jnp.int32, (R, R), 0)
    c = lax.broadcasted_iota(jnp.int32, (R, R), 1)
    same = (r // Q) == (c // Q)
    incl = same & ((c >= r) if rev else (c <= r))
    strict = same & ((c > r) if rev else (c < r))
    own = (lax.broadcasted_iota(jnp.int32, (R, H * HEAD), 0) // Q) == (lax.broadcasted_iota(jnp.int32, (R, H * HEAD), 1) // HEAD)
    blk = lambda x: jnp.where(own, _cat([x] * H), 0.0)
    q, k, v = rows(qkv[:, :256]) * (HEAD ** -0.5), rows(qkv[:, 256:512]), rows(qkv[:, 512:])
    b, Gs, tots = col(beta), col(G), per_head(tot, Q)
    dec = _decay(Gs, incl)
    kb = k * b
    Lm = jnp.where(strict, _mm(kb, k, _NT) * dec, 0.0)
    T = _inv_unit_tri(Lm, Q) if kept is None else _inv_known(Lm, blk(kept))
    eG = jnp.exp(Gs)
    uw = _mm(T, _cat([v * b, kb * eG]), _NN)
    vnew = uw[:, :HEAD] - _mm(blk(uw[:, HEAD:]), S, _NN)
    o = _mm(blk(q * eG), S, _NN) + _mm(_mm(q, k, _NT) * dec, vnew, _NN)
    S_next = S * jnp.exp(per_head(tot, HEAD)) + _mm(blk(k * jnp.exp(tots - Gs)), vnew, _TN)
    out = S_next, _cat([o[h * Q:(h + 1) * Q, :] for h in range(H)])
    if kept is None:
        out += (jnp.concatenate([T[h * Q:(h + 1) * Q, h * Q:(h + 1) * Q] for h in range(H)], axis=0),)
    return out


def scan_fwd(chunk_fn, Q, x, p, win_blk, a_log, dt_bias, name, keeps=False):
    L, W = x.shape
    nc = L // Q
    fidx, ridx = (lambda t: t), (lambda t: nc - 1 - t)

    def body(xf_ref, xr_ref, wf_ref, wr_ref, al_ref, db_ref, yf_ref, yr_ref, sf_ref, sr_ref, *rest):
        (sf_scr, sr_scr), keep_refs = rest[-2:], rest[:-2]

        @pl.when(pl.program_id(0) == 0)
        def _():
            sf_scr[...] = jnp.zeros_like(sf_scr)
            sr_scr[...] = jnp.zeros_like(sr_scr)

        for d, (x_ref, w_ref, y_ref, sin_ref, s_scr) in enumerate(((xf_ref, wf_ref, yf_ref, sf_ref, sf_scr),
                                                                   (xr_ref, wr_ref, yr_ref, sr_ref, sr_scr))):
            S = s_scr[...]
            sin_ref[...] = S
            res = chunk_fn(S, x_ref[...], w_ref[...], al_ref[...], db_ref[...], d == 1)
            s_scr[...], y_ref[...] = res[:2]
            if keeps:
                keep_refs[d][...] = res[2]

    row = lambda w, idx, b=0: pl.BlockSpec((Q, w), lambda t: (idx(t), b))
    small = pl.BlockSpec(a_log.shape, lambda t: (0, 0))
    state = lambda idx: pl.BlockSpec((STATE_ROWS, HEAD), lambda t: (idx(t), 0))
    n_state = 4 if keeps else 2
    return pl.pallas_call(
        body, name=name, grid=(nc,),
        in_specs=[row(W, fidx), row(W, ridx), row(HPAD, fidx, win_blk), row(HPAD, ridx, win_blk), small, small],
        out_specs=[row(STATE_ROWS, fidx), row(STATE_ROWS, ridx)] + [state(fidx), state(ridx)] * (n_state // 2),
        out_shape=[jax.ShapeDtypeStruct((L, STATE_ROWS), f32)] * 2 + [jax.ShapeDtypeStruct((nc * STATE_ROWS, HEAD), f32)] * n_state,
        scratch_shapes=[pltpu.VMEM((STATE_ROWS, HEAD), f32)] * 2, compiler_params=_params(("arbitrary",)),
    )(x, x, p, p, a_log, dt_bias)


def scan_bwd(chunk_fn, Q, x, p, win_blk, a_log, dt_bias, s_fwd, s_rev, dy, name, kept=()):
    L, W = x.shape
    nc = L // Q
    fidx, ridx = (lambda t: nc - 1 - t), (lambda t: t)
    nk = len(kept)

    def body(xf_ref, xr_ref, wf_ref, wr_ref, al_ref, db_ref, sf_ref, sr_ref, dyf_ref, dyr_ref, *rest):
        keep_refs = rest[:nk]
        dxf_ref, dxr_ref, dwf_ref, dwr_ref, dal_ref, ddb_ref, dsf_scr, dsr_scr = rest[nk:]

        @pl.when(pl.program_id(0) == 0)
        def _():
            dsf_scr[...] = jnp.zeros_like(dsf_scr)
            dsr_scr[...] = jnp.zeros_like(dsr_scr)
            dal_ref[...] = jnp.zeros_like(dal_ref)
            ddb_ref[...] = jnp.zeros_like(ddb_ref)

        dal_sum, ddb_sum = dal_ref[...], ddb_ref[...]
        for rev, x_ref, w_ref, sin_ref, dy_ref, dx_ref, dw_ref, ds_scr in (
                (False, xf_ref, wf_ref, sf_ref, dyf_ref, dxf_ref, dwf_ref, dsf_scr),
                (True, xr_ref, wr_ref, sr_ref, dyr_ref, dxr_ref, dwr_ref, dsr_scr)):
            extra = (keep_refs[int(rev)][...],) if nk else ()
            _, vjp = jax.vjp(lambda S, xv, wv, al, db, rev=rev, extra=extra: chunk_fn(S, xv, wv, al, db, rev, *extra),
                             sin_ref[...], x_ref[...], w_ref[...], al_ref[...], db_ref[...])
            ds_scr[...], dx_ref[...], dw_ref[...], dal, ddb = vjp((ds_scr[...], dy_ref[...]))
            dal_sum, ddb_sum = dal_sum + dal, ddb_sum + ddb
        dal_ref[...] = dal_sum
        ddb_ref[...] = ddb_sum

    row = lambda w, idx, b=0: pl.BlockSpec((Q, w), lambda t: (idx(t), b))
    small = pl.BlockSpec(a_log.shape, lambda t: (0, 0))
    state = lambda idx: pl.BlockSpec((STATE_ROWS, HEAD), lambda t: (idx(t), 0))
    return pl.pallas_call(
        body, name=name, grid=(nc,),
        in_specs=[row(W, fidx), row(W, ridx), row(HPAD, fidx, win_blk), row(HPAD, ridx, win_blk), small, small,
                  state(fidx), state(ridx), row(STATE_ROWS, fidx), row(STATE_ROWS, ridx)] + [state(fidx), state(ridx)][:nk],
        out_specs=[row(W, fidx), row(W, ridx), row(HPAD, fidx), row(HPAD, ridx), small, small],
        out_shape=[jax.ShapeDtypeStruct((L, W), f32)] * 2 + [jax.ShapeDtypeStruct((L, HPAD), f32)] * 2
        + [jax.ShapeDtypeStruct(a_log.shape, f32)] * 2,
        scratch_shapes=[pltpu.VMEM((STATE_ROWS, HEAD), f32)] * 2, compiler_params=_params(("arbitrary",)),
    )(x, x, p, p, a_log, dt_bias, s_fwd, s_rev, dy, dy, *kept)


def _cat(xs):
    return jnp.concatenate(xs, axis=-1)


def _a_prep(pa, cos, sin, qn, wuq, kvn, wk, wv):
    cq, ckv, kr = pa[:, :256], pa[:, 256:384], pa[:, 384:512]
    q = _rope(_mm(_rms(cq, qn, A_Q_LORA), wuq, _NN), _cat([cos] * 4), _cat([sin] * 4), A_NOPE, A_ROPE // 4)
    kvh = _rms(ckv, kvn)
    k = _mm(kvh, wk, _NN) + _cat([_rope(kr, cos, sin, A_NOPE, A_ROPE // 4)] * 4)
    return q, k, _mm(kvh, wv, _NN)


def _b_prep(pb, cos, sin, qn, kn):
    cos4, sin4 = _cat([cos] * 4), _cat([sin] * 4)
    q = _pad_heads([_rms(h, qn) for h in _heads(pb[:, :256], 4)])
    k = _pad_heads([_rms(h, kn) for h in _heads(pb[:, 256:384], 2) for _ in range(2)])
    v = _pad_heads([h for h in _heads(pb[:, 384:512], 2) for _ in range(2)])
    return _rope(q, cos4, sin4, 0, HEAD // 4), _rope(k, cos4, sin4, 0, HEAD // 4), v


def _c_act(c):
    return _silu(c)


def _d_act(c):
    s = _silu(c)
    return _cat([h * lax.rsqrt(jnp.sum(h * h, axis=-1, keepdims=True) + 1e-6) for h in _heads(s[:, :512], 8)] + [s[:, 512:]])


def _ffn_act(c):
    return _silu(c[:, :FF_BLK]) * c[:, FF_BLK:]


def _mix_post(oa, ob, yc_f, yc_r, xbc, cz, od_f, od_r, dz, a_out, b_out, dskip, c_out, d_out):
    o_a = _rms(_unpad_heads(oa, 4), a_out)
    o_b = _rms(_unpad_heads(ob, 4), b_out)
    skip = _cat([jnp.broadcast_to(dskip[:, h:h + 1], (1, HEAD)) for h in range(4)])
    o_c = _rms((yc_f + yc_r + xbc[:, :256] * skip) * _silu(cz), c_out)
    o_d = _cat([_rms(h, d_out) for h in _heads(od_f + od_r, 4)]) * _silu(dz)
    return _cat([o_a, o_b, o_c, o_d])


def _post_mix(x, y1, g_pm, g_pf):
    x1 = x + _rms(y1, g_pm)
    return x1, _rms(x1, g_pf)


def _post_ffn(x1, y2, g):
    return x1 + _rms(y2, g)


def _pre_bwd(x, g):
    return x, _rms(x, g)


def _layer_fwd(x, W, tabs, n, side=(), late_side=(), late_weights=None):
    mx = MXU_DTYPE
    cos_a, sin_a, cos_b, sin_b = tabs
    full = lambda a: (a, a.shape[1], 0)
    s = {"x": x}
    (s["h"],) = rowwise(_rms, [full(x)], [W["g_pre"]], [(D_MODEL, mx)], n + "pre")
    p = s["p"] = matmul(s["h"], W["w_in"], "nn", f32, n + "in")
    s["qa"], s["ka"], s["va"] = rowwise(_a_prep, [(p, 512, SEG_A // 512), full(cos_a), full(sin_a)],
                                        [W["a_qn"], W["a_wuq"], W["a_kvn"], W["a_wk"], W["a_wv"]], [(512, mx)] * 3, n + "a_prep")
    s["qb"], s["kb"], s["vb"] = rowwise(_b_prep, [(p, 512, SEG_B // 512), full(cos_b), full(sin_b)], [W["b_qn"], W["b_kn"]],
                                        [(512, mx)] * 3, n + "b_prep")
    s["oa"], s["lsea"], side_out = flash_fwd(s["qa"], s["ka"], s["va"], 4, 4, (A_NOPE + A_ROPE) ** -0.5, n + "a_attn", side)
    s["ob"], s["lseb"], late_out = flash_fwd(s["qb"], s["kb"], s["vb"], 4, 4, HEAD ** -0.5, n + "b_attn", late_side)
    if late_weights is not None:
        W = {**W, **late_weights(late_out)}
    xbc, s["c_c"] = conv_fwd(_c_act, p, SEG_CX // 512, 1, 512, W["c_cw"], W["c_cb"], 512, f32, n + "c_conv")
    s["xbc"] = xbc
    s["yc_f"], s["yc_r"], s["sc_f"], s["sc_r"] = scan_fwd(_ssd_chunk, min(SSD_CHUNK, xbc.shape[0]), xbc, p, SEG_CDT // HPAD, W["c_alog"], W["c_dtb"], n + "c_ssd")
    qkv, s["d_c"] = conv_fwd(_d_act, p, SEG_DX // 768, 1, 768, W["d_cw"], W["d_cb"], 768, f32, n + "d_conv")
    s["qkv"] = qkv
    s["od_f"], s["od_r"], s["sd_f"], s["sd_r"], s["td_f"], s["td_r"] = scan_fwd(
        _dn_chunk, DN_CHUNK, qkv, p, SEG_DAB // HPAD, W["d_alog"], W["d_dtb"], n + "d_dn", keeps=True)
    (s["omix"],) = rowwise(_mix_post, _mix_rows(s), _mix_params(W), [(D_MODEL, mx)], n + "mix_post")
    s["y1"] = matmul(s["omix"], W["w_out"], "nn", f32, n + "out")
    s["x1"], s["h2"] = rowwise(_post_mix, [full(x), full(s["y1"])], [W["g_pm"], W["g_pf"]], [(D_MODEL, f32), (D_MODEL, mx)], n + "post_mix")
    s["u"] = matmul(s["h2"], W["f_win"], "nn", f32, n + "f_in")
    s["a"], s["f_c"] = conv_fwd(_ffn_act, s["u"], 0, D_FF // FF_BLK, 2 * FF_BLK, W["f_cw"], W["f_cb"], FF_BLK, mx, n + "f_conv", tm=512)
    s["y2"] = matmul(s["a"], W["f_wout"], "nn", f32, n + "f_out")
    (x2,) = rowwise(_post_ffn, [full(s["x1"]), full(s["y2"])], [W["g_po"]], [(D_MODEL, f32)], n + "post_ffn")
    return x2, s, side_out, W


def _mix_rows(s):
    p = s["p"]
    return [(s["oa"], 512, 0), (s["ob"], 512, 0), (s["yc_f"], 256, 0), (s["yc_r"], 256, 0), (s["xbc"], 512, 0),
            (p, 256, SEG_CZ // 256), (s["od_f"], 256, 0), (s["od_r"], 256, 0), (p, 256, SEG_DZ // 256)]


def _mix_params(W):
    return [W["a_out"], W["b_out"], W["c_dskip"], W["c_out"], W["d_out"]]


def _layer_bwd(s, dx2, W, tabs, n, side=(), late_side=None):
    cos_a, sin_a, cos_b, sin_b = tabs
    full = lambda a: (a, a.shape[1], 0)
    p, g = s["p"], {}
    (dx1, dy2), (g["g_po"],) = rowwise_vjp(_post_ffn, [full(s["x1"]), full(s["y2"])], [W["g_po"]], [full(dx2)], n + "post_ffn_b", [0, 1], [0],
                                           row_dtypes=[f32, MXU_DTYPE])
    da = matmul(dy2, W["f_wout"], "nt", f32, n + "f_out_dx")
    g["f_wout"] = matmul(s["a"], dy2, "tn", f32, n + "f_out_dw")
    du, g["f_cw"], g["f_cb"] = conv_bwd(_ffn_act, s["u"], s["f_c"], 0, D_FF // FF_BLK, 2 * FF_BLK, W["f_cw"], [da], FF_BLK, n + "f_conv_b", tm=512,
                                        dx_dtype=MXU_DTYPE)
    dh2 = matmul(du, W["f_win"], "nt", f32, n + "f_in_dx")
    g["f_win"] = matmul(s["h2"], du, "tn", f32, n + "f_in_dw")
    (dx, dy1), (g["g_pm"], g["g_pf"]) = rowwise_vjp(_post_mix, [full(s["x"]), full(s["y1"])], [W["g_pm"], W["g_pf"]],
                                                   [full(dx1), full(dh2)], n + "post_mix_b", [0, 1], [0, 1],
                                                   row_dtypes=[f32, MXU_DTYPE])
    domix = matmul(dy1, W["w_out"], "nt", f32, n + "out_dx")
    g["w_out"] = matmul(s["omix"], dy1, "tn", f32, n + "out_dw")
    (doa, dob, dyc, dxbc_skip, dcz, dod, ddz), (g["a_out"], g["b_out"], g["c_dskip"], g["c_out"], g["d_out"]) = rowwise_vjp(
        _mix_post, _mix_rows(s), _mix_params(W), [full(domix)], n + "mix_post_b", [0, 1, 2, 4, 5, 6, 8], list(range(5)))
    dqkv_f, dqkv_r, ddab_f, ddab_r, g["d_alog"], g["d_dtb"] = scan_bwd(
        _dn_chunk, DN_CHUNK, s["qkv"], p, SEG_DAB // HPAD, W["d_alog"], W["d_dtb"], s["sd_f"], s["sd_r"], dod, n + "d_dn_b",
        kept=(s["td_f"], s["td_r"]))
    ddx, g["d_cw"], _ = conv_bwd(_d_act, p, s["d_c"], SEG_DX // 768, 1, 768, W["d_cw"], [dqkv_f, dqkv_r], 768, n + "d_conv_b")
    dxbc_f, dxbc_r, dcdt_f, dcdt_r, g["c_alog"], g["c_dtb"] = scan_bwd(
        _ssd_chunk, min(SSD_CHUNK, p.shape[0]), s["xbc"], p, SEG_CDT // HPAD, W["c_alog"], W["c_dtb"], s["sc_f"], s["sc_r"], dyc, n + "c_ssd_b")
    dcx, g["c_cw"], g["c_cb"] = conv_bwd(_c_act, p, s["c_c"], SEG_CX // 512, 1, 512, W["c_cw"], [dxbc_f, dxbc_r, dxbc_skip], 512,
                                         n + "c_conv_b")
    (dsmall,) = rowwise(lambda a, b, c, d: _cat([a + b, c + d]), [full(dcdt_f), full(dcdt_r), full(ddab_f), full(ddab_r)], [],
                        [(2 * HPAD, f32)], n + "dwin_sum")
    dqa, dka, dva, side_out = flash_bwd(s["qa"], s["ka"], s["va"], s["oa"], s["lsea"], doa, 4, 4, (A_NOPE + A_ROPE) ** -0.5,
                                        n + "a_attn_b", side)
    dqb, dkb, dvb, late_out = flash_bwd(s["qb"], s["kb"], s["vb"], s["ob"], s["lseb"], dob, 4, 4, HEAD ** -0.5, n + "b_attn_b",
                                        late_side(g) if late_side is not None else ())
    (dpa,), (g["a_qn"], g["a_wuq"], g["a_kvn"], g["a_wk"], g["a_wv"]) = rowwise_vjp(
        _a_prep, [(p, 512, SEG_A // 512), full(cos_a), full(sin_a)], [W["a_qn"], W["a_wuq"], W["a_kvn"], W["a_wk"], W["a_wv"]],
        [full(dqa), full(dka), full(dva)], n + "a_prep_b", [0], list(range(5)))
    (dpb,), (g["b_qn"], g["b_kn"]) = rowwise_vjp(_b_prep, [(p, 512, SEG_B // 512), full(cos_b), full(sin_b)], [W["b_qn"], W["b_kn"]],
                                               [full(dqb), full(dkb), full(dvb)], n + "b_prep_b", [0], [0, 1])
    dp = jnp.concatenate([dpa, dpb, dcx, ddx, dcz, ddz, dsmall], axis=1).astype(MXU_DTYPE)
    dh = matmul(dp, W["w_in"], "nt", f32, n + "in_dx")
    g["w_in"] = matmul(s["h"], dp, "tn", f32, n + "in_dw")
    (dx0,), (g["g_pre"],) = rowwise_vjp(_pre_bwd, [full(s["x"])], [W["g_pre"]], [full(dx), full(dh)], n + "pre_b", [0], [0])
    return dx0, g, side_out, late_out


def loss_and_grad(y, target, name):
    L, D = y.shape
    tm = _tile(L, 512)

    def body(y_ref, t_ref, loss_ref, dy_ref):
        e = y_ref[...] - t_ref[...]
        dy_ref[...] = e * (1.0 / D)
        part = 0.5 * jnp.sum(jnp.sum(e * e, axis=1, keepdims=True) * (1.0 / D), axis=0, keepdims=True)

        @pl.when(pl.program_id(0) == 0)
        def _():
            loss_ref[...] = part

        @pl.when(pl.program_id(0) > 0)
        def _():
            loss_ref[...] += part

    row = pl.BlockSpec((tm, D), lambda i: (i, 0))
    return pl.pallas_call(
        body, name=name, grid=(L // tm,), in_specs=[row, row], out_specs=[pl.BlockSpec((1, 1), lambda i: (0, 0)), row],
        out_shape=[jax.ShapeDtypeStruct((1, 1), f32), jax.ShapeDtypeStruct((L, D), f32)], compiler_params=_params(("arbitrary",)),
    )(y, target)


_IN_SEGS = [(0, 192), (256, 128), (384 + A_NOPE, 32), (512, 256), (768, 128), (896, 128), (SEG_CZ, 256), (SEG_CX, 512),
            (SEG_CDT, 8), (SEG_DX, 768), (SEG_DZ, 256), (SEG_DAB, 16)]


def _pad_in(w):
    src, pieces = 0, {}
    for off, wd in _IN_SEGS:
        pieces[off] = w[..., src:src + wd]
        src += wd
    out, pos = [], 0
    for off in sorted(pieces):
        if off > pos:
            out.append(jnp.zeros(w.shape[:-1] + (off - pos,), w.dtype))
        out.append(pieces[off])
        pos = off + pieces[off].shape[-1]
    out.append(jnp.zeros(w.shape[:-1] + (PIN - pos,), w.dtype))
    return jnp.concatenate(out, axis=-1)


def _unpad_in(wp):
    return jnp.concatenate([wp[..., off:off + wd] for off, wd in _IN_SEGS], axis=-1)


def _pad_last(a, n):
    return jnp.pad(a, [(0, 0)] * (a.ndim - 1) + [(0, n - a.shape[-1])])


def _ff_interleave(w):
    nb = D_FF // FF_BLK
    if w.size <= 4 * D_FF * 2:
        lead = w.shape[:-1]
        return jnp.stack([w[..., :D_FF].reshape(lead + (nb, FF_BLK)), w[..., D_FF:].reshape(lead + (nb, FF_BLK))],
                         axis=-2).reshape(lead + (2 * D_FF,))
    return jnp.concatenate([w[..., off + j * FF_BLK:off + (j + 1) * FF_BLK] for j in range(nb) for off in (0, D_FF)], axis=-1)


def _ff_deinterleave(w):
    nb = D_FF // FF_BLK
    if w.size <= 4 * D_FF * 2:
        lead = w.shape[:-1]
        t = w.reshape(lead + (nb, 2, FF_BLK))
        return jnp.concatenate([t[..., 0, :].reshape(lead + (D_FF,)), t[..., 1, :].reshape(lead + (D_FF,))], axis=-1)
    return jnp.concatenate([w[..., (2 * j + half) * FF_BLK:(2 * j + half + 1) * FF_BLK] for half in (0, 1) for j in range(nb)], axis=-1)


def _row(v):
    return v.reshape(1, -1)


_LATE = ["w_out", "f_w_in", "f_w_out"]


def _late_kernel_weights(P, l, wdt):
    return {"w_out": P["w_out"][l].astype(wdt), "f_win": _ff_interleave(P["f_w_in"][l]).astype(wdt), "f_wout": P["f_w_out"][l].astype(wdt)}


def _kernel_weights(P, l, wdt):
    late = {} if P["w_out"][l] is None else _late_kernel_weights(P, l, wdt)
    uq = P["a_w_uq"][l].reshape(A_Q_LORA, 4, A_NOPE + A_ROPE)
    ukv = P["a_w_ukv"][l].reshape(A_KV_LORA, 4, 2 * HEAD)
    z = jnp.zeros((A_KV_LORA, 4, HEAD), ukv.dtype)
    return {
        "g_pre": _row(P["pre_mix_norm"][l]), "w_in": _pad_in(P["w_in"][l]).astype(wdt),
        "a_qn": _pad_last(_row(P["a_q_norm"][l]), 256),
        "a_wuq": jnp.pad(_pad_last(uq, HPAD).reshape(A_Q_LORA, 4 * HPAD), ((0, 256 - A_Q_LORA), (0, 0))).astype(wdt),
        "a_kvn": _row(P["a_kv_norm"][l]),
        "a_wk": jnp.concatenate([ukv[..., :HEAD], z], axis=-1).reshape(A_KV_LORA, 4 * HPAD).astype(wdt),
        "a_wv": jnp.concatenate([ukv[..., HEAD:], z], axis=-1).reshape(A_KV_LORA, 4 * HPAD).astype(wdt),
        "a_out": _row(P["a_out_norm"][l]), "b_qn": _row(P["b_q_norm"][l]), "b_kn": _row(P["b_k_norm"][l]), "b_out": _row(P["b_out_norm"][l]),
        "c_cw": P["c_conv_w"][l], "c_cb": _row(P["c_conv_b"][l]), "c_alog": P["c_a_log"][l], "c_dtb": P["c_dt_bias"][l],
        "c_dskip": _row(P["c_d_skip"][l]), "c_out": _row(P["c_out_norm"][l]),
        "d_cw": P["d_conv_w"][l], "d_cb": jnp.zeros((1, 768), f32), "d_alog": P["d_a_log"][l], "d_dtb": P["d_dt_bias"][l],
        "d_out": _row(P["d_out_norm"][l]), "g_pm": _row(P["post_mix_norm"][l]), "g_pf": _row(P["pre_ffn_norm"][l]),
        "f_cw": _ff_interleave(P["f_conv_w"][l]), "f_cb": _row(_ff_interleave(P["f_conv_b"][l])),
        "g_po": _row(P["post_ffn_norm"][l]), **late,
    }


def _reference_grads(g):
    uq = g["a_wuq"][:A_Q_LORA].reshape(A_Q_LORA, 4, HPAD)[..., :A_NOPE + A_ROPE].reshape(A_Q_LORA, 4 * (A_NOPE + A_ROPE))
    wk = g["a_wk"].reshape(A_KV_LORA, 4, HPAD)[..., :HEAD]
    wv = g["a_wv"].reshape(A_KV_LORA, 4, HPAD)[..., :HEAD]
    return {
        "pre_mix_norm": g["g_pre"][0], "w_in": _unpad_in(g["w_in"]), "a_q_norm": g["a_qn"][0, :A_Q_LORA], "a_w_uq": uq,
        "a_kv_norm": g["a_kvn"][0], "a_w_ukv": jnp.concatenate([wk, wv], axis=-1).reshape(A_KV_LORA, 8 * HEAD),
        "a_out_norm": g["a_out"][0], "b_q_norm": g["b_qn"][0], "b_k_norm": g["b_kn"][0], "b_out_norm": g["b_out"][0],
        "c_conv_w": g["c_cw"], "c_conv_b": g["c_cb"][0], "c_a_log": g["c_alog"], "c_dt_bias": g["c_dtb"], "c_d_skip": g["c_dskip"][0],
        "c_out_norm": g["c_out"][0], "d_conv_w": g["d_cw"], "d_a_log": g["d_alog"], "d_dt_bias": g["d_dtb"], "d_out_norm": g["d_out"][0],
        "w_out": g["w_out"], "post_mix_norm": g["g_pm"][0], "pre_ffn_norm": g["g_pf"][0], "f_w_in": _ff_deinterleave(g["f_win"]),
        "f_conv_w": _ff_deinterleave(g["f_cw"]), "f_conv_b": _ff_deinterleave(g["f_cb"][0]), "f_w_out": g["f_wout"],
        "post_ffn_norm": g["g_po"][0],
    }


def _rope_tables(L):
    def tables(rot):
        rows = L // GRID_W
        row = jnp.repeat(jnp.arange(rows), GRID_W).astype(f32)
        col = jnp.tile(jnp.arange(GRID_W), rows).astype(f32)
        sec = rot // 2
        inv = ROPE_BASE ** (-jnp.arange(0, sec, 2, dtype=f32) / sec)
        ang = jnp.concatenate([row[:, None] * inv] * 2 + [col[:, None] * inv] * 2, axis=-1)
        return jnp.cos(ang), jnp.sin(ang)

    ca, sa = tables(A_ROPE)
    cb, sb = tables(HEAD)
    one, zero = jnp.ones, jnp.zeros
    return (jnp.concatenate([one((L, A_NOPE), f32), ca, one((L, HPAD - A_NOPE - A_ROPE), f32)], axis=1),
            jnp.concatenate([zero((L, A_NOPE), f32), sa, zero((L, HPAD - A_NOPE - A_ROPE), f32)], axis=1),
            jnp.concatenate([cb, one((L, HPAD - HEAD), f32)], axis=1), jnp.concatenate([sb, zero((L, HPAD - HEAD), f32)], axis=1))


def exchange(items, name):
    n = len(items)

    def body(*refs):
        start, wait = _exchange_ops([g for _, g in items], refs[:n], refs[n:2 * n], *refs[2 * n:])
        start()
        wait()

    ex = _exchange_io(items)
    res = pl.pallas_call(body, name=name, in_specs=ex.specs, out_specs=ex.specs, out_shape=ex.out_shape,
                         scratch_shapes=ex.scratch)(*ex.args)
    return list(res)


def _exchange_ops(modes, src_refs, out_refs, send_sems, recv_sems, local_sems):
    n = len(modes)
    x, y, c = lax.axis_index("x"), lax.axis_index("y"), lax.axis_index("c")
    me = 4 * x + 2 * y + c

    def copy(a, k, arriving):
        px, py, pc = (1 - x if k & 4 else x), (1 - y if k & 2 else y), (1 - c if k & 1 else c)
        pid = 4 * px + 2 * py + pc
        sem = a * (N_DEV - 1) + k - 1
        return pltpu.make_async_remote_copy(
            src_ref=src_refs[a] if modes[a] else src_refs[a].at[pid], dst_ref=out_refs[a].at[pid if arriving else me],
            send_sem=send_sems.at[sem], recv_sem=recv_sems.at[sem], device_id=(px, py, pc), device_id_type=pl.DeviceIdType.MESH)

    def local(a):
        return pltpu.make_async_copy(src_refs[a] if modes[a] else src_refs[a].at[me], out_refs[a].at[me], local_sems.at[a])

    pairs = [(a, k) for k in range(1, N_DEV) for a in range(n)]

    def start():
        for a, k in pairs:
            copy(a, k, False).start()
        for a in range(n):
            local(a).start()

    def wait():
        for a, k in pairs:
            copy(a, k, False).wait_send()
        for a, k in pairs:
            copy(a, k, True).wait_recv()
        for a in range(n):
            local(a).wait()

    return start, wait


class _ExchangeIO:
    def __init__(self, items):
        n = len(items)
        self.args = [s for s, _ in items]
        self.specs = [pl.BlockSpec(memory_space=pl.ANY)] * n
        self.out_shape = [jax.ShapeDtypeStruct((N_DEV,) + tuple(s.shape if g else s.shape[1:]), s.dtype) for s, g in items]
        self.scratch = [pltpu.SemaphoreType.DMA((n * (N_DEV - 1),)), pltpu.SemaphoreType.DMA((n * (N_DEV - 1),)),
                        pltpu.SemaphoreType.DMA((n,))] if n else []


def _exchange_io(items):
    return _ExchangeIO(list(items))


def _with_exchange(body, n_in, n_out, side, grid):
    ex = _exchange_io(side)
    n = len(ex.args)
    if not n:
        return body, ex
    modes = [g for _, g in side]

    def wrapped(*refs):
        ins, src = refs[:n_in], refs[n_in:n_in + n]
        outs, dst = refs[n_in + n:n_in + n + n_out], refs[n_in + n + n_out:n_in + 2 * n + n_out]
        sems = refs[n_in + 2 * n + n_out:]
        start, wait = _exchange_ops(modes, src, dst, *sems)
        ids = [pl.program_id(a) for a in range(len(grid))]
        first, last = ids[0] == 0, ids[0] == grid[0] - 1
        for a in range(1, len(grid)):
            first, last = first & (ids[a] == 0), last & (ids[a] == grid[a] - 1)
        pl.when(first)(start)
        body(*ins, *outs)
        pl.when(last)(wait)

    return wrapped, ex


LANES = 1024


def adamw(gstacks, w, m, v, name):
    D, R, C = w.shape
    S = gstacks[0].shape[0]
    tr = R if R <= 512 else _tile(R, 256)

    def body(*refs):
        g_refs = refs[:D]
        w_ref, m_ref, v_ref, go_ref, d_ref, mo_ref, vo_ref = refs[D:]
        layer = pl.program_id(0)
        g = None
        for d, g_ref in enumerate(g_refs):
            gd = g_ref[0].astype(f32)
            for sl in range(1, S):
                gd = gd + g_ref[sl].astype(f32)
            g = gd if g is None else jnp.where(layer == d, gd, g)
        g = g[None]
        m_new = ADAM_B1 * m_ref[...] + (1.0 - ADAM_B1) * g
        v_new = ADAM_B2 * v_ref[...] + (1.0 - ADAM_B2) * jnp.square(g)
        m_hat = m_new / (1.0 - ADAM_B1 ** ADAM_STEP)
        v_hat = v_new / (1.0 - ADAM_B2 ** ADAM_STEP)
        go_ref[...] = g
        d_ref[...] = -ADAM_LR * (m_hat / (jnp.sqrt(v_hat) + ADAM_EPS) + ADAM_WD * w_ref[...])
        mo_ref[...] = m_new
        vo_ref[...] = v_new

    row = pl.BlockSpec((1, tr, C), lambda l, i: (l, i, 0))
    g_specs = [pl.BlockSpec((S, tr, C), lambda l, i, d=d: (0, jnp.where(l == d, i, 0), 0)) for d in range(D)]
    return pl.pallas_call(
        body, name=name, grid=(D, R // tr), in_specs=g_specs + [row, row, row],
        out_specs=[row] * 4, out_shape=[jax.ShapeDtypeStruct((D, R, C), f32)] * 4, compiler_params=_params(("parallel", "parallel")),
    )(*gstacks, w, m, v)


def sum_slots(gstack, name):
    S, R, _ = gstack.shape

    def body(g_ref, o_ref):
        g = g_ref[0]
        for sl in range(1, S):
            g = g + g_ref[sl]
        o_ref[...] = g

    return pl.pallas_call(body, name=name, out_shape=jax.ShapeDtypeStruct((R, LANES), f32), compiler_params=_params())(gstack)


def _pack(parts, rows, dtype=f32):
    flat = jnp.concatenate([q.reshape(-1).astype(dtype) for q in parts])
    return jnp.pad(flat, (0, rows * LANES - flat.shape[0])).reshape(rows, LANES)


def _unpack(buf, shapes):
    lead = buf.shape[:-2]
    flat = buf.reshape(lead + (-1,))
    out, off = [], 0
    for shp in shapes:
        n = math.prod(shp)
        out.append(flat[..., off:off + n].reshape(lead + tuple(shp)))
        off += n
    return out


_WEIGHTS = ["pre_mix_norm", "w_in", "a_q_norm", "a_w_uq", "a_kv_norm", "a_w_ukv", "a_out_norm", "b_q_norm", "b_k_norm", "b_out_norm",
            "c_conv_w", "c_conv_b", "c_a_log", "c_dt_bias", "c_d_skip", "c_out_norm", "d_conv_w", "d_a_log", "d_dt_bias", "d_out_norm",
            "w_out", "post_mix_norm", "pre_ffn_norm", "f_w_in", "f_conv_w", "f_conv_b", "f_w_out", "post_ffn_norm"]
_BIG = {"w_in": 1, "a_w_uq": 1, "a_w_ukv": 1, "w_out": 0, "f_w_in": 1, "f_w_out": 0}
_CONV = ["c_conv_w", "d_conv_w", "f_conv_w"]
_REP = [n for n in _WEIGHTS if n not in _BIG and n not in _CONV]
SMALLG_ROWS, SMALLW_ROWS = 64, 32


def _join(blocks, axis):
    return jnp.concatenate([blocks[d] for d in range(N_DEV)], axis=axis)


def _split(full, axis):
    n = full.shape[axis] // N_DEV
    return jnp.stack([lax.slice_in_dim(full, d * n, (d + 1) * n, axis=axis) for d in range(N_DEV)])


def kernel(x, pre_mix_norm, w_in, a_q_norm, a_w_uq, a_kv_norm, a_w_ukv, a_out_norm, b_q_norm, b_k_norm, b_out_norm, c_conv_w, c_conv_b, c_a_log, c_dt_bias, c_d_skip, c_out_norm, d_conv_w, d_a_log, d_dt_bias, d_out_norm, w_out, post_mix_norm, pre_ffn_norm, f_w_in, f_conv_w, f_conv_b, f_w_out, post_ffn_norm, loss_target, m_pre_mix_norm, m_w_in, m_a_q_norm, m_a_w_uq, m_a_kv_norm, m_a_w_ukv, m_a_out_norm, m_b_q_norm, m_b_k_norm, m_b_out_norm, m_c_conv_w, m_c_conv_b, m_c_a_log, m_c_dt_bias, m_c_d_skip, m_c_out_norm, m_d_conv_w, m_d_a_log, m_d_dt_bias, m_d_out_norm, m_w_out, m_post_mix_norm, m_pre_ffn_norm, m_f_w_in, m_f_conv_w, m_f_conv_b, m_f_w_out, m_post_ffn_norm, v_pre_mix_norm, v_w_in, v_a_q_norm, v_a_w_uq, v_a_kv_norm, v_a_w_ukv, v_a_out_norm, v_b_q_norm, v_b_k_norm, v_b_out_norm, v_c_conv_w, v_c_conv_b, v_c_a_log, v_c_dt_bias, v_c_d_skip, v_c_out_norm, v_d_conv_w, v_d_a_log, v_d_dt_bias, v_d_out_norm, v_w_out, v_post_mix_norm, v_pre_ffn_norm, v_f_w_in, v_f_conv_w, v_f_conv_b, v_f_w_out, v_post_ffn_norm):
    given = dict(locals())
    w = {n: given[n] for n in _WEIGHTS}
    mom = {n: given["m_" + n] for n in _WEIGHTS}
    var = {n: given["v_" + n] for n in _WEIGHTS}
    me = 4 * lax.axis_index("x") + 2 * lax.axis_index("y") + lax.axis_index("c")
    layered = lambda names: [(l, n) for l in range(DEPTH) for n in names]

    sharded = list(_BIG) + _CONV
    early = [n for n in sharded if n not in _LATE]
    weight_items = lambda l, names: [(w[n][l].astype(MXU_DTYPE if n in _BIG else f32), True) for n in names]
    grad_items = lambda g, names: [(_split(g[n], _BIG[n]).astype(WIRE_DTYPE), False) for n in names]
    P = {n: [w[n][l] for l in range(DEPTH)] for n in _REP}
    P.update({n: [None] * DEPTH for n in sharded})

    def place(l, names, gathered):
        for n, got in zip(names, gathered):
            P[n][l] = _join(got, _BIG.get(n, 1))

    def late_weights_l0(gathered):
        place(0, _LATE, gathered)
        return _late_kernel_weights(P, 0, MXU_DTYPE)

    tabs = _rope_tables(x.shape[1])
    place(0, early, exchange(weight_items(0, early), "gather_weights"))
    x1, s0, got, W0 = _layer_fwd(x[0], _kernel_weights(P, 0, MXU_DTYPE), tabs, "l0_", weight_items(1, sharded),
                                 weight_items(0, _LATE), late_weights_l0)
    place(1, sharded, got)
    x2, s1, _, W1 = _layer_fwd(x1, _kernel_weights(P, 1, MXU_DTYPE), tabs, "l1_")
    loss, dx = loss_and_grad(x2, loss_target[0], "loss")
    loss = lax.psum(loss[0, 0], ("x", "y", "c"))

    def late_grads_l0(g):
        return grad_items({"w_out": g["w_out"], "f_w_in": _ff_deinterleave(g["f_win"]), "f_w_out": g["f_wout"]}, _LATE)

    dx, g1, _, _ = _layer_bwd(s1, dx, W1, tabs, "l1_")
    g1 = _reference_grads(g1)
    dx, g0, recv1, recv0_late = _layer_bwd(s0, dx, W0, tabs, "l0_", grad_items(g1, _BIG), late_grads_l0)
    grads = [_reference_grads(g0), g1]
    small = _REP + _CONV
    full_shapes = [grads[l][n].shape for l, n in layered(small)]
    first = [n for n in _BIG if n not in _LATE]
    *recv0_first, gsmall = exchange(grad_items(grads[0], first) + [(_pack([grads[l][n] for l, n in layered(small)], SMALLG_ROWS), True)],
                                    "exchange_grads")
    recv0 = dict(zip(first + _LATE, recv0_first + recv0_late))

    out = {}
    for n, got1 in zip(_BIG, recv1):
        out[n] = adamw([recv0[n], got1], w[n], mom[n], var[n], "adamw_" + n)
    gsum = dict(zip(layered(small), _unpack(sum_slots(gsmall, "sum_small_grads"), full_shapes)))
    for l, n in layered(_CONV):
        cols = w[n].shape[2]
        gsum[(l, n)] = lax.dynamic_slice_in_dim(gsum[(l, n)], me * cols, cols, axis=1)
    held_shapes = [w[n].shape[1:] for _, n in layered(small)]
    pack_small = lambda d: _pack([d[n][l] for l, n in layered(small)], SMALLW_ROWS)[None]
    res_small = adamw([_pack([gsum[k] for k in layered(small)], SMALLW_ROWS)[None]], pack_small(w), pack_small(mom),
                      pack_small(var), "adamw_small")
    per_kind = [dict(zip(layered(small), _unpack(buf[0], held_shapes))) for buf in res_small]
    for n in small:
        out[n] = [jnp.stack([per_kind[kind][(l, n)] for l in range(DEPTH)]) for kind in range(4)]
    return (loss, dx[None], *[out[n][kind] for kind in range(4) for n in _WEIGHTS])
```

```python
import functools
import math

import jax
import jax.numpy as jnp
from jax import lax
from jax.experimental import pallas as pl
from jax.experimental.pallas import tpu as pltpu

f32 = jnp.float32
MXU_DTYPE = jnp.bfloat16
WIRE_DTYPE = jnp.bfloat16
HI = lax.Precision.HIGHEST

D_MODEL = 1024
DEPTH = 2
GRID_W = 64
ROPE_BASE = 10000.0
EPS = 1e-6
A_Q_LORA, A_KV_LORA, A_ROPE, A_NOPE = 192, 128, 32, 64
SSD_CHUNK = 256
DN_CHUNK = 64
HEAD = 64
HPAD = 128
D_FF = 2816
FF_BLK = 256
N_DEV = 8
ADAM_LR, ADAM_B1, ADAM_B2, ADAM_EPS, ADAM_WD, ADAM_STEP = 0.001, 0.9, 0.999, 1e-08, 0.01, 10

V7X_VMEM_BYTES = 64 * 2 ** 20
VMEM_LIMIT = (V7X_VMEM_BYTES * 3) // 4

PIN = 3072
SEG_A, SEG_B, SEG_CX, SEG_DX, SEG_CZ, SEG_DZ, SEG_CDT, SEG_DAB = 0, 512, 1024, 1536, 2304, 2560, 2816, 2944


def _tile(n, pref):
    for t in (512, 256, 128, 64, 32, 16, 8):
        if t <= pref and n % t == 0:
            return t
    raise ValueError(f"no tile for {n}")


LANE = 128


def _tile_div(n, pref):
    if n <= pref:
        return n
    return max(d for d in range(LANE, pref + 1, LANE) if n % d == 0)


def _params(sem=None):
    return pltpu.CompilerParams(vmem_limit_bytes=VMEM_LIMIT, dimension_semantics=sem)


def _tup(r):
    return tuple(r) if isinstance(r, (tuple, list)) else (r,)


def matmul(a, b, form, out_dtype, name):
    if form == "nn":
        (M, K), N = a.shape, b.shape[1]
    elif form == "nt":
        (M, K), N = a.shape, b.shape[0]
    else:
        (K, M), N = a.shape, b.shape[1]
    if form == "tn":
        tm, tn, tk = (512 if M % 512 == 0 else _tile_div(M, 1408)), _tile_div(N, 1408), _tile_div(K, 2048)
    else:
        tm, tn, tk = _tile_div(M, 512), _tile_div(N, 2816), _tile_div(K, 5632 if jnp.dtype(a.dtype).itemsize == 2 else 2816)
    nk = K // tk
    dims = {"nn": ((1,), (0,)), "nt": ((1,), (1,)), "tn": ((0,), (0,))}[form]

    def body(a_ref, b_ref, o_ref, *acc):
        part = lax.dot_general(a_ref[...].astype(MXU_DTYPE), b_ref[...].astype(MXU_DTYPE), (dims, ((), ())),
                               preferred_element_type=f32)
        if nk == 1:
            o_ref[...] = part.astype(out_dtype)
            return
        (acc_ref,) = acc
        k = pl.program_id(2)

        @pl.when(k == 0)
        def _():
            acc_ref[...] = part

        @pl.when((k > 0) & (k < nk - 1))
        def _():
            acc_ref[...] += part

        @pl.when(k == nk - 1)
        def _():
            o_ref[...] = (acc_ref[...] + part).astype(out_dtype)

    a_spec = pl.BlockSpec((tk, tm), lambda i, j, k: (k, i)) if form == "tn" else pl.BlockSpec((tm, tk), lambda i, j, k: (i, k))
    b_spec = pl.BlockSpec((tn, tk), lambda i, j, k: (j, k)) if form == "nt" else pl.BlockSpec((tk, tn), lambda i, j, k: (k, j))
    return pl.pallas_call(
        body, name=name, grid=(M // tm, N // tn, nk), in_specs=[a_spec, b_spec],
        out_specs=pl.BlockSpec((tm, tn), lambda i, j, k: (i, j)), out_shape=jax.ShapeDtypeStruct((M, N), out_dtype),
        scratch_shapes=[pltpu.VMEM((tm, tn), f32)] if nk > 1 else [], compiler_params=_params(("parallel", "parallel", "arbitrary")),
    )(a, b)


def _row_specs(rows, tm):
    return [pl.BlockSpec((tm, w), lambda i, b=b: (i, b)) for (_, w, b) in rows]


def _full_specs(params):
    return [pl.BlockSpec(p.shape, lambda i: (0, 0)) for p in params]


def rowwise(fn, rows, params, outs, name, tm=512):
    L = rows[0][0].shape[0]
    tm = _tile(L, tm)
    n_in = len(rows) + len(params)

    def body(*refs):
        res = _tup(fn(*[r[...].astype(f32) for r in refs[:n_in]]))
        for o_ref, r in zip(refs[n_in:], res, strict=True):
            o_ref[...] = r.astype(o_ref.dtype)

    res = pl.pallas_call(
        body, name=name, grid=(L // tm,), in_specs=_row_specs(rows, tm) + _full_specs(params),
        out_specs=[pl.BlockSpec((tm, w), lambda i: (i, 0)) for (w, _) in outs],
        out_shape=[jax.ShapeDtypeStruct((L, w), dt) for (w, dt) in outs], compiler_params=_params(("parallel",)),
    )(*[r[0] for r in rows], *params)
    return list(res)


def rowwise_vjp(fn, rows, params, cts, name, row_grads, param_grads, tm=512, row_dtypes=None):
    L = rows[0][0].shape[0]
    tm = _tile(L, tm)
    nr, npar, nct = len(rows), len(params), len(cts)
    row_dtypes = [f32] * len(row_grads) if row_dtypes is None else row_dtypes

    def body(*refs):
        i = pl.program_id(0)
        rv = [r[...].astype(f32) for r in refs[:nr]]
        pv = [r[...].astype(f32) for r in refs[nr:nr + npar]]
        cv = tuple(r[...].astype(f32) for r in refs[nr + npar:nr + npar + nct])
        out_refs = refs[nr + npar + nct:]

        def g(*diff):
            rr, pp = list(rv), list(pv)
            for k, v in zip(row_grads, diff[:len(row_grads)]):
                rr[k] = v
            for k, v in zip(param_grads, diff[len(row_grads):]):
                pp[k] = v
            return _tup(fn(*rr, *pp))

        _, vjp = jax.vjp(g, *[rv[k] for k in row_grads], *[pv[k] for k in param_grads])
        grads = vjp(cv)
        for o_ref, gval in zip(out_refs[:len(row_grads)], grads[:len(row_grads)]):
            o_ref[...] = gval.astype(o_ref.dtype)
        for o_ref, gval in zip(out_refs[len(row_grads):], grads[len(row_grads):]):
            @pl.when(i == 0)
            def _(o_ref=o_ref, gval=gval):
                o_ref[...] = gval

            @pl.when(i > 0)
            def _(o_ref=o_ref, gval=gval):
                o_ref[...] += gval

    out_specs = [pl.BlockSpec((tm, rows[k][1]), lambda i: (i, 0)) for k in row_grads] + \
                [pl.BlockSpec(params[k].shape, lambda i: (0, 0)) for k in param_grads]
    out_shape = [jax.ShapeDtypeStruct((L, rows[k][1]), dt) for k, dt in zip(row_grads, row_dtypes)] + \
                [jax.ShapeDtypeStruct(params[k].shape, f32) for k in param_grads]
    res = pl.pallas_call(
        body, name=name, grid=(L // tm,), in_specs=_row_specs(rows, tm) + _full_specs(params) + _row_specs(cts, tm),
        out_specs=out_specs, out_shape=out_shape, compiler_params=_params(("arbitrary",)),
    )(*[r[0] for r in rows], *params, *[c[0] for c in cts])
    res = list(res)
    return res[:len(row_grads)], res[len(row_grads):]


def _rms(x, w, n=None):
    n = x.shape[-1] if n is None else n
    return x * lax.rsqrt(jnp.sum(x * x, axis=-1, keepdims=True) * (1.0 / n) + EPS) * w


def _silu(x):
    return x * jax.nn.sigmoid(x)


def _softplus(x):
    return jnp.maximum(x, 0.0) + jnp.log1p(jnp.exp(-jnp.abs(x)))


def _mm(a, b, dims, precision=None):
    if precision is None:
        a, b = a.astype(MXU_DTYPE), b.astype(MXU_DTYPE)
    return lax.dot_general(a, b, (dims, ((), ())), precision=precision, preferred_element_type=f32)


_NN, _NT, _TN = ((1,), (0,)), ((1,), (1,)), ((0,), (0,))


@functools.partial(jax.custom_vjp, nondiff_argnums=(1, 2))
def _roll(x, shift, axis):
    return pltpu.roll(x, shift, axis)


def _roll_fwd(x, shift, axis):
    return pltpu.roll(x, shift, axis), None


def _roll_bwd(shift, axis, _, ct):
    return (pltpu.roll(ct, (ct.shape[axis] - shift) % ct.shape[axis], axis),)


_roll.defvjp(_roll_fwd, _roll_bwd)


def _rope(x, cos, sin, lo, half):
    n = x.shape[-1]
    lane = lax.broadcasted_iota(jnp.int32, x.shape, x.ndim - 1) % HPAD - lo
    first = ((lane >= 0) & (lane < half)) | ((lane >= 2 * half) & (lane < 3 * half))
    rot = jnp.where(first, -_roll(x, n - half, x.ndim - 1), _roll(x, half, x.ndim - 1))
    return x * cos + rot * sin


def _heads(x, n, width=HEAD):
    return [x[:, h * width:(h + 1) * width] for h in range(n)]


def _pad_heads(hs):
    z = jnp.zeros_like(hs[0])
    return jnp.concatenate([t for h in hs for t in (h, z)], axis=-1)


def _unpad_heads(x, n):
    return jnp.concatenate([x[:, h * HPAD:h * HPAD + HEAD] for h in range(n)], axis=-1)


HALO = 8


def _conv_specs(tm, tc, c0, L):
    nh = tm // HALO
    last = L // HALO - 1
    return [pl.BlockSpec((tm, tc), lambda i, j: (i, c0 + j)),
            pl.BlockSpec((HALO, tc), lambda i, j: (jnp.maximum(i * nh - 1, 0), c0 + j)),
            pl.BlockSpec((HALO, tc), lambda i, j: (jnp.minimum((i + 1) * nh, last), c0 + j))]


def _shift3(x):
    n = x.shape[0]
    return _roll(x, 1, 0), _roll(x, n - 1, 0)


def conv_fwd(gfn, x, c0, ncol, tc, w, b, tco, out_dtype, name, tm=512):
    L = x.shape[0]
    tm = _tile_div(L, tm)
    ni = L // tm

    def body(x_ref, p_ref, n_ref, w_ref, b_ref, o_ref, c_ref):
        i = pl.program_id(0)
        xv = x_ref[...].astype(f32)
        xp = jnp.where(i == 0, 0.0, p_ref[HALO - 1:HALO, :].astype(f32))
        xn = jnp.where(i == ni - 1, 0.0, n_ref[0:1, :].astype(f32))
        rid = lax.broadcasted_iota(jnp.int32, xv.shape, 0)
        dn, up = _shift3(xv)
        dn = jnp.where(rid == 0, xp, dn)
        up = jnp.where(rid == tm - 1, xn, up)
        c = w_ref[0:1, :] * dn + w_ref[1:2, :] * xv + w_ref[2:3, :] * up + b_ref[...]
        c_ref[...] = c
        o_ref[...] = gfn(c).astype(o_ref.dtype)

    return pl.pallas_call(
        body, name=name, grid=(ni, ncol),
        in_specs=_conv_specs(tm, tc, c0, L) + [pl.BlockSpec((3, tc), lambda i, j: (0, j)), pl.BlockSpec((1, tc), lambda i, j: (0, j))],
        out_specs=[pl.BlockSpec((tm, tco), lambda i, j: (i, j)), pl.BlockSpec((tm, tc), lambda i, j: (i, j))],
        out_shape=[jax.ShapeDtypeStruct((L, ncol * tco), out_dtype), jax.ShapeDtypeStruct((L, ncol * tc), f32)],
        compiler_params=_params(("parallel", "parallel")),
    )(x, x, x, w, b)


def conv_bwd(gfn, x, c, c0, ncol, tc, w, dys, tco, name, tm=512, dx_dtype=f32):
    L = x.shape[0]
    tm = _tile_div(L, tm)
    ni = L // tm
    nd = len(dys)

    def body(*refs):
        x_ref, xp_ref, xn_ref, c_ref, cp_ref, cn_ref = refs[:6]
        d_refs = refs[6:6 + 3 * nd]
        w_ref, dx_ref, dw_ref, db_ref = refs[6 + 3 * nd:]
        i = pl.program_id(1)
        first, lastb = i == 0, i == ni - 1

        def ext(m, p, n):
            return jnp.concatenate([jnp.where(first, 0.0, p[...].astype(f32)), m[...].astype(f32),
                                    jnp.where(lastb, 0.0, n[...].astype(f32))], axis=0)

        xe = ext(x_ref, xp_ref, xn_ref)
        de = ext(*d_refs[:3])
        for q in range(1, nd):
            de = de + ext(*d_refs[3 * q:3 * q + 3])
        x_dn, x_up = _shift3(xe)
        w0, w1, w2 = w_ref[0:1, :], w_ref[1:2, :], w_ref[2:3, :]
        _, vjp = jax.vjp(gfn, ext(c_ref, cp_ref, cn_ref))
        (dce,) = vjp(de)
        dc_dn, dc_up = _shift3(dce)
        dx_ref[...] = (w0 * dc_up + w1 * dce + w2 * dc_dn)[HALO:HALO + tm, :].astype(dx_ref.dtype)
        inner = slice(HALO, HALO + tm)
        dci = dce[inner]
        dw = jnp.concatenate([jnp.sum(dci * x_dn[inner], axis=0, keepdims=True), jnp.sum(dci * xe[inner], axis=0, keepdims=True),
                              jnp.sum(dci * x_up[inner], axis=0, keepdims=True)], axis=0)
        db = jnp.sum(dci, axis=0, keepdims=True)

        @pl.when(first)
        def _():
            dw_ref[...] = dw
            db_ref[...] = db

        @pl.when(i > 0)
        def _():
            dw_ref[...] += dw
            db_ref[...] += db

    res = pl.pallas_call(
        body, name=name, grid=(ncol, ni),
        in_specs=[pl.BlockSpec(s.block_shape, (lambda j, i, f=s.index_map: f(i, j))) for s in _conv_specs(tm, tc, c0, L)]
        + [pl.BlockSpec(s.block_shape, (lambda j, i, f=s.index_map: f(i, j))) for s in _conv_specs(tm, tc, 0, L)]
        + [pl.BlockSpec(s.block_shape, (lambda j, i, f=s.index_map: f(i, j))) for s in _conv_specs(tm, tco, 0, L)] * nd
        + [pl.BlockSpec((3, tc), lambda j, i: (0, j))],
        out_specs=[pl.BlockSpec((tm, tc), lambda j, i: (i, j)), pl.BlockSpec((3, tc), lambda j, i: (0, j)),
                   pl.BlockSpec((1, tc), lambda j, i: (0, j))],
        out_shape=[jax.ShapeDtypeStruct((L, ncol * tc), dx_dtype), jax.ShapeDtypeStruct((3, ncol * tc), f32),
                   jax.ShapeDtypeStruct((1, ncol * tc), f32)],
        compiler_params=_params(("parallel", "arbitrary")),
    )(x, x, x, c, c, c, *[d for d in dys for _ in range(3)], w)
    return res


LOG2E = math.log2(math.e)


def flash_fwd(q, k, v, n_q, n_kv, scale, name, side=(), tq=512, tk=8192):
    L = q.shape[0]
    tq, tk = _tile_div(L, tq), _tile_div(L, tk)
    rep = n_q // n_kv
    nkv = L // tk
    c2 = scale * LOG2E

    def body(q_ref, k_ref, v_ref, o_ref, lse_ref):
        qv = q_ref[...]
        ones_lane = lax.broadcasted_iota(jnp.int32, (tk, HPAD), 1) == HPAD - 1

        def step(c, carry):
            m, acc = carry
            off = pl.multiple_of(c * tk, tk)
            s = _mm(qv, k_ref[pl.ds(off, tk), :], _NT)
            m_new = jnp.maximum(m, jnp.max(s, axis=-1, keepdims=True))
            p = jnp.exp2((s - m_new) * c2)
            vv = v_ref[pl.ds(off, tk), :]
            acc = jnp.exp2((m - m_new) * c2) * acc + _mm(p, jnp.where(ones_lane, jnp.ones_like(vv), vv), _NN)
            return m_new, acc

        m, acc = lax.fori_loop(0, nkv, step, (jnp.full((tq, 1), -jnp.inf, f32), jnp.zeros((tq, HPAD), f32)))
        l = acc[:, HPAD - 1:]
        o_ref[...] = jnp.where(lax.broadcasted_iota(jnp.int32, (tq, HPAD), 1) == HPAD - 1, 0.0, acc / l)
        lse_ref[0] = m * scale + jnp.log(l)

    grid = (n_q, L // tq)
    body, ex = _with_exchange(body, 3, 2, side, grid)
    res = pl.pallas_call(
        body, name=name, grid=grid,
        in_specs=[pl.BlockSpec((tq, HPAD), lambda h, i: (i, h)), pl.BlockSpec((L, HPAD), lambda h, i: (0, h // rep)),
                  pl.BlockSpec((L, HPAD), lambda h, i: (0, h // rep))] + ex.specs,
        out_specs=[pl.BlockSpec((tq, HPAD), lambda h, i: (i, h)), pl.BlockSpec((1, tq, 1), lambda h, i: (h, i, 0))] + ex.specs,
        out_shape=[jax.ShapeDtypeStruct((L, n_q * HPAD), f32), jax.ShapeDtypeStruct((n_q, L, 1), f32)] + ex.out_shape,
        scratch_shapes=ex.scratch, compiler_params=_params(("arbitrary", "arbitrary") if side else ("parallel", "parallel")),
    )(q, k, v, *ex.args)
    return res[0], res[1], list(res[2:])


def flash_bwd(q, k, v, o, lse, do, n_q, n_kv, scale, name, side=(), tq=2048, tk=1024):
    L = q.shape[0]
    tq, tk = _tile_div(L, tq), _tile_div(L, tk)
    rep = n_q // n_kv
    c2 = scale * LOG2E

    def body(k_ref, v_ref, q_ref, do_ref, o_ref, lse_ref, dq_ref, dk_ref, dv_ref):
        j, i = pl.program_id(1), pl.program_id(2)
        kv, vv = k_ref[...], v_ref[...]
        rows = pl.ds(pl.multiple_of(i * tq, tq), tq)
        dk, dv = jnp.zeros((tk, HPAD), f32), jnp.zeros((tk, HPAD), f32)
        for r in range(rep):
            cols = slice(r * HPAD, (r + 1) * HPAD)
            qv = q_ref[:, cols]
            dov = do_ref[:, cols]
            delta = jnp.sum(dov * o_ref[:, cols], axis=-1, keepdims=True)
            p = jnp.exp2(_mm(qv, kv, _NT) * c2 - lse_ref[r] * LOG2E)
            dv = dv + _mm(p, dov, _TN)
            ds = p * (_mm(dov, vv, _NT) - delta) * scale
            dk = dk + _mm(ds, qv, _TN)
            dq = _mm(ds, kv, _NN)

            @pl.when(j == 0)
            def _(dq=dq, cols=cols):
                dq_ref[rows, cols] = dq

            @pl.when(j > 0)
            def _(dq=dq, cols=cols):
                dq_ref[rows, cols] += dq

        @pl.when(i == 0)
        def _():
            dk_ref[...] = dk
            dv_ref[...] = dv

        @pl.when(i > 0)
        def _():
            dk_ref[...] += dk
            dv_ref[...] += dv

    w = rep * HPAD
    grid = (n_kv, L // tk, L // tq)
    body, ex = _with_exchange(body, 6, 3, side, grid)
    res = pl.pallas_call(
        body, name=name, grid=grid,
        in_specs=[pl.BlockSpec((tk, HPAD), lambda g, j, i: (j, g)), pl.BlockSpec((tk, HPAD), lambda g, j, i: (j, g)),
                  pl.BlockSpec((tq, w), lambda g, j, i: (i, g)), pl.BlockSpec((tq, w), lambda g, j, i: (i, g)),
                  pl.BlockSpec((tq, w), lambda g, j, i: (i, g)), pl.BlockSpec((rep, tq, 1), lambda g, j, i: (g, i, 0))] + ex.specs,
        out_specs=[pl.BlockSpec((L, w), lambda g, j, i: (0, g)), pl.BlockSpec((tk, HPAD), lambda g, j, i: (j, g)),
                   pl.BlockSpec((tk, HPAD), lambda g, j, i: (j, g))] + ex.specs,
        out_shape=[jax.ShapeDtypeStruct((L, n_q * HPAD), f32), jax.ShapeDtypeStruct((L, n_kv * HPAD), f32),
                   jax.ShapeDtypeStruct((L, n_kv * HPAD), f32)] + ex.out_shape,
        scratch_shapes=ex.scratch, compiler_params=_params(("arbitrary",) * 3 if side else ("parallel", "arbitrary", "arbitrary")),
    )(k, v, q, do, o, lse, *ex.args)
    return res[0], res[1], res[2], list(res[3:])


N_SCAN_HEADS = 4
STATE_ROWS = N_SCAN_HEADS * HEAD


def _tri(n, rev):
    i = lax.broadcasted_iota(jnp.int32, (n, n), 0)
    k = lax.broadcasted_iota(jnp.int32, (n, n), 1)
    return ((k >= i), (k > i)) if rev else ((k <= i), (k < i))


def _decay(acol, incl):
    n = acol.shape[0]
    m1 = jnp.broadcast_to(acol, (n, n))
    return jnp.exp(jnp.where(incl, m1 - m1.T, -jnp.inf))


def _ssd_chunk(S, xbc, win, a_log, dt_bias, rev):
    Q = xbc.shape[0]
    d = 1 if rev else 0
    incl, _ = _tri(Q, rev)
    dt = _softplus(win[:, 4 * d:4 * d + 4] + dt_bias[d:d + 1, :])
    a = dt * (-jnp.exp(a_log[d:d + 1, :]))
    acum = _mm(incl.astype(f32), a, _NN, HI)
    tot = jnp.sum(a, axis=0, keepdims=True)
    xs, Bm, Cm = xbc[:, :256], xbc[:, 256:384], xbc[:, 384:512]
    H, W = N_SCAN_HEADS, N_SCAN_HEADS * HEAD
    lanes = lambda x: _cat([jnp.broadcast_to(x[:, h:h + 1], (x.shape[0], HEAD)) for h in range(H)])
    xdt = xs * lanes(dt)
    cb = [_mm(Cm[:, g * HEAD:(g + 1) * HEAD], Bm[:, g * HEAD:(g + 1) * HEAD], _NT) for g in range(2)]
    scores = _cat([cb[h // 2] * _decay(acum[:, h:h + 1], incl) for h in range(H)])
    own = (lax.broadcasted_iota(jnp.int32, (H * Q, W), 0) // Q) == (lax.broadcasted_iota(jnp.int32, (H * Q, W), 1) // HEAD)
    y = _mm(scores, jnp.where(own, jnp.concatenate([xdt] * H, axis=0), 0.0), _NN)
    grp = (lax.broadcasted_iota(jnp.int32, (W, 2 * HEAD), 0) // (2 * HEAD)) == (lax.broadcasted_iota(jnp.int32, (W, 2 * HEAD), 1) // HEAD)
    y = y + _mm(Cm, jnp.where(grp, _cat([S, S]), 0.0), _NT) * jnp.exp(lanes(acum))
    st = _mm(xdt * jnp.exp(lanes(tot - acum)), Bm, _TN)
    first = lax.broadcasted_iota(jnp.int32, (W, HEAD), 0) < 2 * HEAD
    exp_tot = jnp.concatenate([jnp.broadcast_to(jnp.exp(tot[:, h:h + 1]), (HEAD, 1)) for h in range(H)], axis=0)
    return S * exp_tot + jnp.where(first, st[:, :HEAD], st[:, HEAD:]), y


@functools.partial(jax.custom_vjp, nondiff_argnums=(1,))
def _inv_unit_tri(Lm, order):
    n = Lm.shape[0]
    eye = (lax.broadcasted_iota(jnp.int32, (n, n), 0) == lax.broadcasted_iota(jnp.int32, (n, n), 1)).astype(f32)
    P = -Lm
    T = eye + P
    k = 1
    while 2 * k < order:
        P = _mm(P, P, _NN, lax.Precision.HIGH)
        T = T + _mm(T, P, _NN, lax.Precision.HIGH)
        k *= 2
    return T


def _inv_unit_tri_fwd(Lm, order):
    T = _inv_unit_tri(Lm, order)
    return T, T


def _inv_unit_tri_bwd(order, T, dT):
    return (-_mm(_mm(T, dT, _TN, lax.Precision.HIGH), T, _NT, lax.Precision.HIGH),)


_inv_unit_tri.defvjp(_inv_unit_tri_fwd, _inv_unit_tri_bwd)


@jax.custom_vjp
def _inv_known(Lm, T):
    return T


def _inv_known_fwd(Lm, T):
    return T, T


def _inv_known_bwd(T, dT):
    return _inv_unit_tri_bwd(None, T, dT) + (jnp.zeros_like(T),)


_inv_known.defvjp(_inv_known_fwd, _inv_known_bwd)


def _dn_chunk(S, qkv, win, a_log, dt_bias, rev, kept=None):
    Q = qkv.shape[0]
    H, R = N_SCAN_HEADS, N_SCAN_HEADS * qkv.shape[0]
    d = 1 if rev else 0
    beta = jax.nn.sigmoid(win[:, 4 * d:4 * d + 4])
    gl = -jnp.exp(a_log[d:d + 1, :]) * _softplus(win[:, 8 + 4 * d:12 + 4 * d] + dt_bias[d:d + 1, :])
    G = _mm(_tri(Q, rev)[0].astype(f32), gl, _NN, HI)
    tot = jnp.sum(gl, axis=0, keepdims=True)
    rows = lambda x: jnp.concatenate(_heads(x, H), axis=0)
    col = lambda x: jnp.concatenate([x[:, h:h + 1] for h in range(H)], axis=0)
    per_head = lambda x, n: jnp.concatenate([jnp.broadcast_to(x[:, h:h + 1], (n, 1)) for h in range(H)], axis=0)
    r = lax.broadcasted_iota(jnp.int32, (R, R), 0)
    c = lax.broadcasted_iota(jnp.int32, (R, R), 1)
    same = (r // Q) == (c // Q)
    incl = same & ((c >= r) if rev else (c <= r))
    strict = same & ((c > r) if rev else (c < r))
    own = (lax.broadcasted_iota(jnp.int32, (R, H * HEAD), 0) // Q) == (lax.broadcasted_iota(jnp.int32, (R, H * HEAD), 1) // HEAD)
    blk = lambda x: jnp.where(own, _cat([x] * H), 0.0)
    q, k, v = rows(qkv[:, :256]) * (HEAD ** -0.5), rows(qkv[:, 256:512]), rows(qkv[:, 512:])
    b, Gs, tots = col(beta), col(G), per_head(tot, Q)
    dec = _decay(Gs, incl)
    kb = k * b
    Lm = jnp.where(strict, _mm(kb, k, _NT) * dec, 0.0)
    T = _inv_unit_tri(Lm, Q) if kept is None else _inv_known(Lm, blk(kept))
    eG = jnp.exp(Gs)
    uw = _mm(T, _cat([v * b, kb * eG]), _NN)
    vnew = uw[:, :HEAD] - _mm(blk(uw[:, HEAD:]), S, _NN)
    o = _mm(blk(q * eG), S, _NN) + _mm(_mm(q, k, _NT) * dec, vnew, _NN)
    S_next = S * jnp.exp(per_head(tot, HEAD)) + _mm(blk(k * jnp.exp(tots - Gs)), vnew, _TN)
    out = S_next, _cat([o[h * Q:(h + 1) * Q, :] for h in range(H)])
    if kept is None:
        out += (jnp.concatenate([T[h * Q:(h + 1) * Q, h * Q:(h + 1) * Q] for h in range(H)], axis=0),)
    return out


def scan_fwd(chunk_fn, Q, x, p, win_blk, a_log, dt_bias, name, keeps=False):
    L, W = x.shape
    nc = L // Q
    fidx, ridx = (lambda t: t), (lambda t: nc - 1 - t)

    def body(xf_ref, xr_ref, wf_ref, wr_ref, al_ref, db_ref, yf_ref, yr_ref, sf_ref, sr_ref, *rest):
        (sf_scr, sr_scr), keep_refs = rest[-2:], rest[:-2]

        @pl.when(pl.program_id(0) == 0)
        def _():
            sf_scr[...] = jnp.zeros_like(sf_scr)
            sr_scr[...] = jnp.zeros_like(sr_scr)

        for d, (x_ref, w_ref, y_ref, sin_ref, s_scr) in enumerate(((xf_ref, wf_ref, yf_ref, sf_ref, sf_scr),
                                                                   (xr_ref, wr_ref, yr_ref, sr_ref, sr_scr))):
            S = s_scr[...]
            sin_ref[...] = S
            res = chunk_fn(S, x_ref[...], w_ref[...], al_ref[...], db_ref[...], d == 1)
            s_scr[...], y_ref[...] = res[:2]
            if keeps:
                keep_refs[d][...] = res[2]

    row = lambda w, idx, b=0: pl.BlockSpec((Q, w), lambda t: (idx(t), b))
    small = pl.BlockSpec(a_log.shape, lambda t: (0, 0))
    state = lambda idx: pl.BlockSpec((STATE_ROWS, HEAD), lambda t: (idx(t), 0))
    n_state = 4 if keeps else 2
    return pl.pallas_call(
        body, name=name, grid=(nc,),
        in_specs=[row(W, fidx), row(W, ridx), row(HPAD, fidx, win_blk), row(HPAD, ridx, win_blk), small, small],
        out_specs=[row(STATE_ROWS, fidx), row(STATE_ROWS, ridx)] + [state(fidx), state(ridx)] * (n_state // 2),
        out_shape=[jax.ShapeDtypeStruct((L, STATE_ROWS), f32)] * 2 + [jax.ShapeDtypeStruct((nc * STATE_ROWS, HEAD), f32)] * n_state,
        scratch_shapes=[pltpu.VMEM((STATE_ROWS, HEAD), f32)] * 2, compiler_params=_params(("arbitrary",)),
    )(x, x, p, p, a_log, dt_bias)


def scan_bwd(chunk_fn, Q, x, p, win_blk, a_log, dt_bias, s_fwd, s_rev, dy, name, kept=()):
    L, W = x.shape
    nc = L // Q
    fidx, ridx = (lambda t: nc - 1 - t), (lambda t: t)
    nk = len(kept)

    def body(xf_ref, xr_ref, wf_ref, wr_ref, al_ref, db_ref, sf_ref, sr_ref, dyf_ref, dyr_ref, *rest):
        keep_refs = rest[:nk]
        dxf_ref, dxr_ref, dwf_ref, dwr_ref, dal_ref, ddb_ref, dsf_scr, dsr_scr = rest[nk:]

        @pl.when(pl.program_id(0) == 0)
        def _():
            dsf_scr[...] = jnp.zeros_like(dsf_scr)
            dsr_scr[...] = jnp.zeros_like(dsr_scr)
            dal_ref[...] = jnp.zeros_like(dal_ref)
            ddb_ref[...] = jnp.zeros_like(ddb_ref)

        dal_sum, ddb_sum = dal_ref[...], ddb_ref[...]
        for rev, x_ref, w_ref, sin_ref, dy_ref, dx_ref, dw_ref, ds_scr in (
                (False, xf_ref, wf_ref, sf_ref, dyf_ref, dxf_ref, dwf_ref, dsf_scr),
                (True, xr_ref, wr_ref, sr_ref, dyr_ref, dxr_ref, dwr_ref, dsr_scr)):
            extra = (keep_refs[int(rev)][...],) if nk else ()
            _, vjp = jax.vjp(lambda S, xv, wv, al, db, rev=rev, extra=extra: chunk_fn(S, xv, wv, al, db, rev, *extra),
                             sin_ref[...], x_ref[...], w_ref[...], al_ref[...], db_ref[...])
            ds_scr[...], dx_ref[...], dw_ref[...], dal, ddb = vjp((ds_scr[...], dy_ref[...]))
            dal_sum, ddb_sum = dal_sum + dal, ddb_sum + ddb
        dal_ref[...] = dal_sum
        ddb_ref[...] = ddb_sum

    row = lambda w, idx, b=0: pl.BlockSpec((Q, w), lambda t: (idx(t), b))
    small = pl.BlockSpec(a_log.shape, lambda t: (0, 0))
    state = lambda idx: pl.BlockSpec((STATE_ROWS, HEAD), lambda t: (idx(t), 0))
    return pl.pallas_call(
        body, name=name, grid=(nc,),
        in_specs=[row(W, fidx), row(W, ridx), row(HPAD, fidx, win_blk), row(HPAD, ridx, win_blk), small, small,
                  state(fidx), state(ridx), row(STATE_ROWS, fidx), row(STATE_ROWS, ridx)] + [state(fidx), state(ridx)][:nk],
        out_specs=[row(W, fidx), row(W, ridx), row(HPAD, fidx), row(HPAD, ridx), small, small],
        out_shape=[jax.ShapeDtypeStruct((L, W), f32)] * 2 + [jax.ShapeDtypeStruct((L, HPAD), f32)] * 2
        + [jax.ShapeDtypeStruct(a_log.shape, f32)] * 2,
        scratch_shapes=[pltpu.VMEM((STATE_ROWS, HEAD), f32)] * 2, compiler_params=_params(("arbitrary",)),
    )(x, x, p, p, a_log, dt_bias, s_fwd, s_rev, dy, dy, *kept)


def _cat(xs):
    return jnp.concatenate(xs, axis=-1)


def _a_prep(pa, cos, sin, qn, wuq, kvn, wk, wv):
    cq, ckv, kr = pa[:, :256], pa[:, 256:384], pa[:, 384:512]
    q = _rope(_mm(_rms(cq, qn, A_Q_LORA), wuq, _NN), _cat([cos] * 4), _cat([sin] * 4), A_NOPE, A_ROPE // 4)
    kvh = _rms(ckv, kvn)
    k = _mm(kvh, wk, _NN) + _cat([_rope(kr, cos, sin, A_NOPE, A_ROPE // 4)] * 4)
    return q, k, _mm(kvh, wv, _NN)


def _b_prep(pb, cos, sin, qn, kn):
    cos4, sin4 = _cat([cos] * 4), _cat([sin] * 4)
    q = _pad_heads([_rms(h, qn) for h in _heads(pb[:, :256], 4)])
    k = _pad_heads([_rms(h, kn) for h in _heads(pb[:, 256:384], 2) for _ in range(2)])
    v = _pad_heads([h for h in _heads(pb[:, 384:512], 2) for _ in range(2)])
    return _rope(q, cos4, sin4, 0, HEAD // 4), _rope(k, cos4, sin4, 0, HEAD // 4), v


def _c_act(c):
    return _silu(c)


def _d_act(c):
    s = _silu(c)
    return _cat([h * lax.rsqrt(jnp.sum(h * h, axis=-1, keepdims=True) + 1e-6) for h in _heads(s[:, :512], 8)] + [s[:, 512:]])


def _ffn_act(c):
    return _silu(c[:, :FF_BLK]) * c[:, FF_BLK:]


def _mix_post(oa, ob, yc_f, yc_r, xbc, cz, od_f, od_r, dz, a_out, b_out, dskip, c_out, d_out):
    o_a = _rms(_unpad_heads(oa, 4), a_out)
    o_b = _rms(_unpad_heads(ob, 4), b_out)
    skip = _cat([jnp.broadcast_to(dskip[:, h:h + 1], (1, HEAD)) for h in range(4)])
    o_c = _rms((yc_f + yc_r + xbc[:, :256] * skip) * _silu(cz), c_out)
    o_d = _cat([_rms(h, d_out) for h in _heads(od_f + od_r, 4)]) * _silu(dz)
    return _cat([o_a, o_b, o_c, o_d])


def _post_mix(x, y1, g_pm, g_pf):
    x1 = x + _rms(y1, g_pm)
    return x1, _rms(x1, g_pf)


def _post_ffn(x1, y2, g):
    return x1 + _rms(y2, g)


def _pre_bwd(x, g):
    return x, _rms(x, g)


def _layer_fwd(x, W, tabs, n, side=(), late_side=(), late_weights=None):
    mx = MXU_DTYPE
    cos_a, sin_a, cos_b, sin_b = tabs
    full = lambda a: (a, a.shape[1], 0)
    s = {"x": x}
    (s["h"],) = rowwise(_rms, [full(x)], [W["g_pre"]], [(D_MODEL, mx)], n + "pre")
    p = s["p"] = matmul(s["h"], W["w_in"], "nn", f32, n + "in")
    s["qa"], s["ka"], s["va"] = rowwise(_a_prep, [(p, 512, SEG_A // 512), full(cos_a), full(sin_a)],
                                        [W["a_qn"], W["a_wuq"], W["a_kvn"], W["a_wk"], W["a_wv"]], [(512, mx)] * 3, n + "a_prep")
    s["qb"], s["kb"], s["vb"] = rowwise(_b_prep, [(p, 512, SEG_B // 512), full(cos_b), full(sin_b)], [W["b_qn"], W["b_kn"]],
                                        [(512, mx)] * 3, n + "b_prep")
    s["oa"], s["lsea"], side_out = flash_fwd(s["qa"], s["ka"], s["va"], 4, 4, (A_NOPE + A_ROPE) ** -0.5, n + "a_attn", side)
    s["ob"], s["lseb"], late_out = flash_fwd(s["qb"], s["kb"], s["vb"], 4, 4, HEAD ** -0.5, n + "b_attn", late_side)
    if late_weights is not None:
        W = {**W, **late_weights(late_out)}
    xbc, s["c_c"] = conv_fwd(_c_act, p, SEG_CX // 512, 1, 512, W["c_cw"], W["c_cb"], 512, f32, n + "c_conv")
    s["xbc"] = xbc
    s["yc_f"], s["yc_r"], s["sc_f"], s["sc_r"] = scan_fwd(_ssd_chunk, min(SSD_CHUNK, xbc.shape[0]), xbc, p, SEG_CDT // HPAD, W["c_alog"], W["c_dtb"], n + "c_ssd")
    qkv, s["d_c"] = conv_fwd(_d_act, p, SEG_DX // 768, 1, 768, W["d_cw"], W["d_cb"], 768, f32, n + "d_conv")
    s["qkv"] = qkv
    s["od_f"], s["od_r"], s["sd_f"], s["sd_r"], s["td_f"], s["td_r"] = scan_fwd(
        _dn_chunk, DN_CHUNK, qkv, p, SEG_DAB // HPAD, W["d_alog"], W["d_dtb"], n + "d_dn", keeps=True)
    (s["omix"],) = rowwise(_mix_post, _mix_rows(s), _mix_params(W), [(D_MODEL, mx)], n + "mix_post")
    s["y1"] = matmul(s["omix"], W["w_out"], "nn", f32, n + "out")
    s["x1"], s["h2"] = rowwise(_post_mix, [full(x), full(s["y1"])], [W["g_pm"], W["g_pf"]], [(D_MODEL, f32), (D_MODEL, mx)], n + "post_mix")
    s["u"] = matmul(s["h2"], W["f_win"], "nn", f32, n + "f_in")
    s["a"], s["f_c"] = conv_fwd(_ffn_act, s["u"], 0, D_FF // FF_BLK, 2 * FF_BLK, W["f_cw"], W["f_cb"], FF_BLK, mx, n + "f_conv", tm=1024)
    s["y2"] = matmul(s["a"], W["f_wout"], "nn", f32, n + "f_out")
    (x2,) = rowwise(_post_ffn, [full(s["x1"]), full(s["y2"])], [W["g_po"]], [(D_MODEL, f32)], n + "post_ffn")
    return x2, s, side_out, W


def _mix_rows(s):
    p = s["p"]
    return [(s["oa"], 512, 0), (s["ob"], 512, 0), (s["yc_f"], 256, 0), (s["yc_r"], 256, 0), (s["xbc"], 512, 0),
            (p, 256, SEG_CZ // 256), (s["od_f"], 256, 0), (s["od_r"], 256, 0), (p, 256, SEG_DZ // 256)]


def _mix_params(W):
    return [W["a_out"], W["b_out"], W["c_dskip"], W["c_out"], W["d_out"]]


def _layer_bwd(s, dx2, W, tabs, n, side=(), late_side=None):
    cos_a, sin_a, cos_b, sin_b = tabs
    full = lambda a: (a, a.shape[1], 0)
    p, g = s["p"], {}
    (dx1, dy2), (g["g_po"],) = rowwise_vjp(_post_ffn, [full(s["x1"]), full(s["y2"])], [W["g_po"]], [full(dx2)], n + "post_ffn_b", [0, 1], [0],
                                           row_dtypes=[f32, MXU_DTYPE])
    da = matmul(dy2, W["f_wout"], "nt", f32, n + "f_out_dx")
    g["f_wout"] = matmul(s["a"], dy2, "tn", f32, n + "f_out_dw")
    du, g["f_cw"], g["f_cb"] = conv_bwd(_ffn_act, s["u"], s["f_c"], 0, D_FF // FF_BLK, 2 * FF_BLK, W["f_cw"], [da], FF_BLK, n + "f_conv_b", tm=1024,
                                        dx_dtype=MXU_DTYPE)
    dh2 = matmul(du, W["f_win"], "nt", f32, n + "f_in_dx")
    g["f_win"] = matmul(s["h2"], du, "tn", f32, n + "f_in_dw")
    (dx, dy1), (g["g_pm"], g["g_pf"]) = rowwise_vjp(_post_mix, [full(s["x"]), full(s["y1"])], [W["g_pm"], W["g_pf"]],
                                                   [full(dx1), full(dh2)], n + "post_mix_b", [0, 1], [0, 1],
                                                   row_dtypes=[f32, MXU_DTYPE])
    domix = matmul(dy1, W["w_out"], "nt", f32, n + "out_dx")
    g["w_out"] = matmul(s["omix"], dy1, "tn", f32, n + "out_dw")
    (doa, dob, dyc, dxbc_skip, dcz, dod, ddz), (g["a_out"], g["b_out"], g["c_dskip"], g["c_out"], g["d_out"]) = rowwise_vjp(
        _mix_post, _mix_rows(s), _mix_params(W), [full(domix)], n + "mix_post_b", [0, 1, 2, 4, 5, 6, 8], list(range(5)))
    dqkv_f, dqkv_r, ddab_f, ddab_r, g["d_alog"], g["d_dtb"] = scan_bwd(
        _dn_chunk, DN_CHUNK, s["qkv"], p, SEG_DAB // HPAD, W["d_alog"], W["d_dtb"], s["sd_f"], s["sd_r"], dod, n + "d_dn_b",
        kept=(s["td_f"], s["td_r"]))
    ddx, g["d_cw"], _ = conv_bwd(_d_act, p, s["d_c"], SEG_DX // 768, 1, 768, W["d_cw"], [dqkv_f, dqkv_r], 768, n + "d_conv_b")
    dxbc_f, dxbc_r, dcdt_f, dcdt_r, g["c_alog"], g["c_dtb"] = scan_bwd(
        _ssd_chunk, min(SSD_CHUNK, p.shape[0]), s["xbc"], p, SEG_CDT // HPAD, W["c_alog"], W["c_dtb"], s["sc_f"], s["sc_r"], dyc, n + "c_ssd_b")
    dcx, g["c_cw"], g["c_cb"] = conv_bwd(_c_act, p, s["c_c"], SEG_CX // 512, 1, 512, W["c_cw"], [dxbc_f, dxbc_r, dxbc_skip], 512,
                                         n + "c_conv_b")
    (dsmall,) = rowwise(lambda a, b, c, d: _cat([a + b, c + d]), [full(dcdt_f), full(dcdt_r), full(ddab_f), full(ddab_r)], [],
                        [(2 * HPAD, f32)], n + "dwin_sum")
    dqa, dka, dva, side_out = flash_bwd(s["qa"], s["ka"], s["va"], s["oa"], s["lsea"], doa, 4, 4, (A_NOPE + A_ROPE) ** -0.5,
                                        n + "a_attn_b", side)
    dqb, dkb, dvb, late_out = flash_bwd(s["qb"], s["kb"], s["vb"], s["ob"], s["lseb"], dob, 4, 4, HEAD ** -0.5, n + "b_attn_b",
                                        late_side(g) if late_side is not None else ())
    (dpa,), (g["a_qn"], g["a_wuq"], g["a_kvn"], g["a_wk"], g["a_wv"]) = rowwise_vjp(
        _a_prep, [(p, 512, SEG_A // 512), full(cos_a), full(sin_a)], [W["a_qn"], W["a_wuq"], W["a_kvn"], W["a_wk"], W["a_wv"]],
        [full(dqa), full(dka), full(dva)], n + "a_prep_b", [0], list(range(5)))
    (dpb,), (g["b_qn"], g["b_kn"]) = rowwise_vjp(_b_prep, [(p, 512, SEG_B // 512), full(cos_b), full(sin_b)], [W["b_qn"], W["b_kn"]],
                                               [full(dqb), full(dkb), full(dvb)], n + "b_prep_b", [0], [0, 1])
    dp = jnp.concatenate([dpa, dpb, dcx, ddx, dcz, ddz, dsmall], axis=1).astype(MXU_DTYPE)
    dh = matmul(dp, W["w_in"], "nt", f32, n + "in_dx")
    g["w_in"] = matmul(s["h"], dp, "tn", f32, n + "in_dw")
    (dx0,), (g["g_pre"],) = rowwise_vjp(_pre_bwd, [full(s["x"])], [W["g_pre"]], [full(dx), full(dh)], n + "pre_b", [0], [0])
    return dx0, g, side_out, late_out


def loss_and_grad(y, target, name):
    L, D = y.shape
    tm = _tile(L, 512)

    def body(y_ref, t_ref, loss_ref, dy_ref):
        e = y_ref[...] - t_ref[...]
        dy_ref[...] = e * (1.0 / D)
        part = 0.5 * jnp.sum(jnp.sum(e * e, axis=1, keepdims=True) * (1.0 / D), axis=0, keepdims=True)

        @pl.when(pl.program_id(0) == 0)
        def _():
            loss_ref[...] = part

        @pl.when(pl.program_id(0) > 0)
        def _():
            loss_ref[...] += part

    row = pl.BlockSpec((tm, D), lambda i: (i, 0))
    return pl.pallas_call(
        body, name=name, grid=(L // tm,), in_specs=[row, row], out_specs=[pl.BlockSpec((1, 1), lambda i: (0, 0)), row],
        out_shape=[jax.ShapeDtypeStruct((1, 1), f32), jax.ShapeDtypeStruct((L, D), f32)], compiler_params=_params(("arbitrary",)),
    )(y, target)


_IN_SEGS = [(0, 192), (256, 128), (384 + A_NOPE, 32), (512, 256), (768, 128), (896, 128), (SEG_CZ, 256), (SEG_CX, 512),
            (SEG_CDT, 8), (SEG_DX, 768), (SEG_DZ, 256), (SEG_DAB, 16)]


def _pad_in(w):
    src, pieces = 0, {}
    for off, wd in _IN_SEGS:
        pieces[off] = w[..., src:src + wd]
        src += wd
    out, pos = [], 0
    for off in sorted(pieces):
        if off > pos:
            out.append(jnp.zeros(w.shape[:-1] + (off - pos,), w.dtype))
        out.append(pieces[off])
        pos = off + pieces[off].shape[-1]
    out.append(jnp.zeros(w.shape[:-1] + (PIN - pos,), w.dtype))
    return jnp.concatenate(out, axis=-1)


def _unpad_in(wp):
    return jnp.concatenate([wp[..., off:off + wd] for off, wd in _IN_SEGS], axis=-1)


def _pad_last(a, n):
    return jnp.pad(a, [(0, 0)] * (a.ndim - 1) + [(0, n - a.shape[-1])])


def _ff_interleave(w):
    nb = D_FF // FF_BLK
    if w.size <= 4 * D_FF * 2:
        lead = w.shape[:-1]
        return jnp.stack([w[..., :D_FF].reshape(lead + (nb, FF_BLK)), w[..., D_FF:].reshape(lead + (nb, FF_BLK))],
                         axis=-2).reshape(lead + (2 * D_FF,))
    return jnp.concatenate([w[..., off + j * FF_BLK:off + (j + 1) * FF_BLK] for j in range(nb) for off in (0, D_FF)], axis=-1)


def _ff_deinterleave(w):
    nb = D_FF // FF_BLK
    if w.size <= 4 * D_FF * 2:
        lead = w.shape[:-1]
        t = w.reshape(lead + (nb, 2, FF_BLK))
        return jnp.concatenate([t[..., 0, :].reshape(lead + (D_FF,)), t[..., 1, :].reshape(lead + (D_FF,))], axis=-1)
    return jnp.concatenate([w[..., (2 * j + half) * FF_BLK:(2 * j + half + 1) * FF_BLK] for half in (0, 1) for j in range(nb)], axis=-1)


def _row(v):
    return v.reshape(1, -1)


_LATE = ["w_out", "f_w_in", "f_w_out"]


def _late_kernel_weights(P, l, wdt):
    return {"w_out": P["w_out"][l].astype(wdt), "f_win": _ff_interleave(P["f_w_in"][l]).astype(wdt), "f_wout": P["f_w_out"][l].astype(wdt)}


def _kernel_weights(P, l, wdt):
    late = {} if P["w_out"][l] is None else _late_kernel_weights(P, l, wdt)
    uq = P["a_w_uq"][l].reshape(A_Q_LORA, 4, A_NOPE + A_ROPE)
    ukv = P["a_w_ukv"][l].reshape(A_KV_LORA, 4, 2 * HEAD)
    z = jnp.zeros((A_KV_LORA, 4, HEAD), ukv.dtype)
    return {
        "g_pre": _row(P["pre_mix_norm"][l]), "w_in": _pad_in(P["w_in"][l]).astype(wdt),
        "a_qn": _pad_last(_row(P["a_q_norm"][l]), 256),
        "a_wuq": jnp.pad(_pad_last(uq, HPAD).reshape(A_Q_LORA, 4 * HPAD), ((0, 256 - A_Q_LORA), (0, 0))).astype(wdt),
        "a_kvn": _row(P["a_kv_norm"][l]),
        "a_wk": jnp.concatenate([ukv[..., :HEAD], z], axis=-1).reshape(A_KV_LORA, 4 * HPAD).astype(wdt),
        "a_wv": jnp.concatenate([ukv[..., HEAD:], z], axis=-1).reshape(A_KV_LORA, 4 * HPAD).astype(wdt),
        "a_out": _row(P["a_out_norm"][l]), "b_qn": _row(P["b_q_norm"][l]), "b_kn": _row(P["b_k_norm"][l]), "b_out": _row(P["b_out_norm"][l]),
        "c_cw": P["c_conv_w"][l], "c_cb": _row(P["c_conv_b"][l]), "c_alog": P["c_a_log"][l], "c_dtb": P["c_dt_bias"][l],
        "c_dskip": _row(P["c_d_skip"][l]), "c_out": _row(P["c_out_norm"][l]),
        "d_cw": P["d_conv_w"][l], "d_cb": jnp.zeros((1, 768), f32), "d_alog": P["d_a_log"][l], "d_dtb": P["d_dt_bias"][l],
        "d_out": _row(P["d_out_norm"][l]), "g_pm": _row(P["post_mix_norm"][l]), "g_pf": _row(P["pre_ffn_norm"][l]),
        "f_cw": _ff_interleave(P["f_conv_w"][l]), "f_cb": _row(_ff_interleave(P["f_conv_b"][l])),
        "g_po": _row(P["post_ffn_norm"][l]), **late,
    }


def _reference_grads(g):
    uq = g["a_wuq"][:A_Q_LORA].reshape(A_Q_LORA, 4, HPAD)[..., :A_NOPE + A_ROPE].reshape(A_Q_LORA, 4 * (A_NOPE + A_ROPE))
    wk = g["a_wk"].reshape(A_KV_LORA, 4, HPAD)[..., :HEAD]
    wv = g["a_wv"].reshape(A_KV_LORA, 4, HPAD)[..., :HEAD]
    return {
        "pre_mix_norm": g["g_pre"][0], "w_in": _unpad_in(g["w_in"]), "a_q_norm": g["a_qn"][0, :A_Q_LORA], "a_w_uq": uq,
        "a_kv_norm": g["a_kvn"][0], "a_w_ukv": jnp.concatenate([wk, wv], axis=-1).reshape(A_KV_LORA, 8 * HEAD),
        "a_out_norm": g["a_out"][0], "b_q_norm": g["b_qn"][0], "b_k_norm": g["b_kn"][0], "b_out_norm": g["b_out"][0],
        "c_conv_w": g["c_cw"], "c_conv_b": g["c_cb"][0], "c_a_log": g["c_alog"], "c_dt_bias": g["c_dtb"], "c_d_skip": g["c_dskip"][0],
        "c_out_norm": g["c_out"][0], "d_conv_w": g["d_cw"], "d_a_log": g["d_alog"], "d_dt_bias": g["d_dtb"], "d_out_norm": g["d_out"][0],
        "w_out": g["w_out"], "post_mix_norm": g["g_pm"][0], "pre_ffn_norm": g["g_pf"][0], "f_w_in": _ff_deinterleave(g["f_win"]),
        "f_conv_w": _ff_deinterleave(g["f_cw"]), "f_conv_b": _ff_deinterleave(g["f_cb"][0]), "f_w_out": g["f_wout"],
        "post_ffn_norm": g["g_po"][0],
    }


def _rope_tables(L):
    def tables(rot):
        rows = L // GRID_W
        row = jnp.repeat(jnp.arange(rows), GRID_W).astype(f32)
        col = jnp.tile(jnp.arange(GRID_W), rows).astype(f32)
        sec = rot // 2
        inv = ROPE_BASE ** (-jnp.arange(0, sec, 2, dtype=f32) / sec)
        ang = jnp.concatenate([row[:, None] * inv] * 2 + [col[:, None] * inv] * 2, axis=-1)
        return jnp.cos(ang), jnp.sin(ang)

    ca, sa = tables(A_ROPE)
    cb, sb = tables(HEAD)
    one, zero = jnp.ones, jnp.zeros
    return (jnp.concatenate([one((L, A_NOPE), f32), ca, one((L, HPAD - A_NOPE - A_ROPE), f32)], axis=1),
            jnp.concatenate([zero((L, A_NOPE), f32), sa, zero((L, HPAD - A_NOPE - A_ROPE), f32)], axis=1),
            jnp.concatenate([cb, one((L, HPAD - HEAD), f32)], axis=1), jnp.concatenate([sb, zero((L, HPAD - HEAD), f32)], axis=1))


def exchange(items, name):
    n = len(items)

    def body(*refs):
        start, wait = _exchange_ops([g for _, g in items], refs[:n], refs[n:2 * n], *refs[2 * n:])
        start()
        wait()

    ex = _exchange_io(items)
    res = pl.pallas_call(body, name=name, in_specs=ex.specs, out_specs=ex.specs, out_shape=ex.out_shape,
                         scratch_shapes=ex.scratch)(*ex.args)
    return list(res)


def _exchange_ops(modes, src_refs, out_refs, send_sems, recv_sems, local_sems):
    n = len(modes)
    x, y, c = lax.axis_index("x"), lax.axis_index("y"), lax.axis_index("c")
    me = 4 * x + 2 * y + c

    def copy(a, k, arriving):
        px, py, pc = (1 - x if k & 4 else x), (1 - y if k & 2 else y), (1 - c if k & 1 else c)
        pid = 4 * px + 2 * py + pc
        sem = a * (N_DEV - 1) + k - 1
        return pltpu.make_async_remote_copy(
            src_ref=src_refs[a] if modes[a] else src_refs[a].at[pid], dst_ref=out_refs[a].at[pid if arriving else me],
            send_sem=send_sems.at[sem], recv_sem=recv_sems.at[sem], device_id=(px, py, pc), device_id_type=pl.DeviceIdType.MESH)

    def local(a):
        return pltpu.make_async_copy(src_refs[a] if modes[a] else src_refs[a].at[me], out_refs[a].at[me], local_sems.at[a])

    pairs = [(a, k) for k in range(1, N_DEV) for a in range(n)]

    def start():
        for a, k in pairs:
            copy(a, k, False).start()
        for a in range(n):
            local(a).start()

    def wait():
        for a, k in pairs:
            copy(a, k, False).wait_send()
        for a, k in pairs:
            copy(a, k, True).wait_recv()
        for a in range(n):
            local(a).wait()

    return start, wait


class _ExchangeIO:
    def __init__(self, items):
        n = len(items)
        self.args = [s for s, _ in items]
        self.specs = [pl.BlockSpec(memory_space=pl.ANY)] * n
        self.out_shape = [jax.ShapeDtypeStruct((N_DEV,) + tuple(s.shape if g else s.shape[1:]), s.dtype) for s, g in items]
        self.scratch = [pltpu.SemaphoreType.DMA((n * (N_DEV - 1),)), pltpu.SemaphoreType.DMA((n * (N_DEV - 1),)),
                        pltpu.SemaphoreType.DMA((n,))] if n else []


def _exchange_io(items):
    return _ExchangeIO(list(items))


def _with_exchange(body, n_in, n_out, side, grid):
    ex = _exchange_io(side)
    n = len(ex.args)
    if not n:
        return body, ex
    modes = [g for _, g in side]

    def wrapped(*refs):
        ins, src = refs[:n_in], refs[n_in:n_in + n]
        outs, dst = refs[n_in + n:n_in + n + n_out], refs[n_in + n + n_out:n_in + 2 * n + n_out]
        sems = refs[n_in + 2 * n + n_out:]
        start, wait = _exchange_ops(modes, src, dst, *sems)
        ids = [pl.program_id(a) for a in range(len(grid))]
        first, last = ids[0] == 0, ids[0] == grid[0] - 1
        for a in range(1, len(grid)):
            first, last = first & (ids[a] == 0), last & (ids[a] == grid[a] - 1)
        pl.when(first)(start)
        body(*ins, *outs)
        pl.when(last)(wait)

    return wrapped, ex


LANES = 1024


def adamw(gstacks, w, m, v, name):
    D, R, C = w.shape
    S = gstacks[0].shape[0]
    tr = R if R <= 512 else _tile(R, 256)

    def body(*refs):
        g_refs = refs[:D]
        w_ref, m_ref, v_ref, go_ref, d_ref, mo_ref, vo_ref = refs[D:]
        layer = pl.program_id(0)
        g = None
        for d, g_ref in enumerate(g_refs):
            gd = g_ref[0].astype(f32)
            for sl in range(1, S):
                gd = gd + g_ref[sl].astype(f32)
            g = gd if g is None else jnp.where(layer == d, gd, g)
        g = g[None]
        m_new = ADAM_B1 * m_ref[...] + (1.0 - ADAM_B1) * g
        v_new = ADAM_B2 * v_ref[...] + (1.0 - ADAM_B2) * jnp.square(g)
        m_hat = m_new / (1.0 - ADAM_B1 ** ADAM_STEP)
        v_hat = v_new / (1.0 - ADAM_B2 ** ADAM_STEP)
        go_ref[...] = g
        d_ref[...] = -ADAM_LR * (m_hat / (jnp.sqrt(v_hat) + ADAM_EPS) + ADAM_WD * w_ref[...])
        mo_ref[...] = m_new
        vo_ref[...] = v_new

    row = pl.BlockSpec((1, tr, C), lambda l, i: (l, i, 0))
    g_specs = [pl.BlockSpec((S, tr, C), lambda l, i, d=d: (0, jnp.where(l == d, i, 0), 0)) for d in range(D)]
    return pl.pallas_call(
        body, name=name, grid=(D, R // tr), in_specs=g_specs + [row, row, row],
        out_specs=[row] * 4, out_shape=[jax.ShapeDtypeStruct((D, R, C), f32)] * 4, compiler_params=_params(("parallel", "parallel")),
    )(*gstacks, w, m, v)


def sum_slots(gstack, name):
    S, R, _ = gstack.shape

    def body(g_ref, o_ref):
        g = g_ref[0]
        for sl in range(1, S):
            g = g + g_ref[sl]
        o_ref[...] = g

    return pl.pallas_call(body, name=name, out_shape=jax.ShapeDtypeStruct((R, LANES), f32), compiler_params=_params())(gstack)


def _pack(parts, rows, dtype=f32):
    flat = jnp.concatenate([q.reshape(-1).astype(dtype) for q in parts])
    return jnp.pad(flat, (0, rows * LANES - flat.shape[0])).reshape(rows, LANES)


def _unpack(buf, shapes):
    lead = buf.shape[:-2]
    flat = buf.reshape(lead + (-1,))
    out, off = [], 0
    for shp in shapes:
        n = math.prod(shp)
        out.append(flat[..., off:off + n].reshape(lead + tuple(shp)))
        off += n
    return out


_WEIGHTS = ["pre_mix_norm", "w_in", "a_q_norm", "a_w_uq", "a_kv_norm", "a_w_ukv", "a_out_norm", "b_q_norm", "b_k_norm", "b_out_norm",
            "c_conv_w", "c_conv_b", "c_a_log", "c_dt_bias", "c_d_skip", "c_out_norm", "d_conv_w", "d_a_log", "d_dt_bias", "d_out_norm",
            "w_out", "post_mix_norm", "pre_ffn_norm", "f_w_in", "f_conv_w", "f_conv_b", "f_w_out", "post_ffn_norm"]
_BIG = {"w_in": 1, "a_w_uq": 1, "a_w_ukv": 1, "w_out": 0, "f_w_in": 1, "f_w_out": 0}
_CONV = ["c_conv_w", "d_conv_w", "f_conv_w"]
_REP = [n for n in _WEIGHTS if n not in _BIG and n not in _CONV]
SMALLG_ROWS, SMALLW_ROWS = 64, 32


def _join(blocks, axis):
    return jnp.concatenate([blocks[d] for d in range(N_DEV)], axis=axis)


def _split(full, axis):
    n = full.shape[axis] // N_DEV
    return jnp.stack([lax.slice_in_dim(full, d * n, (d + 1) * n, axis=axis) for d in range(N_DEV)])


def kernel(x, pre_mix_norm, w_in, a_q_norm, a_w_uq, a_kv_norm, a_w_ukv, a_out_norm, b_q_norm, b_k_norm, b_out_norm, c_conv_w, c_conv_b, c_a_log, c_dt_bias, c_d_skip, c_out_norm, d_conv_w, d_a_log, d_dt_bias, d_out_norm, w_out, post_mix_norm, pre_ffn_norm, f_w_in, f_conv_w, f_conv_b, f_w_out, post_ffn_norm, loss_target, m_pre_mix_norm, m_w_in, m_a_q_norm, m_a_w_uq, m_a_kv_norm, m_a_w_ukv, m_a_out_norm, m_b_q_norm, m_b_k_norm, m_b_out_norm, m_c_conv_w, m_c_conv_b, m_c_a_log, m_c_dt_bias, m_c_d_skip, m_c_out_norm, m_d_conv_w, m_d_a_log, m_d_dt_bias, m_d_out_norm, m_w_out, m_post_mix_norm, m_pre_ffn_norm, m_f_w_in, m_f_conv_w, m_f_conv_b, m_f_w_out, m_post_ffn_norm, v_pre_mix_norm, v_w_in, v_a_q_norm, v_a_w_uq, v_a_kv_norm, v_a_w_ukv, v_a_out_norm, v_b_q_norm, v_b_k_norm, v_b_out_norm, v_c_conv_w, v_c_conv_b, v_c_a_log, v_c_dt_bias, v_c_d_skip, v_c_out_norm, v_d_conv_w, v_d_a_log, v_d_dt_bias, v_d_out_norm, v_w_out, v_post_mix_norm, v_pre_ffn_norm, v_f_w_in, v_f_conv_w, v_f_conv_b, v_f_w_out, v_post_ffn_norm):
    given = dict(locals())
    w = {n: given[n] for n in _WEIGHTS}
    mom = {n: given["m_" + n] for n in _WEIGHTS}
    var = {n: given["v_" + n] for n in _WEIGHTS}
    me = 4 * lax.axis_index("x") + 2 * lax.axis_index("y") + lax.axis_index("c")
    layered = lambda names: [(l, n) for l in range(DEPTH) for n in names]

    sharded = list(_BIG) + _CONV
    early = [n for n in sharded if n not in _LATE]
    weight_items = lambda l, names: [(w[n][l].astype(MXU_DTYPE if n in _BIG else f32), True) for n in names]
    grad_items = lambda g, names: [(_split(g[n], _BIG[n]).astype(WIRE_DTYPE), False) for n in names]
    P = {n: [w[n][l] for l in range(DEPTH)] for n in _REP}
    P.update({n: [None] * DEPTH for n in sharded})

    def place(l, names, gathered):
        for n, got in zip(names, gathered):
            P[n][l] = _join(got, _BIG.get(n, 1))

    def late_weights_l0(gathered):
        place(0, _LATE, gathered)
        return _late_kernel_weights(P, 0, MXU_DTYPE)

    tabs = _rope_tables(x.shape[1])
    place(0, early, exchange(weight_items(0, early), "gather_weights"))
    x1, s0, got, W0 = _layer_fwd(x[0], _kernel_weights(P, 0, MXU_DTYPE), tabs, "l0_", weight_items(1, sharded),
                                 weight_items(0, _LATE), late_weights_l0)
    place(1, sharded, got)
    x2, s1, _, W1 = _layer_fwd(x1, _kernel_weights(P, 1, MXU_DTYPE), tabs, "l1_")
    loss, dx = loss_and_grad(x2, loss_target[0], "loss")
    loss = lax.psum(loss[0, 0], ("x", "y", "c"))

    def late_grads_l0(g):
        return grad_items({"w_out": g["w_out"], "f_w_in": _ff_deinterleave(g["f_win"]), "f_w_out": g["f_wout"]}, _LATE)

    dx, g1, _, _ = _layer_bwd(s1, dx, W1, tabs, "l1_")
    g1 = _reference_grads(g1)
    dx, g0, recv1, recv0_late = _layer_bwd(s0, dx, W0, tabs, "l0_", grad_items(g1, _BIG), late_grads_l0)
    grads = [_reference_grads(g0), g1]
    small = _REP + _CONV
    full_shapes = [grads[l][n].shape for l, n in layered(small)]
    first = [n for n in _BIG if n not in _LATE]
    *recv0_first, gsmall = exchange(grad_items(grads[0], first) + [(_pack([grads[l][n] for l, n in layered(small)], SMALLG_ROWS), True)],
                                    "exchange_grads")
    recv0 = dict(zip(first + _LATE, recv0_first + recv0_late))

    out = {}
    for n, got1 in zip(_BIG, recv1):
        out[n] = adamw([recv0[n], got1], w[n], mom[n], var[n], "adamw_" + n)
    gsum = dict(zip(layered(small), _unpack(sum_slots(gsmall, "sum_small_grads"), full_shapes)))
    for l, n in layered(_CONV):
        cols = w[n].shape[2]
        gsum[(l, n)] = lax.dynamic_slice_in_dim(gsum[(l, n)], me * cols, cols, axis=1)
    held_shapes = [w[n].shape[1:] for _, n in layered(small)]
    pack_small = lambda d: _pack([d[n][l] for l, n in layered(small)], SMALLW_ROWS)[None]
    res_small = adamw([_pack([gsum[k] for k in layered(small)], SMALLW_ROWS)[None]], pack_small(w), pack_small(mom),
                      pack_small(var), "adamw_small")
    per_kind = [dict(zip(layered(small), _unpack(buf[0], held_shapes))) for buf in res_small]
    for n in small:
        out[n] = [jnp.stack([per_kind[kind][(l, n)] for l in range(DEPTH)]) for kind in range(4)]
    return (loss, dx[None], *[out[n][kind] for kind in range(4) for n in _WEIGHTS])
```

```python
import functools
import math

import jax
import jax.numpy as jnp
from jax import lax
from jax.experimental import pallas as pl
from jax.experimental.pallas import tpu as pltpu

f32 = jnp.float32
MXU_DTYPE = jnp.bfloat16
WIRE_DTYPE = jnp.bfloat16
HI = lax.Precision.HIGHEST

D_MODEL = 1024
DEPTH = 2
GRID_W = 64
ROPE_BASE = 10000.0
EPS = 1e-6
A_Q_LORA, A_KV_LORA, A_ROPE, A_NOPE = 192, 128, 32, 64
SSD_CHUNK = 256
DN_CHUNK = 64
HEAD = 64
HPAD = 128
D_FF = 2816
FF_BLK = 256
N_DEV = 8
ADAM_LR, ADAM_B1, ADAM_B2, ADAM_EPS, ADAM_WD, ADAM_STEP = 0.001, 0.9, 0.999, 1e-08, 0.01, 10

V7X_VMEM_BYTES = 64 * 2 ** 20
VMEM_LIMIT = (V7X_VMEM_BYTES * 3) // 4

PIN = 3072
SEG_A, SEG_B, SEG_CX, SEG_DX, SEG_CZ, SEG_DZ, SEG_CDT, SEG_DAB = 0, 512, 1024, 1536, 2304, 2560, 2816, 2944


def _tile(n, pref):
    for t in (512, 256, 128, 64, 32, 16, 8):
        if t <= pref and n % t == 0:
            return t
    raise ValueError(f"no tile for {n}")


LANE = 128


def _tile_div(n, pref):
    if n <= pref:
        return n
    return max(d for d in range(LANE, pref + 1, LANE) if n % d == 0)


def _params(sem=None):
    return pltpu.CompilerParams(vmem_limit_bytes=VMEM_LIMIT, dimension_semantics=sem)


def _tup(r):
    return tuple(r) if isinstance(r, (tuple, list)) else (r,)


def matmul(a, b, form, out_dtype, name):
    if form == "nn":
        (M, K), N = a.shape, b.shape[1]
    elif form == "nt":
        (M, K), N = a.shape, b.shape[0]
    else:
        (K, M), N = a.shape, b.shape[1]
    if form == "tn":
        tm, tn, tk = (512 if M % 512 == 0 else _tile_div(M, 1408)), _tile_div(N, 1408), _tile_div(K, 2048)
    else:
        tm, tn, tk = _tile_div(M, 512), _tile_div(N, 2816), _tile_div(K, 5632 if jnp.dtype(a.dtype).itemsize == 2 else 2816)
    nk = K // tk
    dims = {"nn": ((1,), (0,)), "nt": ((1,), (1,)), "tn": ((0,), (0,))}[form]

    def body(a_ref, b_ref, o_ref, *acc):
        part = lax.dot_general(a_ref[...].astype(MXU_DTYPE), b_ref[...].astype(MXU_DTYPE), (dims, ((), ())),
                               preferred_element_type=f32)
        if nk == 1:
            o_ref[...] = part.astype(out_dtype)
            return
        (acc_ref,) = acc
        k = pl.program_id(2)

        @pl.when(k == 0)
        def _():
            acc_ref[...] = part

        @pl.when((k > 0) & (k < nk - 1))
        def _():
            acc_ref[...] += part

        @pl.when(k == nk - 1)
        def _():
            o_ref[...] = (acc_ref[...] + part).astype(out_dtype)

    a_spec = pl.BlockSpec((tk, tm), lambda i, j, k: (k, i)) if form == "tn" else pl.BlockSpec((tm, tk), lambda i, j, k: (i, k))
    b_spec = pl.BlockSpec((tn, tk), lambda i, j, k: (j, k)) if form == "nt" else pl.BlockSpec((tk, tn), lambda i, j, k: (k, j))
    return pl.pallas_call(
        body, name=name, grid=(M // tm, N // tn, nk), in_specs=[a_spec, b_spec],
        out_specs=pl.BlockSpec((tm, tn), lambda i, j, k: (i, j)), out_shape=jax.ShapeDtypeStruct((M, N), out_dtype),
        scratch_shapes=[pltpu.VMEM((tm, tn), f32)] if nk > 1 else [], compiler_params=_params(("parallel", "parallel", "arbitrary")),
    )(a, b)


def _row_specs(rows, tm):
    return [pl.BlockSpec((tm, w), lambda i, b=b: (i, b)) for (_, w, b) in rows]


def _full_specs(params):
    return [pl.BlockSpec(p.shape, lambda i: (0, 0)) for p in params]


def rowwise(fn, rows, params, outs, name, tm=512):
    L = rows[0][0].shape[0]
    tm = _tile(L, tm)
    n_in = len(rows) + len(params)

    def body(*refs):
        res = _tup(fn(*[r[...].astype(f32) for r in refs[:n_in]]))
        for o_ref, r in zip(refs[n_in:], res, strict=True):
            o_ref[...] = r.astype(o_ref.dtype)

    res = pl.pallas_call(
        body, name=name, grid=(L // tm,), in_specs=_row_specs(rows, tm) + _full_specs(params),
        out_specs=[pl.BlockSpec((tm, w), lambda i: (i, 0)) for (w, _) in outs],
        out_shape=[jax.ShapeDtypeStruct((L, w), dt) for (w, dt) in outs], compiler_params=_params(("parallel",)),
    )(*[r[0] for r in rows], *params)
    return list(res)


def rowwise_vjp(fn, rows, params, cts, name, row_grads, param_grads, tm=512, row_dtypes=None):
    L = rows[0][0].shape[0]
    tm = _tile(L, tm)
    nr, npar, nct = len(rows), len(params), len(cts)
    row_dtypes = [f32] * len(row_grads) if row_dtypes is None else row_dtypes

    def body(*refs):
        i = pl.program_id(0)
        rv = [r[...].astype(f32) for r in refs[:nr]]
        pv = [r[...].astype(f32) for r in refs[nr:nr + npar]]
        cv = tuple(r[...].astype(f32) for r in refs[nr + npar:nr + npar + nct])
        out_refs = refs[nr + npar + nct:]

        def g(*diff):
            rr, pp = list(rv), list(pv)
            for k, v in zip(row_grads, diff[:len(row_grads)]):
                rr[k] = v
            for k, v in zip(param_grads, diff[len(row_grads):]):
                pp[k] = v
            return _tup(fn(*rr, *pp))

        _, vjp = jax.vjp(g, *[rv[k] for k in row_grads], *[pv[k] for k in param_grads])
        grads = vjp(cv)
        for o_ref, gval in zip(out_refs[:len(row_grads)], grads[:len(row_grads)]):
            o_ref[...] = gval.astype(o_ref.dtype)
        for o_ref, gval in zip(out_refs[len(row_grads):], grads[len(row_grads):]):
            @pl.when(i == 0)
            def _(o_ref=o_ref, gval=gval):
                o_ref[...] = gval

            @pl.when(i > 0)
            def _(o_ref=o_ref, gval=gval):
                o_ref[...] += gval

    out_specs = [pl.BlockSpec((tm, rows[k][1]), lambda i: (i, 0)) for k in row_grads] + \
                [pl.BlockSpec(params[k].shape, lambda i: (0, 0)) for k in param_grads]
    out_shape = [jax.ShapeDtypeStruct((L, rows[k][1]), dt) for k, dt in zip(row_grads, row_dtypes)] + \
                [jax.ShapeDtypeStruct(params[k].shape, f32) for k in param_grads]
    res = pl.pallas_call(
        body, name=name, grid=(L // tm,), in_specs=_row_specs(rows, tm) + _full_specs(params) + _row_specs(cts, tm),
        out_specs=out_specs, out_shape=out_shape, compiler_params=_params(("arbitrary",)),
    )(*[r[0] for r in rows], *params, *[c[0] for c in cts])
    res = list(res)
    return res[:len(row_grads)], res[len(row_grads):]


def _rms(x, w, n=None):
    n = x.shape[-1] if n is None else n
    return x * lax.rsqrt(jnp.sum(x * x, axis=-1, keepdims=True) * (1.0 / n) + EPS) * w


def _silu(x):
    return x * jax.nn.sigmoid(x)


def _softplus(x):
    return jnp.maximum(x, 0.0) + jnp.log1p(jnp.exp(-jnp.abs(x)))


def _mm(a, b, dims, precision=None):
    if precision is None:
        a, b = a.astype(MXU_DTYPE), b.astype(MXU_DTYPE)
    return lax.dot_general(a, b, (dims, ((), ())), precision=precision, preferred_element_type=f32)


_NN, _NT, _TN = ((1,), (0,)), ((1,), (1,)), ((0,), (0,))


@functools.partial(jax.custom_vjp, nondiff_argnums=(1, 2))
def _roll(x, shift, axis):
    return pltpu.roll(x, shift, axis)


def _roll_fwd(x, shift, axis):
    return pltpu.roll(x, shift, axis), None


def _roll_bwd(shift, axis, _, ct):
    return (pltpu.roll(ct, (ct.shape[axis] - shift) % ct.shape[axis], axis),)


_roll.defvjp(_roll_fwd, _roll_bwd)


def _rope(x, cos, sin, lo, half):
    n = x.shape[-1]
    lane = lax.broadcasted_iota(jnp.int32, x.shape, x.ndim - 1) % HPAD - lo
    first = ((lane >= 0) & (lane < half)) | ((lane >= 2 * half) & (lane < 3 * half))
    rot = jnp.where(first, -_roll(x, n - half, x.ndim - 1), _roll(x, half, x.ndim - 1))
    return x * cos + rot * sin


def _heads(x, n, width=HEAD):
    return [x[:, h * width:(h + 1) * width] for h in range(n)]


def _pad_heads(hs):
    z = jnp.zeros_like(hs[0])
    return jnp.concatenate([t for h in hs for t in (h, z)], axis=-1)


def _unpad_heads(x, n):
    return jnp.concatenate([x[:, h * HPAD:h * HPAD + HEAD] for h in range(n)], axis=-1)


HALO = 8


def _conv_specs(tm, tc, c0, L):
    nh = tm // HALO
    last = L // HALO - 1
    return [pl.BlockSpec((tm, tc), lambda i, j: (i, c0 + j)),
            pl.BlockSpec((HALO, tc), lambda i, j: (jnp.maximum(i * nh - 1, 0), c0 + j)),
            pl.BlockSpec((HALO, tc), lambda i, j: (jnp.minimum((i + 1) * nh, last), c0 + j))]


def _shift3(x):
    n = x.shape[0]
    return _roll(x, 1, 0), _roll(x, n - 1, 0)


PREFETCH_DEPTH = 3


def conv_fwd(gfn, x, c0, ncol, tc, w, b, tco, out_dtype, name, tm=512, prefetch=False):
    L = x.shape[0]
    tm = _tile_div(L, tm)
    ni = L // tm
    total = ni * ncol

    def fetch(x_hbm, buf, sems, step, slot):
        rows = pl.ds(pl.multiple_of((step // ncol) * tm, tm), tm)
        cols = pl.ds(pl.multiple_of((c0 + step % ncol) * tc, tc), tc)
        return pltpu.make_async_copy(x_hbm.at[rows, cols], buf.at[slot], sems.at[slot])

    def body(x_ref, p_ref, n_ref, w_ref, b_ref, o_ref, c_ref, *scratch):
        i = pl.program_id(0)
        if prefetch:
            buf, sems = scratch
            step = i * ncol + pl.program_id(1)
            slot = step % PREFETCH_DEPTH

            @pl.when(step == 0)
            def _():
                for k in range(min(PREFETCH_DEPTH, total)):
                    fetch(x_ref, buf, sems, k, k).start()

            fetch(x_ref, buf, sems, step, slot).wait()
            xv = buf[slot].astype(f32)
        else:
            xv = x_ref[...].astype(f32)
        xp = jnp.where(i == 0, 0.0, p_ref[HALO - 1:HALO, :].astype(f32))
        xn = jnp.where(i == ni - 1, 0.0, n_ref[0:1, :].astype(f32))
        rid = lax.broadcasted_iota(jnp.int32, xv.shape, 0)
        dn, up = _shift3(xv)
        dn = jnp.where(rid == 0, xp, dn)
        up = jnp.where(rid == tm - 1, xn, up)
        c = w_ref[0:1, :] * dn + w_ref[1:2, :] * xv + w_ref[2:3, :] * up + b_ref[...]
        c_ref[...] = c
        o_ref[...] = gfn(c).astype(o_ref.dtype)
        if prefetch:
            @pl.when(step + PREFETCH_DEPTH < total)
            def _():
                fetch(x_ref, buf, sems, step + PREFETCH_DEPTH, slot).start()

    specs = _conv_specs(tm, tc, c0, L)
    if prefetch:
        specs[0] = pl.BlockSpec(memory_space=pl.ANY)
    return pl.pallas_call(
        body, name=name, grid=(ni, ncol),
        in_specs=specs + [pl.BlockSpec((3, tc), lambda i, j: (0, j)), pl.BlockSpec((1, tc), lambda i, j: (0, j))],
        out_specs=[pl.BlockSpec((tm, tco), lambda i, j: (i, j)), pl.BlockSpec((tm, tc), lambda i, j: (i, j))],
        out_shape=[jax.ShapeDtypeStruct((L, ncol * tco), out_dtype), jax.ShapeDtypeStruct((L, ncol * tc), f32)],
        scratch_shapes=[pltpu.VMEM((PREFETCH_DEPTH, tm, tc), x.dtype), pltpu.SemaphoreType.DMA((PREFETCH_DEPTH,))] if prefetch else [],
        compiler_params=_params(("arbitrary", "arbitrary") if prefetch else ("parallel", "parallel")),
    )(x, x, x, w, b)


def conv_bwd(gfn, x, c, c0, ncol, tc, w, dys, tco, name, tm=512, dx_dtype=f32):
    L = x.shape[0]
    tm = _tile_div(L, tm)
    ni = L // tm
    nd = len(dys)

    def body(*refs):
        x_ref, xp_ref, xn_ref, c_ref, cp_ref, cn_ref = refs[:6]
        d_refs = refs[6:6 + 3 * nd]
        w_ref, dx_ref, dw_ref, db_ref = refs[6 + 3 * nd:]
        i = pl.program_id(1)
        first, lastb = i == 0, i == ni - 1

        def ext(m, p, n):
            return jnp.concatenate([jnp.where(first, 0.0, p[...].astype(f32)), m[...].astype(f32),
                                    jnp.where(lastb, 0.0, n[...].astype(f32))], axis=0)

        xe = ext(x_ref, xp_ref, xn_ref)
        de = ext(*d_refs[:3])
        for q in range(1, nd):
            de = de + ext(*d_refs[3 * q:3 * q + 3])
        x_dn, x_up = _shift3(xe)
        w0, w1, w2 = w_ref[0:1, :], w_ref[1:2, :], w_ref[2:3, :]
        _, vjp = jax.vjp(gfn, ext(c_ref, cp_ref, cn_ref))
        (dce,) = vjp(de)
        dc_dn, dc_up = _shift3(dce)
        dx_ref[...] = (w0 * dc_up + w1 * dce + w2 * dc_dn)[HALO:HALO + tm, :].astype(dx_ref.dtype)
        inner = slice(HALO, HALO + tm)
        dci = dce[inner]
        dw = jnp.concatenate([jnp.sum(dci * x_dn[inner], axis=0, keepdims=True), jnp.sum(dci * xe[inner], axis=0, keepdims=True),
                              jnp.sum(dci * x_up[inner], axis=0, keepdims=True)], axis=0)
        db = jnp.sum(dci, axis=0, keepdims=True)

        @pl.when(first)
        def _():
            dw_ref[...] = dw
            db_ref[...] = db

        @pl.when(i > 0)
        def _():
            dw_ref[...] += dw
            db_ref[...] += db

    res = pl.pallas_call(
        body, name=name, grid=(ncol, ni),
        in_specs=[pl.BlockSpec(s.block_shape, (lambda j, i, f=s.index_map: f(i, j))) for s in _conv_specs(tm, tc, c0, L)]
        + [pl.BlockSpec(s.block_shape, (lambda j, i, f=s.index_map: f(i, j))) for s in _conv_specs(tm, tc, 0, L)]
        + [pl.BlockSpec(s.block_shape, (lambda j, i, f=s.index_map: f(i, j))) for s in _conv_specs(tm, tco, 0, L)] * nd
        + [pl.BlockSpec((3, tc), lambda j, i: (0, j))],
        out_specs=[pl.BlockSpec((tm, tc), lambda j, i: (i, j)), pl.BlockSpec((3, tc), lambda j, i: (0, j)),
                   pl.BlockSpec((1, tc), lambda j, i: (0, j))],
        out_shape=[jax.ShapeDtypeStruct((L, ncol * tc), dx_dtype), jax.ShapeDtypeStruct((3, ncol * tc), f32),
                   jax.ShapeDtypeStruct((1, ncol * tc), f32)],
        compiler_params=_params(("parallel", "arbitrary")),
    )(x, x, x, c, c, c, *[d for d in dys for _ in range(3)], w)
    return res


LOG2E = math.log2(math.e)


def flash_fwd(q, k, v, n_q, n_kv, scale, name, side=(), tq=512, tk=8192):
    L = q.shape[0]
    tq, tk = _tile_div(L, tq), _tile_div(L, tk)
    rep = n_q // n_kv
    nkv = L // tk
    c2 = scale * LOG2E

    def body(q_ref, k_ref, v_ref, o_ref, lse_ref):
        qv = q_ref[...]
        ones_lane = lax.broadcasted_iota(jnp.int32, (tk, HPAD), 1) == HPAD - 1

        def step(c, carry):
            m, acc = carry
            off = pl.multiple_of(c * tk, tk)
            s = _mm(qv, k_ref[pl.ds(off, tk), :], _NT)
            m_new = jnp.maximum(m, jnp.max(s, axis=-1, keepdims=True))
            p = jnp.exp2((s - m_new) * c2)
            vv = v_ref[pl.ds(off, tk), :]
            acc = jnp.exp2((m - m_new) * c2) * acc + _mm(p, jnp.where(ones_lane, jnp.ones_like(vv), vv), _NN)
            return m_new, acc

        m, acc = lax.fori_loop(0, nkv, step, (jnp.full((tq, 1), -jnp.inf, f32), jnp.zeros((tq, HPAD), f32)))
        l = acc[:, HPAD - 1:]
        o_ref[...] = jnp.where(lax.broadcasted_iota(jnp.int32, (tq, HPAD), 1) == HPAD - 1, 0.0, acc / l)
        lse_ref[0] = m * scale + jnp.log(l)

    grid = (n_q, L // tq)
    body, ex = _with_exchange(body, 3, 2, side, grid)
    res = pl.pallas_call(
        body, name=name, grid=grid,
        in_specs=[pl.BlockSpec((tq, HPAD), lambda h, i: (i, h)), pl.BlockSpec((L, HPAD), lambda h, i: (0, h // rep)),
                  pl.BlockSpec((L, HPAD), lambda h, i: (0, h // rep))] + ex.specs,
        out_specs=[pl.BlockSpec((tq, HPAD), lambda h, i: (i, h)), pl.BlockSpec((1, tq, 1), lambda h, i: (h, i, 0))] + ex.specs,
        out_shape=[jax.ShapeDtypeStruct((L, n_q * HPAD), f32), jax.ShapeDtypeStruct((n_q, L, 1), f32)] + ex.out_shape,
        scratch_shapes=ex.scratch, compiler_params=_params(("arbitrary", "arbitrary") if side else ("parallel", "parallel")),
    )(q, k, v, *ex.args)
    return res[0], res[1], list(res[2:])


def flash_bwd(q, k, v, o, lse, do, n_q, n_kv, scale, name, side=(), tq=2048, tk=1024):
    L = q.shape[0]
    tq, tk = _tile_div(L, tq), _tile_div(L, tk)
    rep = n_q // n_kv
    c2 = scale * LOG2E

    def body(k_ref, v_ref, q_ref, do_ref, o_ref, lse_ref, dq_ref, dk_ref, dv_ref):
        j, i = pl.program_id(1), pl.program_id(2)
        kv, vv = k_ref[...], v_ref[...]
        rows = pl.ds(pl.multiple_of(i * tq, tq), tq)
        dk, dv = jnp.zeros((tk, HPAD), f32), jnp.zeros((tk, HPAD), f32)
        for r in range(rep):
            cols = slice(r * HPAD, (r + 1) * HPAD)
            qv = q_ref[:, cols]
            dov = do_ref[:, cols]
            delta = jnp.sum(dov * o_ref[:, cols], axis=-1, keepdims=True)
            p = jnp.exp2(_mm(qv, kv, _NT) * c2 - lse_ref[r] * LOG2E)
            dv = dv + _mm(p, dov, _TN)
            ds = p * (_mm(dov, vv, _NT) - delta) * scale
            dk = dk + _mm(ds, qv, _TN)
            dq = _mm(ds, kv, _NN)

            @pl.when(j == 0)
            def _(dq=dq, cols=cols):
                dq_ref[rows, cols] = dq

            @pl.when(j > 0)
            def _(dq=dq, cols=cols):
                dq_ref[rows, cols] += dq

        @pl.when(i == 0)
        def _():
            dk_ref[...] = dk
            dv_ref[...] = dv

        @pl.when(i > 0)
        def _():
            dk_ref[...] += dk
            dv_ref[...] += dv

    w = rep * HPAD
    grid = (n_kv, L // tk, L // tq)
    body, ex = _with_exchange(body, 6, 3, side, grid)
    res = pl.pallas_call(
        body, name=name, grid=grid,
        in_specs=[pl.BlockSpec((tk, HPAD), lambda g, j, i: (j, g)), pl.BlockSpec((tk, HPAD), lambda g, j, i: (j, g)),
                  pl.BlockSpec((tq, w), lambda g, j, i: (i, g)), pl.BlockSpec((tq, w), lambda g, j, i: (i, g)),
                  pl.BlockSpec((tq, w), lambda g, j, i: (i, g)), pl.BlockSpec((rep, tq, 1), lambda g, j, i: (g, i, 0))] + ex.specs,
        out_specs=[pl.BlockSpec((L, w), lambda g, j, i: (0, g)), pl.BlockSpec((tk, HPAD), lambda g, j, i: (j, g)),
                   pl.BlockSpec((tk, HPAD), lambda g, j, i: (j, g))] + ex.specs,
        out_shape=[jax.ShapeDtypeStruct((L, n_q * HPAD), f32), jax.ShapeDtypeStruct((L, n_kv * HPAD), f32),
                   jax.ShapeDtypeStruct((L, n_kv * HPAD), f32)] + ex.out_shape,
        scratch_shapes=ex.scratch, compiler_params=_params(("arbitrary",) * 3 if side else ("parallel", "arbitrary", "arbitrary")),
    )(k, v, q, do, o, lse, *ex.args)
    return res[0], res[1], res[2], list(res[3:])


N_SCAN_HEADS = 4
STATE_ROWS = N_SCAN_HEADS * HEAD


def _tri(n, rev):
    i = lax.broadcasted_iota(jnp.int32, (n, n), 0)
    k = lax.broadcasted_iota(jnp.int32, (n, n), 1)
    return ((k >= i), (k > i)) if rev else ((k <= i), (k < i))


def _decay(acol, incl):
    n = acol.shape[0]
    m1 = jnp.broadcast_to(acol, (n, n))
    return jnp.exp(jnp.where(incl, m1 - m1.T, -jnp.inf))


def _ssd_chunk(S, xbc, win, a_log, dt_bias, rev):
    Q = xbc.shape[0]
    d = 1 if rev else 0
    incl, _ = _tri(Q, rev)
    dt = _softplus(win[:, 4 * d:4 * d + 4] + dt_bias[d:d + 1, :])
    a = dt * (-jnp.exp(a_log[d:d + 1, :]))
    acum = _mm(incl.astype(f32), a, _NN, HI)
    tot = jnp.sum(a, axis=0, keepdims=True)
    xs, Bm, Cm = xbc[:, :256], xbc[:, 256:384], xbc[:, 384:512]
    H, W = N_SCAN_HEADS, N_SCAN_HEADS * HEAD
    lanes = lambda x: _cat([jnp.broadcast_to(x[:, h:h + 1], (x.shape[0], HEAD)) for h in range(H)])
    xdt = xs * lanes(dt)
    cb = [_mm(Cm[:, g * HEAD:(g + 1) * HEAD], Bm[:, g * HEAD:(g + 1) * HEAD], _NT) for g in range(2)]
    scores = _cat([cb[h // 2] * _decay(acum[:, h:h + 1], incl) for h in range(H)])
    own = (lax.broadcasted_iota(jnp.int32, (H * Q, W), 0) // Q) == (lax.broadcasted_iota(jnp.int32, (H * Q, W), 1) // HEAD)
    y = _mm(scores, jnp.where(own, jnp.concatenate([xdt] * H, axis=0), 0.0), _NN)
    grp = (lax.broadcasted_iota(jnp.int32, (W, 2 * HEAD), 0) // (2 * HEAD)) == (lax.broadcasted_iota(jnp.int32, (W, 2 * HEAD), 1) // HEAD)
    y = y + _mm(Cm, jnp.where(grp, _cat([S, S]), 0.0), _NT) * jnp.exp(lanes(acum))
    st = _mm(xdt * jnp.exp(lanes(tot - acum)), Bm, _TN)
    first = lax.broadcasted_iota(jnp.int32, (W, HEAD), 0) < 2 * HEAD
    exp_tot = jnp.concatenate([jnp.broadcast_to(jnp.exp(tot[:, h:h + 1]), (HEAD, 1)) for h in range(H)], axis=0)
    return S * exp_tot + jnp.where(first, st[:, :HEAD], st[:, HEAD:]), y


@functools.partial(jax.custom_vjp, nondiff_argnums=(1,))
def _inv_unit_tri(Lm, order):
    n = Lm.shape[0]
    eye = (lax.broadcasted_iota(jnp.int32, (n, n), 0) == lax.broadcasted_iota(jnp.int32, (n, n), 1)).astype(f32)
    P = -Lm
    T = eye + P
    k = 1
    while 2 * k < order:
        P = _mm(P, P, _NN, lax.Precision.HIGH)
        T = T + _mm(T, P, _NN, lax.Precision.HIGH)
        k *= 2
    return T


def _inv_unit_tri_fwd(Lm, order):
    T = _inv_unit_tri(Lm, order)
    return T, T


def _inv_unit_tri_bwd(order, T, dT):
    return (-_mm(_mm(T, dT, _TN, lax.Precision.HIGH), T, _NT, lax.Precision.HIGH),)


_inv_unit_tri.defvjp(_inv_unit_tri_fwd, _inv_unit_tri_bwd)


@jax.custom_vjp
def _inv_known(Lm, T):
    return T


def _inv_known_fwd(Lm, T):
    return T, T


def _inv_known_bwd(T, dT):
    return _inv_unit_tri_bwd(None, T, dT) + (jnp.zeros_like(T),)


_inv_known.defvjp(_inv_known_fwd, _inv_known_bwd)


def _dn_chunk(S, qkv, win, a_log, dt_bias, rev, kept=None):
    Q = qkv.shape[0]
    H, R = N_SCAN_HEADS, N_SCAN_HEADS * qkv.shape[0]
    d = 1 if rev else 0
    beta = jax.nn.sigmoid(win[:, 4 * d:4 * d + 4])
    gl = -jnp.exp(a_log[d:d + 1, :]) * _softplus(win[:, 8 + 4 * d:12 + 4 * d] + dt_bias[d:d + 1, :])
    G = _mm(_tri(Q, rev)[0].astype(f32), gl, _NN, HI)
    tot = jnp.sum(gl, axis=0, keepdims=True)
    rows = lambda x: jnp.concatenate(_heads(x, H), axis=0)
    col = lambda x: jnp.concatenate([x[:, h:h + 1] for h in range(H)], axis=0)
    per_head = lambda x, n: jnp.concatenate([jnp.broadcast_to(x[:, h:h + 1], (n, 1)) for h in range(H)], axis=0)
    r = lax.broadcasted_iota(jnp.int32, (R, R), 0)
    c = lax.broadcasted_iota(jnp.int32, (R, R), 1)
    same = (r // Q) == (c // Q)
    incl = same & ((c >= r) if rev else (c <= r))
    strict = same & ((c > r) if rev else (c < r))
    own = (lax.broadcasted_iota(jnp.int32, (R, H * HEAD), 0) // Q) == (lax.broadcasted_iota(jnp.int32, (R, H * HEAD), 1) // HEAD)
    blk = lambda x: jnp.where(own, _cat([x] * H), 0.0)
    q, k, v = rows(qkv[:, :256]) * (HEAD ** -0.5), rows(qkv[:, 256:512]), rows(qkv[:, 512:])
    b, Gs, tots = col(beta), col(G), per_head(tot, Q)
    dec = _decay(Gs, incl)
    kb = k * b
    Lm = jnp.where(strict, _mm(kb, k, _NT) * dec, 0.0)
    T = _inv_unit_tri(Lm, Q) if kept is None else _inv_known(Lm, blk(kept))
    eG = jnp.exp(Gs)
    uw = _mm(T, _cat([v * b, kb * eG]), _NN)
    vnew = uw[:, :HEAD] - _mm(blk(uw[:, HEAD:]), S, _NN)
    o = _mm(blk(q * eG), S, _NN) + _mm(_mm(q, k, _NT) * dec, vnew, _NN)
    S_next = S * jnp.exp(per_head(tot, HEAD)) + _mm(blk(k * jnp.exp(tots - Gs)), vnew, _TN)
    out = S_next, _cat([o[h * Q:(h + 1) * Q, :] for h in range(H)])
    if kept is None:
        out += (jnp.concatenate([T[h * Q:(h + 1) * Q, h * Q:(h + 1) * Q] for h in range(H)], axis=0),)
    return out


def scan_fwd(chunk_fn, Q, x, p, win_blk, a_log, dt_bias, name, keeps=False):
    L, W = x.shape
    nc = L // Q
    fidx, ridx = (lambda t: t), (lambda t: nc - 1 - t)

    def body(xf_ref, xr_ref, wf_ref, wr_ref, al_ref, db_ref, yf_ref, yr_ref, sf_ref, sr_ref, *rest):
        (sf_scr, sr_scr), keep_refs = rest[-2:], rest[:-2]

        @pl.when(pl.program_id(0) == 0)
        def _():
            sf_scr[...] = jnp.zeros_like(sf_scr)
            sr_scr[...] = jnp.zeros_like(sr_scr)

        for d, (x_ref, w_ref, y_ref, sin_ref, s_scr) in enumerate(((xf_ref, wf_ref, yf_ref, sf_ref, sf_scr),
                                                                   (xr_ref, wr_ref, yr_ref, sr_ref, sr_scr))):
            S = s_scr[...]
            sin_ref[...] = S
            res = chunk_fn(S, x_ref[...], w_ref[...], al_ref[...], db_ref[...], d == 1)
            s_scr[...], y_ref[...] = res[:2]
            if keeps:
                keep_refs[d][...] = res[2]

    row = lambda w, idx, b=0: pl.BlockSpec((Q, w), lambda t: (idx(t), b))
    small = pl.BlockSpec(a_log.shape, lambda t: (0, 0))
    state = lambda idx: pl.BlockSpec((STATE_ROWS, HEAD), lambda t: (idx(t), 0))
    n_state = 4 if keeps else 2
    return pl.pallas_call(
        body, name=name, grid=(nc,),
        in_specs=[row(W, fidx), row(W, ridx), row(HPAD, fidx, win_blk), row(HPAD, ridx, win_blk), small, small],
        out_specs=[row(STATE_ROWS, fidx), row(STATE_ROWS, ridx)] + [state(fidx), state(ridx)] * (n_state // 2),
        out_shape=[jax.ShapeDtypeStruct((L, STATE_ROWS), f32)] * 2 + [jax.ShapeDtypeStruct((nc * STATE_ROWS, HEAD), f32)] * n_state,
        scratch_shapes=[pltpu.VMEM((STATE_ROWS, HEAD), f32)] * 2, compiler_params=_params(("arbitrary",)),
    )(x, x, p, p, a_log, dt_bias)


def scan_bwd(chunk_fn, Q, x, p, win_blk, a_log, dt_bias, s_fwd, s_rev, dy, name, kept=()):
    L, W = x.shape
    nc = L // Q
    fidx, ridx = (lambda t: nc - 1 - t), (lambda t: t)
    nk = len(kept)

    def body(xf_ref, xr_ref, wf_ref, wr_ref, al_ref, db_ref, sf_ref, sr_ref, dyf_ref, dyr_ref, *rest):
        keep_refs = rest[:nk]
        dxf_ref, dxr_ref, dwf_ref, dwr_ref, dal_ref, ddb_ref, dsf_scr, dsr_scr = rest[nk:]

        @pl.when(pl.program_id(0) == 0)
        def _():
            dsf_scr[...] = jnp.zeros_like(dsf_scr)
            dsr_scr[...] = jnp.zeros_like(dsr_scr)
            dal_ref[...] = jnp.zeros_like(dal_ref)
            ddb_ref[...] = jnp.zeros_like(ddb_ref)

        dal_sum, ddb_sum = dal_ref[...], ddb_ref[...]
        for rev, x_ref, w_ref, sin_ref, dy_ref, dx_ref, dw_ref, ds_scr in (
                (False, xf_ref, wf_ref, sf_ref, dyf_ref, dxf_ref, dwf_ref, dsf_scr),
                (True, xr_ref, wr_ref, sr_ref, dyr_ref, dxr_ref, dwr_ref, dsr_scr)):
            extra = (keep_refs[int(rev)][...],) if nk else ()
            _, vjp = jax.vjp(lambda S, xv, wv, al, db, rev=rev, extra=extra: chunk_fn(S, xv, wv, al, db, rev, *extra),
                             sin_ref[...], x_ref[...], w_ref[...], al_ref[...], db_ref[...])
            ds_scr[...], dx_ref[...], dw_ref[...], dal, ddb = vjp((ds_scr[...], dy_ref[...]))
            dal_sum, ddb_sum = dal_sum + dal, ddb_sum + ddb
        dal_ref[...] = dal_sum
        ddb_ref[...] = ddb_sum

    row = lambda w, idx, b=0: pl.BlockSpec((Q, w), lambda t: (idx(t), b))
    small = pl.BlockSpec(a_log.shape, lambda t: (0, 0))
    state = lambda idx: pl.BlockSpec((STATE_ROWS, HEAD), lambda t: (idx(t), 0))
    return pl.pallas_call(
        body, name=name, grid=(nc,),
        in_specs=[row(W, fidx), row(W, ridx), row(HPAD, fidx, win_blk), row(HPAD, ridx, win_blk), small, small,
                  state(fidx), state(ridx), row(STATE_ROWS, fidx), row(STATE_ROWS, ridx)] + [state(fidx), state(ridx)][:nk],
        out_specs=[row(W, fidx), row(W, ridx), row(HPAD, fidx), row(HPAD, ridx), small, small],
        out_shape=[jax.ShapeDtypeStruct((L, W), f32)] * 2 + [jax.ShapeDtypeStruct((L, HPAD), f32)] * 2
        + [jax.ShapeDtypeStruct(a_log.shape, f32)] * 2,
        scratch_shapes=[pltpu.VMEM((STATE_ROWS, HEAD), f32)] * 2, compiler_params=_params(("arbitrary",)),
    )(x, x, p, p, a_log, dt_bias, s_fwd, s_rev, dy, dy, *kept)


def _cat(xs):
    return jnp.concatenate(xs, axis=-1)


def _a_prep(pa, cos, sin, qn, wuq, kvn, wk, wv):
    cq, ckv, kr = pa[:, :256], pa[:, 256:384], pa[:, 384:512]
    q = _rope(_mm(_rms(cq, qn, A_Q_LORA), wuq, _NN), _cat([cos] * 4), _cat([sin] * 4), A_NOPE, A_ROPE // 4)
    kvh = _rms(ckv, kvn)
    k = _mm(kvh, wk, _NN) + _cat([_rope(kr, cos, sin, A_NOPE, A_ROPE // 4)] * 4)
    return q, k, _mm(kvh, wv, _NN)


def _b_prep(pb, cos, sin, qn, kn):
    cos4, sin4 = _cat([cos] * 4), _cat([sin] * 4)
    q = _pad_heads([_rms(h, qn) for h in _heads(pb[:, :256], 4)])
    k = _pad_heads([_rms(h, kn) for h in _heads(pb[:, 256:384], 2) for _ in range(2)])
    v = _pad_heads([h for h in _heads(pb[:, 384:512], 2) for _ in range(2)])
    return _rope(q, cos4, sin4, 0, HEAD // 4), _rope(k, cos4, sin4, 0, HEAD // 4), v


def _c_act(c):
    return _silu(c)


def _d_act(c):
    s = _silu(c)
    return _cat([h * lax.rsqrt(jnp.sum(h * h, axis=-1, keepdims=True) + 1e-6) for h in _heads(s[:, :512], 8)] + [s[:, 512:]])


def _ffn_act(c):
    return _silu(c[:, :FF_BLK]) * c[:, FF_BLK:]


def _mix_post(oa, ob, yc_f, yc_r, xbc, cz, od_f, od_r, dz, a_out, b_out, dskip, c_out, d_out):
    o_a = _rms(_unpad_heads(oa, 4), a_out)
    o_b = _rms(_unpad_heads(ob, 4), b_out)
    skip = _cat([jnp.broadcast_to(dskip[:, h:h + 1], (1, HEAD)) for h in range(4)])
    o_c = _rms((yc_f + yc_r + xbc[:, :256] * skip) * _silu(cz), c_out)
    o_d = _cat([_rms(h, d_out) for h in _heads(od_f + od_r, 4)]) * _silu(dz)
    return _cat([o_a, o_b, o_c, o_d])


def _post_mix(x, y1, g_pm, g_pf):
    x1 = x + _rms(y1, g_pm)
    return x1, _rms(x1, g_pf)


def _post_ffn(x1, y2, g):
    return x1 + _rms(y2, g)


def _pre_bwd(x, g):
    return x, _rms(x, g)


def _layer_fwd(x, W, tabs, n, side=(), late_side=(), late_weights=None):
    mx = MXU_DTYPE
    cos_a, sin_a, cos_b, sin_b = tabs
    full = lambda a: (a, a.shape[1], 0)
    s = {"x": x}
    (s["h"],) = rowwise(_rms, [full(x)], [W["g_pre"]], [(D_MODEL, mx)], n + "pre")
    p = s["p"] = matmul(s["h"], W["w_in"], "nn", f32, n + "in")
    s["qa"], s["ka"], s["va"] = rowwise(_a_prep, [(p, 512, SEG_A // 512), full(cos_a), full(sin_a)],
                                        [W["a_qn"], W["a_wuq"], W["a_kvn"], W["a_wk"], W["a_wv"]], [(512, mx)] * 3, n + "a_prep")
    s["qb"], s["kb"], s["vb"] = rowwise(_b_prep, [(p, 512, SEG_B // 512), full(cos_b), full(sin_b)], [W["b_qn"], W["b_kn"]],
                                        [(512, mx)] * 3, n + "b_prep")
    s["oa"], s["lsea"], side_out = flash_fwd(s["qa"], s["ka"], s["va"], 4, 4, (A_NOPE + A_ROPE) ** -0.5, n + "a_attn", side)
    s["ob"], s["lseb"], late_out = flash_fwd(s["qb"], s["kb"], s["vb"], 4, 4, HEAD ** -0.5, n + "b_attn", late_side)
    if late_weights is not None:
        W = {**W, **late_weights(late_out)}
    xbc, s["c_c"] = conv_fwd(_c_act, p, SEG_CX // 512, 1, 512, W["c_cw"], W["c_cb"], 512, f32, n + "c_conv")
    s["xbc"] = xbc
    s["yc_f"], s["yc_r"], s["sc_f"], s["sc_r"] = scan_fwd(_ssd_chunk, min(SSD_CHUNK, xbc.shape[0]), xbc, p, SEG_CDT // HPAD, W["c_alog"], W["c_dtb"], n + "c_ssd")
    qkv, s["d_c"] = conv_fwd(_d_act, p, SEG_DX // 768, 1, 768, W["d_cw"], W["d_cb"], 768, f32, n + "d_conv")
    s["qkv"] = qkv
    s["od_f"], s["od_r"], s["sd_f"], s["sd_r"], s["td_f"], s["td_r"] = scan_fwd(
        _dn_chunk, DN_CHUNK, qkv, p, SEG_DAB // HPAD, W["d_alog"], W["d_dtb"], n + "d_dn", keeps=True)
    (s["omix"],) = rowwise(_mix_post, _mix_rows(s), _mix_params(W), [(D_MODEL, mx)], n + "mix_post")
    s["y1"] = matmul(s["omix"], W["w_out"], "nn", f32, n + "out")
    s["x1"], s["h2"] = rowwise(_post_mix, [full(x), full(s["y1"])], [W["g_pm"], W["g_pf"]], [(D_MODEL, f32), (D_MODEL, mx)], n + "post_mix")
    s["u"] = matmul(s["h2"], W["f_win"], "nn", f32, n + "f_in")
    s["a"], s["f_c"] = conv_fwd(_ffn_act, s["u"], 0, D_FF // FF_BLK, 2 * FF_BLK, W["f_cw"], W["f_cb"], FF_BLK, mx, n + "f_conv", tm=1024,
                                   prefetch=True)
    s["y2"] = matmul(s["a"], W["f_wout"], "nn", f32, n + "f_out")
    (x2,) = rowwise(_post_ffn, [full(s["x1"]), full(s["y2"])], [W["g_po"]], [(D_MODEL, f32)], n + "post_ffn")
    return x2, s, side_out, W


def _mix_rows(s):
    p = s["p"]
    return [(s["oa"], 512, 0), (s["ob"], 512, 0), (s["yc_f"], 256, 0), (s["yc_r"], 256, 0), (s["xbc"], 512, 0),
            (p, 256, SEG_CZ // 256), (s["od_f"], 256, 0), (s["od_r"], 256, 0), (p, 256, SEG_DZ // 256)]


def _mix_params(W):
    return [W["a_out"], W["b_out"], W["c_dskip"], W["c_out"], W["d_out"]]


def _layer_bwd(s, dx2, W, tabs, n, side=(), late_side=None):
    cos_a, sin_a, cos_b, sin_b = tabs
    full = lambda a: (a, a.shape[1], 0)
    p, g = s["p"], {}
    (dx1, dy2), (g["g_po"],) = rowwise_vjp(_post_ffn, [full(s["x1"]), full(s["y2"])], [W["g_po"]], [full(dx2)], n + "post_ffn_b", [0, 1], [0],
                                           row_dtypes=[f32, MXU_DTYPE])
    da = matmul(dy2, W["f_wout"], "nt", f32, n + "f_out_dx")
    g["f_wout"] = matmul(s["a"], dy2, "tn", f32, n + "f_out_dw")
    du, g["f_cw"], g["f_cb"] = conv_bwd(_ffn_act, s["u"], s["f_c"], 0, D_FF // FF_BLK, 2 * FF_BLK, W["f_cw"], [da], FF_BLK, n + "f_conv_b", tm=1024,
                                        dx_dtype=MXU_DTYPE)
    dh2 = matmul(du, W["f_win"], "nt", f32, n + "f_in_dx")
    g["f_win"] = matmul(s["h2"], du, "tn", f32, n + "f_in_dw")
    (dx, dy1), (g["g_pm"], g["g_pf"]) = rowwise_vjp(_post_mix, [full(s["x"]), full(s["y1"])], [W["g_pm"], W["g_pf"]],
                                                   [full(dx1), full(dh2)], n + "post_mix_b", [0, 1], [0, 1],
                                                   row_dtypes=[f32, MXU_DTYPE])
    domix = matmul(dy1, W["w_out"], "nt", f32, n + "out_dx")
    g["w_out"] = matmul(s["omix"], dy1, "tn", f32, n + "out_dw")
    (doa, dob, dyc, dxbc_skip, dcz, dod, ddz), (g["a_out"], g["b_out"], g["c_dskip"], g["c_out"], g["d_out"]) = rowwise_vjp(
        _mix_post, _mix_rows(s), _mix_params(W), [full(domix)], n + "mix_post_b", [0, 1, 2, 4, 5, 6, 8], list(range(5)))
    dqkv_f, dqkv_r, ddab_f, ddab_r, g["d_alog"], g["d_dtb"] = scan_bwd(
        _dn_chunk, DN_CHUNK, s["qkv"], p, SEG_DAB // HPAD, W["d_alog"], W["d_dtb"], s["sd_f"], s["sd_r"], dod, n + "d_dn_b",
        kept=(s["td_f"], s["td_r"]))
    ddx, g["d_cw"], _ = conv_bwd(_d_act, p, s["d_c"], SEG_DX // 768, 1, 768, W["d_cw"], [dqkv_f, dqkv_r], 768, n + "d_conv_b")
    dxbc_f, dxbc_r, dcdt_f, dcdt_r, g["c_alog"], g["c_dtb"] = scan_bwd(
        _ssd_chunk, min(SSD_CHUNK, p.shape[0]), s["xbc"], p, SEG_CDT // HPAD, W["c_alog"], W["c_dtb"], s["sc_f"], s["sc_r"], dyc, n + "c_ssd_b")
    dcx, g["c_cw"], g["c_cb"] = conv_bwd(_c_act, p, s["c_c"], SEG_CX // 512, 1, 512, W["c_cw"], [dxbc_f, dxbc_r, dxbc_skip], 512,
                                         n + "c_conv_b")
    (dsmall,) = rowwise(lambda a, b, c, d: _cat([a + b, c + d]), [full(dcdt_f), full(dcdt_r), full(ddab_f), full(ddab_r)], [],
                        [(2 * HPAD, f32)], n + "dwin_sum")
    dqa, dka, dva, side_out = flash_bwd(s["qa"], s["ka"], s["va"], s["oa"], s["lsea"], doa, 4, 4, (A_NOPE + A_ROPE) ** -0.5,
                                        n + "a_attn_b", side)
    dqb, dkb, dvb, late_out = flash_bwd(s["qb"], s["kb"], s["vb"], s["ob"], s["lseb"], dob, 4, 4, HEAD ** -0.5, n + "b_attn_b",
                                        late_side(g) if late_side is not None else ())
    (dpa,), (g["a_qn"], g["a_wuq"], g["a_kvn"], g["a_wk"], g["a_wv"]) = rowwise_vjp(
        _a_prep, [(p, 512, SEG_A // 512), full(cos_a), full(sin_a)], [W["a_qn"], W["a_wuq"], W["a_kvn"], W["a_wk"], W["a_wv"]],
        [full(dqa), full(dka), full(dva)], n + "a_prep_b", [0], list(range(5)))
    (dpb,), (g["b_qn"], g["b_kn"]) = rowwise_vjp(_b_prep, [(p, 512, SEG_B // 512), full(cos_b), full(sin_b)], [W["b_qn"], W["b_kn"]],
                                               [full(dqb), full(dkb), full(dvb)], n + "b_prep_b", [0], [0, 1])
    dp = jnp.concatenate([dpa, dpb, dcx, ddx, dcz, ddz, dsmall], axis=1).astype(MXU_DTYPE)
    dh = matmul(dp, W["w_in"], "nt", f32, n + "in_dx")
    g["w_in"] = matmul(s["h"], dp, "tn", f32, n + "in_dw")
    (dx0,), (g["g_pre"],) = rowwise_vjp(_pre_bwd, [full(s["x"])], [W["g_pre"]], [full(dx), full(dh)], n + "pre_b", [0], [0])
    return dx0, g, side_out, late_out


def loss_and_grad(y, target, name):
    L, D = y.shape
    tm = _tile(L, 512)

    def body(y_ref, t_ref, loss_ref, dy_ref):
        e = y_ref[...] - t_ref[...]
        dy_ref[...] = e * (1.0 / D)
        part = 0.5 * jnp.sum(jnp.sum(e * e, axis=1, keepdims=True) * (1.0 / D), axis=0, keepdims=True)

        @pl.when(pl.program_id(0) == 0)
        def _():
            loss_ref[...] = part

        @pl.when(pl.program_id(0) > 0)
        def _():
            loss_ref[...] += part

    row = pl.BlockSpec((tm, D), lambda i: (i, 0))
    return pl.pallas_call(
        body, name=name, grid=(L // tm,), in_specs=[row, row], out_specs=[pl.BlockSpec((1, 1), lambda i: (0, 0)), row],
        out_shape=[jax.ShapeDtypeStruct((1, 1), f32), jax.ShapeDtypeStruct((L, D), f32)], compiler_params=_params(("arbitrary",)),
    )(y, target)


_IN_SEGS = [(0, 192), (256, 128), (384 + A_NOPE, 32), (512, 256), (768, 128), (896, 128), (SEG_CZ, 256), (SEG_CX, 512),
            (SEG_CDT, 8), (SEG_DX, 768), (SEG_DZ, 256), (SEG_DAB, 16)]


def _pad_in(w):
    src, pieces = 0, {}
    for off, wd in _IN_SEGS:
        pieces[off] = w[..., src:src + wd]
        src += wd
    out, pos = [], 0
    for off in sorted(pieces):
        if off > pos:
            out.append(jnp.zeros(w.shape[:-1] + (off - pos,), w.dtype))
        out.append(pieces[off])
        pos = off + pieces[off].shape[-1]
    out.append(jnp.zeros(w.shape[:-1] + (PIN - pos,), w.dtype))
    return jnp.concatenate(out, axis=-1)


def _unpad_in(wp):
    return jnp.concatenate([wp[..., off:off + wd] for off, wd in _IN_SEGS], axis=-1)


def _pad_last(a, n):
    return jnp.pad(a, [(0, 0)] * (a.ndim - 1) + [(0, n - a.shape[-1])])


def _ff_interleave(w):
    nb = D_FF // FF_BLK
    if w.size <= 4 * D_FF * 2:
        lead = w.shape[:-1]
        return jnp.stack([w[..., :D_FF].reshape(lead + (nb, FF_BLK)), w[..., D_FF:].reshape(lead + (nb, FF_BLK))],
                         axis=-2).reshape(lead + (2 * D_FF,))
    return jnp.concatenate([w[..., off + j * FF_BLK:off + (j + 1) * FF_BLK] for j in range(nb) for off in (0, D_FF)], axis=-1)


def _ff_deinterleave(w):
    nb = D_FF // FF_BLK
    if w.size <= 4 * D_FF * 2:
        lead = w.shape[:-1]
        t = w.reshape(lead + (nb, 2, FF_BLK))
        return jnp.concatenate([t[..., 0, :].reshape(lead + (D_FF,)), t[..., 1, :].reshape(lead + (D_FF,))], axis=-1)
    return jnp.concatenate([w[..., (2 * j + half) * FF_BLK:(2 * j + half + 1) * FF_BLK] for half in (0, 1) for j in range(nb)], axis=-1)


def _row(v):
    return v.reshape(1, -1)


_LATE = ["w_out", "f_w_in", "f_w_out"]


def _late_kernel_weights(P, l, wdt):
    return {"w_out": P["w_out"][l].astype(wdt), "f_win": _ff_interleave(P["f_w_in"][l]).astype(wdt), "f_wout": P["f_w_out"][l].astype(wdt)}


def _kernel_weights(P, l, wdt):
    late = {} if P["w_out"][l] is None else _late_kernel_weights(P, l, wdt)
    uq = P["a_w_uq"][l].reshape(A_Q_LORA, 4, A_NOPE + A_ROPE)
    ukv = P["a_w_ukv"][l].reshape(A_KV_LORA, 4, 2 * HEAD)
    z = jnp.zeros((A_KV_LORA, 4, HEAD), ukv.dtype)
    return {
        "g_pre": _row(P["pre_mix_norm"][l]), "w_in": _pad_in(P["w_in"][l]).astype(wdt),
        "a_qn": _pad_last(_row(P["a_q_norm"][l]), 256),
        "a_wuq": jnp.pad(_pad_last(uq, HPAD).reshape(A_Q_LORA, 4 * HPAD), ((0, 256 - A_Q_LORA), (0, 0))).astype(wdt),
        "a_kvn": _row(P["a_kv_norm"][l]),
        "a_wk": jnp.concatenate([ukv[..., :HEAD], z], axis=-1).reshape(A_KV_LORA, 4 * HPAD).astype(wdt),
        "a_wv": jnp.concatenate([ukv[..., HEAD:], z], axis=-1).reshape(A_KV_LORA, 4 * HPAD).astype(wdt),
        "a_out": _row(P["a_out_norm"][l]), "b_qn": _row(P["b_q_norm"][l]), "b_kn": _row(P["b_k_norm"][l]), "b_out": _row(P["b_out_norm"][l]),
        "c_cw": P["c_conv_w"][l], "c_cb": _row(P["c_conv_b"][l]), "c_alog": P["c_a_log"][l], "c_dtb": P["c_dt_bias"][l],
        "c_dskip": _row(P["c_d_skip"][l]), "c_out": _row(P["c_out_norm"][l]),
        "d_cw": P["d_conv_w"][l], "d_cb": jnp.zeros((1, 768), f32), "d_alog": P["d_a_log"][l], "d_dtb": P["d_dt_bias"][l],
        "d_out": _row(P["d_out_norm"][l]), "g_pm": _row(P["post_mix_norm"][l]), "g_pf": _row(P["pre_ffn_norm"][l]),
        "f_cw": _ff_interleave(P["f_conv_w"][l]), "f_cb": _row(_ff_interleave(P["f_conv_b"][l])),
        "g_po": _row(P["post_ffn_norm"][l]), **late,
    }


def _reference_grads(g):
    uq = g["a_wuq"][:A_Q_LORA].reshape(A_Q_LORA, 4, HPAD)[..., :A_NOPE + A_ROPE].reshape(A_Q_LORA, 4 * (A_NOPE + A_ROPE))
    wk = g["a_wk"].reshape(A_KV_LORA, 4, HPAD)[..., :HEAD]
    wv = g["a_wv"].reshape(A_KV_LORA, 4, HPAD)[..., :HEAD]
    return {
        "pre_mix_norm": g["g_pre"][0], "w_in": _unpad_in(g["w_in"]), "a_q_norm": g["a_qn"][0, :A_Q_LORA], "a_w_uq": uq,
        "a_kv_norm": g["a_kvn"][0], "a_w_ukv": jnp.concatenate([wk, wv], axis=-1).reshape(A_KV_LORA, 8 * HEAD),
        "a_out_norm": g["a_out"][0], "b_q_norm": g["b_qn"][0], "b_k_norm": g["b_kn"][0], "b_out_norm": g["b_out"][0],
        "c_conv_w": g["c_cw"], "c_conv_b": g["c_cb"][0], "c_a_log": g["c_alog"], "c_dt_bias": g["c_dtb"], "c_d_skip": g["c_dskip"][0],
        "c_out_norm": g["c_out"][0], "d_conv_w": g["d_cw"], "d_a_log": g["d_alog"], "d_dt_bias": g["d_dtb"], "d_out_norm": g["d_out"][0],
        "w_out": g["w_out"], "post_mix_norm": g["g_pm"][0], "pre_ffn_norm": g["g_pf"][0], "f_w_in": _ff_deinterleave(g["f_win"]),
        "f_conv_w": _ff_deinterleave(g["f_cw"]), "f_conv_b": _ff_deinterleave(g["f_cb"][0]), "f_w_out": g["f_wout"],
        "post_ffn_norm": g["g_po"][0],
    }


def _rope_tables(L):
    def tables(rot):
        rows = L // GRID_W
        row = jnp.repeat(jnp.arange(rows), GRID_W).astype(f32)
        col = jnp.tile(jnp.arange(GRID_W), rows).astype(f32)
        sec = rot // 2
        inv = ROPE_BASE ** (-jnp.arange(0, sec, 2, dtype=f32) / sec)
        ang = jnp.concatenate([row[:, None] * inv] * 2 + [col[:, None] * inv] * 2, axis=-1)
        return jnp.cos(ang), jnp.sin(ang)

    ca, sa = tables(A_ROPE)
    cb, sb = tables(HEAD)
    one, zero = jnp.ones, jnp.zeros
    return (jnp.concatenate([one((L, A_NOPE), f32), ca, one((L, HPAD - A_NOPE - A_ROPE), f32)], axis=1),
            jnp.concatenate([zero((L, A_NOPE), f32), sa, zero((L, HPAD - A_NOPE - A_ROPE), f32)], axis=1),
            jnp.concatenate([cb, one((L, HPAD - HEAD), f32)], axis=1), jnp.concatenate([sb, zero((L, HPAD - HEAD), f32)], axis=1))


def exchange(items, name):
    n = len(items)

    def body(*refs):
        start, wait = _exchange_ops([g for _, g in items], refs[:n], refs[n:2 * n], *refs[2 * n:])
        start()
        wait()

    ex = _exchange_io(items)
    res = pl.pallas_call(body, name=name, in_specs=ex.specs, out_specs=ex.specs, out_shape=ex.out_shape,
                         scratch_shapes=ex.scratch)(*ex.args)
    return list(res)


def _exchange_ops(modes, src_refs, out_refs, send_sems, recv_sems, local_sems):
    n = len(modes)
    x, y, c = lax.axis_index("x"), lax.axis_index("y"), lax.axis_index("c")
    me = 4 * x + 2 * y + c

    def copy(a, k, arriving):
        px, py, pc = (1 - x if k & 4 else x), (1 - y if k & 2 else y), (1 - c if k & 1 else c)
        pid = 4 * px + 2 * py + pc
        sem = a * (N_DEV - 1) + k - 1
        return pltpu.make_async_remote_copy(
            src_ref=src_refs[a] if modes[a] else src_refs[a].at[pid], dst_ref=out_refs[a].at[pid if arriving else me],
            send_sem=send_sems.at[sem], recv_sem=recv_sems.at[sem], device_id=(px, py, pc), device_id_type=pl.DeviceIdType.MESH)

    def local(a):
        return pltpu.make_async_copy(src_refs[a] if modes[a] else src_refs[a].at[me], out_refs[a].at[me], local_sems.at[a])

    pairs = [(a, k) for k in range(1, N_DEV) for a in range(n)]

    def start():
        for a, k in pairs:
            copy(a, k, False).start()
        for a in range(n):
            local(a).start()

    def wait():
        for a, k in pairs:
            copy(a, k, False).wait_send()
        for a, k in pairs:
            copy(a, k, True).wait_recv()
        for a in range(n):
            local(a).wait()

    return start, wait


class _ExchangeIO:
    def __init__(self, items):
        n = len(items)
        self.args = [s for s, _ in items]
        self.specs = [pl.BlockSpec(memory_space=pl.ANY)] * n
        self.out_shape = [jax.ShapeDtypeStruct((N_DEV,) + tuple(s.shape if g else s.shape[1:]), s.dtype) for s, g in items]
        self.scratch = [pltpu.SemaphoreType.DMA((n * (N_DEV - 1),)), pltpu.SemaphoreType.DMA((n * (N_DEV - 1),)),
                        pltpu.SemaphoreType.DMA((n,))] if n else []


def _exchange_io(items):
    return _ExchangeIO(list(items))


def _with_exchange(body, n_in, n_out, side, grid):
    ex = _exchange_io(side)
    n = len(ex.args)
    if not n:
        return body, ex
    modes = [g for _, g in side]

    def wrapped(*refs):
        ins, src = refs[:n_in], refs[n_in:n_in + n]
        outs, dst = refs[n_in + n:n_in + n + n_out], refs[n_in + n + n_out:n_in + 2 * n + n_out]
        sems = refs[n_in + 2 * n + n_out:]
        start, wait = _exchange_ops(modes, src, dst, *sems)
        ids = [pl.program_id(a) for a in range(len(grid))]
        first, last = ids[0] == 0, ids[0] == grid[0] - 1
        for a in range(1, len(grid)):
            first, last = first & (ids[a] == 0), last & (ids[a] == grid[a] - 1)
        pl.when(first)(start)
        body(*ins, *outs)
        pl.when(last)(wait)

    return wrapped, ex


LANES = 1024


def adamw(gstacks, w, m, v, name):
    D, R, C = w.shape
    S = gstacks[0].shape[0]
    tr = R if R <= 512 else _tile(R, 256)

    def body(*refs):
        g_refs = refs[:D]
        w_ref, m_ref, v_ref, go_ref, d_ref, mo_ref, vo_ref = refs[D:]
        layer = pl.program_id(0)
        g = None
        for d, g_ref in enumerate(g_refs):
            gd = g_ref[0].astype(f32)
            for sl in range(1, S):
                gd = gd + g_ref[sl].astype(f32)
            g = gd if g is None else jnp.where(layer == d, gd, g)
        g = g[None]
        m_new = ADAM_B1 * m_ref[...] + (1.0 - ADAM_B1) * g
        v_new = ADAM_B2 * v_ref[...] + (1.0 - ADAM_B2) * jnp.square(g)
        m_hat = m_new / (1.0 - ADAM_B1 ** ADAM_STEP)
        v_hat = v_new / (1.0 - ADAM_B2 ** ADAM_STEP)
        go_ref[...] = g
        d_ref[...] = -ADAM_LR * (m_hat / (jnp.sqrt(v_hat) + ADAM_EPS) + ADAM_WD * w_ref[...])
        mo_ref[...] = m_new
        vo_ref[...] = v_new

    row = pl.BlockSpec((1, tr, C), lambda l, i: (l, i, 0))
    g_specs = [pl.BlockSpec((S, tr, C), lambda l, i, d=d: (0, jnp.where(l == d, i, 0), 0)) for d in range(D)]
    return pl.pallas_call(
        body, name=name, grid=(D, R // tr), in_specs=g_specs + [row, row, row],
        out_specs=[row] * 4, out_shape=[jax.ShapeDtypeStruct((D, R, C), f32)] * 4, compiler_params=_params(("parallel", "parallel")),
    )(*gstacks, w, m, v)


def sum_slots(gstack, name):
    S, R, _ = gstack.shape

    def body(g_ref, o_ref):
        g = g_ref[0]
        for sl in range(1, S):
            g = g + g_ref[sl]
        o_ref[...] = g

    return pl.pallas_call(body, name=name, out_shape=jax.ShapeDtypeStruct((R, LANES), f32), compiler_params=_params())(gstack)


def _pack(parts, rows, dtype=f32):
    flat = jnp.concatenate([q.reshape(-1).astype(dtype) for q in parts])
    return jnp.pad(flat, (0, rows * LANES - flat.shape[0])).reshape(rows, LANES)


def _unpack(buf, shapes):
    lead = buf.shape[:-2]
    flat = buf.reshape(lead + (-1,))
    out, off = [], 0
    for shp in shapes:
        n = math.prod(shp)
        out.append(flat[..., off:off + n].reshape(lead + tuple(shp)))
        off += n
    return out


_WEIGHTS = ["pre_mix_norm", "w_in", "a_q_norm", "a_w_uq", "a_kv_norm", "a_w_ukv", "a_out_norm", "b_q_norm", "b_k_norm", "b_out_norm",
            "c_conv_w", "c_conv_b", "c_a_log", "c_dt_bias", "c_d_skip", "c_out_norm", "d_conv_w", "d_a_log", "d_dt_bias", "d_out_norm",
            "w_out", "post_mix_norm", "pre_ffn_norm", "f_w_in", "f_conv_w", "f_conv_b", "f_w_out", "post_ffn_norm"]
_BIG = {"w_in": 1, "a_w_uq": 1, "a_w_ukv": 1, "w_out": 0, "f_w_in": 1, "f_w_out": 0}
_CONV = ["c_conv_w", "d_conv_w", "f_conv_w"]
_REP = [n for n in _WEIGHTS if n not in _BIG and n not in _CONV]
SMALLG_ROWS, SMALLW_ROWS = 64, 32


def _join(blocks, axis):
    return jnp.concatenate([blocks[d] for d in range(N_DEV)], axis=axis)


def _split(full, axis):
    n = full.shape[axis] // N_DEV
    return jnp.stack([lax.slice_in_dim(full, d * n, (d + 1) * n, axis=axis) for d in range(N_DEV)])


def kernel(x, pre_mix_norm, w_in, a_q_norm, a_w_uq, a_kv_norm, a_w_ukv, a_out_norm, b_q_norm, b_k_norm, b_out_norm, c_conv_w, c_conv_b, c_a_log, c_dt_bias, c_d_skip, c_out_norm, d_conv_w, d_a_log, d_dt_bias, d_out_norm, w_out, post_mix_norm, pre_ffn_norm, f_w_in, f_conv_w, f_conv_b, f_w_out, post_ffn_norm, loss_target, m_pre_mix_norm, m_w_in, m_a_q_norm, m_a_w_uq, m_a_kv_norm, m_a_w_ukv, m_a_out_norm, m_b_q_norm, m_b_k_norm, m_b_out_norm, m_c_conv_w, m_c_conv_b, m_c_a_log, m_c_dt_bias, m_c_d_skip, m_c_out_norm, m_d_conv_w, m_d_a_log, m_d_dt_bias, m_d_out_norm, m_w_out, m_post_mix_norm, m_pre_ffn_norm, m_f_w_in, m_f_conv_w, m_f_conv_b, m_f_w_out, m_post_ffn_norm, v_pre_mix_norm, v_w_in, v_a_q_norm, v_a_w_uq, v_a_kv_norm, v_a_w_ukv, v_a_out_norm, v_b_q_norm, v_b_k_norm, v_b_out_norm, v_c_conv_w, v_c_conv_b, v_c_a_log, v_c_dt_bias, v_c_d_skip, v_c_out_norm, v_d_conv_w, v_d_a_log, v_d_dt_bias, v_d_out_norm, v_w_out, v_post_mix_norm, v_pre_ffn_norm, v_f_w_in, v_f_conv_w, v_f_conv_b, v_f_w_out, v_post_ffn_norm):
    given = dict(locals())
    w = {n: given[n] for n in _WEIGHTS}
    mom = {n: given["m_" + n] for n in _WEIGHTS}
    var = {n: given["v_" + n] for n in _WEIGHTS}
    me = 4 * lax.axis_index("x") + 2 * lax.axis_index("y") + lax.axis_index("c")
    layered = lambda names: [(l, n) for l in range(DEPTH) for n in names]

    sharded = list(_BIG) + _CONV
    early = [n for n in sharded if n not in _LATE]
    weight_items = lambda l, names: [(w[n][l].astype(MXU_DTYPE if n in _BIG else f32), True) for n in names]
    grad_items = lambda g, names: [(_split(g[n], _BIG[n]).astype(WIRE_DTYPE), False) for n in names]
    P = {n: [w[n][l] for l in range(DEPTH)] for n in _REP}
    P.update({n: [None] * DEPTH for n in sharded})

    def place(l, names, gathered):
        for n, got in zip(names, gathered):
            P[n][l] = _join(got, _BIG.get(n, 1))

    def late_weights_l0(gathered):
        place(0, _LATE, gathered)
        return _late_kernel_weights(P, 0, MXU_DTYPE)

    tabs = _rope_tables(x.shape[1])
    place(0, early, exchange(weight_items(0, early), "gather_weights"))
    x1, s0, got, W0 = _layer_fwd(x[0], _kernel_weights(P, 0, MXU_DTYPE), tabs, "l0_", weight_items(1, sharded),
                                 weight_items(0, _LATE), late_weights_l0)
    place(1, sharded, got)
    x2, s1, _, W1 = _layer_fwd(x1, _kernel_weights(P, 1, MXU_DTYPE), tabs, "l1_")
    loss, dx = loss_and_grad(x2, loss_target[0], "loss")
    loss = lax.psum(loss[0, 0], ("x", "y", "c"))

    def late_grads_l0(g):
        return grad_items({"w_out": g["w_out"], "f_w_in": _ff_deinterleave(g["f_win"]), "f_w_out": g["f_wout"]}, _LATE)

    dx, g1, _, _ = _layer_bwd(s1, dx, W1, tabs, "l1_")
    g1 = _reference_grads(g1)
    dx, g0, recv1, recv0_late = _layer_bwd(s0, dx, W0, tabs, "l0_", grad_items(g1, _BIG), late_grads_l0)
    grads = [_reference_grads(g0), g1]
    small = _REP + _CONV
    full_shapes = [grads[l][n].shape for l, n in layered(small)]
    first = [n for n in _BIG if n not in _LATE]
    *recv0_first, gsmall = exchange(grad_items(grads[0], first) + [(_pack([grads[l][n] for l, n in layered(small)], SMALLG_ROWS), True)],
                                    "exchange_grads")
    recv0 = dict(zip(first + _LATE, recv0_first + recv0_late))

    out = {}
    for n, got1 in zip(_BIG, recv1):
        out[n] = adamw([recv0[n], got1], w[n], mom[n], var[n], "adamw_" + n)
    gsum = dict(zip(layered(small), _unpack(sum_slots(gsmall, "sum_small_grads"), full_shapes)))
    for l, n in layered(_CONV):
        cols = w[n].shape[2]
        gsum[(l, n)] = lax.dynamic_slice_in_dim(gsum[(l, n)], me * cols, cols, axis=1)
    held_shapes = [w[n].shape[1:] for _, n in layered(small)]
    pack_small = lambda d: _pack([d[n][l] for l, n in layered(small)], SMALLW_ROWS)[None]
    res_small = adamw([_pack([gsum[k] for k in layered(small)], SMALLW_ROWS)[None]], pack_small(w), pack_small(mom),
                      pack_small(var), "adamw_small")
    per_kind = [dict(zip(layered(small), _unpack(buf[0], held_shapes))) for buf in res_small]
    for n in small:
        out[n] = [jnp.stack([per_kind[kind][(l, n)] for l in range(DEPTH)]) for kind in range(4)]
    return (loss, dx[None], *[out[n][kind] for kind in range(4) for n in _WEIGHTS])
```

```python
import functools
import math

import jax
import jax.numpy as jnp
from jax import lax
from jax.experimental import pallas as pl
from jax.experimental.pallas import tpu as pltpu

f32 = jnp.float32
MXU_DTYPE = jnp.bfloat16
WIRE_DTYPE = jnp.bfloat16
HI = lax.Precision.HIGHEST

D_MODEL = 1024
DEPTH = 2
GRID_W = 64
ROPE_BASE = 10000.0
EPS = 1e-6
A_Q_LORA, A_KV_LORA, A_ROPE, A_NOPE = 192, 128, 32, 64
SSD_CHUNK = 256
DN_CHUNK = 64
HEAD = 64
HPAD = 128
D_FF = 2816
FF_BLK = 256
N_DEV = 8
ADAM_LR, ADAM_B1, ADAM_B2, ADAM_EPS, ADAM_WD, ADAM_STEP = 0.001, 0.9, 0.999, 1e-08, 0.01, 10

V7X_VMEM_BYTES = 64 * 2 ** 20
VMEM_LIMIT = (V7X_VMEM_BYTES * 3) // 4

PIN = 3072
SEG_A, SEG_B, SEG_CX, SEG_DX, SEG_CZ, SEG_DZ, SEG_CDT, SEG_DAB = 0, 512, 1024, 1536, 2304, 2560, 2816, 2944


def _tile(n, pref):
    for t in (512, 256, 128, 64, 32, 16, 8):
        if t <= pref and n % t == 0:
            return t
    raise ValueError(f"no tile for {n}")


LANE = 128


def _tile_div(n, pref):
    if n <= pref:
        return n
    return max(d for d in range(LANE, pref + 1, LANE) if n % d == 0)


def _params(sem=None):
    return pltpu.CompilerParams(vmem_limit_bytes=VMEM_LIMIT, dimension_semantics=sem)


def _tup(r):
    return tuple(r) if isinstance(r, (tuple, list)) else (r,)


def matmul(a, b, form, out_dtype, name):
    if form == "nn":
        (M, K), N = a.shape, b.shape[1]
    elif form == "nt":
        (M, K), N = a.shape, b.shape[0]
    else:
        (K, M), N = a.shape, b.shape[1]
    if form == "tn":
        tm, tn, tk = (512 if M % 512 == 0 else _tile_div(M, 1408)), _tile_div(N, 1408), _tile_div(K, 2048)
    else:
        tm, tn, tk = _tile_div(M, 512), _tile_div(N, 2816), _tile_div(K, 5632 if jnp.dtype(a.dtype).itemsize == 2 else 2816)
    nk = K // tk
    dims = {"nn": ((1,), (0,)), "nt": ((1,), (1,)), "tn": ((0,), (0,))}[form]

    def body(a_ref, b_ref, o_ref, *acc):
        part = lax.dot_general(a_ref[...].astype(MXU_DTYPE), b_ref[...].astype(MXU_DTYPE), (dims, ((), ())),
                               preferred_element_type=f32)
        if nk == 1:
            o_ref[...] = part.astype(out_dtype)
            return
        (acc_ref,) = acc
        k = pl.program_id(2)

        @pl.when(k == 0)
        def _():
            acc_ref[...] = part

        @pl.when((k > 0) & (k < nk - 1))
        def _():
            acc_ref[...] += part

        @pl.when(k == nk - 1)
        def _():
            o_ref[...] = (acc_ref[...] + part).astype(out_dtype)

    a_spec = pl.BlockSpec((tk, tm), lambda i, j, k: (k, i)) if form == "tn" else pl.BlockSpec((tm, tk), lambda i, j, k: (i, k))
    b_spec = pl.BlockSpec((tn, tk), lambda i, j, k: (j, k)) if form == "nt" else pl.BlockSpec((tk, tn), lambda i, j, k: (k, j))
    return pl.pallas_call(
        body, name=name, grid=(M // tm, N // tn, nk), in_specs=[a_spec, b_spec],
        out_specs=pl.BlockSpec((tm, tn), lambda i, j, k: (i, j)), out_shape=jax.ShapeDtypeStruct((M, N), out_dtype),
        scratch_shapes=[pltpu.VMEM((tm, tn), f32)] if nk > 1 else [], compiler_params=_params(("parallel", "parallel", "arbitrary")),
    )(a, b)


def _row_specs(rows, tm):
    return [pl.BlockSpec((tm, w), lambda i, b=b: (i, b)) for (_, w, b) in rows]


def _full_specs(params):
    return [pl.BlockSpec(p.shape, lambda i: (0, 0)) for p in params]


def rowwise(fn, rows, params, outs, name, tm=512):
    L = rows[0][0].shape[0]
    tm = _tile(L, tm)
    n_in = len(rows) + len(params)

    def body(*refs):
        res = _tup(fn(*[r[...].astype(f32) for r in refs[:n_in]]))
        for o_ref, r in zip(refs[n_in:], res, strict=True):
            o_ref[...] = r.astype(o_ref.dtype)

    res = pl.pallas_call(
        body, name=name, grid=(L // tm,), in_specs=_row_specs(rows, tm) + _full_specs(params),
        out_specs=[pl.BlockSpec((tm, w), lambda i: (i, 0)) for (w, _) in outs],
        out_shape=[jax.ShapeDtypeStruct((L, w), dt) for (w, dt) in outs], compiler_params=_params(("parallel",)),
    )(*[r[0] for r in rows], *params)
    return list(res)


def rowwise_vjp(fn, rows, params, cts, name, row_grads, param_grads, tm=512, row_dtypes=None):
    L = rows[0][0].shape[0]
    tm = _tile(L, tm)
    nr, npar, nct = len(rows), len(params), len(cts)
    row_dtypes = [f32] * len(row_grads) if row_dtypes is None else row_dtypes

    def body(*refs):
        i = pl.program_id(0)
        rv = [r[...].astype(f32) for r in refs[:nr]]
        pv = [r[...].astype(f32) for r in refs[nr:nr + npar]]
        cv = tuple(r[...].astype(f32) for r in refs[nr + npar:nr + npar + nct])
        out_refs = refs[nr + npar + nct:]

        def g(*diff):
            rr, pp = list(rv), list(pv)
            for k, v in zip(row_grads, diff[:len(row_grads)]):
                rr[k] = v
            for k, v in zip(param_grads, diff[len(row_grads):]):
                pp[k] = v
            return _tup(fn(*rr, *pp))

        _, vjp = jax.vjp(g, *[rv[k] for k in row_grads], *[pv[k] for k in param_grads])
        grads = vjp(cv)
        for o_ref, gval in zip(out_refs[:len(row_grads)], grads[:len(row_grads)]):
            o_ref[...] = gval.astype(o_ref.dtype)
        for o_ref, gval in zip(out_refs[len(row_grads):], grads[len(row_grads):]):
            @pl.when(i == 0)
            def _(o_ref=o_ref, gval=gval):
                o_ref[...] = gval

            @pl.when(i > 0)
            def _(o_ref=o_ref, gval=gval):
                o_ref[...] += gval

    out_specs = [pl.BlockSpec((tm, rows[k][1]), lambda i: (i, 0)) for k in row_grads] + \
                [pl.BlockSpec(params[k].shape, lambda i: (0, 0)) for k in param_grads]
    out_shape = [jax.ShapeDtypeStruct((L, rows[k][1]), dt) for k, dt in zip(row_grads, row_dtypes)] + \
                [jax.ShapeDtypeStruct(params[k].shape, f32) for k in param_grads]
    res = pl.pallas_call(
        body, name=name, grid=(L // tm,), in_specs=_row_specs(rows, tm) + _full_specs(params) + _row_specs(cts, tm),
        out_specs=out_specs, out_shape=out_shape, compiler_params=_params(("arbitrary",)),
    )(*[r[0] for r in rows], *params, *[c[0] for c in cts])
    res = list(res)
    return res[:len(row_grads)], res[len(row_grads):]


def _rms(x, w, n=None):
    n = x.shape[-1] if n is None else n
    return x * lax.rsqrt(jnp.sum(x * x, axis=-1, keepdims=True) * (1.0 / n) + EPS) * w


def _silu(x):
    return x * jax.nn.sigmoid(x)


def _softplus(x):
    return jnp.maximum(x, 0.0) + jnp.log1p(jnp.exp(-jnp.abs(x)))


def _mm(a, b, dims, precision=None):
    if precision is None:
        a, b = a.astype(MXU_DTYPE), b.astype(MXU_DTYPE)
    return lax.dot_general(a, b, (dims, ((), ())), precision=precision, preferred_element_type=f32)


_NN, _NT, _TN = ((1,), (0,)), ((1,), (1,)), ((0,), (0,))


@functools.partial(jax.custom_vjp, nondiff_argnums=(1, 2))
def _roll(x, shift, axis):
    return pltpu.roll(x, shift, axis)


def _roll_fwd(x, shift, axis):
    return pltpu.roll(x, shift, axis), None


def _roll_bwd(shift, axis, _, ct):
    return (pltpu.roll(ct, (ct.shape[axis] - shift) % ct.shape[axis], axis),)


_roll.defvjp(_roll_fwd, _roll_bwd)


def _rope(x, cos, sin, lo, half):
    n = x.shape[-1]
    lane = lax.broadcasted_iota(jnp.int32, x.shape, x.ndim - 1) % HPAD - lo
    first = ((lane >= 0) & (lane < half)) | ((lane >= 2 * half) & (lane < 3 * half))
    rot = jnp.where(first, -_roll(x, n - half, x.ndim - 1), _roll(x, half, x.ndim - 1))
    return x * cos + rot * sin


def _heads(x, n, width=HEAD):
    return [x[:, h * width:(h + 1) * width] for h in range(n)]


def _pad_heads(hs):
    z = jnp.zeros_like(hs[0])
    return jnp.concatenate([t for h in hs for t in (h, z)], axis=-1)


def _unpad_heads(x, n):
    return jnp.concatenate([x[:, h * HPAD:h * HPAD + HEAD] for h in range(n)], axis=-1)


HALO = 8


def _conv_specs(tm, tc, c0, L):
    nh = tm // HALO
    last = L // HALO - 1
    return [pl.BlockSpec((tm, tc), lambda i, j: (i, c0 + j)),
            pl.BlockSpec((HALO, tc), lambda i, j: (jnp.maximum(i * nh - 1, 0), c0 + j)),
            pl.BlockSpec((HALO, tc), lambda i, j: (jnp.minimum((i + 1) * nh, last), c0 + j))]


def _shift3(x):
    n = x.shape[0]
    return _roll(x, 1, 0), _roll(x, n - 1, 0)


PREFETCH_DEPTH = 3


def conv_fwd(gfn, x, c0, ncol, tc, w, b, tco, out_dtype, name, tm=512, prefetch=False):
    L = x.shape[0]
    tm = _tile_div(L, tm)
    ni = L // tm
    total = ni * ncol

    def fetch(x_hbm, buf, sems, step, slot):
        rows = pl.ds(pl.multiple_of((step // ncol) * tm, tm), tm)
        cols = pl.ds(pl.multiple_of((c0 + step % ncol) * tc, tc), tc)
        return pltpu.make_async_copy(x_hbm.at[rows, cols], buf.at[slot], sems.at[slot])

    def body(x_ref, p_ref, n_ref, w_ref, b_ref, o_ref, c_ref, *scratch):
        i = pl.program_id(0)
        if prefetch:
            buf, sems = scratch
            step = i * ncol + pl.program_id(1)
            slot = step % PREFETCH_DEPTH

            @pl.when(step == 0)
            def _():
                for k in range(min(PREFETCH_DEPTH, total)):
                    fetch(x_ref, buf, sems, k, k).start()

            fetch(x_ref, buf, sems, step, slot).wait()
            xv = buf[slot].astype(f32)
        else:
            xv = x_ref[...].astype(f32)
        xp = jnp.where(i == 0, 0.0, p_ref[HALO - 1:HALO, :].astype(f32))
        xn = jnp.where(i == ni - 1, 0.0, n_ref[0:1, :].astype(f32))
        rid = lax.broadcasted_iota(jnp.int32, xv.shape, 0)
        dn, up = _shift3(xv)
        dn = jnp.where(rid == 0, xp, dn)
        up = jnp.where(rid == tm - 1, xn, up)
        c = w_ref[0:1, :] * dn + w_ref[1:2, :] * xv + w_ref[2:3, :] * up + b_ref[...]
        c_ref[...] = c
        o_ref[...] = gfn(c).astype(o_ref.dtype)
        if prefetch:
            @pl.when(step + PREFETCH_DEPTH < total)
            def _():
                fetch(x_ref, buf, sems, step + PREFETCH_DEPTH, slot).start()

    specs = _conv_specs(tm, tc, c0, L)
    if prefetch:
        specs[0] = pl.BlockSpec(memory_space=pl.ANY)
    return pl.pallas_call(
        body, name=name, grid=(ni, ncol),
        in_specs=specs + [pl.BlockSpec((3, tc), lambda i, j: (0, j)), pl.BlockSpec((1, tc), lambda i, j: (0, j))],
        out_specs=[pl.BlockSpec((tm, tco), lambda i, j: (i, j)), pl.BlockSpec((tm, tc), lambda i, j: (i, j))],
        out_shape=[jax.ShapeDtypeStruct((L, ncol * tco), out_dtype), jax.ShapeDtypeStruct((L, ncol * tc), f32)],
        scratch_shapes=[pltpu.VMEM((PREFETCH_DEPTH, tm, tc), x.dtype), pltpu.SemaphoreType.DMA((PREFETCH_DEPTH,))] if prefetch else [],
        compiler_params=_params(("arbitrary", "arbitrary") if prefetch else ("parallel", "parallel")),
    )(x, x, x, w, b)


def conv_bwd(gfn, x, c, c0, ncol, tc, w, dys, tco, name, tm=512, dx_dtype=f32, prefetch=False):
    L = x.shape[0]
    tm = _tile_div(L, tm)
    ni = L // tm
    nd = len(dys)
    total = ni * ncol

    def fetch(hbm, col0, buf, sems, step, slot):
        rows = pl.ds(pl.multiple_of((step % ni) * tm, tm), tm)
        cols = pl.ds(pl.multiple_of((col0 + step // ni) * tc, tc), tc)
        return pltpu.make_async_copy(hbm.at[rows, cols], buf.at[slot], sems.at[slot])

    def body(*refs):
        x_ref, xp_ref, xn_ref, c_ref, cp_ref, cn_ref = refs[:6]
        d_refs = refs[6:6 + 3 * nd]
        w_ref, dx_ref, dw_ref, db_ref, *scratch = refs[6 + 3 * nd:]
        i = pl.program_id(1)
        first, lastb = i == 0, i == ni - 1
        if prefetch:
            xbuf, cbuf, xsems, csems = scratch
            step = pl.program_id(0) * ni + i
            slot = step % PREFETCH_DEPTH
            x_hbm, c_hbm = x_ref, c_ref
            both = lambda st, sl: (fetch(x_hbm, c0, xbuf, xsems, st, sl), fetch(c_hbm, 0, cbuf, csems, st, sl))

            @pl.when(step == 0)
            def _():
                for k in range(min(PREFETCH_DEPTH, total)):
                    for cp in both(k, k):
                        cp.start()

            for cp in both(step, slot):
                cp.wait()
            x_ref, c_ref = xbuf[slot], cbuf[slot]

        def ext(m, p, n):
            return jnp.concatenate([jnp.where(first, 0.0, p[...].astype(f32)), m[...].astype(f32),
                                    jnp.where(lastb, 0.0, n[...].astype(f32))], axis=0)

        xe = ext(x_ref, xp_ref, xn_ref)
        de = ext(*d_refs[:3])
        for q in range(1, nd):
            de = de + ext(*d_refs[3 * q:3 * q + 3])
        x_dn, x_up = _shift3(xe)
        w0, w1, w2 = w_ref[0:1, :], w_ref[1:2, :], w_ref[2:3, :]
        _, vjp = jax.vjp(gfn, ext(c_ref, cp_ref, cn_ref))
        (dce,) = vjp(de)
        dc_dn, dc_up = _shift3(dce)
        dx_ref[...] = (w0 * dc_up + w1 * dce + w2 * dc_dn)[HALO:HALO + tm, :].astype(dx_ref.dtype)
        inner = slice(HALO, HALO + tm)
        dci = dce[inner]
        dw = jnp.concatenate([jnp.sum(dci * x_dn[inner], axis=0, keepdims=True), jnp.sum(dci * xe[inner], axis=0, keepdims=True),
                              jnp.sum(dci * x_up[inner], axis=0, keepdims=True)], axis=0)
        db = jnp.sum(dci, axis=0, keepdims=True)

        @pl.when(first)
        def _():
            dw_ref[...] = dw
            db_ref[...] = db

        @pl.when(i > 0)
        def _():
            dw_ref[...] += dw
            db_ref[...] += db

        if prefetch:
            @pl.when(step + PREFETCH_DEPTH < total)
            def _():
                for cp in both(step + PREFETCH_DEPTH, slot):
                    cp.start()

    swap = lambda specs: [pl.BlockSpec(s.block_shape, (lambda j, i, f=s.index_map: f(i, j))) for s in specs]
    x_specs, c_specs = swap(_conv_specs(tm, tc, c0, L)), swap(_conv_specs(tm, tc, 0, L))
    scratch = []
    if prefetch:
        x_specs[0] = c_specs[0] = pl.BlockSpec(memory_space=pl.ANY)
        scratch = [pltpu.VMEM((PREFETCH_DEPTH, tm, tc), x.dtype), pltpu.VMEM((PREFETCH_DEPTH, tm, tc), c.dtype),
                   pltpu.SemaphoreType.DMA((PREFETCH_DEPTH,)), pltpu.SemaphoreType.DMA((PREFETCH_DEPTH,))]
    res = pl.pallas_call(
        body, name=name, grid=(ncol, ni),
        in_specs=x_specs + c_specs + swap(_conv_specs(tm, tco, 0, L)) * nd + [pl.BlockSpec((3, tc), lambda j, i: (0, j))],
        out_specs=[pl.BlockSpec((tm, tc), lambda j, i: (i, j)), pl.BlockSpec((3, tc), lambda j, i: (0, j)),
                   pl.BlockSpec((1, tc), lambda j, i: (0, j))],
        out_shape=[jax.ShapeDtypeStruct((L, ncol * tc), dx_dtype), jax.ShapeDtypeStruct((3, ncol * tc), f32),
                   jax.ShapeDtypeStruct((1, ncol * tc), f32)],
        scratch_shapes=scratch, compiler_params=_params(("arbitrary", "arbitrary") if prefetch else ("parallel", "arbitrary")),
    )(x, x, x, c, c, c, *[d for d in dys for _ in range(3)], w)
    return res


LOG2E = math.log2(math.e)


def flash_fwd(q, k, v, n_q, n_kv, scale, name, side=(), tq=512, tk=8192):
    L = q.shape[0]
    tq, tk = _tile_div(L, tq), _tile_div(L, tk)
    rep = n_q // n_kv
    nkv = L // tk
    c2 = scale * LOG2E

    def body(q_ref, k_ref, v_ref, o_ref, lse_ref):
        qv = q_ref[...]
        ones_lane = lax.broadcasted_iota(jnp.int32, (tk, HPAD), 1) == HPAD - 1

        def step(c, carry):
            m, acc = carry
            off = pl.multiple_of(c * tk, tk)
            s = _mm(qv, k_ref[pl.ds(off, tk), :], _NT)
            m_new = jnp.maximum(m, jnp.max(s, axis=-1, keepdims=True))
            p = jnp.exp2((s - m_new) * c2)
            vv = v_ref[pl.ds(off, tk), :]
            acc = jnp.exp2((m - m_new) * c2) * acc + _mm(p, jnp.where(ones_lane, jnp.ones_like(vv), vv), _NN)
            return m_new, acc

        m, acc = lax.fori_loop(0, nkv, step, (jnp.full((tq, 1), -jnp.inf, f32), jnp.zeros((tq, HPAD), f32)))
        l = acc[:, HPAD - 1:]
        o_ref[...] = jnp.where(lax.broadcasted_iota(jnp.int32, (tq, HPAD), 1) == HPAD - 1, 0.0, acc / l)
        lse_ref[0] = m * scale + jnp.log(l)

    grid = (n_q, L // tq)
    body, ex = _with_exchange(body, 3, 2, side, grid)
    res = pl.pallas_call(
        body, name=name, grid=grid,
        in_specs=[pl.BlockSpec((tq, HPAD), lambda h, i: (i, h)), pl.BlockSpec((L, HPAD), lambda h, i: (0, h // rep)),
                  pl.BlockSpec((L, HPAD), lambda h, i: (0, h // rep))] + ex.specs,
        out_specs=[pl.BlockSpec((tq, HPAD), lambda h, i: (i, h)), pl.BlockSpec((1, tq, 1), lambda h, i: (h, i, 0))] + ex.specs,
        out_shape=[jax.ShapeDtypeStruct((L, n_q * HPAD), f32), jax.ShapeDtypeStruct((n_q, L, 1), f32)] + ex.out_shape,
        scratch_shapes=ex.scratch, compiler_params=_params(("arbitrary", "arbitrary") if side else ("parallel", "parallel")),
    )(q, k, v, *ex.args)
    return res[0], res[1], list(res[2:])


def flash_bwd(q, k, v, o, lse, do, n_q, n_kv, scale, name, side=(), tq=2048, tk=1024):
    L = q.shape[0]
    tq, tk = _tile_div(L, tq), _tile_div(L, tk)
    rep = n_q // n_kv
    c2 = scale * LOG2E

    def body(k_ref, v_ref, q_ref, do_ref, o_ref, lse_ref, dq_ref, dk_ref, dv_ref):
        j, i = pl.program_id(1), pl.program_id(2)
        kv, vv = k_ref[...], v_ref[...]
        rows = pl.ds(pl.multiple_of(i * tq, tq), tq)
        dk, dv = jnp.zeros((tk, HPAD), f32), jnp.zeros((tk, HPAD), f32)
        for r in range(rep):
            cols = slice(r * HPAD, (r + 1) * HPAD)
            qv = q_ref[:, cols]
            dov = do_ref[:, cols]
            delta = jnp.sum(dov * o_ref[:, cols], axis=-1, keepdims=True)
            p = jnp.exp2(_mm(qv, kv, _NT) * c2 - lse_ref[r] * LOG2E)
            dv = dv + _mm(p, dov, _TN)
            ds = p * (_mm(dov, vv, _NT) - delta) * scale
            dk = dk + _mm(ds, qv, _TN)
            dq = _mm(ds, kv, _NN)

            @pl.when(j == 0)
            def _(dq=dq, cols=cols):
                dq_ref[rows, cols] = dq

            @pl.when(j > 0)
            def _(dq=dq, cols=cols):
                dq_ref[rows, cols] += dq

        @pl.when(i == 0)
        def _():
            dk_ref[...] = dk
            dv_ref[...] = dv

        @pl.when(i > 0)
        def _():
            dk_ref[...] += dk
            dv_ref[...] += dv

    w = rep * HPAD
    grid = (n_kv, L // tk, L // tq)
    body, ex = _with_exchange(body, 6, 3, side, grid)
    res = pl.pallas_call(
        body, name=name, grid=grid,
        in_specs=[pl.BlockSpec((tk, HPAD), lambda g, j, i: (j, g)), pl.BlockSpec((tk, HPAD), lambda g, j, i: (j, g)),
                  pl.BlockSpec((tq, w), lambda g, j, i: (i, g)), pl.BlockSpec((tq, w), lambda g, j, i: (i, g)),
                  pl.BlockSpec((tq, w), lambda g, j, i: (i, g)), pl.BlockSpec((rep, tq, 1), lambda g, j, i: (g, i, 0))] + ex.specs,
        out_specs=[pl.BlockSpec((L, w), lambda g, j, i: (0, g)), pl.BlockSpec((tk, HPAD), lambda g, j, i: (j, g)),
                   pl.BlockSpec((tk, HPAD), lambda g, j, i: (j, g))] + ex.specs,
        out_shape=[jax.ShapeDtypeStruct((L, n_q * HPAD), f32), jax.ShapeDtypeStruct((L, n_kv * HPAD), f32),
                   jax.ShapeDtypeStruct((L, n_kv * HPAD), f32)] + ex.out_shape,
        scratch_shapes=ex.scratch, compiler_params=_params(("arbitrary",) * 3 if side else ("parallel", "arbitrary", "arbitrary")),
    )(k, v, q, do, o, lse, *ex.args)
    return res[0], res[1], res[2], list(res[3:])


N_SCAN_HEADS = 4
STATE_ROWS = N_SCAN_HEADS * HEAD


def _tri(n, rev):
    i = lax.broadcasted_iota(jnp.int32, (n, n), 0)
    k = lax.broadcasted_iota(jnp.int32, (n, n), 1)
    return ((k >= i), (k > i)) if rev else ((k <= i), (k < i))


def _decay(acol, incl):
    n = acol.shape[0]
    m1 = jnp.broadcast_to(acol, (n, n))
    return jnp.exp(jnp.where(incl, m1 - m1.T, -jnp.inf))


def _ssd_chunk(S, xbc, win, a_log, dt_bias, rev):
    Q = xbc.shape[0]
    d = 1 if rev else 0
    incl, _ = _tri(Q, rev)
    dt = _softplus(win[:, 4 * d:4 * d + 4] + dt_bias[d:d + 1, :])
    a = dt * (-jnp.exp(a_log[d:d + 1, :]))
    acum = _mm(incl.astype(f32), a, _NN, HI)
    tot = jnp.sum(a, axis=0, keepdims=True)
    xs, Bm, Cm = xbc[:, :256], xbc[:, 256:384], xbc[:, 384:512]
    H, W = N_SCAN_HEADS, N_SCAN_HEADS * HEAD
    lanes = lambda x: _cat([jnp.broadcast_to(x[:, h:h + 1], (x.shape[0], HEAD)) for h in range(H)])
    xdt = xs * lanes(dt)
    cb = [_mm(Cm[:, g * HEAD:(g + 1) * HEAD], Bm[:, g * HEAD:(g + 1) * HEAD], _NT) for g in range(2)]
    scores = _cat([cb[h // 2] * _decay(acum[:, h:h + 1], incl) for h in range(H)])
    own = (lax.broadcasted_iota(jnp.int32, (H * Q, W), 0) // Q) == (lax.broadcasted_iota(jnp.int32, (H * Q, W), 1) // HEAD)
    y = _mm(scores, jnp.where(own, jnp.concatenate([xdt] * H, axis=0), 0.0), _NN)
    grp = (lax.broadcasted_iota(jnp.int32, (W, 2 * HEAD), 0) // (2 * HEAD)) == (lax.broadcasted_iota(jnp.int32, (W, 2 * HEAD), 1) // HEAD)
    y = y + _mm(Cm, jnp.where(grp, _cat([S, S]), 0.0), _NT) * jnp.exp(lanes(acum))
    st = _mm(xdt * jnp.exp(lanes(tot - acum)), Bm, _TN)
    first = lax.broadcasted_iota(jnp.int32, (W, HEAD), 0) < 2 * HEAD
    exp_tot = jnp.concatenate([jnp.broadcast_to(jnp.exp(tot[:, h:h + 1]), (HEAD, 1)) for h in range(H)], axis=0)
    return S * exp_tot + jnp.where(first, st[:, :HEAD], st[:, HEAD:]), y


@functools.partial(jax.custom_vjp, nondiff_argnums=(1,))
def _inv_unit_tri(Lm, order):
    n = Lm.shape[0]
    eye = (lax.broadcasted_iota(jnp.int32, (n, n), 0) == lax.broadcasted_iota(jnp.int32, (n, n), 1)).astype(f32)
    P = -Lm
    T = eye + P
    k = 1
    while 2 * k < order:
        P = _mm(P, P, _NN, lax.Precision.HIGH)
        T = T + _mm(T, P, _NN, lax.Precision.HIGH)
        k *= 2
    return T


def _inv_unit_tri_fwd(Lm, order):
    T = _inv_unit_tri(Lm, order)
    return T, T


def _inv_unit_tri_bwd(order, T, dT):
    return (-_mm(_mm(T, dT, _TN, lax.Precision.HIGH), T, _NT, lax.Precision.HIGH),)


_inv_unit_tri.defvjp(_inv_unit_tri_fwd, _inv_unit_tri_bwd)


@jax.custom_vjp
def _inv_known(Lm, T):
    return T


def _inv_known_fwd(Lm, T):
    return T, T


def _inv_known_bwd(T, dT):
    return _inv_unit_tri_bwd(None, T, dT) + (jnp.zeros_like(T),)


_inv_known.defvjp(_inv_known_fwd, _inv_known_bwd)


def _dn_chunk(S, qkv, win, a_log, dt_bias, rev, kept=None):
    Q = qkv.shape[0]
    H, R = N_SCAN_HEADS, N_SCAN_HEADS * qkv.shape[0]
    d = 1 if rev else 0
    beta = jax.nn.sigmoid(win[:, 4 * d:4 * d + 4])
    gl = -jnp.exp(a_log[d:d + 1, :]) * _softplus(win[:, 8 + 4 * d:12 + 4 * d] + dt_bias[d:d + 1, :])
    G = _mm(_tri(Q, rev)[0].astype(f32), gl, _NN, HI)
    tot = jnp.sum(gl, axis=0, keepdims=True)
    rows = lambda x: jnp.concatenate(_heads(x, H), axis=0)
    col = lambda x: jnp.concatenate([x[:, h:h + 1] for h in range(H)], axis=0)
    per_head = lambda x, n: jnp.concatenate([jnp.broadcast_to(x[:, h:h + 1], (n, 1)) for h in range(H)], axis=0)
    r = lax.broadcasted_iota(jnp.int32, (R, R), 0)
    c = lax.broadcasted_iota(jnp.int32, (R, R), 1)
    same = (r // Q) == (c // Q)
    incl = same & ((c >= r) if rev else (c <= r))
    strict = same & ((c > r) if rev else (c < r))
    own = (lax.broadcasted_iota(jnp.int32, (R, H * HEAD), 0) // Q) == (lax.broadcasted_iota(jnp.int32, (R, H * HEAD), 1) // HEAD)
    blk = lambda x: jnp.where(own, _cat([x] * H), 0.0)
    q, k, v = rows(qkv[:, :256]) * (HEAD ** -0.5), rows(qkv[:, 256:512]), rows(qkv[:, 512:])
    b, Gs, tots = col(beta), col(G), per_head(tot, Q)
    dec = _decay(Gs, incl)
    kb = k * b
    Lm = jnp.where(strict, _mm(kb, k, _NT) * dec, 0.0)
    T = _inv_unit_tri(Lm, Q) if kept is None else _inv_known(Lm, blk(kept))
    eG = jnp.exp(Gs)
    uw = _mm(T, _cat([v * b, kb * eG]), _NN)
    vnew = uw[:, :HEAD] - _mm(blk(uw[:, HEAD:]), S, _NN)
    o = _mm(blk(q * eG), S, _NN) + _mm(_mm(q, k, _NT) * dec, vnew, _NN)
    S_next = S * jnp.exp(per_head(tot, HEAD)) + _mm(blk(k * jnp.exp(tots - Gs)), vnew, _TN)
    out = S_next, _cat([o[h * Q:(h + 1) * Q, :] for h in range(H)])
    if kept is None:
        out += (jnp.concatenate([T[h * Q:(h + 1) * Q, h * Q:(h + 1) * Q] for h in range(H)], axis=0),)
    return out


def scan_fwd(chunk_fn, Q, x, p, win_blk, a_log, dt_bias, name, keeps=False):
    L, W = x.shape
    nc = L // Q
    fidx, ridx = (lambda t: t), (lambda t: nc - 1 - t)

    def body(xf_ref, xr_ref, wf_ref, wr_ref, al_ref, db_ref, yf_ref, yr_ref, sf_ref, sr_ref, *rest):
        (sf_scr, sr_scr), keep_refs = rest[-2:], rest[:-2]

        @pl.when(pl.program_id(0) == 0)
        def _():
            sf_scr[...] = jnp.zeros_like(sf_scr)
            sr_scr[...] = jnp.zeros_like(sr_scr)

        for d, (x_ref, w_ref, y_ref, sin_ref, s_scr) in enumerate(((xf_ref, wf_ref, yf_ref, sf_ref, sf_scr),
                                                                   (xr_ref, wr_ref, yr_ref, sr_ref, sr_scr))):
            S = s_scr[...]
            sin_ref[...] = S
            res = chunk_fn(S, x_ref[...], w_ref[...], al_ref[...], db_ref[...], d == 1)
            s_scr[...], y_ref[...] = res[:2]
            if keeps:
                keep_refs[d][...] = res[2]

    row = lambda w, idx, b=0: pl.BlockSpec((Q, w), lambda t: (idx(t), b))
    small = pl.BlockSpec(a_log.shape, lambda t: (0, 0))
    state = lambda idx: pl.BlockSpec((STATE_ROWS, HEAD), lambda t: (idx(t), 0))
    n_state = 4 if keeps else 2
    return pl.pallas_call(
        body, name=name, grid=(nc,),
        in_specs=[row(W, fidx), row(W, ridx), row(HPAD, fidx, win_blk), row(HPAD, ridx, win_blk), small, small],
        out_specs=[row(STATE_ROWS, fidx), row(STATE_ROWS, ridx)] + [state(fidx), state(ridx)] * (n_state // 2),
        out_shape=[jax.ShapeDtypeStruct((L, STATE_ROWS), f32)] * 2 + [jax.ShapeDtypeStruct((nc * STATE_ROWS, HEAD), f32)] * n_state,
        scratch_shapes=[pltpu.VMEM((STATE_ROWS, HEAD), f32)] * 2, compiler_params=_params(("arbitrary",)),
    )(x, x, p, p, a_log, dt_bias)


def scan_bwd(chunk_fn, Q, x, p, win_blk, a_log, dt_bias, s_fwd, s_rev, dy, name, kept=()):
    L, W = x.shape
    nc = L // Q
    fidx, ridx = (lambda t: nc - 1 - t), (lambda t: t)
    nk = len(kept)

    def body(xf_ref, xr_ref, wf_ref, wr_ref, al_ref, db_ref, sf_ref, sr_ref, dyf_ref, dyr_ref, *rest):
        keep_refs = rest[:nk]
        dxf_ref, dxr_ref, dwf_ref, dwr_ref, dal_ref, ddb_ref, dsf_scr, dsr_scr = rest[nk:]

        @pl.when(pl.program_id(0) == 0)
        def _():
            dsf_scr[...] = jnp.zeros_like(dsf_scr)
            dsr_scr[...] = jnp.zeros_like(dsr_scr)
            dal_ref[...] = jnp.zeros_like(dal_ref)
            ddb_ref[...] = jnp.zeros_like(ddb_ref)

        dal_sum, ddb_sum = dal_ref[...], ddb_ref[...]
        for rev, x_ref, w_ref, sin_ref, dy_ref, dx_ref, dw_ref, ds_scr in (
                (False, xf_ref, wf_ref, sf_ref, dyf_ref, dxf_ref, dwf_ref, dsf_scr),
                (True, xr_ref, wr_ref, sr_ref, dyr_ref, dxr_ref, dwr_ref, dsr_scr)):
            extra = (keep_refs[int(rev)][...],) if nk else ()
            _, vjp = jax.vjp(lambda S, xv, wv, al, db, rev=rev, extra=extra: chunk_fn(S, xv, wv, al, db, rev, *extra),
                             sin_ref[...], x_ref[...], w_ref[...], al_ref[...], db_ref[...])
            ds_scr[...], dx_ref[...], dw_ref[...], dal, ddb = vjp((ds_scr[...], dy_ref[...]))
            dal_sum, ddb_sum = dal_sum + dal, ddb_sum + ddb
        dal_ref[...] = dal_sum
        ddb_ref[...] = ddb_sum

    row = lambda w, idx, b=0: pl.BlockSpec((Q, w), lambda t: (idx(t), b))
    small = pl.BlockSpec(a_log.shape, lambda t: (0, 0))
    state = lambda idx: pl.BlockSpec((STATE_ROWS, HEAD), lambda t: (idx(t), 0))
    return pl.pallas_call(
        body, name=name, grid=(nc,),
        in_specs=[row(W, fidx), row(W, ridx), row(HPAD, fidx, win_blk), row(HPAD, ridx, win_blk), small, small,
                  state(fidx), state(ridx), row(STATE_ROWS, fidx), row(STATE_ROWS, ridx)] + [state(fidx), state(ridx)][:nk],
        out_specs=[row(W, fidx), row(W, ridx), row(HPAD, fidx), row(HPAD, ridx), small, small],
        out_shape=[jax.ShapeDtypeStruct((L, W), f32)] * 2 + [jax.ShapeDtypeStruct((L, HPAD), f32)] * 2
        + [jax.ShapeDtypeStruct(a_log.shape, f32)] * 2,
        scratch_shapes=[pltpu.VMEM((STATE_ROWS, HEAD), f32)] * 2, compiler_params=_params(("arbitrary",)),
    )(x, x, p, p, a_log, dt_bias, s_fwd, s_rev, dy, dy, *kept)


def _cat(xs):
    return jnp.concatenate(xs, axis=-1)


def _a_prep(pa, cos, sin, qn, wuq, kvn, wk, wv):
    cq, ckv, kr = pa[:, :256], pa[:, 256:384], pa[:, 384:512]
    q = _rope(_mm(_rms(cq, qn, A_Q_LORA), wuq, _NN), _cat([cos] * 4), _cat([sin] * 4), A_NOPE, A_ROPE // 4)
    kvh = _rms(ckv, kvn)
    k = _mm(kvh, wk, _NN) + _cat([_rope(kr, cos, sin, A_NOPE, A_ROPE // 4)] * 4)
    return q, k, _mm(kvh, wv, _NN)


def _b_prep(pb, cos, sin, qn, kn):
    cos4, sin4 = _cat([cos] * 4), _cat([sin] * 4)
    q = _pad_heads([_rms(h, qn) for h in _heads(pb[:, :256], 4)])
    k = _pad_heads([_rms(h, kn) for h in _heads(pb[:, 256:384], 2) for _ in range(2)])
    v = _pad_heads([h for h in _heads(pb[:, 384:512], 2) for _ in range(2)])
    return _rope(q, cos4, sin4, 0, HEAD // 4), _rope(k, cos4, sin4, 0, HEAD // 4), v


def _c_act(c):
    return _silu(c)


def _d_act(c):
    s = _silu(c)
    return _cat([h * lax.rsqrt(jnp.sum(h * h, axis=-1, keepdims=True) + 1e-6) for h in _heads(s[:, :512], 8)] + [s[:, 512:]])


def _ffn_act(c):
    return _silu(c[:, :FF_BLK]) * c[:, FF_BLK:]


def _mix_post(oa, ob, yc_f, yc_r, xbc, cz, od_f, od_r, dz, a_out, b_out, dskip, c_out, d_out):
    o_a = _rms(_unpad_heads(oa, 4), a_out)
    o_b = _rms(_unpad_heads(ob, 4), b_out)
    skip = _cat([jnp.broadcast_to(dskip[:, h:h + 1], (1, HEAD)) for h in range(4)])
    o_c = _rms((yc_f + yc_r + xbc[:, :256] * skip) * _silu(cz), c_out)
    o_d = _cat([_rms(h, d_out) for h in _heads(od_f + od_r, 4)]) * _silu(dz)
    return _cat([o_a, o_b, o_c, o_d])


def _post_mix(x, y1, g_pm, g_pf):
    x1 = x + _rms(y1, g_pm)
    return x1, _rms(x1, g_pf)


def _post_ffn(x1, y2, g):
    return x1 + _rms(y2, g)


def _pre_bwd(x, g):
    return x, _rms(x, g)


def _layer_fwd(x, W, tabs, n, side=(), late_side=(), late_weights=None):
    mx = MXU_DTYPE
    cos_a, sin_a, cos_b, sin_b = tabs
    full = lambda a: (a, a.shape[1], 0)
    s = {"x": x}
    (s["h"],) = rowwise(_rms, [full(x)], [W["g_pre"]], [(D_MODEL, mx)], n + "pre")
    p = s["p"] = matmul(s["h"], W["w_in"], "nn", f32, n + "in")
    s["qa"], s["ka"], s["va"] = rowwise(_a_prep, [(p, 512, SEG_A // 512), full(cos_a), full(sin_a)],
                                        [W["a_qn"], W["a_wuq"], W["a_kvn"], W["a_wk"], W["a_wv"]], [(512, mx)] * 3, n + "a_prep")
    s["qb"], s["kb"], s["vb"] = rowwise(_b_prep, [(p, 512, SEG_B // 512), full(cos_b), full(sin_b)], [W["b_qn"], W["b_kn"]],
                                        [(512, mx)] * 3, n + "b_prep")
    s["oa"], s["lsea"], side_out = flash_fwd(s["qa"], s["ka"], s["va"], 4, 4, (A_NOPE + A_ROPE) ** -0.5, n + "a_attn", side)
    s["ob"], s["lseb"], late_out = flash_fwd(s["qb"], s["kb"], s["vb"], 4, 4, HEAD ** -0.5, n + "b_attn", late_side)
    if late_weights is not None:
        W = {**W, **late_weights(late_out)}
    xbc, s["c_c"] = conv_fwd(_c_act, p, SEG_CX // 512, 1, 512, W["c_cw"], W["c_cb"], 512, f32, n + "c_conv")
    s["xbc"] = xbc
    s["yc_f"], s["yc_r"], s["sc_f"], s["sc_r"] = scan_fwd(_ssd_chunk, min(SSD_CHUNK, xbc.shape[0]), xbc, p, SEG_CDT // HPAD, W["c_alog"], W["c_dtb"], n + "c_ssd")
    qkv, s["d_c"] = conv_fwd(_d_act, p, SEG_DX // 768, 1, 768, W["d_cw"], W["d_cb"], 768, f32, n + "d_conv")
    s["qkv"] = qkv
    s["od_f"], s["od_r"], s["sd_f"], s["sd_r"], s["td_f"], s["td_r"] = scan_fwd(
        _dn_chunk, DN_CHUNK, qkv, p, SEG_DAB // HPAD, W["d_alog"], W["d_dtb"], n + "d_dn", keeps=True)
    (s["omix"],) = rowwise(_mix_post, _mix_rows(s), _mix_params(W), [(D_MODEL, mx)], n + "mix_post")
    s["y1"] = matmul(s["omix"], W["w_out"], "nn", f32, n + "out")
    s["x1"], s["h2"] = rowwise(_post_mix, [full(x), full(s["y1"])], [W["g_pm"], W["g_pf"]], [(D_MODEL, f32), (D_MODEL, mx)], n + "post_mix")
    s["u"] = matmul(s["h2"], W["f_win"], "nn", f32, n + "f_in")
    s["a"], s["f_c"] = conv_fwd(_ffn_act, s["u"], 0, D_FF // FF_BLK, 2 * FF_BLK, W["f_cw"], W["f_cb"], FF_BLK, mx, n + "f_conv", tm=1024,
                                   prefetch=True)
    s["y2"] = matmul(s["a"], W["f_wout"], "nn", f32, n + "f_out")
    (x2,) = rowwise(_post_ffn, [full(s["x1"]), full(s["y2"])], [W["g_po"]], [(D_MODEL, f32)], n + "post_ffn")
    return x2, s, side_out, W


def _mix_rows(s):
    p = s["p"]
    return [(s["oa"], 512, 0), (s["ob"], 512, 0), (s["yc_f"], 256, 0), (s["yc_r"], 256, 0), (s["xbc"], 512, 0),
            (p, 256, SEG_CZ // 256), (s["od_f"], 256, 0), (s["od_r"], 256, 0), (p, 256, SEG_DZ // 256)]


def _mix_params(W):
    return [W["a_out"], W["b_out"], W["c_dskip"], W["c_out"], W["d_out"]]


def _layer_bwd(s, dx2, W, tabs, n, side=(), late_side=None):
    cos_a, sin_a, cos_b, sin_b = tabs
    full = lambda a: (a, a.shape[1], 0)
    p, g = s["p"], {}
    (dx1, dy2), (g["g_po"],) = rowwise_vjp(_post_ffn, [full(s["x1"]), full(s["y2"])], [W["g_po"]], [full(dx2)], n + "post_ffn_b", [0, 1], [0],
                                           row_dtypes=[f32, MXU_DTYPE])
    da = matmul(dy2, W["f_wout"], "nt", f32, n + "f_out_dx")
    g["f_wout"] = matmul(s["a"], dy2, "tn", f32, n + "f_out_dw")
    du, g["f_cw"], g["f_cb"] = conv_bwd(_ffn_act, s["u"], s["f_c"], 0, D_FF // FF_BLK, 2 * FF_BLK, W["f_cw"], [da], FF_BLK, n + "f_conv_b", tm=1024,
                                        dx_dtype=MXU_DTYPE, prefetch=True)
    dh2 = matmul(du, W["f_win"], "nt", f32, n + "f_in_dx")
    g["f_win"] = matmul(s["h2"], du, "tn", f32, n + "f_in_dw")
    (dx, dy1), (g["g_pm"], g["g_pf"]) = rowwise_vjp(_post_mix, [full(s["x"]), full(s["y1"])], [W["g_pm"], W["g_pf"]],
                                                   [full(dx1), full(dh2)], n + "post_mix_b", [0, 1], [0, 1],
                                                   row_dtypes=[f32, MXU_DTYPE])
    domix = matmul(dy1, W["w_out"], "nt", f32, n + "out_dx")
    g["w_out"] = matmul(s["omix"], dy1, "tn", f32, n + "out_dw")
    (doa, dob, dyc, dxbc_skip, dcz, dod, ddz), (g["a_out"], g["b_out"], g["c_dskip"], g["c_out"], g["d_out"]) = rowwise_vjp(
        _mix_post, _mix_rows(s), _mix_params(W), [full(domix)], n + "mix_post_b", [0, 1, 2, 4, 5, 6, 8], list(range(5)))
    dqkv_f, dqkv_r, ddab_f, ddab_r, g["d_alog"], g["d_dtb"] = scan_bwd(
        _dn_chunk, DN_CHUNK, s["qkv"], p, SEG_DAB // HPAD, W["d_alog"], W["d_dtb"], s["sd_f"], s["sd_r"], dod, n + "d_dn_b",
        kept=(s["td_f"], s["td_r"]))
    ddx, g["d_cw"], _ = conv_bwd(_d_act, p, s["d_c"], SEG_DX // 768, 1, 768, W["d_cw"], [dqkv_f, dqkv_r], 768, n + "d_conv_b")
    dxbc_f, dxbc_r, dcdt_f, dcdt_r, g["c_alog"], g["c_dtb"] = scan_bwd(
        _ssd_chunk, min(SSD_CHUNK, p.shape[0]), s["xbc"], p, SEG_CDT // HPAD, W["c_alog"], W["c_dtb"], s["sc_f"], s["sc_r"], dyc, n + "c_ssd_b")
    dcx, g["c_cw"], g["c_cb"] = conv_bwd(_c_act, p, s["c_c"], SEG_CX // 512, 1, 512, W["c_cw"], [dxbc_f, dxbc_r, dxbc_skip], 512,
                                         n + "c_conv_b")
    (dsmall,) = rowwise(lambda a, b, c, d: _cat([a + b, c + d]), [full(dcdt_f), full(dcdt_r), full(ddab_f), full(ddab_r)], [],
                        [(2 * HPAD, f32)], n + "dwin_sum")
    dqa, dka, dva, side_out = flash_bwd(s["qa"], s["ka"], s["va"], s["oa"], s["lsea"], doa, 4, 4, (A_NOPE + A_ROPE) ** -0.5,
                                        n + "a_attn_b", side)
    dqb, dkb, dvb, late_out = flash_bwd(s["qb"], s["kb"], s["vb"], s["ob"], s["lseb"], dob, 4, 4, HEAD ** -0.5, n + "b_attn_b",
                                        late_side(g) if late_side is not None else ())
    (dpa,), (g["a_qn"], g["a_wuq"], g["a_kvn"], g["a_wk"], g["a_wv"]) = rowwise_vjp(
        _a_prep, [(p, 512, SEG_A // 512), full(cos_a), full(sin_a)], [W["a_qn"], W["a_wuq"], W["a_kvn"], W["a_wk"], W["a_wv"]],
        [full(dqa), full(dka), full(dva)], n + "a_prep_b", [0], list(range(5)))
    (dpb,), (g["b_qn"], g["b_kn"]) = rowwise_vjp(_b_prep, [(p, 512, SEG_B // 512), full(cos_b), full(sin_b)], [W["b_qn"], W["b_kn"]],
                                               [full(dqb), full(dkb), full(dvb)], n + "b_prep_b", [0], [0, 1])
    dp = jnp.concatenate([dpa, dpb, dcx, ddx, dcz, ddz, dsmall], axis=1).astype(MXU_DTYPE)
    dh = matmul(dp, W["w_in"], "nt", f32, n + "in_dx")
    g["w_in"] = matmul(s["h"], dp, "tn", f32, n + "in_dw")
    (dx0,), (g["g_pre"],) = rowwise_vjp(_pre_bwd, [full(s["x"])], [W["g_pre"]], [full(dx), full(dh)], n + "pre_b", [0], [0])
    return dx0, g, side_out, late_out


def loss_and_grad(y, target, name):
    L, D = y.shape
    tm = _tile(L, 512)

    def body(y_ref, t_ref, loss_ref, dy_ref):
        e = y_ref[...] - t_ref[...]
        dy_ref[...] = e * (1.0 / D)
        part = 0.5 * jnp.sum(jnp.sum(e * e, axis=1, keepdims=True) * (1.0 / D), axis=0, keepdims=True)

        @pl.when(pl.program_id(0) == 0)
        def _():
            loss_ref[...] = part

        @pl.when(pl.program_id(0) > 0)
        def _():
            loss_ref[...] += part

    row = pl.BlockSpec((tm, D), lambda i: (i, 0))
    return pl.pallas_call(
        body, name=name, grid=(L // tm,), in_specs=[row, row], out_specs=[pl.BlockSpec((1, 1), lambda i: (0, 0)), row],
        out_shape=[jax.ShapeDtypeStruct((1, 1), f32), jax.ShapeDtypeStruct((L, D), f32)], compiler_params=_params(("arbitrary",)),
    )(y, target)


_IN_SEGS = [(0, 192), (256, 128), (384 + A_NOPE, 32), (512, 256), (768, 128), (896, 128), (SEG_CZ, 256), (SEG_CX, 512),
            (SEG_CDT, 8), (SEG_DX, 768), (SEG_DZ, 256), (SEG_DAB, 16)]


def _pad_in(w):
    src, pieces = 0, {}
    for off, wd in _IN_SEGS:
        pieces[off] = w[..., src:src + wd]
        src += wd
    out, pos = [], 0
    for off in sorted(pieces):
        if off > pos:
            out.append(jnp.zeros(w.shape[:-1] + (off - pos,), w.dtype))
        out.append(pieces[off])
        pos = off + pieces[off].shape[-1]
    out.append(jnp.zeros(w.shape[:-1] + (PIN - pos,), w.dtype))
    return jnp.concatenate(out, axis=-1)


def _unpad_in(wp):
    return jnp.concatenate([wp[..., off:off + wd] for off, wd in _IN_SEGS], axis=-1)


def _pad_last(a, n):
    return jnp.pad(a, [(0, 0)] * (a.ndim - 1) + [(0, n - a.shape[-1])])


def _ff_interleave(w):
    nb = D_FF // FF_BLK
    if w.size <= 4 * D_FF * 2:
        lead = w.shape[:-1]
        return jnp.stack([w[..., :D_FF].reshape(lead + (nb, FF_BLK)), w[..., D_FF:].reshape(lead + (nb, FF_BLK))],
                         axis=-2).reshape(lead + (2 * D_FF,))
    return jnp.concatenate([w[..., off + j * FF_BLK:off + (j + 1) * FF_BLK] for j in range(nb) for off in (0, D_FF)], axis=-1)


def _ff_deinterleave(w):
    nb = D_FF // FF_BLK
    if w.size <= 4 * D_FF * 2:
        lead = w.shape[:-1]
        t = w.reshape(lead + (nb, 2, FF_BLK))
        return jnp.concatenate([t[..., 0, :].reshape(lead + (D_FF,)), t[..., 1, :].reshape(lead + (D_FF,))], axis=-1)
    return jnp.concatenate([w[..., (2 * j + half) * FF_BLK:(2 * j + half + 1) * FF_BLK] for half in (0, 1) for j in range(nb)], axis=-1)


def _row(v):
    return v.reshape(1, -1)


_LATE = ["w_out", "f_w_in", "f_w_out"]


def _late_kernel_weights(P, l, wdt):
    return {"w_out": P["w_out"][l].astype(wdt), "f_win": _ff_interleave(P["f_w_in"][l]).astype(wdt), "f_wout": P["f_w_out"][l].astype(wdt)}


def _kernel_weights(P, l, wdt):
    late = {} if P["w_out"][l] is None else _late_kernel_weights(P, l, wdt)
    uq = P["a_w_uq"][l].reshape(A_Q_LORA, 4, A_NOPE + A_ROPE)
    ukv = P["a_w_ukv"][l].reshape(A_KV_LORA, 4, 2 * HEAD)
    z = jnp.zeros((A_KV_LORA, 4, HEAD), ukv.dtype)
    return {
        "g_pre": _row(P["pre_mix_norm"][l]), "w_in": _pad_in(P["w_in"][l]).astype(wdt),
        "a_qn": _pad_last(_row(P["a_q_norm"][l]), 256),
        "a_wuq": jnp.pad(_pad_last(uq, HPAD).reshape(A_Q_LORA, 4 * HPAD), ((0, 256 - A_Q_LORA), (0, 0))).astype(wdt),
        "a_kvn": _row(P["a_kv_norm"][l]),
        "a_wk": jnp.concatenate([ukv[..., :HEAD], z], axis=-1).reshape(A_KV_LORA, 4 * HPAD).astype(wdt),
        "a_wv": jnp.concatenate([ukv[..., HEAD:], z], axis=-1).reshape(A_KV_LORA, 4 * HPAD).astype(wdt),
        "a_out": _row(P["a_out_norm"][l]), "b_qn": _row(P["b_q_norm"][l]), "b_kn": _row(P["b_k_norm"][l]), "b_out": _row(P["b_out_norm"][l]),
        "c_cw": P["c_conv_w"][l], "c_cb": _row(P["c_conv_b"][l]), "c_alog": P["c_a_log"][l], "c_dtb": P["c_dt_bias"][l],
        "c_dskip": _row(P["c_d_skip"][l]), "c_out": _row(P["c_out_norm"][l]),
        "d_cw": P["d_conv_w"][l], "d_cb": jnp.zeros((1, 768), f32), "d_alog": P["d_a_log"][l], "d_dtb": P["d_dt_bias"][l],
        "d_out": _row(P["d_out_norm"][l]), "g_pm": _row(P["post_mix_norm"][l]), "g_pf": _row(P["pre_ffn_norm"][l]),
        "f_cw": _ff_interleave(P["f_conv_w"][l]), "f_cb": _row(_ff_interleave(P["f_conv_b"][l])),
        "g_po": _row(P["post_ffn_norm"][l]), **late,
    }


def _reference_grads(g):
    uq = g["a_wuq"][:A_Q_LORA].reshape(A_Q_LORA, 4, HPAD)[..., :A_NOPE + A_ROPE].reshape(A_Q_LORA, 4 * (A_NOPE + A_ROPE))
    wk = g["a_wk"].reshape(A_KV_LORA, 4, HPAD)[..., :HEAD]
    wv = g["a_wv"].reshape(A_KV_LORA, 4, HPAD)[..., :HEAD]
    return {
        "pre_mix_norm": g["g_pre"][0], "w_in": _unpad_in(g["w_in"]), "a_q_norm": g["a_qn"][0, :A_Q_LORA], "a_w_uq": uq,
        "a_kv_norm": g["a_kvn"][0], "a_w_ukv": jnp.concatenate([wk, wv], axis=-1).reshape(A_KV_LORA, 8 * HEAD),
        "a_out_norm": g["a_out"][0], "b_q_norm": g["b_qn"][0], "b_k_norm": g["b_kn"][0], "b_out_norm": g["b_out"][0],
        "c_conv_w": g["c_cw"], "c_conv_b": g["c_cb"][0], "c_a_log": g["c_alog"], "c_dt_bias": g["c_dtb"], "c_d_skip": g["c_dskip"][0],
        "c_out_norm": g["c_out"][0], "d_conv_w": g["d_cw"], "d_a_log": g["d_alog"], "d_dt_bias": g["d_dtb"], "d_out_norm": g["d_out"][0],
        "w_out": g["w_out"], "post_mix_norm": g["g_pm"][0], "pre_ffn_norm": g["g_pf"][0], "f_w_in": _ff_deinterleave(g["f_win"]),
        "f_conv_w": _ff_deinterleave(g["f_cw"]), "f_conv_b": _ff_deinterleave(g["f_cb"][0]), "f_w_out": g["f_wout"],
        "post_ffn_norm": g["g_po"][0],
    }


def _rope_tables(L):
    def tables(rot):
        rows = L // GRID_W
        row = jnp.repeat(jnp.arange(rows), GRID_W).astype(f32)
        col = jnp.tile(jnp.arange(GRID_W), rows).astype(f32)
        sec = rot // 2
        inv = ROPE_BASE ** (-jnp.arange(0, sec, 2, dtype=f32) / sec)
        ang = jnp.concatenate([row[:, None] * inv] * 2 + [col[:, None] * inv] * 2, axis=-1)
        return jnp.cos(ang), jnp.sin(ang)

    ca, sa = tables(A_ROPE)
    cb, sb = tables(HEAD)
    one, zero = jnp.ones, jnp.zeros
    return (jnp.concatenate([one((L, A_NOPE), f32), ca, one((L, HPAD - A_NOPE - A_ROPE), f32)], axis=1),
            jnp.concatenate([zero((L, A_NOPE), f32), sa, zero((L, HPAD - A_NOPE - A_ROPE), f32)], axis=1),
            jnp.concatenate([cb, one((L, HPAD - HEAD), f32)], axis=1), jnp.concatenate([sb, zero((L, HPAD - HEAD), f32)], axis=1))


def exchange(items, name):
    n = len(items)

    def body(*refs):
        start, wait = _exchange_ops([g for _, g in items], refs[:n], refs[n:2 * n], *refs[2 * n:])
        start()
        wait()

    ex = _exchange_io(items)
    res = pl.pallas_call(body, name=name, in_specs=ex.specs, out_specs=ex.specs, out_shape=ex.out_shape,
                         scratch_shapes=ex.scratch)(*ex.args)
    return list(res)


def _exchange_ops(modes, src_refs, out_refs, send_sems, recv_sems, local_sems):
    n = len(modes)
    x, y, c = lax.axis_index("x"), lax.axis_index("y"), lax.axis_index("c")
    me = 4 * x + 2 * y + c

    def copy(a, k, arriving):
        px, py, pc = (1 - x if k & 4 else x), (1 - y if k & 2 else y), (1 - c if k & 1 else c)
        pid = 4 * px + 2 * py + pc
        sem = a * (N_DEV - 1) + k - 1
        return pltpu.make_async_remote_copy(
            src_ref=src_refs[a] if modes[a] else src_refs[a].at[pid], dst_ref=out_refs[a].at[pid if arriving else me],
            send_sem=send_sems.at[sem], recv_sem=recv_sems.at[sem], device_id=(px, py, pc), device_id_type=pl.DeviceIdType.MESH)

    def local(a):
        return pltpu.make_async_copy(src_refs[a] if modes[a] else src_refs[a].at[me], out_refs[a].at[me], local_sems.at[a])

    pairs = [(a, k) for k in range(1, N_DEV) for a in range(n)]

    def start():
        for a, k in pairs:
            copy(a, k, False).start()
        for a in range(n):
            local(a).start()

    def wait():
        for a, k in pairs:
            copy(a, k, False).wait_send()
        for a, k in pairs:
            copy(a, k, True).wait_recv()
        for a in range(n):
            local(a).wait()

    return start, wait


class _ExchangeIO:
    def __init__(self, items):
        n = len(items)
        self.args = [s for s, _ in items]
        self.specs = [pl.BlockSpec(memory_space=pl.ANY)] * n
        self.out_shape = [jax.ShapeDtypeStruct((N_DEV,) + tuple(s.shape if g else s.shape[1:]), s.dtype) for s, g in items]
        self.scratch = [pltpu.SemaphoreType.DMA((n * (N_DEV - 1),)), pltpu.SemaphoreType.DMA((n * (N_DEV - 1),)),
                        pltpu.SemaphoreType.DMA((n,))] if n else []


def _exchange_io(items):
    return _ExchangeIO(list(items))


def _with_exchange(body, n_in, n_out, side, grid):
    ex = _exchange_io(side)
    n = len(ex.args)
    if not n:
        return body, ex
    modes = [g for _, g in side]

    def wrapped(*refs):
        ins, src = refs[:n_in], refs[n_in:n_in + n]
        outs, dst = refs[n_in + n:n_in + n + n_out], refs[n_in + n + n_out:n_in + 2 * n + n_out]
        sems = refs[n_in + 2 * n + n_out:]
        start, wait = _exchange_ops(modes, src, dst, *sems)
        ids = [pl.program_id(a) for a in range(len(grid))]
        first, last = ids[0] == 0, ids[0] == grid[0] - 1
        for a in range(1, len(grid)):
            first, last = first & (ids[a] == 0), last & (ids[a] == grid[a] - 1)
        pl.when(first)(start)
        body(*ins, *outs)
        pl.when(last)(wait)

    return wrapped, ex


LANES = 1024


def adamw(gstacks, w, m, v, name):
    D, R, C = w.shape
    S = gstacks[0].shape[0]
    tr = R if R <= 512 else _tile(R, 256)

    def body(*refs):
        g_refs = refs[:D]
        w_ref, m_ref, v_ref, go_ref, d_ref, mo_ref, vo_ref = refs[D:]
        layer = pl.program_id(0)
        g = None
        for d, g_ref in enumerate(g_refs):
            gd = g_ref[0].astype(f32)
            for sl in range(1, S):
                gd = gd + g_ref[sl].astype(f32)
            g = gd if g is None else jnp.where(layer == d, gd, g)
        g = g[None]
        m_new = ADAM_B1 * m_ref[...] + (1.0 - ADAM_B1) * g
        v_new = ADAM_B2 * v_ref[...] + (1.0 - ADAM_B2) * jnp.square(g)
        m_hat = m_new / (1.0 - ADAM_B1 ** ADAM_STEP)
        v_hat = v_new / (1.0 - ADAM_B2 ** ADAM_STEP)
        go_ref[...] = g
        d_ref[...] = -ADAM_LR * (m_hat / (jnp.sqrt(v_hat) + ADAM_EPS) + ADAM_WD * w_ref[...])
        mo_ref[...] = m_new
        vo_ref[...] = v_new

    row = pl.BlockSpec((1, tr, C), lambda l, i: (l, i, 0))
    g_specs = [pl.BlockSpec((S, tr, C), lambda l, i, d=d: (0, jnp.where(l == d, i, 0), 0)) for d in range(D)]
    return pl.pallas_call(
        body, name=name, grid=(D, R // tr), in_specs=g_specs + [row, row, row],
        out_specs=[row] * 4, out_shape=[jax.ShapeDtypeStruct((D, R, C), f32)] * 4, compiler_params=_params(("parallel", "parallel")),
    )(*gstacks, w, m, v)


def sum_slots(gstack, name):
    S, R, _ = gstack.shape

    def body(g_ref, o_ref):
        g = g_ref[0]
        for sl in range(1, S):
            g = g + g_ref[sl]
        o_ref[...] = g

    return pl.pallas_call(body, name=name, out_shape=jax.ShapeDtypeStruct((R, LANES), f32), compiler_params=_params())(gstack)


def _pack(parts, rows, dtype=f32):
    flat = jnp.concatenate([q.reshape(-1).astype(dtype) for q in parts])
    return jnp.pad(flat, (0, rows * LANES - flat.shape[0])).reshape(rows, LANES)


def _unpack(buf, shapes):
    lead = buf.shape[:-2]
    flat = buf.reshape(lead + (-1,))
    out, off = [], 0
    for shp in shapes:
        n = math.prod(shp)
        out.append(flat[..., off:off + n].reshape(lead + tuple(shp)))
        off += n
    return out


_WEIGHTS = ["pre_mix_norm", "w_in", "a_q_norm", "a_w_uq", "a_kv_norm", "a_w_ukv", "a_out_norm", "b_q_norm", "b_k_norm", "b_out_norm",
            "c_conv_w", "c_conv_b", "c_a_log", "c_dt_bias", "c_d_skip", "c_out_norm", "d_conv_w", "d_a_log", "d_dt_bias", "d_out_norm",
            "w_out", "post_mix_norm", "pre_ffn_norm", "f_w_in", "f_conv_w", "f_conv_b", "f_w_out", "post_ffn_norm"]
_BIG = {"w_in": 1, "a_w_uq": 1, "a_w_ukv": 1, "w_out": 0, "f_w_in": 1, "f_w_out": 0}
_CONV = ["c_conv_w", "d_conv_w", "f_conv_w"]
_REP = [n for n in _WEIGHTS if n not in _BIG and n not in _CONV]
SMALLG_ROWS, SMALLW_ROWS = 64, 32


def _join(blocks, axis):
    return jnp.concatenate([blocks[d] for d in range(N_DEV)], axis=axis)


def _split(full, axis):
    n = full.shape[axis] // N_DEV
    return jnp.stack([lax.slice_in_dim(full, d * n, (d + 1) * n, axis=axis) for d in range(N_DEV)])


def kernel(x, pre_mix_norm, w_in, a_q_norm, a_w_uq, a_kv_norm, a_w_ukv, a_out_norm, b_q_norm, b_k_norm, b_out_norm, c_conv_w, c_conv_b, c_a_log, c_dt_bias, c_d_skip, c_out_norm, d_conv_w, d_a_log, d_dt_bias, d_out_norm, w_out, post_mix_norm, pre_ffn_norm, f_w_in, f_conv_w, f_conv_b, f_w_out, post_ffn_norm, loss_target, m_pre_mix_norm, m_w_in, m_a_q_norm, m_a_w_uq, m_a_kv_norm, m_a_w_ukv, m_a_out_norm, m_b_q_norm, m_b_k_norm, m_b_out_norm, m_c_conv_w, m_c_conv_b, m_c_a_log, m_c_dt_bias, m_c_d_skip, m_c_out_norm, m_d_conv_w, m_d_a_log, m_d_dt_bias, m_d_out_norm, m_w_out, m_post_mix_norm, m_pre_ffn_norm, m_f_w_in, m_f_conv_w, m_f_conv_b, m_f_w_out, m_post_ffn_norm, v_pre_mix_norm, v_w_in, v_a_q_norm, v_a_w_uq, v_a_kv_norm, v_a_w_ukv, v_a_out_norm, v_b_q_norm, v_b_k_norm, v_b_out_norm, v_c_conv_w, v_c_conv_b, v_c_a_log, v_c_dt_bias, v_c_d_skip, v_c_out_norm, v_d_conv_w, v_d_a_log, v_d_dt_bias, v_d_out_norm, v_w_out, v_post_mix_norm, v_pre_ffn_norm, v_f_w_in, v_f_conv_w, v_f_conv_b, v_f_w_out, v_post_ffn_norm):
    given = dict(locals())
    w = {n: given[n] for n in _WEIGHTS}
    mom = {n: given["m_" + n] for n in _WEIGHTS}
    var = {n: given["v_" + n] for n in _WEIGHTS}
    me = 4 * lax.axis_index("x") + 2 * lax.axis_index("y") + lax.axis_index("c")
    layered = lambda names: [(l, n) for l in range(DEPTH) for n in names]

    sharded = list(_BIG) + _CONV
    early = [n for n in sharded if n not in _LATE]
    weight_items = lambda l, names: [(w[n][l].astype(MXU_DTYPE if n in _BIG else f32), True) for n in names]
    grad_items = lambda g, names: [(_split(g[n], _BIG[n]).astype(WIRE_DTYPE), False) for n in names]
    P = {n: [w[n][l] for l in range(DEPTH)] for n in _REP}
    P.update({n: [None] * DEPTH for n in sharded})

    def place(l, names, gathered):
        for n, got in zip(names, gathered):
            P[n][l] = _join(got, _BIG.get(n, 1))

    def late_weights_l0(gathered):
        place(0, _LATE, gathered)
        return _late_kernel_weights(P, 0, MXU_DTYPE)

    tabs = _rope_tables(x.shape[1])
    place(0, early, exchange(weight_items(0, early), "gather_weights"))
    x1, s0, got, W0 = _layer_fwd(x[0], _kernel_weights(P, 0, MXU_DTYPE), tabs, "l0_", weight_items(1, sharded),
                                 weight_items(0, _LATE), late_weights_l0)
    place(1, sharded, got)
    x2, s1, _, W1 = _layer_fwd(x1, _kernel_weights(P, 1, MXU_DTYPE), tabs, "l1_")
    loss, dx = loss_and_grad(x2, loss_target[0], "loss")
    loss = lax.psum(loss[0, 0], ("x", "y", "c"))

    def late_grads_l0(g):
        return grad_items({"w_out": g["w_out"], "f_w_in": _ff_deinterleave(g["f_win"]), "f_w_out": g["f_wout"]}, _LATE)

    dx, g1, _, _ = _layer_bwd(s1, dx, W1, tabs, "l1_")
    g1 = _reference_grads(g1)
    dx, g0, recv1, recv0_late = _layer_bwd(s0, dx, W0, tabs, "l0_", grad_items(g1, _BIG), late_grads_l0)
    grads = [_reference_grads(g0), g1]
    small = _REP + _CONV
    full_shapes = [grads[l][n].shape for l, n in layered(small)]
    first = [n for n in _BIG if n not in _LATE]
    *recv0_first, gsmall = exchange(grad_items(grads[0], first) + [(_pack([grads[l][n] for l, n in layered(small)], SMALLG_ROWS), True)],
                                    "exchange_grads")
    recv0 = dict(zip(first + _LATE, recv0_first + recv0_late))

    out = {}
    for n, got1 in zip(_BIG, recv1):
        out[n] = adamw([recv0[n], got1], w[n], mom[n], var[n], "adamw_" + n)
    gsum = dict(zip(layered(small), _unpack(sum_slots(gsmall, "sum_small_grads"), full_shapes)))
    for l, n in layered(_CONV):
        cols = w[n].shape[2]
        gsum[(l, n)] = lax.dynamic_slice_in_dim(gsum[(l, n)], me * cols, cols, axis=1)
    held_shapes = [w[n].shape[1:] for _, n in layered(small)]
    pack_small = lambda d: _pack([d[n][l] for l, n in layered(small)], SMALLW_ROWS)[None]
    res_small = adamw([_pack([gsum[k] for k in layered(small)], SMALLW_ROWS)[None]], pack_small(w), pack_small(mom),
                      pack_small(var), "adamw_small")
    per_kind = [dict(zip(layered(small), _unpack(buf[0], held_shapes))) for buf in res_small]
    for n in small:
        out[n] = [jnp.stack([per_kind[kind][(l, n)] for l in range(DEPTH)]) for kind in range(4)]
    return (loss, dx[None], *[out[n][kind] for kind in range(4) for n in _WEIGHTS])
```
